```python
import math
import jax, jax.numpy as jnp
from jax import lax
import numpy as np

D_MODEL = 1024
BATCH = 8
SEQ = 4096
DEPTH = 2

D_MIX = D_MODEL
D_S5 = D_MIX // 2
S5_GROUP = 16
S5_GROUPS = D_S5 // S5_GROUP
S5_STATE = 64
D_GLA = D_MIX - D_S5
GLA_HEADS = 4
GLA_DV = D_GLA // GLA_HEADS
GLA_DK = GLA_DV // 2
D_GLA_K = GLA_HEADS * GLA_DK
GLA_GATE_RANK = 16
GLA_TAU = 16.0
GLA_CHUNK = 64
D_IN = D_S5 + 2 * D_GLA_K + 2 * D_GLA + GLA_GATE_RANK
SPLITS = (D_S5, D_S5 + D_GLA_K, D_S5 + 2 * D_GLA_K, D_S5 + 2 * D_GLA_K + D_GLA,
          D_S5 + 2 * D_GLA_K + 2 * D_GLA)
D_FF = 2816
N_EXPERTS = 8
TOP_K = 2
D_FF_EXPERT = 3584
N_DENSE = (DEPTH + 1) // 2
N_MOE = DEPTH // 2
EPS = 1e-6
DT_MIN = 1e-3
DT_MAX = 1e-1

kernel_name = "hymba_s5_gla_moe_trunk"


def rms_norm(x, g):
    xf = x.astype(jnp.float32)
    y = xf * lax.rsqrt(jnp.mean(xf * xf, axis=-1, keepdims=True) + EPS)
    return (y * g.astype(jnp.float32)).astype(x.dtype)


def swiglu(h, w_gate, w_up, w_down):
    return (jax.nn.silu(h @ w_gate) * (h @ w_up)) @ w_down


def s5_mixer(u, lam_re, lam_im, log_dt, b_re, b_im, c_re, c_im, d_skip, w_glu, b_glu):
    f32 = jnp.float32
    bsz, seq, _ = u.shape
    uf = u.astype(f32).reshape(bsz, seq, S5_GROUPS, S5_GROUP)
    lr = jnp.minimum(lam_re.astype(f32), -1e-4)
    li = lam_im.astype(f32)
    dt = jnp.exp(log_dt.astype(f32))[:, None]
    mag = jnp.exp(lr * dt)
    ab_re = mag * jnp.cos(li * dt)
    ab_im = mag * jnp.sin(li * dt)
    nr = ab_re - 1.0
    ni = ab_im
    den = lr * lr + li * li
    f_re = (nr * lr + ni * li) / den
    f_im = (ni * lr - nr * li) / den
    br = b_re.astype(f32)
    bi = b_im.astype(f32)
    bb_re = f_re[..., None] * br - f_im[..., None] * bi
    bb_im = f_re[..., None] * bi + f_im[..., None] * br
    bu_re = jnp.einsum('blgc,gpc->blgp', uf, bb_re)
    bu_im = jnp.einsum('blgc,gpc->blgp', uf, bb_im)
    a_re = jnp.broadcast_to(ab_re, bu_re.shape)
    a_im = jnp.broadcast_to(ab_im, bu_im.shape)

    def combine(e1, e2):
        a1r, a1i, b1r, b1i = e1
        a2r, a2i, b2r, b2i = e2
        return (a1r * a2r - a1i * a2i,
                a1r * a2i + a1i * a2r,
                a2r * b1r - a2i * b1i + b2r,
                a2r * b1i + a2i * b1r + b2i)

    _, _, h_re, h_im = lax.associative_scan(combine, (a_re, a_im, bu_re, bu_im), axis=1)
    y = (jnp.einsum('blgp,gcp->blgc', h_re, c_re.astype(f32))
         - jnp.einsum('blgp,gcp->blgc', h_im, c_im.astype(f32)))
    y = y + d_skip.astype(f32).reshape(S5_GROUPS, S5_GROUP) * uf
    y = jax.nn.gelu(y.reshape(bsz, seq, D_S5))
    y = y * jax.nn.sigmoid(y @ w_glu.astype(f32) + b_glu.astype(f32))
    return y


def gla_mixer(q, k, v, r, g_low, w_a2, b_a2, g_norm):
    f32 = jnp.float32
    bsz, seq, _ = q.shape
    n_chunks = seq // GLA_CHUNK
    c = GLA_CHUNK

    def heads(t, dh):
        return t.astype(f32).reshape(bsz, n_chunks, c, GLA_HEADS, dh).transpose(0, 3, 1, 2, 4)

    qh = heads(q, GLA_DK) * (GLA_DK ** -0.5)
    kh = heads(k, GLA_DK)
    vh = heads(v, GLA_DV)
    log_a = jax.nn.log_sigmoid(g_low.astype(f32) @ w_a2.astype(f32) + b_a2.astype(f32)) / GLA_TAU
    cum = jnp.cumsum(heads(log_a, GLA_DK), axis=3)
    cum_last = cum[..., -1:, :]
    q_t = qh * jnp.exp(cum)
    k_t = kh * jnp.exp(-cum)
    k_end = kh * jnp.exp(cum_last - cum)
    causal = jnp.tril(jnp.ones((c, c), dtype=bool))
    scores = jnp.where(causal, jnp.einsum('bhnik,bhnjk->bhnij', q_t, k_t), 0.0)
    o_intra = jnp.einsum('bhnij,bhnjv->bhniv', scores, vh)
    upd = jnp.einsum('bhnjk,bhnjv->bhnkv', k_end, vh)
    decay = jnp.exp(cum_last[..., 0, :])

    def step(state, inp):
        dec, u = inp
        return dec[..., None] * state + u, state

    s0 = jnp.zeros((bsz, GLA_HEADS, GLA_DK, GLA_DV), f32)
    _, s_prev = lax.scan(step, s0, (jnp.moveaxis(decay, 2, 0), jnp.moveaxis(upd, 2, 0)))
    s_prev = jnp.moveaxis(s_prev, 0, 2)
    o = o_intra + jnp.einsum('bhnik,bhnkv->bhniv', q_t, s_prev)
    o = o * lax.rsqrt(jnp.mean(o * o, axis=-1, keepdims=True) + EPS)
    o = o.transpose(0, 2, 3, 1, 4).reshape(bsz, seq, D_GLA) * g_norm.astype(f32)
    return o * jax.nn.silu(r.astype(f32))


def moe_swiglu(h, w_router, w_gate, w_up, w_down):
    f32 = jnp.float32
    logits = h.astype(f32) @ w_router.astype(f32)
    top_v, top_i = lax.top_k(logits, TOP_K)
    wts = jax.nn.softmax(top_v, axis=-1)
    gates = jnp.sum(jax.nn.one_hot(top_i, N_EXPERTS, dtype=f32) * wts[..., None], axis=-2)
    out = jnp.zeros(h.shape, f32)
    for e in range(N_EXPERTS):
        out = out + gates[..., e:e + 1] * swiglu(h, w_gate[e], w_up[e], w_down[e]).astype(f32)
    return out


def setup_inputs(seed: int = 0) -> dict:
    key = jax.random.key(seed)
    ks = jax.random.split(key, 32)
    f32 = jnp.float32

    def nrm(k, shape, scale):
        return jax.random.normal(k, shape, f32) * scale

    n_idx = jnp.arange(S5_STATE, dtype=f32)
    return {
        "x": nrm(ks[0], (BATCH, SEQ, D_MODEL), 1.0),
        "norm_mix": 1.0 + nrm(ks[1], (DEPTH, D_MODEL), 0.02),
        "w_in": nrm(ks[2], (DEPTH, D_MODEL, D_IN), D_MODEL ** -0.5),
        "s5_lambda_re": -0.5 + nrm(ks[3], (DEPTH, S5_GROUPS, S5_STATE), 0.01),
        "s5_lambda_im": math.pi * n_idx + nrm(ks[4], (DEPTH, S5_GROUPS, S5_STATE), 0.01),
        "s5_log_dt": jax.random.uniform(ks[5], (DEPTH, S5_GROUPS), f32,
                                        minval=math.log(DT_MIN), maxval=math.log(DT_MAX)),
        "s5_b_re": nrm(ks[6], (DEPTH, S5_GROUPS, S5_STATE, S5_GROUP), (2 * S5_GROUP) ** -0.5),
        "s5_b_im": nrm(ks[7], (DEPTH, S5_GROUPS, S5_STATE, S5_GROUP), (2 * S5_GROUP) ** -0.5),
        "s5_c_re": nrm(ks[8], (DEPTH, S5_GROUPS, S5_GROUP, S5_STATE), S5_STATE ** -0.5),
        "s5_c_im": nrm(ks[9], (DEPTH, S5_GROUPS, S5_GROUP, S5_STATE), S5_STATE ** -0.5),
        "s5_d": nrm(ks[10], (DEPTH, D_S5), 1.0),
        "s5_w_glu": nrm(ks[11], (DEPTH, D_S5, D_S5), D_S5 ** -0.5),
        "s5_b_glu": nrm(ks[12], (DEPTH, D_S5), 0.02),
        "s5_out_norm": 1.0 + nrm(ks[13], (DEPTH, D_S5), 0.02),
        "gla_w_a2": nrm(ks[14], (DEPTH, GLA_GATE_RANK, D_GLA_K), GLA_GATE_RANK ** -0.5),
        "gla_b_a2": nrm(ks[15], (DEPTH, D_GLA_K), 0.1),
        "gla_out_norm": 1.0 + nrm(ks[16], (DEPTH, D_GLA), 0.02),
        "w_out": nrm(ks[17], (DEPTH, D_MIX, D_MODEL), D_MIX ** -0.5),
        "norm_ffn": 1.0 + nrm(ks[18], (DEPTH, D_MODEL), 0.02),
        "ffn_w_gate": nrm(ks[19], (N_DENSE, D_MODEL, D_FF), D_MODEL ** -0.5),
        "ffn_w_up": nrm(ks[20], (N_DENSE, D_MODEL, D_FF), D_MODEL ** -0.5),
        "ffn_w_down": nrm(ks[21], (N_DENSE, D_FF, D_MODEL), D_FF ** -0.5),
        "moe_w_router": nrm(ks[22], (N_MOE, D_MODEL, N_EXPERTS), D_MODEL ** -0.5),
        "moe_w_gate": nrm(ks[23], (N_MOE, N_EXPERTS, D_MODEL, D_FF_EXPERT), D_MODEL ** -0.5),
        "moe_w_up": nrm(ks[24], (N_MOE, N_EXPERTS, D_MODEL, D_FF_EXPERT), D_MODEL ** -0.5),
        "moe_w_down": nrm(ks[25], (N_MOE, N_EXPERTS, D_FF_EXPERT, D_MODEL), D_FF_EXPERT ** -0.5),
        "norm_final": 1.0 + nrm(ks[26], (D_MODEL,), 0.02),
    }


def reference(x, norm_mix, w_in, s5_lambda_re, s5_lambda_im, s5_log_dt, s5_b_re, s5_b_im,
              s5_c_re, s5_c_im, s5_d, s5_w_glu, s5_b_glu, s5_out_norm, gla_w_a2, gla_b_a2,
              gla_out_norm, w_out, norm_ffn, ffn_w_gate, ffn_w_up, ffn_w_down, moe_w_router,
              moe_w_gate, moe_w_up, moe_w_down, norm_final):
    for l in range(DEPTH):
        h = rms_norm(x, norm_mix[l])
        proj = h @ w_in[l]
        u, q, k, v, r, g_low = jnp.split(proj, SPLITS, axis=-1)
        y_s5 = s5_mixer(u, s5_lambda_re[l], s5_lambda_im[l], s5_log_dt[l], s5_b_re[l],
                        s5_b_im[l], s5_c_re[l], s5_c_im[l], s5_d[l], s5_w_glu[l], s5_b_glu[l])
        y_s5 = rms_norm(y_s5, s5_out_norm[l])
        y_gla = gla_mixer(q, k, v, r, g_low, gla_w_a2[l], gla_b_a2[l], gla_out_norm[l])
        mixed = jnp.concatenate([y_s5, y_gla], axis=-1).astype(x.dtype)
        x = x + mixed @ w_out[l]
        h = rms_norm(x, norm_ffn[l])
        i = l // 2
        if l % 2 == 0:
            x = x + swiglu(h, ffn_w_gate[i], ffn_w_up[i], ffn_w_down[i]).astype(x.dtype)
        else:
            x = x + moe_swiglu(h, moe_w_router[i], moe_w_gate[i], moe_w_up[i],
                               moe_w_down[i]).astype(x.dtype)
    return rms_norm(x, norm_final)
```

```python
import functools
import math

import jax
import jax.numpy as jnp
from jax import lax
from jax.experimental import pallas as pl
from jax.experimental.pallas import tpu as pltpu
from jax.experimental.pallas import tpu_sc as plsc

f32 = jnp.float32
bf16 = jnp.bfloat16
u32 = jnp.uint32
i32 = jnp.int32
SDS = jax.ShapeDtypeStruct

D_MODEL = 1024
D_S5 = 512
S5_GROUP = 16
S5_GROUPS = 32
S5_STATE = 64
N_STATE = S5_GROUPS * S5_STATE
D_GLA = 512
GLA_HEADS = 4
GLA_DV = 128
GLA_DK = 64
D_GLA_K = 256
GLA_GATE_RANK = 16
GLA_TAU = 16.0
GLA_CHUNK = 64
D_FF = 2816
N_EXPERTS = 8
D_FF_EXPERT = 3584
EPS = 1e-6

LANE = 128
SUBLANE = 8
VMEM_LIMIT = 56 * 1024 * 1024

TM_PROJ = 512
TT_S5 = 128
S5_SLABS = 4
TG_GLA = 256
TM_FFN = 512
TM_MOE = 512
TF_MOE = 512
TM_ROUTE = 512
SC_WINDOW = 128
SC_ROW = 256


def _cparams(*sem):
    return pltpu.CompilerParams(dimension_semantics=sem, vmem_limit_bytes=VMEM_LIMIT)


def _rms(x, g):
    ms = jnp.mean(x * x, axis=-1, keepdims=True)
    return x * lax.rsqrt(ms + EPS) * g


def _inproj_body(x_ref, g_ref, w_ref, wkt_ref, u_ref, q_ref, kt_ref, v_ref, r_ref, gl_ref):
    hn = _rms(x_ref[...], g_ref[...]).astype(bf16)

    def proj(lo, hi):
        return jnp.dot(hn, w_ref[:, lo:hi], preferred_element_type=f32).astype(bf16)

    u_ref[...] = proj(0, 512)
    q_ref[...] = proj(512, 768)
    v_ref[...] = proj(768, 1280)
    r_ref[...] = proj(1280, 1792)
    gl_ref[...] = proj(1792, 1920)
    kt_ref[...] = lax.dot_general(wkt_ref[...], hn, (((1,), (1,)), ((), ())),
                                  preferred_element_type=f32).astype(bf16)


def _inproj(x2, g, w_cat, w_kt, bsz, seq):
    n = x2.shape[0]
    tm = TM_PROJ
    per_b = seq // tm
    row = lambda i: (i, 0)
    const = lambda i: (0, 0)
    return pl.pallas_call(
        _inproj_body,
        grid=(n // tm,),
        in_specs=[pl.BlockSpec((tm, D_MODEL), row),
                  pl.BlockSpec((1, D_MODEL), const),
                  pl.BlockSpec(w_cat.shape, const),
                  pl.BlockSpec(w_kt.shape, const)],
        out_specs=[pl.BlockSpec((tm, D_S5), row),
                   pl.BlockSpec((tm, D_GLA_K), row),
                   pl.BlockSpec((None, D_GLA_K, tm), lambda i: (i // per_b, 0, i % per_b)),
                   pl.BlockSpec((tm, D_GLA), row),
                   pl.BlockSpec((tm, D_GLA), row),
                   pl.BlockSpec((tm, LANE), row)],
        out_shape=[SDS((n, D_S5), bf16), SDS((n, D_GLA_K), bf16),
                   SDS((bsz, D_GLA_K, seq), bf16), SDS((n, D_GLA), bf16),
                   SDS((n, D_GLA), bf16), SDS((n, LANE), bf16)],
        compiler_params=_cparams("parallel"),
        name="inproj",
    )(x2, g, w_cat, w_kt)


N_SLAB = N_STATE // LANE


def _s5_prep(lam_re, lam_im, log_dt, b_re, b_im, c_re, c_im):
    lr = jnp.minimum(lam_re.astype(f32), -1e-4)
    li = lam_im.astype(f32)
    dt = jnp.exp(log_dt.astype(f32))[:, None]
    mag = jnp.exp(lr * dt)
    ab_re = mag * jnp.cos(li * dt)
    ab_im = mag * jnp.sin(li * dt)
    nr = ab_re - 1.0
    ni = ab_im
    den = lr * lr + li * li
    f_re = (nr * lr + ni * li) / den
    f_im = (ni * lr - nr * li) / den
    br = b_re.astype(f32)
    bi = b_im.astype(f32)
    bb_re = f_re[..., None] * br - f_im[..., None] * bi
    bb_im = f_re[..., None] * bi + f_im[..., None] * br

    def b_tiles(bb):
        bt = jnp.transpose(bb, (0, 2, 1)).reshape(8, 4, S5_GROUP, S5_STATE)
        gl = jnp.arange(8)[None, :, None]
        gs = jnp.arange(4)[None, None, :]
        nn = jnp.arange(8)[:, None, None]
        sel = (gl == 4 * (nn % 2) + gs).astype(f32)
        t = jnp.einsum('ngs,nscp->ngcsp', sel, bt)
        return t.reshape(8, LANE, 4 * S5_STATE)

    bmat = jnp.concatenate([b_tiles(bb_re), b_tiles(bb_im)], axis=0).astype(bf16)

    def c_tiles(c, sign):
        ct = jnp.transpose(c.astype(f32), (0, 2, 1)).reshape(2, 16, S5_STATE, S5_GROUP)
        eye = jnp.eye(16, dtype=f32)
        t = jnp.einsum('gh,jgpo->jgpho', eye, ct) * sign
        return t.reshape(2, 16 * S5_STATE, 16 * S5_GROUP)

    cmat = jnp.stack([c_tiles(c_re, 1.0), c_tiles(c_im, -1.0)], axis=1).astype(bf16)
    a_re = ab_re.reshape(N_SLAB, 1, LANE)
    a_im = ab_im.reshape(N_SLAB, 1, LANE)
    return bmat, cmat, a_re, a_im


def _s5_body(u_ref, bm_ref, cm_ref, are_ref, aim_ref, d_ref, y_ref, hbuf, hstate):
    tt = u_ref.shape[1]
    rows = SUBLANE * tt

    @pl.when(pl.program_id(0) == 0)
    def _():
        hstate[...] = jnp.zeros_like(hstate)

    u2 = u_ref[...].reshape(rows, D_S5)

    for n in range(2 * 8):
        ks = LANE * ((n % 8) // 2)
        res = jnp.dot(u2[:, ks:ks + LANE], bm_ref[n], preferred_element_type=f32)
        hbuf[2 * n] = res[:, :LANE]
        hbuf[2 * n + 1] = res[:, LANE:]

    for c0 in range(0, N_SLAB, S5_SLABS):
        ar = [jnp.broadcast_to(are_ref[c0 + s], (SUBLANE, LANE)) for s in range(S5_SLABS)]
        ai = [jnp.broadcast_to(aim_ref[c0 + s], (SUBLANE, LANE)) for s in range(S5_SLABS)]

        def step(t, carry):
            out = []
            for s in range(S5_SLABS):
                hr, hi = carry[2 * s], carry[2 * s + 1]
                sl = pl.ds(t, SUBLANE, stride=tt)
                bur = hbuf[c0 + s, sl, :]
                bui = hbuf[N_SLAB + c0 + s, sl, :]
                nr = ar[s] * hr - ai[s] * hi + bur
                ni = ar[s] * hi + ai[s] * hr + bui
                hbuf[c0 + s, sl, :] = nr
                hbuf[N_SLAB + c0 + s, sl, :] = ni
                out += [nr, ni]
            return tuple(out)

        init = []
        for s in range(S5_SLABS):
            init += [hstate[c0 + s], hstate[N_SLAB + c0 + s]]
        fin = lax.fori_loop(0, tt, step, tuple(init), unroll=4)
        for s in range(S5_SLABS):
            hstate[c0 + s] = fin[2 * s]
            hstate[N_SLAB + c0 + s] = fin[2 * s + 1]

    for j in range(2):
        h_re = jnp.concatenate([hbuf[8 * j + s] for s in range(8)], axis=1).astype(bf16)
        h_im = jnp.concatenate([hbuf[N_SLAB + 8 * j + s] for s in range(8)], axis=1).astype(bf16)
        y = jnp.dot(h_re, cm_ref[j, 0], preferred_element_type=f32)
        y = y + jnp.dot(h_im, cm_ref[j, 1], preferred_element_type=f32)
        cs = slice(256 * j, 256 * j + 256)
        y = y + d_ref[:, cs] * u2[:, cs].astype(f32)
        y_ref[:, :, cs] = jax.nn.gelu(y).astype(bf16).reshape(SUBLANE, tt, 256)


def _s5(u3, bmat, cmat, a_re, a_im, d_skip):
    bsz, seq, _ = u3.shape
    assert bsz == SUBLANE
    tt = TT_S5
    blk = lambda i: (0, i, 0)
    return pl.pallas_call(
        _s5_body,
        grid=(seq // tt,),
        in_specs=[pl.BlockSpec((bsz, tt, D_S5), blk),
                  pl.BlockSpec(bmat.shape, lambda i: (0, 0, 0)),
                  pl.BlockSpec(cmat.shape, lambda i: (0, 0, 0, 0)),
                  pl.BlockSpec(a_re.shape, lambda i: (0, 0, 0)),
                  pl.BlockSpec(a_im.shape, lambda i: (0, 0, 0)),
                  pl.BlockSpec((1, D_S5), lambda i: (0, 0))],
        out_specs=pl.BlockSpec((bsz, tt, D_S5), blk),
        out_shape=SDS((bsz, seq, D_S5), bf16),
        scratch_shapes=[pltpu.VMEM((2 * N_SLAB, bsz * tt, LANE), f32),
                        pltpu.VMEM((2 * N_SLAB, SUBLANE, LANE), f32)],
        compiler_params=_cparams("arbitrary"),
        name="s5_scan",
    )(u3, bmat, cmat, a_re, a_im, d_skip)


def _split2(x):
    hi = x.astype(bf16)
    lo = (x - hi.astype(f32)).astype(bf16)
    return hi, lo


def _log_sigmoid(x):
    return -(jnp.maximum(-x, 0.0) + jnp.log1p(jnp.exp(-jnp.abs(x))))


def _gla_body(q_ref, kt_ref, v_ref, r_ref, g_ref, wa_ref, wat_ref, ba_ref, bat_ref, gn_ref,
              o_ref, s_ref):
    c = GLA_CHUNK
    tg = q_ref.shape[0]

    @pl.when(pl.program_id(1) == 0)
    def _():
        s_ref[...] = jnp.zeros_like(s_ref)

    ri = lax.broadcasted_iota(i32, (c, c), 0)
    ci = lax.broadcasted_iota(i32, (c, c), 1)
    tril = (ri >= ci).astype(bf16)
    triu = (ri <= ci).astype(bf16)
    head_of_lane = lax.broadcasted_iota(i32, (c, D_GLA_K), 1) // GLA_DK
    row_i = lax.broadcasted_iota(i32, (GLA_HEADS * c, c), 0) % c
    col_j = lax.broadcasted_iota(i32, (GLA_HEADS * c, c), 1)
    causal = row_i >= col_j

    for sc in range(tg // c):
        sl = slice(sc * c, sc * c + c)
        g = g_ref[sl, :]
        la = _log_sigmoid(jnp.dot(g, wa_ref[...], preferred_element_type=f32) + ba_ref[...]) / GLA_TAU
        lat = _log_sigmoid(lax.dot_general(wat_ref[...], g, (((1,), (1,)), ((), ())),
                                           preferred_element_type=f32) + bat_ref[...]) / GLA_TAU
        la_hi, la_lo = _split2(la)
        cum = (jnp.dot(tril, la_hi, preferred_element_type=f32)
               + jnp.dot(tril, la_lo, preferred_element_type=f32))
        lat_hi, lat_lo = _split2(lat)
        cumt = (jnp.dot(lat_hi, triu, preferred_element_type=f32)
                + jnp.dot(lat_lo, triu, preferred_element_type=f32))
        cl_col = cumt[:, c - 1:c]

        qt = q_ref[sl, :].astype(f32) * jnp.exp(cum)
        ktt = kt_ref[:, sl].astype(f32)
        k_t = (ktt * jnp.exp(-cumt)).astype(bf16)
        k_end = (ktt * jnp.exp(cl_col - cumt)).astype(bf16)
        v = v_ref[sl, :]

        q_stack = jnp.concatenate(
            [jnp.where(head_of_lane == h, qt, 0.0) for h in range(GLA_HEADS)], axis=0).astype(bf16)
        scores = jnp.dot(q_stack, k_t, preferred_element_type=f32)
        scores = jnp.where(causal, scores, 0.0).astype(bf16)
        o_full = jnp.dot(scores, v, preferred_element_type=f32)
        state = s_ref[...]
        o_inter = jnp.dot(q_stack, state.astype(bf16), preferred_element_type=f32)

        outs = []
        for h in range(GLA_HEADS):
            o = o_full[h * c:(h + 1) * c, h * GLA_DV:(h + 1) * GLA_DV] + o_inter[h * c:(h + 1) * c, :]
            o = o * lax.rsqrt(jnp.mean(o * o, axis=-1, keepdims=True) + EPS)
            outs.append(o)
        o_cat = jnp.concatenate(outs, axis=1)
        o_ref[sl, :] = (o_cat * gn_ref[...] * jax.nn.silu(r_ref[sl, :].astype(f32))).astype(bf16)

        upd_full = jnp.dot(k_end, v, preferred_element_type=f32)
        upd = jnp.concatenate(
            [upd_full[h * GLA_DK:(h + 1) * GLA_DK, h * GLA_DV:(h + 1) * GLA_DV]
             for h in range(GLA_HEADS)], axis=0)
        s_ref[...] = jnp.exp(cl_col) * state + upd


def _gla(q3, kt3, v3, r3, g3, wa, wat, ba, bat, gn):
    bsz, seq, _ = q3.shape
    tg = TG_GLA
    tok = lambda b, i: (b, i, 0)
    c2 = lambda b, i: (0, 0)
    return pl.pallas_call(
        _gla_body,
        grid=(bsz, seq // tg),
        in_specs=[pl.BlockSpec((None, tg, D_GLA_K), tok),
                  pl.BlockSpec((None, D_GLA_K, tg), lambda b, i: (b, 0, i)),
                  pl.BlockSpec((None, tg, D_GLA), tok),
                  pl.BlockSpec((None, tg, D_GLA), tok),
                  pl.BlockSpec((None, tg, LANE), tok),
                  pl.BlockSpec(wa.shape, c2), pl.BlockSpec(wat.shape, c2),
                  pl.BlockSpec(ba.shape, c2), pl.BlockSpec(bat.shape, c2),
                  pl.BlockSpec(gn.shape, c2)],
        out_specs=pl.BlockSpec((None, tg, D_GLA), tok),
        out_shape=SDS((bsz, seq, D_GLA), bf16),
        scratch_shapes=[pltpu.VMEM((D_GLA_K, GLA_DV), f32)],
        compiler_params=_cparams("parallel", "arbitrary"),
        name="gla",
    )(q3, kt3, v3, r3, g3, wa, wat, ba, bat, gn)


def _outproj_body(x_ref, ys_ref, yg_ref, wg_ref, bg_ref, gs_ref, wo_ref, o_ref):
    y = ys_ref[...]
    z = jnp.dot(y, wg_ref[...], preferred_element_type=f32) + bg_ref[...]
    yf = y.astype(f32)
    yf = yf * jax.nn.sigmoid(z)
    ys = _rms(yf, gs_ref[...]).astype(bf16)
    acc = jnp.dot(ys, wo_ref[0:D_S5, :], preferred_element_type=f32)
    acc = acc + jnp.dot(yg_ref[...], wo_ref[D_S5:, :], preferred_element_type=f32)
    o_ref[...] = x_ref[...] + acc


def _outproj(x2, ys, yg, w_glu, b_glu, g_s5, w_out):
    n = x2.shape[0]
    tm = TM_PROJ
    row = lambda i: (i, 0)
    const = lambda i: (0, 0)
    return pl.pallas_call(
        _outproj_body,
        grid=(n // tm,),
        in_specs=[pl.BlockSpec((tm, D_MODEL), row),
                  pl.BlockSpec((tm, D_S5), row),
                  pl.BlockSpec((tm, D_GLA), row),
                  pl.BlockSpec(w_glu.shape, const),
                  pl.BlockSpec((1, D_S5), const),
                  pl.BlockSpec((1, D_S5), const),
                  pl.BlockSpec(w_out.shape, const)],
        out_specs=pl.BlockSpec((tm, D_MODEL), row),
        out_shape=SDS((n, D_MODEL), f32),
        compiler_params=_cparams("parallel"),
        name="outproj",
    )(x2, ys, yg, w_glu, b_glu, g_s5, w_out)


def _ffn_body(x_ref, g_ref, wg_ref, wu_ref, wd_ref, o_ref):
    x = x_ref[...]
    hn = _rms(x, g_ref[...]).astype(bf16)
    gate = jnp.dot(hn, wg_ref[...], preferred_element_type=f32)
    up = jnp.dot(hn, wu_ref[...], preferred_element_type=f32)
    act = (jax.nn.silu(gate) * up).astype(bf16)
    o_ref[...] = x + jnp.dot(act, wd_ref[...], preferred_element_type=f32)


def _ffn(x2, g, wg, wu, wd):
    n = x2.shape[0]
    tm = TM_FFN
    row = lambda i: (i, 0)
    const = lambda i: (0, 0)
    once = dict(pipeline_mode=pl.Buffered(1))
    return pl.pallas_call(
        _ffn_body,
        grid=(n // tm,),
        in_specs=[pl.BlockSpec((tm, D_MODEL), row),
                  pl.BlockSpec((1, D_MODEL), const),
                  pl.BlockSpec(wg.shape, const, **once),
                  pl.BlockSpec(wu.shape, const, **once),
                  pl.BlockSpec(wd.shape, const, **once)],
        out_specs=pl.BlockSpec((tm, D_MODEL), row),
        out_shape=SDS((n, D_MODEL), f32),
        compiler_params=_cparams("parallel"),
        name="ffn_dense",
    )(x2, g, wg, wu, wd)


def _pack_bf16_pairs(a):
    bits = pltpu.bitcast(a.astype(bf16).astype(f32), u32)
    half = a.shape[1] // 2
    return bits[:, :half] | (bits[:, half:] >> 16)


def _unpack_bf16_pairs(p):
    hi = pltpu.bitcast(p & jnp.uint32(0xFFFF0000), f32).astype(bf16)
    lo = pltpu.bitcast(p << 16, f32).astype(bf16)
    return hi, lo


def _router_body(x_ref, g_ref, wh_ref, wl_ref, hp_ref, meta_ref, cnt_ref, carry):
    tm = x_ref.shape[0]

    @pl.when(pl.program_id(0) == 0)
    def _():
        carry[...] = jnp.zeros_like(carry)

    hn = _rms(x_ref[...], g_ref[...])
    packed = _pack_bf16_pairs(hn)
    hp_ref[0] = packed[:, :SC_ROW]
    hp_ref[1] = packed[:, SC_ROW:]

    h_hi, h_lo = _split2(hn)
    logits = (jnp.dot(h_hi, wh_ref[...], preferred_element_type=f32)
              + jnp.dot(h_lo, wh_ref[...], preferred_element_type=f32)
              + jnp.dot(h_hi, wl_ref[...], preferred_element_type=f32))
    lane = lax.broadcasted_iota(i32, (tm, LANE), 1)
    neg = jnp.float32(-jnp.inf)
    logits = jnp.where(lane < N_EXPERTS, logits, neg)
    m1 = jnp.max(logits, axis=-1, keepdims=True)
    i1 = jnp.min(jnp.where(logits == m1, lane, LANE), axis=-1, keepdims=True)
    l2 = jnp.where(lane == i1, neg, logits)
    m2 = jnp.max(l2, axis=-1, keepdims=True)
    i2 = jnp.min(jnp.where(l2 == m2, lane, LANE), axis=-1, keepdims=True)
    e21 = jnp.exp(m2 - m1)
    w1 = 1.0 / (1.0 + e21)
    w2 = e21 / (1.0 + e21)

    sel1 = lane == i1
    sel2 = lane == i2
    sel = (sel1 | sel2).astype(f32)
    ri = lax.broadcasted_iota(i32, (tm, tm), 0)
    ci = lax.broadcasted_iota(i32, (tm, tm), 1)
    tril = (ri >= ci).astype(bf16)
    incl = jnp.dot(tril, sel.astype(bf16), preferred_element_type=f32)
    rank = incl - sel + carry[0:1, :]
    r1 = jnp.sum(jnp.where(sel1, rank, 0.0), axis=-1, keepdims=True)
    r2 = jnp.sum(jnp.where(sel2, rank, 0.0), axis=-1, keepdims=True)
    new_cnt = carry[0:1, :] + incl[tm - 1:tm, :]
    carry[...] = jnp.broadcast_to(new_cnt, carry.shape)
    cnt_ref[...] = jnp.broadcast_to(new_cnt, cnt_ref.shape)

    meta = jnp.where(lane == 0, i1.astype(f32), 0.0)
    meta = jnp.where(lane == 1, i2.astype(f32), meta)
    meta = jnp.where(lane == 2, w1, meta)
    meta = jnp.where(lane == 3, w2, meta)
    meta = jnp.where(lane == 4, r1, meta)
    meta = jnp.where(lane == 5, r2, meta)
    meta_ref[...] = meta


def _router(x2, g, w_hi, w_lo):
    n = x2.shape[0]
    tm = TM_ROUTE
    row = lambda i: (i, 0)
    const = lambda i: (0, 0)
    return pl.pallas_call(
        _router_body,
        grid=(n // tm,),
        in_specs=[pl.BlockSpec((tm, D_MODEL), row),
                  pl.BlockSpec((1, D_MODEL), const),
                  pl.BlockSpec(w_hi.shape, const),
                  pl.BlockSpec(w_lo.shape, const)],
        out_specs=[pl.BlockSpec((2, tm, SC_ROW), lambda i: (0, i, 0)),
                   pl.BlockSpec((tm, LANE), row),
                   pl.BlockSpec((SUBLANE, LANE), const)],
        out_shape=[SDS((2, n, SC_ROW), u32), SDS((n, LANE), f32), SDS((SUBLANE, LANE), f32)],
        scratch_shapes=[pltpu.VMEM((SUBLANE, LANE), f32)],
        compiler_params=_cparams("arbitrary"),
        name="moe_router",
    )(x2, g, w_hi, w_lo)


def _sc_gather(table, idx):
    ni = idx.shape[0]
    mesh = plsc.VectorSubcoreMesh(core_axis_name="core", subcore_axis_name="subcore")
    idx2 = idx.reshape(1, ni)

    @pl.kernel(out_type=SDS((ni, SC_ROW), table.dtype), mesh=mesh)
    def kern(t_hbm, i_hbm, o_hbm):
        def body(i_vmem, o_vmem):
            pltpu.sync_copy(t_hbm.at[i_vmem.at[0]], o_vmem)

        pltpu.emit_pipeline(
            body, grid=(ni // SC_WINDOW,),
            in_specs=[pl.BlockSpec((1, SC_WINDOW), index_map=lambda i: (0, i))],
            out_specs=[pl.BlockSpec((SC_WINDOW, SC_ROW), index_map=lambda i: (i, 0))],
            core_axis_name=("core", "subcore"),
            dimension_semantics=(pltpu.PARALLEL,),
        )(i_hbm, o_hbm)

    return kern(table, idx2)


def _moe_ffn_body(be_ref, bv_ref, xs_ref, wg_ref, wu_ref, wd_ref, ys_ref, xb, acc):
    i = pl.program_id(0)
    f = pl.program_id(1)
    nf = pl.num_programs(1)

    @pl.when(bv_ref[i] == 1)
    def _():
        @pl.when(f == 0)
        def _():
            hi0, lo0 = _unpack_bf16_pairs(xs_ref[0])
            hi1, lo1 = _unpack_bf16_pairs(xs_ref[1])
            xb[...] = jnp.concatenate([hi0, hi1, lo0, lo1], axis=1)
            acc[...] = jnp.zeros_like(acc)

        x = xb[...]
        gate = jnp.dot(x, wg_ref[...], preferred_element_type=f32)
        up = jnp.dot(x, wu_ref[...], preferred_element_type=f32)
        act = (jax.nn.silu(gate) * up).astype(bf16)
        acc[...] += jnp.dot(act, wd_ref[...], preferred_element_type=f32)

        @pl.when(f == nf - 1)
        def _():
            packed = _pack_bf16_pairs(acc[...])
            ys_ref[0] = packed[:, :SC_ROW]
            ys_ref[1] = packed[:, SC_ROW:]

    @pl.when((bv_ref[i] == 0) & (f == nf - 1))
    def _():
        ys_ref[...] = jnp.zeros_like(ys_ref)


def _moe_ffn(blk_e, blk_v, xs, wg, wu, wd):
    npad = xs.shape[1]
    tm, tf = TM_MOE, TF_MOE
    nblk = npad // tm
    nf = D_FF_EXPERT // tf
    grid_spec = pltpu.PrefetchScalarGridSpec(
        num_scalar_prefetch=2,
        grid=(nblk, nf),
        in_specs=[pl.BlockSpec((2, tm, SC_ROW), lambda i, f, be, bv: (0, i, 0)),
                  pl.BlockSpec((None, D_MODEL, tf), lambda i, f, be, bv: (be[i], 0, f)),
                  pl.BlockSpec((None, D_MODEL, tf), lambda i, f, be, bv: (be[i], 0, f)),
                  pl.BlockSpec((None, tf, D_MODEL), lambda i, f, be, bv: (be[i], f, 0))],
        out_specs=pl.BlockSpec((2, tm, SC_ROW), lambda i, f, be, bv: (0, i, 0)),
        scratch_shapes=[pltpu.VMEM((tm, D_MODEL), bf16), pltpu.VMEM((tm, D_MODEL), f32)],
    )
    return pl.pallas_call(
        _moe_ffn_body,
        grid_spec=grid_spec,
        out_shape=SDS((2, npad, SC_ROW), u32),
        compiler_params=_cparams("parallel", "arbitrary"),
        name="moe_ffn",
    )(blk_e, blk_v, xs, wg, wu, wd)


def _combine_body(x_ref, yg_ref, meta_ref, g_ref, o_ref, *, final_norm):
    w1 = meta_ref[:, 2:3]
    w2 = meta_ref[:, 3:4]

    def rows(k):
        hi0, lo0 = _unpack_bf16_pairs(yg_ref[0, k])
        hi1, lo1 = _unpack_bf16_pairs(yg_ref[1, k])
        return jnp.concatenate([hi0, hi1, lo0, lo1], axis=1).astype(f32)

    xo = x_ref[...] + (w1 * rows(0) + w2 * rows(1))
    if final_norm:
        xo = _rms(xo, g_ref[...])
    o_ref[...] = xo


def _combine(x2, yg, meta, g_final, final_norm):
    n = x2.shape[0]
    tm = TM_ROUTE
    row = lambda i: (i, 0)
    return pl.pallas_call(
        functools.partial(_combine_body, final_norm=final_norm),
        grid=(n // tm,),
        in_specs=[pl.BlockSpec((tm, D_MODEL), row),
                  pl.BlockSpec((2, 2, tm, SC_ROW), lambda i: (0, 0, i, 0)),
                  pl.BlockSpec((tm, LANE), row),
                  pl.BlockSpec((1, D_MODEL), lambda i: (0, 0))],
        out_specs=pl.BlockSpec((tm, D_MODEL), row),
        out_shape=SDS((n, D_MODEL), f32),
        compiler_params=_cparams("parallel"),
        name="moe_combine",
    )(x2, yg, meta, g_final)


def _moe_layer(x2, g_ffn, w_router, wg, wu, wd, g_final, final_norm):
    n = x2.shape[0]
    tm = TM_MOE
    npad = 2 * n + N_EXPERTS * tm
    wr = jnp.zeros((D_MODEL, LANE), f32).at[:, :N_EXPERTS].set(w_router.astype(f32))
    wr_hi = wr.astype(bf16)
    wr_lo = (wr - wr_hi.astype(f32)).astype(bf16)
    hp, meta, cnt = _router(x2, g_ffn, wr_hi, wr_lo)

    counts = cnt[0, :N_EXPERTS].astype(i32)
    padded = ((counts + tm - 1) // tm) * tm
    ends = jnp.cumsum(padded)
    offs = ends - padded
    e12 = meta[:, 0:2].astype(i32)
    r12 = meta[:, 4:6].astype(i32)
    pos = offs[e12] + r12
    tok = jnp.broadcast_to(jnp.arange(n, dtype=i32)[:, None], (n, 2))
    src = jnp.zeros((npad,), i32).at[pos.reshape(-1)].set(tok.reshape(-1))
    blk_start = jnp.arange(npad // tm, dtype=i32) * tm
    blk_v = (blk_start < ends[-1]).astype(i32)
    blk_e = jnp.minimum(jnp.searchsorted(ends, blk_start, side='right').astype(i32), N_EXPERTS - 1)
    last_e = blk_e[jnp.maximum(ends[-1] // tm - 1, 0)]
    blk_e = jnp.where(blk_v == 1, blk_e, last_e)

    xs = _sc_gather(hp.reshape(2 * n, SC_ROW),
                    jnp.concatenate([src, src + n])).reshape(2, npad, SC_ROW)
    ys = _moe_ffn(blk_e, blk_v, xs, wg, wu, wd)
    pos_t = pos.T
    gidx = jnp.concatenate([pos_t.reshape(-1), pos_t.reshape(-1) + npad])
    yg = _sc_gather(ys.reshape(2 * npad, SC_ROW), gidx).reshape(2, 2, n, SC_ROW)
    return _combine(x2, yg, meta, g_final, final_norm)


def _row(v):
    return v.astype(f32).reshape(1, -1)


def kernel(x, norm_mix, w_in, s5_lambda_re, s5_lambda_im, s5_log_dt, s5_b_re, s5_b_im, s5_c_re, s5_c_im, s5_d, s5_w_glu, s5_b_glu, s5_out_norm, gla_w_a2, gla_b_a2, gla_out_norm, w_out, norm_ffn, ffn_w_gate, ffn_w_up, ffn_w_down, moe_w_router, moe_w_gate, moe_w_up, moe_w_down, norm_final):
    bsz, seq, _ = x.shape
    n = bsz * seq
    depth = w_in.shape[0]
    x2 = x.reshape(n, D_MODEL)
    for l in range(depth):
        wi = w_in[l]
        w_u, w_q, w_k, w_v, w_r, w_g = (wi[:, 0:512], wi[:, 512:768], wi[:, 768:1024],
                                        wi[:, 1024:1536], wi[:, 1536:2048], wi[:, 2048:2064])
        w_gp = jnp.zeros((D_MODEL, LANE), f32).at[:, :GLA_GATE_RANK].set(w_g)
        w_cat = jnp.concatenate([w_u, w_q * (GLA_DK ** -0.5), w_v, w_r, w_gp], axis=1).astype(bf16)
        w_kt = w_k.T.astype(bf16)
        u, q, kt, v, r, gl = _inproj(x2, _row(norm_mix[l]), w_cat, w_kt, bsz, seq)

        bmat, cmat, a_re, a_im = _s5_prep(s5_lambda_re[l], s5_lambda_im[l], s5_log_dt[l],
                                          s5_b_re[l], s5_b_im[l], s5_c_re[l], s5_c_im[l])
        ys = _s5(u.reshape(bsz, seq, D_S5), bmat, cmat, a_re, a_im, _row(s5_d[l]))

        wa = jnp.zeros((LANE, D_GLA_K), f32).at[:GLA_GATE_RANK].set(gla_w_a2[l]).astype(bf16)
        yg = _gla(q.reshape(bsz, seq, D_GLA_K), kt, v.reshape(bsz, seq, D_GLA),
                  r.reshape(bsz, seq, D_GLA), gl.reshape(bsz, seq, LANE),
                  wa, wa.T, _row(gla_b_a2[l]), gla_b_a2[l].astype(f32).reshape(-1, 1),
                  _row(gla_out_norm[l]))

        x2 = _outproj(x2, ys.reshape(n, D_S5), yg.reshape(n, D_GLA),
                      s5_w_glu[l].astype(bf16), _row(s5_b_glu[l]), _row(s5_out_norm[l]),
                      w_out[l].astype(bf16))

        i = l // 2
        last = l == depth - 1
        if l % 2 == 0:
            x2 = _ffn(x2, _row(norm_ffn[l]), ffn_w_gate[i].astype(bf16),
                      ffn_w_up[i].astype(bf16), ffn_w_down[i].astype(bf16))
            if last:
                x2 = _final_norm(x2, _row(norm_final))
        else:
            x2 = _moe_layer(x2, _row(norm_ffn[l]), moe_w_router[i], moe_w_gate[i].astype(bf16),
                            moe_w_up[i].astype(bf16), moe_w_down[i].astype(bf16),
                            _row(norm_final), last)
    return x2.reshape(bsz, seq, D_MODEL)


def _final_norm_body(x_ref, g_ref, o_ref):
    o_ref[...] = _rms(x_ref[...], g_ref[...])


def _final_norm(x2, g):
    n = x2.shape[0]
    tm = TM_ROUTE
    return pl.pallas_call(
        _final_norm_body,
        grid=(n // tm,),
        in_specs=[pl.BlockSpec((tm, D_MODEL), lambda i: (i, 0)),
                  pl.BlockSpec((1, D_MODEL), lambda i: (0, 0))],
        out_specs=pl.BlockSpec((tm, D_MODEL), lambda i: (i, 0)),
        out_shape=SDS((n, D_MODEL), f32),
        compiler_params=_cparams("parallel"),
        name="final_norm",
    )(x2, g)
```

```python
import functools
import math

import jax
import jax.numpy as jnp
from jax import lax
from jax.experimental import pallas as pl
from jax.experimental.pallas import tpu as pltpu
from jax.experimental.pallas import tpu_sc as plsc

f32 = jnp.float32
bf16 = jnp.bfloat16
u32 = jnp.uint32
i32 = jnp.int32
SDS = jax.ShapeDtypeStruct

D_MODEL = 1024
D_S5 = 512
S5_GROUP = 16
S5_GROUPS = 32
S5_STATE = 64
N_STATE = S5_GROUPS * S5_STATE
D_GLA = 512
GLA_HEADS = 4
GLA_DV = 128
GLA_DK = 64
D_GLA_K = 256
GLA_GATE_RANK = 16
GLA_TAU = 16.0
GLA_CHUNK = 64
D_FF = 2816
N_EXPERTS = 8
D_FF_EXPERT = 3584
EPS = 1e-6

LANE = 128
SUBLANE = 8
VMEM_LIMIT = 56 * 1024 * 1024

TM_PROJ = 512
TT_S5 = 128
S5_SLABS = 4
S5_PITCH_PAD = 8
TG_GLA = 256
TM_FFN = 512
TM_MOE = 1024
TF_MOE = 512
TM_ROUTE = 512
SC_WINDOW = 128
SC_ROW = 256


def _cparams(*sem):
    return pltpu.CompilerParams(dimension_semantics=sem, vmem_limit_bytes=VMEM_LIMIT)


def _rms(x, g):
    ms = jnp.mean(x * x, axis=-1, keepdims=True)
    return x * lax.rsqrt(ms + EPS) * g


def _inproj_body(x_ref, g_ref, w_ref, wkt_ref, u_ref, q_ref, kt_ref, v_ref, r_ref, gl_ref):
    hn = _rms(x_ref[...], g_ref[...]).astype(bf16)

    def proj(lo, hi):
        return jnp.dot(hn, w_ref[:, lo:hi], preferred_element_type=f32).astype(bf16)

    u_ref[...] = proj(0, 512)
    q_ref[...] = proj(512, 768)
    v_ref[...] = proj(768, 1280)
    r_ref[...] = proj(1280, 1792)
    gl_ref[...] = proj(1792, 1920)
    kt_ref[...] = lax.dot_general(wkt_ref[...], hn, (((1,), (1,)), ((), ())),
                                  preferred_element_type=f32).astype(bf16)


def _inproj(x2, g, w_cat, w_kt, bsz, seq):
    n = x2.shape[0]
    tm = TM_PROJ
    per_b = seq // tm
    row = lambda i: (i, 0)
    const = lambda i: (0, 0)
    return pl.pallas_call(
        _inproj_body,
        grid=(n // tm,),
        in_specs=[pl.BlockSpec((tm, D_MODEL), row),
                  pl.BlockSpec((1, D_MODEL), const),
                  pl.BlockSpec(w_cat.shape, const),
                  pl.BlockSpec(w_kt.shape, const)],
        out_specs=[pl.BlockSpec((tm, D_S5), row),
                   pl.BlockSpec((tm, D_GLA_K), row),
                   pl.BlockSpec((None, D_GLA_K, tm), lambda i: (i // per_b, 0, i % per_b)),
                   pl.BlockSpec((tm, D_GLA), row),
                   pl.BlockSpec((tm, D_GLA), row),
                   pl.BlockSpec((tm, LANE), row)],
        out_shape=[SDS((n, D_S5), bf16), SDS((n, D_GLA_K), bf16),
                   SDS((bsz, D_GLA_K, seq), bf16), SDS((n, D_GLA), bf16),
                   SDS((n, D_GLA), bf16), SDS((n, LANE), bf16)],
        compiler_params=_cparams("parallel"),
        name="inproj",
    )(x2, g, w_cat, w_kt)


N_SLAB = N_STATE // LANE


def _s5_prep(lam_re, lam_im, log_dt, b_re, b_im, c_re, c_im):
    lr = jnp.minimum(lam_re.astype(f32), -1e-4)
    li = lam_im.astype(f32)
    dt = jnp.exp(log_dt.astype(f32))[:, None]
    mag = jnp.exp(lr * dt)
    ab_re = mag * jnp.cos(li * dt)
    ab_im = mag * jnp.sin(li * dt)
    nr = ab_re - 1.0
    ni = ab_im
    den = lr * lr + li * li
    f_re = (nr * lr + ni * li) / den
    f_im = (ni * lr - nr * li) / den
    br = b_re.astype(f32)
    bi = b_im.astype(f32)
    bb_re = f_re[..., None] * br - f_im[..., None] * bi
    bb_im = f_re[..., None] * bi + f_im[..., None] * br

    def b_tiles(bb):
        bt = jnp.transpose(bb, (0, 2, 1)).reshape(8, 4, S5_GROUP, S5_STATE)
        gl = jnp.arange(8)[None, :, None]
        gs = jnp.arange(4)[None, None, :]
        nn = jnp.arange(8)[:, None, None]
        sel = (gl == 4 * (nn % 2) + gs).astype(f32)
        t = jnp.einsum('ngs,nscp->ngcsp', sel, bt)
        return t.reshape(8, LANE, 4 * S5_STATE)

    bmat = jnp.concatenate([b_tiles(bb_re), b_tiles(bb_im)], axis=0).astype(bf16)

    def c_tiles(c, sign):
        ct = jnp.transpose(c.astype(f32), (0, 2, 1)).reshape(2, 16, S5_STATE, S5_GROUP)
        eye = jnp.eye(16, dtype=f32)
        t = jnp.einsum('gh,jgpo->jgpho', eye, ct) * sign
        return t.reshape(2, 16 * S5_STATE, 16 * S5_GROUP)

    cmat = jnp.stack([c_tiles(c_re, 1.0), c_tiles(c_im, -1.0)], axis=1).astype(bf16)
    a_re = ab_re.reshape(N_SLAB, 1, LANE)
    a_im = ab_im.reshape(N_SLAB, 1, LANE)
    return bmat, cmat, a_re, a_im


def _s5_body(u_ref, bm_ref, cm_ref, are_ref, aim_ref, d_ref, y_ref, hbuf, hstate):
    tt = u_ref.shape[1]
    rows = SUBLANE * tt
    pitch = tt + S5_PITCH_PAD

    @pl.when(pl.program_id(0) == 0)
    def _():
        hstate[...] = jnp.zeros_like(hstate)

    u2 = u_ref[...].reshape(rows, D_S5)

    def put(slab, val):
        for b in range(SUBLANE):
            hbuf[slab, b * pitch:b * pitch + tt, :] = val[b * tt:(b + 1) * tt, :]

    def get(slab):
        return jnp.concatenate([hbuf[slab, b * pitch:b * pitch + tt, :] for b in range(SUBLANE)], axis=0)

    for n in range(2 * 8):
        ks = LANE * ((n % 8) // 2)
        res = jnp.dot(u2[:, ks:ks + LANE], bm_ref[n], preferred_element_type=f32)
        put(2 * n, res[:, :LANE])
        put(2 * n + 1, res[:, LANE:])

    for c0 in range(0, N_SLAB, S5_SLABS):
        ar = [jnp.broadcast_to(are_ref[c0 + s], (SUBLANE, LANE)) for s in range(S5_SLABS)]
        ai = [jnp.broadcast_to(aim_ref[c0 + s], (SUBLANE, LANE)) for s in range(S5_SLABS)]

        def step(t, carry):
            out = []
            for s in range(S5_SLABS):
                hr, hi = carry[2 * s], carry[2 * s + 1]
                sl = pl.ds(t, SUBLANE, stride=pitch)
                bur = hbuf[c0 + s, sl, :]
                bui = hbuf[N_SLAB + c0 + s, sl, :]
                nr = ar[s] * hr - ai[s] * hi + bur
                ni = ar[s] * hi + ai[s] * hr + bui
                hbuf[c0 + s, sl, :] = nr
                hbuf[N_SLAB + c0 + s, sl, :] = ni
                out += [nr, ni]
            return tuple(out)

        init = []
        for s in range(S5_SLABS):
            init += [hstate[c0 + s], hstate[N_SLAB + c0 + s]]
        fin = lax.fori_loop(0, tt, step, tuple(init), unroll=4)
        for s in range(S5_SLABS):
            hstate[c0 + s] = fin[2 * s]
            hstate[N_SLAB + c0 + s] = fin[2 * s + 1]

    for j in range(2):
        h_re = jnp.concatenate([get(8 * j + s) for s in range(8)], axis=1).astype(bf16)
        h_im = jnp.concatenate([get(N_SLAB + 8 * j + s) for s in range(8)], axis=1).astype(bf16)
        y = jnp.dot(h_re, cm_ref[j, 0], preferred_element_type=f32)
        y = y + jnp.dot(h_im, cm_ref[j, 1], preferred_element_type=f32)
        cs = slice(256 * j, 256 * j + 256)
        y = y + d_ref[:, cs] * u2[:, cs].astype(f32)
        y_ref[:, :, cs] = jax.nn.gelu(y).astype(bf16).reshape(SUBLANE, tt, 256)


def _s5(u3, bmat, cmat, a_re, a_im, d_skip):
    bsz, seq, _ = u3.shape
    assert bsz == SUBLANE
    tt = TT_S5
    blk = lambda i: (0, i, 0)
    return pl.pallas_call(
        _s5_body,
        grid=(seq // tt,),
        in_specs=[pl.BlockSpec((bsz, tt, D_S5), blk),
                  pl.BlockSpec(bmat.shape, lambda i: (0, 0, 0)),
                  pl.BlockSpec(cmat.shape, lambda i: (0, 0, 0, 0)),
                  pl.BlockSpec(a_re.shape, lambda i: (0, 0, 0)),
                  pl.BlockSpec(a_im.shape, lambda i: (0, 0, 0)),
                  pl.BlockSpec((1, D_S5), lambda i: (0, 0))],
        out_specs=pl.BlockSpec((bsz, tt, D_S5), blk),
        out_shape=SDS((bsz, seq, D_S5), bf16),
        scratch_shapes=[pltpu.VMEM((2 * N_SLAB, bsz * (tt + S5_PITCH_PAD), LANE), f32),
                        pltpu.VMEM((2 * N_SLAB, SUBLANE, LANE), f32)],
        compiler_params=_cparams("arbitrary"),
        name="s5_scan",
    )(u3, bmat, cmat, a_re, a_im, d_skip)


def _split2(x):
    hi = x.astype(bf16)
    lo = (x - hi.astype(f32)).astype(bf16)
    return hi, lo


def _log_sigmoid(x):
    return -(jnp.maximum(-x, 0.0) + jnp.log1p(jnp.exp(-jnp.abs(x))))


def _gla_body(q_ref, kt_ref, v_ref, r_ref, g_ref, wa_ref, wat_ref, ba_ref, bat_ref, gn_ref,
              o_ref, s_ref):
    c = GLA_CHUNK
    tg = q_ref.shape[0]

    @pl.when(pl.program_id(1) == 0)
    def _():
        s_ref[...] = jnp.zeros_like(s_ref)

    nc = tg // c
    nh = GLA_HEADS
    ri = lax.broadcasted_iota(i32, (tg, tg), 0)
    ci = lax.broadcasted_iota(i32, (tg, tg), 1)
    same = (ri // c) == (ci // c)
    tril = (same & (ri >= ci)).astype(bf16)
    triu = (same & (ri <= ci)).astype(bf16)
    blk = same.astype(bf16)

    g = g_ref[...]
    la = _log_sigmoid(jnp.dot(g, wa_ref[...], preferred_element_type=f32) + ba_ref[...]) / GLA_TAU
    lat = _log_sigmoid(lax.dot_general(wat_ref[...], g, (((1,), (1,)), ((), ())),
                                       preferred_element_type=f32) + bat_ref[...]) / GLA_TAU
    la_hi, la_lo = _split2(la)
    cum = (jnp.dot(tril, la_hi, preferred_element_type=f32)
           + jnp.dot(tril, la_lo, preferred_element_type=f32))
    lat_hi, lat_lo = _split2(lat)
    cumt = (jnp.dot(lat_hi, triu, preferred_element_type=f32)
            + jnp.dot(lat_lo, triu, preferred_element_type=f32))
    clt = (jnp.dot(lat_hi, blk, preferred_element_type=f32)
           + jnp.dot(lat_lo, blk, preferred_element_type=f32))

    qt = q_ref[...].astype(f32) * jnp.exp(cum)
    ktt = kt_ref[...].astype(f32)
    k_t = (ktt * jnp.exp(-cumt)).astype(bf16)
    k_end = ktt * jnp.exp(clt - cumt)
    v = v_ref[...]

    head_of_lane = lax.broadcasted_iota(i32, (c, D_GLA_K), 1) // GLA_DK
    q_stack = jnp.concatenate(
        [jnp.where(head_of_lane == h, qt[cc * c:(cc + 1) * c, :], 0.0)
         for cc in range(nc) for h in range(nh)], axis=0).astype(bf16)
    scores = jnp.dot(q_stack, k_t, preferred_element_type=f32)
    r_idx = lax.broadcasted_iota(i32, (nc * nh * c, tg), 0)
    c_idx = lax.broadcasted_iota(i32, (nc * nh * c, tg), 1)
    causal = (r_idx // (nh * c) == c_idx // c) & (r_idx % c >= c_idx % c)
    scores = jnp.where(causal, scores, 0.0).astype(bf16)
    o_full = jnp.dot(scores, v, preferred_element_type=f32)

    chunk_of_lane = lax.broadcasted_iota(i32, (D_GLA_K, tg), 1) // c
    k_stack = jnp.concatenate(
        [jnp.where(chunk_of_lane == cc, k_end, 0.0) for cc in range(nc)], axis=0).astype(bf16)
    upd_full = jnp.dot(k_stack, v, preferred_element_type=f32)

    state = s_ref[...]
    o_inter = []
    for cc in range(nc):
        o_inter.append(jnp.dot(q_stack[cc * nh * c:(cc + 1) * nh * c, :], state.astype(bf16),
                               preferred_element_type=f32))
        upd = jnp.concatenate(
            [upd_full[cc * D_GLA_K + h * GLA_DK:cc * D_GLA_K + (h + 1) * GLA_DK,
                      h * GLA_DV:(h + 1) * GLA_DV] for h in range(nh)], axis=0)
        state = jnp.exp(clt[:, cc * c:cc * c + 1]) * state + upd
    s_ref[...] = state

    rows = []
    for cc in range(nc):
        outs = []
        for h in range(nh):
            r0 = (cc * nh + h) * c
            o = o_full[r0:r0 + c, h * GLA_DV:(h + 1) * GLA_DV] + o_inter[cc][h * c:(h + 1) * c, :]
            o = o * lax.rsqrt(jnp.mean(o * o, axis=-1, keepdims=True) + EPS)
            outs.append(o)
        rows.append(jnp.concatenate(outs, axis=1))
    o_cat = jnp.concatenate(rows, axis=0)
    o_ref[...] = (o_cat * gn_ref[...] * jax.nn.silu(r_ref[...].astype(f32))).astype(bf16)


def _gla(q3, kt3, v3, r3, g3, wa, wat, ba, bat, gn):
    bsz, seq, _ = q3.shape
    tg = TG_GLA
    tok = lambda b, i: (b, i, 0)
    c2 = lambda b, i: (0, 0)
    return pl.pallas_call(
        _gla_body,
        grid=(bsz, seq // tg),
        in_specs=[pl.BlockSpec((None, tg, D_GLA_K), tok),
                  pl.BlockSpec((None, D_GLA_K, tg), lambda b, i: (b, 0, i)),
                  pl.BlockSpec((None, tg, D_GLA), tok),
                  pl.BlockSpec((None, tg, D_GLA), tok),
                  pl.BlockSpec((None, tg, LANE), tok),
                  pl.BlockSpec(wa.shape, c2), pl.BlockSpec(wat.shape, c2),
                  pl.BlockSpec(ba.shape, c2), pl.BlockSpec(bat.shape, c2),
                  pl.BlockSpec(gn.shape, c2)],
        out_specs=pl.BlockSpec((None, tg, D_GLA), tok),
        out_shape=SDS((bsz, seq, D_GLA), bf16),
        scratch_shapes=[pltpu.VMEM((D_GLA_K, GLA_DV), f32)],
        compiler_params=_cparams("parallel", "arbitrary"),
        name="gla",
    )(q3, kt3, v3, r3, g3, wa, wat, ba, bat, gn)


def _outproj_body(x_ref, ys_ref, yg_ref, wg_ref, bg_ref, gs_ref, wo_ref, o_ref):
    y = ys_ref[...]
    z = jnp.dot(y, wg_ref[...], preferred_element_type=f32) + bg_ref[...]
    yf = y.astype(f32)
    yf = yf * jax.nn.sigmoid(z)
    ys = _rms(yf, gs_ref[...]).astype(bf16)
    acc = jnp.dot(ys, wo_ref[0:D_S5, :], preferred_element_type=f32)
    acc = acc + jnp.dot(yg_ref[...], wo_ref[D_S5:, :], preferred_element_type=f32)
    o_ref[...] = x_ref[...] + acc


def _outproj(x2, ys, yg, w_glu, b_glu, g_s5, w_out):
    n = x2.shape[0]
    tm = TM_PROJ
    row = lambda i: (i, 0)
    const = lambda i: (0, 0)
    return pl.pallas_call(
        _outproj_body,
        grid=(n // tm,),
        in_specs=[pl.BlockSpec((tm, D_MODEL), row),
                  pl.BlockSpec((tm, D_S5), row),
                  pl.BlockSpec((tm, D_GLA), row),
                  pl.BlockSpec(w_glu.shape, const),
                  pl.BlockSpec((1, D_S5), const),
                  pl.BlockSpec((1, D_S5), const),
                  pl.BlockSpec(w_out.shape, const)],
        out_specs=pl.BlockSpec((tm, D_MODEL), row),
        out_shape=SDS((n, D_MODEL), f32),
        compiler_params=_cparams("parallel"),
        name="outproj",
    )(x2, ys, yg, w_glu, b_glu, g_s5, w_out)


def _ffn_body(x_ref, g_ref, wg_ref, wu_ref, wd_ref, o_ref):
    x = x_ref[...]
    hn = _rms(x, g_ref[...]).astype(bf16)
    gate = jnp.dot(hn, wg_ref[...], preferred_element_type=f32)
    up = jnp.dot(hn, wu_ref[...], preferred_element_type=f32)
    act = (jax.nn.silu(gate) * up).astype(bf16)
    o_ref[...] = x + jnp.dot(act, wd_ref[...], preferred_element_type=f32)


def _ffn(x2, g, wg, wu, wd):
    n = x2.shape[0]
    tm = TM_FFN
    row = lambda i: (i, 0)
    const = lambda i: (0, 0)
    once = dict(pipeline_mode=pl.Buffered(1))
    return pl.pallas_call(
        _ffn_body,
        grid=(n // tm,),
        in_specs=[pl.BlockSpec((tm, D_MODEL), row),
                  pl.BlockSpec((1, D_MODEL), const),
                  pl.BlockSpec(wg.shape, const, **once),
                  pl.BlockSpec(wu.shape, const, **once),
                  pl.BlockSpec(wd.shape, const, **once)],
        out_specs=pl.BlockSpec((tm, D_MODEL), row),
        out_shape=SDS((n, D_MODEL), f32),
        compiler_params=_cparams("parallel"),
        name="ffn_dense",
    )(x2, g, wg, wu, wd)


def _pack_bf16_pairs(a):
    bits = pltpu.bitcast(a.astype(bf16).astype(f32), u32)
    half = a.shape[1] // 2
    return bits[:, :half] | (bits[:, half:] >> 16)


def _unpack_bf16_pairs(p):
    hi = pltpu.bitcast(p & jnp.uint32(0xFFFF0000), f32).astype(bf16)
    lo = pltpu.bitcast(p << 16, f32).astype(bf16)
    return hi, lo


def _router_body(x_ref, g_ref, wh_ref, wl_ref, hp_ref, meta_ref, cnt_ref, carry):
    tm = x_ref.shape[0]

    @pl.when(pl.program_id(0) == 0)
    def _():
        carry[...] = jnp.zeros_like(carry)

    hn = _rms(x_ref[...], g_ref[...])
    packed = _pack_bf16_pairs(hn)
    hp_ref[0] = packed[:, :SC_ROW]
    hp_ref[1] = packed[:, SC_ROW:]

    h_hi, h_lo = _split2(hn)
    logits = (jnp.dot(h_hi, wh_ref[...], preferred_element_type=f32)
              + jnp.dot(h_lo, wh_ref[...], preferred_element_type=f32)
              + jnp.dot(h_hi, wl_ref[...], preferred_element_type=f32))
    lane = lax.broadcasted_iota(i32, (tm, LANE), 1)
    neg = jnp.float32(-jnp.inf)
    logits = jnp.where(lane < N_EXPERTS, logits, neg)
    m1 = jnp.max(logits, axis=-1, keepdims=True)
    i1 = jnp.min(jnp.where(logits == m1, lane, LANE), axis=-1, keepdims=True)
    l2 = jnp.where(lane == i1, neg, logits)
    m2 = jnp.max(l2, axis=-1, keepdims=True)
    i2 = jnp.min(jnp.where(l2 == m2, lane, LANE), axis=-1, keepdims=True)
    e21 = jnp.exp(m2 - m1)
    w1 = 1.0 / (1.0 + e21)
    w2 = e21 / (1.0 + e21)

    sel1 = lane == i1
    sel2 = lane == i2
    sel = (sel1 | sel2).astype(f32)
    ri = lax.broadcasted_iota(i32, (tm, tm), 0)
    ci = lax.broadcasted_iota(i32, (tm, tm), 1)
    tril = (ri >= ci).astype(bf16)
    incl = jnp.dot(tril, sel.astype(bf16), preferred_element_type=f32)
    rank = incl - sel + carry[0:1, :]
    r1 = jnp.sum(jnp.where(sel1, rank, 0.0), axis=-1, keepdims=True)
    r2 = jnp.sum(jnp.where(sel2, rank, 0.0), axis=-1, keepdims=True)
    new_cnt = carry[0:1, :] + incl[tm - 1:tm, :]
    carry[...] = jnp.broadcast_to(new_cnt, carry.shape)
    cnt_ref[...] = jnp.broadcast_to(new_cnt, cnt_ref.shape)

    meta = jnp.where(lane == 0, i1.astype(f32), 0.0)
    meta = jnp.where(lane == 1, i2.astype(f32), meta)
    meta = jnp.where(lane == 2, w1, meta)
    meta = jnp.where(lane == 3, w2, meta)
    meta = jnp.where(lane == 4, r1, meta)
    meta = jnp.where(lane == 5, r2, meta)
    meta_ref[...] = meta


def _router(x2, g, w_hi, w_lo):
    n = x2.shape[0]
    tm = TM_ROUTE
    row = lambda i: (i, 0)
    const = lambda i: (0, 0)
    return pl.pallas_call(
        _router_body,
        grid=(n // tm,),
        in_specs=[pl.BlockSpec((tm, D_MODEL), row),
                  pl.BlockSpec((1, D_MODEL), const),
                  pl.BlockSpec(w_hi.shape, const),
                  pl.BlockSpec(w_lo.shape, const)],
        out_specs=[pl.BlockSpec((2, tm, SC_ROW), lambda i: (0, i, 0)),
                   pl.BlockSpec((tm, LANE), row),
                   pl.BlockSpec((SUBLANE, LANE), const)],
        out_shape=[SDS((2, n, SC_ROW), u32), SDS((n, LANE), f32), SDS((SUBLANE, LANE), f32)],
        scratch_shapes=[pltpu.VMEM((SUBLANE, LANE), f32)],
        compiler_params=_cparams("arbitrary"),
        name="moe_router",
    )(x2, g, w_hi, w_lo)


def _sc_gather(table, idx):
    ni = idx.shape[0]
    mesh = plsc.VectorSubcoreMesh(core_axis_name="core", subcore_axis_name="subcore")
    idx2 = idx.reshape(1, ni)

    @pl.kernel(out_type=SDS((ni, SC_ROW), table.dtype), mesh=mesh)
    def kern(t_hbm, i_hbm, o_hbm):
        def body(i_vmem, o_vmem):
            pltpu.sync_copy(t_hbm.at[i_vmem.at[0]], o_vmem)

        pltpu.emit_pipeline(
            body, grid=(ni // SC_WINDOW,),
            in_specs=[pl.BlockSpec((1, SC_WINDOW), index_map=lambda i: (0, i))],
            out_specs=[pl.BlockSpec((SC_WINDOW, SC_ROW), index_map=lambda i: (i, 0))],
            core_axis_name=("core", "subcore"),
            dimension_semantics=(pltpu.PARALLEL,),
        )(i_hbm, o_hbm)

    return kern(table, idx2)


def _sc_scatter2(x, idx0, idx1, nrows):
    ni = x.shape[0]
    mesh = plsc.VectorSubcoreMesh(core_axis_name="core", subcore_axis_name="subcore")

    @pl.kernel(out_type=SDS((nrows, SC_ROW), x.dtype), mesh=mesh)
    def kern(x_hbm, i0_hbm, i1_hbm, o_hbm):
        def body(x_vmem, i0_vmem, i1_vmem):
            pltpu.sync_copy(x_vmem, o_hbm.at[i0_vmem.at[0]])
            pltpu.sync_copy(x_vmem, o_hbm.at[i1_vmem.at[0]])

        pltpu.emit_pipeline(
            body, grid=(ni // SC_WINDOW,),
            in_specs=[pl.BlockSpec((SC_WINDOW, SC_ROW), index_map=lambda i: (i, 0)),
                      pl.BlockSpec((1, SC_WINDOW), index_map=lambda i: (0, i)),
                      pl.BlockSpec((1, SC_WINDOW), index_map=lambda i: (0, i))],
            out_specs=[],
            core_axis_name=("core", "subcore"),
            dimension_semantics=(pltpu.PARALLEL,),
        )(x_hbm, i0_hbm, i1_hbm)

    return kern(x, idx0.reshape(1, ni), idx1.reshape(1, ni))


def _moe_ffn_body(be_ref, bn_ref, xs_ref, wg_ref, wu_ref, wd_ref, ys_ref, xb, acc):
    i = pl.program_id(0)
    f = pl.program_id(1)
    nf = pl.num_programs(1)
    nvalid = bn_ref[i]

    @pl.when(nvalid > 0)
    def _():
        @pl.when(f == 0)
        def _():
            live = lax.broadcasted_iota(i32, (xs_ref.shape[1], SC_ROW), 0) < nvalid
            hi0, lo0 = _unpack_bf16_pairs(jnp.where(live, xs_ref[0], jnp.uint32(0)))
            hi1, lo1 = _unpack_bf16_pairs(jnp.where(live, xs_ref[1], jnp.uint32(0)))
            xb[...] = jnp.concatenate([hi0, hi1, lo0, lo1], axis=1)
            acc[...] = jnp.zeros_like(acc)

        x = xb[...]
        gate = jnp.dot(x, wg_ref[...].astype(bf16), preferred_element_type=f32)
        up = jnp.dot(x, wu_ref[...].astype(bf16), preferred_element_type=f32)
        act = (jax.nn.silu(gate) * up).astype(bf16)
        acc[...] += jnp.dot(act, wd_ref[...].astype(bf16), preferred_element_type=f32)

        @pl.when(f == nf - 1)
        def _():
            packed = _pack_bf16_pairs(acc[...])
            ys_ref[0] = packed[:, :SC_ROW]
            ys_ref[1] = packed[:, SC_ROW:]

    @pl.when((nvalid == 0) & (f == nf - 1))
    def _():
        ys_ref[...] = jnp.zeros_like(ys_ref)


def _moe_ffn(blk_e, blk_n, xs, wg, wu, wd):
    npad = xs.shape[1]
    tm, tf = TM_MOE, TF_MOE
    nblk = npad // tm
    nf = D_FF_EXPERT // tf

    def ftile(i, f, bn):
        return jnp.where(bn[i] > 0, f, nf - 1)

    grid_spec = pltpu.PrefetchScalarGridSpec(
        num_scalar_prefetch=2,
        grid=(nblk, nf),
        in_specs=[pl.BlockSpec((2, tm, SC_ROW), lambda i, f, be, bn: (0, i, 0)),
                  pl.BlockSpec((None, D_MODEL, tf), lambda i, f, be, bn: (be[i], 0, ftile(i, f, bn))),
                  pl.BlockSpec((None, D_MODEL, tf), lambda i, f, be, bn: (be[i], 0, ftile(i, f, bn))),
                  pl.BlockSpec((None, tf, D_MODEL), lambda i, f, be, bn: (be[i], ftile(i, f, bn), 0))],
        out_specs=pl.BlockSpec((2, tm, SC_ROW), lambda i, f, be, bn: (0, i, 0)),
        scratch_shapes=[pltpu.VMEM((tm, D_MODEL), bf16), pltpu.VMEM((tm, D_MODEL), f32)],
    )
    return pl.pallas_call(
        _moe_ffn_body,
        grid_spec=grid_spec,
        out_shape=SDS((2, npad, SC_ROW), u32),
        compiler_params=_cparams("parallel", "arbitrary"),
        name="moe_ffn",
    )(blk_e, blk_n, xs, wg, wu, wd)


def _combine_body(x_ref, yg_ref, meta_ref, g_ref, o_ref, *, final_norm):
    w1 = meta_ref[:, 2:3]
    w2 = meta_ref[:, 3:4]

    def rows(k):
        hi0, lo0 = _unpack_bf16_pairs(yg_ref[0, k])
        hi1, lo1 = _unpack_bf16_pairs(yg_ref[1, k])
        return jnp.concatenate([hi0, hi1, lo0, lo1], axis=1).astype(f32)

    xo = x_ref[...] + (w1 * rows(0) + w2 * rows(1))
    if final_norm:
        xo = _rms(xo, g_ref[...])
    o_ref[...] = xo


def _combine(x2, yg, meta, g_final, final_norm):
    n = x2.shape[0]
    tm = TM_ROUTE
    row = lambda i: (i, 0)
    return pl.pallas_call(
        functools.partial(_combine_body, final_norm=final_norm),
        grid=(n // tm,),
        in_specs=[pl.BlockSpec((tm, D_MODEL), row),
                  pl.BlockSpec((2, 2, tm, SC_ROW), lambda i: (0, 0, i, 0)),
                  pl.BlockSpec((tm, LANE), row),
                  pl.BlockSpec((1, D_MODEL), lambda i: (0, 0))],
        out_specs=pl.BlockSpec((tm, D_MODEL), row),
        out_shape=SDS((n, D_MODEL), f32),
        compiler_params=_cparams("parallel"),
        name="moe_combine",
    )(x2, yg, meta, g_final)


def _moe_layer(x2, g_ffn, w_router, wg, wu, wd, g_final, final_norm):
    n = x2.shape[0]
    tm = TM_MOE
    npad = 2 * n + N_EXPERTS * tm
    wr = jnp.zeros((D_MODEL, LANE), f32).at[:, :N_EXPERTS].set(w_router.astype(f32))
    wr_hi = wr.astype(bf16)
    wr_lo = (wr - wr_hi.astype(f32)).astype(bf16)
    hp, meta, cnt = _router(x2, g_ffn, wr_hi, wr_lo)

    counts = cnt[0, :N_EXPERTS].astype(i32)
    padded = ((counts + tm - 1) // tm) * tm
    ends = jnp.cumsum(padded)
    offs = ends - padded
    e12 = meta[:, 0:2].astype(i32)
    r12 = meta[:, 4:6].astype(i32)
    pos = offs[e12] + r12
    blk_start = jnp.arange(npad // tm, dtype=i32) * tm
    blk_e = jnp.minimum(jnp.searchsorted(ends, blk_start, side='right').astype(i32), N_EXPERTS - 1)
    blk_n = jnp.clip(offs[blk_e] + counts[blk_e] - blk_start, 0, tm)
    blk_n = jnp.where(blk_start < ends[-1], blk_n, 0)
    last_e = blk_e[jnp.maximum(ends[-1] // tm - 1, 0)]
    blk_e = jnp.where(blk_n > 0, blk_e, last_e)

    pos_t = pos.T
    xs = _sc_scatter2(hp.reshape(2 * n, SC_ROW),
                      jnp.concatenate([pos_t[0], pos_t[0] + npad]),
                      jnp.concatenate([pos_t[1], pos_t[1] + npad]),
                      2 * npad).reshape(2, npad, SC_ROW)
    ys = _moe_ffn(blk_e, blk_n, xs, wg, wu, wd)
    gidx = jnp.concatenate([pos_t.reshape(-1), pos_t.reshape(-1) + npad])
    yg = _sc_gather(ys.reshape(2 * npad, SC_ROW), gidx).reshape(2, 2, n, SC_ROW)
    return _combine(x2, yg, meta, g_final, final_norm)


def _row(v):
    return v.astype(f32).reshape(1, -1)


def kernel(x, norm_mix, w_in, s5_lambda_re, s5_lambda_im, s5_log_dt, s5_b_re, s5_b_im, s5_c_re, s5_c_im, s5_d, s5_w_glu, s5_b_glu, s5_out_norm, gla_w_a2, gla_b_a2, gla_out_norm, w_out, norm_ffn, ffn_w_gate, ffn_w_up, ffn_w_down, moe_w_router, moe_w_gate, moe_w_up, moe_w_down, norm_final):
    bsz, seq, _ = x.shape
    n = bsz * seq
    depth = w_in.shape[0]
    x2 = x.reshape(n, D_MODEL)
    for l in range(depth):
        wi = w_in[l]
        w_u, w_q, w_k, w_v, w_r, w_g = (wi[:, 0:512], wi[:, 512:768], wi[:, 768:1024],
                                        wi[:, 1024:1536], wi[:, 1536:2048], wi[:, 2048:2064])
        w_gp = jnp.zeros((D_MODEL, LANE), f32).at[:, :GLA_GATE_RANK].set(w_g)
        w_cat = jnp.concatenate([w_u, w_q * (GLA_DK ** -0.5), w_v, w_r, w_gp], axis=1).astype(bf16)
        w_kt = w_k.T.astype(bf16)
        u, q, kt, v, r, gl = _inproj(x2, _row(norm_mix[l]), w_cat, w_kt, bsz, seq)

        bmat, cmat, a_re, a_im = _s5_prep(s5_lambda_re[l], s5_lambda_im[l], s5_log_dt[l],
                                          s5_b_re[l], s5_b_im[l], s5_c_re[l], s5_c_im[l])
        ys = _s5(u.reshape(bsz, seq, D_S5), bmat, cmat, a_re, a_im, _row(s5_d[l]))

        wa = jnp.zeros((LANE, D_GLA_K), f32).at[:GLA_GATE_RANK].set(gla_w_a2[l]).astype(bf16)
        yg = _gla(q.reshape(bsz, seq, D_GLA_K), kt, v.reshape(bsz, seq, D_GLA),
                  r.reshape(bsz, seq, D_GLA), gl.reshape(bsz, seq, LANE),
                  wa, wa.T, _row(gla_b_a2[l]), gla_b_a2[l].astype(f32).reshape(-1, 1),
                  _row(gla_out_norm[l]))

        x2 = _outproj(x2, ys.reshape(n, D_S5), yg.reshape(n, D_GLA),
                      s5_w_glu[l].astype(bf16), _row(s5_b_glu[l]), _row(s5_out_norm[l]),
                      w_out[l].astype(bf16))

        i = l // 2
        last = l == depth - 1
        if l % 2 == 0:
            x2 = _ffn(x2, _row(norm_ffn[l]), ffn_w_gate[i].astype(bf16),
                      ffn_w_up[i].astype(bf16), ffn_w_down[i].astype(bf16))
            if last:
                x2 = _final_norm(x2, _row(norm_final))
        else:
            x2 = _moe_layer(x2, _row(norm_ffn[l]), moe_w_router[i], moe_w_gate[i],
                            moe_w_up[i], moe_w_down[i], _row(norm_final), last)
    return x2.reshape(bsz, seq, D_MODEL)


def _final_norm_body(x_ref, g_ref, o_ref):
    o_ref[...] = _rms(x_ref[...], g_ref[...])


def _final_norm(x2, g):
    n = x2.shape[0]
    tm = TM_ROUTE
    return pl.pallas_call(
        _final_norm_body,
        grid=(n // tm,),
        in_specs=[pl.BlockSpec((tm, D_MODEL), lambda i: (i, 0)),
                  pl.BlockSpec((1, D_MODEL), lambda i: (0, 0))],
        out_specs=pl.BlockSpec((tm, D_MODEL), lambda i: (i, 0)),
        out_shape=SDS((n, D_MODEL), f32),
        compiler_params=_cparams("parallel"),
        name="final_norm",
    )(x2, g)
```

```python
import functools
import math

import jax
import jax.numpy as jnp
from jax import lax
from jax.experimental import pallas as pl
from jax.experimental.pallas import tpu as pltpu
from jax.experimental.pallas import tpu_sc as plsc

f32 = jnp.float32
bf16 = jnp.bfloat16
u32 = jnp.uint32
i32 = jnp.int32
SDS = jax.ShapeDtypeStruct

D_MODEL = 1024
D_S5 = 512
S5_GROUP = 16
S5_GROUPS = 32
S5_STATE = 64
N_STATE = S5_GROUPS * S5_STATE
D_GLA = 512
GLA_HEADS = 4
GLA_DV = 128
GLA_DK = 64
D_GLA_K = 256
GLA_GATE_RANK = 16
GLA_TAU = 16.0
GLA_CHUNK = 64
D_FF = 2816
N_EXPERTS = 8
D_FF_EXPERT = 3584
EPS = 1e-6

LANE = 128
SUBLANE = 8
VMEM_LIMIT = 56 * 1024 * 1024

TM_PROJ = 512
TT_S5 = 128
S5_SLABS = 4
S5_PITCH_PAD = 8
TG_GLA = 256
TM_FFN = 512
TM_MOE = 1024
TF_MOE = 512
TM_ROUTE = 512
SC_WINDOW = 128
SC_ROW = 256


def _cparams(*sem):
    return pltpu.CompilerParams(dimension_semantics=sem, vmem_limit_bytes=VMEM_LIMIT)


def _rms(x, g):
    ms = jnp.mean(x * x, axis=-1, keepdims=True)
    return x * lax.rsqrt(ms + EPS) * g


def _inproj_body(x_ref, g_ref, w_ref, wkt_ref, u_ref, q_ref, kt_ref, v_ref, r_ref, gl_ref):
    hn = _rms(x_ref[...], g_ref[...]).astype(bf16)

    def proj(lo, hi):
        return jnp.dot(hn, w_ref[:, lo:hi], preferred_element_type=f32).astype(bf16)

    u_ref[...] = proj(0, 512)
    q_ref[...] = proj(512, 768)
    v_ref[...] = proj(768, 1280)
    r_ref[...] = proj(1280, 1792)
    gl_ref[...] = proj(1792, 1920)
    kt_ref[...] = lax.dot_general(wkt_ref[...], hn, (((1,), (1,)), ((), ())),
                                  preferred_element_type=f32).astype(bf16)


def _inproj(x2, g, w_cat, w_kt, bsz, seq):
    n = x2.shape[0]
    tm = TM_PROJ
    per_b = seq // tm
    row = lambda i: (i, 0)
    const = lambda i: (0, 0)
    return pl.pallas_call(
        _inproj_body,
        grid=(n // tm,),
        in_specs=[pl.BlockSpec((tm, D_MODEL), row),
                  pl.BlockSpec((1, D_MODEL), const),
                  pl.BlockSpec(w_cat.shape, const),
                  pl.BlockSpec(w_kt.shape, const)],
        out_specs=[pl.BlockSpec((tm, D_S5), row),
                   pl.BlockSpec((tm, D_GLA_K), row),
                   pl.BlockSpec((None, D_GLA_K, tm), lambda i: (i // per_b, 0, i % per_b)),
                   pl.BlockSpec((tm, D_GLA), row),
                   pl.BlockSpec((tm, D_GLA), row),
                   pl.BlockSpec((tm, LANE), row)],
        out_shape=[SDS((n, D_S5), bf16), SDS((n, D_GLA_K), bf16),
                   SDS((bsz, D_GLA_K, seq), bf16), SDS((n, D_GLA), bf16),
                   SDS((n, D_GLA), bf16), SDS((n, LANE), bf16)],
        compiler_params=_cparams("parallel"),
        name="inproj",
    )(x2, g, w_cat, w_kt)


N_SLAB = N_STATE // LANE


def _s5_prep(lam_re, lam_im, log_dt, b_re, b_im, c_re, c_im):
    lr = jnp.minimum(lam_re.astype(f32), -1e-4)
    li = lam_im.astype(f32)
    dt = jnp.exp(log_dt.astype(f32))[:, None]
    mag = jnp.exp(lr * dt)
    ab_re = mag * jnp.cos(li * dt)
    ab_im = mag * jnp.sin(li * dt)
    nr = ab_re - 1.0
    ni = ab_im
    den = lr * lr + li * li
    f_re = (nr * lr + ni * li) / den
    f_im = (ni * lr - nr * li) / den
    br = b_re.astype(f32)
    bi = b_im.astype(f32)
    bb_re = f_re[..., None] * br - f_im[..., None] * bi
    bb_im = f_re[..., None] * bi + f_im[..., None] * br

    def b_tiles(bb):
        bt = jnp.transpose(bb, (0, 2, 1)).reshape(8, 4, S5_GROUP, S5_STATE)
        gl = jnp.arange(8)[None, :, None]
        gs = jnp.arange(4)[None, None, :]
        nn = jnp.arange(8)[:, None, None]
        sel = (gl == 4 * (nn % 2) + gs).astype(f32)
        t = jnp.einsum('ngs,nscp->ngcsp', sel, bt)
        return t.reshape(8, LANE, 4 * S5_STATE)

    bmat = jnp.concatenate([b_tiles(bb_re), b_tiles(bb_im)], axis=0).astype(bf16)

    def c_tiles(c, sign):
        ct = jnp.transpose(c.astype(f32), (0, 2, 1)).reshape(2, 16, S5_STATE, S5_GROUP)
        eye = jnp.eye(16, dtype=f32)
        t = jnp.einsum('gh,jgpo->jgpho', eye, ct) * sign
        return t.reshape(2, 16 * S5_STATE, 16 * S5_GROUP)

    cmat = jnp.stack([c_tiles(c_re, 1.0), c_tiles(c_im, -1.0)], axis=1).astype(bf16)
    a_re = ab_re.reshape(N_SLAB, 1, LANE)
    a_im = ab_im.reshape(N_SLAB, 1, LANE)
    return bmat, cmat, a_re, a_im


def _s5_body(u_ref, bm_ref, cm_ref, are_ref, aim_ref, d_ref, y_ref, hbuf, hstate):
    tt = u_ref.shape[1]
    rows = SUBLANE * tt
    pitch = tt + S5_PITCH_PAD

    @pl.when(pl.program_id(0) == 0)
    def _():
        hstate[...] = jnp.zeros_like(hstate)

    u2 = u_ref[...].reshape(rows, D_S5)

    def put(slab, val):
        for b in range(SUBLANE):
            hbuf[slab, b * pitch:b * pitch + tt, :] = val[b * tt:(b + 1) * tt, :]

    def get(slab):
        return jnp.concatenate([hbuf[slab, b * pitch:b * pitch + tt, :] for b in range(SUBLANE)], axis=0)

    for n in range(2 * 8):
        ks = LANE * ((n % 8) // 2)
        res = jnp.dot(u2[:, ks:ks + LANE], bm_ref[n], preferred_element_type=f32)
        put(2 * n, res[:, :LANE])
        put(2 * n + 1, res[:, LANE:])

    for c0 in range(0, N_SLAB, S5_SLABS):
        ar = [jnp.broadcast_to(are_ref[c0 + s], (SUBLANE, LANE)) for s in range(S5_SLABS)]
        ai = [jnp.broadcast_to(aim_ref[c0 + s], (SUBLANE, LANE)) for s in range(S5_SLABS)]

        def step(t, carry):
            out = []
            for s in range(S5_SLABS):
                hr, hi = carry[2 * s], carry[2 * s + 1]
                sl = pl.ds(t, SUBLANE, stride=pitch)
                bur = hbuf[c0 + s, sl, :]
                bui = hbuf[N_SLAB + c0 + s, sl, :]
                nr = ar[s] * hr - ai[s] * hi + bur
                ni = ar[s] * hi + ai[s] * hr + bui
                hbuf[c0 + s, sl, :] = nr
                hbuf[N_SLAB + c0 + s, sl, :] = ni
                out += [nr, ni]
            return tuple(out)

        init = []
        for s in range(S5_SLABS):
            init += [hstate[c0 + s], hstate[N_SLAB + c0 + s]]
        fin = lax.fori_loop(0, tt, step, tuple(init), unroll=4)
        for s in range(S5_SLABS):
            hstate[c0 + s] = fin[2 * s]
            hstate[N_SLAB + c0 + s] = fin[2 * s + 1]

    for j in range(2):
        h_re = jnp.concatenate([get(8 * j + s) for s in range(8)], axis=1).astype(bf16)
        h_im = jnp.concatenate([get(N_SLAB + 8 * j + s) for s in range(8)], axis=1).astype(bf16)
        y = jnp.dot(h_re, cm_ref[j, 0], preferred_element_type=f32)
        y = y + jnp.dot(h_im, cm_ref[j, 1], preferred_element_type=f32)
        cs = slice(256 * j, 256 * j + 256)
        y = y + d_ref[:, cs] * u2[:, cs].astype(f32)
        y_ref[:, :, cs] = jax.nn.gelu(y).astype(bf16).reshape(SUBLANE, tt, 256)


def _s5(u3, bmat, cmat, a_re, a_im, d_skip):
    bsz, seq, _ = u3.shape
    assert bsz == SUBLANE
    tt = TT_S5
    blk = lambda i: (0, i, 0)
    return pl.pallas_call(
        _s5_body,
        grid=(seq // tt,),
        in_specs=[pl.BlockSpec((bsz, tt, D_S5), blk),
                  pl.BlockSpec(bmat.shape, lambda i: (0, 0, 0)),
                  pl.BlockSpec(cmat.shape, lambda i: (0, 0, 0, 0)),
                  pl.BlockSpec(a_re.shape, lambda i: (0, 0, 0)),
                  pl.BlockSpec(a_im.shape, lambda i: (0, 0, 0)),
                  pl.BlockSpec((1, D_S5), lambda i: (0, 0))],
        out_specs=pl.BlockSpec((bsz, tt, D_S5), blk),
        out_shape=SDS((bsz, seq, D_S5), bf16),
        scratch_shapes=[pltpu.VMEM((2 * N_SLAB, bsz * (tt + S5_PITCH_PAD), LANE), f32),
                        pltpu.VMEM((2 * N_SLAB, SUBLANE, LANE), f32)],
        compiler_params=_cparams("arbitrary"),
        name="s5_scan",
    )(u3, bmat, cmat, a_re, a_im, d_skip)


def _split2(x):
    hi = x.astype(bf16)
    lo = (x - hi.astype(f32)).astype(bf16)
    return hi, lo


def _log_sigmoid(x):
    return -(jnp.maximum(-x, 0.0) + jnp.log1p(jnp.exp(-jnp.abs(x))))


def _gla_body(q_ref, kt_ref, v_ref, r_ref, g_ref, wa_ref, wat_ref, ba_ref, bat_ref, gn_ref,
              o_ref, s_ref):
    c = GLA_CHUNK
    tg = q_ref.shape[0]

    @pl.when(pl.program_id(1) == 0)
    def _():
        s_ref[...] = jnp.zeros_like(s_ref)

    nc = tg // c
    nh = GLA_HEADS
    ri = lax.broadcasted_iota(i32, (tg, tg), 0)
    ci = lax.broadcasted_iota(i32, (tg, tg), 1)
    same = (ri // c) == (ci // c)
    tril = (same & (ri >= ci)).astype(bf16)
    triu = (same & (ri <= ci)).astype(bf16)
    blk = same.astype(bf16)

    g = g_ref[...]
    la = _log_sigmoid(jnp.dot(g, wa_ref[...], preferred_element_type=f32) + ba_ref[...]) / GLA_TAU
    lat = _log_sigmoid(lax.dot_general(wat_ref[...], g, (((1,), (1,)), ((), ())),
                                       preferred_element_type=f32) + bat_ref[...]) / GLA_TAU
    la_hi, la_lo = _split2(la)
    cum = (jnp.dot(tril, la_hi, preferred_element_type=f32)
           + jnp.dot(tril, la_lo, preferred_element_type=f32))
    lat_hi, lat_lo = _split2(lat)
    cumt = (jnp.dot(lat_hi, triu, preferred_element_type=f32)
            + jnp.dot(lat_lo, triu, preferred_element_type=f32))
    clt = (jnp.dot(lat_hi, blk, preferred_element_type=f32)
           + jnp.dot(lat_lo, blk, preferred_element_type=f32))

    qt = q_ref[...].astype(f32) * jnp.exp(cum)
    ktt = kt_ref[...].astype(f32)
    k_t = (ktt * jnp.exp(-cumt)).astype(bf16)
    k_end = ktt * jnp.exp(clt - cumt)
    v = v_ref[...]

    head_of_lane = lax.broadcasted_iota(i32, (c, D_GLA_K), 1) // GLA_DK
    q_stack = jnp.concatenate(
        [jnp.where(head_of_lane == h, qt[cc * c:(cc + 1) * c, :], 0.0)
         for cc in range(nc) for h in range(nh)], axis=0).astype(bf16)
    scores = jnp.dot(q_stack, k_t, preferred_element_type=f32)
    r_idx = lax.broadcasted_iota(i32, (nc * nh * c, tg), 0)
    c_idx = lax.broadcasted_iota(i32, (nc * nh * c, tg), 1)
    causal = (r_idx // (nh * c) == c_idx // c) & (r_idx % c >= c_idx % c)
    scores = jnp.where(causal, scores, 0.0).astype(bf16)
    o_full = jnp.dot(scores, v, preferred_element_type=f32)

    chunk_of_lane = lax.broadcasted_iota(i32, (D_GLA_K, tg), 1) // c
    k_stack = jnp.concatenate(
        [jnp.where(chunk_of_lane == cc, k_end, 0.0) for cc in range(nc)], axis=0).astype(bf16)
    upd_full = jnp.dot(k_stack, v, preferred_element_type=f32)

    state = s_ref[...]
    o_inter = []
    for cc in range(nc):
        o_inter.append(jnp.dot(q_stack[cc * nh * c:(cc + 1) * nh * c, :], state.astype(bf16),
                               preferred_element_type=f32))
        upd = jnp.concatenate(
            [upd_full[cc * D_GLA_K + h * GLA_DK:cc * D_GLA_K + (h + 1) * GLA_DK,
                      h * GLA_DV:(h + 1) * GLA_DV] for h in range(nh)], axis=0)
        state = jnp.exp(clt[:, cc * c:cc * c + 1]) * state + upd
    s_ref[...] = state

    rows = []
    for cc in range(nc):
        outs = []
        for h in range(nh):
            r0 = (cc * nh + h) * c
            o = o_full[r0:r0 + c, h * GLA_DV:(h + 1) * GLA_DV] + o_inter[cc][h * c:(h + 1) * c, :]
            o = o * lax.rsqrt(jnp.mean(o * o, axis=-1, keepdims=True) + EPS)
            outs.append(o)
        rows.append(jnp.concatenate(outs, axis=1))
    o_cat = jnp.concatenate(rows, axis=0)
    o_ref[...] = (o_cat * gn_ref[...] * jax.nn.silu(r_ref[...].astype(f32))).astype(bf16)


def _gla(q3, kt3, v3, r3, g3, wa, wat, ba, bat, gn):
    bsz, seq, _ = q3.shape
    tg = TG_GLA
    tok = lambda b, i: (b, i, 0)
    c2 = lambda b, i: (0, 0)
    return pl.pallas_call(
        _gla_body,
        grid=(bsz, seq // tg),
        in_specs=[pl.BlockSpec((None, tg, D_GLA_K), tok),
                  pl.BlockSpec((None, D_GLA_K, tg), lambda b, i: (b, 0, i)),
                  pl.BlockSpec((None, tg, D_GLA), tok),
                  pl.BlockSpec((None, tg, D_GLA), tok),
                  pl.BlockSpec((None, tg, LANE), tok),
                  pl.BlockSpec(wa.shape, c2), pl.BlockSpec(wat.shape, c2),
                  pl.BlockSpec(ba.shape, c2), pl.BlockSpec(bat.shape, c2),
                  pl.BlockSpec(gn.shape, c2)],
        out_specs=pl.BlockSpec((None, tg, D_GLA), tok),
        out_shape=SDS((bsz, seq, D_GLA), bf16),
        scratch_shapes=[pltpu.VMEM((D_GLA_K, GLA_DV), f32)],
        compiler_params=_cparams("parallel", "arbitrary"),
        name="gla",
    )(q3, kt3, v3, r3, g3, wa, wat, ba, bat, gn)


def _mix_out(x_ref, ys_ref, yg_ref, wgl_ref, bg_ref, gs_ref, wo_ref):
    y = ys_ref[...]
    z = jnp.dot(y, wgl_ref[...], preferred_element_type=f32) + bg_ref[...]
    yf = y.astype(f32) * jax.nn.sigmoid(z)
    ys = _rms(yf, gs_ref[...]).astype(bf16)
    acc = jnp.dot(ys, wo_ref[0:D_S5, :], preferred_element_type=f32)
    acc = acc + jnp.dot(yg_ref[...], wo_ref[D_S5:, :], preferred_element_type=f32)
    return x_ref[...] + acc


def _mix_specs(tm, w_glu, w_out):
    row = lambda i: (i, 0)
    const = lambda i: (0, 0)
    once = dict(pipeline_mode=pl.Buffered(1))
    return [pl.BlockSpec((tm, D_MODEL), row),
            pl.BlockSpec((tm, D_S5), row),
            pl.BlockSpec((tm, D_GLA), row),
            pl.BlockSpec(w_glu.shape, const, **once),
            pl.BlockSpec((1, D_S5), const),
            pl.BlockSpec((1, D_S5), const),
            pl.BlockSpec(w_out.shape, const, **once)]


def _mix_ffn_body(x_ref, ys_ref, yg_ref, wgl_ref, bg_ref, gs_ref, wo_ref,
                  g_ref, wg_ref, wu_ref, wd_ref, o_ref):
    x = _mix_out(x_ref, ys_ref, yg_ref, wgl_ref, bg_ref, gs_ref, wo_ref)
    hn = _rms(x, g_ref[...]).astype(bf16)
    gate = jnp.dot(hn, wg_ref[...], preferred_element_type=f32)
    up = jnp.dot(hn, wu_ref[...], preferred_element_type=f32)
    act = (jax.nn.silu(gate) * up).astype(bf16)
    o_ref[...] = x + jnp.dot(act, wd_ref[...], preferred_element_type=f32)


def _mix_ffn(x2, ys, yg, w_glu, b_glu, g_s5, w_out, g, wg, wu, wd):
    n = x2.shape[0]
    tm = TM_FFN
    row = lambda i: (i, 0)
    const = lambda i: (0, 0)
    once = dict(pipeline_mode=pl.Buffered(1))
    return pl.pallas_call(
        _mix_ffn_body,
        grid=(n // tm,),
        in_specs=_mix_specs(tm, w_glu, w_out) + [
            pl.BlockSpec((1, D_MODEL), const),
            pl.BlockSpec(wg.shape, const, **once),
            pl.BlockSpec(wu.shape, const, **once),
            pl.BlockSpec(wd.shape, const, **once)],
        out_specs=pl.BlockSpec((tm, D_MODEL), row),
        out_shape=SDS((n, D_MODEL), f32),
        compiler_params=_cparams("parallel"),
        name="mix_ffn",
    )(x2, ys, yg, w_glu, b_glu, g_s5, w_out, g, wg, wu, wd)


def _pack_bf16_pairs(a):
    bits = pltpu.bitcast(a.astype(bf16).astype(f32), u32)
    half = a.shape[1] // 2
    return bits[:, :half] | (bits[:, half:] >> 16)


def _unpack_bf16_pairs(p):
    hi = pltpu.bitcast(p & jnp.uint32(0xFFFF0000), f32).astype(bf16)
    lo = pltpu.bitcast(p << 16, f32).astype(bf16)
    return hi, lo


def _mix_router_body(x_ref, ys_ref, yg_ref, wgl_ref, bg_ref, gs_ref, wo_ref,
                     g_ref, wh_ref, wl_ref, xo_ref, hp_ref, meta_ref, cnt_ref, carry):
    tm = x_ref.shape[0]

    @pl.when(pl.program_id(0) == 0)
    def _():
        carry[...] = jnp.zeros_like(carry)

    x = _mix_out(x_ref, ys_ref, yg_ref, wgl_ref, bg_ref, gs_ref, wo_ref)
    xo_ref[...] = x
    hn = _rms(x, g_ref[...])
    packed = _pack_bf16_pairs(hn)
    hp_ref[0] = packed[:, :SC_ROW]
    hp_ref[1] = packed[:, SC_ROW:]

    h_hi, h_lo = _split2(hn)
    logits = (jnp.dot(h_hi, wh_ref[...], preferred_element_type=f32)
              + jnp.dot(h_lo, wh_ref[...], preferred_element_type=f32)
              + jnp.dot(h_hi, wl_ref[...], preferred_element_type=f32))
    lane = lax.broadcasted_iota(i32, (tm, LANE), 1)
    neg = jnp.float32(-jnp.inf)
    logits = jnp.where(lane < N_EXPERTS, logits, neg)
    m1 = jnp.max(logits, axis=-1, keepdims=True)
    i1 = jnp.min(jnp.where(logits == m1, lane, LANE), axis=-1, keepdims=True)
    l2 = jnp.where(lane == i1, neg, logits)
    m2 = jnp.max(l2, axis=-1, keepdims=True)
    i2 = jnp.min(jnp.where(l2 == m2, lane, LANE), axis=-1, keepdims=True)
    e21 = jnp.exp(m2 - m1)
    w1 = 1.0 / (1.0 + e21)
    w2 = e21 / (1.0 + e21)

    sel1 = lane == i1
    sel2 = lane == i2
    sel = (sel1 | sel2).astype(f32)
    ri = lax.broadcasted_iota(i32, (tm, tm), 0)
    ci = lax.broadcasted_iota(i32, (tm, tm), 1)
    tril = (ri >= ci).astype(bf16)
    incl = jnp.dot(tril, sel.astype(bf16), preferred_element_type=f32)
    rank = incl - sel + carry[0:1, :]
    r1 = jnp.sum(jnp.where(sel1, rank, 0.0), axis=-1, keepdims=True)
    r2 = jnp.sum(jnp.where(sel2, rank, 0.0), axis=-1, keepdims=True)
    new_cnt = carry[0:1, :] + incl[tm - 1:tm, :]
    carry[...] = jnp.broadcast_to(new_cnt, carry.shape)
    cnt_ref[...] = jnp.broadcast_to(new_cnt, cnt_ref.shape)

    meta = jnp.where(lane == 0, i1.astype(f32), 0.0)
    meta = jnp.where(lane == 1, i2.astype(f32), meta)
    meta = jnp.where(lane == 2, w1, meta)
    meta = jnp.where(lane == 3, w2, meta)
    meta = jnp.where(lane == 4, r1, meta)
    meta = jnp.where(lane == 5, r2, meta)
    meta_ref[...] = meta


def _mix_router(x2, ys, yg, w_glu, b_glu, g_s5, w_out, g, w_hi, w_lo):
    n = x2.shape[0]
    tm = TM_ROUTE
    row = lambda i: (i, 0)
    const = lambda i: (0, 0)
    return pl.pallas_call(
        _mix_router_body,
        grid=(n // tm,),
        in_specs=_mix_specs(tm, w_glu, w_out) + [
            pl.BlockSpec((1, D_MODEL), const),
            pl.BlockSpec(w_hi.shape, const),
            pl.BlockSpec(w_lo.shape, const)],
        out_specs=[pl.BlockSpec((tm, D_MODEL), row),
                   pl.BlockSpec((2, tm, SC_ROW), lambda i: (0, i, 0)),
                   pl.BlockSpec((tm, LANE), row),
                   pl.BlockSpec((SUBLANE, LANE), const)],
        out_shape=[SDS((n, D_MODEL), f32), SDS((2, n, SC_ROW), u32), SDS((n, LANE), f32),
                   SDS((SUBLANE, LANE), f32)],
        scratch_shapes=[pltpu.VMEM((SUBLANE, LANE), f32)],
        compiler_params=_cparams("arbitrary"),
        name="mix_router",
    )(x2, ys, yg, w_glu, b_glu, g_s5, w_out, g, w_hi, w_lo)


def _sc_gather(table, idx):
    ni = idx.shape[0]
    mesh = plsc.VectorSubcoreMesh(core_axis_name="core", subcore_axis_name="subcore")
    idx2 = idx.reshape(1, ni)

    @pl.kernel(out_type=SDS((ni, SC_ROW), table.dtype), mesh=mesh)
    def kern(t_hbm, i_hbm, o_hbm):
        def body(i_vmem, o_vmem):
            pltpu.sync_copy(t_hbm.at[i_vmem.at[0]], o_vmem)

        pltpu.emit_pipeline(
            body, grid=(ni // SC_WINDOW,),
            in_specs=[pl.BlockSpec((1, SC_WINDOW), index_map=lambda i: (0, i))],
            out_specs=[pl.BlockSpec((SC_WINDOW, SC_ROW), index_map=lambda i: (i, 0))],
            core_axis_name=("core", "subcore"),
            dimension_semantics=(pltpu.PARALLEL,),
        )(i_hbm, o_hbm)

    return kern(table, idx2)


def _sc_scatter2(x, idx0, idx1, nrows):
    ni = x.shape[0]
    mesh = plsc.VectorSubcoreMesh(core_axis_name="core", subcore_axis_name="subcore")

    @pl.kernel(out_type=SDS((nrows, SC_ROW), x.dtype), mesh=mesh)
    def kern(x_hbm, i0_hbm, i1_hbm, o_hbm):
        def body(x_vmem, i0_vmem, i1_vmem):
            pltpu.sync_copy(x_vmem, o_hbm.at[i0_vmem.at[0]])
            pltpu.sync_copy(x_vmem, o_hbm.at[i1_vmem.at[0]])

        pltpu.emit_pipeline(
            body, grid=(ni // SC_WINDOW,),
            in_specs=[pl.BlockSpec((SC_WINDOW, SC_ROW), index_map=lambda i: (i, 0)),
                      pl.BlockSpec((1, SC_WINDOW), index_map=lambda i: (0, i)),
                      pl.BlockSpec((1, SC_WINDOW), index_map=lambda i: (0, i))],
            out_specs=[],
            core_axis_name=("core", "subcore"),
            dimension_semantics=(pltpu.PARALLEL,),
        )(x_hbm, i0_hbm, i1_hbm)

    return kern(x, idx0.reshape(1, ni), idx1.reshape(1, ni))


def _moe_ffn_body(be_ref, bn_ref, xs_ref, wg_ref, wu_ref, wd_ref, ys_ref, xb, acc):
    i = pl.program_id(0)
    f = pl.program_id(1)
    nf = pl.num_programs(1)
    nvalid = bn_ref[i]

    @pl.when(nvalid > 0)
    def _():
        @pl.when(f == 0)
        def _():
            live = lax.broadcasted_iota(i32, (xs_ref.shape[1], SC_ROW), 0) < nvalid
            hi0, lo0 = _unpack_bf16_pairs(jnp.where(live, xs_ref[0], jnp.uint32(0)))
            hi1, lo1 = _unpack_bf16_pairs(jnp.where(live, xs_ref[1], jnp.uint32(0)))
            xb[...] = jnp.concatenate([hi0, hi1, lo0, lo1], axis=1)
            acc[...] = jnp.zeros_like(acc)

        x = xb[...]
        gate = jnp.dot(x, wg_ref[...].astype(bf16), preferred_element_type=f32)
        up = jnp.dot(x, wu_ref[...].astype(bf16), preferred_element_type=f32)
        act = (jax.nn.silu(gate) * up).astype(bf16)
        acc[...] += jnp.dot(act, wd_ref[...].astype(bf16), preferred_element_type=f32)

        @pl.when(f == nf - 1)
        def _():
            packed = _pack_bf16_pairs(acc[...])
            ys_ref[0] = packed[:, :SC_ROW]
            ys_ref[1] = packed[:, SC_ROW:]

    @pl.when((nvalid == 0) & (f == nf - 1))
    def _():
        ys_ref[...] = jnp.zeros_like(ys_ref)


def _moe_ffn(blk_e, blk_n, xs, wg, wu, wd):
    npad = xs.shape[1]
    tm, tf = TM_MOE, TF_MOE
    nblk = npad // tm
    nf = D_FF_EXPERT // tf

    def ftile(i, f, bn):
        return jnp.where(bn[i] > 0, f, nf - 1)

    grid_spec = pltpu.PrefetchScalarGridSpec(
        num_scalar_prefetch=2,
        grid=(nblk, nf),
        in_specs=[pl.BlockSpec((2, tm, SC_ROW), lambda i, f, be, bn: (0, i, 0)),
                  pl.BlockSpec((None, D_MODEL, tf), lambda i, f, be, bn: (be[i], 0, ftile(i, f, bn))),
                  pl.BlockSpec((None, D_MODEL, tf), lambda i, f, be, bn: (be[i], 0, ftile(i, f, bn))),
                  pl.BlockSpec((None, tf, D_MODEL), lambda i, f, be, bn: (be[i], ftile(i, f, bn), 0))],
        out_specs=pl.BlockSpec((2, tm, SC_ROW), lambda i, f, be, bn: (0, i, 0)),
        scratch_shapes=[pltpu.VMEM((tm, D_MODEL), bf16), pltpu.VMEM((tm, D_MODEL), f32)],
    )
    return pl.pallas_call(
        _moe_ffn_body,
        grid_spec=grid_spec,
        out_shape=SDS((2, npad, SC_ROW), u32),
        compiler_params=_cparams("parallel", "arbitrary"),
        name="moe_ffn",
    )(blk_e, blk_n, xs, wg, wu, wd)


def _combine_body(x_ref, yg_ref, meta_ref, g_ref, o_ref, *, final_norm):
    w1 = meta_ref[:, 2:3]
    w2 = meta_ref[:, 3:4]

    def rows(k):
        hi0, lo0 = _unpack_bf16_pairs(yg_ref[0, k])
        hi1, lo1 = _unpack_bf16_pairs(yg_ref[1, k])
        return jnp.concatenate([hi0, hi1, lo0, lo1], axis=1).astype(f32)

    xo = x_ref[...] + (w1 * rows(0) + w2 * rows(1))
    if final_norm:
        xo = _rms(xo, g_ref[...])
    o_ref[...] = xo


def _combine(x2, yg, meta, g_final, final_norm):
    n = x2.shape[0]
    tm = TM_ROUTE
    row = lambda i: (i, 0)
    return pl.pallas_call(
        functools.partial(_combine_body, final_norm=final_norm),
        grid=(n // tm,),
        in_specs=[pl.BlockSpec((tm, D_MODEL), row),
                  pl.BlockSpec((2, 2, tm, SC_ROW), lambda i: (0, 0, i, 0)),
                  pl.BlockSpec((tm, LANE), row),
                  pl.BlockSpec((1, D_MODEL), lambda i: (0, 0))],
        out_specs=pl.BlockSpec((tm, D_MODEL), row),
        out_shape=SDS((n, D_MODEL), f32),
        compiler_params=_cparams("parallel"),
        name="moe_combine",
    )(x2, yg, meta, g_final)


def _moe_layer(mix_args, g_ffn, w_router, wg, wu, wd, g_final, final_norm):
    n = mix_args[0].shape[0]
    tm = TM_MOE
    npad = 2 * n + N_EXPERTS * tm
    wr = jnp.zeros((D_MODEL, LANE), f32).at[:, :N_EXPERTS].set(w_router.astype(f32))
    wr_hi = wr.astype(bf16)
    wr_lo = (wr - wr_hi.astype(f32)).astype(bf16)
    x2, hp, meta, cnt = _mix_router(*mix_args, g_ffn, wr_hi, wr_lo)

    counts = cnt[0, :N_EXPERTS].astype(i32)
    padded = ((counts + tm - 1) // tm) * tm
    ends = jnp.cumsum(padded)
    offs = ends - padded
    e12 = meta[:, 0:2].astype(i32)
    r12 = meta[:, 4:6].astype(i32)
    pos = offs[e12] + r12
    blk_start = jnp.arange(npad // tm, dtype=i32) * tm
    blk_e = jnp.minimum(jnp.searchsorted(ends, blk_start, side='right').astype(i32), N_EXPERTS - 1)
    blk_n = jnp.clip(offs[blk_e] + counts[blk_e] - blk_start, 0, tm)
    blk_n = jnp.where(blk_start < ends[-1], blk_n, 0)
    last_e = blk_e[jnp.maximum(ends[-1] // tm - 1, 0)]
    blk_e = jnp.where(blk_n > 0, blk_e, last_e)

    pos_t = pos.T
    xs = _sc_scatter2(hp.reshape(2 * n, SC_ROW),
                      jnp.concatenate([pos_t[0], pos_t[0] + npad]),
                      jnp.concatenate([pos_t[1], pos_t[1] + npad]),
                      2 * npad).reshape(2, npad, SC_ROW)
    ys = _moe_ffn(blk_e, blk_n, xs, wg, wu, wd)
    gidx = jnp.concatenate([pos_t.reshape(-1), pos_t.reshape(-1) + npad])
    yg = _sc_gather(ys.reshape(2 * npad, SC_ROW), gidx).reshape(2, 2, n, SC_ROW)
    return _combine(x2, yg, meta, g_final, final_norm)


def _row(v):
    return v.astype(f32).reshape(1, -1)


def kernel(x, norm_mix, w_in, s5_lambda_re, s5_lambda_im, s5_log_dt, s5_b_re, s5_b_im, s5_c_re, s5_c_im, s5_d, s5_w_glu, s5_b_glu, s5_out_norm, gla_w_a2, gla_b_a2, gla_out_norm, w_out, norm_ffn, ffn_w_gate, ffn_w_up, ffn_w_down, moe_w_router, moe_w_gate, moe_w_up, moe_w_down, norm_final):
    bsz, seq, _ = x.shape
    n = bsz * seq
    depth = w_in.shape[0]
    x2 = x.reshape(n, D_MODEL)
    for l in range(depth):
        wi = w_in[l]
        w_u, w_q, w_k, w_v, w_r, w_g = (wi[:, 0:512], wi[:, 512:768], wi[:, 768:1024],
                                        wi[:, 1024:1536], wi[:, 1536:2048], wi[:, 2048:2064])
        w_gp = jnp.zeros((D_MODEL, LANE), f32).at[:, :GLA_GATE_RANK].set(w_g)
        w_cat = jnp.concatenate([w_u, w_q * (GLA_DK ** -0.5), w_v, w_r, w_gp], axis=1).astype(bf16)
        w_kt = w_k.T.astype(bf16)
        u, q, kt, v, r, gl = _inproj(x2, _row(norm_mix[l]), w_cat, w_kt, bsz, seq)

        bmat, cmat, a_re, a_im = _s5_prep(s5_lambda_re[l], s5_lambda_im[l], s5_log_dt[l],
                                          s5_b_re[l], s5_b_im[l], s5_c_re[l], s5_c_im[l])
        ys = _s5(u.reshape(bsz, seq, D_S5), bmat, cmat, a_re, a_im, _row(s5_d[l]))

        wa = jnp.zeros((LANE, D_GLA_K), f32).at[:GLA_GATE_RANK].set(gla_w_a2[l]).astype(bf16)
        yg = _gla(q.reshape(bsz, seq, D_GLA_K), kt, v.reshape(bsz, seq, D_GLA),
                  r.reshape(bsz, seq, D_GLA), gl.reshape(bsz, seq, LANE),
                  wa, wa.T, _row(gla_b_a2[l]), gla_b_a2[l].astype(f32).reshape(-1, 1),
                  _row(gla_out_norm[l]))

        mix_args = (x2, ys.reshape(n, D_S5), yg.reshape(n, D_GLA), s5_w_glu[l].astype(bf16),
                    _row(s5_b_glu[l]), _row(s5_out_norm[l]), w_out[l].astype(bf16))

        i = l // 2
        last = l == depth - 1
        if l % 2 == 0:
            x2 = _mix_ffn(*mix_args, _row(norm_ffn[l]), ffn_w_gate[i].astype(bf16),
                          ffn_w_up[i].astype(bf16), ffn_w_down[i].astype(bf16))
            if last:
                x2 = _final_norm(x2, _row(norm_final))
        else:
            x2 = _moe_layer(mix_args, _row(norm_ffn[l]), moe_w_router[i], moe_w_gate[i],
                            moe_w_up[i], moe_w_down[i], _row(norm_final), last)
    return x2.reshape(bsz, seq, D_MODEL)


def _final_norm_body(x_ref, g_ref, o_ref):
    o_ref[...] = _rms(x_ref[...], g_ref[...])


def _final_norm(x2, g):
    n = x2.shape[0]
    tm = TM_ROUTE
    return pl.pallas_call(
        _final_norm_body,
        grid=(n // tm,),
        in_specs=[pl.BlockSpec((tm, D_MODEL), lambda i: (i, 0)),
                  pl.BlockSpec((1, D_MODEL), lambda i: (0, 0))],
        out_specs=pl.BlockSpec((tm, D_MODEL), lambda i: (i, 0)),
        out_shape=SDS((n, D_MODEL), f32),
        compiler_params=_cparams("parallel"),
        name="final_norm",
    )(x2, g)
```

```python
import functools
import math

import jax
import jax.numpy as jnp
from jax import lax
from jax.experimental import pallas as pl
from jax.experimental.pallas import tpu as pltpu
from jax.experimental.pallas import tpu_sc as plsc

f32 = jnp.float32
bf16 = jnp.bfloat16
u32 = jnp.uint32
i32 = jnp.int32
SDS = jax.ShapeDtypeStruct

D_MODEL = 1024
D_S5 = 512
S5_GROUP = 16
S5_GROUPS = 32
S5_STATE = 64
N_STATE = S5_GROUPS * S5_STATE
D_GLA = 512
GLA_HEADS = 4
GLA_DV = 128
GLA_DK = 64
D_GLA_K = 256
GLA_GATE_RANK = 16
GLA_TAU = 16.0
GLA_CHUNK = 64
D_FF = 2816
N_EXPERTS = 8
D_FF_EXPERT = 3584
EPS = 1e-6

LANE = 128
SUBLANE = 8
VMEM_LIMIT = 56 * 1024 * 1024

TM_PROJ = 512
TT_S5 = 128
S5_SLABS = 4
S5_PITCH_PAD = 8
TG_GLA = 256
TM_FFN = 512
TM_MOE = 1024
TF_MOE = 512
TM_ROUTE = 512
SC_WINDOW = 128
SC_ROW = 256


def _cparams(*sem):
    return pltpu.CompilerParams(dimension_semantics=sem, vmem_limit_bytes=VMEM_LIMIT)


def _rms(x, g):
    ms = jnp.mean(x * x, axis=-1, keepdims=True)
    return x * lax.rsqrt(ms + EPS) * g


def _inproj_body(x_ref, g_ref, w_ref, wkt_ref, u_ref, q_ref, kt_ref, v_ref, r_ref, gl_ref):
    hn = _rms(x_ref[...], g_ref[...]).astype(bf16)

    def proj(lo, hi):
        return jnp.dot(hn, w_ref[:, lo:hi], preferred_element_type=f32).astype(bf16)

    u_ref[...] = proj(0, 512)
    q_ref[...] = proj(512, 768)
    v_ref[...] = proj(768, 1280)
    r_ref[...] = proj(1280, 1792)
    gl_ref[...] = proj(1792, 1920)
    kt_ref[...] = lax.dot_general(wkt_ref[...], hn, (((1,), (1,)), ((), ())),
                                  preferred_element_type=f32).astype(bf16)


def _inproj(x2, g, w_cat, w_kt, bsz, seq):
    n = x2.shape[0]
    tm = TM_PROJ
    per_b = seq // tm
    row = lambda i: (i, 0)
    const = lambda i: (0, 0)
    return pl.pallas_call(
        _inproj_body,
        grid=(n // tm,),
        in_specs=[pl.BlockSpec((tm, D_MODEL), row),
                  pl.BlockSpec((1, D_MODEL), const),
                  pl.BlockSpec(w_cat.shape, const),
                  pl.BlockSpec(w_kt.shape, const)],
        out_specs=[pl.BlockSpec((tm, D_S5), row),
                   pl.BlockSpec((tm, D_GLA_K), row),
                   pl.BlockSpec((None, D_GLA_K, tm), lambda i: (i // per_b, 0, i % per_b)),
                   pl.BlockSpec((tm, D_GLA), row),
                   pl.BlockSpec((tm, D_GLA), row),
                   pl.BlockSpec((tm, LANE), row)],
        out_shape=[SDS((n, D_S5), bf16), SDS((n, D_GLA_K), bf16),
                   SDS((bsz, D_GLA_K, seq), bf16), SDS((n, D_GLA), bf16),
                   SDS((n, D_GLA), bf16), SDS((n, LANE), bf16)],
        compiler_params=_cparams("parallel"),
        name="inproj",
    )(x2, g, w_cat, w_kt)


N_SLAB = N_STATE // LANE


def _s5_prep(lam_re, lam_im, log_dt, b_re, b_im, c_re, c_im):
    lr = jnp.minimum(lam_re.astype(f32), -1e-4)
    li = lam_im.astype(f32)
    dt = jnp.exp(log_dt.astype(f32))[:, None]
    mag = jnp.exp(lr * dt)
    ab_re = mag * jnp.cos(li * dt)
    ab_im = mag * jnp.sin(li * dt)
    nr = ab_re - 1.0
    ni = ab_im
    den = lr * lr + li * li
    f_re = (nr * lr + ni * li) / den
    f_im = (ni * lr - nr * li) / den
    br = b_re.astype(f32)
    bi = b_im.astype(f32)
    bb_re = f_re[..., None] * br - f_im[..., None] * bi
    bb_im = f_re[..., None] * bi + f_im[..., None] * br
    abb_re = ab_re[..., None] * bb_re - ab_im[..., None] * bb_im
    abb_im = ab_re[..., None] * bb_im + ab_im[..., None] * bb_re
    cr = c_re.astype(f32)
    ci = c_im.astype(f32)
    ca_re = cr * ab_re[:, None, :] - ci * ab_im[:, None, :]
    ca_im = cr * ab_im[:, None, :] + ci * ab_re[:, None, :]
    cb = jnp.einsum('gop,gpc->goc', cr, bb_re) - jnp.einsum('gop,gpc->goc', ci, bb_im)

    def b_tiles(bb):
        bt = jnp.transpose(bb, (0, 2, 1)).reshape(8, 4, S5_GROUP, S5_STATE)
        gl = jnp.arange(8)[None, :, None]
        gs = jnp.arange(4)[None, None, :]
        nn = jnp.arange(8)[:, None, None]
        sel = (gl == 4 * (nn % 2) + gs).astype(f32)
        t = jnp.einsum('ngs,nscp->ngcsp', sel, bt)
        return t.reshape(8, LANE, 4 * S5_STATE)

    bmat = jnp.concatenate(
        [jnp.concatenate([b_tiles(abb_re), b_tiles(bb_re)], axis=1),
         jnp.concatenate([b_tiles(abb_im), b_tiles(bb_im)], axis=1)], axis=0).astype(bf16)

    def c_tiles(c, sign):
        ct = jnp.transpose(c, (0, 2, 1)).reshape(2, 16, S5_STATE, S5_GROUP)
        eye = jnp.eye(16, dtype=f32)
        t = jnp.einsum('gh,jgpo->jgpho', eye, ct) * sign
        return t.reshape(2, 16 * S5_STATE, 16 * S5_GROUP)

    cmat = jnp.stack([c_tiles(cr, 1.0), c_tiles(ci, -1.0)], axis=1).astype(bf16)
    camat = jnp.stack([c_tiles(ca_re, 1.0), c_tiles(ca_im, -1.0)], axis=1).astype(bf16)
    cbt = jnp.transpose(cb, (0, 2, 1)).reshape(2, 16, S5_GROUP, S5_GROUP)
    cbmat = jnp.einsum('gh,jgco->jgcho', jnp.eye(16, dtype=f32), cbt).reshape(2, 256, 256).astype(bf16)
    a2_re = (ab_re * ab_re - ab_im * ab_im).reshape(N_SLAB, 1, LANE)
    a2_im = (2.0 * ab_re * ab_im).reshape(N_SLAB, 1, LANE)
    return bmat, cmat, camat, cbmat, a2_re, a2_im


def _s5_body(u_ref, bm_ref, cm_ref, cam_ref, cbm_ref, are_ref, aim_ref, d_ref, y_ref,
             hbuf, hstate, zcarry):
    tt = u_ref.shape[1]
    tp = tt // 2
    rows = SUBLANE * tp
    pitch = tp + S5_PITCH_PAD

    @pl.when(pl.program_id(0) == 0)
    def _():
        hstate[...] = jnp.zeros_like(hstate)
        zcarry[...] = jnp.zeros_like(zcarry)

    w = pltpu.bitcast(u_ref[...].reshape(SUBLANE * tt, D_S5), u32)
    ue_f = pltpu.bitcast(w << 16, f32)
    uo_f = pltpu.bitcast(w & jnp.uint32(0xFFFF0000), f32)
    ue = ue_f.astype(bf16)
    uo = uo_f.astype(bf16)

    def put(slab, val):
        for b in range(SUBLANE):
            hbuf[slab, b * pitch:b * pitch + tp, :] = val[b * tp:(b + 1) * tp, :]

    def get(slab):
        return jnp.concatenate([hbuf[slab, b * pitch:b * pitch + tp, :] for b in range(SUBLANE)], axis=0)

    for n in range(2 * 8):
        ks = LANE * ((n % 8) // 2)
        lhs = jnp.concatenate([ue[:, ks:ks + LANE], uo[:, ks:ks + LANE]], axis=1)
        res = jnp.dot(lhs, bm_ref[n], preferred_element_type=f32)
        put(2 * n, res[:, :LANE])
        put(2 * n + 1, res[:, LANE:])

    for c0 in range(0, N_SLAB, S5_SLABS):
        ar = [jnp.broadcast_to(are_ref[c0 + s], (SUBLANE, LANE)) for s in range(S5_SLABS)]
        ai = [jnp.broadcast_to(aim_ref[c0 + s], (SUBLANE, LANE)) for s in range(S5_SLABS)]

        def step(t, carry):
            out = []
            for s in range(S5_SLABS):
                hr, hi = carry[2 * s], carry[2 * s + 1]
                sl = pl.ds(t, SUBLANE, stride=pitch)
                bur = hbuf[c0 + s, sl, :]
                bui = hbuf[N_SLAB + c0 + s, sl, :]
                nr = ar[s] * hr - ai[s] * hi + bur
                ni = ar[s] * hi + ai[s] * hr + bui
                hbuf[c0 + s, sl, :] = nr
                hbuf[N_SLAB + c0 + s, sl, :] = ni
                out += [nr, ni]
            return tuple(out)

        init = []
        for s in range(S5_SLABS):
            init += [hstate[c0 + s], hstate[N_SLAB + c0 + s]]
        fin = lax.fori_loop(0, tp, step, tuple(init), unroll=4)
        for s in range(S5_SLABS):
            hstate[c0 + s] = fin[2 * s]
            hstate[N_SLAB + c0 + s] = fin[2 * s + 1]

    first = lax.broadcasted_iota(i32, (rows, 256), 0) % tp == 0
    for j in range(2):
        h_re = jnp.concatenate([get(8 * j + s) for s in range(8)], axis=1).astype(bf16)
        h_im = jnp.concatenate([get(N_SLAB + 8 * j + s) for s in range(8)], axis=1).astype(bf16)
        cs = slice(256 * j, 256 * j + 256)
        dj = d_ref[:, cs]
        yo = jnp.dot(h_re, cm_ref[j, 0], preferred_element_type=f32)
        yo = yo + jnp.dot(h_im, cm_ref[j, 1], preferred_element_type=f32)
        yo = yo + dj * uo_f[:, cs]
        z = jnp.dot(h_re, cam_ref[j, 0], preferred_element_type=f32)
        z = z + jnp.dot(h_im, cam_ref[j, 1], preferred_element_type=f32)
        prev = jnp.concatenate([jnp.broadcast_to(zcarry[b:b + 1, cs], (tp, 256))
                                for b in range(SUBLANE)], axis=0)
        zs = jnp.where(first, prev, pltpu.roll(z, 1, 0))
        zcarry[:, cs] = jnp.concatenate([z[(b + 1) * tp - 1:(b + 1) * tp, :] for b in range(SUBLANE)], axis=0)
        ye = zs + jnp.dot(ue[:, cs], cbm_ref[j], preferred_element_type=f32) + dj * ue_f[:, cs]
        ge = pltpu.bitcast(jax.nn.gelu(ye).astype(bf16).astype(f32), u32)
        go = pltpu.bitcast(jax.nn.gelu(yo).astype(bf16).astype(f32), u32)
        packed = (ge >> 16) | go
        y_ref[:, :, cs] = pltpu.bitcast(packed, bf16).reshape(SUBLANE, tt, 256)


def _s5(u3, bmat, cmat, camat, cbmat, a2_re, a2_im, d_skip):
    bsz, seq, _ = u3.shape
    assert bsz == SUBLANE
    tt = TT_S5
    blk = lambda i: (0, i, 0)
    c3 = lambda i: (0, 0, 0)
    c4 = lambda i: (0, 0, 0, 0)
    return pl.pallas_call(
        _s5_body,
        grid=(seq // tt,),
        in_specs=[pl.BlockSpec((bsz, tt, D_S5), blk),
                  pl.BlockSpec(bmat.shape, c3),
                  pl.BlockSpec(cmat.shape, c4),
                  pl.BlockSpec(camat.shape, c4),
                  pl.BlockSpec(cbmat.shape, c3),
                  pl.BlockSpec(a2_re.shape, c3),
                  pl.BlockSpec(a2_im.shape, c3),
                  pl.BlockSpec((1, D_S5), lambda i: (0, 0))],
        out_specs=pl.BlockSpec((bsz, tt, D_S5), blk),
        out_shape=SDS((bsz, seq, D_S5), bf16),
        scratch_shapes=[pltpu.VMEM((2 * N_SLAB, bsz * (tt // 2 + S5_PITCH_PAD), LANE), f32),
                        pltpu.VMEM((2 * N_SLAB, SUBLANE, LANE), f32),
                        pltpu.VMEM((SUBLANE, D_S5), f32)],
        compiler_params=_cparams("arbitrary"),
        name="s5_scan",
    )(u3, bmat, cmat, camat, cbmat, a2_re, a2_im, d_skip)


def _split2(x):
    hi = x.astype(bf16)
    lo = (x - hi.astype(f32)).astype(bf16)
    return hi, lo


def _log_sigmoid(x):
    return -(jnp.maximum(-x, 0.0) + jnp.log1p(jnp.exp(-jnp.abs(x))))


def _gla_body(q_ref, kt_ref, v_ref, r_ref, g_ref, wa_ref, wat_ref, ba_ref, bat_ref, gn_ref,
              o_ref, s_ref):
    c = GLA_CHUNK
    tg = q_ref.shape[0]

    @pl.when(pl.program_id(1) == 0)
    def _():
        s_ref[...] = jnp.zeros_like(s_ref)

    nc = tg // c
    nh = GLA_HEADS
    ri = lax.broadcasted_iota(i32, (tg, tg), 0)
    ci = lax.broadcasted_iota(i32, (tg, tg), 1)
    same = (ri // c) == (ci // c)
    tril = (same & (ri >= ci)).astype(bf16)
    triu = (same & (ri <= ci)).astype(bf16)
    blk = same.astype(bf16)

    g = g_ref[...]
    la = _log_sigmoid(jnp.dot(g, wa_ref[...], preferred_element_type=f32) + ba_ref[...]) / GLA_TAU
    lat = _log_sigmoid(lax.dot_general(wat_ref[...], g, (((1,), (1,)), ((), ())),
                                       preferred_element_type=f32) + bat_ref[...]) / GLA_TAU
    la_hi, la_lo = _split2(la)
    cum = (jnp.dot(tril, la_hi, preferred_element_type=f32)
           + jnp.dot(tril, la_lo, preferred_element_type=f32))
    lat_hi, lat_lo = _split2(lat)
    cumt = (jnp.dot(lat_hi, triu, preferred_element_type=f32)
            + jnp.dot(lat_lo, triu, preferred_element_type=f32))
    clt = (jnp.dot(lat_hi, blk, preferred_element_type=f32)
           + jnp.dot(lat_lo, blk, preferred_element_type=f32))

    qt = q_ref[...].astype(f32) * jnp.exp(cum)
    ktt = kt_ref[...].astype(f32)
    k_t = (ktt * jnp.exp(-cumt)).astype(bf16)
    k_end = ktt * jnp.exp(clt - cumt)
    v = v_ref[...]

    head_of_lane = lax.broadcasted_iota(i32, (c, D_GLA_K), 1) // GLA_DK
    q_stack = jnp.concatenate(
        [jnp.where(head_of_lane == h, qt[cc * c:(cc + 1) * c, :], 0.0)
         for cc in range(nc) for h in range(nh)], axis=0).astype(bf16)
    scores = jnp.dot(q_stack, k_t, preferred_element_type=f32)
    r_idx = lax.broadcasted_iota(i32, (nc * nh * c, tg), 0)
    c_idx = lax.broadcasted_iota(i32, (nc * nh * c, tg), 1)
    causal = (r_idx // (nh * c) == c_idx // c) & (r_idx % c >= c_idx % c)
    scores = jnp.where(causal, scores, 0.0).astype(bf16)
    o_full = jnp.dot(scores, v, preferred_element_type=f32)

    chunk_of_lane = lax.broadcasted_iota(i32, (D_GLA_K, tg), 1) // c
    k_stack = jnp.concatenate(
        [jnp.where(chunk_of_lane == cc, k_end, 0.0) for cc in range(nc)], axis=0).astype(bf16)
    upd_full = jnp.dot(k_stack, v, preferred_element_type=f32)

    state = s_ref[...]
    o_inter = []
    for cc in range(nc):
        o_inter.append(jnp.dot(q_stack[cc * nh * c:(cc + 1) * nh * c, :], state.astype(bf16),
                               preferred_element_type=f32))
        upd = jnp.concatenate(
            [upd_full[cc * D_GLA_K + h * GLA_DK:cc * D_GLA_K + (h + 1) * GLA_DK,
                      h * GLA_DV:(h + 1) * GLA_DV] for h in range(nh)], axis=0)
        state = jnp.exp(clt[:, cc * c:cc * c + 1]) * state + upd
    s_ref[...] = state

    rows = []
    for cc in range(nc):
        outs = []
        for h in range(nh):
            r0 = (cc * nh + h) * c
            o = o_full[r0:r0 + c, h * GLA_DV:(h + 1) * GLA_DV] + o_inter[cc][h * c:(h + 1) * c, :]
            o = o * lax.rsqrt(jnp.mean(o * o, axis=-1, keepdims=True) + EPS)
            outs.append(o)
        rows.append(jnp.concatenate(outs, axis=1))
    o_cat = jnp.concatenate(rows, axis=0)
    o_ref[...] = (o_cat * gn_ref[...] * jax.nn.silu(r_ref[...].astype(f32))).astype(bf16)


def _gla(q3, kt3, v3, r3, g3, wa, wat, ba, bat, gn):
    bsz, seq, _ = q3.shape
    tg = TG_GLA
    tok = lambda b, i: (b, i, 0)
    c2 = lambda b, i: (0, 0)
    return pl.pallas_call(
        _gla_body,
        grid=(bsz, seq // tg),
        in_specs=[pl.BlockSpec((None, tg, D_GLA_K), tok),
                  pl.BlockSpec((None, D_GLA_K, tg), lambda b, i: (b, 0, i)),
                  pl.BlockSpec((None, tg, D_GLA), tok),
                  pl.BlockSpec((None, tg, D_GLA), tok),
                  pl.BlockSpec((None, tg, LANE), tok),
                  pl.BlockSpec(wa.shape, c2), pl.BlockSpec(wat.shape, c2),
                  pl.BlockSpec(ba.shape, c2), pl.BlockSpec(bat.shape, c2),
                  pl.BlockSpec(gn.shape, c2)],
        out_specs=pl.BlockSpec((None, tg, D_GLA), tok),
        out_shape=SDS((bsz, seq, D_GLA), bf16),
        scratch_shapes=[pltpu.VMEM((D_GLA_K, GLA_DV), f32)],
        compiler_params=_cparams("parallel", "arbitrary"),
        name="gla",
    )(q3, kt3, v3, r3, g3, wa, wat, ba, bat, gn)


def _mix_out(x_ref, ys_ref, yg_ref, wgl_ref, bg_ref, gs_ref, wo_ref):
    y = ys_ref[...]
    z = jnp.dot(y, wgl_ref[...], preferred_element_type=f32) + bg_ref[...]
    yf = y.astype(f32) * jax.nn.sigmoid(z)
    ys = _rms(yf, gs_ref[...]).astype(bf16)
    acc = jnp.dot(ys, wo_ref[0:D_S5, :], preferred_element_type=f32)
    acc = acc + jnp.dot(yg_ref[...], wo_ref[D_S5:, :], preferred_element_type=f32)
    return x_ref[...] + acc


def _mix_specs(tm, w_glu, w_out):
    row = lambda i: (i, 0)
    const = lambda i: (0, 0)
    once = dict(pipeline_mode=pl.Buffered(1))
    return [pl.BlockSpec((tm, D_MODEL), row),
            pl.BlockSpec((tm, D_S5), row),
            pl.BlockSpec((tm, D_GLA), row),
            pl.BlockSpec(w_glu.shape, const, **once),
            pl.BlockSpec((1, D_S5), const),
            pl.BlockSpec((1, D_S5), const),
            pl.BlockSpec(w_out.shape, const, **once)]


def _mix_ffn_body(x_ref, ys_ref, yg_ref, wgl_ref, bg_ref, gs_ref, wo_ref,
                  g_ref, wg_ref, wu_ref, wd_ref, o_ref):
    x = _mix_out(x_ref, ys_ref, yg_ref, wgl_ref, bg_ref, gs_ref, wo_ref)
    hn = _rms(x, g_ref[...]).astype(bf16)
    gate = jnp.dot(hn, wg_ref[...], preferred_element_type=f32)
    up = jnp.dot(hn, wu_ref[...], preferred_element_type=f32)
    act = (jax.nn.silu(gate) * up).astype(bf16)
    o_ref[...] = x + jnp.dot(act, wd_ref[...], preferred_element_type=f32)


def _mix_ffn(x2, ys, yg, w_glu, b_glu, g_s5, w_out, g, wg, wu, wd):
    n = x2.shape[0]
    tm = TM_FFN
    row = lambda i: (i, 0)
    const = lambda i: (0, 0)
    once = dict(pipeline_mode=pl.Buffered(1))
    return pl.pallas_call(
        _mix_ffn_body,
        grid=(n // tm,),
        in_specs=_mix_specs(tm, w_glu, w_out) + [
            pl.BlockSpec((1, D_MODEL), const),
            pl.BlockSpec(wg.shape, const, **once),
            pl.BlockSpec(wu.shape, const, **once),
            pl.BlockSpec(wd.shape, const, **once)],
        out_specs=pl.BlockSpec((tm, D_MODEL), row),
        out_shape=SDS((n, D_MODEL), f32),
        compiler_params=_cparams("parallel"),
        name="mix_ffn",
    )(x2, ys, yg, w_glu, b_glu, g_s5, w_out, g, wg, wu, wd)


def _pack_bf16_pairs(a):
    bits = pltpu.bitcast(a.astype(bf16).astype(f32), u32)
    half = a.shape[1] // 2
    return bits[:, :half] | (bits[:, half:] >> 16)


def _unpack_bf16_pairs(p):
    hi = pltpu.bitcast(p & jnp.uint32(0xFFFF0000), f32).astype(bf16)
    lo = pltpu.bitcast(p << 16, f32).astype(bf16)
    return hi, lo


def _mix_router_body(x_ref, ys_ref, yg_ref, wgl_ref, bg_ref, gs_ref, wo_ref,
                     g_ref, wh_ref, wl_ref, xo_ref, hp_ref, meta_ref, cnt_ref, carry):
    tm = x_ref.shape[0]

    @pl.when(pl.program_id(0) == 0)
    def _():
        carry[...] = jnp.zeros_like(carry)

    x = _mix_out(x_ref, ys_ref, yg_ref, wgl_ref, bg_ref, gs_ref, wo_ref)
    xo_ref[...] = x
    hn = _rms(x, g_ref[...])
    packed = _pack_bf16_pairs(hn)
    hp_ref[0] = packed[:, :SC_ROW]
    hp_ref[1] = packed[:, SC_ROW:]

    h_hi, h_lo = _split2(hn)
    logits = (jnp.dot(h_hi, wh_ref[...], preferred_element_type=f32)
              + jnp.dot(h_lo, wh_ref[...], preferred_element_type=f32)
              + jnp.dot(h_hi, wl_ref[...], preferred_element_type=f32))
    lane = lax.broadcasted_iota(i32, (tm, LANE), 1)
    neg = jnp.float32(-jnp.inf)
    logits = jnp.where(lane < N_EXPERTS, logits, neg)
    m1 = jnp.max(logits, axis=-1, keepdims=True)
    i1 = jnp.min(jnp.where(logits == m1, lane, LANE), axis=-1, keepdims=True)
    l2 = jnp.where(lane == i1, neg, logits)
    m2 = jnp.max(l2, axis=-1, keepdims=True)
    i2 = jnp.min(jnp.where(l2 == m2, lane, LANE), axis=-1, keepdims=True)
    e21 = jnp.exp(m2 - m1)
    w1 = 1.0 / (1.0 + e21)
    w2 = e21 / (1.0 + e21)

    sel1 = lane == i1
    sel2 = lane == i2
    sel = (sel1 | sel2).astype(f32)
    ri = lax.broadcasted_iota(i32, (tm, tm), 0)
    ci = lax.broadcasted_iota(i32, (tm, tm), 1)
    tril = (ri >= ci).astype(bf16)
    incl = jnp.dot(tril, sel.astype(bf16), preferred_element_type=f32)
    rank = incl - sel + carry[0:1, :]
    r1 = jnp.sum(jnp.where(sel1, rank, 0.0), axis=-1, keepdims=True)
    r2 = jnp.sum(jnp.where(sel2, rank, 0.0), axis=-1, keepdims=True)
    new_cnt = carry[0:1, :] + incl[tm - 1:tm, :]
    carry[...] = jnp.broadcast_to(new_cnt, carry.shape)
    cnt_ref[...] = jnp.broadcast_to(new_cnt, cnt_ref.shape)

    meta = jnp.where(lane == 0, i1.astype(f32), 0.0)
    meta = jnp.where(lane == 1, i2.astype(f32), meta)
    meta = jnp.where(lane == 2, w1, meta)
    meta = jnp.where(lane == 3, w2, meta)
    meta = jnp.where(lane == 4, r1, meta)
    meta = jnp.where(lane == 5, r2, meta)
    meta_ref[...] = meta


def _mix_router(x2, ys, yg, w_glu, b_glu, g_s5, w_out, g, w_hi, w_lo):
    n = x2.shape[0]
    tm = TM_ROUTE
    row = lambda i: (i, 0)
    const = lambda i: (0, 0)
    return pl.pallas_call(
        _mix_router_body,
        grid=(n // tm,),
        in_specs=_mix_specs(tm, w_glu, w_out) + [
            pl.BlockSpec((1, D_MODEL), const),
            pl.BlockSpec(w_hi.shape, const),
            pl.BlockSpec(w_lo.shape, const)],
        out_specs=[pl.BlockSpec((tm, D_MODEL), row),
                   pl.BlockSpec((2, tm, SC_ROW), lambda i: (0, i, 0)),
                   pl.BlockSpec((tm, LANE), row),
                   pl.BlockSpec((SUBLANE, LANE), const)],
        out_shape=[SDS((n, D_MODEL), f32), SDS((2, n, SC_ROW), u32), SDS((n, LANE), f32),
                   SDS((SUBLANE, LANE), f32)],
        scratch_shapes=[pltpu.VMEM((SUBLANE, LANE), f32)],
        compiler_params=_cparams("arbitrary"),
        name="mix_router",
    )(x2, ys, yg, w_glu, b_glu, g_s5, w_out, g, w_hi, w_lo)


def _sc_gather(table, idx):
    ni = idx.shape[0]
    mesh = plsc.VectorSubcoreMesh(core_axis_name="core", subcore_axis_name="subcore")
    idx2 = idx.reshape(1, ni)

    @pl.kernel(out_type=SDS((ni, SC_ROW), table.dtype), mesh=mesh)
    def kern(t_hbm, i_hbm, o_hbm):
        def body(i_vmem, o_vmem):
            pltpu.sync_copy(t_hbm.at[i_vmem.at[0]], o_vmem)

        pltpu.emit_pipeline(
            body, grid=(ni // SC_WINDOW,),
            in_specs=[pl.BlockSpec((1, SC_WINDOW), index_map=lambda i: (0, i))],
            out_specs=[pl.BlockSpec((SC_WINDOW, SC_ROW), index_map=lambda i: (i, 0))],
            core_axis_name=("core", "subcore"),
            dimension_semantics=(pltpu.PARALLEL,),
        )(i_hbm, o_hbm)

    return kern(table, idx2)


def _sc_scatter2(x, idx0, idx1, nrows):
    ni = x.shape[0]
    mesh = plsc.VectorSubcoreMesh(core_axis_name="core", subcore_axis_name="subcore")

    @pl.kernel(out_type=SDS((nrows, SC_ROW), x.dtype), mesh=mesh)
    def kern(x_hbm, i0_hbm, i1_hbm, o_hbm):
        def body(x_vmem, i0_vmem, i1_vmem):
            pltpu.sync_copy(x_vmem, o_hbm.at[i0_vmem.at[0]])
            pltpu.sync_copy(x_vmem, o_hbm.at[i1_vmem.at[0]])

        pltpu.emit_pipeline(
            body, grid=(ni // SC_WINDOW,),
            in_specs=[pl.BlockSpec((SC_WINDOW, SC_ROW), index_map=lambda i: (i, 0)),
                      pl.BlockSpec((1, SC_WINDOW), index_map=lambda i: (0, i)),
                      pl.BlockSpec((1, SC_WINDOW), index_map=lambda i: (0, i))],
            out_specs=[],
            core_axis_name=("core", "subcore"),
            dimension_semantics=(pltpu.PARALLEL,),
        )(x_hbm, i0_hbm, i1_hbm)

    return kern(x, idx0.reshape(1, ni), idx1.reshape(1, ni))


def _moe_ffn_body(be_ref, bn_ref, xs_ref, wg_ref, wu_ref, wd_ref, ys_ref, xb, acc):
    i = pl.program_id(0)
    f = pl.program_id(1)
    nf = pl.num_programs(1)
    nvalid = bn_ref[i]

    @pl.when(nvalid > 0)
    def _():
        @pl.when(f == 0)
        def _():
            live = lax.broadcasted_iota(i32, (xs_ref.shape[1], SC_ROW), 0) < nvalid
            hi0, lo0 = _unpack_bf16_pairs(jnp.where(live, xs_ref[0], jnp.uint32(0)))
            hi1, lo1 = _unpack_bf16_pairs(jnp.where(live, xs_ref[1], jnp.uint32(0)))
            xb[...] = jnp.concatenate([hi0, hi1, lo0, lo1], axis=1)
            acc[...] = jnp.zeros_like(acc)

        x = xb[...]
        gate = jnp.dot(x, wg_ref[...].astype(bf16), preferred_element_type=f32)
        up = jnp.dot(x, wu_ref[...].astype(bf16), preferred_element_type=f32)
        act = (jax.nn.silu(gate) * up).astype(bf16)
        acc[...] += jnp.dot(act, wd_ref[...].astype(bf16), preferred_element_type=f32)

        @pl.when(f == nf - 1)
        def _():
            packed = _pack_bf16_pairs(acc[...])
            ys_ref[0] = packed[:, :SC_ROW]
            ys_ref[1] = packed[:, SC_ROW:]

    @pl.when((nvalid == 0) & (f == nf - 1))
    def _():
        ys_ref[...] = jnp.zeros_like(ys_ref)


def _moe_ffn(blk_e, blk_n, xs, wg, wu, wd):
    npad = xs.shape[1]
    tm, tf = TM_MOE, TF_MOE
    nblk = npad // tm
    nf = D_FF_EXPERT // tf

    def ftile(i, f, bn):
        return jnp.where(bn[i] > 0, f, nf - 1)

    grid_spec = pltpu.PrefetchScalarGridSpec(
        num_scalar_prefetch=2,
        grid=(nblk, nf),
        in_specs=[pl.BlockSpec((2, tm, SC_ROW), lambda i, f, be, bn: (0, i, 0)),
                  pl.BlockSpec((None, D_MODEL, tf), lambda i, f, be, bn: (be[i], 0, ftile(i, f, bn))),
                  pl.BlockSpec((None, D_MODEL, tf), lambda i, f, be, bn: (be[i], 0, ftile(i, f, bn))),
                  pl.BlockSpec((None, tf, D_MODEL), lambda i, f, be, bn: (be[i], ftile(i, f, bn), 0))],
        out_specs=pl.BlockSpec((2, tm, SC_ROW), lambda i, f, be, bn: (0, i, 0)),
        scratch_shapes=[pltpu.VMEM((tm, D_MODEL), bf16), pltpu.VMEM((tm, D_MODEL), f32)],
    )
    return pl.pallas_call(
        _moe_ffn_body,
        grid_spec=grid_spec,
        out_shape=SDS((2, npad, SC_ROW), u32),
        compiler_params=_cparams("parallel", "arbitrary"),
        name="moe_ffn",
    )(blk_e, blk_n, xs, wg, wu, wd)


def _combine_body(x_ref, yg_ref, meta_ref, g_ref, o_ref, *, final_norm):
    w1 = meta_ref[:, 2:3]
    w2 = meta_ref[:, 3:4]

    def rows(k):
        hi0, lo0 = _unpack_bf16_pairs(yg_ref[0, k])
        hi1, lo1 = _unpack_bf16_pairs(yg_ref[1, k])
        return jnp.concatenate([hi0, hi1, lo0, lo1], axis=1).astype(f32)

    xo = x_ref[...] + (w1 * rows(0) + w2 * rows(1))
    if final_norm:
        xo = _rms(xo, g_ref[...])
    o_ref[...] = xo


def _combine(x2, yg, meta, g_final, final_norm):
    n = x2.shape[0]
    tm = TM_ROUTE
    row = lambda i: (i, 0)
    return pl.pallas_call(
        functools.partial(_combine_body, final_norm=final_norm),
        grid=(n // tm,),
        in_specs=[pl.BlockSpec((tm, D_MODEL), row),
                  pl.BlockSpec((2, 2, tm, SC_ROW), lambda i: (0, 0, i, 0)),
                  pl.BlockSpec((tm, LANE), row),
                  pl.BlockSpec((1, D_MODEL), lambda i: (0, 0))],
        out_specs=pl.BlockSpec((tm, D_MODEL), row),
        out_shape=SDS((n, D_MODEL), f32),
        compiler_params=_cparams("parallel"),
        name="moe_combine",
    )(x2, yg, meta, g_final)


def _moe_layer(mix_args, g_ffn, w_router, wg, wu, wd, g_final, final_norm):
    n = mix_args[0].shape[0]
    tm = TM_MOE
    npad = 2 * n + N_EXPERTS * tm
    wr = jnp.zeros((D_MODEL, LANE), f32).at[:, :N_EXPERTS].set(w_router.astype(f32))
    wr_hi = wr.astype(bf16)
    wr_lo = (wr - wr_hi.astype(f32)).astype(bf16)
    x2, hp, meta, cnt = _mix_router(*mix_args, g_ffn, wr_hi, wr_lo)

    counts = cnt[0, :N_EXPERTS].astype(i32)
    padded = ((counts + tm - 1) // tm) * tm
    ends = jnp.cumsum(padded)
    offs = ends - padded
    ids = jnp.arange(N_EXPERTS, dtype=i32)
    e12 = meta[:, 0:2].astype(i32).T
    r12 = meta[:, 4:6].astype(i32).T
    pos_t = r12 + jnp.sum(jnp.where(e12[..., None] == ids, offs, 0), axis=-1)
    blk_start = jnp.arange(npad // tm, dtype=i32) * tm
    blk_e = jnp.minimum(jnp.sum((blk_start[:, None] >= ends[None, :]).astype(i32), axis=1), N_EXPERTS - 1)
    own = blk_e[:, None] == ids
    blk_end = jnp.sum(jnp.where(own, offs + counts, 0), axis=1)
    blk_n = jnp.where(blk_start < ends[-1], jnp.clip(blk_end - blk_start, 0, tm), 0)
    last_e = jnp.max(jnp.where(blk_n > 0, blk_e, 0))
    blk_e = jnp.where(blk_n > 0, blk_e, last_e)

    xs = _sc_scatter2(hp.reshape(2 * n, SC_ROW),
                      jnp.concatenate([pos_t[0], pos_t[0] + npad]),
                      jnp.concatenate([pos_t[1], pos_t[1] + npad]),
                      2 * npad).reshape(2, npad, SC_ROW)
    ys = _moe_ffn(blk_e, blk_n, xs, wg, wu, wd)
    gidx = jnp.concatenate([pos_t.reshape(-1), pos_t.reshape(-1) + npad])
    yg = _sc_gather(ys.reshape(2 * npad, SC_ROW), gidx).reshape(2, 2, n, SC_ROW)
    return _combine(x2, yg, meta, g_final, final_norm)


def _row(v):
    return v.astype(f32).reshape(1, -1)


def kernel(x, norm_mix, w_in, s5_lambda_re, s5_lambda_im, s5_log_dt, s5_b_re, s5_b_im, s5_c_re, s5_c_im, s5_d, s5_w_glu, s5_b_glu, s5_out_norm, gla_w_a2, gla_b_a2, gla_out_norm, w_out, norm_ffn, ffn_w_gate, ffn_w_up, ffn_w_down, moe_w_router, moe_w_gate, moe_w_up, moe_w_down, norm_final):
    bsz, seq, _ = x.shape
    n = bsz * seq
    depth = w_in.shape[0]
    x2 = x.reshape(n, D_MODEL)
    for l in range(depth):
        wi = w_in[l]
        w_u, w_q, w_k, w_v, w_r, w_g = (wi[:, 0:512], wi[:, 512:768], wi[:, 768:1024],
                                        wi[:, 1024:1536], wi[:, 1536:2048], wi[:, 2048:2064])
        w_gp = jnp.zeros((D_MODEL, LANE), f32).at[:, :GLA_GATE_RANK].set(w_g)
        w_cat = jnp.concatenate([w_u, w_q * (GLA_DK ** -0.5), w_v, w_r, w_gp], axis=1).astype(bf16)
        w_kt = w_k.T.astype(bf16)
        u, q, kt, v, r, gl = _inproj(x2, _row(norm_mix[l]), w_cat, w_kt, bsz, seq)

        s5_mats = _s5_prep(s5_lambda_re[l], s5_lambda_im[l], s5_log_dt[l],
                           s5_b_re[l], s5_b_im[l], s5_c_re[l], s5_c_im[l])
        ys = _s5(u.reshape(bsz, seq, D_S5), *s5_mats, _row(s5_d[l]))

        wa = jnp.zeros((LANE, D_GLA_K), f32).at[:GLA_GATE_RANK].set(gla_w_a2[l]).astype(bf16)
        yg = _gla(q.reshape(bsz, seq, D_GLA_K), kt, v.reshape(bsz, seq, D_GLA),
                  r.reshape(bsz, seq, D_GLA), gl.reshape(bsz, seq, LANE),
                  wa, wa.T, _row(gla_b_a2[l]), gla_b_a2[l].astype(f32).reshape(-1, 1),
                  _row(gla_out_norm[l]))

        mix_args = (x2, ys.reshape(n, D_S5), yg.reshape(n, D_GLA), s5_w_glu[l].astype(bf16),
                    _row(s5_b_glu[l]), _row(s5_out_norm[l]), w_out[l].astype(bf16))

        i = l // 2
        last = l == depth - 1
        if l % 2 == 0:
            x2 = _mix_ffn(*mix_args, _row(norm_ffn[l]), ffn_w_gate[i].astype(bf16),
                          ffn_w_up[i].astype(bf16), ffn_w_down[i].astype(bf16))
            if last:
                x2 = _final_norm(x2, _row(norm_final))
        else:
            x2 = _moe_layer(mix_args, _row(norm_ffn[l]), moe_w_router[i], moe_w_gate[i],
                            moe_w_up[i], moe_w_down[i], _row(norm_final), last)
    return x2.reshape(bsz, seq, D_MODEL)


def _final_norm_body(x_ref, g_ref, o_ref):
    o_ref[...] = _rms(x_ref[...], g_ref[...])


def _final_norm(x2, g):
    n = x2.shape[0]
    tm = TM_ROUTE
    return pl.pallas_call(
        _final_norm_body,
        grid=(n // tm,),
        in_specs=[pl.BlockSpec((tm, D_MODEL), lambda i: (i, 0)),
                  pl.BlockSpec((1, D_MODEL), lambda i: (0, 0))],
        out_specs=pl.BlockSpec((tm, D_MODEL), lambda i: (i, 0)),
        out_shape=SDS((n, D_MODEL), f32),
        compiler_params=_cparams("parallel"),
        name="final_norm",
    )(x2, g)
```

```python
import functools
import math

import jax
import jax.numpy as jnp
from jax import lax
from jax.experimental import pallas as pl
from jax.experimental.pallas import tpu as pltpu
from jax.experimental.pallas import tpu_sc as plsc

f32 = jnp.float32
bf16 = jnp.bfloat16
u32 = jnp.uint32
i32 = jnp.int32
SDS = jax.ShapeDtypeStruct

D_MODEL = 1024
D_S5 = 512
S5_GROUP = 16
S5_GROUPS = 32
S5_STATE = 64
N_STATE = S5_GROUPS * S5_STATE
D_GLA = 512
GLA_HEADS = 4
GLA_DV = 128
GLA_DK = 64
D_GLA_K = 256
GLA_GATE_RANK = 16
GLA_TAU = 16.0
GLA_CHUNK = 64
D_FF = 2816
N_EXPERTS = 8
D_FF_EXPERT = 3584
EPS = 1e-6

LANE = 128
SUBLANE = 8
VMEM_LIMIT = 56 * 1024 * 1024

TM_PROJ = 512
TT_S5 = 128
S5_SLABS = 4
S5_PITCH_PAD = 8
TG_GLA = 256
TM_FFN = 512
TM_MOE = 1024
TF_MOE = 512
TM_ROUTE = 512
SC_WINDOW = 128
SC_ROW = 256


def _cparams(*sem):
    return pltpu.CompilerParams(dimension_semantics=sem, vmem_limit_bytes=VMEM_LIMIT)


def _rms(x, g):
    ms = jnp.mean(x * x, axis=-1, keepdims=True)
    return x * lax.rsqrt(ms + EPS) * g


def _inproj_body(x_ref, g_ref, w_ref, wkt_ref, u_ref, q_ref, kt_ref, v_ref, r_ref, gl_ref):
    hn = _rms(x_ref[...], g_ref[...]).astype(bf16)

    def proj(lo, hi):
        return jnp.dot(hn, w_ref[:, lo:hi], preferred_element_type=f32).astype(bf16)

    u_ref[...] = proj(0, 512)
    q_ref[...] = proj(512, 768)
    v_ref[...] = proj(768, 1280)
    r_ref[...] = proj(1280, 1792)
    gl_ref[...] = proj(1792, 1920)
    kt_ref[...] = lax.dot_general(wkt_ref[...], hn, (((1,), (1,)), ((), ())),
                                  preferred_element_type=f32).astype(bf16)


def _inproj(x2, g, w_cat, w_kt, bsz, seq):
    n = x2.shape[0]
    tm = TM_PROJ
    per_b = seq // tm
    row = lambda i: (i, 0)
    const = lambda i: (0, 0)
    return pl.pallas_call(
        _inproj_body,
        grid=(n // tm,),
        in_specs=[pl.BlockSpec((tm, D_MODEL), row),
                  pl.BlockSpec((1, D_MODEL), const),
                  pl.BlockSpec(w_cat.shape, const),
                  pl.BlockSpec(w_kt.shape, const)],
        out_specs=[pl.BlockSpec((tm, D_S5), row),
                   pl.BlockSpec((tm, D_GLA_K), row),
                   pl.BlockSpec((None, D_GLA_K, tm), lambda i: (i // per_b, 0, i % per_b)),
                   pl.BlockSpec((tm, D_GLA), row),
                   pl.BlockSpec((tm, D_GLA), row),
                   pl.BlockSpec((tm, LANE), row)],
        out_shape=[SDS((n, D_S5), bf16), SDS((n, D_GLA_K), bf16),
                   SDS((bsz, D_GLA_K, seq), bf16), SDS((n, D_GLA), bf16),
                   SDS((n, D_GLA), bf16), SDS((n, LANE), bf16)],
        compiler_params=_cparams("parallel"),
        name="inproj",
    )(x2, g, w_cat, w_kt)


N_SLAB = N_STATE // LANE


def _s5_prep(lam_re, lam_im, log_dt, b_re, b_im, c_re, c_im):
    lr = jnp.minimum(lam_re.astype(f32), -1e-4)
    li = lam_im.astype(f32)
    dt = jnp.exp(log_dt.astype(f32))[:, None]
    mag = jnp.exp(lr * dt)
    ab_re = mag * jnp.cos(li * dt)
    ab_im = mag * jnp.sin(li * dt)
    nr = ab_re - 1.0
    ni = ab_im
    den = lr * lr + li * li
    f_re = (nr * lr + ni * li) / den
    f_im = (ni * lr - nr * li) / den
    br = b_re.astype(f32)
    bi = b_im.astype(f32)
    bb_re = f_re[..., None] * br - f_im[..., None] * bi
    bb_im = f_re[..., None] * bi + f_im[..., None] * br
    abb_re = ab_re[..., None] * bb_re - ab_im[..., None] * bb_im
    abb_im = ab_re[..., None] * bb_im + ab_im[..., None] * bb_re
    cr = c_re.astype(f32)
    ci = c_im.astype(f32)
    ca_re = cr * ab_re[:, None, :] - ci * ab_im[:, None, :]
    ca_im = cr * ab_im[:, None, :] + ci * ab_re[:, None, :]
    cb = jnp.einsum('gop,gpc->goc', cr, bb_re) - jnp.einsum('gop,gpc->goc', ci, bb_im)

    def b_tiles(bb):
        bt = jnp.transpose(bb, (0, 2, 1)).reshape(8, 4, S5_GROUP, S5_STATE)
        gl = jnp.arange(8)[None, :, None]
        gs = jnp.arange(4)[None, None, :]
        nn = jnp.arange(8)[:, None, None]
        sel = (gl == 4 * (nn % 2) + gs).astype(f32)
        t = jnp.einsum('ngs,nscp->ngcsp', sel, bt)
        return t.reshape(8, LANE, 4 * S5_STATE)

    bmat = jnp.concatenate(
        [jnp.concatenate([b_tiles(abb_re), b_tiles(bb_re)], axis=1),
         jnp.concatenate([b_tiles(abb_im), b_tiles(bb_im)], axis=1)], axis=0).astype(bf16)

    def c_tiles(c, sign):
        ct = jnp.transpose(c, (0, 2, 1)).reshape(2, 16, S5_STATE, S5_GROUP)
        eye = jnp.eye(16, dtype=f32)
        t = jnp.einsum('gh,jgpo->jgpho', eye, ct) * sign
        return t.reshape(2, 16 * S5_STATE, 16 * S5_GROUP)

    cmat = jnp.stack([c_tiles(cr, 1.0), c_tiles(ci, -1.0)], axis=1).astype(bf16)
    camat = jnp.stack([c_tiles(ca_re, 1.0), c_tiles(ca_im, -1.0)], axis=1).astype(bf16)
    cbt = jnp.transpose(cb, (0, 2, 1)).reshape(2, 16, S5_GROUP, S5_GROUP)
    cbmat = jnp.einsum('gh,jgco->jgcho', jnp.eye(16, dtype=f32), cbt).reshape(2, 256, 256).astype(bf16)
    a2_re = (ab_re * ab_re - ab_im * ab_im).reshape(N_SLAB, 1, LANE)
    a2_im = (2.0 * ab_re * ab_im).reshape(N_SLAB, 1, LANE)
    return bmat, cmat, camat, cbmat, a2_re, a2_im


def _s5_body(u_ref, bm_ref, cm_ref, cam_ref, cbm_ref, are_ref, aim_ref, d_ref, y_ref,
             hbuf, hstate, zcarry):
    tt = u_ref.shape[1]
    tp = tt // 2
    rows = SUBLANE * tp
    pitch = tp + S5_PITCH_PAD

    @pl.when(pl.program_id(0) == 0)
    def _():
        hstate[...] = jnp.zeros_like(hstate)
        zcarry[...] = jnp.zeros_like(zcarry)

    w = pltpu.bitcast(u_ref[...].reshape(SUBLANE * tt, D_S5), u32)
    ue_f = pltpu.bitcast(w << 16, f32)
    uo_f = pltpu.bitcast(w & jnp.uint32(0xFFFF0000), f32)
    ue = ue_f.astype(bf16)
    uo = uo_f.astype(bf16)

    def put(slab, val):
        for b in range(SUBLANE):
            hbuf[slab, b * pitch:b * pitch + tp, :] = val[b * tp:(b + 1) * tp, :]

    def get(slab):
        return jnp.concatenate([hbuf[slab, b * pitch:b * pitch + tp, :] for b in range(SUBLANE)], axis=0)

    for n in range(2 * 8):
        ks = LANE * ((n % 8) // 2)
        lhs = jnp.concatenate([ue[:, ks:ks + LANE], uo[:, ks:ks + LANE]], axis=1)
        res = jnp.dot(lhs, bm_ref[n], preferred_element_type=f32)
        put(2 * n, res[:, :LANE])
        put(2 * n + 1, res[:, LANE:])

    for c0 in range(0, N_SLAB, S5_SLABS):
        ar = [jnp.broadcast_to(are_ref[c0 + s], (SUBLANE, LANE)) for s in range(S5_SLABS)]
        ai = [jnp.broadcast_to(aim_ref[c0 + s], (SUBLANE, LANE)) for s in range(S5_SLABS)]

        def step(t, carry):
            out = []
            for s in range(S5_SLABS):
                hr, hi = carry[2 * s], carry[2 * s + 1]
                sl = pl.ds(t, SUBLANE, stride=pitch)
                bur = hbuf[c0 + s, sl, :]
                bui = hbuf[N_SLAB + c0 + s, sl, :]
                nr = ar[s] * hr - ai[s] * hi + bur
                ni = ar[s] * hi + ai[s] * hr + bui
                hbuf[c0 + s, sl, :] = nr
                hbuf[N_SLAB + c0 + s, sl, :] = ni
                out += [nr, ni]
            return tuple(out)

        init = []
        for s in range(S5_SLABS):
            init += [hstate[c0 + s], hstate[N_SLAB + c0 + s]]
        fin = lax.fori_loop(0, tp, step, tuple(init), unroll=4)
        for s in range(S5_SLABS):
            hstate[c0 + s] = fin[2 * s]
            hstate[N_SLAB + c0 + s] = fin[2 * s + 1]

    first = lax.broadcasted_iota(i32, (rows, 256), 0) % tp == 0
    for j in range(2):
        h_re = jnp.concatenate([get(8 * j + s) for s in range(8)], axis=1).astype(bf16)
        h_im = jnp.concatenate([get(N_SLAB + 8 * j + s) for s in range(8)], axis=1).astype(bf16)
        cs = slice(256 * j, 256 * j + 256)
        dj = d_ref[:, cs]
        yo = jnp.dot(h_re, cm_ref[j, 0], preferred_element_type=f32)
        yo = yo + jnp.dot(h_im, cm_ref[j, 1], preferred_element_type=f32)
        yo = yo + dj * uo_f[:, cs]
        z = jnp.dot(h_re, cam_ref[j, 0], preferred_element_type=f32)
        z = z + jnp.dot(h_im, cam_ref[j, 1], preferred_element_type=f32)
        prev = jnp.concatenate([jnp.broadcast_to(zcarry[b:b + 1, cs], (tp, 256))
                                for b in range(SUBLANE)], axis=0)
        zs = jnp.where(first, prev, pltpu.roll(z, 1, 0))
        zcarry[:, cs] = jnp.concatenate([z[(b + 1) * tp - 1:(b + 1) * tp, :] for b in range(SUBLANE)], axis=0)
        ye = zs + jnp.dot(ue[:, cs], cbm_ref[j], preferred_element_type=f32) + dj * ue_f[:, cs]
        ge = pltpu.bitcast(jax.nn.gelu(ye).astype(bf16).astype(f32), u32)
        go = pltpu.bitcast(jax.nn.gelu(yo).astype(bf16).astype(f32), u32)
        packed = (ge >> 16) | go
        y_ref[:, :, cs] = pltpu.bitcast(packed, bf16).reshape(SUBLANE, tt, 256)


def _s5(u3, bmat, cmat, camat, cbmat, a2_re, a2_im, d_skip):
    bsz, seq, _ = u3.shape
    assert bsz == SUBLANE
    tt = TT_S5
    blk = lambda i: (0, i, 0)
    c3 = lambda i: (0, 0, 0)
    c4 = lambda i: (0, 0, 0, 0)
    return pl.pallas_call(
        _s5_body,
        grid=(seq // tt,),
        in_specs=[pl.BlockSpec((bsz, tt, D_S5), blk),
                  pl.BlockSpec(bmat.shape, c3),
                  pl.BlockSpec(cmat.shape, c4),
                  pl.BlockSpec(camat.shape, c4),
                  pl.BlockSpec(cbmat.shape, c3),
                  pl.BlockSpec(a2_re.shape, c3),
                  pl.BlockSpec(a2_im.shape, c3),
                  pl.BlockSpec((1, D_S5), lambda i: (0, 0))],
        out_specs=pl.BlockSpec((bsz, tt, D_S5), blk),
        out_shape=SDS((bsz, seq, D_S5), bf16),
        scratch_shapes=[pltpu.VMEM((2 * N_SLAB, bsz * (tt // 2 + S5_PITCH_PAD), LANE), f32),
                        pltpu.VMEM((2 * N_SLAB, SUBLANE, LANE), f32),
                        pltpu.VMEM((SUBLANE, D_S5), f32)],
        compiler_params=_cparams("arbitrary"),
        name="s5_scan",
    )(u3, bmat, cmat, camat, cbmat, a2_re, a2_im, d_skip)


def _split2(x):
    hi = x.astype(bf16)
    lo = (x - hi.astype(f32)).astype(bf16)
    return hi, lo


def _log_sigmoid(x):
    return -(jnp.maximum(-x, 0.0) + jnp.log1p(jnp.exp(-jnp.abs(x))))


def _gla_body(q_ref, kt_ref, v_ref, r_ref, g_ref, wa_ref, wat_ref, ba_ref, bat_ref, gn_ref,
              o_ref, s_ref):
    c = GLA_CHUNK
    tg = q_ref.shape[0]

    @pl.when(pl.program_id(1) == 0)
    def _():
        s_ref[...] = jnp.zeros_like(s_ref)

    nc = tg // c
    nh = GLA_HEADS
    ri = lax.broadcasted_iota(i32, (tg, tg), 0)
    ci = lax.broadcasted_iota(i32, (tg, tg), 1)
    same = (ri // c) == (ci // c)
    tril = (same & (ri >= ci)).astype(bf16)
    triu = (same & (ri <= ci)).astype(bf16)
    blk = same.astype(bf16)

    g = g_ref[...]
    la = _log_sigmoid(jnp.dot(g, wa_ref[...], preferred_element_type=f32) + ba_ref[...]) / GLA_TAU
    lat = _log_sigmoid(lax.dot_general(wat_ref[...], g, (((1,), (1,)), ((), ())),
                                       preferred_element_type=f32) + bat_ref[...]) / GLA_TAU
    la_hi, la_lo = _split2(la)
    cum = (jnp.dot(tril, la_hi, preferred_element_type=f32)
           + jnp.dot(tril, la_lo, preferred_element_type=f32))
    lat_hi, lat_lo = _split2(lat)
    cumt = (jnp.dot(lat_hi, triu, preferred_element_type=f32)
            + jnp.dot(lat_lo, triu, preferred_element_type=f32))
    clt = (jnp.dot(lat_hi, blk, preferred_element_type=f32)
           + jnp.dot(lat_lo, blk, preferred_element_type=f32))

    qt = q_ref[...].astype(f32) * jnp.exp(cum)
    ktt = kt_ref[...].astype(f32)
    k_t = (ktt * jnp.exp(-cumt)).astype(bf16)
    k_end = ktt * jnp.exp(clt - cumt)
    v = v_ref[...]

    head_of_lane = lax.broadcasted_iota(i32, (c, D_GLA_K), 1) // GLA_DK
    q_stack = jnp.concatenate(
        [jnp.where(head_of_lane == h, qt[cc * c:(cc + 1) * c, :], 0.0)
         for cc in range(nc) for h in range(nh)], axis=0).astype(bf16)
    scores = jnp.dot(q_stack, k_t, preferred_element_type=f32)
    r_idx = lax.broadcasted_iota(i32, (nc * nh * c, tg), 0)
    c_idx = lax.broadcasted_iota(i32, (nc * nh * c, tg), 1)
    causal = (r_idx // (nh * c) == c_idx // c) & (r_idx % c >= c_idx % c)
    scores = jnp.where(causal, scores, 0.0).astype(bf16)
    o_full = jnp.dot(scores, v, preferred_element_type=f32)

    chunk_of_lane = lax.broadcasted_iota(i32, (D_GLA_K, tg), 1) // c
    k_stack = jnp.concatenate(
        [jnp.where(chunk_of_lane == cc, k_end, 0.0) for cc in range(nc)], axis=0).astype(bf16)
    upd_full = jnp.dot(k_stack, v, preferred_element_type=f32)

    state = s_ref[...]
    o_inter = []
    for cc in range(nc):
        o_inter.append(jnp.dot(q_stack[cc * nh * c:(cc + 1) * nh * c, :], state.astype(bf16),
                               preferred_element_type=f32))
        upd = jnp.concatenate(
            [upd_full[cc * D_GLA_K + h * GLA_DK:cc * D_GLA_K + (h + 1) * GLA_DK,
                      h * GLA_DV:(h + 1) * GLA_DV] for h in range(nh)], axis=0)
        state = jnp.exp(clt[:, cc * c:cc * c + 1]) * state + upd
    s_ref[...] = state

    rows = []
    for cc in range(nc):
        outs = []
        for h in range(nh):
            r0 = (cc * nh + h) * c
            o = o_full[r0:r0 + c, h * GLA_DV:(h + 1) * GLA_DV] + o_inter[cc][h * c:(h + 1) * c, :]
            o = o * lax.rsqrt(jnp.mean(o * o, axis=-1, keepdims=True) + EPS)
            outs.append(o)
        rows.append(jnp.concatenate(outs, axis=1))
    o_cat = jnp.concatenate(rows, axis=0)
    o_ref[...] = (o_cat * gn_ref[...] * jax.nn.silu(r_ref[...].astype(f32))).astype(bf16)


def _gla(q3, kt3, v3, r3, g3, wa, wat, ba, bat, gn):
    bsz, seq, _ = q3.shape
    tg = TG_GLA
    tok = lambda b, i: (b, i, 0)
    c2 = lambda b, i: (0, 0)
    return pl.pallas_call(
        _gla_body,
        grid=(bsz, seq // tg),
        in_specs=[pl.BlockSpec((None, tg, D_GLA_K), tok),
                  pl.BlockSpec((None, D_GLA_K, tg), lambda b, i: (b, 0, i)),
                  pl.BlockSpec((None, tg, D_GLA), tok),
                  pl.BlockSpec((None, tg, D_GLA), tok),
                  pl.BlockSpec((None, tg, LANE), tok),
                  pl.BlockSpec(wa.shape, c2), pl.BlockSpec(wat.shape, c2),
                  pl.BlockSpec(ba.shape, c2), pl.BlockSpec(bat.shape, c2),
                  pl.BlockSpec(gn.shape, c2)],
        out_specs=pl.BlockSpec((None, tg, D_GLA), tok),
        out_shape=SDS((bsz, seq, D_GLA), bf16),
        scratch_shapes=[pltpu.VMEM((D_GLA_K, GLA_DV), f32)],
        compiler_params=_cparams("parallel", "arbitrary"),
        name="gla",
    )(q3, kt3, v3, r3, g3, wa, wat, ba, bat, gn)


def _mix_out(x_ref, ys_ref, yg_ref, wgl_ref, bg_ref, gs_ref, wo_ref):
    y = ys_ref[...]
    z = jnp.dot(y, wgl_ref[...], preferred_element_type=f32) + bg_ref[...]
    yf = y.astype(f32) * jax.nn.sigmoid(z)
    ys = _rms(yf, gs_ref[...]).astype(bf16)
    acc = jnp.dot(ys, wo_ref[0:D_S5, :], preferred_element_type=f32)
    acc = acc + jnp.dot(yg_ref[...], wo_ref[D_S5:, :], preferred_element_type=f32)
    return x_ref[...] + acc


def _mix_specs(tm, w_glu, w_out):
    row = lambda i: (i, 0)
    const = lambda i: (0, 0)
    once = dict(pipeline_mode=pl.Buffered(1))
    return [pl.BlockSpec((tm, D_MODEL), row),
            pl.BlockSpec((tm, D_S5), row),
            pl.BlockSpec((tm, D_GLA), row),
            pl.BlockSpec(w_glu.shape, const, **once),
            pl.BlockSpec((1, D_S5), const),
            pl.BlockSpec((1, D_S5), const),
            pl.BlockSpec(w_out.shape, const, **once)]


def _mix_ffn_body(x_ref, ys_ref, yg_ref, wgl_ref, bg_ref, gs_ref, wo_ref,
                  g_ref, wg_ref, wu_ref, wd_ref, o_ref):
    x = _mix_out(x_ref, ys_ref, yg_ref, wgl_ref, bg_ref, gs_ref, wo_ref)
    hn = _rms(x, g_ref[...]).astype(bf16)
    gate = jnp.dot(hn, wg_ref[...], preferred_element_type=f32)
    up = jnp.dot(hn, wu_ref[...], preferred_element_type=f32)
    act = (jax.nn.silu(gate) * up).astype(bf16)
    o_ref[...] = x + jnp.dot(act, wd_ref[...], preferred_element_type=f32)


def _mix_ffn(x2, ys, yg, w_glu, b_glu, g_s5, w_out, g, wg, wu, wd):
    n = x2.shape[0]
    tm = TM_FFN
    row = lambda i: (i, 0)
    const = lambda i: (0, 0)
    once = dict(pipeline_mode=pl.Buffered(1))
    return pl.pallas_call(
        _mix_ffn_body,
        grid=(n // tm,),
        in_specs=_mix_specs(tm, w_glu, w_out) + [
            pl.BlockSpec((1, D_MODEL), const),
            pl.BlockSpec(wg.shape, const, **once),
            pl.BlockSpec(wu.shape, const, **once),
            pl.BlockSpec(wd.shape, const, **once)],
        out_specs=pl.BlockSpec((tm, D_MODEL), row),
        out_shape=SDS((n, D_MODEL), f32),
        compiler_params=_cparams("parallel"),
        name="mix_ffn",
    )(x2, ys, yg, w_glu, b_glu, g_s5, w_out, g, wg, wu, wd)


def _pack_bf16_pairs(a):
    bits = pltpu.bitcast(a.astype(bf16).astype(f32), u32)
    half = a.shape[1] // 2
    return bits[:, :half] | (bits[:, half:] >> 16)


def _unpack_bf16_pairs(p):
    hi = pltpu.bitcast(p & jnp.uint32(0xFFFF0000), f32).astype(bf16)
    lo = pltpu.bitcast(p << 16, f32).astype(bf16)
    return hi, lo


def _mix_router_body(x_ref, ys_ref, yg_ref, wgl_ref, bg_ref, gs_ref, wo_ref,
                     g_ref, wh_ref, wl_ref, xo_ref, hp_ref, meta_ref, cnt_ref, carry):
    tm = x_ref.shape[0]

    @pl.when(pl.program_id(0) == 0)
    def _():
        carry[...] = jnp.zeros_like(carry)

    x = _mix_out(x_ref, ys_ref, yg_ref, wgl_ref, bg_ref, gs_ref, wo_ref)
    xo_ref[...] = x
    hn = _rms(x, g_ref[...])
    packed = _pack_bf16_pairs(hn)
    hp_ref[0] = packed[:, :SC_ROW]
    hp_ref[1] = packed[:, SC_ROW:]

    h_hi, h_lo = _split2(hn)
    logits = (jnp.dot(h_hi, wh_ref[...], preferred_element_type=f32)
              + jnp.dot(h_lo, wh_ref[...], preferred_element_type=f32)
              + jnp.dot(h_hi, wl_ref[...], preferred_element_type=f32))
    lane = lax.broadcasted_iota(i32, (tm, LANE), 1)
    neg = jnp.float32(-jnp.inf)
    logits = jnp.where(lane < N_EXPERTS, logits, neg)
    m1 = jnp.max(logits, axis=-1, keepdims=True)
    i1 = jnp.min(jnp.where(logits == m1, lane, LANE), axis=-1, keepdims=True)
    l2 = jnp.where(lane == i1, neg, logits)
    m2 = jnp.max(l2, axis=-1, keepdims=True)
    i2 = jnp.min(jnp.where(l2 == m2, lane, LANE), axis=-1, keepdims=True)
    e21 = jnp.exp(m2 - m1)
    w1 = 1.0 / (1.0 + e21)
    w2 = e21 / (1.0 + e21)

    sel1 = lane == i1
    sel2 = lane == i2
    sel = (sel1 | sel2).astype(f32)
    ri = lax.broadcasted_iota(i32, (tm, tm), 0)
    ci = lax.broadcasted_iota(i32, (tm, tm), 1)
    tril = (ri >= ci).astype(bf16)
    incl = jnp.dot(tril, sel.astype(bf16), preferred_element_type=f32)
    rank = incl - sel + carry[0:1, :]
    r1 = jnp.sum(jnp.where(sel1, rank, 0.0), axis=-1, keepdims=True)
    r2 = jnp.sum(jnp.where(sel2, rank, 0.0), axis=-1, keepdims=True)
    new_cnt = carry[0:1, :] + incl[tm - 1:tm, :]
    carry[...] = jnp.broadcast_to(new_cnt, carry.shape)
    cnt_ref[...] = jnp.broadcast_to(new_cnt, cnt_ref.shape)

    meta = jnp.where(lane == 0, i1.astype(f32), 0.0)
    meta = jnp.where(lane == 1, i2.astype(f32), meta)
    meta = jnp.where(lane == 2, w1, meta)
    meta = jnp.where(lane == 3, w2, meta)
    meta = jnp.where(lane == 4, r1, meta)
    meta = jnp.where(lane == 5, r2, meta)
    meta_ref[...] = meta


def _mix_router(x2, ys, yg, w_glu, b_glu, g_s5, w_out, g, w_hi, w_lo):
    n = x2.shape[0]
    tm = TM_ROUTE
    row = lambda i: (i, 0)
    const = lambda i: (0, 0)
    return pl.pallas_call(
        _mix_router_body,
        grid=(n // tm,),
        in_specs=_mix_specs(tm, w_glu, w_out) + [
            pl.BlockSpec((1, D_MODEL), const),
            pl.BlockSpec(w_hi.shape, const),
            pl.BlockSpec(w_lo.shape, const)],
        out_specs=[pl.BlockSpec((tm, D_MODEL), row),
                   pl.BlockSpec((2, tm, SC_ROW), lambda i: (0, i, 0)),
                   pl.BlockSpec((tm, LANE), row),
                   pl.BlockSpec((SUBLANE, LANE), const)],
        out_shape=[SDS((n, D_MODEL), f32), SDS((2, n, SC_ROW), u32), SDS((n, LANE), f32),
                   SDS((SUBLANE, LANE), f32)],
        scratch_shapes=[pltpu.VMEM((SUBLANE, LANE), f32)],
        compiler_params=_cparams("arbitrary"),
        name="mix_router",
    )(x2, ys, yg, w_glu, b_glu, g_s5, w_out, g, w_hi, w_lo)


def _sc_gather(table, idx):
    ni = idx.shape[0]
    mesh = plsc.VectorSubcoreMesh(core_axis_name="core", subcore_axis_name="subcore")
    idx2 = idx.reshape(1, ni)

    @pl.kernel(out_type=SDS((ni, SC_ROW), table.dtype), mesh=mesh)
    def kern(t_hbm, i_hbm, o_hbm):
        def body(i_vmem, o_vmem):
            pltpu.sync_copy(t_hbm.at[i_vmem.at[0]], o_vmem)

        pltpu.emit_pipeline(
            body, grid=(ni // SC_WINDOW,),
            in_specs=[pl.BlockSpec((1, SC_WINDOW), index_map=lambda i: (0, i))],
            out_specs=[pl.BlockSpec((SC_WINDOW, SC_ROW), index_map=lambda i: (i, 0))],
            core_axis_name=("core", "subcore"),
            dimension_semantics=(pltpu.PARALLEL,),
        )(i_hbm, o_hbm)

    return kern(table, idx2)


def _sc_scatter2(x, idx0, idx1, nrows):
    ni = x.shape[0]
    mesh = plsc.VectorSubcoreMesh(core_axis_name="core", subcore_axis_name="subcore")

    @pl.kernel(out_type=SDS((nrows, SC_ROW), x.dtype), mesh=mesh)
    def kern(x_hbm, i0_hbm, i1_hbm, o_hbm):
        def body(x_vmem, i0_vmem, i1_vmem):
            pltpu.sync_copy(x_vmem, o_hbm.at[i0_vmem.at[0]])
            pltpu.sync_copy(x_vmem, o_hbm.at[i1_vmem.at[0]])

        pltpu.emit_pipeline(
            body, grid=(ni // SC_WINDOW,),
            in_specs=[pl.BlockSpec((SC_WINDOW, SC_ROW), index_map=lambda i: (i, 0)),
                      pl.BlockSpec((1, SC_WINDOW), index_map=lambda i: (0, i)),
                      pl.BlockSpec((1, SC_WINDOW), index_map=lambda i: (0, i))],
            out_specs=[],
            core_axis_name=("core", "subcore"),
            dimension_semantics=(pltpu.PARALLEL,),
        )(x_hbm, i0_hbm, i1_hbm)

    return kern(x, idx0.reshape(1, ni), idx1.reshape(1, ni))


def _moe_ffn_body(be_ref, bn_ref, xs_ref, wg_ref, wu_ref, wd_ref, ys_ref, xb, act0, act1, acc):
    i = pl.program_id(0)
    f = pl.program_id(1)
    nf = pl.num_programs(1) - 1
    nvalid = bn_ref[i]

    def gate_up():
        x = xb[...]
        gate = jnp.dot(x, wg_ref[...].astype(bf16), preferred_element_type=f32)
        up = jnp.dot(x, wu_ref[...].astype(bf16), preferred_element_type=f32)
        return (jax.nn.silu(gate) * up).astype(bf16)

    def down(act_ref):
        return jnp.dot(act_ref[...], wd_ref[...].astype(bf16), preferred_element_type=f32)

    @pl.when(nvalid > 0)
    def _():
        @pl.when(f == 0)
        def _():
            live = lax.broadcasted_iota(i32, (xs_ref.shape[1], SC_ROW), 0) < nvalid
            hi0, lo0 = _unpack_bf16_pairs(jnp.where(live, xs_ref[0], jnp.uint32(0)))
            hi1, lo1 = _unpack_bf16_pairs(jnp.where(live, xs_ref[1], jnp.uint32(0)))
            xb[...] = jnp.concatenate([hi0, hi1, lo0, lo1], axis=1)
            acc[...] = jnp.zeros_like(acc)
            act0[...] = gate_up()

        @pl.when((f > 0) & (f < nf) & (f % 2 == 1))
        def _():
            act1[...] = gate_up()
            acc[...] += down(act0)

        @pl.when((f > 0) & (f < nf) & (f % 2 == 0))
        def _():
            act0[...] = gate_up()
            acc[...] += down(act1)

        @pl.when(f == nf)
        def _():
            last = act0 if (D_FF_EXPERT // TF_MOE - 1) % 2 == 0 else act1
            packed = _pack_bf16_pairs(acc[...] + down(last))
            ys_ref[0] = packed[:, :SC_ROW]
            ys_ref[1] = packed[:, SC_ROW:]

    @pl.when((nvalid == 0) & (f == nf))
    def _():
        ys_ref[...] = jnp.zeros_like(ys_ref)


def _moe_ffn(blk_e, blk_n, xs, wg, wu, wd):
    npad = xs.shape[1]
    tm, tf = TM_MOE, TF_MOE
    nblk = npad // tm
    nf = D_FF_EXPERT // tf

    def tile(i, f, bn):
        return jnp.where(bn[i] > 0, jnp.clip(f, 0, nf - 1), nf - 1)

    grid_spec = pltpu.PrefetchScalarGridSpec(
        num_scalar_prefetch=2,
        grid=(nblk, nf + 1),
        in_specs=[pl.BlockSpec((2, tm, SC_ROW), lambda i, f, be, bn: (0, i, 0)),
                  pl.BlockSpec((None, D_MODEL, tf), lambda i, f, be, bn: (be[i], 0, tile(i, f, bn))),
                  pl.BlockSpec((None, D_MODEL, tf), lambda i, f, be, bn: (be[i], 0, tile(i, f, bn))),
                  pl.BlockSpec((None, tf, D_MODEL), lambda i, f, be, bn: (be[i], tile(i, f - 1, bn), 0))],
        out_specs=pl.BlockSpec((2, tm, SC_ROW), lambda i, f, be, bn: (0, i, 0)),
        scratch_shapes=[pltpu.VMEM((tm, D_MODEL), bf16), pltpu.VMEM((tm, tf), bf16),
                        pltpu.VMEM((tm, tf), bf16), pltpu.VMEM((tm, D_MODEL), f32)],
    )
    return pl.pallas_call(
        _moe_ffn_body,
        grid_spec=grid_spec,
        out_shape=SDS((2, npad, SC_ROW), u32),
        compiler_params=_cparams("parallel", "arbitrary"),
        name="moe_ffn",
    )(blk_e, blk_n, xs, wg, wu, wd)


def _combine_body(x_ref, yg_ref, meta_ref, g_ref, o_ref, *, final_norm):
    w1 = meta_ref[:, 2:3]
    w2 = meta_ref[:, 3:4]

    def rows(k):
        hi0, lo0 = _unpack_bf16_pairs(yg_ref[0, k])
        hi1, lo1 = _unpack_bf16_pairs(yg_ref[1, k])
        return jnp.concatenate([hi0, hi1, lo0, lo1], axis=1).astype(f32)

    xo = x_ref[...] + (w1 * rows(0) + w2 * rows(1))
    if final_norm:
        xo = _rms(xo, g_ref[...])
    o_ref[...] = xo


def _combine(x2, yg, meta, g_final, final_norm):
    n = x2.shape[0]
    tm = TM_ROUTE
    row = lambda i: (i, 0)
    return pl.pallas_call(
        functools.partial(_combine_body, final_norm=final_norm),
        grid=(n // tm,),
        in_specs=[pl.BlockSpec((tm, D_MODEL), row),
                  pl.BlockSpec((2, 2, tm, SC_ROW), lambda i: (0, 0, i, 0)),
                  pl.BlockSpec((tm, LANE), row),
                  pl.BlockSpec((1, D_MODEL), lambda i: (0, 0))],
        out_specs=pl.BlockSpec((tm, D_MODEL), row),
        out_shape=SDS((n, D_MODEL), f32),
        compiler_params=_cparams("parallel"),
        name="moe_combine",
    )(x2, yg, meta, g_final)


def _moe_layer(mix_args, g_ffn, w_router, wg, wu, wd, g_final, final_norm):
    n = mix_args[0].shape[0]
    tm = TM_MOE
    npad = 2 * n + N_EXPERTS * tm
    wr = jnp.zeros((D_MODEL, LANE), f32).at[:, :N_EXPERTS].set(w_router.astype(f32))
    wr_hi = wr.astype(bf16)
    wr_lo = (wr - wr_hi.astype(f32)).astype(bf16)
    x2, hp, meta, cnt = _mix_router(*mix_args, g_ffn, wr_hi, wr_lo)

    counts = cnt[0, :N_EXPERTS].astype(i32)
    padded = ((counts + tm - 1) // tm) * tm
    ends = jnp.cumsum(padded)
    offs = ends - padded
    ids = jnp.arange(N_EXPERTS, dtype=i32)
    e12 = meta[:, 0:2].astype(i32).T
    r12 = meta[:, 4:6].astype(i32).T
    pos_t = r12 + jnp.sum(jnp.where(e12[..., None] == ids, offs, 0), axis=-1)
    blk_start = jnp.arange(npad // tm, dtype=i32) * tm
    blk_e = jnp.minimum(jnp.sum((blk_start[:, None] >= ends[None, :]).astype(i32), axis=1), N_EXPERTS - 1)
    own = blk_e[:, None] == ids
    blk_end = jnp.sum(jnp.where(own, offs + counts, 0), axis=1)
    blk_n = jnp.where(blk_start < ends[-1], jnp.clip(blk_end - blk_start, 0, tm), 0)
    last_e = jnp.max(jnp.where(blk_n > 0, blk_e, 0))
    blk_e = jnp.where(blk_n > 0, blk_e, last_e)

    xs = _sc_scatter2(hp.reshape(2 * n, SC_ROW),
                      jnp.concatenate([pos_t[0], pos_t[0] + npad]),
                      jnp.concatenate([pos_t[1], pos_t[1] + npad]),
                      2 * npad).reshape(2, npad, SC_ROW)
    ys = _moe_ffn(blk_e, blk_n, xs, wg, wu, wd)
    gidx = jnp.concatenate([pos_t.reshape(-1), pos_t.reshape(-1) + npad])
    yg = _sc_gather(ys.reshape(2 * npad, SC_ROW), gidx).reshape(2, 2, n, SC_ROW)
    return _combine(x2, yg, meta, g_final, final_norm)


def _row(v):
    return v.astype(f32).reshape(1, -1)


def kernel(x, norm_mix, w_in, s5_lambda_re, s5_lambda_im, s5_log_dt, s5_b_re, s5_b_im, s5_c_re, s5_c_im, s5_d, s5_w_glu, s5_b_glu, s5_out_norm, gla_w_a2, gla_b_a2, gla_out_norm, w_out, norm_ffn, ffn_w_gate, ffn_w_up, ffn_w_down, moe_w_router, moe_w_gate, moe_w_up, moe_w_down, norm_final):
    bsz, seq, _ = x.shape
    n = bsz * seq
    depth = w_in.shape[0]
    x2 = x.reshape(n, D_MODEL)
    for l in range(depth):
        wi = w_in[l]
        w_u, w_q, w_k, w_v, w_r, w_g = (wi[:, 0:512], wi[:, 512:768], wi[:, 768:1024],
                                        wi[:, 1024:1536], wi[:, 1536:2048], wi[:, 2048:2064])
        w_gp = jnp.zeros((D_MODEL, LANE), f32).at[:, :GLA_GATE_RANK].set(w_g)
        w_cat = jnp.concatenate([w_u, w_q * (GLA_DK ** -0.5), w_v, w_r, w_gp], axis=1).astype(bf16)
        w_kt = w_k.T.astype(bf16)
        u, q, kt, v, r, gl = _inproj(x2, _row(norm_mix[l]), w_cat, w_kt, bsz, seq)

        s5_mats = _s5_prep(s5_lambda_re[l], s5_lambda_im[l], s5_log_dt[l],
                           s5_b_re[l], s5_b_im[l], s5_c_re[l], s5_c_im[l])
        ys = _s5(u.reshape(bsz, seq, D_S5), *s5_mats, _row(s5_d[l]))

        wa = jnp.zeros((LANE, D_GLA_K), f32).at[:GLA_GATE_RANK].set(gla_w_a2[l]).astype(bf16)
        yg = _gla(q.reshape(bsz, seq, D_GLA_K), kt, v.reshape(bsz, seq, D_GLA),
                  r.reshape(bsz, seq, D_GLA), gl.reshape(bsz, seq, LANE),
                  wa, wa.T, _row(gla_b_a2[l]), gla_b_a2[l].astype(f32).reshape(-1, 1),
                  _row(gla_out_norm[l]))

        mix_args = (x2, ys.reshape(n, D_S5), yg.reshape(n, D_GLA), s5_w_glu[l].astype(bf16),
                    _row(s5_b_glu[l]), _row(s5_out_norm[l]), w_out[l].astype(bf16))

        i = l // 2
        last = l == depth - 1
        if l % 2 == 0:
            x2 = _mix_ffn(*mix_args, _row(norm_ffn[l]), ffn_w_gate[i].astype(bf16),
                          ffn_w_up[i].astype(bf16), ffn_w_down[i].astype(bf16))
            if last:
                x2 = _final_norm(x2, _row(norm_final))
        else:
            x2 = _moe_layer(mix_args, _row(norm_ffn[l]), moe_w_router[i], moe_w_gate[i],
                            moe_w_up[i], moe_w_down[i], _row(norm_final), last)
    return x2.reshape(bsz, seq, D_MODEL)


def _final_norm_body(x_ref, g_ref, o_ref):
    o_ref[...] = _rms(x_ref[...], g_ref[...])


def _final_norm(x2, g):
    n = x2.shape[0]
    tm = TM_ROUTE
    return pl.pallas_call(
        _final_norm_body,
        grid=(n // tm,),
        in_specs=[pl.BlockSpec((tm, D_MODEL), lambda i: (i, 0)),
                  pl.BlockSpec((1, D_MODEL), lambda i: (0, 0))],
        out_specs=pl.BlockSpec((tm, D_MODEL), lambda i: (i, 0)),
        out_shape=SDS((n, D_MODEL), f32),
        compiler_params=_cparams("parallel"),
        name="final_norm",
    )(x2, g)
```

```python
import functools
import math

import jax
import jax.numpy as jnp
from jax import lax
from jax.experimental import pallas as pl
from jax.experimental.pallas import tpu as pltpu
from jax.experimental.pallas import tpu_sc as plsc

f32 = jnp.float32
bf16 = jnp.bfloat16
u32 = jnp.uint32
i32 = jnp.int32
SDS = jax.ShapeDtypeStruct

D_MODEL = 1024
D_S5 = 512
S5_GROUP = 16
S5_GROUPS = 32
S5_STATE = 64
N_STATE = S5_GROUPS * S5_STATE
D_GLA = 512
GLA_HEADS = 4
GLA_DV = 128
GLA_DK = 64
D_GLA_K = 256
GLA_GATE_RANK = 16
GLA_TAU = 16.0
GLA_CHUNK = 64
D_FF = 2816
N_EXPERTS = 8
D_FF_EXPERT = 3584
EPS = 1e-6

LANE = 128
SUBLANE = 8
VMEM_LIMIT = 56 * 1024 * 1024

TM_PROJ = 512
TT_S5 = 128
S5_SLABS = 4
S5_PITCH_PAD = 8
TG_GLA = 256
TM_FFN = 512
TM_MOE = 2048
TF_MOE = 512
TM_ROUTE = 512
SC_WINDOW = 128
SC_ROW = 256


def _cparams(*sem):
    return pltpu.CompilerParams(dimension_semantics=sem, vmem_limit_bytes=VMEM_LIMIT)


def _rms(x, g):
    ms = jnp.mean(x * x, axis=-1, keepdims=True)
    return x * lax.rsqrt(ms + EPS) * g


def _inproj_body(x_ref, g_ref, w_ref, wkt_ref, u_ref, q_ref, kt_ref, v_ref, r_ref, gl_ref):
    hn = _rms(x_ref[...], g_ref[...]).astype(bf16)

    def proj(lo, hi):
        return jnp.dot(hn, w_ref[:, lo:hi], preferred_element_type=f32).astype(bf16)

    u_ref[...] = proj(0, 512)
    q_ref[...] = proj(512, 768)
    v_ref[...] = proj(768, 1280)
    r_ref[...] = proj(1280, 1792)
    gl_ref[...] = proj(1792, 1920)
    kt_ref[...] = lax.dot_general(wkt_ref[...], hn, (((1,), (1,)), ((), ())),
                                  preferred_element_type=f32).astype(bf16)


def _inproj(x2, g, w_cat, w_kt, bsz, seq):
    n = x2.shape[0]
    tm = TM_PROJ
    per_b = seq // tm
    row = lambda i: (i, 0)
    const = lambda i: (0, 0)
    return pl.pallas_call(
        _inproj_body,
        grid=(n // tm,),
        in_specs=[pl.BlockSpec((tm, D_MODEL), row),
                  pl.BlockSpec((1, D_MODEL), const),
                  pl.BlockSpec(w_cat.shape, const),
                  pl.BlockSpec(w_kt.shape, const)],
        out_specs=[pl.BlockSpec((tm, D_S5), row),
                   pl.BlockSpec((tm, D_GLA_K), row),
                   pl.BlockSpec((None, D_GLA_K, tm), lambda i: (i // per_b, 0, i % per_b)),
                   pl.BlockSpec((tm, D_GLA), row),
                   pl.BlockSpec((tm, D_GLA), row),
                   pl.BlockSpec((tm, LANE), row)],
        out_shape=[SDS((n, D_S5), bf16), SDS((n, D_GLA_K), bf16),
                   SDS((bsz, D_GLA_K, seq), bf16), SDS((n, D_GLA), bf16),
                   SDS((n, D_GLA), bf16), SDS((n, LANE), bf16)],
        compiler_params=_cparams("parallel"),
        name="inproj",
    )(x2, g, w_cat, w_kt)


N_SLAB = N_STATE // LANE


def _s5_prep(lam_re, lam_im, log_dt, b_re, b_im, c_re, c_im):
    lr = jnp.minimum(lam_re.astype(f32), -1e-4)
    li = lam_im.astype(f32)
    dt = jnp.exp(log_dt.astype(f32))[:, None]
    mag = jnp.exp(lr * dt)
    ab_re = mag * jnp.cos(li * dt)
    ab_im = mag * jnp.sin(li * dt)
    nr = ab_re - 1.0
    ni = ab_im
    den = lr * lr + li * li
    f_re = (nr * lr + ni * li) / den
    f_im = (ni * lr - nr * li) / den
    br = b_re.astype(f32)
    bi = b_im.astype(f32)
    bb_re = f_re[..., None] * br - f_im[..., None] * bi
    bb_im = f_re[..., None] * bi + f_im[..., None] * br
    abb_re = ab_re[..., None] * bb_re - ab_im[..., None] * bb_im
    abb_im = ab_re[..., None] * bb_im + ab_im[..., None] * bb_re
    cr = c_re.astype(f32)
    ci = c_im.astype(f32)
    ca_re = cr * ab_re[:, None, :] - ci * ab_im[:, None, :]
    ca_im = cr * ab_im[:, None, :] + ci * ab_re[:, None, :]
    cb = jnp.einsum('gop,gpc->goc', cr, bb_re) - jnp.einsum('gop,gpc->goc', ci, bb_im)

    def b_tiles(bb):
        bt = jnp.transpose(bb, (0, 2, 1)).reshape(8, 4, S5_GROUP, S5_STATE)
        gl = jnp.arange(8)[None, :, None]
        gs = jnp.arange(4)[None, None, :]
        nn = jnp.arange(8)[:, None, None]
        sel = (gl == 4 * (nn % 2) + gs).astype(f32)
        t = jnp.einsum('ngs,nscp->ngcsp', sel, bt)
        return t.reshape(8, LANE, 4 * S5_STATE)

    bmat = jnp.concatenate(
        [jnp.concatenate([b_tiles(abb_re), b_tiles(bb_re)], axis=1),
         jnp.concatenate([b_tiles(abb_im), b_tiles(bb_im)], axis=1)], axis=0).astype(bf16)

    def c_tiles(c, sign):
        ct = jnp.transpose(c, (0, 2, 1)).reshape(2, 16, S5_STATE, S5_GROUP)
        eye = jnp.eye(16, dtype=f32)
        t = jnp.einsum('gh,jgpo->jgpho', eye, ct) * sign
        return t.reshape(2, 16 * S5_STATE, 16 * S5_GROUP)

    cmat = jnp.stack([c_tiles(cr, 1.0), c_tiles(ci, -1.0)], axis=1).astype(bf16)
    camat = jnp.stack([c_tiles(ca_re, 1.0), c_tiles(ca_im, -1.0)], axis=1).astype(bf16)
    cbt = jnp.transpose(cb, (0, 2, 1)).reshape(2, 16, S5_GROUP, S5_GROUP)
    cbmat = jnp.einsum('gh,jgco->jgcho', jnp.eye(16, dtype=f32), cbt).reshape(2, 256, 256).astype(bf16)
    a2_re = (ab_re * ab_re - ab_im * ab_im).reshape(N_SLAB, 1, LANE)
    a2_im = (2.0 * ab_re * ab_im).reshape(N_SLAB, 1, LANE)
    return bmat, cmat, camat, cbmat, a2_re, a2_im


def _s5_body(u_ref, bm_ref, cm_ref, cam_ref, cbm_ref, are_ref, aim_ref, d_ref, y_ref,
             hbuf, hstate, zcarry):
    tt = u_ref.shape[1]
    tp = tt // 2
    rows = SUBLANE * tp
    pitch = tp + S5_PITCH_PAD

    @pl.when(pl.program_id(0) == 0)
    def _():
        hstate[...] = jnp.zeros_like(hstate)
        zcarry[...] = jnp.zeros_like(zcarry)

    w = pltpu.bitcast(u_ref[...].reshape(SUBLANE * tt, D_S5), u32)
    ue_f = pltpu.bitcast(w << 16, f32)
    uo_f = pltpu.bitcast(w & jnp.uint32(0xFFFF0000), f32)
    ue = ue_f.astype(bf16)
    uo = uo_f.astype(bf16)

    def put(slab, val):
        for b in range(SUBLANE):
            hbuf[slab, b * pitch:b * pitch + tp, :] = val[b * tp:(b + 1) * tp, :]

    def get(slab):
        return jnp.concatenate([hbuf[slab, b * pitch:b * pitch + tp, :] for b in range(SUBLANE)], axis=0)

    for n in range(2 * 8):
        ks = LANE * ((n % 8) // 2)
        lhs = jnp.concatenate([ue[:, ks:ks + LANE], uo[:, ks:ks + LANE]], axis=1)
        res = jnp.dot(lhs, bm_ref[n], preferred_element_type=f32)
        put(2 * n, res[:, :LANE])
        put(2 * n + 1, res[:, LANE:])

    for c0 in range(0, N_SLAB, S5_SLABS):
        ar = [jnp.broadcast_to(are_ref[c0 + s], (SUBLANE, LANE)) for s in range(S5_SLABS)]
        ai = [jnp.broadcast_to(aim_ref[c0 + s], (SUBLANE, LANE)) for s in range(S5_SLABS)]

        def step(t, carry):
            out = []
            for s in range(S5_SLABS):
                hr, hi = carry[2 * s], carry[2 * s + 1]
                sl = pl.ds(t, SUBLANE, stride=pitch)
                bur = hbuf[c0 + s, sl, :]
                bui = hbuf[N_SLAB + c0 + s, sl, :]
                nr = ar[s] * hr - ai[s] * hi + bur
                ni = ar[s] * hi + ai[s] * hr + bui
                hbuf[c0 + s, sl, :] = nr
                hbuf[N_SLAB + c0 + s, sl, :] = ni
                out += [nr, ni]
            return tuple(out)

        init = []
        for s in range(S5_SLABS):
            init += [hstate[c0 + s], hstate[N_SLAB + c0 + s]]
        fin = lax.fori_loop(0, tp, step, tuple(init), unroll=4)
        for s in range(S5_SLABS):
            hstate[c0 + s] = fin[2 * s]
            hstate[N_SLAB + c0 + s] = fin[2 * s + 1]

    first = lax.broadcasted_iota(i32, (rows, 256), 0) % tp == 0
    for j in range(2):
        h_re = jnp.concatenate([get(8 * j + s) for s in range(8)], axis=1).astype(bf16)
        h_im = jnp.concatenate([get(N_SLAB + 8 * j + s) for s in range(8)], axis=1).astype(bf16)
        cs = slice(256 * j, 256 * j + 256)
        dj = d_ref[:, cs]
        yo = jnp.dot(h_re, cm_ref[j, 0], preferred_element_type=f32)
        yo = yo + jnp.dot(h_im, cm_ref[j, 1], preferred_element_type=f32)
        yo = yo + dj * uo_f[:, cs]
        z = jnp.dot(h_re, cam_ref[j, 0], preferred_element_type=f32)
        z = z + jnp.dot(h_im, cam_ref[j, 1], preferred_element_type=f32)
        prev = jnp.concatenate([jnp.broadcast_to(zcarry[b:b + 1, cs], (tp, 256))
                                for b in range(SUBLANE)], axis=0)
        zs = jnp.where(first, prev, pltpu.roll(z, 1, 0))
        zcarry[:, cs] = jnp.concatenate([z[(b + 1) * tp - 1:(b + 1) * tp, :] for b in range(SUBLANE)], axis=0)
        ye = zs + jnp.dot(ue[:, cs], cbm_ref[j], preferred_element_type=f32) + dj * ue_f[:, cs]
        ge = pltpu.bitcast(jax.nn.gelu(ye).astype(bf16).astype(f32), u32)
        go = pltpu.bitcast(jax.nn.gelu(yo).astype(bf16).astype(f32), u32)
        packed = (ge >> 16) | go
        y_ref[:, :, cs] = pltpu.bitcast(packed, bf16).reshape(SUBLANE, tt, 256)


def _s5(u3, bmat, cmat, camat, cbmat, a2_re, a2_im, d_skip):
    bsz, seq, _ = u3.shape
    assert bsz == SUBLANE
    tt = TT_S5
    blk = lambda i: (0, i, 0)
    c3 = lambda i: (0, 0, 0)
    c4 = lambda i: (0, 0, 0, 0)
    return pl.pallas_call(
        _s5_body,
        grid=(seq // tt,),
        in_specs=[pl.BlockSpec((bsz, tt, D_S5), blk),
                  pl.BlockSpec(bmat.shape, c3),
                  pl.BlockSpec(cmat.shape, c4),
                  pl.BlockSpec(camat.shape, c4),
                  pl.BlockSpec(cbmat.shape, c3),
                  pl.BlockSpec(a2_re.shape, c3),
                  pl.BlockSpec(a2_im.shape, c3),
                  pl.BlockSpec((1, D_S5), lambda i: (0, 0))],
        out_specs=pl.BlockSpec((bsz, tt, D_S5), blk),
        out_shape=SDS((bsz, seq, D_S5), bf16),
        scratch_shapes=[pltpu.VMEM((2 * N_SLAB, bsz * (tt // 2 + S5_PITCH_PAD), LANE), f32),
                        pltpu.VMEM((2 * N_SLAB, SUBLANE, LANE), f32),
                        pltpu.VMEM((SUBLANE, D_S5), f32)],
        compiler_params=_cparams("arbitrary"),
        name="s5_scan",
    )(u3, bmat, cmat, camat, cbmat, a2_re, a2_im, d_skip)


def _split2(x):
    hi = x.astype(bf16)
    lo = (x - hi.astype(f32)).astype(bf16)
    return hi, lo


def _log_sigmoid(x):
    return -(jnp.maximum(-x, 0.0) + jnp.log1p(jnp.exp(-jnp.abs(x))))


def _gla_body(q_ref, kt_ref, v_ref, r_ref, g_ref, wa_ref, wat_ref, ba_ref, bat_ref, gn_ref,
              o_ref, s_ref):
    c = GLA_CHUNK
    tg = q_ref.shape[0]

    @pl.when(pl.program_id(1) == 0)
    def _():
        s_ref[...] = jnp.zeros_like(s_ref)

    nc = tg // c
    nh = GLA_HEADS
    ri = lax.broadcasted_iota(i32, (tg, tg), 0)
    ci = lax.broadcasted_iota(i32, (tg, tg), 1)
    same = (ri // c) == (ci // c)
    tril = (same & (ri >= ci)).astype(bf16)
    triu = (same & (ri <= ci)).astype(bf16)
    blk = same.astype(bf16)

    g = g_ref[...]
    la = _log_sigmoid(jnp.dot(g, wa_ref[...], preferred_element_type=f32) + ba_ref[...]) / GLA_TAU
    lat = _log_sigmoid(lax.dot_general(wat_ref[...], g, (((1,), (1,)), ((), ())),
                                       preferred_element_type=f32) + bat_ref[...]) / GLA_TAU
    la_hi, la_lo = _split2(la)
    cum = (jnp.dot(tril, la_hi, preferred_element_type=f32)
           + jnp.dot(tril, la_lo, preferred_element_type=f32))
    lat_hi, lat_lo = _split2(lat)
    cumt = (jnp.dot(lat_hi, triu, preferred_element_type=f32)
            + jnp.dot(lat_lo, triu, preferred_element_type=f32))
    clt = (jnp.dot(lat_hi, blk, preferred_element_type=f32)
           + jnp.dot(lat_lo, blk, preferred_element_type=f32))

    qt = q_ref[...].astype(f32) * jnp.exp(cum)
    ktt = kt_ref[...].astype(f32)
    k_t = (ktt * jnp.exp(-cumt)).astype(bf16)
    k_end = ktt * jnp.exp(clt - cumt)
    v = v_ref[...]

    head_of_lane = lax.broadcasted_iota(i32, (c, D_GLA_K), 1) // GLA_DK
    q_stack = jnp.concatenate(
        [jnp.where(head_of_lane == h, qt[cc * c:(cc + 1) * c, :], 0.0)
         for cc in range(nc) for h in range(nh)], axis=0).astype(bf16)
    scores = jnp.dot(q_stack, k_t, preferred_element_type=f32)
    r_idx = lax.broadcasted_iota(i32, (nc * nh * c, tg), 0)
    c_idx = lax.broadcasted_iota(i32, (nc * nh * c, tg), 1)
    causal = (r_idx // (nh * c) == c_idx // c) & (r_idx % c >= c_idx % c)
    scores = jnp.where(causal, scores, 0.0).astype(bf16)
    o_full = jnp.dot(scores, v, preferred_element_type=f32)

    chunk_of_lane = lax.broadcasted_iota(i32, (D_GLA_K, tg), 1) // c
    k_stack = jnp.concatenate(
        [jnp.where(chunk_of_lane == cc, k_end, 0.0) for cc in range(nc)], axis=0).astype(bf16)
    upd_full = jnp.dot(k_stack, v, preferred_element_type=f32)

    state = s_ref[...]
    o_inter = []
    for cc in range(nc):
        o_inter.append(jnp.dot(q_stack[cc * nh * c:(cc + 1) * nh * c, :], state.astype(bf16),
                               preferred_element_type=f32))
        upd = jnp.concatenate(
            [upd_full[cc * D_GLA_K + h * GLA_DK:cc * D_GLA_K + (h + 1) * GLA_DK,
                      h * GLA_DV:(h + 1) * GLA_DV] for h in range(nh)], axis=0)
        state = jnp.exp(clt[:, cc * c:cc * c + 1]) * state + upd
    s_ref[...] = state

    rows = []
    for cc in range(nc):
        outs = []
        for h in range(nh):
            r0 = (cc * nh + h) * c
            o = o_full[r0:r0 + c, h * GLA_DV:(h + 1) * GLA_DV] + o_inter[cc][h * c:(h + 1) * c, :]
            o = o * lax.rsqrt(jnp.mean(o * o, axis=-1, keepdims=True) + EPS)
            outs.append(o)
        rows.append(jnp.concatenate(outs, axis=1))
    o_cat = jnp.concatenate(rows, axis=0)
    o_ref[...] = (o_cat * gn_ref[...] * jax.nn.silu(r_ref[...].astype(f32))).astype(bf16)


def _gla(q3, kt3, v3, r3, g3, wa, wat, ba, bat, gn):
    bsz, seq, _ = q3.shape
    tg = TG_GLA
    tok = lambda b, i: (b, i, 0)
    c2 = lambda b, i: (0, 0)
    return pl.pallas_call(
        _gla_body,
        grid=(bsz, seq // tg),
        in_specs=[pl.BlockSpec((None, tg, D_GLA_K), tok),
                  pl.BlockSpec((None, D_GLA_K, tg), lambda b, i: (b, 0, i)),
                  pl.BlockSpec((None, tg, D_GLA), tok),
                  pl.BlockSpec((None, tg, D_GLA), tok),
                  pl.BlockSpec((None, tg, LANE), tok),
                  pl.BlockSpec(wa.shape, c2), pl.BlockSpec(wat.shape, c2),
                  pl.BlockSpec(ba.shape, c2), pl.BlockSpec(bat.shape, c2),
                  pl.BlockSpec(gn.shape, c2)],
        out_specs=pl.BlockSpec((None, tg, D_GLA), tok),
        out_shape=SDS((bsz, seq, D_GLA), bf16),
        scratch_shapes=[pltpu.VMEM((D_GLA_K, GLA_DV), f32)],
        compiler_params=_cparams("parallel", "arbitrary"),
        name="gla",
    )(q3, kt3, v3, r3, g3, wa, wat, ba, bat, gn)


def _mix_out(x_ref, ys_ref, yg_ref, wgl_ref, bg_ref, gs_ref, wo_ref):
    y = ys_ref[...]
    z = jnp.dot(y, wgl_ref[...], preferred_element_type=f32) + bg_ref[...]
    yf = y.astype(f32) * jax.nn.sigmoid(z)
    ys = _rms(yf, gs_ref[...]).astype(bf16)
    acc = jnp.dot(ys, wo_ref[0:D_S5, :], preferred_element_type=f32)
    acc = acc + jnp.dot(yg_ref[...], wo_ref[D_S5:, :], preferred_element_type=f32)
    return x_ref[...] + acc


def _mix_specs(tm, w_glu, w_out):
    row = lambda i: (i, 0)
    const = lambda i: (0, 0)
    once = dict(pipeline_mode=pl.Buffered(1))
    return [pl.BlockSpec((tm, D_MODEL), row),
            pl.BlockSpec((tm, D_S5), row),
            pl.BlockSpec((tm, D_GLA), row),
            pl.BlockSpec(w_glu.shape, const, **once),
            pl.BlockSpec((1, D_S5), const),
            pl.BlockSpec((1, D_S5), const),
            pl.BlockSpec(w_out.shape, const, **once)]


def _mix_ffn_body(x_ref, ys_ref, yg_ref, wgl_ref, bg_ref, gs_ref, wo_ref,
                  g_ref, wg_ref, wu_ref, wd_ref, o_ref):
    x = _mix_out(x_ref, ys_ref, yg_ref, wgl_ref, bg_ref, gs_ref, wo_ref)
    hn = _rms(x, g_ref[...]).astype(bf16)
    gate = jnp.dot(hn, wg_ref[...], preferred_element_type=f32)
    up = jnp.dot(hn, wu_ref[...], preferred_element_type=f32)
    act = (jax.nn.silu(gate) * up).astype(bf16)
    o_ref[...] = x + jnp.dot(act, wd_ref[...], preferred_element_type=f32)


def _mix_ffn(x2, ys, yg, w_glu, b_glu, g_s5, w_out, g, wg, wu, wd):
    n = x2.shape[0]
    tm = TM_FFN
    row = lambda i: (i, 0)
    const = lambda i: (0, 0)
    once = dict(pipeline_mode=pl.Buffered(1))
    return pl.pallas_call(
        _mix_ffn_body,
        grid=(n // tm,),
        in_specs=_mix_specs(tm, w_glu, w_out) + [
            pl.BlockSpec((1, D_MODEL), const),
            pl.BlockSpec(wg.shape, const, **once),
            pl.BlockSpec(wu.shape, const, **once),
            pl.BlockSpec(wd.shape, const, **once)],
        out_specs=pl.BlockSpec((tm, D_MODEL), row),
        out_shape=SDS((n, D_MODEL), f32),
        compiler_params=_cparams("parallel"),
        name="mix_ffn",
    )(x2, ys, yg, w_glu, b_glu, g_s5, w_out, g, wg, wu, wd)


def _pack_bf16_pairs(a):
    bits = pltpu.bitcast(a.astype(bf16).astype(f32), u32)
    half = a.shape[1] // 2
    return bits[:, :half] | (bits[:, half:] >> 16)


def _unpack_bf16_pairs(p):
    hi = pltpu.bitcast(p & jnp.uint32(0xFFFF0000), f32).astype(bf16)
    lo = pltpu.bitcast(p << 16, f32).astype(bf16)
    return hi, lo


def _mix_router_body(x_ref, ys_ref, yg_ref, wgl_ref, bg_ref, gs_ref, wo_ref,
                     g_ref, wh_ref, wl_ref, xo_ref, hp_ref, meta_ref, cnt_ref, carry):
    tm = x_ref.shape[0]

    @pl.when(pl.program_id(0) == 0)
    def _():
        carry[...] = jnp.zeros_like(carry)

    x = _mix_out(x_ref, ys_ref, yg_ref, wgl_ref, bg_ref, gs_ref, wo_ref)
    xo_ref[...] = x
    hn = _rms(x, g_ref[...])
    packed = _pack_bf16_pairs(hn)
    hp_ref[0] = packed[:, :SC_ROW]
    hp_ref[1] = packed[:, SC_ROW:]

    h_hi, h_lo = _split2(hn)
    logits = (jnp.dot(h_hi, wh_ref[...], preferred_element_type=f32)
              + jnp.dot(h_lo, wh_ref[...], preferred_element_type=f32)
              + jnp.dot(h_hi, wl_ref[...], preferred_element_type=f32))
    lane = lax.broadcasted_iota(i32, (tm, LANE), 1)
    neg = jnp.float32(-jnp.inf)
    logits = jnp.where(lane < N_EXPERTS, logits, neg)
    m1 = jnp.max(logits, axis=-1, keepdims=True)
    i1 = jnp.min(jnp.where(logits == m1, lane, LANE), axis=-1, keepdims=True)
    l2 = jnp.where(lane == i1, neg, logits)
    m2 = jnp.max(l2, axis=-1, keepdims=True)
    i2 = jnp.min(jnp.where(l2 == m2, lane, LANE), axis=-1, keepdims=True)
    e21 = jnp.exp(m2 - m1)
    w1 = 1.0 / (1.0 + e21)
    w2 = e21 / (1.0 + e21)

    sel1 = lane == i1
    sel2 = lane == i2
    sel = (sel1 | sel2).astype(f32)
    ri = lax.broadcasted_iota(i32, (tm, tm), 0)
    ci = lax.broadcasted_iota(i32, (tm, tm), 1)
    tril = (ri >= ci).astype(bf16)
    incl = jnp.dot(tril, sel.astype(bf16), preferred_element_type=f32)
    rank = incl - sel + carry[0:1, :]
    r1 = jnp.sum(jnp.where(sel1, rank, 0.0), axis=-1, keepdims=True)
    r2 = jnp.sum(jnp.where(sel2, rank, 0.0), axis=-1, keepdims=True)
    new_cnt = carry[0:1, :] + incl[tm - 1:tm, :]
    carry[...] = jnp.broadcast_to(new_cnt, carry.shape)
    cnt_ref[...] = jnp.broadcast_to(new_cnt, cnt_ref.shape)

    meta = jnp.where(lane == 0, i1.astype(f32), 0.0)
    meta = jnp.where(lane == 1, i2.astype(f32), meta)
    meta = jnp.where(lane == 2, w1, meta)
    meta = jnp.where(lane == 3, w2, meta)
    meta = jnp.where(lane == 4, r1, meta)
    meta = jnp.where(lane == 5, r2, meta)
    meta_ref[...] = meta


def _mix_router(x2, ys, yg, w_glu, b_glu, g_s5, w_out, g, w_hi, w_lo):
    n = x2.shape[0]
    tm = TM_ROUTE
    row = lambda i: (i, 0)
    const = lambda i: (0, 0)
    return pl.pallas_call(
        _mix_router_body,
        grid=(n // tm,),
        in_specs=_mix_specs(tm, w_glu, w_out) + [
            pl.BlockSpec((1, D_MODEL), const),
            pl.BlockSpec(w_hi.shape, const),
            pl.BlockSpec(w_lo.shape, const)],
        out_specs=[pl.BlockSpec((tm, D_MODEL), row),
                   pl.BlockSpec((2, tm, SC_ROW), lambda i: (0, i, 0)),
                   pl.BlockSpec((tm, LANE), row),
                   pl.BlockSpec((SUBLANE, LANE), const)],
        out_shape=[SDS((n, D_MODEL), f32), SDS((2, n, SC_ROW), u32), SDS((n, LANE), f32),
                   SDS((SUBLANE, LANE), f32)],
        scratch_shapes=[pltpu.VMEM((SUBLANE, LANE), f32)],
        compiler_params=_cparams("arbitrary"),
        name="mix_router",
    )(x2, ys, yg, w_glu, b_glu, g_s5, w_out, g, w_hi, w_lo)


def _sc_gather(table, idx):
    ni = idx.shape[0]
    mesh = plsc.VectorSubcoreMesh(core_axis_name="core", subcore_axis_name="subcore")
    idx2 = idx.reshape(1, ni)

    @pl.kernel(out_type=SDS((ni, SC_ROW), table.dtype), mesh=mesh)
    def kern(t_hbm, i_hbm, o_hbm):
        def body(i_vmem, o_vmem):
            pltpu.sync_copy(t_hbm.at[i_vmem.at[0]], o_vmem)

        pltpu.emit_pipeline(
            body, grid=(ni // SC_WINDOW,),
            in_specs=[pl.BlockSpec((1, SC_WINDOW), index_map=lambda i: (0, i))],
            out_specs=[pl.BlockSpec((SC_WINDOW, SC_ROW), index_map=lambda i: (i, 0))],
            core_axis_name=("core", "subcore"),
            dimension_semantics=(pltpu.PARALLEL,),
        )(i_hbm, o_hbm)

    return kern(table, idx2)


def _sc_scatter2(x, idx0, idx1, nrows):
    ni = x.shape[0]
    mesh = plsc.VectorSubcoreMesh(core_axis_name="core", subcore_axis_name="subcore")

    @pl.kernel(out_type=SDS((nrows, SC_ROW), x.dtype), mesh=mesh)
    def kern(x_hbm, i0_hbm, i1_hbm, o_hbm):
        def body(x_vmem, i0_vmem, i1_vmem):
            pltpu.sync_copy(x_vmem, o_hbm.at[i0_vmem.at[0]])
            pltpu.sync_copy(x_vmem, o_hbm.at[i1_vmem.at[0]])

        pltpu.emit_pipeline(
            body, grid=(ni // SC_WINDOW,),
            in_specs=[pl.BlockSpec((SC_WINDOW, SC_ROW), index_map=lambda i: (i, 0)),
                      pl.BlockSpec((1, SC_WINDOW), index_map=lambda i: (0, i)),
                      pl.BlockSpec((1, SC_WINDOW), index_map=lambda i: (0, i))],
            out_specs=[],
            core_axis_name=("core", "subcore"),
            dimension_semantics=(pltpu.PARALLEL,),
        )(x_hbm, i0_hbm, i1_hbm)

    return kern(x, idx0.reshape(1, ni), idx1.reshape(1, ni))


def _moe_ffn_body(be_ref, bn_ref, xs_ref, wg_ref, wu_ref, wd_ref, ys_ref, xb, act0, act1, acc):
    i = pl.program_id(0)
    f = pl.program_id(1)
    nf = pl.num_programs(1) - 1
    nvalid = bn_ref[i]
    half = xs_ref.shape[1] // 2

    for h in range(2):
        rs = pl.ds(h * half, half)
        live_half = nvalid > h * half

        def gate_up(rs=rs):
            x = xb[rs, :]
            gate = jnp.dot(x, wg_ref[...].astype(bf16), preferred_element_type=f32)
            up = jnp.dot(x, wu_ref[...].astype(bf16), preferred_element_type=f32)
            return (jax.nn.silu(gate) * up).astype(bf16)

        def down(act_ref, rs=rs):
            return jnp.dot(act_ref[rs, :], wd_ref[...].astype(bf16), preferred_element_type=f32)

        @pl.when(live_half & (f == 0))
        def _(rs=rs, h=h, gate_up=gate_up):
            live = lax.broadcasted_iota(i32, (half, SC_ROW), 0) + h * half < nvalid
            hi0, lo0 = _unpack_bf16_pairs(jnp.where(live, xs_ref[0, rs, :], jnp.uint32(0)))
            hi1, lo1 = _unpack_bf16_pairs(jnp.where(live, xs_ref[1, rs, :], jnp.uint32(0)))
            xb[rs, :] = jnp.concatenate([hi0, hi1, lo0, lo1], axis=1)
            acc[rs, :] = jnp.zeros((half, D_MODEL), f32)
            act0[rs, :] = gate_up()

        @pl.when(live_half & (f > 0) & (f < nf) & (f % 2 == 1))
        def _(rs=rs, gate_up=gate_up, down=down):
            act1[rs, :] = gate_up()
            acc[rs, :] += down(act0)

        @pl.when(live_half & (f > 0) & (f < nf) & (f % 2 == 0))
        def _(rs=rs, gate_up=gate_up, down=down):
            act0[rs, :] = gate_up()
            acc[rs, :] += down(act1)

        @pl.when(live_half & (f == nf))
        def _(rs=rs, down=down):
            last = act0 if (D_FF_EXPERT // TF_MOE - 1) % 2 == 0 else act1
            packed = _pack_bf16_pairs(acc[rs, :] + down(last))
            ys_ref[0, rs, :] = packed[:, :SC_ROW]
            ys_ref[1, rs, :] = packed[:, SC_ROW:]

        @pl.when(jnp.logical_not(live_half) & (f == nf))
        def _(rs=rs):
            ys_ref[:, rs, :] = jnp.zeros((2, half, SC_ROW), u32)


def _moe_ffn(blk_e, blk_n, xs, wg, wu, wd):
    npad = xs.shape[1]
    tm, tf = TM_MOE, TF_MOE
    nblk = npad // tm
    nf = D_FF_EXPERT // tf

    def tile(i, f, bn):
        return jnp.where(bn[i] > 0, jnp.clip(f, 0, nf - 1), nf - 1)

    grid_spec = pltpu.PrefetchScalarGridSpec(
        num_scalar_prefetch=2,
        grid=(nblk, nf + 1),
        in_specs=[pl.BlockSpec((2, tm, SC_ROW), lambda i, f, be, bn: (0, i, 0)),
                  pl.BlockSpec((None, D_MODEL, tf), lambda i, f, be, bn: (be[i], 0, tile(i, f, bn))),
                  pl.BlockSpec((None, D_MODEL, tf), lambda i, f, be, bn: (be[i], 0, tile(i, f, bn))),
                  pl.BlockSpec((None, tf, D_MODEL), lambda i, f, be, bn: (be[i], tile(i, f - 1, bn), 0))],
        out_specs=pl.BlockSpec((2, tm, SC_ROW), lambda i, f, be, bn: (0, i, 0)),
        scratch_shapes=[pltpu.VMEM((tm, D_MODEL), bf16), pltpu.VMEM((tm, tf), bf16),
                        pltpu.VMEM((tm, tf), bf16), pltpu.VMEM((tm, D_MODEL), f32)],
    )
    return pl.pallas_call(
        _moe_ffn_body,
        grid_spec=grid_spec,
        out_shape=SDS((2, npad, SC_ROW), u32),
        compiler_params=_cparams("parallel", "arbitrary"),
        name="moe_ffn",
    )(blk_e, blk_n, xs, wg, wu, wd)


def _combine_body(x_ref, yg_ref, meta_ref, g_ref, o_ref, *, final_norm):
    w1 = meta_ref[:, 2:3]
    w2 = meta_ref[:, 3:4]

    def rows(k):
        hi0, lo0 = _unpack_bf16_pairs(yg_ref[0, k])
        hi1, lo1 = _unpack_bf16_pairs(yg_ref[1, k])
        return jnp.concatenate([hi0, hi1, lo0, lo1], axis=1).astype(f32)

    xo = x_ref[...] + (w1 * rows(0) + w2 * rows(1))
    if final_norm:
        xo = _rms(xo, g_ref[...])
    o_ref[...] = xo


def _combine(x2, yg, meta, g_final, final_norm):
    n = x2.shape[0]
    tm = TM_ROUTE
    row = lambda i: (i, 0)
    return pl.pallas_call(
        functools.partial(_combine_body, final_norm=final_norm),
        grid=(n // tm,),
        in_specs=[pl.BlockSpec((tm, D_MODEL), row),
                  pl.BlockSpec((2, 2, tm, SC_ROW), lambda i: (0, 0, i, 0)),
                  pl.BlockSpec((tm, LANE), row),
                  pl.BlockSpec((1, D_MODEL), lambda i: (0, 0))],
        out_specs=pl.BlockSpec((tm, D_MODEL), row),
        out_shape=SDS((n, D_MODEL), f32),
        compiler_params=_cparams("parallel"),
        name="moe_combine",
    )(x2, yg, meta, g_final)


def _moe_layer(mix_args, g_ffn, w_router, wg, wu, wd, g_final, final_norm):
    n = mix_args[0].shape[0]
    tm = TM_MOE
    npad = 2 * n + N_EXPERTS * tm
    wr = jnp.zeros((D_MODEL, LANE), f32).at[:, :N_EXPERTS].set(w_router.astype(f32))
    wr_hi = wr.astype(bf16)
    wr_lo = (wr - wr_hi.astype(f32)).astype(bf16)
    x2, hp, meta, cnt = _mix_router(*mix_args, g_ffn, wr_hi, wr_lo)

    counts = cnt[0, :N_EXPERTS].astype(i32)
    padded = ((counts + tm - 1) // tm) * tm
    ends = jnp.cumsum(padded)
    offs = ends - padded
    ids = jnp.arange(N_EXPERTS, dtype=i32)
    e12 = meta[:, 0:2].astype(i32).T
    r12 = meta[:, 4:6].astype(i32).T
    pos_t = r12 + jnp.sum(jnp.where(e12[..., None] == ids, offs, 0), axis=-1)
    blk_start = jnp.arange(npad // tm, dtype=i32) * tm
    blk_e = jnp.minimum(jnp.sum((blk_start[:, None] >= ends[None, :]).astype(i32), axis=1), N_EXPERTS - 1)
    own = blk_e[:, None] == ids
    blk_end = jnp.sum(jnp.where(own, offs + counts, 0), axis=1)
    blk_n = jnp.where(blk_start < ends[-1], jnp.clip(blk_end - blk_start, 0, tm), 0)
    last_e = jnp.max(jnp.where(blk_n > 0, blk_e, 0))
    blk_e = jnp.where(blk_n > 0, blk_e, last_e)

    xs = _sc_scatter2(hp.reshape(2 * n, SC_ROW),
                      jnp.concatenate([pos_t[0], pos_t[0] + npad]),
                      jnp.concatenate([pos_t[1], pos_t[1] + npad]),
                      2 * npad).reshape(2, npad, SC_ROW)
    ys = _moe_ffn(blk_e, blk_n, xs, wg, wu, wd)
    gidx = jnp.concatenate([pos_t.reshape(-1), pos_t.reshape(-1) + npad])
    yg = _sc_gather(ys.reshape(2 * npad, SC_ROW), gidx).reshape(2, 2, n, SC_ROW)
    return _combine(x2, yg, meta, g_final, final_norm)


def _row(v):
    return v.astype(f32).reshape(1, -1)


def kernel(x, norm_mix, w_in, s5_lambda_re, s5_lambda_im, s5_log_dt, s5_b_re, s5_b_im, s5_c_re, s5_c_im, s5_d, s5_w_glu, s5_b_glu, s5_out_norm, gla_w_a2, gla_b_a2, gla_out_norm, w_out, norm_ffn, ffn_w_gate, ffn_w_up, ffn_w_down, moe_w_router, moe_w_gate, moe_w_up, moe_w_down, norm_final):
    bsz, seq, _ = x.shape
    n = bsz * seq
    depth = w_in.shape[0]
    x2 = x.reshape(n, D_MODEL)
    for l in range(depth):
        wi = w_in[l]
        w_u, w_q, w_k, w_v, w_r, w_g = (wi[:, 0:512], wi[:, 512:768], wi[:, 768:1024],
                                        wi[:, 1024:1536], wi[:, 1536:2048], wi[:, 2048:2064])
        w_gp = jnp.zeros((D_MODEL, LANE), f32).at[:, :GLA_GATE_RANK].set(w_g)
        w_cat = jnp.concatenate([w_u, w_q * (GLA_DK ** -0.5), w_v, w_r, w_gp], axis=1).astype(bf16)
        w_kt = w_k.T.astype(bf16)
        u, q, kt, v, r, gl = _inproj(x2, _row(norm_mix[l]), w_cat, w_kt, bsz, seq)

        s5_mats = _s5_prep(s5_lambda_re[l], s5_lambda_im[l], s5_log_dt[l],
                           s5_b_re[l], s5_b_im[l], s5_c_re[l], s5_c_im[l])
        ys = _s5(u.reshape(bsz, seq, D_S5), *s5_mats, _row(s5_d[l]))

        wa = jnp.zeros((LANE, D_GLA_K), f32).at[:GLA_GATE_RANK].set(gla_w_a2[l]).astype(bf16)
        yg = _gla(q.reshape(bsz, seq, D_GLA_K), kt, v.reshape(bsz, seq, D_GLA),
                  r.reshape(bsz, seq, D_GLA), gl.reshape(bsz, seq, LANE),
                  wa, wa.T, _row(gla_b_a2[l]), gla_b_a2[l].astype(f32).reshape(-1, 1),
                  _row(gla_out_norm[l]))

        mix_args = (x2, ys.reshape(n, D_S5), yg.reshape(n, D_GLA), s5_w_glu[l].astype(bf16),
                    _row(s5_b_glu[l]), _row(s5_out_norm[l]), w_out[l].astype(bf16))

        i = l // 2
        last = l == depth - 1
        if l % 2 == 0:
            x2 = _mix_ffn(*mix_args, _row(norm_ffn[l]), ffn_w_gate[i].astype(bf16),
                          ffn_w_up[i].astype(bf16), ffn_w_down[i].astype(bf16))
            if last:
                x2 = _final_norm(x2, _row(norm_final))
        else:
            x2 = _moe_layer(mix_args, _row(norm_ffn[l]), moe_w_router[i], moe_w_gate[i],
                            moe_w_up[i], moe_w_down[i], _row(norm_final), last)
    return x2.reshape(bsz, seq, D_MODEL)


def _final_norm_body(x_ref, g_ref, o_ref):
    o_ref[...] = _rms(x_ref[...], g_ref[...])


def _final_norm(x2, g):
    n = x2.shape[0]
    tm = TM_ROUTE
    return pl.pallas_call(
        _final_norm_body,
        grid=(n // tm,),
        in_specs=[pl.BlockSpec((tm, D_MODEL), lambda i: (i, 0)),
                  pl.BlockSpec((1, D_MODEL), lambda i: (0, 0))],
        out_specs=pl.BlockSpec((tm, D_MODEL), lambda i: (i, 0)),
        out_shape=SDS((n, D_MODEL), f32),
        compiler_params=_cparams("parallel"),
        name="final_norm",
    )(x2, g)
```

```python
import functools
import math

import jax
import jax.numpy as jnp
from jax import lax
from jax.experimental import pallas as pl
from jax.experimental.pallas import tpu as pltpu
from jax.experimental.pallas import tpu_sc as plsc

f32 = jnp.float32
bf16 = jnp.bfloat16
u32 = jnp.uint32
i32 = jnp.int32
SDS = jax.ShapeDtypeStruct

D_MODEL = 1024
D_S5 = 512
S5_GROUP = 16
S5_GROUPS = 32
S5_STATE = 64
N_STATE = S5_GROUPS * S5_STATE
D_GLA = 512
GLA_HEADS = 4
GLA_DV = 128
GLA_DK = 64
D_GLA_K = 256
GLA_GATE_RANK = 16
GLA_TAU = 16.0
GLA_CHUNK = 64
D_FF = 2816
N_EXPERTS = 8
D_FF_EXPERT = 3584
EPS = 1e-6

LANE = 128
SUBLANE = 8
VMEM_LIMIT = 56 * 1024 * 1024

TM_PROJ = 512
TT_S5 = 128
S5_SLABS = 4
S5_PITCH_PAD = 8
TG_GLA = 256
TM_FFN = 512
TM_MOE = 2048
TF_MOE = 512
TM_ROUTE = 512
SC_WINDOW = 128
SC_ROW = 256


def _cparams(*sem):
    return pltpu.CompilerParams(dimension_semantics=sem, vmem_limit_bytes=VMEM_LIMIT)


def _rms(x, g):
    ms = jnp.mean(x * x, axis=-1, keepdims=True)
    return x * lax.rsqrt(ms + EPS) * g


def _inproj_body(x_ref, g_ref, w_ref, wkt_ref, u_ref, q_ref, kt_ref, v_ref, r_ref, gl_ref):
    hn = _rms(x_ref[...], g_ref[...]).astype(bf16)

    def proj(lo, hi):
        return jnp.dot(hn, w_ref[:, lo:hi], preferred_element_type=f32).astype(bf16)

    u_ref[...] = proj(0, 512)
    q_ref[...] = proj(512, 768)
    v_ref[...] = proj(768, 1280)
    r_ref[...] = proj(1280, 1792)
    gl_ref[...] = proj(1792, 1920)
    kt_ref[...] = lax.dot_general(wkt_ref[...], hn, (((1,), (1,)), ((), ())),
                                  preferred_element_type=f32).astype(bf16)


def _inproj(x2, g, w_cat, w_kt, bsz, seq):
    n = x2.shape[0]
    tm = TM_PROJ
    per_b = seq // tm
    row = lambda i: (i, 0)
    const = lambda i: (0, 0)
    return pl.pallas_call(
        _inproj_body,
        grid=(n // tm,),
        in_specs=[pl.BlockSpec((tm, D_MODEL), row),
                  pl.BlockSpec((1, D_MODEL), const),
                  pl.BlockSpec(w_cat.shape, const),
                  pl.BlockSpec(w_kt.shape, const)],
        out_specs=[pl.BlockSpec((tm, D_S5), row),
                   pl.BlockSpec((tm, D_GLA_K), row),
                   pl.BlockSpec((None, D_GLA_K, tm), lambda i: (i // per_b, 0, i % per_b)),
                   pl.BlockSpec((tm, D_GLA), row),
                   pl.BlockSpec((tm, D_GLA), row),
                   pl.BlockSpec((tm, LANE), row)],
        out_shape=[SDS((n, D_S5), bf16), SDS((n, D_GLA_K), bf16),
                   SDS((bsz, D_GLA_K, seq), bf16), SDS((n, D_GLA), bf16),
                   SDS((n, D_GLA), bf16), SDS((n, LANE), bf16)],
        compiler_params=_cparams("parallel"),
        name="inproj",
    )(x2, g, w_cat, w_kt)


N_SLAB = N_STATE // LANE


def _s5_prep(lam_re, lam_im, log_dt, b_re, b_im, c_re, c_im):
    lr = jnp.minimum(lam_re.astype(f32), -1e-4)
    li = lam_im.astype(f32)
    dt = jnp.exp(log_dt.astype(f32))[:, None]
    mag = jnp.exp(lr * dt)
    ab_re = mag * jnp.cos(li * dt)
    ab_im = mag * jnp.sin(li * dt)
    nr = ab_re - 1.0
    ni = ab_im
    den = lr * lr + li * li
    f_re = (nr * lr + ni * li) / den
    f_im = (ni * lr - nr * li) / den
    br = b_re.astype(f32)
    bi = b_im.astype(f32)
    bb_re = f_re[..., None] * br - f_im[..., None] * bi
    bb_im = f_re[..., None] * bi + f_im[..., None] * br
    abb_re = ab_re[..., None] * bb_re - ab_im[..., None] * bb_im
    abb_im = ab_re[..., None] * bb_im + ab_im[..., None] * bb_re
    cr = c_re.astype(f32)
    ci = c_im.astype(f32)
    ca_re = cr * ab_re[:, None, :] - ci * ab_im[:, None, :]
    ca_im = cr * ab_im[:, None, :] + ci * ab_re[:, None, :]
    cb = jnp.einsum('gop,gpc->goc', cr, bb_re) - jnp.einsum('gop,gpc->goc', ci, bb_im)

    def b_tiles(bb):
        bt = jnp.transpose(bb, (0, 2, 1)).reshape(8, 4, S5_GROUP, S5_STATE)
        gl = jnp.arange(8)[None, :, None]
        gs = jnp.arange(4)[None, None, :]
        nn = jnp.arange(8)[:, None, None]
        sel = (gl == 4 * (nn % 2) + gs).astype(f32)
        t = jnp.einsum('ngs,nscp->ngcsp', sel, bt)
        return t.reshape(8, LANE, 4 * S5_STATE)

    bmat = jnp.concatenate(
        [jnp.concatenate([b_tiles(abb_re), b_tiles(bb_re)], axis=1),
         jnp.concatenate([b_tiles(abb_im), b_tiles(bb_im)], axis=1)], axis=0).astype(bf16)

    def c_tiles(c, sign):
        ct = jnp.transpose(c, (0, 2, 1)).reshape(2, 16, S5_STATE, S5_GROUP)
        eye = jnp.eye(16, dtype=f32)
        t = jnp.einsum('gh,jgpo->jgpho', eye, ct) * sign
        return t.reshape(2, 16 * S5_STATE, 16 * S5_GROUP)

    cmat = jnp.stack([c_tiles(cr, 1.0), c_tiles(ci, -1.0)], axis=1).astype(bf16)
    camat = jnp.stack([c_tiles(ca_re, 1.0), c_tiles(ca_im, -1.0)], axis=1).astype(bf16)
    cbt = jnp.transpose(cb, (0, 2, 1)).reshape(2, 16, S5_GROUP, S5_GROUP)
    cbmat = jnp.einsum('gh,jgco->jgcho', jnp.eye(16, dtype=f32), cbt).reshape(2, 256, 256).astype(bf16)
    a2_re = (ab_re * ab_re - ab_im * ab_im).reshape(N_SLAB, 1, LANE)
    a2_im = (2.0 * ab_re * ab_im).reshape(N_SLAB, 1, LANE)
    return bmat, cmat, camat, cbmat, a2_re, a2_im


def _s5_body(u_ref, bm_ref, cm_ref, cam_ref, cbm_ref, are_ref, aim_ref, d_ref, y_ref,
             hbuf, hstate, zcarry):
    tt = u_ref.shape[1]
    tp = tt // 2
    rows = SUBLANE * tp
    pitch = tp + S5_PITCH_PAD

    @pl.when(pl.program_id(0) == 0)
    def _():
        hstate[...] = jnp.zeros_like(hstate)
        zcarry[...] = jnp.zeros_like(zcarry)

    w = pltpu.bitcast(u_ref[...].reshape(SUBLANE * tt, D_S5), u32)
    ue_f = pltpu.bitcast(w << 16, f32)
    uo_f = pltpu.bitcast(w & jnp.uint32(0xFFFF0000), f32)
    ue = ue_f.astype(bf16)
    uo = uo_f.astype(bf16)

    def put(slab, val):
        for b in range(SUBLANE):
            hbuf[slab, b * pitch:b * pitch + tp, :] = val[b * tp:(b + 1) * tp, :]

    def get(slab):
        return jnp.concatenate([hbuf[slab, b * pitch:b * pitch + tp, :] for b in range(SUBLANE)], axis=0)

    for n in range(2 * 8):
        ks = LANE * ((n % 8) // 2)
        lhs = jnp.concatenate([ue[:, ks:ks + LANE], uo[:, ks:ks + LANE]], axis=1)
        res = jnp.dot(lhs, bm_ref[n], preferred_element_type=f32)
        put(2 * n, res[:, :LANE])
        put(2 * n + 1, res[:, LANE:])

    for c0 in range(0, N_SLAB, S5_SLABS):
        ar = [jnp.broadcast_to(are_ref[c0 + s], (SUBLANE, LANE)) for s in range(S5_SLABS)]
        ai = [jnp.broadcast_to(aim_ref[c0 + s], (SUBLANE, LANE)) for s in range(S5_SLABS)]

        def step(t, carry):
            out = []
            for s in range(S5_SLABS):
                hr, hi = carry[2 * s], carry[2 * s + 1]
                sl = pl.ds(t, SUBLANE, stride=pitch)
                bur = hbuf[c0 + s, sl, :]
                bui = hbuf[N_SLAB + c0 + s, sl, :]
                nr = ar[s] * hr - ai[s] * hi + bur
                ni = ar[s] * hi + ai[s] * hr + bui
                hbuf[c0 + s, sl, :] = nr
                hbuf[N_SLAB + c0 + s, sl, :] = ni
                out += [nr, ni]
            return tuple(out)

        init = []
        for s in range(S5_SLABS):
            init += [hstate[c0 + s], hstate[N_SLAB + c0 + s]]
        fin = lax.fori_loop(0, tp, step, tuple(init), unroll=4)
        for s in range(S5_SLABS):
            hstate[c0 + s] = fin[2 * s]
            hstate[N_SLAB + c0 + s] = fin[2 * s + 1]

    first = lax.broadcasted_iota(i32, (rows, 256), 0) % tp == 0
    for j in range(2):
        h_re = jnp.concatenate([get(8 * j + s) for s in range(8)], axis=1).astype(bf16)
        h_im = jnp.concatenate([get(N_SLAB + 8 * j + s) for s in range(8)], axis=1).astype(bf16)
        cs = slice(256 * j, 256 * j + 256)
        dj = d_ref[:, cs]
        yo = jnp.dot(h_re, cm_ref[j, 0], preferred_element_type=f32)
        yo = yo + jnp.dot(h_im, cm_ref[j, 1], preferred_element_type=f32)
        yo = yo + dj * uo_f[:, cs]
        z = jnp.dot(h_re, cam_ref[j, 0], preferred_element_type=f32)
        z = z + jnp.dot(h_im, cam_ref[j, 1], preferred_element_type=f32)
        prev = jnp.concatenate([jnp.broadcast_to(zcarry[b:b + 1, cs], (tp, 256))
                                for b in range(SUBLANE)], axis=0)
        zs = jnp.where(first, prev, pltpu.roll(z, 1, 0))
        zcarry[:, cs] = jnp.concatenate([z[(b + 1) * tp - 1:(b + 1) * tp, :] for b in range(SUBLANE)], axis=0)
        ye = zs + jnp.dot(ue[:, cs], cbm_ref[j], preferred_element_type=f32) + dj * ue_f[:, cs]
        ge = pltpu.bitcast(jax.nn.gelu(ye).astype(bf16).astype(f32), u32)
        go = pltpu.bitcast(jax.nn.gelu(yo).astype(bf16).astype(f32), u32)
        packed = (ge >> 16) | go
        y_ref[:, :, cs] = pltpu.bitcast(packed, bf16).reshape(SUBLANE, tt, 256)


def _s5(u3, bmat, cmat, camat, cbmat, a2_re, a2_im, d_skip):
    bsz, seq, _ = u3.shape
    assert bsz == SUBLANE
    tt = TT_S5
    blk = lambda i: (0, i, 0)
    c3 = lambda i: (0, 0, 0)
    c4 = lambda i: (0, 0, 0, 0)
    return pl.pallas_call(
        _s5_body,
        grid=(seq // tt,),
        in_specs=[pl.BlockSpec((bsz, tt, D_S5), blk),
                  pl.BlockSpec(bmat.shape, c3),
                  pl.BlockSpec(cmat.shape, c4),
                  pl.BlockSpec(camat.shape, c4),
                  pl.BlockSpec(cbmat.shape, c3),
                  pl.BlockSpec(a2_re.shape, c3),
                  pl.BlockSpec(a2_im.shape, c3),
                  pl.BlockSpec((1, D_S5), lambda i: (0, 0))],
        out_specs=pl.BlockSpec((bsz, tt, D_S5), blk),
        out_shape=SDS((bsz, seq, D_S5), bf16),
        scratch_shapes=[pltpu.VMEM((2 * N_SLAB, bsz * (tt // 2 + S5_PITCH_PAD), LANE), f32),
                        pltpu.VMEM((2 * N_SLAB, SUBLANE, LANE), f32),
                        pltpu.VMEM((SUBLANE, D_S5), f32)],
        compiler_params=_cparams("arbitrary"),
        name="s5_scan",
    )(u3, bmat, cmat, camat, cbmat, a2_re, a2_im, d_skip)


def _split2(x):
    hi = x.astype(bf16)
    lo = (x - hi.astype(f32)).astype(bf16)
    return hi, lo


def _log_sigmoid(x):
    return -(jnp.maximum(-x, 0.0) + jnp.log1p(jnp.exp(-jnp.abs(x))))


def _gla_body(q_ref, kt_ref, v_ref, r_ref, g_ref, wa_ref, wat_ref, ba_ref, bat_ref, gn_ref,
              o_ref, s_ref):
    c = GLA_CHUNK
    tg = q_ref.shape[0]

    @pl.when(pl.program_id(1) == 0)
    def _():
        s_ref[...] = jnp.zeros_like(s_ref)

    nc = tg // c
    nh = GLA_HEADS
    ri = lax.broadcasted_iota(i32, (tg, tg), 0)
    ci = lax.broadcasted_iota(i32, (tg, tg), 1)
    same = (ri // c) == (ci // c)
    tril = (same & (ri >= ci)).astype(bf16)
    triu = (same & (ri <= ci)).astype(bf16)
    blk = same.astype(bf16)

    g = g_ref[...]
    la = _log_sigmoid(jnp.dot(g, wa_ref[...], preferred_element_type=f32) + ba_ref[...]) / GLA_TAU
    lat = _log_sigmoid(lax.dot_general(wat_ref[...], g, (((1,), (1,)), ((), ())),
                                       preferred_element_type=f32) + bat_ref[...]) / GLA_TAU
    la_hi, la_lo = _split2(la)
    cum = (jnp.dot(tril, la_hi, preferred_element_type=f32)
           + jnp.dot(tril, la_lo, preferred_element_type=f32))
    lat_hi, lat_lo = _split2(lat)
    cumt = (jnp.dot(lat_hi, triu, preferred_element_type=f32)
            + jnp.dot(lat_lo, triu, preferred_element_type=f32))
    clt = (jnp.dot(lat_hi, blk, preferred_element_type=f32)
           + jnp.dot(lat_lo, blk, preferred_element_type=f32))

    qt = q_ref[...].astype(f32) * jnp.exp(cum)
    ktt = kt_ref[...].astype(f32)
    k_t = (ktt * jnp.exp(-cumt)).astype(bf16)
    k_end = ktt * jnp.exp(clt - cumt)
    v = v_ref[...]

    head_of_lane = lax.broadcasted_iota(i32, (c, D_GLA_K), 1) // GLA_DK
    q_stack = jnp.concatenate(
        [jnp.where(head_of_lane == h, qt[cc * c:(cc + 1) * c, :], 0.0)
         for cc in range(nc) for h in range(nh)], axis=0).astype(bf16)
    scores = jnp.dot(q_stack, k_t, preferred_element_type=f32)
    r_idx = lax.broadcasted_iota(i32, (nc * nh * c, tg), 0)
    c_idx = lax.broadcasted_iota(i32, (nc * nh * c, tg), 1)
    causal = (r_idx // (nh * c) == c_idx // c) & (r_idx % c >= c_idx % c)
    scores = jnp.where(causal, scores, 0.0).astype(bf16)
    o_full = jnp.dot(scores, v, preferred_element_type=f32)

    chunk_of_lane = lax.broadcasted_iota(i32, (D_GLA_K, tg), 1) // c
    k_stack = jnp.concatenate(
        [jnp.where(chunk_of_lane == cc, k_end, 0.0) for cc in range(nc)], axis=0).astype(bf16)
    upd_full = jnp.dot(k_stack, v, preferred_element_type=f32)

    state = s_ref[...]
    o_inter = []
    for cc in range(nc):
        o_inter.append(jnp.dot(q_stack[cc * nh * c:(cc + 1) * nh * c, :], state.astype(bf16),
                               preferred_element_type=f32))
        upd = jnp.concatenate(
            [upd_full[cc * D_GLA_K + h * GLA_DK:cc * D_GLA_K + (h + 1) * GLA_DK,
                      h * GLA_DV:(h + 1) * GLA_DV] for h in range(nh)], axis=0)
        state = jnp.exp(clt[:, cc * c:cc * c + 1]) * state + upd
    s_ref[...] = state

    rows = []
    for cc in range(nc):
        outs = []
        for h in range(nh):
            r0 = (cc * nh + h) * c
            o = o_full[r0:r0 + c, h * GLA_DV:(h + 1) * GLA_DV] + o_inter[cc][h * c:(h + 1) * c, :]
            o = o * lax.rsqrt(jnp.mean(o * o, axis=-1, keepdims=True) + EPS)
            outs.append(o)
        rows.append(jnp.concatenate(outs, axis=1))
    o_cat = jnp.concatenate(rows, axis=0)
    o_ref[...] = (o_cat * gn_ref[...] * jax.nn.silu(r_ref[...].astype(f32))).astype(bf16)


def _gla(q3, kt3, v3, r3, g3, wa, wat, ba, bat, gn):
    bsz, seq, _ = q3.shape
    tg = TG_GLA
    tok = lambda b, i: (b, i, 0)
    c2 = lambda b, i: (0, 0)
    return pl.pallas_call(
        _gla_body,
        grid=(bsz, seq // tg),
        in_specs=[pl.BlockSpec((None, tg, D_GLA_K), tok),
                  pl.BlockSpec((None, D_GLA_K, tg), lambda b, i: (b, 0, i)),
                  pl.BlockSpec((None, tg, D_GLA), tok),
                  pl.BlockSpec((None, tg, D_GLA), tok),
                  pl.BlockSpec((None, tg, LANE), tok),
                  pl.BlockSpec(wa.shape, c2), pl.BlockSpec(wat.shape, c2),
                  pl.BlockSpec(ba.shape, c2), pl.BlockSpec(bat.shape, c2),
                  pl.BlockSpec(gn.shape, c2)],
        out_specs=pl.BlockSpec((None, tg, D_GLA), tok),
        out_shape=SDS((bsz, seq, D_GLA), bf16),
        scratch_shapes=[pltpu.VMEM((D_GLA_K, GLA_DV), f32)],
        compiler_params=_cparams("parallel", "arbitrary"),
        name="gla",
    )(q3, kt3, v3, r3, g3, wa, wat, ba, bat, gn)


def _mix_out(x_ref, ys_ref, yg_ref, wgl_ref, bg_ref, gs_ref, wo_ref):
    y = ys_ref[...]
    z = jnp.dot(y, wgl_ref[...], preferred_element_type=f32) + bg_ref[...]
    yf = y.astype(f32) * jax.nn.sigmoid(z)
    ys = _rms(yf, gs_ref[...]).astype(bf16)
    acc = jnp.dot(ys, wo_ref[0:D_S5, :], preferred_element_type=f32)
    acc = acc + jnp.dot(yg_ref[...], wo_ref[D_S5:, :], preferred_element_type=f32)
    return x_ref[...] + acc


def _mix_specs(tm, w_glu, w_out):
    row = lambda i: (i, 0)
    const = lambda i: (0, 0)
    once = dict(pipeline_mode=pl.Buffered(1))
    return [pl.BlockSpec((tm, D_MODEL), row),
            pl.BlockSpec((tm, D_S5), row),
            pl.BlockSpec((tm, D_GLA), row),
            pl.BlockSpec(w_glu.shape, const, **once),
            pl.BlockSpec((1, D_S5), const),
            pl.BlockSpec((1, D_S5), const),
            pl.BlockSpec(w_out.shape, const, **once)]


def _mix_ffn_body(x_ref, ys_ref, yg_ref, wgl_ref, bg_ref, gs_ref, wo_ref,
                  g_ref, wg_ref, wu_ref, wd_ref, o_ref):
    x = _mix_out(x_ref, ys_ref, yg_ref, wgl_ref, bg_ref, gs_ref, wo_ref)
    hn = _rms(x, g_ref[...]).astype(bf16)
    gate = jnp.dot(hn, wg_ref[...], preferred_element_type=f32)
    up = jnp.dot(hn, wu_ref[...], preferred_element_type=f32)
    act = (jax.nn.silu(gate) * up).astype(bf16)
    o_ref[...] = x + jnp.dot(act, wd_ref[...], preferred_element_type=f32)


def _mix_ffn(x2, ys, yg, w_glu, b_glu, g_s5, w_out, g, wg, wu, wd):
    n = x2.shape[0]
    tm = TM_FFN
    row = lambda i: (i, 0)
    const = lambda i: (0, 0)
    once = dict(pipeline_mode=pl.Buffered(1))
    return pl.pallas_call(
        _mix_ffn_body,
        grid=(n // tm,),
        in_specs=_mix_specs(tm, w_glu, w_out) + [
            pl.BlockSpec((1, D_MODEL), const),
            pl.BlockSpec(wg.shape, const, **once),
            pl.BlockSpec(wu.shape, const, **once),
            pl.BlockSpec(wd.shape, const, **once)],
        out_specs=pl.BlockSpec((tm, D_MODEL), row),
        out_shape=SDS((n, D_MODEL), f32),
        compiler_params=_cparams("parallel"),
        name="mix_ffn",
    )(x2, ys, yg, w_glu, b_glu, g_s5, w_out, g, wg, wu, wd)


def _pack_bf16_pairs(a):
    bits = pltpu.bitcast(a.astype(bf16).astype(f32), u32)
    half = a.shape[1] // 2
    return bits[:, :half] | (bits[:, half:] >> 16)


def _unpack_bf16_pairs(p):
    hi = pltpu.bitcast(p & jnp.uint32(0xFFFF0000), f32).astype(bf16)
    lo = pltpu.bitcast(p << 16, f32).astype(bf16)
    return hi, lo


def _mix_router_body(x_ref, ys_ref, yg_ref, wgl_ref, bg_ref, gs_ref, wo_ref,
                     g_ref, wh_ref, wl_ref, xo_ref, hp_ref, meta_ref, cnt_ref, carry):
    tm = x_ref.shape[0]

    @pl.when(pl.program_id(0) == 0)
    def _():
        carry[...] = jnp.zeros_like(carry)

    x = _mix_out(x_ref, ys_ref, yg_ref, wgl_ref, bg_ref, gs_ref, wo_ref)
    xo_ref[...] = x
    hn = _rms(x, g_ref[...])
    packed = _pack_bf16_pairs(hn)
    hp_ref[0] = packed[:, :SC_ROW]
    hp_ref[1] = packed[:, SC_ROW:]

    h_hi, h_lo = _split2(hn)
    logits = (jnp.dot(h_hi, wh_ref[...], preferred_element_type=f32)
              + jnp.dot(h_lo, wh_ref[...], preferred_element_type=f32)
              + jnp.dot(h_hi, wl_ref[...], preferred_element_type=f32))
    lane = lax.broadcasted_iota(i32, (tm, LANE), 1)
    neg = jnp.float32(-jnp.inf)
    logits = jnp.where(lane < N_EXPERTS, logits, neg)
    m1 = jnp.max(logits, axis=-1, keepdims=True)
    i1 = jnp.min(jnp.where(logits == m1, lane, LANE), axis=-1, keepdims=True)
    l2 = jnp.where(lane == i1, neg, logits)
    m2 = jnp.max(l2, axis=-1, keepdims=True)
    i2 = jnp.min(jnp.where(l2 == m2, lane, LANE), axis=-1, keepdims=True)
    e21 = jnp.exp(m2 - m1)
    w1 = 1.0 / (1.0 + e21)
    w2 = e21 / (1.0 + e21)

    sel1 = lane == i1
    sel2 = lane == i2
    sel = (sel1 | sel2).astype(f32)
    ri = lax.broadcasted_iota(i32, (tm, tm), 0)
    ci = lax.broadcasted_iota(i32, (tm, tm), 1)
    tril = (ri >= ci).astype(bf16)
    incl = jnp.dot(tril, sel.astype(bf16), preferred_element_type=f32)
    rank = incl - sel + carry[0:1, :]
    r1 = jnp.sum(jnp.where(sel1, rank, 0.0), axis=-1, keepdims=True)
    r2 = jnp.sum(jnp.where(sel2, rank, 0.0), axis=-1, keepdims=True)
    new_cnt = carry[0:1, :] + incl[tm - 1:tm, :]
    carry[...] = jnp.broadcast_to(new_cnt, carry.shape)
    cnt_ref[...] = jnp.broadcast_to(new_cnt, cnt_ref.shape)

    meta = jnp.where(lane == 0, i1.astype(f32), 0.0)
    meta = jnp.where(lane == 1, i2.astype(f32), meta)
    meta = jnp.where(lane == 2, w1, meta)
    meta = jnp.where(lane == 3, w2, meta)
    meta = jnp.where(lane == 4, r1, meta)
    meta = jnp.where(lane == 5, r2, meta)
    meta_ref[...] = meta


def _mix_router(x2, ys, yg, w_glu, b_glu, g_s5, w_out, g, w_hi, w_lo):
    n = x2.shape[0]
    tm = TM_ROUTE
    row = lambda i: (i, 0)
    const = lambda i: (0, 0)
    return pl.pallas_call(
        _mix_router_body,
        grid=(n // tm,),
        in_specs=_mix_specs(tm, w_glu, w_out) + [
            pl.BlockSpec((1, D_MODEL), const),
            pl.BlockSpec(w_hi.shape, const),
            pl.BlockSpec(w_lo.shape, const)],
        out_specs=[pl.BlockSpec((tm, D_MODEL), row),
                   pl.BlockSpec((2, tm, SC_ROW), lambda i: (0, i, 0)),
                   pl.BlockSpec((tm, LANE), row),
                   pl.BlockSpec((SUBLANE, LANE), const)],
        out_shape=[SDS((n, D_MODEL), f32), SDS((2, n, SC_ROW), u32), SDS((n, LANE), f32),
                   SDS((SUBLANE, LANE), f32)],
        scratch_shapes=[pltpu.VMEM((SUBLANE, LANE), f32)],
        compiler_params=_cparams("arbitrary"),
        name="mix_router",
    )(x2, ys, yg, w_glu, b_glu, g_s5, w_out, g, w_hi, w_lo)


def _sc_gather(table, idx):
    ni = idx.shape[0]
    mesh = plsc.VectorSubcoreMesh(core_axis_name="core", subcore_axis_name="subcore")
    idx2 = idx.reshape(1, ni)

    @pl.kernel(out_type=SDS((ni, SC_ROW), table.dtype), mesh=mesh)
    def kern(t_hbm, i_hbm, o_hbm):
        def body(i_vmem, o_vmem):
            pltpu.sync_copy(t_hbm.at[i_vmem.at[0]], o_vmem)

        pltpu.emit_pipeline(
            body, grid=(ni // SC_WINDOW,),
            in_specs=[pl.BlockSpec((1, SC_WINDOW), index_map=lambda i: (0, i))],
            out_specs=[pl.BlockSpec((SC_WINDOW, SC_ROW), index_map=lambda i: (i, 0))],
            core_axis_name=("core", "subcore"),
            dimension_semantics=(pltpu.PARALLEL,),
        )(i_hbm, o_hbm)

    return kern(table, idx2)


def _sc_scatter2(x, idx0, idx1, nrows):
    ni = x.shape[0]
    mesh = plsc.VectorSubcoreMesh(core_axis_name="core", subcore_axis_name="subcore")

    @pl.kernel(out_type=SDS((nrows, SC_ROW), x.dtype), mesh=mesh)
    def kern(x_hbm, i0_hbm, i1_hbm, o_hbm):
        def body(x_vmem, i0_vmem, i1_vmem):
            pltpu.sync_copy(x_vmem, o_hbm.at[i0_vmem.at[0]])
            pltpu.sync_copy(x_vmem, o_hbm.at[i1_vmem.at[0]])

        pltpu.emit_pipeline(
            body, grid=(ni // SC_WINDOW,),
            in_specs=[pl.BlockSpec((SC_WINDOW, SC_ROW), index_map=lambda i: (i, 0)),
                      pl.BlockSpec((1, SC_WINDOW), index_map=lambda i: (0, i)),
                      pl.BlockSpec((1, SC_WINDOW), index_map=lambda i: (0, i))],
            out_specs=[],
            core_axis_name=("core", "subcore"),
            dimension_semantics=(pltpu.PARALLEL,),
        )(x_hbm, i0_hbm, i1_hbm)

    return kern(x, idx0.reshape(1, ni), idx1.reshape(1, ni))


def _moe_ffn_body(be_ref, bn_ref, xs_ref, wg_ref, wu_ref, wd_ref, ys_ref, xb, act0, act1, acc):
    i = pl.program_id(0)
    f = pl.program_id(1)
    nf = pl.num_programs(1) - 1
    nvalid = bn_ref[i]
    half = xs_ref.shape[1] // 2

    for h in range(2):
        rs = pl.ds(h * half, half)
        live_half = nvalid > h * half

        def gate_up(rs=rs):
            x = xb[rs, :]
            gate = jnp.dot(x, wg_ref[...].astype(bf16), preferred_element_type=f32)
            up = jnp.dot(x, wu_ref[...].astype(bf16), preferred_element_type=f32)
            return (jax.nn.silu(gate) * up).astype(bf16)

        def down(act_ref, rs=rs):
            return jnp.dot(act_ref[rs, :], wd_ref[...].astype(bf16), preferred_element_type=f32)

        @pl.when(live_half & (f == 0))
        def _(rs=rs, h=h, gate_up=gate_up):
            live = lax.broadcasted_iota(i32, (half, SC_ROW), 0) + h * half < nvalid
            hi0, lo0 = _unpack_bf16_pairs(jnp.where(live, xs_ref[0, rs, :], jnp.uint32(0)))
            hi1, lo1 = _unpack_bf16_pairs(jnp.where(live, xs_ref[1, rs, :], jnp.uint32(0)))
            xb[rs, :] = jnp.concatenate([hi0, hi1, lo0, lo1], axis=1)
            acc[rs, :] = jnp.zeros((half, D_MODEL), f32)
            act0[rs, :] = gate_up()

        @pl.when(live_half & (f > 0) & (f < nf) & (f % 2 == 1))
        def _(rs=rs, gate_up=gate_up, down=down):
            act1[rs, :] = gate_up()
            acc[rs, :] += down(act0)

        @pl.when(live_half & (f > 0) & (f < nf) & (f % 2 == 0))
        def _(rs=rs, gate_up=gate_up, down=down):
            act0[rs, :] = gate_up()
            acc[rs, :] += down(act1)

        @pl.when(live_half & (f == nf))
        def _(rs=rs, down=down):
            last = act0 if (D_FF_EXPERT // TF_MOE - 1) % 2 == 0 else act1
            packed = _pack_bf16_pairs(acc[rs, :] + down(last))
            ys_ref[0, rs, :] = packed[:, :SC_ROW]
            ys_ref[1, rs, :] = packed[:, SC_ROW:]

        @pl.when(jnp.logical_not(live_half) & (f == nf))
        def _(rs=rs):
            ys_ref[:, rs, :] = jnp.zeros((2, half, SC_ROW), u32)


def _moe_ffn(blk_e, blk_n, xs, wg, wu, wd):
    npad = xs.shape[1]
    tm, tf = TM_MOE, TF_MOE
    nblk = npad // tm
    nf = D_FF_EXPERT // tf

    def nxt(i):
        return jnp.minimum(i + 1, nblk - 1)

    def x_idx(i, f, be, bn):
        return (0, jnp.where(f == nf, nxt(i), i), 0)

    def gu_idx(i, f, be, bn):
        ahead = (f == nf) & (bn[i] > 0) & (bn[nxt(i)] > 0)
        e = jnp.where(ahead, be[nxt(i)], be[i])
        t = jnp.where(ahead, 0, jnp.where(bn[i] > 0, jnp.minimum(f, nf - 1), nf - 1))
        return (e, 0, t)

    def d_idx(i, f, be, bn):
        prev = jnp.maximum(i - 1, 0)
        keep = (f == 0) & (i > 0)
        e = jnp.where(keep, be[prev], be[i])
        t = jnp.where(keep | (bn[i] == 0), nf - 1, jnp.maximum(f - 1, 0))
        return (e, t, 0)

    grid_spec = pltpu.PrefetchScalarGridSpec(
        num_scalar_prefetch=2,
        grid=(nblk, nf + 1),
        in_specs=[pl.BlockSpec((2, tm, SC_ROW), x_idx),
                  pl.BlockSpec((None, D_MODEL, tf), gu_idx),
                  pl.BlockSpec((None, D_MODEL, tf), gu_idx),
                  pl.BlockSpec((None, tf, D_MODEL), d_idx)],
        out_specs=pl.BlockSpec((2, tm, SC_ROW), lambda i, f, be, bn: (0, i, 0)),
        scratch_shapes=[pltpu.VMEM((tm, D_MODEL), bf16), pltpu.VMEM((tm, tf), bf16),
                        pltpu.VMEM((tm, tf), bf16), pltpu.VMEM((tm, D_MODEL), f32)],
    )
    return pl.pallas_call(
        _moe_ffn_body,
        grid_spec=grid_spec,
        out_shape=SDS((2, npad, SC_ROW), u32),
        compiler_params=_cparams("parallel", "arbitrary"),
        name="moe_ffn",
    )(blk_e, blk_n, xs, wg, wu, wd)


def _combine_body(x_ref, yg_ref, meta_ref, g_ref, o_ref, *, final_norm):
    w1 = meta_ref[:, 2:3]
    w2 = meta_ref[:, 3:4]

    def rows(k):
        hi0, lo0 = _unpack_bf16_pairs(yg_ref[0, k])
        hi1, lo1 = _unpack_bf16_pairs(yg_ref[1, k])
        return jnp.concatenate([hi0, hi1, lo0, lo1], axis=1).astype(f32)

    xo = x_ref[...] + (w1 * rows(0) + w2 * rows(1))
    if final_norm:
        xo = _rms(xo, g_ref[...])
    o_ref[...] = xo


def _combine(x2, yg, meta, g_final, final_norm):
    n = x2.shape[0]
    tm = TM_ROUTE
    row = lambda i: (i, 0)
    return pl.pallas_call(
        functools.partial(_combine_body, final_norm=final_norm),
        grid=(n // tm,),
        in_specs=[pl.BlockSpec((tm, D_MODEL), row),
                  pl.BlockSpec((2, 2, tm, SC_ROW), lambda i: (0, 0, i, 0)),
                  pl.BlockSpec((tm, LANE), row),
                  pl.BlockSpec((1, D_MODEL), lambda i: (0, 0))],
        out_specs=pl.BlockSpec((tm, D_MODEL), row),
        out_shape=SDS((n, D_MODEL), f32),
        compiler_params=_cparams("parallel"),
        name="moe_combine",
    )(x2, yg, meta, g_final)


def _moe_layer(mix_args, g_ffn, w_router, wg, wu, wd, g_final, final_norm):
    n = mix_args[0].shape[0]
    tm = TM_MOE
    npad = 2 * n + N_EXPERTS * tm
    wr = jnp.zeros((D_MODEL, LANE), f32).at[:, :N_EXPERTS].set(w_router.astype(f32))
    wr_hi = wr.astype(bf16)
    wr_lo = (wr - wr_hi.astype(f32)).astype(bf16)
    x2, hp, meta, cnt = _mix_router(*mix_args, g_ffn, wr_hi, wr_lo)

    counts = cnt[0, :N_EXPERTS].astype(i32)
    padded = ((counts + tm - 1) // tm) * tm
    ends = jnp.cumsum(padded)
    offs = ends - padded
    ids = jnp.arange(N_EXPERTS, dtype=i32)
    e12 = meta[:, 0:2].astype(i32).T
    r12 = meta[:, 4:6].astype(i32).T
    pos_t = r12 + jnp.sum(jnp.where(e12[..., None] == ids, offs, 0), axis=-1)
    blk_start = jnp.arange(npad // tm, dtype=i32) * tm
    blk_e = jnp.minimum(jnp.sum((blk_start[:, None] >= ends[None, :]).astype(i32), axis=1), N_EXPERTS - 1)
    own = blk_e[:, None] == ids
    blk_end = jnp.sum(jnp.where(own, offs + counts, 0), axis=1)
    blk_n = jnp.where(blk_start < ends[-1], jnp.clip(blk_end - blk_start, 0, tm), 0)
    last_e = jnp.max(jnp.where(blk_n > 0, blk_e, 0))
    blk_e = jnp.where(blk_n > 0, blk_e, last_e)

    xs = _sc_scatter2(hp.reshape(2 * n, SC_ROW),
                      jnp.concatenate([pos_t[0], pos_t[0] + npad]),
                      jnp.concatenate([pos_t[1], pos_t[1] + npad]),
                      2 * npad).reshape(2, npad, SC_ROW)
    ys = _moe_ffn(blk_e, blk_n, xs, wg, wu, wd)
    gidx = jnp.concatenate([pos_t.reshape(-1), pos_t.reshape(-1) + npad])
    yg = _sc_gather(ys.reshape(2 * npad, SC_ROW), gidx).reshape(2, 2, n, SC_ROW)
    return _combine(x2, yg, meta, g_final, final_norm)


def _row(v):
    return v.astype(f32).reshape(1, -1)


def kernel(x, norm_mix, w_in, s5_lambda_re, s5_lambda_im, s5_log_dt, s5_b_re, s5_b_im, s5_c_re, s5_c_im, s5_d, s5_w_glu, s5_b_glu, s5_out_norm, gla_w_a2, gla_b_a2, gla_out_norm, w_out, norm_ffn, ffn_w_gate, ffn_w_up, ffn_w_down, moe_w_router, moe_w_gate, moe_w_up, moe_w_down, norm_final):
    bsz, seq, _ = x.shape
    n = bsz * seq
    depth = w_in.shape[0]
    x2 = x.reshape(n, D_MODEL)
    for l in range(depth):
        wi = w_in[l]
        w_u, w_q, w_k, w_v, w_r, w_g = (wi[:, 0:512], wi[:, 512:768], wi[:, 768:1024],
                                        wi[:, 1024:1536], wi[:, 1536:2048], wi[:, 2048:2064])
        w_gp = jnp.zeros((D_MODEL, LANE), f32).at[:, :GLA_GATE_RANK].set(w_g)
        w_cat = jnp.concatenate([w_u, w_q * (GLA_DK ** -0.5), w_v, w_r, w_gp], axis=1).astype(bf16)
        w_kt = w_k.T.astype(bf16)
        u, q, kt, v, r, gl = _inproj(x2, _row(norm_mix[l]), w_cat, w_kt, bsz, seq)

        s5_mats = _s5_prep(s5_lambda_re[l], s5_lambda_im[l], s5_log_dt[l],
                           s5_b_re[l], s5_b_im[l], s5_c_re[l], s5_c_im[l])
        ys = _s5(u.reshape(bsz, seq, D_S5), *s5_mats, _row(s5_d[l]))

        wa = jnp.zeros((LANE, D_GLA_K), f32).at[:GLA_GATE_RANK].set(gla_w_a2[l]).astype(bf16)
        yg = _gla(q.reshape(bsz, seq, D_GLA_K), kt, v.reshape(bsz, seq, D_GLA),
                  r.reshape(bsz, seq, D_GLA), gl.reshape(bsz, seq, LANE),
                  wa, wa.T, _row(gla_b_a2[l]), gla_b_a2[l].astype(f32).reshape(-1, 1),
                  _row(gla_out_norm[l]))

        mix_args = (x2, ys.reshape(n, D_S5), yg.reshape(n, D_GLA), s5_w_glu[l].astype(bf16),
                    _row(s5_b_glu[l]), _row(s5_out_norm[l]), w_out[l].astype(bf16))

        i = l // 2
        last = l == depth - 1
        if l % 2 == 0:
            x2 = _mix_ffn(*mix_args, _row(norm_ffn[l]), ffn_w_gate[i].astype(bf16),
                          ffn_w_up[i].astype(bf16), ffn_w_down[i].astype(bf16))
            if last:
                x2 = _final_norm(x2, _row(norm_final))
        else:
            x2 = _moe_layer(mix_args, _row(norm_ffn[l]), moe_w_router[i], moe_w_gate[i],
                            moe_w_up[i], moe_w_down[i], _row(norm_final), last)
    return x2.reshape(bsz, seq, D_MODEL)


def _final_norm_body(x_ref, g_ref, o_ref):
    o_ref[...] = _rms(x_ref[...], g_ref[...])


def _final_norm(x2, g):
    n = x2.shape[0]
    tm = TM_ROUTE
    return pl.pallas_call(
        _final_norm_body,
        grid=(n // tm,),
        in_specs=[pl.BlockSpec((tm, D_MODEL), lambda i: (i, 0)),
                  pl.BlockSpec((1, D_MODEL), lambda i: (0, 0))],
        out_specs=pl.BlockSpec((tm, D_MODEL), lambda i: (i, 0)),
        out_shape=SDS((n, D_MODEL), f32),
        compiler_params=_cparams("parallel"),
        name="final_norm",
    )(x2, g)
```

```python
import functools
import math

import jax
import jax.numpy as jnp
from jax import lax
from jax.experimental import pallas as pl
from jax.experimental.pallas import tpu as pltpu
from jax.experimental.pallas import tpu_sc as plsc

f32 = jnp.float32
bf16 = jnp.bfloat16
u32 = jnp.uint32
i32 = jnp.int32
SDS = jax.ShapeDtypeStruct

D_MODEL = 1024
D_S5 = 512
S5_GROUP = 16
S5_GROUPS = 32
S5_STATE = 64
N_STATE = S5_GROUPS * S5_STATE
D_GLA = 512
GLA_HEADS = 4
GLA_DV = 128
GLA_DK = 64
D_GLA_K = 256
GLA_GATE_RANK = 16
GLA_TAU = 16.0
GLA_CHUNK = 64
D_FF = 2816
N_EXPERTS = 8
D_FF_EXPERT = 3584
EPS = 1e-6

LANE = 128
SUBLANE = 8
VMEM_LIMIT = 56 * 1024 * 1024

TM_PROJ = 512
TT_S5 = 128
S5_SLABS = 4
S5_PITCH_PAD = 8
TG_GLA = 256
TM_FFN = 512
TM_MOE = 2048
TF_MOE = 512
TM_ROUTE = 512
SC_WINDOW = 128
SC_ROW = 256


def _cparams(*sem):
    return pltpu.CompilerParams(dimension_semantics=sem, vmem_limit_bytes=VMEM_LIMIT)


def _rms(x, g):
    ms = jnp.mean(x * x, axis=-1, keepdims=True)
    return x * lax.rsqrt(ms + EPS) * g


def _inproj_body(x_ref, g_ref, w_ref, wkt_ref, u_ref, q_ref, kt_ref, v_ref, r_ref, gl_ref):
    hn = _rms(x_ref[...], g_ref[...]).astype(bf16)

    def proj(lo, hi):
        return jnp.dot(hn, w_ref[:, lo:hi], preferred_element_type=f32).astype(bf16)

    u_ref[...] = proj(0, 512)
    q_ref[...] = proj(512, 768)
    v_ref[...] = proj(768, 1280)
    r_ref[...] = proj(1280, 1792)
    gl_ref[...] = proj(1792, 1920)
    kt_ref[...] = lax.dot_general(wkt_ref[...], hn, (((1,), (1,)), ((), ())),
                                  preferred_element_type=f32).astype(bf16)


def _inproj(x2, g, w_cat, w_kt, bsz, seq):
    n = x2.shape[0]
    tm = TM_PROJ
    per_b = seq // tm
    row = lambda i: (i, 0)
    const = lambda i: (0, 0)
    return pl.pallas_call(
        _inproj_body,
        grid=(n // tm,),
        in_specs=[pl.BlockSpec((tm, D_MODEL), row),
                  pl.BlockSpec((1, D_MODEL), const),
                  pl.BlockSpec(w_cat.shape, const),
                  pl.BlockSpec(w_kt.shape, const)],
        out_specs=[pl.BlockSpec((tm, D_S5), row),
                   pl.BlockSpec((tm, D_GLA_K), row),
                   pl.BlockSpec((None, D_GLA_K, tm), lambda i: (i // per_b, 0, i % per_b)),
                   pl.BlockSpec((tm, D_GLA), row),
                   pl.BlockSpec((tm, D_GLA), row),
                   pl.BlockSpec((tm, LANE), row)],
        out_shape=[SDS((n, D_S5), bf16), SDS((n, D_GLA_K), bf16),
                   SDS((bsz, D_GLA_K, seq), bf16), SDS((n, D_GLA), bf16),
                   SDS((n, D_GLA), bf16), SDS((n, LANE), bf16)],
        compiler_params=_cparams("parallel"),
        name="inproj",
    )(x2, g, w_cat, w_kt)


N_SLAB = N_STATE // LANE


def _s5_prep(lam_re, lam_im, log_dt, b_re, b_im, c_re, c_im):
    lr = jnp.minimum(lam_re.astype(f32), -1e-4)
    li = lam_im.astype(f32)
    dt = jnp.exp(log_dt.astype(f32))[:, None]
    mag = jnp.exp(lr * dt)
    ab_re = mag * jnp.cos(li * dt)
    ab_im = mag * jnp.sin(li * dt)
    nr = ab_re - 1.0
    ni = ab_im
    den = lr * lr + li * li
    f_re = (nr * lr + ni * li) / den
    f_im = (ni * lr - nr * li) / den
    br = b_re.astype(f32)
    bi = b_im.astype(f32)
    bb_re = f_re[..., None] * br - f_im[..., None] * bi
    bb_im = f_re[..., None] * bi + f_im[..., None] * br
    abb_re = ab_re[..., None] * bb_re - ab_im[..., None] * bb_im
    abb_im = ab_re[..., None] * bb_im + ab_im[..., None] * bb_re
    cr = c_re.astype(f32)
    ci = c_im.astype(f32)
    ca_re = cr * ab_re[:, None, :] - ci * ab_im[:, None, :]
    ca_im = cr * ab_im[:, None, :] + ci * ab_re[:, None, :]
    cb = jnp.einsum('gop,gpc->goc', cr, bb_re) - jnp.einsum('gop,gpc->goc', ci, bb_im)

    def b_tiles(bb):
        bt = jnp.transpose(bb, (0, 2, 1)).reshape(8, 4, S5_GROUP, S5_STATE)
        gl = jnp.arange(8)[None, :, None]
        gs = jnp.arange(4)[None, None, :]
        nn = jnp.arange(8)[:, None, None]
        sel = (gl == 4 * (nn % 2) + gs).astype(f32)
        t = jnp.einsum('ngs,nscp->ngcsp', sel, bt)
        return t.reshape(8, LANE, 4 * S5_STATE)

    bmat = jnp.concatenate(
        [jnp.concatenate([b_tiles(abb_re), b_tiles(bb_re)], axis=1),
         jnp.concatenate([b_tiles(abb_im), b_tiles(bb_im)], axis=1)], axis=0).astype(bf16)

    def c_tiles(c, sign):
        ct = jnp.transpose(c, (0, 2, 1)).reshape(2, 16, S5_STATE, S5_GROUP)
        eye = jnp.eye(16, dtype=f32)
        t = jnp.einsum('gh,jgpo->jgpho', eye, ct) * sign
        return t.reshape(2, 16 * S5_STATE, 16 * S5_GROUP)

    cmat = jnp.stack([c_tiles(cr, 1.0), c_tiles(ci, -1.0)], axis=1).astype(bf16)
    camat = jnp.stack([c_tiles(ca_re, 1.0), c_tiles(ca_im, -1.0)], axis=1).astype(bf16)
    cbt = jnp.transpose(cb, (0, 2, 1)).reshape(2, 16, S5_GROUP, S5_GROUP)
    cbmat = jnp.einsum('gh,jgco->jgcho', jnp.eye(16, dtype=f32), cbt).reshape(2, 256, 256).astype(bf16)
    a2_re = (ab_re * ab_re - ab_im * ab_im).reshape(N_SLAB, 1, LANE)
    a2_im = (2.0 * ab_re * ab_im).reshape(N_SLAB, 1, LANE)
    return bmat, cmat, camat, cbmat, a2_re, a2_im


def _s5_body(u_ref, bm_ref, cm_ref, cam_ref, cbm_ref, are_ref, aim_ref, d_ref, y_ref,
             hbuf, hstate, zcarry):
    tt = u_ref.shape[1]
    tp = tt // 2
    rows = SUBLANE * tp
    pitch = tp + S5_PITCH_PAD

    @pl.when(pl.program_id(0) == 0)
    def _():
        hstate[...] = jnp.zeros_like(hstate)
        zcarry[...] = jnp.zeros_like(zcarry)

    w = pltpu.bitcast(u_ref[...].reshape(SUBLANE * tt, D_S5), u32)
    ue_f = pltpu.bitcast(w << 16, f32)
    uo_f = pltpu.bitcast(w & jnp.uint32(0xFFFF0000), f32)
    ue = ue_f.astype(bf16)
    uo = uo_f.astype(bf16)

    def put(slab, val):
        for b in range(SUBLANE):
            hbuf[slab, b * pitch:b * pitch + tp, :] = val[b * tp:(b + 1) * tp, :]

    def get(slab):
        return jnp.concatenate([hbuf[slab, b * pitch:b * pitch + tp, :] for b in range(SUBLANE)], axis=0)

    for n in range(2 * 8):
        ks = LANE * ((n % 8) // 2)
        lhs = jnp.concatenate([ue[:, ks:ks + LANE], uo[:, ks:ks + LANE]], axis=1)
        res = jnp.dot(lhs, bm_ref[n], preferred_element_type=f32)
        put(2 * n, res[:, :LANE])
        put(2 * n + 1, res[:, LANE:])

    for c0 in range(0, N_SLAB, S5_SLABS):
        ar = [jnp.broadcast_to(are_ref[c0 + s], (SUBLANE, LANE)) for s in range(S5_SLABS)]
        ai = [jnp.broadcast_to(aim_ref[c0 + s], (SUBLANE, LANE)) for s in range(S5_SLABS)]

        def step(t, carry):
            out = []
            for s in range(S5_SLABS):
                hr, hi = carry[2 * s], carry[2 * s + 1]
                sl = pl.ds(t, SUBLANE, stride=pitch)
                bur = hbuf[c0 + s, sl, :]
                bui = hbuf[N_SLAB + c0 + s, sl, :]
                nr = ar[s] * hr - ai[s] * hi + bur
                ni = ar[s] * hi + ai[s] * hr + bui
                hbuf[c0 + s, sl, :] = nr
                hbuf[N_SLAB + c0 + s, sl, :] = ni
                out += [nr, ni]
            return tuple(out)

        init = []
        for s in range(S5_SLABS):
            init += [hstate[c0 + s], hstate[N_SLAB + c0 + s]]
        fin = lax.fori_loop(0, tp, step, tuple(init), unroll=4)
        for s in range(S5_SLABS):
            hstate[c0 + s] = fin[2 * s]
            hstate[N_SLAB + c0 + s] = fin[2 * s + 1]

    first = lax.broadcasted_iota(i32, (rows, 256), 0) % tp == 0
    for j in range(2):
        h_re = jnp.concatenate([get(8 * j + s) for s in range(8)], axis=1).astype(bf16)
        h_im = jnp.concatenate([get(N_SLAB + 8 * j + s) for s in range(8)], axis=1).astype(bf16)
        cs = slice(256 * j, 256 * j + 256)
        dj = d_ref[:, cs]
        yo = jnp.dot(h_re, cm_ref[j, 0], preferred_element_type=f32)
        yo = yo + jnp.dot(h_im, cm_ref[j, 1], preferred_element_type=f32)
        yo = yo + dj * uo_f[:, cs]
        z = jnp.dot(h_re, cam_ref[j, 0], preferred_element_type=f32)
        z = z + jnp.dot(h_im, cam_ref[j, 1], preferred_element_type=f32)
        prev = jnp.concatenate([jnp.broadcast_to(zcarry[b:b + 1, cs], (tp, 256))
                                for b in range(SUBLANE)], axis=0)
        zs = jnp.where(first, prev, pltpu.roll(z, 1, 0))
        zcarry[:, cs] = jnp.concatenate([z[(b + 1) * tp - 1:(b + 1) * tp, :] for b in range(SUBLANE)], axis=0)
        ye = zs + jnp.dot(ue[:, cs], cbm_ref[j], preferred_element_type=f32) + dj * ue_f[:, cs]
        ge = pltpu.bitcast(jax.nn.gelu(ye).astype(bf16).astype(f32), u32)
        go = pltpu.bitcast(jax.nn.gelu(yo).astype(bf16).astype(f32), u32)
        packed = (ge >> 16) | go
        y_ref[:, :, cs] = pltpu.bitcast(packed, bf16).reshape(SUBLANE, tt, 256)


def _s5(u3, bmat, cmat, camat, cbmat, a2_re, a2_im, d_skip):
    bsz, seq, _ = u3.shape
    assert bsz == SUBLANE
    tt = TT_S5
    blk = lambda i: (0, i, 0)
    c3 = lambda i: (0, 0, 0)
    c4 = lambda i: (0, 0, 0, 0)
    return pl.pallas_call(
        _s5_body,
        grid=(seq // tt,),
        in_specs=[pl.BlockSpec((bsz, tt, D_S5), blk),
                  pl.BlockSpec(bmat.shape, c3),
                  pl.BlockSpec(cmat.shape, c4),
                  pl.BlockSpec(camat.shape, c4),
                  pl.BlockSpec(cbmat.shape, c3),
                  pl.BlockSpec(a2_re.shape, c3),
                  pl.BlockSpec(a2_im.shape, c3),
                  pl.BlockSpec((1, D_S5), lambda i: (0, 0))],
        out_specs=pl.BlockSpec((bsz, tt, D_S5), blk),
        out_shape=SDS((bsz, seq, D_S5), bf16),
        scratch_shapes=[pltpu.VMEM((2 * N_SLAB, bsz * (tt // 2 + S5_PITCH_PAD), LANE), f32),
                        pltpu.VMEM((2 * N_SLAB, SUBLANE, LANE), f32),
                        pltpu.VMEM((SUBLANE, D_S5), f32)],
        compiler_params=_cparams("arbitrary"),
        name="s5_scan",
    )(u3, bmat, cmat, camat, cbmat, a2_re, a2_im, d_skip)


def _split2(x):
    hi = x.astype(bf16)
    lo = (x - hi.astype(f32)).astype(bf16)
    return hi, lo


def _log_sigmoid(x):
    return -(jnp.maximum(-x, 0.0) + jnp.log1p(jnp.exp(-jnp.abs(x))))


def _gla_body(q_ref, kt_ref, v_ref, r_ref, g_ref, wa_ref, ba_ref, gn_ref,
              o_ref, s_ref):
    c = GLA_CHUNK
    tg = q_ref.shape[0]

    @pl.when(pl.program_id(1) == 0)
    def _():
        s_ref[...] = jnp.zeros_like(s_ref)

    nc = tg // c
    nh = GLA_HEADS
    ri = lax.broadcasted_iota(i32, (tg, tg), 0)
    ci = lax.broadcasted_iota(i32, (tg, tg), 1)
    same = (ri // c) == (ci // c)
    tril = (same & (ri >= ci)).astype(bf16)
    triu = (same & (ri <= ci)).astype(bf16)
    blk = same.astype(bf16)

    g = g_ref[...]
    la = _log_sigmoid(jnp.dot(g, wa_ref[...], preferred_element_type=f32) + ba_ref[...]) / GLA_TAU
    lat = la.T
    la_hi, la_lo = _split2(la)
    cum = (jnp.dot(tril, la_hi, preferred_element_type=f32)
           + jnp.dot(tril, la_lo, preferred_element_type=f32))
    lat_hi, lat_lo = _split2(lat)
    cumt = (jnp.dot(lat_hi, triu, preferred_element_type=f32)
            + jnp.dot(lat_lo, triu, preferred_element_type=f32))
    clt = (jnp.dot(lat_hi, blk, preferred_element_type=f32)
           + jnp.dot(lat_lo, blk, preferred_element_type=f32))

    qt = q_ref[...].astype(f32) * jnp.exp(cum)
    ktt = kt_ref[...].astype(f32)
    k_t = (ktt * jnp.exp(-cumt)).astype(bf16)
    k_end = ktt * jnp.exp(clt - cumt)
    v = v_ref[...]

    head_of_lane = lax.broadcasted_iota(i32, (c, D_GLA_K), 1) // GLA_DK
    q_stack = jnp.concatenate(
        [jnp.where(head_of_lane == h, qt[cc * c:(cc + 1) * c, :], 0.0)
         for cc in range(nc) for h in range(nh)], axis=0).astype(bf16)
    scores = jnp.dot(q_stack, k_t, preferred_element_type=f32)
    r_idx = lax.broadcasted_iota(i32, (nc * nh * c, tg), 0)
    c_idx = lax.broadcasted_iota(i32, (nc * nh * c, tg), 1)
    causal = (r_idx // (nh * c) == c_idx // c) & (r_idx % c >= c_idx % c)
    scores = jnp.where(causal, scores, 0.0).astype(bf16)
    o_full = jnp.dot(scores, v, preferred_element_type=f32)

    chunk_of_lane = lax.broadcasted_iota(i32, (D_GLA_K, tg), 1) // c
    k_stack = jnp.concatenate(
        [jnp.where(chunk_of_lane == cc, k_end, 0.0) for cc in range(nc)], axis=0).astype(bf16)
    upd_full = jnp.dot(k_stack, v, preferred_element_type=f32)

    state = s_ref[...]
    o_inter = []
    for cc in range(nc):
        o_inter.append(jnp.dot(q_stack[cc * nh * c:(cc + 1) * nh * c, :], state.astype(bf16),
                               preferred_element_type=f32))
        upd = jnp.concatenate(
            [upd_full[cc * D_GLA_K + h * GLA_DK:cc * D_GLA_K + (h + 1) * GLA_DK,
                      h * GLA_DV:(h + 1) * GLA_DV] for h in range(nh)], axis=0)
        state = jnp.exp(clt[:, cc * c:cc * c + 1]) * state + upd
    s_ref[...] = state

    rows = []
    for cc in range(nc):
        outs = []
        for h in range(nh):
            r0 = (cc * nh + h) * c
            o = o_full[r0:r0 + c, h * GLA_DV:(h + 1) * GLA_DV] + o_inter[cc][h * c:(h + 1) * c, :]
            o = o * lax.rsqrt(jnp.mean(o * o, axis=-1, keepdims=True) + EPS)
            outs.append(o)
        rows.append(jnp.concatenate(outs, axis=1))
    o_cat = jnp.concatenate(rows, axis=0)
    o_ref[...] = (o_cat * gn_ref[...] * jax.nn.silu(r_ref[...].astype(f32))).astype(bf16)


def _gla(q3, kt3, v3, r3, g3, wa, ba, gn):
    bsz, seq, _ = q3.shape
    tg = TG_GLA
    tok = lambda b, i: (b, i, 0)
    c2 = lambda b, i: (0, 0)
    return pl.pallas_call(
        _gla_body,
        grid=(bsz, seq // tg),
        in_specs=[pl.BlockSpec((None, tg, D_GLA_K), tok),
                  pl.BlockSpec((None, D_GLA_K, tg), lambda b, i: (b, 0, i)),
                  pl.BlockSpec((None, tg, D_GLA), tok),
                  pl.BlockSpec((None, tg, D_GLA), tok),
                  pl.BlockSpec((None, tg, LANE), tok),
                  pl.BlockSpec(wa.shape, c2), pl.BlockSpec(ba.shape, c2),
                  pl.BlockSpec(gn.shape, c2)],
        out_specs=pl.BlockSpec((None, tg, D_GLA), tok),
        out_shape=SDS((bsz, seq, D_GLA), bf16),
        scratch_shapes=[pltpu.VMEM((D_GLA_K, GLA_DV), f32)],
        compiler_params=_cparams("parallel", "arbitrary"),
        name="gla",
    )(q3, kt3, v3, r3, g3, wa, ba, gn)


def _mix_out(x_ref, ys_ref, yg_ref, wgl_ref, bg_ref, gs_ref, wo_ref):
    y = ys_ref[...]
    z = jnp.dot(y, wgl_ref[...], preferred_element_type=f32) + bg_ref[...]
    yf = y.astype(f32) * jax.nn.sigmoid(z)
    ys = _rms(yf, gs_ref[...]).astype(bf16)
    acc = jnp.dot(ys, wo_ref[0:D_S5, :], preferred_element_type=f32)
    acc = acc + jnp.dot(yg_ref[...], wo_ref[D_S5:, :], preferred_element_type=f32)
    return x_ref[...] + acc


def _mix_specs(tm, w_glu, w_out):
    row = lambda i: (i, 0)
    const = lambda i: (0, 0)
    once = dict(pipeline_mode=pl.Buffered(1))
    return [pl.BlockSpec((tm, D_MODEL), row),
            pl.BlockSpec((tm, D_S5), row),
            pl.BlockSpec((tm, D_GLA), row),
            pl.BlockSpec(w_glu.shape, const, **once),
            pl.BlockSpec((1, D_S5), const),
            pl.BlockSpec((1, D_S5), const),
            pl.BlockSpec(w_out.shape, const, **once)]


def _mix_ffn_body(x_ref, ys_ref, yg_ref, wgl_ref, bg_ref, gs_ref, wo_ref,
                  g_ref, wg_ref, wu_ref, wd_ref, o_ref):
    x = _mix_out(x_ref, ys_ref, yg_ref, wgl_ref, bg_ref, gs_ref, wo_ref)
    hn = _rms(x, g_ref[...]).astype(bf16)
    gate = jnp.dot(hn, wg_ref[...], preferred_element_type=f32)
    up = jnp.dot(hn, wu_ref[...], preferred_element_type=f32)
    act = (jax.nn.silu(gate) * up).astype(bf16)
    o_ref[...] = x + jnp.dot(act, wd_ref[...], preferred_element_type=f32)


def _mix_ffn(x2, ys, yg, w_glu, b_glu, g_s5, w_out, g, wg, wu, wd):
    n = x2.shape[0]
    tm = TM_FFN
    row = lambda i: (i, 0)
    const = lambda i: (0, 0)
    once = dict(pipeline_mode=pl.Buffered(1))
    return pl.pallas_call(
        _mix_ffn_body,
        grid=(n // tm,),
        in_specs=_mix_specs(tm, w_glu, w_out) + [
            pl.BlockSpec((1, D_MODEL), const),
            pl.BlockSpec(wg.shape, const, **once),
            pl.BlockSpec(wu.shape, const, **once),
            pl.BlockSpec(wd.shape, const, **once)],
        out_specs=pl.BlockSpec((tm, D_MODEL), row),
        out_shape=SDS((n, D_MODEL), f32),
        compiler_params=_cparams("parallel"),
        name="mix_ffn",
    )(x2, ys, yg, w_glu, b_glu, g_s5, w_out, g, wg, wu, wd)


def _pack_bf16_pairs(a):
    bits = pltpu.bitcast(a.astype(bf16).astype(f32), u32)
    half = a.shape[1] // 2
    return bits[:, :half] | (bits[:, half:] >> 16)


def _unpack_bf16_pairs(p):
    hi = pltpu.bitcast(p & jnp.uint32(0xFFFF0000), f32).astype(bf16)
    lo = pltpu.bitcast(p << 16, f32).astype(bf16)
    return hi, lo


def _mix_router_body(x_ref, ys_ref, yg_ref, wgl_ref, bg_ref, gs_ref, wo_ref,
                     g_ref, wh_ref, wl_ref, xo_ref, hp_ref, meta_ref, wcol_ref, cnt_ref, carry):
    tm = x_ref.shape[0]

    @pl.when(pl.program_id(0) == 0)
    def _():
        carry[...] = jnp.zeros_like(carry)

    x = _mix_out(x_ref, ys_ref, yg_ref, wgl_ref, bg_ref, gs_ref, wo_ref)
    xo_ref[...] = x
    hn = _rms(x, g_ref[...])
    packed = _pack_bf16_pairs(hn)
    hp_ref[0] = packed[:, :SC_ROW]
    hp_ref[1] = packed[:, SC_ROW:]

    h_hi, h_lo = _split2(hn)
    nt = (((1,), (1,)), ((), ()))
    logits = (lax.dot_general(wh_ref[...], h_hi, nt, preferred_element_type=f32)
              + lax.dot_general(wh_ref[...], h_lo, nt, preferred_element_type=f32)
              + lax.dot_general(wl_ref[...], h_hi, nt, preferred_element_type=f32))
    er = logits.shape[0]
    row = lax.broadcasted_iota(i32, (er, tm), 0)
    neg = jnp.float32(-jnp.inf)
    logits = jnp.where(row < N_EXPERTS, logits, neg)
    m1 = jnp.max(logits, axis=0, keepdims=True)
    i1 = jnp.min(jnp.where(logits == m1, row, er), axis=0, keepdims=True)
    l2 = jnp.where(row == i1, neg, logits)
    m2 = jnp.max(l2, axis=0, keepdims=True)
    i2 = jnp.min(jnp.where(l2 == m2, row, er), axis=0, keepdims=True)
    e21 = jnp.exp(m2 - m1)
    w1 = 1.0 / (1.0 + e21)
    w2 = e21 / (1.0 + e21)

    sel1 = row == i1
    sel2 = row == i2
    sel = (sel1 | sel2).astype(f32)
    ri = lax.broadcasted_iota(i32, (tm, tm), 0)
    ci = lax.broadcasted_iota(i32, (tm, tm), 1)
    triu = (ri <= ci).astype(bf16)
    incl = jnp.dot(sel.astype(bf16), triu, preferred_element_type=f32)
    rank = incl - sel + carry[:, 0:1]
    r1 = jnp.sum(jnp.where(sel1, rank, 0.0), axis=0, keepdims=True)
    r2 = jnp.sum(jnp.where(sel2, rank, 0.0), axis=0, keepdims=True)
    new_cnt = carry[:, 0:1] + incl[:, tm - 1:tm]
    carry[...] = jnp.broadcast_to(new_cnt, carry.shape)
    cnt_ref[...] = jnp.broadcast_to(new_cnt, cnt_ref.shape)

    srow = lax.broadcasted_iota(i32, (SUBLANE, tm), 0)
    meta = jnp.where(srow == 0, i1.astype(f32), 0.0)
    meta = jnp.where(srow == 1, i2.astype(f32), meta)
    meta = jnp.where(srow == 2, r1, meta)
    meta = jnp.where(srow == 3, r2, meta)
    meta_ref[...] = meta
    prow = lax.broadcasted_iota(i32, (LANE, tm), 0)
    wpad = jnp.where(prow == 0, w1, jnp.where(prow == 1, w2, 0.0))
    wcol_ref[...] = wpad.T


def _mix_router(x2, ys, yg, w_glu, b_glu, g_s5, w_out, g, w_hi, w_lo):
    n = x2.shape[0]
    tm = TM_ROUTE
    row = lambda i: (i, 0)
    const = lambda i: (0, 0)
    return pl.pallas_call(
        _mix_router_body,
        grid=(n // tm,),
        in_specs=_mix_specs(tm, w_glu, w_out) + [
            pl.BlockSpec((1, D_MODEL), const),
            pl.BlockSpec(w_hi.shape, const),
            pl.BlockSpec(w_lo.shape, const)],
        out_specs=[pl.BlockSpec((tm, D_MODEL), row),
                   pl.BlockSpec((2, tm, SC_ROW), lambda i: (0, i, 0)),
                   pl.BlockSpec((SUBLANE, tm), lambda i: (0, i)),
                   pl.BlockSpec((tm, LANE), row),
                   pl.BlockSpec(w_hi.shape[:1] + (LANE,), const)],
        out_shape=[SDS((n, D_MODEL), f32), SDS((2, n, SC_ROW), u32), SDS((SUBLANE, n), f32),
                   SDS((n, LANE), f32), SDS(w_hi.shape[:1] + (LANE,), f32)],
        scratch_shapes=[pltpu.VMEM(w_hi.shape[:1] + (LANE,), f32)],
        compiler_params=_cparams("arbitrary"),
        name="mix_router",
    )(x2, ys, yg, w_glu, b_glu, g_s5, w_out, g, w_hi, w_lo)


def _sc_gather(table, idx):
    ni = idx.shape[0]
    mesh = plsc.VectorSubcoreMesh(core_axis_name="core", subcore_axis_name="subcore")
    idx2 = idx.reshape(1, ni)

    @pl.kernel(out_type=SDS((ni, SC_ROW), table.dtype), mesh=mesh)
    def kern(t_hbm, i_hbm, o_hbm):
        def body(i_vmem, o_vmem):
            pltpu.sync_copy(t_hbm.at[i_vmem.at[0]], o_vmem)

        pltpu.emit_pipeline(
            body, grid=(ni // SC_WINDOW,),
            in_specs=[pl.BlockSpec((1, SC_WINDOW), index_map=lambda i: (0, i))],
            out_specs=[pl.BlockSpec((SC_WINDOW, SC_ROW), index_map=lambda i: (i, 0))],
            core_axis_name=("core", "subcore"),
            dimension_semantics=(pltpu.PARALLEL,),
        )(i_hbm, o_hbm)

    return kern(table, idx2)


def _sc_scatter2(x, idx0, idx1, nrows):
    ni = x.shape[0]
    mesh = plsc.VectorSubcoreMesh(core_axis_name="core", subcore_axis_name="subcore")

    @pl.kernel(out_type=SDS((nrows, SC_ROW), x.dtype), mesh=mesh)
    def kern(x_hbm, i0_hbm, i1_hbm, o_hbm):
        def body(x_vmem, i0_vmem, i1_vmem):
            pltpu.sync_copy(x_vmem, o_hbm.at[i0_vmem.at[0]])
            pltpu.sync_copy(x_vmem, o_hbm.at[i1_vmem.at[0]])

        pltpu.emit_pipeline(
            body, grid=(ni // SC_WINDOW,),
            in_specs=[pl.BlockSpec((SC_WINDOW, SC_ROW), index_map=lambda i: (i, 0)),
                      pl.BlockSpec((1, SC_WINDOW), index_map=lambda i: (0, i)),
                      pl.BlockSpec((1, SC_WINDOW), index_map=lambda i: (0, i))],
            out_specs=[],
            core_axis_name=("core", "subcore"),
            dimension_semantics=(pltpu.PARALLEL,),
        )(x_hbm, i0_hbm, i1_hbm)

    return kern(x, idx0.reshape(1, ni), idx1.reshape(1, ni))


def _moe_ffn_body(be_ref, bn_ref, xs_ref, wg_ref, wu_ref, wd_ref, ys_ref, xb, act0, act1, acc):
    i = pl.program_id(0)
    f = pl.program_id(1)
    nf = pl.num_programs(1) - 1
    nvalid = bn_ref[i]
    half = xs_ref.shape[1] // 2

    for h in range(2):
        rs = pl.ds(h * half, half)
        live_half = nvalid > h * half

        def gate_up(rs=rs):
            x = xb[rs, :]
            gate = jnp.dot(x, wg_ref[...].astype(bf16), preferred_element_type=f32)
            up = jnp.dot(x, wu_ref[...].astype(bf16), preferred_element_type=f32)
            return (jax.nn.silu(gate) * up).astype(bf16)

        def down(act_ref, rs=rs):
            return jnp.dot(act_ref[rs, :], wd_ref[...].astype(bf16), preferred_element_type=f32)

        @pl.when(live_half & (f == 0))
        def _(rs=rs, h=h, gate_up=gate_up):
            live = lax.broadcasted_iota(i32, (half, SC_ROW), 0) + h * half < nvalid
            hi0, lo0 = _unpack_bf16_pairs(jnp.where(live, xs_ref[0, rs, :], jnp.uint32(0)))
            hi1, lo1 = _unpack_bf16_pairs(jnp.where(live, xs_ref[1, rs, :], jnp.uint32(0)))
            xb[rs, :] = jnp.concatenate([hi0, hi1, lo0, lo1], axis=1)
            acc[rs, :] = jnp.zeros((half, D_MODEL), f32)
            act0[rs, :] = gate_up()

        @pl.when(live_half & (f > 0) & (f < nf) & (f % 2 == 1))
        def _(rs=rs, gate_up=gate_up, down=down):
            act1[rs, :] = gate_up()
            acc[rs, :] += down(act0)

        @pl.when(live_half & (f > 0) & (f < nf) & (f % 2 == 0))
        def _(rs=rs, gate_up=gate_up, down=down):
            act0[rs, :] = gate_up()
            acc[rs, :] += down(act1)

        @pl.when(live_half & (f == nf))
        def _(rs=rs, down=down):
            last = act0 if (D_FF_EXPERT // TF_MOE - 1) % 2 == 0 else act1
            packed = _pack_bf16_pairs(acc[rs, :] + down(last))
            ys_ref[0, rs, :] = packed[:, :SC_ROW]
            ys_ref[1, rs, :] = packed[:, SC_ROW:]

        @pl.when(jnp.logical_not(live_half) & (f == nf))
        def _(rs=rs):
            ys_ref[:, rs, :] = jnp.zeros((2, half, SC_ROW), u32)


def _moe_ffn(blk_e, blk_n, xs, wg, wu, wd):
    npad = xs.shape[1]
    tm, tf = TM_MOE, TF_MOE
    nblk = npad // tm
    nf = D_FF_EXPERT // tf

    def nxt(i):
        return jnp.minimum(i + 1, nblk - 1)

    def x_idx(i, f, be, bn):
        return (0, jnp.where(f == nf, nxt(i), i), 0)

    def gu_idx(i, f, be, bn):
        ahead = (f == nf) & (bn[i] > 0) & (bn[nxt(i)] > 0)
        e = jnp.where(ahead, be[nxt(i)], be[i])
        t = jnp.where(ahead, 0, jnp.where(bn[i] > 0, jnp.minimum(f, nf - 1), nf - 1))
        return (e, 0, t)

    def d_idx(i, f, be, bn):
        prev = jnp.maximum(i - 1, 0)
        keep = (f == 0) & (i > 0)
        e = jnp.where(keep, be[prev], be[i])
        t = jnp.where(keep | (bn[i] == 0), nf - 1, jnp.maximum(f - 1, 0))
        return (e, t, 0)

    grid_spec = pltpu.PrefetchScalarGridSpec(
        num_scalar_prefetch=2,
        grid=(nblk, nf + 1),
        in_specs=[pl.BlockSpec((2, tm, SC_ROW), x_idx),
                  pl.BlockSpec((None, D_MODEL, tf), gu_idx),
                  pl.BlockSpec((None, D_MODEL, tf), gu_idx),
                  pl.BlockSpec((None, tf, D_MODEL), d_idx)],
        out_specs=pl.BlockSpec((2, tm, SC_ROW), lambda i, f, be, bn: (0, i, 0)),
        scratch_shapes=[pltpu.VMEM((tm, D_MODEL), bf16), pltpu.VMEM((tm, tf), bf16),
                        pltpu.VMEM((tm, tf), bf16), pltpu.VMEM((tm, D_MODEL), f32)],
    )
    return pl.pallas_call(
        _moe_ffn_body,
        grid_spec=grid_spec,
        out_shape=SDS((2, npad, SC_ROW), u32),
        compiler_params=_cparams("parallel", "arbitrary"),
        name="moe_ffn",
    )(blk_e, blk_n, xs, wg, wu, wd)


def _combine_body(x_ref, yg_ref, wcol_ref, g_ref, o_ref, *, final_norm):
    w1 = wcol_ref[:, 0:1]
    w2 = wcol_ref[:, 1:2]

    def rows(k):
        hi0, lo0 = _unpack_bf16_pairs(yg_ref[0, k])
        hi1, lo1 = _unpack_bf16_pairs(yg_ref[1, k])
        return jnp.concatenate([hi0, hi1, lo0, lo1], axis=1).astype(f32)

    xo = x_ref[...] + (w1 * rows(0) + w2 * rows(1))
    if final_norm:
        xo = _rms(xo, g_ref[...])
    o_ref[...] = xo


def _combine(x2, yg, wcol, g_final, final_norm):
    n = x2.shape[0]
    tm = TM_ROUTE
    row = lambda i: (i, 0)
    return pl.pallas_call(
        functools.partial(_combine_body, final_norm=final_norm),
        grid=(n // tm,),
        in_specs=[pl.BlockSpec((tm, D_MODEL), row),
                  pl.BlockSpec((2, 2, tm, SC_ROW), lambda i: (0, 0, i, 0)),
                  pl.BlockSpec((tm, LANE), row),
                  pl.BlockSpec((1, D_MODEL), lambda i: (0, 0))],
        out_specs=pl.BlockSpec((tm, D_MODEL), row),
        out_shape=SDS((n, D_MODEL), f32),
        compiler_params=_cparams("parallel"),
        name="moe_combine",
    )(x2, yg, wcol, g_final)


def _moe_layer(mix_args, g_ffn, w_router, wg, wu, wd, g_final, final_norm):
    n = mix_args[0].shape[0]
    tm = TM_MOE
    npad = 2 * n + N_EXPERTS * tm
    wr = jnp.zeros((2 * SUBLANE, D_MODEL), f32).at[:N_EXPERTS].set(w_router.astype(f32).T)
    wr_hi = wr.astype(bf16)
    wr_lo = (wr - wr_hi.astype(f32)).astype(bf16)
    x2, hp, meta, wcol, cnt = _mix_router(*mix_args, g_ffn, wr_hi, wr_lo)

    counts = cnt[:N_EXPERTS, 0].astype(i32)
    padded = ((counts + tm - 1) // tm) * tm
    ends = jnp.cumsum(padded)
    offs = ends - padded
    ids = jnp.arange(N_EXPERTS, dtype=i32)
    e12 = meta[0:2].astype(i32)
    r12 = meta[2:4].astype(i32)
    pos_t = r12 + jnp.sum(jnp.where(e12[..., None] == ids, offs, 0), axis=-1)
    blk_start = jnp.arange(npad // tm, dtype=i32) * tm
    blk_e = jnp.minimum(jnp.sum((blk_start[:, None] >= ends[None, :]).astype(i32), axis=1), N_EXPERTS - 1)
    own = blk_e[:, None] == ids
    blk_end = jnp.sum(jnp.where(own, offs + counts, 0), axis=1)
    blk_n = jnp.where(blk_start < ends[-1], jnp.clip(blk_end - blk_start, 0, tm), 0)
    last_e = jnp.max(jnp.where(blk_n > 0, blk_e, 0))
    blk_e = jnp.where(blk_n > 0, blk_e, last_e)

    xs = _sc_scatter2(hp.reshape(2 * n, SC_ROW),
                      jnp.concatenate([pos_t[0], pos_t[0] + npad]),
                      jnp.concatenate([pos_t[1], pos_t[1] + npad]),
                      2 * npad).reshape(2, npad, SC_ROW)
    ys = _moe_ffn(blk_e, blk_n, xs, wg, wu, wd)
    gidx = jnp.concatenate([pos_t.reshape(-1), pos_t.reshape(-1) + npad])
    yg = _sc_gather(ys.reshape(2 * npad, SC_ROW), gidx).reshape(2, 2, n, SC_ROW)
    return _combine(x2, yg, wcol, g_final, final_norm)


def _row(v):
    return v.astype(f32).reshape(1, -1)


def kernel(x, norm_mix, w_in, s5_lambda_re, s5_lambda_im, s5_log_dt, s5_b_re, s5_b_im, s5_c_re, s5_c_im, s5_d, s5_w_glu, s5_b_glu, s5_out_norm, gla_w_a2, gla_b_a2, gla_out_norm, w_out, norm_ffn, ffn_w_gate, ffn_w_up, ffn_w_down, moe_w_router, moe_w_gate, moe_w_up, moe_w_down, norm_final):
    bsz, seq, _ = x.shape
    n = bsz * seq
    depth = w_in.shape[0]
    x2 = x.reshape(n, D_MODEL)
    for l in range(depth):
        wi = w_in[l]
        w_u, w_q, w_k, w_v, w_r, w_g = (wi[:, 0:512], wi[:, 512:768], wi[:, 768:1024],
                                        wi[:, 1024:1536], wi[:, 1536:2048], wi[:, 2048:2064])
        w_gp = jnp.zeros((D_MODEL, LANE), f32).at[:, :GLA_GATE_RANK].set(w_g)
        w_cat = jnp.concatenate([w_u, w_q * (GLA_DK ** -0.5), w_v, w_r, w_gp], axis=1).astype(bf16)
        w_kt = w_k.T.astype(bf16)
        u, q, kt, v, r, gl = _inproj(x2, _row(norm_mix[l]), w_cat, w_kt, bsz, seq)

        s5_mats = _s5_prep(s5_lambda_re[l], s5_lambda_im[l], s5_log_dt[l],
                           s5_b_re[l], s5_b_im[l], s5_c_re[l], s5_c_im[l])
        ys = _s5(u.reshape(bsz, seq, D_S5), *s5_mats, _row(s5_d[l]))

        wa = jnp.zeros((LANE, D_GLA_K), f32).at[:GLA_GATE_RANK].set(gla_w_a2[l]).astype(bf16)
        yg = _gla(q.reshape(bsz, seq, D_GLA_K), kt, v.reshape(bsz, seq, D_GLA),
                  r.reshape(bsz, seq, D_GLA), gl.reshape(bsz, seq, LANE),
                  wa, _row(gla_b_a2[l]), _row(gla_out_norm[l]))

        mix_args = (x2, ys.reshape(n, D_S5), yg.reshape(n, D_GLA), s5_w_glu[l].astype(bf16),
                    _row(s5_b_glu[l]), _row(s5_out_norm[l]), w_out[l].astype(bf16))

        i = l // 2
        last = l == depth - 1
        if l % 2 == 0:
            x2 = _mix_ffn(*mix_args, _row(norm_ffn[l]), ffn_w_gate[i].astype(bf16),
                          ffn_w_up[i].astype(bf16), ffn_w_down[i].astype(bf16))
            if last:
                x2 = _final_norm(x2, _row(norm_final))
        else:
            x2 = _moe_layer(mix_args, _row(norm_ffn[l]), moe_w_router[i], moe_w_gate[i],
                            moe_w_up[i], moe_w_down[i], _row(norm_final), last)
    return x2.reshape(bsz, seq, D_MODEL)


def _final_norm_body(x_ref, g_ref, o_ref):
    o_ref[...] = _rms(x_ref[...], g_ref[...])


def _final_norm(x2, g):
    n = x2.shape[0]
    tm = TM_ROUTE
    return pl.pallas_call(
        _final_norm_body,
        grid=(n // tm,),
        in_specs=[pl.BlockSpec((tm, D_MODEL), lambda i: (i, 0)),
                  pl.BlockSpec((1, D_MODEL), lambda i: (0, 0))],
        out_specs=pl.BlockSpec((tm, D_MODEL), lambda i: (i, 0)),
        out_shape=SDS((n, D_MODEL), f32),
        compiler_params=_cparams("parallel"),
        name="final_norm",
    )(x2, g)
```

```python
import functools
import math

import jax
import jax.numpy as jnp
from jax import lax
from jax.experimental import pallas as pl
from jax.experimental.pallas import tpu as pltpu
from jax.experimental.pallas import tpu_sc as plsc

f32 = jnp.float32
bf16 = jnp.bfloat16
u32 = jnp.uint32
i32 = jnp.int32
SDS = jax.ShapeDtypeStruct

D_MODEL = 1024
D_S5 = 512
S5_GROUP = 16
S5_GROUPS = 32
S5_STATE = 64
N_STATE = S5_GROUPS * S5_STATE
D_GLA = 512
GLA_HEADS = 4
GLA_DV = 128
GLA_DK = 64
D_GLA_K = 256
GLA_GATE_RANK = 16
GLA_TAU = 16.0
GLA_CHUNK = 64
D_FF = 2816
N_EXPERTS = 8
D_FF_EXPERT = 3584
EPS = 1e-6

LANE = 128
SUBLANE = 8
VMEM_LIMIT = 56 * 1024 * 1024

TM_PROJ = 1024
TT_S5 = 128
S5_SLABS = 4
S5_PITCH_PAD = 8
TG_GLA = 512
GLA_SUB = 256
TM_FFN = 512
TM_MOE = 2048
MOE_SUB = 4
TF_MOE = 512
TM_ROUTE = 512
SC_WINDOW = 128
SC_ROW = 256


def _cparams(*sem):
    return pltpu.CompilerParams(dimension_semantics=sem, vmem_limit_bytes=VMEM_LIMIT)


def _rms(x, g):
    ms = jnp.mean(x * x, axis=-1, keepdims=True)
    return x * lax.rsqrt(ms + EPS) * g


def _inproj_body(x_ref, g_ref, w_ref, wkt_ref, u_ref, q_ref, kt_ref, v_ref, r_ref, gl_ref):
    hn = _rms(x_ref[...], g_ref[...]).astype(bf16)

    def proj(lo, hi):
        return jnp.dot(hn, w_ref[:, lo:hi], preferred_element_type=f32).astype(bf16)

    u_ref[...] = proj(0, 512)
    q_ref[...] = proj(512, 768)
    v_ref[...] = proj(768, 1280)
    r_ref[...] = proj(1280, 1792)
    gl_ref[...] = proj(1792, 1920)
    kt_ref[...] = lax.dot_general(wkt_ref[...], hn, (((1,), (1,)), ((), ())),
                                  preferred_element_type=f32).astype(bf16)


def _inproj(x2, g, w_cat, w_kt, bsz, seq):
    n = x2.shape[0]
    tm = TM_PROJ
    per_b = seq // tm
    row = lambda i: (i, 0)
    const = lambda i: (0, 0)
    return pl.pallas_call(
        _inproj_body,
        grid=(n // tm,),
        in_specs=[pl.BlockSpec((tm, D_MODEL), row),
                  pl.BlockSpec((1, D_MODEL), const),
                  pl.BlockSpec(w_cat.shape, const),
                  pl.BlockSpec(w_kt.shape, const)],
        out_specs=[pl.BlockSpec((tm, D_S5), row),
                   pl.BlockSpec((tm, D_GLA_K), row),
                   pl.BlockSpec((None, D_GLA_K, tm), lambda i: (i // per_b, 0, i % per_b)),
                   pl.BlockSpec((tm, D_GLA), row),
                   pl.BlockSpec((tm, D_GLA), row),
                   pl.BlockSpec((tm, LANE), row)],
        out_shape=[SDS((n, D_S5), bf16), SDS((n, D_GLA_K), bf16),
                   SDS((bsz, D_GLA_K, seq), bf16), SDS((n, D_GLA), bf16),
                   SDS((n, D_GLA), bf16), SDS((n, LANE), bf16)],
        compiler_params=_cparams("parallel"),
        name="inproj",
    )(x2, g, w_cat, w_kt)


N_SLAB = N_STATE // LANE


def _s5_prep(lam_re, lam_im, log_dt, b_re, b_im, c_re, c_im):
    lr = jnp.minimum(lam_re.astype(f32), -1e-4)
    li = lam_im.astype(f32)
    dt = jnp.exp(log_dt.astype(f32))[:, None]
    mag = jnp.exp(lr * dt)
    ab_re = mag * jnp.cos(li * dt)
    ab_im = mag * jnp.sin(li * dt)
    nr = ab_re - 1.0
    ni = ab_im
    den = lr * lr + li * li
    f_re = (nr * lr + ni * li) / den
    f_im = (ni * lr - nr * li) / den
    br = b_re.astype(f32)
    bi = b_im.astype(f32)
    bb_re = f_re[..., None] * br - f_im[..., None] * bi
    bb_im = f_re[..., None] * bi + f_im[..., None] * br
    abb_re = ab_re[..., None] * bb_re - ab_im[..., None] * bb_im
    abb_im = ab_re[..., None] * bb_im + ab_im[..., None] * bb_re
    cr = c_re.astype(f32)
    ci = c_im.astype(f32)
    ca_re = cr * ab_re[:, None, :] - ci * ab_im[:, None, :]
    ca_im = cr * ab_im[:, None, :] + ci * ab_re[:, None, :]
    cb = jnp.einsum('gop,gpc->goc', cr, bb_re) - jnp.einsum('gop,gpc->goc', ci, bb_im)

    def b_tiles(bb):
        bt = jnp.transpose(bb, (0, 2, 1)).reshape(8, 4, S5_GROUP, S5_STATE)
        gl = jnp.arange(8)[None, :, None]
        gs = jnp.arange(4)[None, None, :]
        nn = jnp.arange(8)[:, None, None]
        sel = (gl == 4 * (nn % 2) + gs).astype(f32)
        t = jnp.einsum('ngs,nscp->ngcsp', sel, bt)
        return t.reshape(8, LANE, 4 * S5_STATE)

    bmat = jnp.concatenate(
        [jnp.concatenate([b_tiles(abb_re), b_tiles(bb_re)], axis=1),
         jnp.concatenate([b_tiles(abb_im), b_tiles(bb_im)], axis=1)], axis=0).astype(bf16)

    def c_tiles(c, sign):
        ct = jnp.transpose(c, (0, 2, 1)).reshape(2, 16, S5_STATE, S5_GROUP)
        eye = jnp.eye(16, dtype=f32)
        t = jnp.einsum('gh,jgpo->jgpho', eye, ct) * sign
        return t.reshape(2, 16 * S5_STATE, 16 * S5_GROUP)

    cmat = jnp.stack([c_tiles(cr, 1.0), c_tiles(ci, -1.0)], axis=1).astype(bf16)
    camat = jnp.stack([c_tiles(ca_re, 1.0), c_tiles(ca_im, -1.0)], axis=1).astype(bf16)
    cbt = jnp.transpose(cb, (0, 2, 1)).reshape(2, 16, S5_GROUP, S5_GROUP)
    cbmat = jnp.einsum('gh,jgco->jgcho', jnp.eye(16, dtype=f32), cbt).reshape(2, 256, 256).astype(bf16)
    a2_re = (ab_re * ab_re - ab_im * ab_im).reshape(N_SLAB, 1, LANE)
    a2_im = (2.0 * ab_re * ab_im).reshape(N_SLAB, 1, LANE)
    return bmat, cmat, camat, cbmat, a2_re, a2_im


def _s5_body(u_ref, bm_ref, cm_ref, cam_ref, cbm_ref, are_ref, aim_ref, d_ref, y_ref,
             hbuf, hstate, zcarry):
    tt = u_ref.shape[1]
    tp = tt // 2
    rows = SUBLANE * tp
    pitch = tp + S5_PITCH_PAD

    @pl.when(pl.program_id(0) == 0)
    def _():
        hstate[...] = jnp.zeros_like(hstate)
        zcarry[...] = jnp.zeros_like(zcarry)

    w = pltpu.bitcast(u_ref[...].reshape(SUBLANE * tt, D_S5), u32)
    ue_f = pltpu.bitcast(w << 16, f32)
    uo_f = pltpu.bitcast(w & jnp.uint32(0xFFFF0000), f32)
    ue = ue_f.astype(bf16)
    uo = uo_f.astype(bf16)

    def put(slab, val):
        for b in range(SUBLANE):
            hbuf[slab, b * pitch:b * pitch + tp, :] = val[b * tp:(b + 1) * tp, :]

    def get(slab):
        return jnp.concatenate([hbuf[slab, b * pitch:b * pitch + tp, :] for b in range(SUBLANE)], axis=0)

    for n in range(2 * 8):
        ks = LANE * ((n % 8) // 2)
        lhs = jnp.concatenate([ue[:, ks:ks + LANE], uo[:, ks:ks + LANE]], axis=1)
        res = jnp.dot(lhs, bm_ref[n], preferred_element_type=f32)
        put(2 * n, res[:, :LANE])
        put(2 * n + 1, res[:, LANE:])

    for c0 in range(0, N_SLAB, S5_SLABS):
        ar = [jnp.broadcast_to(are_ref[c0 + s], (SUBLANE, LANE)) for s in range(S5_SLABS)]
        ai = [jnp.broadcast_to(aim_ref[c0 + s], (SUBLANE, LANE)) for s in range(S5_SLABS)]

        def step(t, carry):
            out = []
            for s in range(S5_SLABS):
                hr, hi = carry[2 * s], carry[2 * s + 1]
                sl = pl.ds(t, SUBLANE, stride=pitch)
                bur = hbuf[c0 + s, sl, :]
                bui = hbuf[N_SLAB + c0 + s, sl, :]
                nr = ar[s] * hr - ai[s] * hi + bur
                ni = ar[s] * hi + ai[s] * hr + bui
                hbuf[c0 + s, sl, :] = nr
                hbuf[N_SLAB + c0 + s, sl, :] = ni
                out += [nr, ni]
            return tuple(out)

        init = []
        for s in range(S5_SLABS):
            init += [hstate[c0 + s], hstate[N_SLAB + c0 + s]]
        fin = lax.fori_loop(0, tp, step, tuple(init), unroll=4)
        for s in range(S5_SLABS):
            hstate[c0 + s] = fin[2 * s]
            hstate[N_SLAB + c0 + s] = fin[2 * s + 1]

    first = lax.broadcasted_iota(i32, (rows, 256), 0) % tp == 0
    for j in range(2):
        h_re = jnp.concatenate([get(8 * j + s) for s in range(8)], axis=1).astype(bf16)
        h_im = jnp.concatenate([get(N_SLAB + 8 * j + s) for s in range(8)], axis=1).astype(bf16)
        cs = slice(256 * j, 256 * j + 256)
        dj = d_ref[:, cs]
        yo = jnp.dot(h_re, cm_ref[j, 0], preferred_element_type=f32)
        yo = yo + jnp.dot(h_im, cm_ref[j, 1], preferred_element_type=f32)
        yo = yo + dj * uo_f[:, cs]
        z = jnp.dot(h_re, cam_ref[j, 0], preferred_element_type=f32)
        z = z + jnp.dot(h_im, cam_ref[j, 1], preferred_element_type=f32)
        prev = jnp.concatenate([jnp.broadcast_to(zcarry[b:b + 1, cs], (tp, 256))
                                for b in range(SUBLANE)], axis=0)
        zs = jnp.where(first, prev, pltpu.roll(z, 1, 0))
        zcarry[:, cs] = jnp.concatenate([z[(b + 1) * tp - 1:(b + 1) * tp, :] for b in range(SUBLANE)], axis=0)
        ye = zs + jnp.dot(ue[:, cs], cbm_ref[j], preferred_element_type=f32) + dj * ue_f[:, cs]
        ge = pltpu.bitcast(jax.nn.gelu(ye).astype(bf16).astype(f32), u32)
        go = pltpu.bitcast(jax.nn.gelu(yo).astype(bf16).astype(f32), u32)
        packed = (ge >> 16) | go
        y_ref[:, :, cs] = pltpu.bitcast(packed, bf16).reshape(SUBLANE, tt, 256)


def _s5(u3, bmat, cmat, camat, cbmat, a2_re, a2_im, d_skip):
    bsz, seq, _ = u3.shape
    assert bsz == SUBLANE
    tt = TT_S5
    blk = lambda i: (0, i, 0)
    c3 = lambda i: (0, 0, 0)
    c4 = lambda i: (0, 0, 0, 0)
    return pl.pallas_call(
        _s5_body,
        grid=(seq // tt,),
        in_specs=[pl.BlockSpec((bsz, tt, D_S5), blk),
                  pl.BlockSpec(bmat.shape, c3),
                  pl.BlockSpec(cmat.shape, c4),
                  pl.BlockSpec(camat.shape, c4),
                  pl.BlockSpec(cbmat.shape, c3),
                  pl.BlockSpec(a2_re.shape, c3),
                  pl.BlockSpec(a2_im.shape, c3),
                  pl.BlockSpec((1, D_S5), lambda i: (0, 0))],
        out_specs=pl.BlockSpec((bsz, tt, D_S5), blk),
        out_shape=SDS((bsz, seq, D_S5), bf16),
        scratch_shapes=[pltpu.VMEM((2 * N_SLAB, bsz * (tt // 2 + S5_PITCH_PAD), LANE), f32),
                        pltpu.VMEM((2 * N_SLAB, SUBLANE, LANE), f32),
                        pltpu.VMEM((SUBLANE, D_S5), f32)],
        compiler_params=_cparams("arbitrary"),
        name="s5_scan",
    )(u3, bmat, cmat, camat, cbmat, a2_re, a2_im, d_skip)


def _split2(x):
    hi = x.astype(bf16)
    lo = (x - hi.astype(f32)).astype(bf16)
    return hi, lo


def _log_sigmoid(x):
    return -(jnp.maximum(-x, 0.0) + jnp.log1p(jnp.exp(-jnp.abs(x))))


def _gla_body(q_ref, kt_ref, v_ref, r_ref, g_ref, wa_ref, ba_ref, gn_ref,
              o_ref, s_ref):
    c = GLA_CHUNK
    tg = q_ref.shape[0]

    @pl.when(pl.program_id(1) == 0)
    def _():
        s_ref[...] = jnp.zeros_like(s_ref)

    ts = GLA_SUB
    nc = ts // c
    nh = GLA_HEADS
    ri = lax.broadcasted_iota(i32, (ts, ts), 0)
    ci = lax.broadcasted_iota(i32, (ts, ts), 1)
    same = (ri // c) == (ci // c)
    tril = (same & (ri >= ci)).astype(bf16)
    triu = (same & (ri <= ci)).astype(bf16)
    blk = same.astype(bf16)
    head_of_lane = lax.broadcasted_iota(i32, (c, D_GLA_K), 1) // GLA_DK
    r_idx = lax.broadcasted_iota(i32, (nc * nh * c, ts), 0)
    c_idx = lax.broadcasted_iota(i32, (nc * nh * c, ts), 1)
    causal = (r_idx // (nh * c) == c_idx // c) & (r_idx % c >= c_idx % c)
    chunk_of_lane = lax.broadcasted_iota(i32, (D_GLA_K, ts), 1) // c

    state = s_ref[...]
    for st in range(tg // ts):
        rs = slice(st * ts, (st + 1) * ts)
        g = g_ref[rs, :]
        la = _log_sigmoid(jnp.dot(g, wa_ref[...], preferred_element_type=f32) + ba_ref[...]) / GLA_TAU
        lat = la.T
        la_hi, la_lo = _split2(la)
        cum = (jnp.dot(tril, la_hi, preferred_element_type=f32)
               + jnp.dot(tril, la_lo, preferred_element_type=f32))
        lat_hi, lat_lo = _split2(lat)
        cumt = (jnp.dot(lat_hi, triu, preferred_element_type=f32)
                + jnp.dot(lat_lo, triu, preferred_element_type=f32))
        clt = (jnp.dot(lat_hi, blk, preferred_element_type=f32)
               + jnp.dot(lat_lo, blk, preferred_element_type=f32))

        qt = q_ref[rs, :].astype(f32) * jnp.exp(cum)
        ktt = kt_ref[:, rs].astype(f32)
        k_t = (ktt * jnp.exp(-cumt)).astype(bf16)
        k_end = ktt * jnp.exp(clt - cumt)
        v = v_ref[rs, :]

        q_stack = jnp.concatenate(
            [jnp.where(head_of_lane == h, qt[cc * c:(cc + 1) * c, :], 0.0)
             for cc in range(nc) for h in range(nh)], axis=0).astype(bf16)
        scores = jnp.dot(q_stack, k_t, preferred_element_type=f32)
        scores = jnp.where(causal, scores, 0.0).astype(bf16)
        o_full = jnp.dot(scores, v, preferred_element_type=f32)

        k_stack = jnp.concatenate(
            [jnp.where(chunk_of_lane == cc, k_end, 0.0) for cc in range(nc)], axis=0).astype(bf16)
        upd_full = jnp.dot(k_stack, v, preferred_element_type=f32)

        o_inter = []
        for cc in range(nc):
            o_inter.append(jnp.dot(q_stack[cc * nh * c:(cc + 1) * nh * c, :], state.astype(bf16),
                                   preferred_element_type=f32))
            upd = jnp.concatenate(
                [upd_full[cc * D_GLA_K + h * GLA_DK:cc * D_GLA_K + (h + 1) * GLA_DK,
                          h * GLA_DV:(h + 1) * GLA_DV] for h in range(nh)], axis=0)
            state = jnp.exp(clt[:, cc * c:cc * c + 1]) * state + upd

        rows = []
        for cc in range(nc):
            outs = []
            for h in range(nh):
                r0 = (cc * nh + h) * c
                o = o_full[r0:r0 + c, h * GLA_DV:(h + 1) * GLA_DV] + o_inter[cc][h * c:(h + 1) * c, :]
                o = o * lax.rsqrt(jnp.mean(o * o, axis=-1, keepdims=True) + EPS)
                outs.append(o)
            rows.append(jnp.concatenate(outs, axis=1))
        o_cat = jnp.concatenate(rows, axis=0)
        o_ref[rs, :] = (o_cat * gn_ref[...] * jax.nn.silu(r_ref[rs, :].astype(f32))).astype(bf16)
    s_ref[...] = state


def _gla(q3, kt3, v3, r3, g3, wa, ba, gn):
    bsz, seq, _ = q3.shape
    tg = TG_GLA
    tok = lambda b, i: (b, i, 0)
    c2 = lambda b, i: (0, 0)
    return pl.pallas_call(
        _gla_body,
        grid=(bsz, seq // tg),
        in_specs=[pl.BlockSpec((None, tg, D_GLA_K), tok),
                  pl.BlockSpec((None, D_GLA_K, tg), lambda b, i: (b, 0, i)),
                  pl.BlockSpec((None, tg, D_GLA), tok),
                  pl.BlockSpec((None, tg, D_GLA), tok),
                  pl.BlockSpec((None, tg, LANE), tok),
                  pl.BlockSpec(wa.shape, c2), pl.BlockSpec(ba.shape, c2),
                  pl.BlockSpec(gn.shape, c2)],
        out_specs=pl.BlockSpec((None, tg, D_GLA), tok),
        out_shape=SDS((bsz, seq, D_GLA), bf16),
        scratch_shapes=[pltpu.VMEM((D_GLA_K, GLA_DV), f32)],
        compiler_params=_cparams("parallel", "arbitrary"),
        name="gla",
    )(q3, kt3, v3, r3, g3, wa, ba, gn)


def _mix_out(x_ref, ys_ref, yg_ref, wgl_ref, bg_ref, gs_ref, wo_ref):
    y = ys_ref[...]
    z = jnp.dot(y, wgl_ref[...], preferred_element_type=f32) + bg_ref[...]
    yf = y.astype(f32) * jax.nn.sigmoid(z)
    ys = _rms(yf, gs_ref[...]).astype(bf16)
    acc = jnp.dot(ys, wo_ref[0:D_S5, :], preferred_element_type=f32)
    acc = acc + jnp.dot(yg_ref[...], wo_ref[D_S5:, :], preferred_element_type=f32)
    return x_ref[...] + acc


def _mix_specs(tm, w_glu, w_out):
    row = lambda i: (i, 0)
    const = lambda i: (0, 0)
    once = dict(pipeline_mode=pl.Buffered(1))
    return [pl.BlockSpec((tm, D_MODEL), row),
            pl.BlockSpec((tm, D_S5), row),
            pl.BlockSpec((tm, D_GLA), row),
            pl.BlockSpec(w_glu.shape, const, **once),
            pl.BlockSpec((1, D_S5), const),
            pl.BlockSpec((1, D_S5), const),
            pl.BlockSpec(w_out.shape, const, **once)]


def _mix_ffn_body(x_ref, ys_ref, yg_ref, wgl_ref, bg_ref, gs_ref, wo_ref,
                  g_ref, wg_ref, wu_ref, wd_ref, o_ref):
    x = _mix_out(x_ref, ys_ref, yg_ref, wgl_ref, bg_ref, gs_ref, wo_ref)
    hn = _rms(x, g_ref[...]).astype(bf16)
    gate = jnp.dot(hn, wg_ref[...], preferred_element_type=f32)
    up = jnp.dot(hn, wu_ref[...], preferred_element_type=f32)
    act = (jax.nn.silu(gate) * up).astype(bf16)
    o_ref[...] = x + jnp.dot(act, wd_ref[...], preferred_element_type=f32)


def _mix_ffn(x2, ys, yg, w_glu, b_glu, g_s5, w_out, g, wg, wu, wd):
    n = x2.shape[0]
    tm = TM_FFN
    row = lambda i: (i, 0)
    const = lambda i: (0, 0)
    once = dict(pipeline_mode=pl.Buffered(1))
    return pl.pallas_call(
        _mix_ffn_body,
        grid=(n // tm,),
        in_specs=_mix_specs(tm, w_glu, w_out) + [
            pl.BlockSpec((1, D_MODEL), const),
            pl.BlockSpec(wg.shape, const, **once),
            pl.BlockSpec(wu.shape, const, **once),
            pl.BlockSpec(wd.shape, const, **once)],
        out_specs=pl.BlockSpec((tm, D_MODEL), row),
        out_shape=SDS((n, D_MODEL), f32),
        compiler_params=_cparams("parallel"),
        name="mix_ffn",
    )(x2, ys, yg, w_glu, b_glu, g_s5, w_out, g, wg, wu, wd)


def _pack_bf16_pairs(a):
    bits = pltpu.bitcast(a.astype(bf16).astype(f32), u32)
    half = a.shape[1] // 2
    return bits[:, :half] | (bits[:, half:] >> 16)


def _unpack_bf16_pairs(p):
    hi = pltpu.bitcast(p & jnp.uint32(0xFFFF0000), f32).astype(bf16)
    lo = pltpu.bitcast(p << 16, f32).astype(bf16)
    return hi, lo


def _mix_router_body(x_ref, ys_ref, yg_ref, wgl_ref, bg_ref, gs_ref, wo_ref,
                     g_ref, wh_ref, wl_ref, xo_ref, hp_ref, meta_ref, wcol_ref, cnt_ref, carry):
    tm = x_ref.shape[0]

    @pl.when(pl.program_id(0) == 0)
    def _():
        carry[...] = jnp.zeros_like(carry)

    x = _mix_out(x_ref, ys_ref, yg_ref, wgl_ref, bg_ref, gs_ref, wo_ref)
    xo_ref[...] = x
    hn = _rms(x, g_ref[...])
    packed = _pack_bf16_pairs(hn)
    hp_ref[0] = packed[:, :SC_ROW]
    hp_ref[1] = packed[:, SC_ROW:]

    h_hi, h_lo = _split2(hn)
    nt = (((1,), (1,)), ((), ()))
    logits = (lax.dot_general(wh_ref[...], h_hi, nt, preferred_element_type=f32)
              + lax.dot_general(wh_ref[...], h_lo, nt, preferred_element_type=f32)
              + lax.dot_general(wl_ref[...], h_hi, nt, preferred_element_type=f32))
    er = logits.shape[0]
    row = lax.broadcasted_iota(i32, (er, tm), 0)
    neg = jnp.float32(-jnp.inf)
    logits = jnp.where(row < N_EXPERTS, logits, neg)
    m1 = jnp.max(logits, axis=0, keepdims=True)
    i1 = jnp.min(jnp.where(logits == m1, row, er), axis=0, keepdims=True)
    l2 = jnp.where(row == i1, neg, logits)
    m2 = jnp.max(l2, axis=0, keepdims=True)
    i2 = jnp.min(jnp.where(l2 == m2, row, er), axis=0, keepdims=True)
    e21 = jnp.exp(m2 - m1)
    w1 = 1.0 / (1.0 + e21)
    w2 = e21 / (1.0 + e21)

    sel1 = row == i1
    sel2 = row == i2
    sel = (sel1 | sel2).astype(f32)
    ri = lax.broadcasted_iota(i32, (tm, tm), 0)
    ci = lax.broadcasted_iota(i32, (tm, tm), 1)
    triu = (ri <= ci).astype(bf16)
    incl = jnp.dot(sel.astype(bf16), triu, preferred_element_type=f32)
    rank = incl - sel + carry[:, 0:1]
    r1 = jnp.sum(jnp.where(sel1, rank, 0.0), axis=0, keepdims=True)
    r2 = jnp.sum(jnp.where(sel2, rank, 0.0), axis=0, keepdims=True)
    new_cnt = carry[:, 0:1] + incl[:, tm - 1:tm]
    carry[...] = jnp.broadcast_to(new_cnt, carry.shape)
    cnt_ref[...] = jnp.broadcast_to(new_cnt, cnt_ref.shape)

    srow = lax.broadcasted_iota(i32, (SUBLANE, tm), 0)
    meta = jnp.where(srow == 0, i1.astype(f32), 0.0)
    meta = jnp.where(srow == 1, i2.astype(f32), meta)
    meta = jnp.where(srow == 2, r1, meta)
    meta = jnp.where(srow == 3, r2, meta)
    meta_ref[...] = meta
    prow = lax.broadcasted_iota(i32, (LANE, tm), 0)
    wpad = jnp.where(prow == 0, w1, jnp.where(prow == 1, w2, 0.0))
    wcol_ref[...] = wpad.T


def _mix_router(x2, ys, yg, w_glu, b_glu, g_s5, w_out, g, w_hi, w_lo):
    n = x2.shape[0]
    tm = TM_ROUTE
    row = lambda i: (i, 0)
    const = lambda i: (0, 0)
    return pl.pallas_call(
        _mix_router_body,
        grid=(n // tm,),
        in_specs=_mix_specs(tm, w_glu, w_out) + [
            pl.BlockSpec((1, D_MODEL), const),
            pl.BlockSpec(w_hi.shape, const),
            pl.BlockSpec(w_lo.shape, const)],
        out_specs=[pl.BlockSpec((tm, D_MODEL), row),
                   pl.BlockSpec((2, tm, SC_ROW), lambda i: (0, i, 0)),
                   pl.BlockSpec((SUBLANE, tm), lambda i: (0, i)),
                   pl.BlockSpec((tm, LANE), row),
                   pl.BlockSpec(w_hi.shape[:1] + (LANE,), const)],
        out_shape=[SDS((n, D_MODEL), f32), SDS((2, n, SC_ROW), u32), SDS((SUBLANE, n), f32),
                   SDS((n, LANE), f32), SDS(w_hi.shape[:1] + (LANE,), f32)],
        scratch_shapes=[pltpu.VMEM(w_hi.shape[:1] + (LANE,), f32)],
        compiler_params=_cparams("arbitrary"),
        name="mix_router",
    )(x2, ys, yg, w_glu, b_glu, g_s5, w_out, g, w_hi, w_lo)


def _sc_gather(table, idx):
    ni = idx.shape[0]
    mesh = plsc.VectorSubcoreMesh(core_axis_name="core", subcore_axis_name="subcore")
    idx2 = idx.reshape(1, ni)

    @pl.kernel(out_type=SDS((ni, SC_ROW), table.dtype), mesh=mesh)
    def kern(t_hbm, i_hbm, o_hbm):
        def body(i_vmem, o_vmem):
            pltpu.sync_copy(t_hbm.at[i_vmem.at[0]], o_vmem)

        pltpu.emit_pipeline(
            body, grid=(ni // SC_WINDOW,),
            in_specs=[pl.BlockSpec((1, SC_WINDOW), index_map=lambda i: (0, i))],
            out_specs=[pl.BlockSpec((SC_WINDOW, SC_ROW), index_map=lambda i: (i, 0))],
            core_axis_name=("core", "subcore"),
            dimension_semantics=(pltpu.PARALLEL,),
        )(i_hbm, o_hbm)

    return kern(table, idx2)


def _sc_scatter2(x, idx0, idx1, nrows):
    ni = x.shape[0]
    mesh = plsc.VectorSubcoreMesh(core_axis_name="core", subcore_axis_name="subcore")

    @pl.kernel(out_type=SDS((nrows, SC_ROW), x.dtype), mesh=mesh)
    def kern(x_hbm, i0_hbm, i1_hbm, o_hbm):
        def body(x_vmem, i0_vmem, i1_vmem):
            pltpu.sync_copy(x_vmem, o_hbm.at[i0_vmem.at[0]])
            pltpu.sync_copy(x_vmem, o_hbm.at[i1_vmem.at[0]])

        pltpu.emit_pipeline(
            body, grid=(ni // SC_WINDOW,),
            in_specs=[pl.BlockSpec((SC_WINDOW, SC_ROW), index_map=lambda i: (i, 0)),
                      pl.BlockSpec((1, SC_WINDOW), index_map=lambda i: (0, i)),
                      pl.BlockSpec((1, SC_WINDOW), index_map=lambda i: (0, i))],
            out_specs=[],
            core_axis_name=("core", "subcore"),
            dimension_semantics=(pltpu.PARALLEL,),
        )(x_hbm, i0_hbm, i1_hbm)

    return kern(x, idx0.reshape(1, ni), idx1.reshape(1, ni))


def _moe_ffn_body(be_ref, bn_ref, xs_ref, wg_ref, wu_ref, wd_ref, ys_ref, xb, act0, act1, acc):
    i = pl.program_id(0)
    f = pl.program_id(1)
    nf = pl.num_programs(1) - 1
    nvalid = bn_ref[i]
    half = xs_ref.shape[1] // MOE_SUB

    for h in range(MOE_SUB):
        rs = pl.ds(h * half, half)
        live_half = nvalid > h * half

        def gate_up(rs=rs):
            x = xb[rs, :]
            gate = jnp.dot(x, wg_ref[...].astype(bf16), preferred_element_type=f32)
            up = jnp.dot(x, wu_ref[...].astype(bf16), preferred_element_type=f32)
            return (jax.nn.silu(gate) * up).astype(bf16)

        def down(act_ref, rs=rs):
            return jnp.dot(act_ref[rs, :], wd_ref[...].astype(bf16), preferred_element_type=f32)

        @pl.when(live_half & (f == 0))
        def _(rs=rs, h=h, gate_up=gate_up):
            live = lax.broadcasted_iota(i32, (half, SC_ROW), 0) + h * half < nvalid
            hi0, lo0 = _unpack_bf16_pairs(jnp.where(live, xs_ref[0, rs, :], jnp.uint32(0)))
            hi1, lo1 = _unpack_bf16_pairs(jnp.where(live, xs_ref[1, rs, :], jnp.uint32(0)))
            xb[rs, :] = jnp.concatenate([hi0, hi1, lo0, lo1], axis=1)
            acc[rs, :] = jnp.zeros((half, D_MODEL), f32)
            act0[rs, :] = gate_up()

        @pl.when(live_half & (f > 0) & (f < nf) & (f % 2 == 1))
        def _(rs=rs, gate_up=gate_up, down=down):
            act1[rs, :] = gate_up()
            acc[rs, :] += down(act0)

        @pl.when(live_half & (f > 0) & (f < nf) & (f % 2 == 0))
        def _(rs=rs, gate_up=gate_up, down=down):
            act0[rs, :] = gate_up()
            acc[rs, :] += down(act1)

        @pl.when(live_half & (f == nf))
        def _(rs=rs, down=down):
            last = act0 if (D_FF_EXPERT // TF_MOE - 1) % 2 == 0 else act1
            packed = _pack_bf16_pairs(acc[rs, :] + down(last))
            ys_ref[0, rs, :] = packed[:, :SC_ROW]
            ys_ref[1, rs, :] = packed[:, SC_ROW:]

        @pl.when(jnp.logical_not(live_half) & (f == nf))
        def _(rs=rs):
            ys_ref[:, rs, :] = jnp.zeros((2, half, SC_ROW), u32)


def _moe_ffn(blk_e, blk_n, xs, wg, wu, wd):
    npad = xs.shape[1]
    tm, tf = TM_MOE, TF_MOE
    nblk = npad // tm
    nf = D_FF_EXPERT // tf

    def nxt(i):
        return jnp.minimum(i + 1, nblk - 1)

    def x_idx(i, f, be, bn):
        return (0, jnp.where(f == nf, nxt(i), i), 0)

    def gu_idx(i, f, be, bn):
        ahead = (f == nf) & (bn[i] > 0) & (bn[nxt(i)] > 0)
        e = jnp.where(ahead, be[nxt(i)], be[i])
        t = jnp.where(ahead, 0, jnp.where(bn[i] > 0, jnp.minimum(f, nf - 1), nf - 1))
        return (e, 0, t)

    def d_idx(i, f, be, bn):
        prev = jnp.maximum(i - 1, 0)
        keep = (f == 0) & (i > 0)
        e = jnp.where(keep, be[prev], be[i])
        t = jnp.where(keep | (bn[i] == 0), nf - 1, jnp.maximum(f - 1, 0))
        return (e, t, 0)

    grid_spec = pltpu.PrefetchScalarGridSpec(
        num_scalar_prefetch=2,
        grid=(nblk, nf + 1),
        in_specs=[pl.BlockSpec((2, tm, SC_ROW), x_idx),
                  pl.BlockSpec((None, D_MODEL, tf), gu_idx),
                  pl.BlockSpec((None, D_MODEL, tf), gu_idx),
                  pl.BlockSpec((None, tf, D_MODEL), d_idx)],
        out_specs=pl.BlockSpec((2, tm, SC_ROW), lambda i, f, be, bn: (0, i, 0)),
        scratch_shapes=[pltpu.VMEM((tm, D_MODEL), bf16), pltpu.VMEM((tm, tf), bf16),
                        pltpu.VMEM((tm, tf), bf16), pltpu.VMEM((tm, D_MODEL), f32)],
    )
    return pl.pallas_call(
        _moe_ffn_body,
        grid_spec=grid_spec,
        out_shape=SDS((2, npad, SC_ROW), u32),
        compiler_params=_cparams("parallel", "arbitrary"),
        name="moe_ffn",
    )(blk_e, blk_n, xs, wg, wu, wd)


def _combine_body(x_ref, yg_ref, wcol_ref, g_ref, o_ref, *, final_norm):
    w1 = wcol_ref[:, 0:1]
    w2 = wcol_ref[:, 1:2]

    def rows(k):
        hi0, lo0 = _unpack_bf16_pairs(yg_ref[0, k])
        hi1, lo1 = _unpack_bf16_pairs(yg_ref[1, k])
        return jnp.concatenate([hi0, hi1, lo0, lo1], axis=1).astype(f32)

    xo = x_ref[...] + (w1 * rows(0) + w2 * rows(1))
    if final_norm:
        xo = _rms(xo, g_ref[...])
    o_ref[...] = xo


def _combine(x2, yg, wcol, g_final, final_norm):
    n = x2.shape[0]
    tm = TM_ROUTE
    row = lambda i: (i, 0)
    return pl.pallas_call(
        functools.partial(_combine_body, final_norm=final_norm),
        grid=(n // tm,),
        in_specs=[pl.BlockSpec((tm, D_MODEL), row),
                  pl.BlockSpec((2, 2, tm, SC_ROW), lambda i: (0, 0, i, 0)),
                  pl.BlockSpec((tm, LANE), row),
                  pl.BlockSpec((1, D_MODEL), lambda i: (0, 0))],
        out_specs=pl.BlockSpec((tm, D_MODEL), row),
        out_shape=SDS((n, D_MODEL), f32),
        compiler_params=_cparams("parallel"),
        name="moe_combine",
    )(x2, yg, wcol, g_final)


def _moe_layer(mix_args, g_ffn, w_router, wg, wu, wd, g_final, final_norm):
    n = mix_args[0].shape[0]
    tm = TM_MOE
    npad = 2 * n + N_EXPERTS * tm
    wr = jnp.zeros((2 * SUBLANE, D_MODEL), f32).at[:N_EXPERTS].set(w_router.astype(f32).T)
    wr_hi = wr.astype(bf16)
    wr_lo = (wr - wr_hi.astype(f32)).astype(bf16)
    x2, hp, meta, wcol, cnt = _mix_router(*mix_args, g_ffn, wr_hi, wr_lo)

    counts = cnt[:N_EXPERTS, 0].astype(i32)
    padded = ((counts + tm - 1) // tm) * tm
    ends = jnp.cumsum(padded)
    offs = ends - padded
    ids = jnp.arange(N_EXPERTS, dtype=i32)
    e12 = meta[0:2].astype(i32)
    r12 = meta[2:4].astype(i32)
    pos_t = r12 + jnp.sum(jnp.where(e12[..., None] == ids, offs, 0), axis=-1)
    blk_start = jnp.arange(npad // tm, dtype=i32) * tm
    blk_e = jnp.minimum(jnp.sum((blk_start[:, None] >= ends[None, :]).astype(i32), axis=1), N_EXPERTS - 1)
    own = blk_e[:, None] == ids
    blk_end = jnp.sum(jnp.where(own, offs + counts, 0), axis=1)
    blk_n = jnp.where(blk_start < ends[-1], jnp.clip(blk_end - blk_start, 0, tm), 0)
    last_e = jnp.max(jnp.where(blk_n > 0, blk_e, 0))
    blk_e = jnp.where(blk_n > 0, blk_e, last_e)

    xs = _sc_scatter2(hp.reshape(2 * n, SC_ROW),
                      jnp.concatenate([pos_t[0], pos_t[0] + npad]),
                      jnp.concatenate([pos_t[1], pos_t[1] + npad]),
                      2 * npad).reshape(2, npad, SC_ROW)
    ys = _moe_ffn(blk_e, blk_n, xs, wg, wu, wd)
    gidx = jnp.concatenate([pos_t.reshape(-1), pos_t.reshape(-1) + npad])
    yg = _sc_gather(ys.reshape(2 * npad, SC_ROW), gidx).reshape(2, 2, n, SC_ROW)
    return _combine(x2, yg, wcol, g_final, final_norm)


def _row(v):
    return v.astype(f32).reshape(1, -1)


def kernel(x, norm_mix, w_in, s5_lambda_re, s5_lambda_im, s5_log_dt, s5_b_re, s5_b_im, s5_c_re, s5_c_im, s5_d, s5_w_glu, s5_b_glu, s5_out_norm, gla_w_a2, gla_b_a2, gla_out_norm, w_out, norm_ffn, ffn_w_gate, ffn_w_up, ffn_w_down, moe_w_router, moe_w_gate, moe_w_up, moe_w_down, norm_final):
    bsz, seq, _ = x.shape
    n = bsz * seq
    depth = w_in.shape[0]
    x2 = x.reshape(n, D_MODEL)
    for l in range(depth):
        wi = w_in[l]
        w_u, w_q, w_k, w_v, w_r, w_g = (wi[:, 0:512], wi[:, 512:768], wi[:, 768:1024],
                                        wi[:, 1024:1536], wi[:, 1536:2048], wi[:, 2048:2064])
        w_gp = jnp.zeros((D_MODEL, LANE), f32).at[:, :GLA_GATE_RANK].set(w_g)
        w_cat = jnp.concatenate([w_u, w_q * (GLA_DK ** -0.5), w_v, w_r, w_gp], axis=1).astype(bf16)
        w_kt = w_k.T.astype(bf16)
        u, q, kt, v, r, gl = _inproj(x2, _row(norm_mix[l]), w_cat, w_kt, bsz, seq)

        s5_mats = _s5_prep(s5_lambda_re[l], s5_lambda_im[l], s5_log_dt[l],
                           s5_b_re[l], s5_b_im[l], s5_c_re[l], s5_c_im[l])
        ys = _s5(u.reshape(bsz, seq, D_S5), *s5_mats, _row(s5_d[l]))

        wa = jnp.zeros((LANE, D_GLA_K), f32).at[:GLA_GATE_RANK].set(gla_w_a2[l]).astype(bf16)
        yg = _gla(q.reshape(bsz, seq, D_GLA_K), kt, v.reshape(bsz, seq, D_GLA),
                  r.reshape(bsz, seq, D_GLA), gl.reshape(bsz, seq, LANE),
                  wa, _row(gla_b_a2[l]), _row(gla_out_norm[l]))

        mix_args = (x2, ys.reshape(n, D_S5), yg.reshape(n, D_GLA), s5_w_glu[l].astype(bf16),
                    _row(s5_b_glu[l]), _row(s5_out_norm[l]), w_out[l].astype(bf16))

        i = l // 2
        last = l == depth - 1
        if l % 2 == 0:
            x2 = _mix_ffn(*mix_args, _row(norm_ffn[l]), ffn_w_gate[i].astype(bf16),
                          ffn_w_up[i].astype(bf16), ffn_w_down[i].astype(bf16))
            if last:
                x2 = _final_norm(x2, _row(norm_final))
        else:
            x2 = _moe_layer(mix_args, _row(norm_ffn[l]), moe_w_router[i], moe_w_gate[i],
                            moe_w_up[i], moe_w_down[i], _row(norm_final), last)
    return x2.reshape(bsz, seq, D_MODEL)


def _final_norm_body(x_ref, g_ref, o_ref):
    o_ref[...] = _rms(x_ref[...], g_ref[...])


def _final_norm(x2, g):
    n = x2.shape[0]
    tm = TM_ROUTE
    return pl.pallas_call(
        _final_norm_body,
        grid=(n // tm,),
        in_specs=[pl.BlockSpec((tm, D_MODEL), lambda i: (i, 0)),
                  pl.BlockSpec((1, D_MODEL), lambda i: (0, 0))],
        out_specs=pl.BlockSpec((tm, D_MODEL), lambda i: (i, 0)),
        out_shape=SDS((n, D_MODEL), f32),
        compiler_params=_cparams("parallel"),
        name="final_norm",
    )(x2, g)
```

```python
import functools
import math

import jax
import jax.numpy as jnp
from jax import lax
from jax.experimental import pallas as pl
from jax.experimental.pallas import tpu as pltpu
from jax.experimental.pallas import tpu_sc as plsc

f32 = jnp.float32
bf16 = jnp.bfloat16
u32 = jnp.uint32
i32 = jnp.int32
SDS = jax.ShapeDtypeStruct

D_MODEL = 1024
D_S5 = 512
S5_GROUP = 16
S5_GROUPS = 32
S5_STATE = 64
N_STATE = S5_GROUPS * S5_STATE
D_GLA = 512
GLA_HEADS = 4
GLA_DV = 128
GLA_DK = 64
D_GLA_K = 256
GLA_GATE_RANK = 16
GLA_TAU = 16.0
GLA_CHUNK = 64
D_FF = 2816
N_EXPERTS = 8
D_FF_EXPERT = 3584
EPS = 1e-6

LANE = 128
SUBLANE = 8
VMEM_LIMIT = 56 * 1024 * 1024

TM_PROJ = 1024
TT_S5 = 128
S5_SLABS = 4
S5_PITCH_PAD = 8
TG_GLA = 512
GLA_SUB = 256
TM_FFN = 512
TM_MOE = 2048
MOE_SUB = 2
TF_MOE = 512
TM_ROUTE = 512
SC_WINDOW = 128
SC_ROW = 256


def _cparams(*sem):
    return pltpu.CompilerParams(dimension_semantics=sem, vmem_limit_bytes=VMEM_LIMIT)


def _layer(arr, l, **kw):
    return pl.BlockSpec((None,) + arr.shape[1:], lambda *_: (l,) + (0,) * (arr.ndim - 1), **kw)


def _rms(x, g):
    ms = jnp.mean(x * x, axis=-1, keepdims=True)
    return x * lax.rsqrt(ms + EPS) * g


def _inproj_body(x_ref, g_ref, w_ref, wkt_ref, u_ref, q_ref, kt_ref, v_ref, r_ref, gl_ref):
    hn = _rms(x_ref[...], g_ref[...]).astype(bf16)

    def proj(lo, hi):
        return jnp.dot(hn, w_ref[:, lo:hi], preferred_element_type=f32).astype(bf16)

    u_ref[...] = proj(0, 512)
    q_ref[...] = proj(512, 768)
    v_ref[...] = proj(768, 1280)
    r_ref[...] = proj(1280, 1792)
    gl_ref[...] = proj(1792, 1920)
    kt_ref[...] = lax.dot_general(wkt_ref[...], hn, (((1,), (1,)), ((), ())),
                                  preferred_element_type=f32).astype(bf16)


def _inproj(l, x2, g, w_cat, w_kt, bsz, seq):
    n = x2.shape[0]
    tm = TM_PROJ
    per_b = seq // tm
    row = lambda i: (i, 0)
    return pl.pallas_call(
        _inproj_body,
        grid=(n // tm,),
        in_specs=[pl.BlockSpec((tm, D_MODEL), row),
                  _layer(g, l), _layer(w_cat, l), _layer(w_kt, l)],
        out_specs=[pl.BlockSpec((tm, D_S5), row),
                   pl.BlockSpec((tm, D_GLA_K), row),
                   pl.BlockSpec((None, D_GLA_K, tm), lambda i: (i // per_b, 0, i % per_b)),
                   pl.BlockSpec((tm, D_GLA), row),
                   pl.BlockSpec((tm, D_GLA), row),
                   pl.BlockSpec((tm, LANE), row)],
        out_shape=[SDS((n, D_S5), bf16), SDS((n, D_GLA_K), bf16),
                   SDS((bsz, D_GLA_K, seq), bf16), SDS((n, D_GLA), bf16),
                   SDS((n, D_GLA), bf16), SDS((n, LANE), bf16)],
        compiler_params=_cparams("parallel"),
        name="inproj",
    )(x2, g, w_cat, w_kt)


N_SLAB = N_STATE // LANE


def _s5_prep(lam_re, lam_im, log_dt, b_re, b_im, c_re, c_im):
    lr = jnp.minimum(lam_re.astype(f32), -1e-4)
    li = lam_im.astype(f32)
    dt = jnp.exp(log_dt.astype(f32))[:, None]
    mag = jnp.exp(lr * dt)
    ab_re = mag * jnp.cos(li * dt)
    ab_im = mag * jnp.sin(li * dt)
    nr = ab_re - 1.0
    ni = ab_im
    den = lr * lr + li * li
    f_re = (nr * lr + ni * li) / den
    f_im = (ni * lr - nr * li) / den
    br = b_re.astype(f32)
    bi = b_im.astype(f32)
    bb_re = f_re[..., None] * br - f_im[..., None] * bi
    bb_im = f_re[..., None] * bi + f_im[..., None] * br
    abb_re = ab_re[..., None] * bb_re - ab_im[..., None] * bb_im
    abb_im = ab_re[..., None] * bb_im + ab_im[..., None] * bb_re
    cr = c_re.astype(f32)
    ci = c_im.astype(f32)
    ca_re = cr * ab_re[:, None, :] - ci * ab_im[:, None, :]
    ca_im = cr * ab_im[:, None, :] + ci * ab_re[:, None, :]
    cb = jnp.einsum('gop,gpc->goc', cr, bb_re) - jnp.einsum('gop,gpc->goc', ci, bb_im)

    def b_tiles(bb):
        bt = jnp.transpose(bb, (0, 2, 1)).reshape(8, 4, S5_GROUP, S5_STATE)
        gl = jnp.arange(8)[None, :, None]
        gs = jnp.arange(4)[None, None, :]
        nn = jnp.arange(8)[:, None, None]
        sel = (gl == 4 * (nn % 2) + gs).astype(f32)
        t = jnp.einsum('ngs,nscp->ngcsp', sel, bt)
        return t.reshape(8, LANE, 4 * S5_STATE)

    bmat = jnp.concatenate(
        [jnp.concatenate([b_tiles(abb_re), b_tiles(bb_re)], axis=1),
         jnp.concatenate([b_tiles(abb_im), b_tiles(bb_im)], axis=1)], axis=0).astype(bf16)

    def c_tiles(c, sign):
        ct = jnp.transpose(c, (0, 2, 1)).reshape(2, 16, S5_STATE, S5_GROUP)
        eye = jnp.eye(16, dtype=f32)
        t = jnp.einsum('gh,jgpo->jgpho', eye, ct) * sign
        return t.reshape(2, 16 * S5_STATE, 16 * S5_GROUP)

    cmat = jnp.stack([c_tiles(cr, 1.0), c_tiles(ci, -1.0)], axis=1).astype(bf16)
    camat = jnp.stack([c_tiles(ca_re, 1.0), c_tiles(ca_im, -1.0)], axis=1).astype(bf16)
    cbt = jnp.transpose(cb, (0, 2, 1)).reshape(2, 16, S5_GROUP, S5_GROUP)
    cbmat = jnp.einsum('gh,jgco->jgcho', jnp.eye(16, dtype=f32), cbt).reshape(2, 256, 256).astype(bf16)
    a2_re = (ab_re * ab_re - ab_im * ab_im).reshape(N_SLAB, 1, LANE)
    a2_im = (2.0 * ab_re * ab_im).reshape(N_SLAB, 1, LANE)
    return bmat, cmat, camat, cbmat, a2_re, a2_im


def _s5_body(u_ref, bm_ref, cm_ref, cam_ref, cbm_ref, are_ref, aim_ref, d_ref, y_ref,
             hbuf, hstate, zcarry):
    tt = u_ref.shape[1]
    tp = tt // 2
    rows = SUBLANE * tp
    pitch = tp + S5_PITCH_PAD

    @pl.when(pl.program_id(0) == 0)
    def _():
        hstate[...] = jnp.zeros_like(hstate)
        zcarry[...] = jnp.zeros_like(zcarry)

    w = pltpu.bitcast(u_ref[...].reshape(SUBLANE * tt, D_S5), u32)
    ue_f = pltpu.bitcast(w << 16, f32)
    uo_f = pltpu.bitcast(w & jnp.uint32(0xFFFF0000), f32)
    ue = ue_f.astype(bf16)
    uo = uo_f.astype(bf16)

    def put(slab, val):
        for b in range(SUBLANE):
            hbuf[slab, b * pitch:b * pitch + tp, :] = val[b * tp:(b + 1) * tp, :]

    def get(slab):
        return jnp.concatenate([hbuf[slab, b * pitch:b * pitch + tp, :] for b in range(SUBLANE)], axis=0)

    for n in range(2 * 8):
        ks = LANE * ((n % 8) // 2)
        lhs = jnp.concatenate([ue[:, ks:ks + LANE], uo[:, ks:ks + LANE]], axis=1)
        res = jnp.dot(lhs, bm_ref[n], preferred_element_type=f32)
        put(2 * n, res[:, :LANE])
        put(2 * n + 1, res[:, LANE:])

    for c0 in range(0, N_SLAB, S5_SLABS):
        ar = [jnp.broadcast_to(are_ref[c0 + s], (SUBLANE, LANE)) for s in range(S5_SLABS)]
        ai = [jnp.broadcast_to(aim_ref[c0 + s], (SUBLANE, LANE)) for s in range(S5_SLABS)]

        def step(t, carry):
            out = []
            for s in range(S5_SLABS):
                hr, hi = carry[2 * s], carry[2 * s + 1]
                sl = pl.ds(t, SUBLANE, stride=pitch)
                bur = hbuf[c0 + s, sl, :]
                bui = hbuf[N_SLAB + c0 + s, sl, :]
                nr = ar[s] * hr - ai[s] * hi + bur
                ni = ar[s] * hi + ai[s] * hr + bui
                hbuf[c0 + s, sl, :] = nr
                hbuf[N_SLAB + c0 + s, sl, :] = ni
                out += [nr, ni]
            return tuple(out)

        init = []
        for s in range(S5_SLABS):
            init += [hstate[c0 + s], hstate[N_SLAB + c0 + s]]
        fin = lax.fori_loop(0, tp, step, tuple(init), unroll=4)
        for s in range(S5_SLABS):
            hstate[c0 + s] = fin[2 * s]
            hstate[N_SLAB + c0 + s] = fin[2 * s + 1]

    first = lax.broadcasted_iota(i32, (rows, 256), 0) % tp == 0
    for j in range(2):
        h_re = jnp.concatenate([get(8 * j + s) for s in range(8)], axis=1).astype(bf16)
        h_im = jnp.concatenate([get(N_SLAB + 8 * j + s) for s in range(8)], axis=1).astype(bf16)
        cs = slice(256 * j, 256 * j + 256)
        dj = d_ref[:, cs]
        yo = jnp.dot(h_re, cm_ref[j, 0], preferred_element_type=f32)
        yo = yo + jnp.dot(h_im, cm_ref[j, 1], preferred_element_type=f32)
        yo = yo + dj * uo_f[:, cs]
        z = jnp.dot(h_re, cam_ref[j, 0], preferred_element_type=f32)
        z = z + jnp.dot(h_im, cam_ref[j, 1], preferred_element_type=f32)
        prev = jnp.concatenate([jnp.broadcast_to(zcarry[b:b + 1, cs], (tp, 256))
                                for b in range(SUBLANE)], axis=0)
        zs = jnp.where(first, prev, pltpu.roll(z, 1, 0))
        zcarry[:, cs] = jnp.concatenate([z[(b + 1) * tp - 1:(b + 1) * tp, :] for b in range(SUBLANE)], axis=0)
        ye = zs + jnp.dot(ue[:, cs], cbm_ref[j], preferred_element_type=f32) + dj * ue_f[:, cs]
        ge = pltpu.bitcast(jax.nn.gelu(ye).astype(bf16).astype(f32), u32)
        go = pltpu.bitcast(jax.nn.gelu(yo).astype(bf16).astype(f32), u32)
        packed = (ge >> 16) | go
        y_ref[:, :, cs] = pltpu.bitcast(packed, bf16).reshape(SUBLANE, tt, 256)


def _s5(l, u3, bmat, cmat, camat, cbmat, a2_re, a2_im, d_skip):
    bsz, seq, _ = u3.shape
    assert bsz == SUBLANE
    tt = TT_S5
    blk = lambda i: (0, i, 0)
    return pl.pallas_call(
        _s5_body,
        grid=(seq // tt,),
        in_specs=[pl.BlockSpec((bsz, tt, D_S5), blk)]
        + [_layer(a, l) for a in (bmat, cmat, camat, cbmat, a2_re, a2_im, d_skip)],
        out_specs=pl.BlockSpec((bsz, tt, D_S5), blk),
        out_shape=SDS((bsz, seq, D_S5), bf16),
        scratch_shapes=[pltpu.VMEM((2 * N_SLAB, bsz * (tt // 2 + S5_PITCH_PAD), LANE), f32),
                        pltpu.VMEM((2 * N_SLAB, SUBLANE, LANE), f32),
                        pltpu.VMEM((SUBLANE, D_S5), f32)],
        compiler_params=_cparams("arbitrary"),
        name="s5_scan",
    )(u3, bmat, cmat, camat, cbmat, a2_re, a2_im, d_skip)


def _split2(x):
    hi = x.astype(bf16)
    lo = (x - hi.astype(f32)).astype(bf16)
    return hi, lo


def _log_sigmoid(x):
    return -(jnp.maximum(-x, 0.0) + jnp.log1p(jnp.exp(-jnp.abs(x))))


def _gla_body(q_ref, kt_ref, v_ref, r_ref, g_ref, wa_ref, ba_ref, gn_ref,
              o_ref, s_ref):
    c = GLA_CHUNK
    tg = q_ref.shape[0]

    @pl.when(pl.program_id(1) == 0)
    def _():
        s_ref[...] = jnp.zeros_like(s_ref)

    ts = GLA_SUB
    nc = ts // c
    nh = GLA_HEADS
    ri = lax.broadcasted_iota(i32, (ts, ts), 0)
    ci = lax.broadcasted_iota(i32, (ts, ts), 1)
    same = (ri // c) == (ci // c)
    tril = (same & (ri >= ci)).astype(bf16)
    triu = (same & (ri <= ci)).astype(bf16)
    blk = same.astype(bf16)
    head_of_lane = lax.broadcasted_iota(i32, (c, D_GLA_K), 1) // GLA_DK
    r_idx = lax.broadcasted_iota(i32, (nc * nh * c, ts), 0)
    c_idx = lax.broadcasted_iota(i32, (nc * nh * c, ts), 1)
    causal = (r_idx // (nh * c) == c_idx // c) & (r_idx % c >= c_idx % c)
    chunk_of_lane = lax.broadcasted_iota(i32, (D_GLA_K, ts), 1) // c

    state = s_ref[...]
    for st in range(tg // ts):
        rs = slice(st * ts, (st + 1) * ts)
        g = g_ref[rs, :]
        la = _log_sigmoid(jnp.dot(g, wa_ref[...], preferred_element_type=f32) + ba_ref[...]) / GLA_TAU
        lat = la.T
        la_hi, la_lo = _split2(la)
        cum = (jnp.dot(tril, la_hi, preferred_element_type=f32)
               + jnp.dot(tril, la_lo, preferred_element_type=f32))
        lat_hi, lat_lo = _split2(lat)
        cumt = (jnp.dot(lat_hi, triu, preferred_element_type=f32)
                + jnp.dot(lat_lo, triu, preferred_element_type=f32))
        clt = (jnp.dot(lat_hi, blk, preferred_element_type=f32)
               + jnp.dot(lat_lo, blk, preferred_element_type=f32))

        qt = q_ref[rs, :].astype(f32) * jnp.exp(cum)
        ktt = kt_ref[:, rs].astype(f32)
        k_t = (ktt * jnp.exp(-cumt)).astype(bf16)
        k_end = ktt * jnp.exp(clt - cumt)
        v = v_ref[rs, :]

        q_stack = jnp.concatenate(
            [jnp.where(head_of_lane == h, qt[cc * c:(cc + 1) * c, :], 0.0)
             for cc in range(nc) for h in range(nh)], axis=0).astype(bf16)
        scores = jnp.dot(q_stack, k_t, preferred_element_type=f32)
        scores = jnp.where(causal, scores, 0.0).astype(bf16)
        o_full = jnp.dot(scores, v, preferred_element_type=f32)

        k_stack = jnp.concatenate(
            [jnp.where(chunk_of_lane == cc, k_end, 0.0) for cc in range(nc)], axis=0).astype(bf16)
        upd_full = jnp.dot(k_stack, v, preferred_element_type=f32)

        o_inter = []
        for cc in range(nc):
            o_inter.append(jnp.dot(q_stack[cc * nh * c:(cc + 1) * nh * c, :], state.astype(bf16),
                                   preferred_element_type=f32))
            upd = jnp.concatenate(
                [upd_full[cc * D_GLA_K + h * GLA_DK:cc * D_GLA_K + (h + 1) * GLA_DK,
                          h * GLA_DV:(h + 1) * GLA_DV] for h in range(nh)], axis=0)
            state = jnp.exp(clt[:, cc * c:cc * c + 1]) * state + upd

        rows = []
        for cc in range(nc):
            outs = []
            for h in range(nh):
                r0 = (cc * nh + h) * c
                o = o_full[r0:r0 + c, h * GLA_DV:(h + 1) * GLA_DV] + o_inter[cc][h * c:(h + 1) * c, :]
                o = o * lax.rsqrt(jnp.mean(o * o, axis=-1, keepdims=True) + EPS)
                outs.append(o)
            rows.append(jnp.concatenate(outs, axis=1))
        o_cat = jnp.concatenate(rows, axis=0)
        o_ref[rs, :] = (o_cat * gn_ref[...] * jax.nn.silu(r_ref[rs, :].astype(f32))).astype(bf16)
    s_ref[...] = state


def _gla(l, q3, kt3, v3, r3, g3, wa, ba, gn):
    bsz, seq, _ = q3.shape
    tg = TG_GLA
    tok = lambda b, i: (b, i, 0)
    return pl.pallas_call(
        _gla_body,
        grid=(bsz, seq // tg),
        in_specs=[pl.BlockSpec((None, tg, D_GLA_K), tok),
                  pl.BlockSpec((None, D_GLA_K, tg), lambda b, i: (b, 0, i)),
                  pl.BlockSpec((None, tg, D_GLA), tok),
                  pl.BlockSpec((None, tg, D_GLA), tok),
                  pl.BlockSpec((None, tg, LANE), tok),
                  _layer(wa, l), _layer(ba, l), _layer(gn, l)],
        out_specs=pl.BlockSpec((None, tg, D_GLA), tok),
        out_shape=SDS((bsz, seq, D_GLA), bf16),
        scratch_shapes=[pltpu.VMEM((D_GLA_K, GLA_DV), f32)],
        compiler_params=_cparams("parallel", "arbitrary"),
        name="gla",
    )(q3, kt3, v3, r3, g3, wa, ba, gn)


def _mix_out(x_ref, ys_ref, yg_ref, wgl_ref, bg_ref, gs_ref, wo_ref):
    y = ys_ref[...]
    z = jnp.dot(y, wgl_ref[...], preferred_element_type=f32) + bg_ref[...]
    yf = y.astype(f32) * jax.nn.sigmoid(z)
    ys = _rms(yf, gs_ref[...]).astype(bf16)
    acc = jnp.dot(ys, wo_ref[0:D_S5, :], preferred_element_type=f32)
    acc = acc + jnp.dot(yg_ref[...], wo_ref[D_S5:, :], preferred_element_type=f32)
    return x_ref[...] + acc


def _mix_specs(l, tm, w_glu, b_glu, g_s5, w_out):
    row = lambda i: (i, 0)
    once = dict(pipeline_mode=pl.Buffered(1))
    return [pl.BlockSpec((tm, D_MODEL), row),
            pl.BlockSpec((tm, D_S5), row),
            pl.BlockSpec((tm, D_GLA), row),
            _layer(w_glu, l, **once), _layer(b_glu, l), _layer(g_s5, l), _layer(w_out, l, **once)]


def _mix_ffn_body(x_ref, ys_ref, yg_ref, wgl_ref, bg_ref, gs_ref, wo_ref,
                  g_ref, wg_ref, wu_ref, wd_ref, o_ref):
    x = _mix_out(x_ref, ys_ref, yg_ref, wgl_ref, bg_ref, gs_ref, wo_ref)
    hn = _rms(x, g_ref[...]).astype(bf16)
    gate = jnp.dot(hn, wg_ref[...], preferred_element_type=f32)
    up = jnp.dot(hn, wu_ref[...], preferred_element_type=f32)
    act = (jax.nn.silu(gate) * up).astype(bf16)
    o_ref[...] = x + jnp.dot(act, wd_ref[...], preferred_element_type=f32)


def _mix_ffn(l, i_ffn, x2, ys, yg, w_glu, b_glu, g_s5, w_out, g, wg, wu, wd):
    n = x2.shape[0]
    tm = TM_FFN
    row = lambda i: (i, 0)
    once = dict(pipeline_mode=pl.Buffered(1))
    return pl.pallas_call(
        _mix_ffn_body,
        grid=(n // tm,),
        in_specs=_mix_specs(l, tm, w_glu, b_glu, g_s5, w_out) + [
            _layer(g, l), _layer(wg, i_ffn, **once), _layer(wu, i_ffn, **once),
            _layer(wd, i_ffn, **once)],
        out_specs=pl.BlockSpec((tm, D_MODEL), row),
        out_shape=SDS((n, D_MODEL), f32),
        compiler_params=_cparams("parallel"),
        name="mix_ffn",
    )(x2, ys, yg, w_glu, b_glu, g_s5, w_out, g, wg, wu, wd)


def _pack_bf16_pairs(a):
    bits = pltpu.bitcast(a.astype(bf16).astype(f32), u32)
    half = a.shape[1] // 2
    return bits[:, :half] | (bits[:, half:] >> 16)


def _unpack_bf16_pairs(p):
    hi = pltpu.bitcast(p & jnp.uint32(0xFFFF0000), f32).astype(bf16)
    lo = pltpu.bitcast(p << 16, f32).astype(bf16)
    return hi, lo


def _mix_router_body(x_ref, ys_ref, yg_ref, wgl_ref, bg_ref, gs_ref, wo_ref,
                     g_ref, wh_ref, wl_ref, xo_ref, hp_ref, meta_ref, wcol_ref, cnt_ref, carry):
    tm = x_ref.shape[0]

    @pl.when(pl.program_id(0) == 0)
    def _():
        carry[...] = jnp.zeros_like(carry)

    x = _mix_out(x_ref, ys_ref, yg_ref, wgl_ref, bg_ref, gs_ref, wo_ref)
    xo_ref[...] = x
    hn = _rms(x, g_ref[...])
    packed = _pack_bf16_pairs(hn)
    hp_ref[0] = packed[:, :SC_ROW]
    hp_ref[1] = packed[:, SC_ROW:]

    h_hi, h_lo = _split2(hn)
    nt = (((1,), (1,)), ((), ()))
    logits = (lax.dot_general(wh_ref[...], h_hi, nt, preferred_element_type=f32)
              + lax.dot_general(wh_ref[...], h_lo, nt, preferred_element_type=f32)
              + lax.dot_general(wl_ref[...], h_hi, nt, preferred_element_type=f32))
    er = logits.shape[0]
    row = lax.broadcasted_iota(i32, (er, tm), 0)
    neg = jnp.float32(-jnp.inf)
    logits = jnp.where(row < N_EXPERTS, logits, neg)
    m1 = jnp.max(logits, axis=0, keepdims=True)
    i1 = jnp.min(jnp.where(logits == m1, row, er), axis=0, keepdims=True)
    l2 = jnp.where(row == i1, neg, logits)
    m2 = jnp.max(l2, axis=0, keepdims=True)
    i2 = jnp.min(jnp.where(l2 == m2, row, er), axis=0, keepdims=True)
    e21 = jnp.exp(m2 - m1)
    w1 = 1.0 / (1.0 + e21)
    w2 = e21 / (1.0 + e21)

    sel1 = row == i1
    sel2 = row == i2
    sel = (sel1 | sel2).astype(f32)
    ri = lax.broadcasted_iota(i32, (tm, tm), 0)
    ci = lax.broadcasted_iota(i32, (tm, tm), 1)
    triu = (ri <= ci).astype(bf16)
    incl = jnp.dot(sel.astype(bf16), triu, preferred_element_type=f32)
    rank = incl - sel + carry[:, 0:1]
    r1 = jnp.sum(jnp.where(sel1, rank, 0.0), axis=0, keepdims=True)
    r2 = jnp.sum(jnp.where(sel2, rank, 0.0), axis=0, keepdims=True)
    new_cnt = carry[:, 0:1] + incl[:, tm - 1:tm]
    carry[...] = jnp.broadcast_to(new_cnt, carry.shape)
    cnt_ref[...] = jnp.broadcast_to(new_cnt, cnt_ref.shape)

    srow = lax.broadcasted_iota(i32, (SUBLANE, tm), 0)
    meta = jnp.where(srow == 0, i1.astype(f32), 0.0)
    meta = jnp.where(srow == 1, i2.astype(f32), meta)
    meta = jnp.where(srow == 2, r1, meta)
    meta = jnp.where(srow == 3, r2, meta)
    meta_ref[...] = meta
    prow = lax.broadcasted_iota(i32, (LANE, tm), 0)
    wpad = jnp.where(prow == 0, w1, jnp.where(prow == 1, w2, 0.0))
    wcol_ref[...] = wpad.T


def _mix_router(l, x2, ys, yg, w_glu, b_glu, g_s5, w_out, g, w_hi, w_lo):
    n = x2.shape[0]
    tm = TM_ROUTE
    row = lambda i: (i, 0)
    const = lambda i: (0, 0)
    return pl.pallas_call(
        _mix_router_body,
        grid=(n // tm,),
        in_specs=_mix_specs(l, tm, w_glu, b_glu, g_s5, w_out) + [
            _layer(g, l),
            pl.BlockSpec(w_hi.shape, const),
            pl.BlockSpec(w_lo.shape, const)],
        out_specs=[pl.BlockSpec((tm, D_MODEL), row),
                   pl.BlockSpec((2, tm, SC_ROW), lambda i: (0, i, 0)),
                   pl.BlockSpec((SUBLANE, tm), lambda i: (0, i)),
                   pl.BlockSpec((tm, LANE), row),
                   pl.BlockSpec(w_hi.shape[:1] + (LANE,), const)],
        out_shape=[SDS((n, D_MODEL), f32), SDS((2, n, SC_ROW), u32), SDS((SUBLANE, n), f32),
                   SDS((n, LANE), f32), SDS(w_hi.shape[:1] + (LANE,), f32)],
        scratch_shapes=[pltpu.VMEM(w_hi.shape[:1] + (LANE,), f32)],
        compiler_params=_cparams("arbitrary"),
        name="mix_router",
    )(x2, ys, yg, w_glu, b_glu, g_s5, w_out, g, w_hi, w_lo)


def _sc_gather(table, idx):
    ni = idx.shape[0]
    mesh = plsc.VectorSubcoreMesh(core_axis_name="core", subcore_axis_name="subcore")
    idx2 = idx.reshape(1, ni)

    @pl.kernel(out_type=SDS((ni, SC_ROW), table.dtype), mesh=mesh)
    def kern(t_hbm, i_hbm, o_hbm):
        def body(i_vmem, o_vmem):
            pltpu.sync_copy(t_hbm.at[i_vmem.at[0]], o_vmem)

        pltpu.emit_pipeline(
            body, grid=(ni // SC_WINDOW,),
            in_specs=[pl.BlockSpec((1, SC_WINDOW), index_map=lambda i: (0, i))],
            out_specs=[pl.BlockSpec((SC_WINDOW, SC_ROW), index_map=lambda i: (i, 0))],
            core_axis_name=("core", "subcore"),
            dimension_semantics=(pltpu.PARALLEL,),
        )(i_hbm, o_hbm)

    return kern(table, idx2)


def _sc_scatter2(x, idx0, idx1, nrows):
    ni = x.shape[0]
    mesh = plsc.VectorSubcoreMesh(core_axis_name="core", subcore_axis_name="subcore")

    @pl.kernel(out_type=SDS((nrows, SC_ROW), x.dtype), mesh=mesh)
    def kern(x_hbm, i0_hbm, i1_hbm, o_hbm):
        def body(x_vmem, i0_vmem, i1_vmem):
            pltpu.sync_copy(x_vmem, o_hbm.at[i0_vmem.at[0]])
            pltpu.sync_copy(x_vmem, o_hbm.at[i1_vmem.at[0]])

        pltpu.emit_pipeline(
            body, grid=(ni // SC_WINDOW,),
            in_specs=[pl.BlockSpec((SC_WINDOW, SC_ROW), index_map=lambda i: (i, 0)),
                      pl.BlockSpec((1, SC_WINDOW), index_map=lambda i: (0, i)),
                      pl.BlockSpec((1, SC_WINDOW), index_map=lambda i: (0, i))],
            out_specs=[],
            core_axis_name=("core", "subcore"),
            dimension_semantics=(pltpu.PARALLEL,),
        )(x_hbm, i0_hbm, i1_hbm)

    return kern(x, idx0.reshape(1, ni), idx1.reshape(1, ni))


def _moe_ffn_body(be_ref, bn_ref, xs_ref, wg_ref, wu_ref, wd_ref, ys_ref, xb, act0, act1, acc):
    i = pl.program_id(0)
    f = pl.program_id(1)
    nf = pl.num_programs(1) - 1
    nvalid = bn_ref[i]
    half = xs_ref.shape[1] // MOE_SUB

    for h in range(MOE_SUB):
        rs = pl.ds(h * half, half)
        live_half = nvalid > h * half

        def gate_up(rs=rs):
            x = xb[rs, :]
            gate = jnp.dot(x, wg_ref[...].astype(bf16), preferred_element_type=f32)
            up = jnp.dot(x, wu_ref[...].astype(bf16), preferred_element_type=f32)
            return (jax.nn.silu(gate) * up).astype(bf16)

        def down(act_ref, rs=rs):
            return jnp.dot(act_ref[rs, :], wd_ref[...].astype(bf16), preferred_element_type=f32)

        @pl.when(live_half & (f == 0))
        def _(rs=rs, h=h, gate_up=gate_up):
            live = lax.broadcasted_iota(i32, (half, SC_ROW), 0) + h * half < nvalid
            hi0, lo0 = _unpack_bf16_pairs(jnp.where(live, xs_ref[0, rs, :], jnp.uint32(0)))
            hi1, lo1 = _unpack_bf16_pairs(jnp.where(live, xs_ref[1, rs, :], jnp.uint32(0)))
            xb[rs, :] = jnp.concatenate([hi0, hi1, lo0, lo1], axis=1)
            acc[rs, :] = jnp.zeros((half, D_MODEL), f32)
            act0[rs, :] = gate_up()

        @pl.when(live_half & (f > 0) & (f < nf) & (f % 2 == 1))
        def _(rs=rs, gate_up=gate_up, down=down):
            act1[rs, :] = gate_up()
            acc[rs, :] += down(act0)

        @pl.when(live_half & (f > 0) & (f < nf) & (f % 2 == 0))
        def _(rs=rs, gate_up=gate_up, down=down):
            act0[rs, :] = gate_up()
            acc[rs, :] += down(act1)

        @pl.when(live_half & (f == nf))
        def _(rs=rs, down=down):
            last = act0 if (D_FF_EXPERT // TF_MOE - 1) % 2 == 0 else act1
            packed = _pack_bf16_pairs(acc[rs, :] + down(last))
            ys_ref[0, rs, :] = packed[:, :SC_ROW]
            ys_ref[1, rs, :] = packed[:, SC_ROW:]

        @pl.when(jnp.logical_not(live_half) & (f == nf))
        def _(rs=rs):
            ys_ref[:, rs, :] = jnp.zeros((2, half, SC_ROW), u32)


def _moe_ffn(i_moe, blk_e, blk_n, xs, wg, wu, wd):
    npad = xs.shape[1]
    tm, tf = TM_MOE, TF_MOE
    nblk = npad // tm
    nf = D_FF_EXPERT // tf

    def nxt(i):
        return jnp.minimum(i + 1, nblk - 1)

    def x_idx(i, f, be, bn):
        return (0, jnp.where(f == nf, nxt(i), i), 0)

    def gu_idx(i, f, be, bn):
        ahead = (f == nf) & (bn[i] > 0) & (bn[nxt(i)] > 0)
        e = jnp.where(ahead, be[nxt(i)], be[i])
        t = jnp.where(ahead, 0, jnp.where(bn[i] > 0, jnp.minimum(f, nf - 1), nf - 1))
        return (i_moe, e, 0, t)

    def d_idx(i, f, be, bn):
        prev = jnp.maximum(i - 1, 0)
        keep = (f == 0) & (i > 0)
        e = jnp.where(keep, be[prev], be[i])
        t = jnp.where(keep | (bn[i] == 0), nf - 1, jnp.maximum(f - 1, 0))
        return (i_moe, e, t, 0)

    grid_spec = pltpu.PrefetchScalarGridSpec(
        num_scalar_prefetch=2,
        grid=(nblk, nf + 1),
        in_specs=[pl.BlockSpec((2, tm, SC_ROW), x_idx),
                  pl.BlockSpec((None, None, D_MODEL, tf), gu_idx),
                  pl.BlockSpec((None, None, D_MODEL, tf), gu_idx),
                  pl.BlockSpec((None, None, tf, D_MODEL), d_idx)],
        out_specs=pl.BlockSpec((2, tm, SC_ROW), lambda i, f, be, bn: (0, i, 0)),
        scratch_shapes=[pltpu.VMEM((tm, D_MODEL), bf16), pltpu.VMEM((tm, tf), bf16),
                        pltpu.VMEM((tm, tf), bf16), pltpu.VMEM((tm, D_MODEL), f32)],
    )
    return pl.pallas_call(
        _moe_ffn_body,
        grid_spec=grid_spec,
        out_shape=SDS((2, npad, SC_ROW), u32),
        compiler_params=_cparams("parallel", "arbitrary"),
        name="moe_ffn",
    )(blk_e, blk_n, xs, wg, wu, wd)


def _combine_body(x_ref, yg_ref, wcol_ref, g_ref, o_ref, *, final_norm):
    w1 = wcol_ref[:, 0:1]
    w2 = wcol_ref[:, 1:2]

    def rows(k):
        hi0, lo0 = _unpack_bf16_pairs(yg_ref[0, k])
        hi1, lo1 = _unpack_bf16_pairs(yg_ref[1, k])
        return jnp.concatenate([hi0, hi1, lo0, lo1], axis=1).astype(f32)

    xo = x_ref[...] + (w1 * rows(0) + w2 * rows(1))
    if final_norm:
        xo = _rms(xo, g_ref[...])
    o_ref[...] = xo


def _combine(x2, yg, wcol, g_final, final_norm):
    n = x2.shape[0]
    tm = TM_ROUTE
    row = lambda i: (i, 0)
    return pl.pallas_call(
        functools.partial(_combine_body, final_norm=final_norm),
        grid=(n // tm,),
        in_specs=[pl.BlockSpec((tm, D_MODEL), row),
                  pl.BlockSpec((2, 2, tm, SC_ROW), lambda i: (0, 0, i, 0)),
                  pl.BlockSpec((tm, LANE), row),
                  pl.BlockSpec((1, D_MODEL), lambda i: (0, 0))],
        out_specs=pl.BlockSpec((tm, D_MODEL), row),
        out_shape=SDS((n, D_MODEL), f32),
        compiler_params=_cparams("parallel"),
        name="moe_combine",
    )(x2, yg, wcol, g_final)


def _moe_layer(l, i_moe, mix_args, g_ffn, w_router, wg, wu, wd, g_final, final_norm):
    n = mix_args[0].shape[0]
    tm = TM_MOE
    npad = 2 * n + N_EXPERTS * tm
    wr = jnp.zeros((2 * SUBLANE, D_MODEL), f32).at[:N_EXPERTS].set(w_router[i_moe].astype(f32).T)
    wr_hi = wr.astype(bf16)
    wr_lo = (wr - wr_hi.astype(f32)).astype(bf16)
    x2, hp, meta, wcol, cnt = _mix_router(l, *mix_args, g_ffn, wr_hi, wr_lo)

    counts = cnt[:N_EXPERTS, 0].astype(i32)
    padded = ((counts + tm - 1) // tm) * tm
    ends = jnp.cumsum(padded)
    offs = ends - padded
    ids = jnp.arange(N_EXPERTS, dtype=i32)
    e12 = meta[0:2].astype(i32)
    r12 = meta[2:4].astype(i32)
    pos_t = r12 + jnp.sum(jnp.where(e12[..., None] == ids, offs, 0), axis=-1)
    blk_start = jnp.arange(npad // tm, dtype=i32) * tm
    blk_e = jnp.minimum(jnp.sum((blk_start[:, None] >= ends[None, :]).astype(i32), axis=1), N_EXPERTS - 1)
    own = blk_e[:, None] == ids
    blk_end = jnp.sum(jnp.where(own, offs + counts, 0), axis=1)
    blk_n = jnp.where(blk_start < ends[-1], jnp.clip(blk_end - blk_start, 0, tm), 0)
    last_e = jnp.max(jnp.where(blk_n > 0, blk_e, 0))
    blk_e = jnp.where(blk_n > 0, blk_e, last_e)

    xs = _sc_scatter2(hp.reshape(2 * n, SC_ROW),
                      jnp.concatenate([pos_t[0], pos_t[0] + npad]),
                      jnp.concatenate([pos_t[1], pos_t[1] + npad]),
                      2 * npad).reshape(2, npad, SC_ROW)
    ys = _moe_ffn(i_moe, blk_e, blk_n, xs, wg, wu, wd)
    gidx = jnp.concatenate([pos_t.reshape(-1), pos_t.reshape(-1) + npad])
    yg = _sc_gather(ys.reshape(2 * npad, SC_ROW), gidx).reshape(2, 2, n, SC_ROW)
    return _combine(x2, yg, wcol, g_final, final_norm)


def _rows(v):
    return v.astype(f32)[:, None, :]


def kernel(x, norm_mix, w_in, s5_lambda_re, s5_lambda_im, s5_log_dt, s5_b_re, s5_b_im, s5_c_re, s5_c_im, s5_d, s5_w_glu, s5_b_glu, s5_out_norm, gla_w_a2, gla_b_a2, gla_out_norm, w_out, norm_ffn, ffn_w_gate, ffn_w_up, ffn_w_down, moe_w_router, moe_w_gate, moe_w_up, moe_w_down, norm_final):
    bsz, seq, _ = x.shape
    n = bsz * seq
    depth = w_in.shape[0]
    x2 = x.reshape(n, D_MODEL)

    w_gp = jnp.zeros((depth, D_MODEL, LANE), f32).at[:, :, :GLA_GATE_RANK].set(w_in[:, :, 2048:2064])
    w_cat = jnp.concatenate([w_in[:, :, 0:512], w_in[:, :, 512:768] * (GLA_DK ** -0.5),
                             w_in[:, :, 1024:1536], w_in[:, :, 1536:2048], w_gp], axis=2).astype(bf16)
    w_kt = jnp.swapaxes(w_in[:, :, 768:1024], 1, 2).astype(bf16)
    s5_mats = jax.vmap(_s5_prep)(s5_lambda_re, s5_lambda_im, s5_log_dt, s5_b_re, s5_b_im, s5_c_re, s5_c_im)
    wa = jnp.zeros((depth, LANE, D_GLA_K), f32).at[:, :GLA_GATE_RANK].set(gla_w_a2).astype(bf16)
    g_mix, g_ffn, d_skip = _rows(norm_mix), _rows(norm_ffn), _rows(s5_d)
    ba, gn = _rows(gla_b_a2), _rows(gla_out_norm)
    mix_w = (s5_w_glu.astype(bf16), _rows(s5_b_glu), _rows(s5_out_norm), w_out.astype(bf16))
    ffn_w = (ffn_w_gate.astype(bf16), ffn_w_up.astype(bf16), ffn_w_down.astype(bf16))
    g_final = norm_final.astype(f32).reshape(1, D_MODEL)

    for l in range(depth):
        u, q, kt, v, r, gl = _inproj(l, x2, g_mix, w_cat, w_kt, bsz, seq)
        ys = _s5(l, u.reshape(bsz, seq, D_S5), *s5_mats, d_skip)
        yg = _gla(l, q.reshape(bsz, seq, D_GLA_K), kt, v.reshape(bsz, seq, D_GLA),
                  r.reshape(bsz, seq, D_GLA), gl.reshape(bsz, seq, LANE), wa, ba, gn)
        mix_args = (x2, ys.reshape(n, D_S5), yg.reshape(n, D_GLA)) + mix_w

        last = l == depth - 1
        if l % 2 == 0:
            x2 = _mix_ffn(l, l // 2, *mix_args, g_ffn, *ffn_w)
            if last:
                x2 = _final_norm(x2, g_final)
        else:
            x2 = _moe_layer(l, l // 2, mix_args, g_ffn, moe_w_router, moe_w_gate, moe_w_up,
                            moe_w_down, g_final, last)
    return x2.reshape(bsz, seq, D_MODEL)


def _final_norm_body(x_ref, g_ref, o_ref):
    o_ref[...] = _rms(x_ref[...], g_ref[...])


def _final_norm(x2, g):
    n = x2.shape[0]
    tm = TM_ROUTE
    return pl.pallas_call(
        _final_norm_body,
        grid=(n // tm,),
        in_specs=[pl.BlockSpec((tm, D_MODEL), lambda i: (i, 0)),
                  pl.BlockSpec((1, D_MODEL), lambda i: (0, 0))],
        out_specs=pl.BlockSpec((tm, D_MODEL), lambda i: (i, 0)),
        out_shape=SDS((n, D_MODEL), f32),
        compiler_params=_cparams("parallel"),
        name="final_norm",
    )(x2, g)
```

```python
import functools
import math

import jax
import jax.numpy as jnp
from jax import lax
from jax.experimental import pallas as pl
from jax.experimental.pallas import tpu as pltpu
from jax.experimental.pallas import tpu_sc as plsc

f32 = jnp.float32
bf16 = jnp.bfloat16
u32 = jnp.uint32
i32 = jnp.int32
SDS = jax.ShapeDtypeStruct

D_MODEL = 1024
D_S5 = 512
S5_GROUP = 16
S5_GROUPS = 32
S5_STATE = 64
N_STATE = S5_GROUPS * S5_STATE
D_GLA = 512
GLA_HEADS = 4
GLA_DV = 128
GLA_DK = 64
D_GLA_K = 256
GLA_GATE_RANK = 16
GLA_TAU = 16.0
GLA_CHUNK = 64
D_FF = 2816
N_EXPERTS = 8
D_FF_EXPERT = 3584
EPS = 1e-6

LANE = 128
SUBLANE = 8
VMEM_LIMIT = 56 * 1024 * 1024

TM_PROJ = 1024
TT_S5 = 128
S5_SLABS = 4
S5_PITCH_PAD = 8
TG_GLA = 512
GLA_SUB = 256
TM_FFN = 512
TM_MOE = 2048
MOE_SUB = 2
TF_MOE = 512
TM_ROUTE = 512
MOE_PARTS = 4
SC_WINDOW = 128
SC_ROW = 256


def _cparams(*sem):
    return pltpu.CompilerParams(dimension_semantics=sem, vmem_limit_bytes=VMEM_LIMIT)


def _layer(arr, l, **kw):
    return pl.BlockSpec((None,) + arr.shape[1:], lambda *_: (l,) + (0,) * (arr.ndim - 1), **kw)


def _rms(x, g):
    ms = jnp.mean(x * x, axis=-1, keepdims=True)
    return x * lax.rsqrt(ms + EPS) * g


def _inproj_body(x_ref, g_ref, w_ref, wkt_ref, u_ref, q_ref, kt_ref, v_ref, r_ref, gl_ref):
    hn = _rms(x_ref[...], g_ref[...]).astype(bf16)

    def proj(lo, hi):
        return jnp.dot(hn, w_ref[:, lo:hi], preferred_element_type=f32).astype(bf16)

    u_ref[...] = proj(0, 512)
    q_ref[...] = proj(512, 768)
    v_ref[...] = proj(768, 1280)
    r_ref[...] = proj(1280, 1792)
    gl_ref[...] = proj(1792, 1920)
    kt_ref[...] = lax.dot_general(wkt_ref[...], hn, (((1,), (1,)), ((), ())),
                                  preferred_element_type=f32).astype(bf16)


def _inproj(l, x2, g, w_cat, w_kt, bsz, seq):
    n = x2.shape[0]
    tm = TM_PROJ
    per_b = seq // tm
    row = lambda i: (i, 0)
    return pl.pallas_call(
        _inproj_body,
        grid=(n // tm,),
        in_specs=[pl.BlockSpec((tm, D_MODEL), row),
                  _layer(g, l), _layer(w_cat, l), _layer(w_kt, l)],
        out_specs=[pl.BlockSpec((tm, D_S5), row),
                   pl.BlockSpec((tm, D_GLA_K), row),
                   pl.BlockSpec((None, D_GLA_K, tm), lambda i: (i // per_b, 0, i % per_b)),
                   pl.BlockSpec((tm, D_GLA), row),
                   pl.BlockSpec((tm, D_GLA), row),
                   pl.BlockSpec((tm, LANE), row)],
        out_shape=[SDS((n, D_S5), bf16), SDS((n, D_GLA_K), bf16),
                   SDS((bsz, D_GLA_K, seq), bf16), SDS((n, D_GLA), bf16),
                   SDS((n, D_GLA), bf16), SDS((n, LANE), bf16)],
        compiler_params=_cparams("parallel"),
        name="inproj",
    )(x2, g, w_cat, w_kt)


N_SLAB = N_STATE // LANE


def _s5_prep(lam_re, lam_im, log_dt, b_re, b_im, c_re, c_im):
    lr = jnp.minimum(lam_re.astype(f32), -1e-4)
    li = lam_im.astype(f32)
    dt = jnp.exp(log_dt.astype(f32))[:, None]
    mag = jnp.exp(lr * dt)
    ab_re = mag * jnp.cos(li * dt)
    ab_im = mag * jnp.sin(li * dt)
    nr = ab_re - 1.0
    ni = ab_im
    den = lr * lr + li * li
    f_re = (nr * lr + ni * li) / den
    f_im = (ni * lr - nr * li) / den
    br = b_re.astype(f32)
    bi = b_im.astype(f32)
    bb_re = f_re[..., None] * br - f_im[..., None] * bi
    bb_im = f_re[..., None] * bi + f_im[..., None] * br
    abb_re = ab_re[..., None] * bb_re - ab_im[..., None] * bb_im
    abb_im = ab_re[..., None] * bb_im + ab_im[..., None] * bb_re
    cr = c_re.astype(f32)
    ci = c_im.astype(f32)
    ca_re = cr * ab_re[:, None, :] - ci * ab_im[:, None, :]
    ca_im = cr * ab_im[:, None, :] + ci * ab_re[:, None, :]
    cb = jnp.einsum('gop,gpc->goc', cr, bb_re) - jnp.einsum('gop,gpc->goc', ci, bb_im)

    def b_tiles(bb):
        bt = jnp.transpose(bb, (0, 2, 1)).reshape(8, 4, S5_GROUP, S5_STATE)
        gl = jnp.arange(8)[None, :, None]
        gs = jnp.arange(4)[None, None, :]
        nn = jnp.arange(8)[:, None, None]
        sel = (gl == 4 * (nn % 2) + gs).astype(f32)
        t = jnp.einsum('ngs,nscp->ngcsp', sel, bt)
        return t.reshape(8, LANE, 4 * S5_STATE)

    bmat = jnp.concatenate(
        [jnp.concatenate([b_tiles(abb_re), b_tiles(bb_re)], axis=1),
         jnp.concatenate([b_tiles(abb_im), b_tiles(bb_im)], axis=1)], axis=0).astype(bf16)

    def c_tiles(c, sign):
        ct = jnp.transpose(c, (0, 2, 1)).reshape(2, 16, S5_STATE, S5_GROUP)
        eye = jnp.eye(16, dtype=f32)
        t = jnp.einsum('gh,jgpo->jgpho', eye, ct) * sign
        return t.reshape(2, 16 * S5_STATE, 16 * S5_GROUP)

    cmat = jnp.stack([c_tiles(cr, 1.0), c_tiles(ci, -1.0)], axis=1).astype(bf16)
    camat = jnp.stack([c_tiles(ca_re, 1.0), c_tiles(ca_im, -1.0)], axis=1).astype(bf16)
    cbt = jnp.transpose(cb, (0, 2, 1)).reshape(2, 16, S5_GROUP, S5_GROUP)
    cbmat = jnp.einsum('gh,jgco->jgcho', jnp.eye(16, dtype=f32), cbt).reshape(2, 256, 256).astype(bf16)
    a2_re = (ab_re * ab_re - ab_im * ab_im).reshape(N_SLAB, 1, LANE)
    a2_im = (2.0 * ab_re * ab_im).reshape(N_SLAB, 1, LANE)
    return bmat, cmat, camat, cbmat, a2_re, a2_im


def _s5_body(u_ref, bm_ref, cm_ref, cam_ref, cbm_ref, are_ref, aim_ref, d_ref, y_ref,
             hbuf, hstate, zcarry):
    tt = u_ref.shape[1]
    tp = tt // 2
    rows = SUBLANE * tp
    pitch = tp + S5_PITCH_PAD

    @pl.when(pl.program_id(0) == 0)
    def _():
        hstate[...] = jnp.zeros_like(hstate)
        zcarry[...] = jnp.zeros_like(zcarry)

    w = pltpu.bitcast(u_ref[...].reshape(SUBLANE * tt, D_S5), u32)
    ue_f = pltpu.bitcast(w << 16, f32)
    uo_f = pltpu.bitcast(w & jnp.uint32(0xFFFF0000), f32)
    ue = ue_f.astype(bf16)
    uo = uo_f.astype(bf16)

    def put(slab, val):
        for b in range(SUBLANE):
            hbuf[slab, b * pitch:b * pitch + tp, :] = val[b * tp:(b + 1) * tp, :]

    def get(slab):
        return jnp.concatenate([hbuf[slab, b * pitch:b * pitch + tp, :] for b in range(SUBLANE)], axis=0)

    for n in range(2 * 8):
        ks = LANE * ((n % 8) // 2)
        lhs = jnp.concatenate([ue[:, ks:ks + LANE], uo[:, ks:ks + LANE]], axis=1)
        res = jnp.dot(lhs, bm_ref[n], preferred_element_type=f32)
        put(2 * n, res[:, :LANE])
        put(2 * n + 1, res[:, LANE:])

    for c0 in range(0, N_SLAB, S5_SLABS):
        ar = [jnp.broadcast_to(are_ref[c0 + s], (SUBLANE, LANE)) for s in range(S5_SLABS)]
        ai = [jnp.broadcast_to(aim_ref[c0 + s], (SUBLANE, LANE)) for s in range(S5_SLABS)]

        def step(t, carry):
            out = []
            for s in range(S5_SLABS):
                hr, hi = carry[2 * s], carry[2 * s + 1]
                sl = pl.ds(t, SUBLANE, stride=pitch)
                bur = hbuf[c0 + s, sl, :]
                bui = hbuf[N_SLAB + c0 + s, sl, :]
                nr = ar[s] * hr - ai[s] * hi + bur
                ni = ar[s] * hi + ai[s] * hr + bui
                hbuf[c0 + s, sl, :] = nr
                hbuf[N_SLAB + c0 + s, sl, :] = ni
                out += [nr, ni]
            return tuple(out)

        init = []
        for s in range(S5_SLABS):
            init += [hstate[c0 + s], hstate[N_SLAB + c0 + s]]
        fin = lax.fori_loop(0, tp, step, tuple(init), unroll=4)
        for s in range(S5_SLABS):
            hstate[c0 + s] = fin[2 * s]
            hstate[N_SLAB + c0 + s] = fin[2 * s + 1]

    first = lax.broadcasted_iota(i32, (rows, 256), 0) % tp == 0
    for j in range(2):
        h_re = jnp.concatenate([get(8 * j + s) for s in range(8)], axis=1).astype(bf16)
        h_im = jnp.concatenate([get(N_SLAB + 8 * j + s) for s in range(8)], axis=1).astype(bf16)
        cs = slice(256 * j, 256 * j + 256)
        dj = d_ref[:, cs]
        yo = jnp.dot(h_re, cm_ref[j, 0], preferred_element_type=f32)
        yo = yo + jnp.dot(h_im, cm_ref[j, 1], preferred_element_type=f32)
        yo = yo + dj * uo_f[:, cs]
        z = jnp.dot(h_re, cam_ref[j, 0], preferred_element_type=f32)
        z = z + jnp.dot(h_im, cam_ref[j, 1], preferred_element_type=f32)
        prev = jnp.concatenate([jnp.broadcast_to(zcarry[b:b + 1, cs], (tp, 256))
                                for b in range(SUBLANE)], axis=0)
        zs = jnp.where(first, prev, pltpu.roll(z, 1, 0))
        zcarry[:, cs] = jnp.concatenate([z[(b + 1) * tp - 1:(b + 1) * tp, :] for b in range(SUBLANE)], axis=0)
        ye = zs + jnp.dot(ue[:, cs], cbm_ref[j], preferred_element_type=f32) + dj * ue_f[:, cs]
        ge = pltpu.bitcast(jax.nn.gelu(ye).astype(bf16).astype(f32), u32)
        go = pltpu.bitcast(jax.nn.gelu(yo).astype(bf16).astype(f32), u32)
        packed = (ge >> 16) | go
        y_ref[:, :, cs] = pltpu.bitcast(packed, bf16).reshape(SUBLANE, tt, 256)


def _s5(l, u3, bmat, cmat, camat, cbmat, a2_re, a2_im, d_skip):
    bsz, seq, _ = u3.shape
    assert bsz == SUBLANE
    tt = TT_S5
    blk = lambda i: (0, i, 0)
    return pl.pallas_call(
        _s5_body,
        grid=(seq // tt,),
        in_specs=[pl.BlockSpec((bsz, tt, D_S5), blk)]
        + [_layer(a, l) for a in (bmat, cmat, camat, cbmat, a2_re, a2_im, d_skip)],
        out_specs=pl.BlockSpec((bsz, tt, D_S5), blk),
        out_shape=SDS((bsz, seq, D_S5), bf16),
        scratch_shapes=[pltpu.VMEM((2 * N_SLAB, bsz * (tt // 2 + S5_PITCH_PAD), LANE), f32),
                        pltpu.VMEM((2 * N_SLAB, SUBLANE, LANE), f32),
                        pltpu.VMEM((SUBLANE, D_S5), f32)],
        compiler_params=_cparams("arbitrary"),
        name="s5_scan",
    )(u3, bmat, cmat, camat, cbmat, a2_re, a2_im, d_skip)


def _split2(x):
    hi = x.astype(bf16)
    lo = (x - hi.astype(f32)).astype(bf16)
    return hi, lo


def _log_sigmoid(x):
    return -(jnp.maximum(-x, 0.0) + jnp.log1p(jnp.exp(-jnp.abs(x))))


def _gla_body(q_ref, kt_ref, v_ref, r_ref, g_ref, wa_ref, ba_ref, gn_ref,
              o_ref, s_ref):
    c = GLA_CHUNK
    tg = q_ref.shape[0]

    @pl.when(pl.program_id(1) == 0)
    def _():
        s_ref[...] = jnp.zeros_like(s_ref)

    ts = GLA_SUB
    nc = ts // c
    nh = GLA_HEADS
    ri = lax.broadcasted_iota(i32, (ts, ts), 0)
    ci = lax.broadcasted_iota(i32, (ts, ts), 1)
    same = (ri // c) == (ci // c)
    tril = (same & (ri >= ci)).astype(bf16)
    triu = (same & (ri <= ci)).astype(bf16)
    blk = same.astype(bf16)
    head_of_lane = lax.broadcasted_iota(i32, (c, D_GLA_K), 1) // GLA_DK
    r_idx = lax.broadcasted_iota(i32, (nc * nh * c, ts), 0)
    c_idx = lax.broadcasted_iota(i32, (nc * nh * c, ts), 1)
    causal = (r_idx // (nh * c) == c_idx // c) & (r_idx % c >= c_idx % c)
    chunk_of_lane = lax.broadcasted_iota(i32, (D_GLA_K, ts), 1) // c

    state = s_ref[...]
    for st in range(tg // ts):
        rs = slice(st * ts, (st + 1) * ts)
        g = g_ref[rs, :]
        la = _log_sigmoid(jnp.dot(g, wa_ref[...], preferred_element_type=f32) + ba_ref[...]) / GLA_TAU
        lat = la.T
        la_hi, la_lo = _split2(la)
        cum = (jnp.dot(tril, la_hi, preferred_element_type=f32)
               + jnp.dot(tril, la_lo, preferred_element_type=f32))
        lat_hi, lat_lo = _split2(lat)
        cumt = (jnp.dot(lat_hi, triu, preferred_element_type=f32)
                + jnp.dot(lat_lo, triu, preferred_element_type=f32))
        clt = (jnp.dot(lat_hi, blk, preferred_element_type=f32)
               + jnp.dot(lat_lo, blk, preferred_element_type=f32))

        qt = q_ref[rs, :].astype(f32) * jnp.exp(cum)
        ktt = kt_ref[:, rs].astype(f32)
        k_t = (ktt * jnp.exp(-cumt)).astype(bf16)
        k_end = ktt * jnp.exp(clt - cumt)
        v = v_ref[rs, :]

        q_stack = jnp.concatenate(
            [jnp.where(head_of_lane == h, qt[cc * c:(cc + 1) * c, :], 0.0)
             for cc in range(nc) for h in range(nh)], axis=0).astype(bf16)
        scores = jnp.dot(q_stack, k_t, preferred_element_type=f32)
        scores = jnp.where(causal, scores, 0.0).astype(bf16)
        o_full = jnp.dot(scores, v, preferred_element_type=f32)

        k_stack = jnp.concatenate(
            [jnp.where(chunk_of_lane == cc, k_end, 0.0) for cc in range(nc)], axis=0).astype(bf16)
        upd_full = jnp.dot(k_stack, v, preferred_element_type=f32)

        o_inter = []
        for cc in range(nc):
            o_inter.append(jnp.dot(q_stack[cc * nh * c:(cc + 1) * nh * c, :], state.astype(bf16),
                                   preferred_element_type=f32))
            upd = jnp.concatenate(
                [upd_full[cc * D_GLA_K + h * GLA_DK:cc * D_GLA_K + (h + 1) * GLA_DK,
                          h * GLA_DV:(h + 1) * GLA_DV] for h in range(nh)], axis=0)
            state = jnp.exp(clt[:, cc * c:cc * c + 1]) * state + upd

        rows = []
        for cc in range(nc):
            outs = []
            for h in range(nh):
                r0 = (cc * nh + h) * c
                o = o_full[r0:r0 + c, h * GLA_DV:(h + 1) * GLA_DV] + o_inter[cc][h * c:(h + 1) * c, :]
                o = o * lax.rsqrt(jnp.mean(o * o, axis=-1, keepdims=True) + EPS)
                outs.append(o)
            rows.append(jnp.concatenate(outs, axis=1))
        o_cat = jnp.concatenate(rows, axis=0)
        o_ref[rs, :] = (o_cat * gn_ref[...] * jax.nn.silu(r_ref[rs, :].astype(f32))).astype(bf16)
    s_ref[...] = state


def _gla(l, q3, kt3, v3, r3, g3, wa, ba, gn):
    bsz, seq, _ = q3.shape
    tg = TG_GLA
    tok = lambda b, i: (b, i, 0)
    return pl.pallas_call(
        _gla_body,
        grid=(bsz, seq // tg),
        in_specs=[pl.BlockSpec((None, tg, D_GLA_K), tok),
                  pl.BlockSpec((None, D_GLA_K, tg), lambda b, i: (b, 0, i)),
                  pl.BlockSpec((None, tg, D_GLA), tok),
                  pl.BlockSpec((None, tg, D_GLA), tok),
                  pl.BlockSpec((None, tg, LANE), tok),
                  _layer(wa, l), _layer(ba, l), _layer(gn, l)],
        out_specs=pl.BlockSpec((None, tg, D_GLA), tok),
        out_shape=SDS((bsz, seq, D_GLA), bf16),
        scratch_shapes=[pltpu.VMEM((D_GLA_K, GLA_DV), f32)],
        compiler_params=_cparams("parallel", "arbitrary"),
        name="gla",
    )(q3, kt3, v3, r3, g3, wa, ba, gn)


def _mix_out(x_ref, ys_ref, yg_ref, wgl_ref, bg_ref, gs_ref, wo_ref):
    y = ys_ref[...]
    z = jnp.dot(y, wgl_ref[...], preferred_element_type=f32) + bg_ref[...]
    yf = y.astype(f32) * jax.nn.sigmoid(z)
    ys = _rms(yf, gs_ref[...]).astype(bf16)
    acc = jnp.dot(ys, wo_ref[0:D_S5, :], preferred_element_type=f32)
    acc = acc + jnp.dot(yg_ref[...], wo_ref[D_S5:, :], preferred_element_type=f32)
    return x_ref[...] + acc


def _mix_specs(l, tm, w_glu, b_glu, g_s5, w_out):
    row = lambda i: (i, 0)
    once = dict(pipeline_mode=pl.Buffered(1))
    return [pl.BlockSpec((tm, D_MODEL), row),
            pl.BlockSpec((tm, D_S5), row),
            pl.BlockSpec((tm, D_GLA), row),
            _layer(w_glu, l, **once), _layer(b_glu, l), _layer(g_s5, l), _layer(w_out, l, **once)]


def _mix_ffn_body(x_ref, ys_ref, yg_ref, wgl_ref, bg_ref, gs_ref, wo_ref,
                  g_ref, wg_ref, wu_ref, wd_ref, o_ref):
    x = _mix_out(x_ref, ys_ref, yg_ref, wgl_ref, bg_ref, gs_ref, wo_ref)
    hn = _rms(x, g_ref[...]).astype(bf16)
    gate = jnp.dot(hn, wg_ref[...], preferred_element_type=f32)
    up = jnp.dot(hn, wu_ref[...], preferred_element_type=f32)
    act = (jax.nn.silu(gate) * up).astype(bf16)
    o_ref[...] = x + jnp.dot(act, wd_ref[...], preferred_element_type=f32)


def _mix_ffn(l, i_ffn, x2, ys, yg, w_glu, b_glu, g_s5, w_out, g, wg, wu, wd):
    n = x2.shape[0]
    tm = TM_FFN
    row = lambda i: (i, 0)
    once = dict(pipeline_mode=pl.Buffered(1))
    return pl.pallas_call(
        _mix_ffn_body,
        grid=(n // tm,),
        in_specs=_mix_specs(l, tm, w_glu, b_glu, g_s5, w_out) + [
            _layer(g, l), _layer(wg, i_ffn, **once), _layer(wu, i_ffn, **once),
            _layer(wd, i_ffn, **once)],
        out_specs=pl.BlockSpec((tm, D_MODEL), row),
        out_shape=SDS((n, D_MODEL), f32),
        compiler_params=_cparams("parallel"),
        name="mix_ffn",
    )(x2, ys, yg, w_glu, b_glu, g_s5, w_out, g, wg, wu, wd)


def _pack_bf16_pairs(a):
    bits = pltpu.bitcast(a.astype(bf16).astype(f32), u32)
    half = a.shape[1] // 2
    return bits[:, :half] | (bits[:, half:] >> 16)


def _unpack_bf16_pairs(p):
    hi = pltpu.bitcast(p & jnp.uint32(0xFFFF0000), f32).astype(bf16)
    lo = pltpu.bitcast(p << 16, f32).astype(bf16)
    return hi, lo


def _mix_router_body(x_ref, ys_ref, yg_ref, wgl_ref, bg_ref, gs_ref, wo_ref,
                     g_ref, wh_ref, wl_ref, xo_ref, hp_ref, meta_ref, wcol_ref, cnt_ref, carry):
    tm = x_ref.shape[0]

    @pl.when(pl.program_id(0) == 0)
    def _():
        carry[...] = jnp.zeros_like(carry)

    x = _mix_out(x_ref, ys_ref, yg_ref, wgl_ref, bg_ref, gs_ref, wo_ref)
    xo_ref[...] = x
    hn = _rms(x, g_ref[...])
    packed = _pack_bf16_pairs(hn)
    hp_ref[0] = packed[:, :SC_ROW]
    hp_ref[1] = packed[:, SC_ROW:]

    h_hi, h_lo = _split2(hn)
    nt = (((1,), (1,)), ((), ()))
    logits = (lax.dot_general(wh_ref[...], h_hi, nt, preferred_element_type=f32)
              + lax.dot_general(wh_ref[...], h_lo, nt, preferred_element_type=f32)
              + lax.dot_general(wl_ref[...], h_hi, nt, preferred_element_type=f32))
    er = logits.shape[0]
    row = lax.broadcasted_iota(i32, (er, tm), 0)
    neg = jnp.float32(-jnp.inf)
    logits = jnp.where(row < N_EXPERTS, logits, neg)
    m1 = jnp.max(logits, axis=0, keepdims=True)
    i1 = jnp.min(jnp.where(logits == m1, row, er), axis=0, keepdims=True)
    l2 = jnp.where(row == i1, neg, logits)
    m2 = jnp.max(l2, axis=0, keepdims=True)
    i2 = jnp.min(jnp.where(l2 == m2, row, er), axis=0, keepdims=True)
    e21 = jnp.exp(m2 - m1)
    w1 = 1.0 / (1.0 + e21)
    w2 = e21 / (1.0 + e21)

    sel1 = row == i1
    sel2 = row == i2
    sel = (sel1 | sel2).astype(f32)
    ri = lax.broadcasted_iota(i32, (tm, tm), 0)
    ci = lax.broadcasted_iota(i32, (tm, tm), 1)
    triu = (ri <= ci).astype(bf16)
    incl = jnp.dot(sel.astype(bf16), triu, preferred_element_type=f32)
    rank = incl - sel + carry[:, 0:1]
    r1 = jnp.sum(jnp.where(sel1, rank, 0.0), axis=0, keepdims=True)
    r2 = jnp.sum(jnp.where(sel2, rank, 0.0), axis=0, keepdims=True)
    new_cnt = carry[:, 0:1] + incl[:, tm - 1:tm]
    carry[...] = jnp.broadcast_to(new_cnt, carry.shape)
    cnt_ref[...] = jnp.broadcast_to(new_cnt, cnt_ref.shape)

    srow = lax.broadcasted_iota(i32, (SUBLANE, tm), 0)
    meta = jnp.where(srow == 0, i1.astype(f32), 0.0)
    meta = jnp.where(srow == 1, i2.astype(f32), meta)
    meta = jnp.where(srow == 2, r1, meta)
    meta = jnp.where(srow == 3, r2, meta)
    meta_ref[...] = meta
    prow = lax.broadcasted_iota(i32, (LANE, tm), 0)
    wpad = jnp.where(prow == 0, w1, jnp.where(prow == 1, w2, 0.0))
    wcol_ref[...] = wpad.T


def _mix_router(l, x2, ys, yg, w_glu, b_glu, g_s5, w_out, g, w_hi, w_lo):
    n = x2.shape[0]
    tm = TM_ROUTE
    row = lambda i: (i, 0)
    const = lambda i: (0, 0)
    return pl.pallas_call(
        _mix_router_body,
        grid=(n // tm,),
        in_specs=_mix_specs(l, tm, w_glu, b_glu, g_s5, w_out) + [
            _layer(g, l),
            pl.BlockSpec(w_hi.shape, const),
            pl.BlockSpec(w_lo.shape, const)],
        out_specs=[pl.BlockSpec((tm, D_MODEL), row),
                   pl.BlockSpec((2, tm, SC_ROW), lambda i: (0, i, 0)),
                   pl.BlockSpec((SUBLANE, tm), lambda i: (0, i)),
                   pl.BlockSpec((tm, LANE), row),
                   pl.BlockSpec(w_hi.shape[:1] + (LANE,), const)],
        out_shape=[SDS((n, D_MODEL), f32), SDS((2, n, SC_ROW), u32), SDS((SUBLANE, n), f32),
                   SDS((n, LANE), f32), SDS(w_hi.shape[:1] + (LANE,), f32)],
        scratch_shapes=[pltpu.VMEM(w_hi.shape[:1] + (LANE,), f32)],
        compiler_params=_cparams("arbitrary"),
        name="mix_router",
    )(x2, ys, yg, w_glu, b_glu, g_s5, w_out, g, w_hi, w_lo)


def _sc_gather(table, idx):
    ni = idx.shape[0]
    mesh = plsc.VectorSubcoreMesh(core_axis_name="core", subcore_axis_name="subcore")
    idx2 = idx.reshape(1, ni)

    @pl.kernel(out_type=SDS((ni, SC_ROW), table.dtype), mesh=mesh)
    def kern(t_hbm, i_hbm, o_hbm):
        def body(i_vmem, o_vmem):
            pltpu.sync_copy(t_hbm.at[i_vmem.at[0]], o_vmem)

        pltpu.emit_pipeline(
            body, grid=(ni // SC_WINDOW,),
            in_specs=[pl.BlockSpec((1, SC_WINDOW), index_map=lambda i: (0, i))],
            out_specs=[pl.BlockSpec((SC_WINDOW, SC_ROW), index_map=lambda i: (i, 0))],
            core_axis_name=("core", "subcore"),
            dimension_semantics=(pltpu.PARALLEL,),
        )(i_hbm, o_hbm)

    return kern(table, idx2)


def _sc_scatter2(x, idx0, idx1, nrows):
    ni = x.shape[0]
    mesh = plsc.VectorSubcoreMesh(core_axis_name="core", subcore_axis_name="subcore")

    @pl.kernel(out_type=SDS((nrows, SC_ROW), x.dtype), mesh=mesh)
    def kern(x_hbm, i0_hbm, i1_hbm, o_hbm):
        def body(x_vmem, i0_vmem, i1_vmem):
            pltpu.sync_copy(x_vmem, o_hbm.at[i0_vmem.at[0]])
            pltpu.sync_copy(x_vmem, o_hbm.at[i1_vmem.at[0]])

        pltpu.emit_pipeline(
            body, grid=(ni // SC_WINDOW,),
            in_specs=[pl.BlockSpec((SC_WINDOW, SC_ROW), index_map=lambda i: (i, 0)),
                      pl.BlockSpec((1, SC_WINDOW), index_map=lambda i: (0, i)),
                      pl.BlockSpec((1, SC_WINDOW), index_map=lambda i: (0, i))],
            out_specs=[],
            core_axis_name=("core", "subcore"),
            dimension_semantics=(pltpu.PARALLEL,),
        )(x_hbm, i0_hbm, i1_hbm)

    return kern(x, idx0.reshape(1, ni), idx1.reshape(1, ni))


def _moe_ffn_body(be_ref, bn_ref, xs_ref, wg_ref, wu_ref, wd_ref, ys_ref, xb, act0, act1, acc):
    i = pl.program_id(0)
    f = pl.program_id(1)
    nf = pl.num_programs(1) - 1
    nvalid = bn_ref[i]
    half = xs_ref.shape[1] // MOE_SUB

    for h in range(MOE_SUB):
        rs = pl.ds(h * half, half)
        live_half = nvalid > h * half

        def gate_up(rs=rs):
            x = xb[rs, :]
            gate = jnp.dot(x, wg_ref[...].astype(bf16), preferred_element_type=f32)
            up = jnp.dot(x, wu_ref[...].astype(bf16), preferred_element_type=f32)
            return (jax.nn.silu(gate) * up).astype(bf16)

        def down(act_ref, rs=rs):
            return jnp.dot(act_ref[rs, :], wd_ref[...].astype(bf16), preferred_element_type=f32)

        @pl.when(live_half & (f == 0))
        def _(rs=rs, h=h, gate_up=gate_up):
            live = lax.broadcasted_iota(i32, (half, SC_ROW), 0) + h * half < nvalid
            hi0, lo0 = _unpack_bf16_pairs(jnp.where(live, xs_ref[0, rs, :], jnp.uint32(0)))
            hi1, lo1 = _unpack_bf16_pairs(jnp.where(live, xs_ref[1, rs, :], jnp.uint32(0)))
            xb[rs, :] = jnp.concatenate([hi0, hi1, lo0, lo1], axis=1)
            acc[rs, :] = jnp.zeros((half, D_MODEL), f32)
            act0[rs, :] = gate_up()

        @pl.when(live_half & (f > 0) & (f < nf) & (f % 2 == 1))
        def _(rs=rs, gate_up=gate_up, down=down):
            act1[rs, :] = gate_up()
            acc[rs, :] += down(act0)

        @pl.when(live_half & (f > 0) & (f < nf) & (f % 2 == 0))
        def _(rs=rs, gate_up=gate_up, down=down):
            act0[rs, :] = gate_up()
            acc[rs, :] += down(act1)

        @pl.when(live_half & (f == nf))
        def _(rs=rs, down=down):
            last = act0 if (D_FF_EXPERT // TF_MOE - 1) % 2 == 0 else act1
            packed = _pack_bf16_pairs(acc[rs, :] + down(last))
            ys_ref[0, rs, :] = packed[:, :SC_ROW]
            ys_ref[1, rs, :] = packed[:, SC_ROW:]

        @pl.when(jnp.logical_not(live_half) & (f == nf))
        def _(rs=rs):
            ys_ref[:, rs, :] = jnp.zeros((2, half, SC_ROW), u32)


def _moe_ffn(i_moe, blk_e, blk_n, xs, wg, wu, wd):
    npad = xs.shape[1]
    tm, tf = TM_MOE, TF_MOE
    nblk = npad // tm
    nf = D_FF_EXPERT // tf

    def nxt(i):
        return jnp.minimum(i + 1, nblk - 1)

    def x_idx(i, f, be, bn):
        return (0, jnp.where(f == nf, nxt(i), i), 0)

    def gu_idx(i, f, be, bn):
        ahead = (f == nf) & (bn[i] > 0) & (bn[nxt(i)] > 0)
        e = jnp.where(ahead, be[nxt(i)], be[i])
        t = jnp.where(ahead, 0, jnp.where(bn[i] > 0, jnp.minimum(f, nf - 1), nf - 1))
        return (i_moe, e, 0, t)

    def d_idx(i, f, be, bn):
        prev = jnp.maximum(i - 1, 0)
        keep = (f == 0) & (i > 0)
        e = jnp.where(keep, be[prev], be[i])
        t = jnp.where(keep | (bn[i] == 0), nf - 1, jnp.maximum(f - 1, 0))
        return (i_moe, e, t, 0)

    grid_spec = pltpu.PrefetchScalarGridSpec(
        num_scalar_prefetch=2,
        grid=(nblk, nf + 1),
        in_specs=[pl.BlockSpec((2, tm, SC_ROW), x_idx),
                  pl.BlockSpec((None, None, D_MODEL, tf), gu_idx),
                  pl.BlockSpec((None, None, D_MODEL, tf), gu_idx),
                  pl.BlockSpec((None, None, tf, D_MODEL), d_idx)],
        out_specs=pl.BlockSpec((2, tm, SC_ROW), lambda i, f, be, bn: (0, i, 0)),
        scratch_shapes=[pltpu.VMEM((tm, D_MODEL), bf16), pltpu.VMEM((tm, tf), bf16),
                        pltpu.VMEM((tm, tf), bf16), pltpu.VMEM((tm, D_MODEL), f32)],
    )
    return pl.pallas_call(
        _moe_ffn_body,
        grid_spec=grid_spec,
        out_shape=SDS((2, npad, SC_ROW), u32),
        compiler_params=_cparams("parallel", "arbitrary"),
        name="moe_ffn",
    )(blk_e, blk_n, xs, wg, wu, wd)


def _combine_body(x_ref, yg_ref, wcol_ref, g_ref, *rest, final_norm):
    o_ref = rest[-1]
    w1 = wcol_ref[:, 0:1]
    w2 = wcol_ref[:, 1:2]

    def rows(k):
        hi0, lo0 = _unpack_bf16_pairs(yg_ref[0, k])
        hi1, lo1 = _unpack_bf16_pairs(yg_ref[1, k])
        return jnp.concatenate([hi0, hi1, lo0, lo1], axis=1).astype(f32)

    xo = x_ref[...] + (w1 * rows(0) + w2 * rows(1))
    if final_norm:
        xo = _rms(xo, g_ref[...])
    o_ref[...] = xo


def _combine(x2, yg, wcol, g_final, final_norm, part, out_prev):
    n = x2.shape[0]
    tm = TM_ROUTE
    steps = yg.shape[2] // tm
    row = lambda i: (part * steps + i, 0)
    in_specs = [pl.BlockSpec((tm, D_MODEL), row),
                pl.BlockSpec((2, 2, tm, SC_ROW), lambda i: (0, 0, i, 0)),
                pl.BlockSpec((tm, LANE), row),
                pl.BlockSpec((1, D_MODEL), lambda i: (0, 0))]
    args = [x2, yg, wcol, g_final]
    aliases = {}
    if out_prev is not None:
        in_specs.append(pl.BlockSpec(memory_space=pl.ANY))
        args.append(out_prev)
        aliases = {len(args) - 1: 0}
    return pl.pallas_call(
        functools.partial(_combine_body, final_norm=final_norm),
        grid=(steps,),
        in_specs=in_specs,
        out_specs=pl.BlockSpec((tm, D_MODEL), row),
        out_shape=SDS((n, D_MODEL), f32),
        input_output_aliases=aliases,
        compiler_params=_cparams("parallel"),
        name="moe_combine",
    )(*args)


def _moe_layer(l, i_moe, mix_args, g_ffn, w_router, wg, wu, wd, g_final, final_norm):
    n = mix_args[0].shape[0]
    tm = TM_MOE
    npad = 2 * n + N_EXPERTS * tm
    wr = jnp.zeros((2 * SUBLANE, D_MODEL), f32).at[:N_EXPERTS].set(w_router[i_moe].astype(f32).T)
    wr_hi = wr.astype(bf16)
    wr_lo = (wr - wr_hi.astype(f32)).astype(bf16)
    x2, hp, meta, wcol, cnt = _mix_router(l, *mix_args, g_ffn, wr_hi, wr_lo)

    counts = cnt[:N_EXPERTS, 0].astype(i32)
    padded = ((counts + tm - 1) // tm) * tm
    ends = jnp.cumsum(padded)
    offs = ends - padded
    ids = jnp.arange(N_EXPERTS, dtype=i32)
    e12 = meta[0:2].astype(i32)
    r12 = meta[2:4].astype(i32)
    pos_t = r12 + jnp.sum(jnp.where(e12[..., None] == ids, offs, 0), axis=-1)
    blk_start = jnp.arange(npad // tm, dtype=i32) * tm
    blk_e = jnp.minimum(jnp.sum((blk_start[:, None] >= ends[None, :]).astype(i32), axis=1), N_EXPERTS - 1)
    own = blk_e[:, None] == ids
    blk_end = jnp.sum(jnp.where(own, offs + counts, 0), axis=1)
    blk_n = jnp.where(blk_start < ends[-1], jnp.clip(blk_end - blk_start, 0, tm), 0)
    last_e = jnp.max(jnp.where(blk_n > 0, blk_e, 0))
    blk_e = jnp.where(blk_n > 0, blk_e, last_e)

    xs = _sc_scatter2(hp.reshape(2 * n, SC_ROW),
                      jnp.concatenate([pos_t[0], pos_t[0] + npad]),
                      jnp.concatenate([pos_t[1], pos_t[1] + npad]),
                      2 * npad).reshape(2, npad, SC_ROW)
    ys = _moe_ffn(i_moe, blk_e, blk_n, xs, wg, wu, wd)
    ys_rows = ys.reshape(2 * npad, SC_ROW)
    npart = n // MOE_PARTS
    out = None
    for p in range(MOE_PARTS):
        pp = pos_t[:, p * npart:(p + 1) * npart].reshape(-1)
        yg = _sc_gather(ys_rows, jnp.concatenate([pp, pp + npad])).reshape(2, 2, npart, SC_ROW)
        out = _combine(x2, yg, wcol, g_final, final_norm, p, out)
    return out


def _rows(v):
    return v.astype(f32)[:, None, :]


def kernel(x, norm_mix, w_in, s5_lambda_re, s5_lambda_im, s5_log_dt, s5_b_re, s5_b_im, s5_c_re, s5_c_im, s5_d, s5_w_glu, s5_b_glu, s5_out_norm, gla_w_a2, gla_b_a2, gla_out_norm, w_out, norm_ffn, ffn_w_gate, ffn_w_up, ffn_w_down, moe_w_router, moe_w_gate, moe_w_up, moe_w_down, norm_final):
    bsz, seq, _ = x.shape
    n = bsz * seq
    depth = w_in.shape[0]
    x2 = x.reshape(n, D_MODEL)

    w_gp = jnp.zeros((depth, D_MODEL, LANE), f32).at[:, :, :GLA_GATE_RANK].set(w_in[:, :, 2048:2064])
    w_cat = jnp.concatenate([w_in[:, :, 0:512], w_in[:, :, 512:768] * (GLA_DK ** -0.5),
                             w_in[:, :, 1024:1536], w_in[:, :, 1536:2048], w_gp], axis=2).astype(bf16)
    w_kt = jnp.swapaxes(w_in[:, :, 768:1024], 1, 2).astype(bf16)
    s5_mats = jax.vmap(_s5_prep)(s5_lambda_re, s5_lambda_im, s5_log_dt, s5_b_re, s5_b_im, s5_c_re, s5_c_im)
    wa = jnp.zeros((depth, LANE, D_GLA_K), f32).at[:, :GLA_GATE_RANK].set(gla_w_a2).astype(bf16)
    g_mix, g_ffn, d_skip = _rows(norm_mix), _rows(norm_ffn), _rows(s5_d)
    ba, gn = _rows(gla_b_a2), _rows(gla_out_norm)
    mix_w = (s5_w_glu.astype(bf16), _rows(s5_b_glu), _rows(s5_out_norm), w_out.astype(bf16))
    ffn_w = (ffn_w_gate.astype(bf16), ffn_w_up.astype(bf16), ffn_w_down.astype(bf16))
    g_final = norm_final.astype(f32).reshape(1, D_MODEL)

    for l in range(depth):
        u, q, kt, v, r, gl = _inproj(l, x2, g_mix, w_cat, w_kt, bsz, seq)
        ys = _s5(l, u.reshape(bsz, seq, D_S5), *s5_mats, d_skip)
        yg = _gla(l, q.reshape(bsz, seq, D_GLA_K), kt, v.reshape(bsz, seq, D_GLA),
                  r.reshape(bsz, seq, D_GLA), gl.reshape(bsz, seq, LANE), wa, ba, gn)
        mix_args = (x2, ys.reshape(n, D_S5), yg.reshape(n, D_GLA)) + mix_w

        last = l == depth - 1
        if l % 2 == 0:
            x2 = _mix_ffn(l, l // 2, *mix_args, g_ffn, *ffn_w)
            if last:
                x2 = _final_norm(x2, g_final)
        else:
            x2 = _moe_layer(l, l // 2, mix_args, g_ffn, moe_w_router, moe_w_gate, moe_w_up,
                            moe_w_down, g_final, last)
    return x2.reshape(bsz, seq, D_MODEL)


def _final_norm_body(x_ref, g_ref, o_ref):
    o_ref[...] = _rms(x_ref[...], g_ref[...])


def _final_norm(x2, g):
    n = x2.shape[0]
    tm = TM_ROUTE
    return pl.pallas_call(
        _final_norm_body,
        grid=(n // tm,),
        in_specs=[pl.BlockSpec((tm, D_MODEL), lambda i: (i, 0)),
                  pl.BlockSpec((1, D_MODEL), lambda i: (0, 0))],
        out_specs=pl.BlockSpec((tm, D_MODEL), lambda i: (i, 0)),
        out_shape=SDS((n, D_MODEL), f32),
        compiler_params=_cparams("parallel"),
        name="final_norm",
    )(x2, g)
```

```python
import functools
import math

import jax
import jax.numpy as jnp
from jax import lax
from jax.experimental import pallas as pl
from jax.experimental.pallas import tpu as pltpu
from jax.experimental.pallas import tpu_sc as plsc

f32 = jnp.float32
bf16 = jnp.bfloat16
u32 = jnp.uint32
i32 = jnp.int32
SDS = jax.ShapeDtypeStruct

D_MODEL = 1024
D_S5 = 512
S5_GROUP = 16
S5_GROUPS = 32
S5_STATE = 64
N_STATE = S5_GROUPS * S5_STATE
D_GLA = 512
GLA_HEADS = 4
GLA_DV = 128
GLA_DK = 64
D_GLA_K = 256
GLA_GATE_RANK = 16
GLA_TAU = 16.0
GLA_CHUNK = 64
D_FF = 2816
N_EXPERTS = 8
D_FF_EXPERT = 3584
EPS = 1e-6

LANE = 128
SUBLANE = 8
VMEM_LIMIT = 56 * 1024 * 1024

TM_PROJ = 1024
TT_S5 = 256
S5_SLABS = 4
S5_PITCH_PAD = 8
TG_GLA = 1024
GLA_SUB = 256
TM_FFN = 512
TM_MOE = 2048
MOE_SUB = 2
TF_MOE = 512
TM_ROUTE = 512
SC_WINDOW = 128
SC_ROW = 256


def _cparams(*sem):
    return pltpu.CompilerParams(dimension_semantics=sem, vmem_limit_bytes=VMEM_LIMIT)


def _layer(arr, l, **kw):
    return pl.BlockSpec((None,) + arr.shape[1:], lambda *_: (l,) + (0,) * (arr.ndim - 1), **kw)


def _rms(x, g):
    ms = jnp.mean(x * x, axis=-1, keepdims=True)
    return x * lax.rsqrt(ms + EPS) * g


def _inproj_body(x_ref, g_ref, w_ref, wkt_ref, u_ref, q_ref, kt_ref, v_ref, r_ref, gl_ref):
    hn = _rms(x_ref[...], g_ref[...]).astype(bf16)

    def proj(lo, hi):
        return jnp.dot(hn, w_ref[:, lo:hi], preferred_element_type=f32).astype(bf16)

    u_ref[...] = proj(0, 512)
    q_ref[...] = proj(512, 768)
    v_ref[...] = proj(768, 1280)
    r_ref[...] = proj(1280, 1792)
    gl_ref[...] = proj(1792, 1920)
    kt_ref[...] = lax.dot_general(wkt_ref[...], hn, (((1,), (1,)), ((), ())),
                                  preferred_element_type=f32).astype(bf16)


def _inproj(l, x2, g, w_cat, w_kt, bsz, seq):
    n = x2.shape[0]
    tm = TM_PROJ
    per_b = seq // tm
    row = lambda i: (i, 0)
    return pl.pallas_call(
        _inproj_body,
        grid=(n // tm,),
        in_specs=[pl.BlockSpec((tm, D_MODEL), row),
                  _layer(g, l), _layer(w_cat, l), _layer(w_kt, l)],
        out_specs=[pl.BlockSpec((tm, D_S5), row),
                   pl.BlockSpec((tm, D_GLA_K), row),
                   pl.BlockSpec((None, D_GLA_K, tm), lambda i: (i // per_b, 0, i % per_b)),
                   pl.BlockSpec((tm, D_GLA), row),
                   pl.BlockSpec((tm, D_GLA), row),
                   pl.BlockSpec((tm, LANE), row)],
        out_shape=[SDS((n, D_S5), bf16), SDS((n, D_GLA_K), bf16),
                   SDS((bsz, D_GLA_K, seq), bf16), SDS((n, D_GLA), bf16),
                   SDS((n, D_GLA), bf16), SDS((n, LANE), bf16)],
        compiler_params=_cparams("parallel"),
        name="inproj",
    )(x2, g, w_cat, w_kt)


N_SLAB = N_STATE // LANE


def _s5_prep(lam_re, lam_im, log_dt, b_re, b_im, c_re, c_im):
    lr = jnp.minimum(lam_re.astype(f32), -1e-4)
    li = lam_im.astype(f32)
    dt = jnp.exp(log_dt.astype(f32))[:, None]
    mag = jnp.exp(lr * dt)
    ab_re = mag * jnp.cos(li * dt)
    ab_im = mag * jnp.sin(li * dt)
    nr = ab_re - 1.0
    ni = ab_im
    den = lr * lr + li * li
    f_re = (nr * lr + ni * li) / den
    f_im = (ni * lr - nr * li) / den
    br = b_re.astype(f32)
    bi = b_im.astype(f32)
    bb_re = f_re[..., None] * br - f_im[..., None] * bi
    bb_im = f_re[..., None] * bi + f_im[..., None] * br
    abb_re = ab_re[..., None] * bb_re - ab_im[..., None] * bb_im
    abb_im = ab_re[..., None] * bb_im + ab_im[..., None] * bb_re
    cr = c_re.astype(f32)
    ci = c_im.astype(f32)
    ca_re = cr * ab_re[:, None, :] - ci * ab_im[:, None, :]
    ca_im = cr * ab_im[:, None, :] + ci * ab_re[:, None, :]
    cb = jnp.einsum('gop,gpc->goc', cr, bb_re) - jnp.einsum('gop,gpc->goc', ci, bb_im)

    def b_tiles(bb):
        bt = jnp.transpose(bb, (0, 2, 1)).reshape(8, 4, S5_GROUP, S5_STATE)
        gl = jnp.arange(8)[None, :, None]
        gs = jnp.arange(4)[None, None, :]
        nn = jnp.arange(8)[:, None, None]
        sel = (gl == 4 * (nn % 2) + gs).astype(f32)
        t = jnp.einsum('ngs,nscp->ngcsp', sel, bt)
        return t.reshape(8, LANE, 4 * S5_STATE)

    bmat = jnp.concatenate(
        [jnp.concatenate([b_tiles(abb_re), b_tiles(bb_re)], axis=1),
         jnp.concatenate([b_tiles(abb_im), b_tiles(bb_im)], axis=1)], axis=0).astype(bf16)

    def c_tiles(c, sign):
        ct = jnp.transpose(c, (0, 2, 1)).reshape(2, 16, S5_STATE, S5_GROUP)
        eye = jnp.eye(16, dtype=f32)
        t = jnp.einsum('gh,jgpo->jgpho', eye, ct) * sign
        return t.reshape(2, 16 * S5_STATE, 16 * S5_GROUP)

    cmat = jnp.stack([c_tiles(cr, 1.0), c_tiles(ci, -1.0)], axis=1).astype(bf16)
    camat = jnp.stack([c_tiles(ca_re, 1.0), c_tiles(ca_im, -1.0)], axis=1).astype(bf16)
    cbt = jnp.transpose(cb, (0, 2, 1)).reshape(2, 16, S5_GROUP, S5_GROUP)
    cbmat = jnp.einsum('gh,jgco->jgcho', jnp.eye(16, dtype=f32), cbt).reshape(2, 256, 256).astype(bf16)
    a2_re = (ab_re * ab_re - ab_im * ab_im).reshape(N_SLAB, 1, LANE)
    a2_im = (2.0 * ab_re * ab_im).reshape(N_SLAB, 1, LANE)
    return bmat, cmat, camat, cbmat, a2_re, a2_im


def _s5_body(u_ref, bm_ref, cm_ref, cam_ref, cbm_ref, are_ref, aim_ref, d_ref, y_ref,
             hbuf, hstate, zcarry):
    tt = u_ref.shape[1]
    tp = tt // 2
    rows = SUBLANE * tp
    pitch = tp + S5_PITCH_PAD

    @pl.when(pl.program_id(0) == 0)
    def _():
        hstate[...] = jnp.zeros_like(hstate)
        zcarry[...] = jnp.zeros_like(zcarry)

    w = pltpu.bitcast(u_ref[...].reshape(SUBLANE * tt, D_S5), u32)
    ue_f = pltpu.bitcast(w << 16, f32)
    uo_f = pltpu.bitcast(w & jnp.uint32(0xFFFF0000), f32)
    ue = ue_f.astype(bf16)
    uo = uo_f.astype(bf16)

    def put(slab, val):
        for b in range(SUBLANE):
            hbuf[slab, b * pitch:b * pitch + tp, :] = val[b * tp:(b + 1) * tp, :]

    def get(slab):
        return jnp.concatenate([hbuf[slab, b * pitch:b * pitch + tp, :] for b in range(SUBLANE)], axis=0)

    for n in range(2 * 8):
        ks = LANE * ((n % 8) // 2)
        lhs = jnp.concatenate([ue[:, ks:ks + LANE], uo[:, ks:ks + LANE]], axis=1)
        res = jnp.dot(lhs, bm_ref[n], preferred_element_type=f32)
        put(2 * n, res[:, :LANE])
        put(2 * n + 1, res[:, LANE:])

    for c0 in range(0, N_SLAB, S5_SLABS):
        ar = [jnp.broadcast_to(are_ref[c0 + s], (SUBLANE, LANE)) for s in range(S5_SLABS)]
        ai = [jnp.broadcast_to(aim_ref[c0 + s], (SUBLANE, LANE)) for s in range(S5_SLABS)]

        def step(t, carry):
            out = []
            for s in range(S5_SLABS):
                hr, hi = carry[2 * s], carry[2 * s + 1]
                sl = pl.ds(t, SUBLANE, stride=pitch)
                bur = hbuf[c0 + s, sl, :]
                bui = hbuf[N_SLAB + c0 + s, sl, :]
                nr = ar[s] * hr - ai[s] * hi + bur
                ni = ar[s] * hi + ai[s] * hr + bui
                hbuf[c0 + s, sl, :] = nr
                hbuf[N_SLAB + c0 + s, sl, :] = ni
                out += [nr, ni]
            return tuple(out)

        init = []
        for s in range(S5_SLABS):
            init += [hstate[c0 + s], hstate[N_SLAB + c0 + s]]
        fin = lax.fori_loop(0, tp, step, tuple(init), unroll=4)
        for s in range(S5_SLABS):
            hstate[c0 + s] = fin[2 * s]
            hstate[N_SLAB + c0 + s] = fin[2 * s + 1]

    first = lax.broadcasted_iota(i32, (rows, 256), 0) % tp == 0
    for j in range(2):
        h_re = jnp.concatenate([get(8 * j + s) for s in range(8)], axis=1).astype(bf16)
        h_im = jnp.concatenate([get(N_SLAB + 8 * j + s) for s in range(8)], axis=1).astype(bf16)
        cs = slice(256 * j, 256 * j + 256)
        dj = d_ref[:, cs]
        yo = jnp.dot(h_re, cm_ref[j, 0], preferred_element_type=f32)
        yo = yo + jnp.dot(h_im, cm_ref[j, 1], preferred_element_type=f32)
        yo = yo + dj * uo_f[:, cs]
        z = jnp.dot(h_re, cam_ref[j, 0], preferred_element_type=f32)
        z = z + jnp.dot(h_im, cam_ref[j, 1], preferred_element_type=f32)
        prev = jnp.concatenate([jnp.broadcast_to(zcarry[b:b + 1, cs], (tp, 256))
                                for b in range(SUBLANE)], axis=0)
        zs = jnp.where(first, prev, pltpu.roll(z, 1, 0))
        zcarry[:, cs] = jnp.concatenate([z[(b + 1) * tp - 1:(b + 1) * tp, :] for b in range(SUBLANE)], axis=0)
        ye = zs + jnp.dot(ue[:, cs], cbm_ref[j], preferred_element_type=f32) + dj * ue_f[:, cs]
        ge = pltpu.bitcast(jax.nn.gelu(ye).astype(bf16).astype(f32), u32)
        go = pltpu.bitcast(jax.nn.gelu(yo).astype(bf16).astype(f32), u32)
        packed = (ge >> 16) | go
        y_ref[:, :, cs] = pltpu.bitcast(packed, bf16).reshape(SUBLANE, tt, 256)


def _s5(l, u3, bmat, cmat, camat, cbmat, a2_re, a2_im, d_skip):
    bsz, seq, _ = u3.shape
    assert bsz == SUBLANE
    tt = TT_S5
    blk = lambda i: (0, i, 0)
    return pl.pallas_call(
        _s5_body,
        grid=(seq // tt,),
        in_specs=[pl.BlockSpec((bsz, tt, D_S5), blk)]
        + [_layer(a, l) for a in (bmat, cmat, camat, cbmat, a2_re, a2_im, d_skip)],
        out_specs=pl.BlockSpec((bsz, tt, D_S5), blk),
        out_shape=SDS((bsz, seq, D_S5), bf16),
        scratch_shapes=[pltpu.VMEM((2 * N_SLAB, bsz * (tt // 2 + S5_PITCH_PAD), LANE), f32),
                        pltpu.VMEM((2 * N_SLAB, SUBLANE, LANE), f32),
                        pltpu.VMEM((SUBLANE, D_S5), f32)],
        compiler_params=_cparams("arbitrary"),
        name="s5_scan",
    )(u3, bmat, cmat, camat, cbmat, a2_re, a2_im, d_skip)


def _split2(x):
    hi = x.astype(bf16)
    lo = (x - hi.astype(f32)).astype(bf16)
    return hi, lo


def _log_sigmoid(x):
    return -(jnp.maximum(-x, 0.0) + jnp.log1p(jnp.exp(-jnp.abs(x))))


def _gla_body(q_ref, kt_ref, v_ref, r_ref, g_ref, wa_ref, ba_ref, gn_ref,
              o_ref, s_ref):
    c = GLA_CHUNK
    tg = q_ref.shape[0]

    @pl.when(pl.program_id(1) == 0)
    def _():
        s_ref[...] = jnp.zeros_like(s_ref)

    ts = GLA_SUB
    nc = ts // c
    nh = GLA_HEADS
    ri = lax.broadcasted_iota(i32, (ts, ts), 0)
    ci = lax.broadcasted_iota(i32, (ts, ts), 1)
    same = (ri // c) == (ci // c)
    tril = (same & (ri >= ci)).astype(bf16)
    triu = (same & (ri <= ci)).astype(bf16)
    blk = same.astype(bf16)
    head_of_lane = lax.broadcasted_iota(i32, (c, D_GLA_K), 1) // GLA_DK
    r_idx = lax.broadcasted_iota(i32, (nc * nh * c, ts), 0)
    c_idx = lax.broadcasted_iota(i32, (nc * nh * c, ts), 1)
    causal = (r_idx // (nh * c) == c_idx // c) & (r_idx % c >= c_idx % c)
    chunk_of_lane = lax.broadcasted_iota(i32, (D_GLA_K, ts), 1) // c

    nsub = tg // ts
    subs = [slice(st * ts, (st + 1) * ts) for st in range(nsub)]

    def decay_stage(rs):
        g = g_ref[rs, :]
        la = _log_sigmoid(jnp.dot(g, wa_ref[...], preferred_element_type=f32) + ba_ref[...]) / GLA_TAU
        lat = la.T
        la_hi, la_lo = _split2(la)
        cum = (jnp.dot(tril, la_hi, preferred_element_type=f32)
               + jnp.dot(tril, la_lo, preferred_element_type=f32))
        lat_hi, lat_lo = _split2(lat)
        cumt = (jnp.dot(lat_hi, triu, preferred_element_type=f32)
                + jnp.dot(lat_lo, triu, preferred_element_type=f32))
        clt = (jnp.dot(lat_hi, blk, preferred_element_type=f32)
               + jnp.dot(lat_lo, blk, preferred_element_type=f32))
        qt = q_ref[rs, :].astype(f32) * jnp.exp(cum)
        ktt = kt_ref[:, rs].astype(f32)
        k_t = (ktt * jnp.exp(-cumt)).astype(bf16)
        k_end = ktt * jnp.exp(clt - cumt)
        q_stack = jnp.concatenate(
            [jnp.where(head_of_lane == h, qt[cc * c:(cc + 1) * c, :], 0.0)
             for cc in range(nc) for h in range(nh)], axis=0).astype(bf16)
        k_stack = jnp.concatenate(
            [jnp.where(chunk_of_lane == cc, k_end, 0.0) for cc in range(nc)], axis=0).astype(bf16)
        dec = [jnp.exp(clt[:, cc * c:cc * c + 1]) for cc in range(nc)]
        return q_stack, k_t, k_stack, dec

    def matmul_stage(rs, staged, state):
        q_stack, k_t, k_stack, dec = staged
        v = v_ref[rs, :]
        scores = jnp.dot(q_stack, k_t, preferred_element_type=f32)
        scores = jnp.where(causal, scores, 0.0).astype(bf16)
        o_full = jnp.dot(scores, v, preferred_element_type=f32)
        upd_full = jnp.dot(k_stack, v, preferred_element_type=f32)
        o_inter = []
        for cc in range(nc):
            o_inter.append(jnp.dot(q_stack[cc * nh * c:(cc + 1) * nh * c, :], state.astype(bf16),
                                   preferred_element_type=f32))
            upd = jnp.concatenate(
                [upd_full[cc * D_GLA_K + h * GLA_DK:cc * D_GLA_K + (h + 1) * GLA_DK,
                          h * GLA_DV:(h + 1) * GLA_DV] for h in range(nh)], axis=0)
            state = dec[cc] * state + upd
        return o_full, o_inter, state

    def output_stage(rs, o_full, o_inter):
        rows = []
        for cc in range(nc):
            outs = []
            for h in range(nh):
                r0 = (cc * nh + h) * c
                o = o_full[r0:r0 + c, h * GLA_DV:(h + 1) * GLA_DV] + o_inter[cc][h * c:(h + 1) * c, :]
                o = o * lax.rsqrt(jnp.mean(o * o, axis=-1, keepdims=True) + EPS)
                outs.append(o)
            rows.append(jnp.concatenate(outs, axis=1))
        o_cat = jnp.concatenate(rows, axis=0)
        o_ref[rs, :] = (o_cat * gn_ref[...] * jax.nn.silu(r_ref[rs, :].astype(f32))).astype(bf16)

    state = s_ref[...]
    staged = decay_stage(subs[0])
    for st in range(nsub):
        nxt = decay_stage(subs[st + 1]) if st + 1 < nsub else None
        o_full, o_inter, state = matmul_stage(subs[st], staged, state)
        output_stage(subs[st], o_full, o_inter)
        staged = nxt
    s_ref[...] = state


def _gla(l, q3, kt3, v3, r3, g3, wa, ba, gn):
    bsz, seq, _ = q3.shape
    tg = TG_GLA
    tok = lambda b, i: (b, i, 0)
    return pl.pallas_call(
        _gla_body,
        grid=(bsz, seq // tg),
        in_specs=[pl.BlockSpec((None, tg, D_GLA_K), tok),
                  pl.BlockSpec((None, D_GLA_K, tg), lambda b, i: (b, 0, i)),
                  pl.BlockSpec((None, tg, D_GLA), tok),
                  pl.BlockSpec((None, tg, D_GLA), tok),
                  pl.BlockSpec((None, tg, LANE), tok),
                  _layer(wa, l), _layer(ba, l), _layer(gn, l)],
        out_specs=pl.BlockSpec((None, tg, D_GLA), tok),
        out_shape=SDS((bsz, seq, D_GLA), bf16),
        scratch_shapes=[pltpu.VMEM((D_GLA_K, GLA_DV), f32)],
        compiler_params=_cparams("parallel", "arbitrary"),
        name="gla",
    )(q3, kt3, v3, r3, g3, wa, ba, gn)


def _mix_out(x_ref, ys_ref, yg_ref, wgl_ref, bg_ref, gs_ref, wo_ref):
    y = ys_ref[...]
    z = jnp.dot(y, wgl_ref[...], preferred_element_type=f32) + bg_ref[...]
    yf = y.astype(f32) * jax.nn.sigmoid(z)
    ys = _rms(yf, gs_ref[...]).astype(bf16)
    acc = jnp.dot(ys, wo_ref[0:D_S5, :], preferred_element_type=f32)
    acc = acc + jnp.dot(yg_ref[...], wo_ref[D_S5:, :], preferred_element_type=f32)
    return x_ref[...] + acc


def _mix_specs(l, tm, w_glu, b_glu, g_s5, w_out):
    row = lambda i: (i, 0)
    once = dict(pipeline_mode=pl.Buffered(1))
    return [pl.BlockSpec((tm, D_MODEL), row),
            pl.BlockSpec((tm, D_S5), row),
            pl.BlockSpec((tm, D_GLA), row),
            _layer(w_glu, l, **once), _layer(b_glu, l), _layer(g_s5, l), _layer(w_out, l, **once)]


def _mix_ffn_body(x_ref, ys_ref, yg_ref, wgl_ref, bg_ref, gs_ref, wo_ref,
                  g_ref, wg_ref, wu_ref, wd_ref, o_ref):
    x = _mix_out(x_ref, ys_ref, yg_ref, wgl_ref, bg_ref, gs_ref, wo_ref)
    hn = _rms(x, g_ref[...]).astype(bf16)
    gate = jnp.dot(hn, wg_ref[...], preferred_element_type=f32)
    up = jnp.dot(hn, wu_ref[...], preferred_element_type=f32)
    act = (jax.nn.silu(gate) * up).astype(bf16)
    o_ref[...] = x + jnp.dot(act, wd_ref[...], preferred_element_type=f32)


def _mix_ffn(l, i_ffn, x2, ys, yg, w_glu, b_glu, g_s5, w_out, g, wg, wu, wd):
    n = x2.shape[0]
    tm = TM_FFN
    row = lambda i: (i, 0)
    once = dict(pipeline_mode=pl.Buffered(1))
    return pl.pallas_call(
        _mix_ffn_body,
        grid=(n // tm,),
        in_specs=_mix_specs(l, tm, w_glu, b_glu, g_s5, w_out) + [
            _layer(g, l), _layer(wg, i_ffn, **once), _layer(wu, i_ffn, **once),
            _layer(wd, i_ffn, **once)],
        out_specs=pl.BlockSpec((tm, D_MODEL), row),
        out_shape=SDS((n, D_MODEL), f32),
        compiler_params=_cparams("parallel"),
        name="mix_ffn",
    )(x2, ys, yg, w_glu, b_glu, g_s5, w_out, g, wg, wu, wd)


def _pack_bf16_pairs(a):
    bits = pltpu.bitcast(a.astype(bf16).astype(f32), u32)
    half = a.shape[1] // 2
    return bits[:, :half] | (bits[:, half:] >> 16)


def _unpack_bf16_pairs(p):
    hi = pltpu.bitcast(p & jnp.uint32(0xFFFF0000), f32).astype(bf16)
    lo = pltpu.bitcast(p << 16, f32).astype(bf16)
    return hi, lo


def _mix_router_body(x_ref, ys_ref, yg_ref, wgl_ref, bg_ref, gs_ref, wo_ref,
                     g_ref, wh_ref, wl_ref, xo_ref, hp_ref, meta_ref, wcol_ref, cnt_ref, carry):
    tm = x_ref.shape[0]

    @pl.when(pl.program_id(0) == 0)
    def _():
        carry[...] = jnp.zeros_like(carry)

    x = _mix_out(x_ref, ys_ref, yg_ref, wgl_ref, bg_ref, gs_ref, wo_ref)
    xo_ref[...] = x
    hn = _rms(x, g_ref[...])
    packed = _pack_bf16_pairs(hn)
    hp_ref[0] = packed[:, :SC_ROW]
    hp_ref[1] = packed[:, SC_ROW:]

    h_hi, h_lo = _split2(hn)
    nt = (((1,), (1,)), ((), ()))
    logits = (lax.dot_general(wh_ref[...], h_hi, nt, preferred_element_type=f32)
              + lax.dot_general(wh_ref[...], h_lo, nt, preferred_element_type=f32)
              + lax.dot_general(wl_ref[...], h_hi, nt, preferred_element_type=f32))
    er = logits.shape[0]
    row = lax.broadcasted_iota(i32, (er, tm), 0)
    neg = jnp.float32(-jnp.inf)
    logits = jnp.where(row < N_EXPERTS, logits, neg)
    m1 = jnp.max(logits, axis=0, keepdims=True)
    i1 = jnp.min(jnp.where(logits == m1, row, er), axis=0, keepdims=True)
    l2 = jnp.where(row == i1, neg, logits)
    m2 = jnp.max(l2, axis=0, keepdims=True)
    i2 = jnp.min(jnp.where(l2 == m2, row, er), axis=0, keepdims=True)
    e21 = jnp.exp(m2 - m1)
    w1 = 1.0 / (1.0 + e21)
    w2 = e21 / (1.0 + e21)

    sel1 = row == i1
    sel2 = row == i2
    sel = (sel1 | sel2).astype(f32)
    ri = lax.broadcasted_iota(i32, (tm, tm), 0)
    ci = lax.broadcasted_iota(i32, (tm, tm), 1)
    triu = (ri <= ci).astype(bf16)
    incl = jnp.dot(sel.astype(bf16), triu, preferred_element_type=f32)
    rank = incl - sel + carry[:, 0:1]
    r1 = jnp.sum(jnp.where(sel1, rank, 0.0), axis=0, keepdims=True)
    r2 = jnp.sum(jnp.where(sel2, rank, 0.0), axis=0, keepdims=True)
    new_cnt = carry[:, 0:1] + incl[:, tm - 1:tm]
    carry[...] = jnp.broadcast_to(new_cnt, carry.shape)
    cnt_ref[...] = jnp.broadcast_to(new_cnt, cnt_ref.shape)

    srow = lax.broadcasted_iota(i32, (SUBLANE, tm), 0)
    meta = jnp.where(srow == 0, i1.astype(f32), 0.0)
    meta = jnp.where(srow == 1, i2.astype(f32), meta)
    meta = jnp.where(srow == 2, r1, meta)
    meta = jnp.where(srow == 3, r2, meta)
    meta_ref[...] = meta
    prow = lax.broadcasted_iota(i32, (LANE, tm), 0)
    wpad = jnp.where(prow == 0, w1, jnp.where(prow == 1, w2, 0.0))
    wcol_ref[...] = wpad.T


def _mix_router(l, x2, ys, yg, w_glu, b_glu, g_s5, w_out, g, w_hi, w_lo):
    n = x2.shape[0]
    tm = TM_ROUTE
    row = lambda i: (i, 0)
    const = lambda i: (0, 0)
    return pl.pallas_call(
        _mix_router_body,
        grid=(n // tm,),
        in_specs=_mix_specs(l, tm, w_glu, b_glu, g_s5, w_out) + [
            _layer(g, l),
            pl.BlockSpec(w_hi.shape, const),
            pl.BlockSpec(w_lo.shape, const)],
        out_specs=[pl.BlockSpec((tm, D_MODEL), row),
                   pl.BlockSpec((2, tm, SC_ROW), lambda i: (0, i, 0)),
                   pl.BlockSpec((SUBLANE, tm), lambda i: (0, i)),
                   pl.BlockSpec((tm, LANE), row),
                   pl.BlockSpec(w_hi.shape[:1] + (LANE,), const)],
        out_shape=[SDS((n, D_MODEL), f32), SDS((2, n, SC_ROW), u32), SDS((SUBLANE, n), f32),
                   SDS((n, LANE), f32), SDS(w_hi.shape[:1] + (LANE,), f32)],
        scratch_shapes=[pltpu.VMEM(w_hi.shape[:1] + (LANE,), f32)],
        compiler_params=_cparams("arbitrary"),
        name="mix_router",
    )(x2, ys, yg, w_glu, b_glu, g_s5, w_out, g, w_hi, w_lo)


def _sc_gather(table, idx):
    ni = idx.shape[0]
    mesh = plsc.VectorSubcoreMesh(core_axis_name="core", subcore_axis_name="subcore")
    idx2 = idx.reshape(1, ni)

    @pl.kernel(out_type=SDS((ni, SC_ROW), table.dtype), mesh=mesh)
    def kern(t_hbm, i_hbm, o_hbm):
        def body(i_vmem, o_vmem):
            pltpu.sync_copy(t_hbm.at[i_vmem.at[0]], o_vmem)

        pltpu.emit_pipeline(
            body, grid=(ni // SC_WINDOW,),
            in_specs=[pl.BlockSpec((1, SC_WINDOW), index_map=lambda i: (0, i))],
            out_specs=[pl.BlockSpec((SC_WINDOW, SC_ROW), index_map=lambda i: (i, 0))],
            core_axis_name=("core", "subcore"),
            dimension_semantics=(pltpu.PARALLEL,),
        )(i_hbm, o_hbm)

    return kern(table, idx2)


def _sc_scatter2(x, idx0, idx1, nrows):
    ni = x.shape[0]
    mesh = plsc.VectorSubcoreMesh(core_axis_name="core", subcore_axis_name="subcore")

    @pl.kernel(out_type=SDS((nrows, SC_ROW), x.dtype), mesh=mesh)
    def kern(x_hbm, i0_hbm, i1_hbm, o_hbm):
        def body(x_vmem, i0_vmem, i1_vmem):
            pltpu.sync_copy(x_vmem, o_hbm.at[i0_vmem.at[0]])
            pltpu.sync_copy(x_vmem, o_hbm.at[i1_vmem.at[0]])

        pltpu.emit_pipeline(
            body, grid=(ni // SC_WINDOW,),
            in_specs=[pl.BlockSpec((SC_WINDOW, SC_ROW), index_map=lambda i: (i, 0)),
                      pl.BlockSpec((1, SC_WINDOW), index_map=lambda i: (0, i)),
                      pl.BlockSpec((1, SC_WINDOW), index_map=lambda i: (0, i))],
            out_specs=[],
            core_axis_name=("core", "subcore"),
            dimension_semantics=(pltpu.PARALLEL,),
        )(x_hbm, i0_hbm, i1_hbm)

    return kern(x, idx0.reshape(1, ni), idx1.reshape(1, ni))


def _moe_ffn_body(be_ref, bn_ref, xs_ref, wg_ref, wu_ref, wd_ref, ys_ref, xb, act0, act1, acc):
    i = pl.program_id(0)
    f = pl.program_id(1)
    nf = pl.num_programs(1) - 1
    nvalid = bn_ref[i]
    half = xs_ref.shape[1] // MOE_SUB

    for h in range(MOE_SUB):
        rs = pl.ds(h * half, half)
        live_half = nvalid > h * half

        def gate_up(rs=rs):
            x = xb[rs, :]
            gate = jnp.dot(x, wg_ref[...].astype(bf16), preferred_element_type=f32)
            up = jnp.dot(x, wu_ref[...].astype(bf16), preferred_element_type=f32)
            return (jax.nn.silu(gate) * up).astype(bf16)

        def down(act_ref, rs=rs):
            return jnp.dot(act_ref[rs, :], wd_ref[...].astype(bf16), preferred_element_type=f32)

        @pl.when(live_half & (f == 0))
        def _(rs=rs, h=h, gate_up=gate_up):
            live = lax.broadcasted_iota(i32, (half, SC_ROW), 0) + h * half < nvalid
            hi0, lo0 = _unpack_bf16_pairs(jnp.where(live, xs_ref[0, rs, :], jnp.uint32(0)))
            hi1, lo1 = _unpack_bf16_pairs(jnp.where(live, xs_ref[1, rs, :], jnp.uint32(0)))
            xb[rs, :] = jnp.concatenate([hi0, hi1, lo0, lo1], axis=1)
            acc[rs, :] = jnp.zeros((half, D_MODEL), f32)
            act0[rs, :] = gate_up()

        @pl.when(live_half & (f > 0) & (f < nf) & (f % 2 == 1))
        def _(rs=rs, gate_up=gate_up, down=down):
            act1[rs, :] = gate_up()
            acc[rs, :] += down(act0)

        @pl.when(live_half & (f > 0) & (f < nf) & (f % 2 == 0))
        def _(rs=rs, gate_up=gate_up, down=down):
            act0[rs, :] = gate_up()
            acc[rs, :] += down(act1)

        @pl.when(live_half & (f == nf))
        def _(rs=rs, down=down):
            last = act0 if (D_FF_EXPERT // TF_MOE - 1) % 2 == 0 else act1
            packed = _pack_bf16_pairs(acc[rs, :] + down(last))
            ys_ref[0, rs, :] = packed[:, :SC_ROW]
            ys_ref[1, rs, :] = packed[:, SC_ROW:]

        @pl.when(jnp.logical_not(live_half) & (f == nf))
        def _(rs=rs):
            ys_ref[:, rs, :] = jnp.zeros((2, half, SC_ROW), u32)


def _moe_ffn(i_moe, blk_e, blk_n, xs, wg, wu, wd):
    npad = xs.shape[1]
    tm, tf = TM_MOE, TF_MOE
    nblk = npad // tm
    nf = D_FF_EXPERT // tf

    def nxt(i):
        return jnp.minimum(i + 1, nblk - 1)

    def x_idx(i, f, be, bn):
        return (0, jnp.where(f == nf, nxt(i), i), 0)

    def gu_idx(i, f, be, bn):
        ahead = (f == nf) & (bn[i] > 0) & (bn[nxt(i)] > 0)
        e = jnp.where(ahead, be[nxt(i)], be[i])
        t = jnp.where(ahead, 0, jnp.where(bn[i] > 0, jnp.minimum(f, nf - 1), nf - 1))
        return (i_moe, e, 0, t)

    def d_idx(i, f, be, bn):
        prev = jnp.maximum(i - 1, 0)
        keep = (f == 0) & (i > 0)
        e = jnp.where(keep, be[prev], be[i])
        t = jnp.where(keep | (bn[i] == 0), nf - 1, jnp.maximum(f - 1, 0))
        return (i_moe, e, t, 0)

    grid_spec = pltpu.PrefetchScalarGridSpec(
        num_scalar_prefetch=2,
        grid=(nblk, nf + 1),
        in_specs=[pl.BlockSpec((2, tm, SC_ROW), x_idx),
                  pl.BlockSpec((None, None, D_MODEL, tf), gu_idx),
                  pl.BlockSpec((None, None, D_MODEL, tf), gu_idx),
                  pl.BlockSpec((None, None, tf, D_MODEL), d_idx)],
        out_specs=pl.BlockSpec((2, tm, SC_ROW), lambda i, f, be, bn: (0, i, 0)),
        scratch_shapes=[pltpu.VMEM((tm, D_MODEL), bf16), pltpu.VMEM((tm, tf), bf16),
                        pltpu.VMEM((tm, tf), bf16), pltpu.VMEM((tm, D_MODEL), f32)],
    )
    return pl.pallas_call(
        _moe_ffn_body,
        grid_spec=grid_spec,
        out_shape=SDS((2, npad, SC_ROW), u32),
        compiler_params=_cparams("parallel", "arbitrary"),
        name="moe_ffn",
    )(blk_e, blk_n, xs, wg, wu, wd)


def _combine_body(x_ref, yg_ref, wcol_ref, g_ref, o_ref, *, final_norm):
    w1 = wcol_ref[:, 0:1]
    w2 = wcol_ref[:, 1:2]

    def rows(k):
        hi0, lo0 = _unpack_bf16_pairs(yg_ref[0, k])
        hi1, lo1 = _unpack_bf16_pairs(yg_ref[1, k])
        return jnp.concatenate([hi0, hi1, lo0, lo1], axis=1).astype(f32)

    xo = x_ref[...] + (w1 * rows(0) + w2 * rows(1))
    if final_norm:
        xo = _rms(xo, g_ref[...])
    o_ref[...] = xo


def _combine(x2, yg, wcol, g_final, final_norm):
    n = x2.shape[0]
    tm = TM_ROUTE
    row = lambda i: (i, 0)
    return pl.pallas_call(
        functools.partial(_combine_body, final_norm=final_norm),
        grid=(n // tm,),
        in_specs=[pl.BlockSpec((tm, D_MODEL), row),
                  pl.BlockSpec((2, 2, tm, SC_ROW), lambda i: (0, 0, i, 0)),
                  pl.BlockSpec((tm, LANE), row),
                  pl.BlockSpec((1, D_MODEL), lambda i: (0, 0))],
        out_specs=pl.BlockSpec((tm, D_MODEL), row),
        out_shape=SDS((n, D_MODEL), f32),
        compiler_params=_cparams("parallel"),
        name="moe_combine",
    )(x2, yg, wcol, g_final)


def _moe_layer(l, i_moe, mix_args, g_ffn, w_router, wg, wu, wd, g_final, final_norm):
    n = mix_args[0].shape[0]
    tm = TM_MOE
    npad = 2 * n + N_EXPERTS * tm
    wr = jnp.zeros((2 * SUBLANE, D_MODEL), f32).at[:N_EXPERTS].set(w_router[i_moe].astype(f32).T)
    wr_hi = wr.astype(bf16)
    wr_lo = (wr - wr_hi.astype(f32)).astype(bf16)
    x2, hp, meta, wcol, cnt = _mix_router(l, *mix_args, g_ffn, wr_hi, wr_lo)

    counts = cnt[:N_EXPERTS, 0].astype(i32)
    padded = ((counts + tm - 1) // tm) * tm
    ends = jnp.cumsum(padded)
    offs = ends - padded
    ids = jnp.arange(N_EXPERTS, dtype=i32)
    e12 = meta[0:2].astype(i32)
    r12 = meta[2:4].astype(i32)
    pos_t = r12 + jnp.sum(jnp.where(e12[..., None] == ids, offs, 0), axis=-1)
    blk_start = jnp.arange(npad // tm, dtype=i32) * tm
    blk_e = jnp.minimum(jnp.sum((blk_start[:, None] >= ends[None, :]).astype(i32), axis=1), N_EXPERTS - 1)
    own = blk_e[:, None] == ids
    blk_end = jnp.sum(jnp.where(own, offs + counts, 0), axis=1)
    blk_n = jnp.where(blk_start < ends[-1], jnp.clip(blk_end - blk_start, 0, tm), 0)
    last_e = jnp.max(jnp.where(blk_n > 0, blk_e, 0))
    blk_e = jnp.where(blk_n > 0, blk_e, last_e)

    xs = _sc_scatter2(hp.reshape(2 * n, SC_ROW),
                      jnp.concatenate([pos_t[0], pos_t[0] + npad]),
                      jnp.concatenate([pos_t[1], pos_t[1] + npad]),
                      2 * npad).reshape(2, npad, SC_ROW)
    ys = _moe_ffn(i_moe, blk_e, blk_n, xs, wg, wu, wd)
    gidx = jnp.concatenate([pos_t.reshape(-1), pos_t.reshape(-1) + npad])
    yg = _sc_gather(ys.reshape(2 * npad, SC_ROW), gidx).reshape(2, 2, n, SC_ROW)
    return _combine(x2, yg, wcol, g_final, final_norm)


def _rows(v):
    return v.astype(f32)[:, None, :]


def kernel(x, norm_mix, w_in, s5_lambda_re, s5_lambda_im, s5_log_dt, s5_b_re, s5_b_im, s5_c_re, s5_c_im, s5_d, s5_w_glu, s5_b_glu, s5_out_norm, gla_w_a2, gla_b_a2, gla_out_norm, w_out, norm_ffn, ffn_w_gate, ffn_w_up, ffn_w_down, moe_w_router, moe_w_gate, moe_w_up, moe_w_down, norm_final):
    bsz, seq, _ = x.shape
    n = bsz * seq
    depth = w_in.shape[0]
    x2 = x.reshape(n, D_MODEL)

    w_gp = jnp.zeros((depth, D_MODEL, LANE), f32).at[:, :, :GLA_GATE_RANK].set(w_in[:, :, 2048:2064])
    w_cat = jnp.concatenate([w_in[:, :, 0:512], w_in[:, :, 512:768] * (GLA_DK ** -0.5),
                             w_in[:, :, 1024:1536], w_in[:, :, 1536:2048], w_gp], axis=2).astype(bf16)
    w_kt = jnp.swapaxes(w_in[:, :, 768:1024], 1, 2).astype(bf16)
    s5_mats = jax.vmap(_s5_prep)(s5_lambda_re, s5_lambda_im, s5_log_dt, s5_b_re, s5_b_im, s5_c_re, s5_c_im)
    wa = jnp.zeros((depth, LANE, D_GLA_K), f32).at[:, :GLA_GATE_RANK].set(gla_w_a2).astype(bf16)
    g_mix, g_ffn, d_skip = _rows(norm_mix), _rows(norm_ffn), _rows(s5_d)
    ba, gn = _rows(gla_b_a2), _rows(gla_out_norm)
    mix_w = (s5_w_glu.astype(bf16), _rows(s5_b_glu), _rows(s5_out_norm), w_out.astype(bf16))
    ffn_w = (ffn_w_gate.astype(bf16), ffn_w_up.astype(bf16), ffn_w_down.astype(bf16))
    g_final = norm_final.astype(f32).reshape(1, D_MODEL)

    for l in range(depth):
        u, q, kt, v, r, gl = _inproj(l, x2, g_mix, w_cat, w_kt, bsz, seq)
        ys = _s5(l, u.reshape(bsz, seq, D_S5), *s5_mats, d_skip)
        yg = _gla(l, q.reshape(bsz, seq, D_GLA_K), kt, v.reshape(bsz, seq, D_GLA),
                  r.reshape(bsz, seq, D_GLA), gl.reshape(bsz, seq, LANE), wa, ba, gn)
        mix_args = (x2, ys.reshape(n, D_S5), yg.reshape(n, D_GLA)) + mix_w

        last = l == depth - 1
        if l % 2 == 0:
            x2 = _mix_ffn(l, l // 2, *mix_args, g_ffn, *ffn_w)
            if last:
                x2 = _final_norm(x2, g_final)
        else:
            x2 = _moe_layer(l, l // 2, mix_args, g_ffn, moe_w_router, moe_w_gate, moe_w_up,
                            moe_w_down, g_final, last)
    return x2.reshape(bsz, seq, D_MODEL)


def _final_norm_body(x_ref, g_ref, o_ref):
    o_ref[...] = _rms(x_ref[...], g_ref[...])


def _final_norm(x2, g):
    n = x2.shape[0]
    tm = TM_ROUTE
    return pl.pallas_call(
        _final_norm_body,
        grid=(n // tm,),
        in_specs=[pl.BlockSpec((tm, D_MODEL), lambda i: (i, 0)),
                  pl.BlockSpec((1, D_MODEL), lambda i: (0, 0))],
        out_specs=pl.BlockSpec((tm, D_MODEL), lambda i: (i, 0)),
        out_shape=SDS((n, D_MODEL), f32),
        compiler_params=_cparams("parallel"),
        name="final_norm",
    )(x2, g)
```

```python
import functools
import math

import jax
import jax.numpy as jnp
from jax import lax
from jax.experimental import pallas as pl
from jax.experimental.pallas import tpu as pltpu
from jax.experimental.pallas import tpu_sc as plsc

f32 = jnp.float32
bf16 = jnp.bfloat16
u32 = jnp.uint32
i32 = jnp.int32
SDS = jax.ShapeDtypeStruct

D_MODEL = 1024
D_S5 = 512
S5_GROUP = 16
S5_GROUPS = 32
S5_STATE = 64
N_STATE = S5_GROUPS * S5_STATE
D_GLA = 512
GLA_HEADS = 4
GLA_DV = 128
GLA_DK = 64
D_GLA_K = 256
GLA_GATE_RANK = 16
GLA_TAU = 16.0
GLA_CHUNK = 64
D_FF = 2816
N_EXPERTS = 8
D_FF_EXPERT = 3584
EPS = 1e-6

LANE = 128
SUBLANE = 8
VMEM_LIMIT = 56 * 1024 * 1024

TM_PROJ = 1024
TT_S5 = 256
S5_SLABS = 4
S5_PITCH_PAD = 8
TG_GLA = 1024
GLA_SUB = 256
TM_FFN = 512
TM_MOE = 2048
MOE_SUB = 2
TF_MOE = 512
TM_ROUTE = 512
SC_WINDOW = 128
SC_ROW = 256


def _cparams(*sem):
    return pltpu.CompilerParams(dimension_semantics=sem, vmem_limit_bytes=VMEM_LIMIT)


def _layer(arr, l, **kw):
    return pl.BlockSpec((None,) + arr.shape[1:], lambda *_: (l,) + (0,) * (arr.ndim - 1), **kw)


def _rms(x, g):
    ms = jnp.mean(x * x, axis=-1, keepdims=True)
    return x * lax.rsqrt(ms + EPS) * g


def _inproj_body(x_ref, g_ref, w_ref, wkt_ref, u_ref, q_ref, kt_ref, v_ref, r_ref, gl_ref):
    hn = _rms(x_ref[...], g_ref[...]).astype(bf16)

    def proj(lo, hi):
        return jnp.dot(hn, w_ref[:, lo:hi], preferred_element_type=f32).astype(bf16)

    u_ref[...] = proj(0, 512)
    q_ref[...] = proj(512, 768)
    v_ref[...] = proj(768, 1280)
    r_ref[...] = proj(1280, 1792)
    gl_ref[...] = proj(1792, 1920)
    kt_ref[...] = lax.dot_general(wkt_ref[...], hn, (((1,), (1,)), ((), ())),
                                  preferred_element_type=f32).astype(bf16)


def _inproj(l, x2, g, w_cat, w_kt, bsz, seq):
    n = x2.shape[0]
    tm = TM_PROJ
    per_b = seq // tm
    row = lambda i: (i, 0)
    return pl.pallas_call(
        _inproj_body,
        grid=(n // tm,),
        in_specs=[pl.BlockSpec((tm, D_MODEL), row),
                  _layer(g, l), _layer(w_cat, l), _layer(w_kt, l)],
        out_specs=[pl.BlockSpec((tm, D_S5), row),
                   pl.BlockSpec((tm, D_GLA_K), row),
                   pl.BlockSpec((None, D_GLA_K, tm), lambda i: (i // per_b, 0, i % per_b)),
                   pl.BlockSpec((tm, D_GLA), row),
                   pl.BlockSpec((tm, D_GLA), row),
                   pl.BlockSpec((tm, LANE), row)],
        out_shape=[SDS((n, D_S5), bf16), SDS((n, D_GLA_K), bf16),
                   SDS((bsz, D_GLA_K, seq), bf16), SDS((n, D_GLA), bf16),
                   SDS((n, D_GLA), bf16), SDS((n, LANE), bf16)],
        compiler_params=_cparams("parallel"),
        name="inproj",
    )(x2, g, w_cat, w_kt)


N_SLAB = N_STATE // LANE


def _s5_prep(lam_re, lam_im, log_dt, b_re, b_im, c_re, c_im):
    lr = jnp.minimum(lam_re.astype(f32), -1e-4)
    li = lam_im.astype(f32)
    dt = jnp.exp(log_dt.astype(f32))[:, None]
    mag = jnp.exp(lr * dt)
    ab_re = mag * jnp.cos(li * dt)
    ab_im = mag * jnp.sin(li * dt)
    nr = ab_re - 1.0
    ni = ab_im
    den = lr * lr + li * li
    f_re = (nr * lr + ni * li) / den
    f_im = (ni * lr - nr * li) / den
    br = b_re.astype(f32)
    bi = b_im.astype(f32)
    bb_re = f_re[..., None] * br - f_im[..., None] * bi
    bb_im = f_re[..., None] * bi + f_im[..., None] * br
    abb_re = ab_re[..., None] * bb_re - ab_im[..., None] * bb_im
    abb_im = ab_re[..., None] * bb_im + ab_im[..., None] * bb_re
    cr = c_re.astype(f32)
    ci = c_im.astype(f32)
    ca_re = cr * ab_re[:, None, :] - ci * ab_im[:, None, :]
    ca_im = cr * ab_im[:, None, :] + ci * ab_re[:, None, :]
    cb = jnp.einsum('gop,gpc->goc', cr, bb_re) - jnp.einsum('gop,gpc->goc', ci, bb_im)

    def b_tiles(bb):
        bt = jnp.transpose(bb, (0, 2, 1)).reshape(8, 4, S5_GROUP, S5_STATE)
        gl = jnp.arange(8)[None, :, None]
        gs = jnp.arange(4)[None, None, :]
        nn = jnp.arange(8)[:, None, None]
        sel = (gl == 4 * (nn % 2) + gs).astype(f32)
        t = jnp.einsum('ngs,nscp->ngcsp', sel, bt)
        return t.reshape(8, LANE, 4 * S5_STATE)

    bmat = jnp.concatenate(
        [jnp.concatenate([b_tiles(abb_re), b_tiles(bb_re)], axis=1),
         jnp.concatenate([b_tiles(abb_im), b_tiles(bb_im)], axis=1)], axis=0).astype(bf16)

    def c_tiles(c, sign):
        ct = jnp.transpose(c, (0, 2, 1)).reshape(2, 16, S5_STATE, S5_GROUP)
        eye = jnp.eye(16, dtype=f32)
        t = jnp.einsum('gh,jgpo->jgpho', eye, ct) * sign
        return t.reshape(2, 16 * S5_STATE, 16 * S5_GROUP)

    cmat = jnp.stack([c_tiles(cr, 1.0), c_tiles(ci, -1.0)], axis=1).astype(bf16)
    camat = jnp.stack([c_tiles(ca_re, 1.0), c_tiles(ca_im, -1.0)], axis=1).astype(bf16)
    cbt = jnp.transpose(cb, (0, 2, 1)).reshape(2, 16, S5_GROUP, S5_GROUP)
    cbmat = jnp.einsum('gh,jgco->jgcho', jnp.eye(16, dtype=f32), cbt).reshape(2, 256, 256).astype(bf16)
    a2_re = (ab_re * ab_re - ab_im * ab_im).reshape(N_SLAB, 1, LANE)
    a2_im = (2.0 * ab_re * ab_im).reshape(N_SLAB, 1, LANE)
    return bmat, cmat, camat, cbmat, a2_re, a2_im


def _s5_body(u_ref, bm_ref, cm_ref, cam_ref, cbm_ref, are_ref, aim_ref, d_ref, y_ref,
             wbuf, hbuf, obuf, hstate, zcarry):
    tt = u_ref.shape[1]
    tp = tt // 2
    rows = SUBLANE * tp
    pitch = tp + S5_PITCH_PAD
    nl = D_S5 // LANE

    @pl.when(pl.program_id(0) == 0)
    def _():
        hstate[...] = jnp.zeros_like(hstate)
        zcarry[...] = jnp.zeros_like(zcarry)

    w = pltpu.bitcast(u_ref[...].reshape(SUBLANE * tt, D_S5), u32)
    for b in range(SUBLANE):
        for k in range(nl):
            wbuf[k, b * pitch:b * pitch + tp, :] = w[b * tp:(b + 1) * tp, k * LANE:(k + 1) * LANE]
    w = jnp.concatenate(
        [jnp.concatenate([wbuf[k, pl.ds(m, SUBLANE, stride=pitch), :] for m in range(tp)], axis=0)
         for k in range(nl)], axis=1)
    ue_f = pltpu.bitcast(w << 16, f32)
    uo_f = pltpu.bitcast(w & jnp.uint32(0xFFFF0000), f32)
    ue = ue_f.astype(bf16)
    uo = uo_f.astype(bf16)

    for n in range(2 * 8):
        ks = LANE * ((n % 8) // 2)
        lhs = jnp.concatenate([ue[:, ks:ks + LANE], uo[:, ks:ks + LANE]], axis=1)
        res = jnp.dot(lhs, bm_ref[n], preferred_element_type=f32)
        hbuf[2 * n] = res[:, :LANE]
        hbuf[2 * n + 1] = res[:, LANE:]

    for c0 in range(0, N_SLAB, S5_SLABS):
        ar = [jnp.broadcast_to(are_ref[c0 + s], (SUBLANE, LANE)) for s in range(S5_SLABS)]
        ai = [jnp.broadcast_to(aim_ref[c0 + s], (SUBLANE, LANE)) for s in range(S5_SLABS)]

        def step(t, carry):
            out = []
            sl = pl.ds(pl.multiple_of(t * SUBLANE, SUBLANE), SUBLANE)
            for s in range(S5_SLABS):
                hr, hi = carry[2 * s], carry[2 * s + 1]
                nr = ar[s] * hr - ai[s] * hi + hbuf[c0 + s, sl, :]
                ni = ar[s] * hi + ai[s] * hr + hbuf[N_SLAB + c0 + s, sl, :]
                hbuf[c0 + s, sl, :] = nr
                hbuf[N_SLAB + c0 + s, sl, :] = ni
                out += [nr, ni]
            return tuple(out)

        init = []
        for s in range(S5_SLABS):
            init += [hstate[c0 + s], hstate[N_SLAB + c0 + s]]
        fin = lax.fori_loop(0, tp, step, tuple(init), unroll=4)
        for s in range(S5_SLABS):
            hstate[c0 + s] = fin[2 * s]
            hstate[N_SLAB + c0 + s] = fin[2 * s + 1]

    for j in range(2):
        h_re = jnp.concatenate([hbuf[8 * j + s] for s in range(8)], axis=1).astype(bf16)
        h_im = jnp.concatenate([hbuf[N_SLAB + 8 * j + s] for s in range(8)], axis=1).astype(bf16)
        cs = slice(256 * j, 256 * j + 256)
        dj = d_ref[:, cs]
        yo = jnp.dot(h_re, cm_ref[j, 0], preferred_element_type=f32)
        yo = yo + jnp.dot(h_im, cm_ref[j, 1], preferred_element_type=f32)
        yo = yo + dj * uo_f[:, cs]
        z = jnp.dot(h_re, cam_ref[j, 0], preferred_element_type=f32)
        z = z + jnp.dot(h_im, cam_ref[j, 1], preferred_element_type=f32)
        zs = jnp.concatenate([zcarry[:, cs], z[:rows - SUBLANE, :]], axis=0)
        zcarry[:, cs] = z[rows - SUBLANE:, :]
        ye = zs + jnp.dot(ue[:, cs], cbm_ref[j], preferred_element_type=f32) + dj * ue_f[:, cs]
        ge = pltpu.bitcast(jax.nn.gelu(ye).astype(bf16).astype(f32), u32)
        go = pltpu.bitcast(jax.nn.gelu(yo).astype(bf16).astype(f32), u32)
        packed = (ge >> 16) | go
        obuf[2 * j] = packed[:, :LANE]
        obuf[2 * j + 1] = packed[:, LANE:]

    for b in range(SUBLANE):
        yb = jnp.concatenate(
            [jnp.concatenate([obuf[k, pl.ds(8 * SUBLANE * i + b, SUBLANE, stride=SUBLANE), :]
                              for i in range(tp // SUBLANE)], axis=0) for k in range(nl)], axis=1)
        y_ref[b] = pltpu.bitcast(yb, bf16)


def _s5(l, u3, bmat, cmat, camat, cbmat, a2_re, a2_im, d_skip):
    bsz, seq, _ = u3.shape
    assert bsz == SUBLANE
    tt = TT_S5
    blk = lambda i: (0, i, 0)
    return pl.pallas_call(
        _s5_body,
        grid=(seq // tt,),
        in_specs=[pl.BlockSpec((bsz, tt, D_S5), blk)]
        + [_layer(a, l) for a in (bmat, cmat, camat, cbmat, a2_re, a2_im, d_skip)],
        out_specs=pl.BlockSpec((bsz, tt, D_S5), blk),
        out_shape=SDS((bsz, seq, D_S5), bf16),
        scratch_shapes=[pltpu.VMEM((D_S5 // LANE, bsz * (tt // 2 + S5_PITCH_PAD), LANE), u32),
                        pltpu.VMEM((2 * N_SLAB, bsz * (tt // 2), LANE), f32),
                        pltpu.VMEM((D_S5 // LANE, bsz * (tt // 2), LANE), u32),
                        pltpu.VMEM((2 * N_SLAB, SUBLANE, LANE), f32),
                        pltpu.VMEM((SUBLANE, D_S5), f32)],
        compiler_params=_cparams("arbitrary"),
        name="s5_scan",
    )(u3, bmat, cmat, camat, cbmat, a2_re, a2_im, d_skip)


def _split2(x):
    hi = x.astype(bf16)
    lo = (x - hi.astype(f32)).astype(bf16)
    return hi, lo


def _log_sigmoid(x):
    return -(jnp.maximum(-x, 0.0) + jnp.log1p(jnp.exp(-jnp.abs(x))))


def _gla_body(q_ref, kt_ref, v_ref, r_ref, g_ref, wa_ref, ba_ref, gn_ref,
              o_ref, s_ref):
    c = GLA_CHUNK
    tg = q_ref.shape[0]

    @pl.when(pl.program_id(1) == 0)
    def _():
        s_ref[...] = jnp.zeros_like(s_ref)

    ts = GLA_SUB
    nc = ts // c
    nh = GLA_HEADS
    ri = lax.broadcasted_iota(i32, (ts, ts), 0)
    ci = lax.broadcasted_iota(i32, (ts, ts), 1)
    same = (ri // c) == (ci // c)
    tril = (same & (ri >= ci)).astype(bf16)
    triu = (same & (ri <= ci)).astype(bf16)
    blk = same.astype(bf16)
    head_of_lane = lax.broadcasted_iota(i32, (c, D_GLA_K), 1) // GLA_DK
    r_idx = lax.broadcasted_iota(i32, (nc * nh * c, ts), 0)
    c_idx = lax.broadcasted_iota(i32, (nc * nh * c, ts), 1)
    causal = (r_idx // (nh * c) == c_idx // c) & (r_idx % c >= c_idx % c)
    chunk_of_lane = lax.broadcasted_iota(i32, (D_GLA_K, ts), 1) // c

    nsub = tg // ts
    subs = [slice(st * ts, (st + 1) * ts) for st in range(nsub)]

    def decay_stage(rs):
        g = g_ref[rs, :]
        la = _log_sigmoid(jnp.dot(g, wa_ref[...], preferred_element_type=f32) + ba_ref[...]) / GLA_TAU
        lat = la.T
        la_hi, la_lo = _split2(la)
        cum = (jnp.dot(tril, la_hi, preferred_element_type=f32)
               + jnp.dot(tril, la_lo, preferred_element_type=f32))
        lat_hi, lat_lo = _split2(lat)
        cumt = (jnp.dot(lat_hi, triu, preferred_element_type=f32)
                + jnp.dot(lat_lo, triu, preferred_element_type=f32))
        clt = (jnp.dot(lat_hi, blk, preferred_element_type=f32)
               + jnp.dot(lat_lo, blk, preferred_element_type=f32))
        qt = q_ref[rs, :].astype(f32) * jnp.exp(cum)
        ktt = kt_ref[:, rs].astype(f32)
        k_t = (ktt * jnp.exp(-cumt)).astype(bf16)
        k_end = ktt * jnp.exp(clt - cumt)
        q_stack = jnp.concatenate(
            [jnp.where(head_of_lane == h, qt[cc * c:(cc + 1) * c, :], 0.0)
             for cc in range(nc) for h in range(nh)], axis=0).astype(bf16)
        k_stack = jnp.concatenate(
            [jnp.where(chunk_of_lane == cc, k_end, 0.0) for cc in range(nc)], axis=0).astype(bf16)
        dec = [jnp.exp(clt[:, cc * c:cc * c + 1]) for cc in range(nc)]
        return q_stack, k_t, k_stack, dec

    def matmul_stage(rs, staged, state):
        q_stack, k_t, k_stack, dec = staged
        v = v_ref[rs, :]
        scores = jnp.dot(q_stack, k_t, preferred_element_type=f32)
        scores = jnp.where(causal, scores, 0.0).astype(bf16)
        o_full = jnp.dot(scores, v, preferred_element_type=f32)
        upd_full = jnp.dot(k_stack, v, preferred_element_type=f32)
        o_inter = []
        for cc in range(nc):
            o_inter.append(jnp.dot(q_stack[cc * nh * c:(cc + 1) * nh * c, :], state.astype(bf16),
                                   preferred_element_type=f32))
            upd = jnp.concatenate(
                [upd_full[cc * D_GLA_K + h * GLA_DK:cc * D_GLA_K + (h + 1) * GLA_DK,
                          h * GLA_DV:(h + 1) * GLA_DV] for h in range(nh)], axis=0)
            state = dec[cc] * state + upd
        return o_full, o_inter, state

    def output_stage(rs, o_full, o_inter):
        rows = []
        for cc in range(nc):
            outs = []
            for h in range(nh):
                r0 = (cc * nh + h) * c
                o = o_full[r0:r0 + c, h * GLA_DV:(h + 1) * GLA_DV] + o_inter[cc][h * c:(h + 1) * c, :]
                o = o * lax.rsqrt(jnp.mean(o * o, axis=-1, keepdims=True) + EPS)
                outs.append(o)
            rows.append(jnp.concatenate(outs, axis=1))
        o_cat = jnp.concatenate(rows, axis=0)
        o_ref[rs, :] = (o_cat * gn_ref[...] * jax.nn.silu(r_ref[rs, :].astype(f32))).astype(bf16)

    state = s_ref[...]
    staged = decay_stage(subs[0])
    for st in range(nsub):
        nxt = decay_stage(subs[st + 1]) if st + 1 < nsub else None
        o_full, o_inter, state = matmul_stage(subs[st], staged, state)
        output_stage(subs[st], o_full, o_inter)
        staged = nxt
    s_ref[...] = state


def _gla(l, q3, kt3, v3, r3, g3, wa, ba, gn):
    bsz, seq, _ = q3.shape
    tg = TG_GLA
    tok = lambda b, i: (b, i, 0)
    return pl.pallas_call(
        _gla_body,
        grid=(bsz, seq // tg),
        in_specs=[pl.BlockSpec((None, tg, D_GLA_K), tok),
                  pl.BlockSpec((None, D_GLA_K, tg), lambda b, i: (b, 0, i)),
                  pl.BlockSpec((None, tg, D_GLA), tok),
                  pl.BlockSpec((None, tg, D_GLA), tok),
                  pl.BlockSpec((None, tg, LANE), tok),
                  _layer(wa, l), _layer(ba, l), _layer(gn, l)],
        out_specs=pl.BlockSpec((None, tg, D_GLA), tok),
        out_shape=SDS((bsz, seq, D_GLA), bf16),
        scratch_shapes=[pltpu.VMEM((D_GLA_K, GLA_DV), f32)],
        compiler_params=_cparams("parallel", "arbitrary"),
        name="gla",
    )(q3, kt3, v3, r3, g3, wa, ba, gn)


def _mix_out(x_ref, ys_ref, yg_ref, wgl_ref, bg_ref, gs_ref, wo_ref):
    y = ys_ref[...]
    z = jnp.dot(y, wgl_ref[...], preferred_element_type=f32) + bg_ref[...]
    yf = y.astype(f32) * jax.nn.sigmoid(z)
    ys = _rms(yf, gs_ref[...]).astype(bf16)
    acc = jnp.dot(ys, wo_ref[0:D_S5, :], preferred_element_type=f32)
    acc = acc + jnp.dot(yg_ref[...], wo_ref[D_S5:, :], preferred_element_type=f32)
    return x_ref[...] + acc


def _mix_specs(l, tm, w_glu, b_glu, g_s5, w_out):
    row = lambda i: (i, 0)
    once = dict(pipeline_mode=pl.Buffered(1))
    return [pl.BlockSpec((tm, D_MODEL), row),
            pl.BlockSpec((tm, D_S5), row),
            pl.BlockSpec((tm, D_GLA), row),
            _layer(w_glu, l, **once), _layer(b_glu, l), _layer(g_s5, l), _layer(w_out, l, **once)]


def _mix_ffn_body(x_ref, ys_ref, yg_ref, wgl_ref, bg_ref, gs_ref, wo_ref,
                  g_ref, wg_ref, wu_ref, wd_ref, o_ref):
    x = _mix_out(x_ref, ys_ref, yg_ref, wgl_ref, bg_ref, gs_ref, wo_ref)
    hn = _rms(x, g_ref[...]).astype(bf16)
    gate = jnp.dot(hn, wg_ref[...], preferred_element_type=f32)
    up = jnp.dot(hn, wu_ref[...], preferred_element_type=f32)
    act = (jax.nn.silu(gate) * up).astype(bf16)
    o_ref[...] = x + jnp.dot(act, wd_ref[...], preferred_element_type=f32)


def _mix_ffn(l, i_ffn, x2, ys, yg, w_glu, b_glu, g_s5, w_out, g, wg, wu, wd):
    n = x2.shape[0]
    tm = TM_FFN
    row = lambda i: (i, 0)
    once = dict(pipeline_mode=pl.Buffered(1))
    return pl.pallas_call(
        _mix_ffn_body,
        grid=(n // tm,),
        in_specs=_mix_specs(l, tm, w_glu, b_glu, g_s5, w_out) + [
            _layer(g, l), _layer(wg, i_ffn, **once), _layer(wu, i_ffn, **once),
            _layer(wd, i_ffn, **once)],
        out_specs=pl.BlockSpec((tm, D_MODEL), row),
        out_shape=SDS((n, D_MODEL), f32),
        compiler_params=_cparams("parallel"),
        name="mix_ffn",
    )(x2, ys, yg, w_glu, b_glu, g_s5, w_out, g, wg, wu, wd)


def _pack_bf16_pairs(a):
    bits = pltpu.bitcast(a.astype(bf16).astype(f32), u32)
    half = a.shape[1] // 2
    return bits[:, :half] | (bits[:, half:] >> 16)


def _unpack_bf16_pairs(p):
    hi = pltpu.bitcast(p & jnp.uint32(0xFFFF0000), f32).astype(bf16)
    lo = pltpu.bitcast(p << 16, f32).astype(bf16)
    return hi, lo


def _mix_router_body(x_ref, ys_ref, yg_ref, wgl_ref, bg_ref, gs_ref, wo_ref,
                     g_ref, wh_ref, wl_ref, xo_ref, hp_ref, meta_ref, wcol_ref, cnt_ref, carry):
    tm = x_ref.shape[0]

    @pl.when(pl.program_id(0) == 0)
    def _():
        carry[...] = jnp.zeros_like(carry)

    x = _mix_out(x_ref, ys_ref, yg_ref, wgl_ref, bg_ref, gs_ref, wo_ref)
    xo_ref[...] = x
    hn = _rms(x, g_ref[...])
    packed = _pack_bf16_pairs(hn)
    hp_ref[0] = packed[:, :SC_ROW]
    hp_ref[1] = packed[:, SC_ROW:]

    h_hi, h_lo = _split2(hn)
    nt = (((1,), (1,)), ((), ()))
    logits = (lax.dot_general(wh_ref[...], h_hi, nt, preferred_element_type=f32)
              + lax.dot_general(wh_ref[...], h_lo, nt, preferred_element_type=f32)
              + lax.dot_general(wl_ref[...], h_hi, nt, preferred_element_type=f32))
    er = logits.shape[0]
    row = lax.broadcasted_iota(i32, (er, tm), 0)
    neg = jnp.float32(-jnp.inf)
    logits = jnp.where(row < N_EXPERTS, logits, neg)
    m1 = jnp.max(logits, axis=0, keepdims=True)
    i1 = jnp.min(jnp.where(logits == m1, row, er), axis=0, keepdims=True)
    l2 = jnp.where(row == i1, neg, logits)
    m2 = jnp.max(l2, axis=0, keepdims=True)
    i2 = jnp.min(jnp.where(l2 == m2, row, er), axis=0, keepdims=True)
    e21 = jnp.exp(m2 - m1)
    w1 = 1.0 / (1.0 + e21)
    w2 = e21 / (1.0 + e21)

    sel1 = row == i1
    sel2 = row == i2
    sel = (sel1 | sel2).astype(f32)
    ri = lax.broadcasted_iota(i32, (tm, tm), 0)
    ci = lax.broadcasted_iota(i32, (tm, tm), 1)
    triu = (ri <= ci).astype(bf16)
    incl = jnp.dot(sel.astype(bf16), triu, preferred_element_type=f32)
    rank = incl - sel + carry[:, 0:1]
    r1 = jnp.sum(jnp.where(sel1, rank, 0.0), axis=0, keepdims=True)
    r2 = jnp.sum(jnp.where(sel2, rank, 0.0), axis=0, keepdims=True)
    new_cnt = carry[:, 0:1] + incl[:, tm - 1:tm]
    carry[...] = jnp.broadcast_to(new_cnt, carry.shape)
    cnt_ref[...] = jnp.broadcast_to(new_cnt, cnt_ref.shape)

    srow = lax.broadcasted_iota(i32, (SUBLANE, tm), 0)
    meta = jnp.where(srow == 0, i1.astype(f32), 0.0)
    meta = jnp.where(srow == 1, i2.astype(f32), meta)
    meta = jnp.where(srow == 2, r1, meta)
    meta = jnp.where(srow == 3, r2, meta)
    meta_ref[...] = meta
    prow = lax.broadcasted_iota(i32, (LANE, tm), 0)
    wpad = jnp.where(prow == 0, w1, jnp.where(prow == 1, w2, 0.0))
    wcol_ref[...] = wpad.T


def _mix_router(l, x2, ys, yg, w_glu, b_glu, g_s5, w_out, g, w_hi, w_lo):
    n = x2.shape[0]
    tm = TM_ROUTE
    row = lambda i: (i, 0)
    const = lambda i: (0, 0)
    return pl.pallas_call(
        _mix_router_body,
        grid=(n // tm,),
        in_specs=_mix_specs(l, tm, w_glu, b_glu, g_s5, w_out) + [
            _layer(g, l),
            pl.BlockSpec(w_hi.shape, const),
            pl.BlockSpec(w_lo.shape, const)],
        out_specs=[pl.BlockSpec((tm, D_MODEL), row),
                   pl.BlockSpec((2, tm, SC_ROW), lambda i: (0, i, 0)),
                   pl.BlockSpec((SUBLANE, tm), lambda i: (0, i)),
                   pl.BlockSpec((tm, LANE), row),
                   pl.BlockSpec(w_hi.shape[:1] + (LANE,), const)],
        out_shape=[SDS((n, D_MODEL), f32), SDS((2, n, SC_ROW), u32), SDS((SUBLANE, n), f32),
                   SDS((n, LANE), f32), SDS(w_hi.shape[:1] + (LANE,), f32)],
        scratch_shapes=[pltpu.VMEM(w_hi.shape[:1] + (LANE,), f32)],
        compiler_params=_cparams("arbitrary"),
        name="mix_router",
    )(x2, ys, yg, w_glu, b_glu, g_s5, w_out, g, w_hi, w_lo)


def _sc_gather(table, idx):
    ni = idx.shape[0]
    mesh = plsc.VectorSubcoreMesh(core_axis_name="core", subcore_axis_name="subcore")
    idx2 = idx.reshape(1, ni)

    @pl.kernel(out_type=SDS((ni, SC_ROW), table.dtype), mesh=mesh)
    def kern(t_hbm, i_hbm, o_hbm):
        def body(i_vmem, o_vmem):
            pltpu.sync_copy(t_hbm.at[i_vmem.at[0]], o_vmem)

        pltpu.emit_pipeline(
            body, grid=(ni // SC_WINDOW,),
            in_specs=[pl.BlockSpec((1, SC_WINDOW), index_map=lambda i: (0, i))],
            out_specs=[pl.BlockSpec((SC_WINDOW, SC_ROW), index_map=lambda i: (i, 0))],
            core_axis_name=("core", "subcore"),
            dimension_semantics=(pltpu.PARALLEL,),
        )(i_hbm, o_hbm)

    return kern(table, idx2)


def _sc_scatter2(x, idx0, idx1, nrows):
    ni = x.shape[0]
    mesh = plsc.VectorSubcoreMesh(core_axis_name="core", subcore_axis_name="subcore")

    @pl.kernel(out_type=SDS((nrows, SC_ROW), x.dtype), mesh=mesh)
    def kern(x_hbm, i0_hbm, i1_hbm, o_hbm):
        def body(x_vmem, i0_vmem, i1_vmem):
            pltpu.sync_copy(x_vmem, o_hbm.at[i0_vmem.at[0]])
            pltpu.sync_copy(x_vmem, o_hbm.at[i1_vmem.at[0]])

        pltpu.emit_pipeline(
            body, grid=(ni // SC_WINDOW,),
            in_specs=[pl.BlockSpec((SC_WINDOW, SC_ROW), index_map=lambda i: (i, 0)),
                      pl.BlockSpec((1, SC_WINDOW), index_map=lambda i: (0, i)),
                      pl.BlockSpec((1, SC_WINDOW), index_map=lambda i: (0, i))],
            out_specs=[],
            core_axis_name=("core", "subcore"),
            dimension_semantics=(pltpu.PARALLEL,),
        )(x_hbm, i0_hbm, i1_hbm)

    return kern(x, idx0.reshape(1, ni), idx1.reshape(1, ni))


def _moe_ffn_body(be_ref, bn_ref, xs_ref, wg_ref, wu_ref, wd_ref, ys_ref, xb, act0, act1, acc):
    i = pl.program_id(0)
    f = pl.program_id(1)
    nf = pl.num_programs(1) - 1
    nvalid = bn_ref[i]
    half = xs_ref.shape[1] // MOE_SUB

    for h in range(MOE_SUB):
        rs = pl.ds(h * half, half)
        live_half = nvalid > h * half

        def gate_up(rs=rs):
            x = xb[rs, :]
            gate = jnp.dot(x, wg_ref[...].astype(bf16), preferred_element_type=f32)
            up = jnp.dot(x, wu_ref[...].astype(bf16), preferred_element_type=f32)
            return (jax.nn.silu(gate) * up).astype(bf16)

        def down(act_ref, rs=rs):
            return jnp.dot(act_ref[rs, :], wd_ref[...].astype(bf16), preferred_element_type=f32)

        @pl.when(live_half & (f == 0))
        def _(rs=rs, h=h, gate_up=gate_up):
            live = lax.broadcasted_iota(i32, (half, SC_ROW), 0) + h * half < nvalid
            hi0, lo0 = _unpack_bf16_pairs(jnp.where(live, xs_ref[0, rs, :], jnp.uint32(0)))
            hi1, lo1 = _unpack_bf16_pairs(jnp.where(live, xs_ref[1, rs, :], jnp.uint32(0)))
            xb[rs, :] = jnp.concatenate([hi0, hi1, lo0, lo1], axis=1)
            acc[rs, :] = jnp.zeros((half, D_MODEL), f32)
            act0[rs, :] = gate_up()

        @pl.when(live_half & (f > 0) & (f < nf) & (f % 2 == 1))
        def _(rs=rs, gate_up=gate_up, down=down):
            act1[rs, :] = gate_up()
            acc[rs, :] += down(act0)

        @pl.when(live_half & (f > 0) & (f < nf) & (f % 2 == 0))
        def _(rs=rs, gate_up=gate_up, down=down):
            act0[rs, :] = gate_up()
            acc[rs, :] += down(act1)

        @pl.when(live_half & (f == nf))
        def _(rs=rs, down=down):
            last = act0 if (D_FF_EXPERT // TF_MOE - 1) % 2 == 0 else act1
            packed = _pack_bf16_pairs(acc[rs, :] + down(last))
            ys_ref[0, rs, :] = packed[:, :SC_ROW]
            ys_ref[1, rs, :] = packed[:, SC_ROW:]

        @pl.when(jnp.logical_not(live_half) & (f == nf))
        def _(rs=rs):
            ys_ref[:, rs, :] = jnp.zeros((2, half, SC_ROW), u32)


def _moe_ffn(i_moe, blk_e, blk_n, xs, wg, wu, wd):
    npad = xs.shape[1]
    tm, tf = TM_MOE, TF_MOE
    nblk = npad // tm
    nf = D_FF_EXPERT // tf

    def nxt(i):
        return jnp.minimum(i + 1, nblk - 1)

    def x_idx(i, f, be, bn):
        return (0, jnp.where(f == nf, nxt(i), i), 0)

    def gu_idx(i, f, be, bn):
        ahead = (f == nf) & (bn[i] > 0) & (bn[nxt(i)] > 0)
        e = jnp.where(ahead, be[nxt(i)], be[i])
        t = jnp.where(ahead, 0, jnp.where(bn[i] > 0, jnp.minimum(f, nf - 1), nf - 1))
        return (i_moe, e, 0, t)

    def d_idx(i, f, be, bn):
        prev = jnp.maximum(i - 1, 0)
        keep = (f == 0) & (i > 0)
        e = jnp.where(keep, be[prev], be[i])
        t = jnp.where(keep | (bn[i] == 0), nf - 1, jnp.maximum(f - 1, 0))
        return (i_moe, e, t, 0)

    grid_spec = pltpu.PrefetchScalarGridSpec(
        num_scalar_prefetch=2,
        grid=(nblk, nf + 1),
        in_specs=[pl.BlockSpec((2, tm, SC_ROW), x_idx),
                  pl.BlockSpec((None, None, D_MODEL, tf), gu_idx),
                  pl.BlockSpec((None, None, D_MODEL, tf), gu_idx),
                  pl.BlockSpec((None, None, tf, D_MODEL), d_idx)],
        out_specs=pl.BlockSpec((2, tm, SC_ROW), lambda i, f, be, bn: (0, i, 0)),
        scratch_shapes=[pltpu.VMEM((tm, D_MODEL), bf16), pltpu.VMEM((tm, tf), bf16),
                        pltpu.VMEM((tm, tf), bf16), pltpu.VMEM((tm, D_MODEL), f32)],
    )
    return pl.pallas_call(
        _moe_ffn_body,
        grid_spec=grid_spec,
        out_shape=SDS((2, npad, SC_ROW), u32),
        compiler_params=_cparams("parallel", "arbitrary"),
        name="moe_ffn",
    )(blk_e, blk_n, xs, wg, wu, wd)


def _combine_body(x_ref, yg_ref, wcol_ref, g_ref, o_ref, *, final_norm):
    w1 = wcol_ref[:, 0:1]
    w2 = wcol_ref[:, 1:2]

    def rows(k):
        hi0, lo0 = _unpack_bf16_pairs(yg_ref[0, k])
        hi1, lo1 = _unpack_bf16_pairs(yg_ref[1, k])
        return jnp.concatenate([hi0, hi1, lo0, lo1], axis=1).astype(f32)

    xo = x_ref[...] + (w1 * rows(0) + w2 * rows(1))
    if final_norm:
        xo = _rms(xo, g_ref[...])
    o_ref[...] = xo


def _combine(x2, yg, wcol, g_final, final_norm):
    n = x2.shape[0]
    tm = TM_ROUTE
    row = lambda i: (i, 0)
    return pl.pallas_call(
        functools.partial(_combine_body, final_norm=final_norm),
        grid=(n // tm,),
        in_specs=[pl.BlockSpec((tm, D_MODEL), row),
                  pl.BlockSpec((2, 2, tm, SC_ROW), lambda i: (0, 0, i, 0)),
                  pl.BlockSpec((tm, LANE), row),
                  pl.BlockSpec((1, D_MODEL), lambda i: (0, 0))],
        out_specs=pl.BlockSpec((tm, D_MODEL), row),
        out_shape=SDS((n, D_MODEL), f32),
        compiler_params=_cparams("parallel"),
        name="moe_combine",
    )(x2, yg, wcol, g_final)


def _moe_layer(l, i_moe, mix_args, g_ffn, w_router, wg, wu, wd, g_final, final_norm):
    n = mix_args[0].shape[0]
    tm = TM_MOE
    npad = 2 * n + N_EXPERTS * tm
    wr = jnp.zeros((2 * SUBLANE, D_MODEL), f32).at[:N_EXPERTS].set(w_router[i_moe].astype(f32).T)
    wr_hi = wr.astype(bf16)
    wr_lo = (wr - wr_hi.astype(f32)).astype(bf16)
    x2, hp, meta, wcol, cnt = _mix_router(l, *mix_args, g_ffn, wr_hi, wr_lo)

    counts = cnt[:N_EXPERTS, 0].astype(i32)
    padded = ((counts + tm - 1) // tm) * tm
    ends = jnp.cumsum(padded)
    offs = ends - padded
    ids = jnp.arange(N_EXPERTS, dtype=i32)
    e12 = meta[0:2].astype(i32)
    r12 = meta[2:4].astype(i32)
    pos_t = r12 + jnp.sum(jnp.where(e12[..., None] == ids, offs, 0), axis=-1)
    blk_start = jnp.arange(npad // tm, dtype=i32) * tm
    blk_e = jnp.minimum(jnp.sum((blk_start[:, None] >= ends[None, :]).astype(i32), axis=1), N_EXPERTS - 1)
    own = blk_e[:, None] == ids
    blk_end = jnp.sum(jnp.where(own, offs + counts, 0), axis=1)
    blk_n = jnp.where(blk_start < ends[-1], jnp.clip(blk_end - blk_start, 0, tm), 0)
    last_e = jnp.max(jnp.where(blk_n > 0, blk_e, 0))
    blk_e = jnp.where(blk_n > 0, blk_e, last_e)

    xs = _sc_scatter2(hp.reshape(2 * n, SC_ROW),
                      jnp.concatenate([pos_t[0], pos_t[0] + npad]),
                      jnp.concatenate([pos_t[1], pos_t[1] + npad]),
                      2 * npad).reshape(2, npad, SC_ROW)
    ys = _moe_ffn(i_moe, blk_e, blk_n, xs, wg, wu, wd)
    gidx = jnp.concatenate([pos_t.reshape(-1), pos_t.reshape(-1) + npad])
    yg = _sc_gather(ys.reshape(2 * npad, SC_ROW), gidx).reshape(2, 2, n, SC_ROW)
    return _combine(x2, yg, wcol, g_final, final_norm)


def _rows(v):
    return v.astype(f32)[:, None, :]


def kernel(x, norm_mix, w_in, s5_lambda_re, s5_lambda_im, s5_log_dt, s5_b_re, s5_b_im, s5_c_re, s5_c_im, s5_d, s5_w_glu, s5_b_glu, s5_out_norm, gla_w_a2, gla_b_a2, gla_out_norm, w_out, norm_ffn, ffn_w_gate, ffn_w_up, ffn_w_down, moe_w_router, moe_w_gate, moe_w_up, moe_w_down, norm_final):
    bsz, seq, _ = x.shape
    n = bsz * seq
    depth = w_in.shape[0]
    x2 = x.reshape(n, D_MODEL)

    w_gp = jnp.zeros((depth, D_MODEL, LANE), f32).at[:, :, :GLA_GATE_RANK].set(w_in[:, :, 2048:2064])
    w_cat = jnp.concatenate([w_in[:, :, 0:512], w_in[:, :, 512:768] * (GLA_DK ** -0.5),
                             w_in[:, :, 1024:1536], w_in[:, :, 1536:2048], w_gp], axis=2).astype(bf16)
    w_kt = jnp.swapaxes(w_in[:, :, 768:1024], 1, 2).astype(bf16)
    s5_mats = jax.vmap(_s5_prep)(s5_lambda_re, s5_lambda_im, s5_log_dt, s5_b_re, s5_b_im, s5_c_re, s5_c_im)
    wa = jnp.zeros((depth, LANE, D_GLA_K), f32).at[:, :GLA_GATE_RANK].set(gla_w_a2).astype(bf16)
    g_mix, g_ffn, d_skip = _rows(norm_mix), _rows(norm_ffn), _rows(s5_d)
    ba, gn = _rows(gla_b_a2), _rows(gla_out_norm)
    mix_w = (s5_w_glu.astype(bf16), _rows(s5_b_glu), _rows(s5_out_norm), w_out.astype(bf16))
    ffn_w = (ffn_w_gate.astype(bf16), ffn_w_up.astype(bf16), ffn_w_down.astype(bf16))
    g_final = norm_final.astype(f32).reshape(1, D_MODEL)

    for l in range(depth):
        u, q, kt, v, r, gl = _inproj(l, x2, g_mix, w_cat, w_kt, bsz, seq)
        ys = _s5(l, u.reshape(bsz, seq, D_S5), *s5_mats, d_skip)
        yg = _gla(l, q.reshape(bsz, seq, D_GLA_K), kt, v.reshape(bsz, seq, D_GLA),
                  r.reshape(bsz, seq, D_GLA), gl.reshape(bsz, seq, LANE), wa, ba, gn)
        mix_args = (x2, ys.reshape(n, D_S5), yg.reshape(n, D_GLA)) + mix_w

        last = l == depth - 1
        if l % 2 == 0:
            x2 = _mix_ffn(l, l // 2, *mix_args, g_ffn, *ffn_w)
            if last:
                x2 = _final_norm(x2, g_final)
        else:
            x2 = _moe_layer(l, l // 2, mix_args, g_ffn, moe_w_router, moe_w_gate, moe_w_up,
                            moe_w_down, g_final, last)
    return x2.reshape(bsz, seq, D_MODEL)


def _final_norm_body(x_ref, g_ref, o_ref):
    o_ref[...] = _rms(x_ref[...], g_ref[...])


def _final_norm(x2, g):
    n = x2.shape[0]
    tm = TM_ROUTE
    return pl.pallas_call(
        _final_norm_body,
        grid=(n // tm,),
        in_specs=[pl.BlockSpec((tm, D_MODEL), lambda i: (i, 0)),
                  pl.BlockSpec((1, D_MODEL), lambda i: (0, 0))],
        out_specs=pl.BlockSpec((tm, D_MODEL), lambda i: (i, 0)),
        out_shape=SDS((n, D_MODEL), f32),
        compiler_params=_cparams("parallel"),
        name="final_norm",
    )(x2, g)
```

```python
import functools
import math

import jax
import jax.numpy as jnp
from jax import lax
from jax.experimental import pallas as pl
from jax.experimental.pallas import tpu as pltpu
from jax.experimental.pallas import tpu_sc as plsc

f32 = jnp.float32
bf16 = jnp.bfloat16
u32 = jnp.uint32
i32 = jnp.int32
SDS = jax.ShapeDtypeStruct

D_MODEL = 1024
D_S5 = 512
S5_GROUP = 16
S5_GROUPS = 32
S5_STATE = 64
N_STATE = S5_GROUPS * S5_STATE
D_GLA = 512
GLA_HEADS = 4
GLA_DV = 128
GLA_DK = 64
D_GLA_K = 256
GLA_GATE_RANK = 16
GLA_TAU = 16.0
GLA_CHUNK = 64
D_FF = 2816
N_EXPERTS = 8
D_FF_EXPERT = 3584
EPS = 1e-6

LANE = 128
SUBLANE = 8
VMEM_LIMIT = 56 * 1024 * 1024

TM_PROJ = 1024
TT_S5 = 256
S5_SLABS = 4
S5_PITCH_PAD = 8
TG_GLA = 1024
GLA_SUB = 256
TM_FFN = 1024
FFN_CHUNKS = ((0, 1536), (1536, 2816))
TM_MOE = 2048
MOE_SUB = 2
TF_MOE = 512
TM_ROUTE = 1024
SC_WINDOW = 128
SC_ROW = 256


def _cparams(*sem):
    return pltpu.CompilerParams(dimension_semantics=sem, vmem_limit_bytes=VMEM_LIMIT)


def _layer(arr, l, **kw):
    return pl.BlockSpec((None,) + arr.shape[1:], lambda *_: (l,) + (0,) * (arr.ndim - 1), **kw)


def _rms(x, g):
    ms = jnp.mean(x * x, axis=-1, keepdims=True)
    return x * lax.rsqrt(ms + EPS) * g


def _inproj_body(x_ref, g_ref, w_ref, wkt_ref, u_ref, q_ref, kt_ref, v_ref, r_ref, gl_ref):
    hn = _rms(x_ref[...], g_ref[...]).astype(bf16)

    def proj(lo, hi):
        return jnp.dot(hn, w_ref[:, lo:hi], preferred_element_type=f32).astype(bf16)

    u_ref[...] = proj(0, 512)
    q_ref[...] = proj(512, 768)
    v_ref[...] = proj(768, 1280)
    r_ref[...] = proj(1280, 1792)
    gl_ref[...] = proj(1792, 1920)
    kt_ref[...] = lax.dot_general(wkt_ref[...], hn, (((1,), (1,)), ((), ())),
                                  preferred_element_type=f32).astype(bf16)


def _inproj(l, x2, g, w_cat, w_kt, bsz, seq):
    n = x2.shape[0]
    tm = TM_PROJ
    per_b = seq // tm
    row = lambda i: (i, 0)
    return pl.pallas_call(
        _inproj_body,
        grid=(n // tm,),
        in_specs=[pl.BlockSpec((tm, D_MODEL), row),
                  _layer(g, l), _layer(w_cat, l), _layer(w_kt, l)],
        out_specs=[pl.BlockSpec((tm, D_S5), row),
                   pl.BlockSpec((tm, D_GLA_K), row),
                   pl.BlockSpec((None, D_GLA_K, tm), lambda i: (i // per_b, 0, i % per_b)),
                   pl.BlockSpec((tm, D_GLA), row),
                   pl.BlockSpec((tm, D_GLA), row),
                   pl.BlockSpec((tm, LANE), row)],
        out_shape=[SDS((n, D_S5), bf16), SDS((n, D_GLA_K), bf16),
                   SDS((bsz, D_GLA_K, seq), bf16), SDS((n, D_GLA), bf16),
                   SDS((n, D_GLA), bf16), SDS((n, LANE), bf16)],
        compiler_params=_cparams("parallel"),
        name="inproj",
    )(x2, g, w_cat, w_kt)


N_SLAB = N_STATE // LANE


def _s5_prep(lam_re, lam_im, log_dt, b_re, b_im, c_re, c_im):
    lr = jnp.minimum(lam_re.astype(f32), -1e-4)
    li = lam_im.astype(f32)
    dt = jnp.exp(log_dt.astype(f32))[:, None]
    mag = jnp.exp(lr * dt)
    ab_re = mag * jnp.cos(li * dt)
    ab_im = mag * jnp.sin(li * dt)
    nr = ab_re - 1.0
    ni = ab_im
    den = lr * lr + li * li
    f_re = (nr * lr + ni * li) / den
    f_im = (ni * lr - nr * li) / den
    br = b_re.astype(f32)
    bi = b_im.astype(f32)
    bb_re = f_re[..., None] * br - f_im[..., None] * bi
    bb_im = f_re[..., None] * bi + f_im[..., None] * br
    abb_re = ab_re[..., None] * bb_re - ab_im[..., None] * bb_im
    abb_im = ab_re[..., None] * bb_im + ab_im[..., None] * bb_re
    cr = c_re.astype(f32)
    ci = c_im.astype(f32)
    ca_re = cr * ab_re[:, None, :] - ci * ab_im[:, None, :]
    ca_im = cr * ab_im[:, None, :] + ci * ab_re[:, None, :]
    cb = jnp.einsum('gop,gpc->goc', cr, bb_re) - jnp.einsum('gop,gpc->goc', ci, bb_im)

    def b_tiles(bb):
        bt = jnp.transpose(bb, (0, 2, 1)).reshape(8, 4, S5_GROUP, S5_STATE)
        gl = jnp.arange(8)[None, :, None]
        gs = jnp.arange(4)[None, None, :]
        nn = jnp.arange(8)[:, None, None]
        sel = (gl == 4 * (nn % 2) + gs).astype(f32)
        t = jnp.einsum('ngs,nscp->ngcsp', sel, bt)
        return t.reshape(8, LANE, 4 * S5_STATE)

    bmat = jnp.concatenate(
        [jnp.concatenate([b_tiles(abb_re), b_tiles(bb_re)], axis=1),
         jnp.concatenate([b_tiles(abb_im), b_tiles(bb_im)], axis=1)], axis=0).astype(bf16)

    def c_tiles(c, sign):
        ct = jnp.transpose(c, (0, 2, 1)).reshape(2, 16, S5_STATE, S5_GROUP)
        eye = jnp.eye(16, dtype=f32)
        t = jnp.einsum('gh,jgpo->jgpho', eye, ct) * sign
        return t.reshape(2, 16 * S5_STATE, 16 * S5_GROUP)

    cmat = jnp.stack([c_tiles(cr, 1.0), c_tiles(ci, -1.0)], axis=1).astype(bf16)
    camat = jnp.stack([c_tiles(ca_re, 1.0), c_tiles(ca_im, -1.0)], axis=1).astype(bf16)
    cbt = jnp.transpose(cb, (0, 2, 1)).reshape(2, 16, S5_GROUP, S5_GROUP)
    cbmat = jnp.einsum('gh,jgco->jgcho', jnp.eye(16, dtype=f32), cbt).reshape(2, 256, 256).astype(bf16)
    a2_re = (ab_re * ab_re - ab_im * ab_im).reshape(N_SLAB, 1, LANE)
    a2_im = (2.0 * ab_re * ab_im).reshape(N_SLAB, 1, LANE)
    return bmat, cmat, camat, cbmat, a2_re, a2_im


def _s5_body(u_ref, bm_ref, cm_ref, cam_ref, cbm_ref, are_ref, aim_ref, d_ref, y_ref,
             wbuf, hbuf, obuf, hstate, zcarry):
    tt = u_ref.shape[1]
    tp = tt // 2
    rows = SUBLANE * tp
    pitch = tp + S5_PITCH_PAD
    nl = D_S5 // LANE

    @pl.when(pl.program_id(0) == 0)
    def _():
        hstate[...] = jnp.zeros_like(hstate)
        zcarry[...] = jnp.zeros_like(zcarry)

    w = pltpu.bitcast(u_ref[...].reshape(SUBLANE * tt, D_S5), u32)
    for b in range(SUBLANE):
        for k in range(nl):
            wbuf[k, b * pitch:b * pitch + tp, :] = w[b * tp:(b + 1) * tp, k * LANE:(k + 1) * LANE]
    w = jnp.concatenate(
        [jnp.concatenate([wbuf[k, pl.ds(m, SUBLANE, stride=pitch), :] for m in range(tp)], axis=0)
         for k in range(nl)], axis=1)
    ue_f = pltpu.bitcast(w << 16, f32)
    uo_f = pltpu.bitcast(w & jnp.uint32(0xFFFF0000), f32)
    ue = ue_f.astype(bf16)
    uo = uo_f.astype(bf16)

    for n in range(2 * 8):
        ks = LANE * ((n % 8) // 2)
        lhs = jnp.concatenate([ue[:, ks:ks + LANE], uo[:, ks:ks + LANE]], axis=1)
        res = jnp.dot(lhs, bm_ref[n], preferred_element_type=f32)
        hbuf[2 * n] = res[:, :LANE]
        hbuf[2 * n + 1] = res[:, LANE:]

    for c0 in range(0, N_SLAB, S5_SLABS):
        ar = [jnp.broadcast_to(are_ref[c0 + s], (SUBLANE, LANE)) for s in range(S5_SLABS)]
        ai = [jnp.broadcast_to(aim_ref[c0 + s], (SUBLANE, LANE)) for s in range(S5_SLABS)]

        def step(t, carry):
            out = []
            sl = pl.ds(pl.multiple_of(t * SUBLANE, SUBLANE), SUBLANE)
            for s in range(S5_SLABS):
                hr, hi = carry[2 * s], carry[2 * s + 1]
                nr = ar[s] * hr - ai[s] * hi + hbuf[c0 + s, sl, :]
                ni = ar[s] * hi + ai[s] * hr + hbuf[N_SLAB + c0 + s, sl, :]
                hbuf[c0 + s, sl, :] = nr
                hbuf[N_SLAB + c0 + s, sl, :] = ni
                out += [nr, ni]
            return tuple(out)

        init = []
        for s in range(S5_SLABS):
            init += [hstate[c0 + s], hstate[N_SLAB + c0 + s]]
        fin = lax.fori_loop(0, tp, step, tuple(init), unroll=4)
        for s in range(S5_SLABS):
            hstate[c0 + s] = fin[2 * s]
            hstate[N_SLAB + c0 + s] = fin[2 * s + 1]

    for j in range(2):
        h_re = jnp.concatenate([hbuf[8 * j + s] for s in range(8)], axis=1).astype(bf16)
        h_im = jnp.concatenate([hbuf[N_SLAB + 8 * j + s] for s in range(8)], axis=1).astype(bf16)
        cs = slice(256 * j, 256 * j + 256)
        dj = d_ref[:, cs]
        yo = jnp.dot(h_re, cm_ref[j, 0], preferred_element_type=f32)
        yo = yo + jnp.dot(h_im, cm_ref[j, 1], preferred_element_type=f32)
        yo = yo + dj * uo_f[:, cs]
        z = jnp.dot(h_re, cam_ref[j, 0], preferred_element_type=f32)
        z = z + jnp.dot(h_im, cam_ref[j, 1], preferred_element_type=f32)
        zs = jnp.concatenate([zcarry[:, cs], z[:rows - SUBLANE, :]], axis=0)
        zcarry[:, cs] = z[rows - SUBLANE:, :]
        ye = zs + jnp.dot(ue[:, cs], cbm_ref[j], preferred_element_type=f32) + dj * ue_f[:, cs]
        ge = pltpu.bitcast(jax.nn.gelu(ye).astype(bf16).astype(f32), u32)
        go = pltpu.bitcast(jax.nn.gelu(yo).astype(bf16).astype(f32), u32)
        packed = (ge >> 16) | go
        obuf[2 * j] = packed[:, :LANE]
        obuf[2 * j + 1] = packed[:, LANE:]

    for b in range(SUBLANE):
        yb = jnp.concatenate(
            [jnp.concatenate([obuf[k, pl.ds(8 * SUBLANE * i + b, SUBLANE, stride=SUBLANE), :]
                              for i in range(tp // SUBLANE)], axis=0) for k in range(nl)], axis=1)
        y_ref[b] = pltpu.bitcast(yb, bf16)


def _s5(l, u3, bmat, cmat, camat, cbmat, a2_re, a2_im, d_skip):
    bsz, seq, _ = u3.shape
    assert bsz == SUBLANE
    tt = TT_S5
    blk = lambda i: (0, i, 0)
    return pl.pallas_call(
        _s5_body,
        grid=(seq // tt,),
        in_specs=[pl.BlockSpec((bsz, tt, D_S5), blk)]
        + [_layer(a, l) for a in (bmat, cmat, camat, cbmat, a2_re, a2_im, d_skip)],
        out_specs=pl.BlockSpec((bsz, tt, D_S5), blk),
        out_shape=SDS((bsz, seq, D_S5), bf16),
        scratch_shapes=[pltpu.VMEM((D_S5 // LANE, bsz * (tt // 2 + S5_PITCH_PAD), LANE), u32),
                        pltpu.VMEM((2 * N_SLAB, bsz * (tt // 2), LANE), f32),
                        pltpu.VMEM((D_S5 // LANE, bsz * (tt // 2), LANE), u32),
                        pltpu.VMEM((2 * N_SLAB, SUBLANE, LANE), f32),
                        pltpu.VMEM((SUBLANE, D_S5), f32)],
        compiler_params=_cparams("arbitrary"),
        name="s5_scan",
    )(u3, bmat, cmat, camat, cbmat, a2_re, a2_im, d_skip)


def _split2(x):
    hi = x.astype(bf16)
    lo = (x - hi.astype(f32)).astype(bf16)
    return hi, lo


def _log_sigmoid(x):
    return -(jnp.maximum(-x, 0.0) + jnp.log1p(jnp.exp(-jnp.abs(x))))


def _gla_body(q_ref, kt_ref, v_ref, r_ref, g_ref, wa_ref, ba_ref, gn_ref,
              o_ref, s_ref):
    c = GLA_CHUNK
    tg = q_ref.shape[0]

    @pl.when(pl.program_id(1) == 0)
    def _():
        s_ref[...] = jnp.zeros_like(s_ref)

    ts = GLA_SUB
    nc = ts // c
    nh = GLA_HEADS
    ri = lax.broadcasted_iota(i32, (ts, ts), 0)
    ci = lax.broadcasted_iota(i32, (ts, ts), 1)
    same = (ri // c) == (ci // c)
    tril = (same & (ri >= ci)).astype(bf16)
    triu = (same & (ri <= ci)).astype(bf16)
    blk = same.astype(bf16)
    head_of_lane = lax.broadcasted_iota(i32, (c, D_GLA_K), 1) // GLA_DK
    r_idx = lax.broadcasted_iota(i32, (nc * nh * c, ts), 0)
    c_idx = lax.broadcasted_iota(i32, (nc * nh * c, ts), 1)
    causal = (r_idx // (nh * c) == c_idx // c) & (r_idx % c >= c_idx % c)
    chunk_of_lane = lax.broadcasted_iota(i32, (D_GLA_K, ts), 1) // c

    nsub = tg // ts
    subs = [slice(st * ts, (st + 1) * ts) for st in range(nsub)]

    def decay_stage(rs):
        g = g_ref[rs, :]
        la = _log_sigmoid(jnp.dot(g, wa_ref[...], preferred_element_type=f32) + ba_ref[...]) / GLA_TAU
        lat = la.T
        la_hi, la_lo = _split2(la)
        cum = (jnp.dot(tril, la_hi, preferred_element_type=f32)
               + jnp.dot(tril, la_lo, preferred_element_type=f32))
        lat_hi, lat_lo = _split2(lat)
        cumt = (jnp.dot(lat_hi, triu, preferred_element_type=f32)
                + jnp.dot(lat_lo, triu, preferred_element_type=f32))
        clt = (jnp.dot(lat_hi, blk, preferred_element_type=f32)
               + jnp.dot(lat_lo, blk, preferred_element_type=f32))
        qt = q_ref[rs, :].astype(f32) * jnp.exp(cum)
        ktt = kt_ref[:, rs].astype(f32)
        k_t = (ktt * jnp.exp(-cumt)).astype(bf16)
        k_end = ktt * jnp.exp(clt - cumt)
        q_stack = jnp.concatenate(
            [jnp.where(head_of_lane == h, qt[cc * c:(cc + 1) * c, :], 0.0)
             for cc in range(nc) for h in range(nh)], axis=0).astype(bf16)
        k_stack = jnp.concatenate(
            [jnp.where(chunk_of_lane == cc, k_end, 0.0) for cc in range(nc)], axis=0).astype(bf16)
        dec = [jnp.exp(clt[:, cc * c:cc * c + 1]) for cc in range(nc)]
        return q_stack, k_t, k_stack, dec

    def matmul_stage(rs, staged, state):
        q_stack, k_t, k_stack, dec = staged
        v = v_ref[rs, :]
        scores = jnp.dot(q_stack, k_t, preferred_element_type=f32)
        scores = jnp.where(causal, scores, 0.0).astype(bf16)
        o_full = jnp.dot(scores, v, preferred_element_type=f32)
        upd_full = jnp.dot(k_stack, v, preferred_element_type=f32)
        o_inter = []
        for cc in range(nc):
            o_inter.append(jnp.dot(q_stack[cc * nh * c:(cc + 1) * nh * c, :], state.astype(bf16),
                                   preferred_element_type=f32))
            upd = jnp.concatenate(
                [upd_full[cc * D_GLA_K + h * GLA_DK:cc * D_GLA_K + (h + 1) * GLA_DK,
                          h * GLA_DV:(h + 1) * GLA_DV] for h in range(nh)], axis=0)
            state = dec[cc] * state + upd
        return o_full, o_inter, state

    def output_stage(rs, o_full, o_inter):
        rows = []
        for cc in range(nc):
            outs = []
            for h in range(nh):
                r0 = (cc * nh + h) * c
                o = o_full[r0:r0 + c, h * GLA_DV:(h + 1) * GLA_DV] + o_inter[cc][h * c:(h + 1) * c, :]
                o = o * lax.rsqrt(jnp.mean(o * o, axis=-1, keepdims=True) + EPS)
                outs.append(o)
            rows.append(jnp.concatenate(outs, axis=1))
        o_cat = jnp.concatenate(rows, axis=0)
        o_ref[rs, :] = (o_cat * gn_ref[...] * jax.nn.silu(r_ref[rs, :].astype(f32))).astype(bf16)

    state = s_ref[...]
    staged = decay_stage(subs[0])
    for st in range(nsub):
        nxt = decay_stage(subs[st + 1]) if st + 1 < nsub else None
        o_full, o_inter, state = matmul_stage(subs[st], staged, state)
        output_stage(subs[st], o_full, o_inter)
        staged = nxt
    s_ref[...] = state


def _gla(l, q3, kt3, v3, r3, g3, wa, ba, gn):
    bsz, seq, _ = q3.shape
    tg = TG_GLA
    tok = lambda b, i: (b, i, 0)
    return pl.pallas_call(
        _gla_body,
        grid=(bsz, seq // tg),
        in_specs=[pl.BlockSpec((None, tg, D_GLA_K), tok),
                  pl.BlockSpec((None, D_GLA_K, tg), lambda b, i: (b, 0, i)),
                  pl.BlockSpec((None, tg, D_GLA), tok),
                  pl.BlockSpec((None, tg, D_GLA), tok),
                  pl.BlockSpec((None, tg, LANE), tok),
                  _layer(wa, l), _layer(ba, l), _layer(gn, l)],
        out_specs=pl.BlockSpec((None, tg, D_GLA), tok),
        out_shape=SDS((bsz, seq, D_GLA), bf16),
        scratch_shapes=[pltpu.VMEM((D_GLA_K, GLA_DV), f32)],
        compiler_params=_cparams("parallel", "arbitrary"),
        name="gla",
    )(q3, kt3, v3, r3, g3, wa, ba, gn)


def _mix_out(x_ref, ys_ref, yg_ref, wgl_ref, bg_ref, gs_ref, wo_ref):
    y = ys_ref[...]
    z = jnp.dot(y, wgl_ref[...], preferred_element_type=f32) + bg_ref[...]
    yf = y.astype(f32) * jax.nn.sigmoid(z)
    ys = _rms(yf, gs_ref[...]).astype(bf16)
    acc = jnp.dot(ys, wo_ref[0:D_S5, :], preferred_element_type=f32)
    acc = acc + jnp.dot(yg_ref[...], wo_ref[D_S5:, :], preferred_element_type=f32)
    return x_ref[...] + acc


def _mix_specs(l, tm, w_glu, b_glu, g_s5, w_out):
    row = lambda i: (i, 0)
    once = dict(pipeline_mode=pl.Buffered(1))
    return [pl.BlockSpec((tm, D_MODEL), row),
            pl.BlockSpec((tm, D_S5), row),
            pl.BlockSpec((tm, D_GLA), row),
            _layer(w_glu, l, **once), _layer(b_glu, l), _layer(g_s5, l), _layer(w_out, l, **once)]


def _mix_ffn_body(x_ref, ys_ref, yg_ref, wgl_ref, bg_ref, gs_ref, wo_ref,
                  g_ref, wg_ref, wu_ref, wd_ref, o_ref):
    x = _mix_out(x_ref, ys_ref, yg_ref, wgl_ref, bg_ref, gs_ref, wo_ref)
    hn = _rms(x, g_ref[...]).astype(bf16)
    acc = x
    for lo, hi in FFN_CHUNKS:
        gate = jnp.dot(hn, wg_ref[:, lo:hi], preferred_element_type=f32)
        up = jnp.dot(hn, wu_ref[:, lo:hi], preferred_element_type=f32)
        act = (jax.nn.silu(gate) * up).astype(bf16)
        acc = acc + jnp.dot(act, wd_ref[lo:hi, :], preferred_element_type=f32)
    o_ref[...] = acc


def _mix_ffn(l, i_ffn, x2, ys, yg, w_glu, b_glu, g_s5, w_out, g, wg, wu, wd):
    n = x2.shape[0]
    tm = TM_FFN
    row = lambda i: (i, 0)
    once = dict(pipeline_mode=pl.Buffered(1))
    return pl.pallas_call(
        _mix_ffn_body,
        grid=(n // tm,),
        in_specs=_mix_specs(l, tm, w_glu, b_glu, g_s5, w_out) + [
            _layer(g, l), _layer(wg, i_ffn, **once), _layer(wu, i_ffn, **once),
            _layer(wd, i_ffn, **once)],
        out_specs=pl.BlockSpec((tm, D_MODEL), row),
        out_shape=SDS((n, D_MODEL), f32),
        compiler_params=_cparams("parallel"),
        name="mix_ffn",
    )(x2, ys, yg, w_glu, b_glu, g_s5, w_out, g, wg, wu, wd)


def _pack_bf16_pairs(a):
    bits = pltpu.bitcast(a.astype(bf16).astype(f32), u32)
    half = a.shape[1] // 2
    return bits[:, :half] | (bits[:, half:] >> 16)


def _unpack_bf16_pairs(p):
    hi = pltpu.bitcast(p & jnp.uint32(0xFFFF0000), f32).astype(bf16)
    lo = pltpu.bitcast(p << 16, f32).astype(bf16)
    return hi, lo


def _mix_router_body(x_ref, ys_ref, yg_ref, wgl_ref, bg_ref, gs_ref, wo_ref,
                     g_ref, wh_ref, wl_ref, xo_ref, hp_ref, meta_ref, wcol_ref, cnt_ref, carry):
    tm = x_ref.shape[0]

    @pl.when(pl.program_id(0) == 0)
    def _():
        carry[...] = jnp.zeros_like(carry)

    x = _mix_out(x_ref, ys_ref, yg_ref, wgl_ref, bg_ref, gs_ref, wo_ref)
    xo_ref[...] = x
    hn = _rms(x, g_ref[...])
    packed = _pack_bf16_pairs(hn)
    hp_ref[0] = packed[:, :SC_ROW]
    hp_ref[1] = packed[:, SC_ROW:]

    h_hi, h_lo = _split2(hn)
    nt = (((1,), (1,)), ((), ()))
    logits = (lax.dot_general(wh_ref[...], h_hi, nt, preferred_element_type=f32)
              + lax.dot_general(wh_ref[...], h_lo, nt, preferred_element_type=f32)
              + lax.dot_general(wl_ref[...], h_hi, nt, preferred_element_type=f32))
    er = logits.shape[0]
    row = lax.broadcasted_iota(i32, (er, tm), 0)
    neg = jnp.float32(-jnp.inf)
    logits = jnp.where(row < N_EXPERTS, logits, neg)
    m1 = jnp.max(logits, axis=0, keepdims=True)
    i1 = jnp.min(jnp.where(logits == m1, row, er), axis=0, keepdims=True)
    l2 = jnp.where(row == i1, neg, logits)
    m2 = jnp.max(l2, axis=0, keepdims=True)
    i2 = jnp.min(jnp.where(l2 == m2, row, er), axis=0, keepdims=True)
    e21 = jnp.exp(m2 - m1)
    w1 = 1.0 / (1.0 + e21)
    w2 = e21 / (1.0 + e21)

    sel1 = row == i1
    sel2 = row == i2
    sel = (sel1 | sel2).astype(f32)
    ri = lax.broadcasted_iota(i32, (tm, tm), 0)
    ci = lax.broadcasted_iota(i32, (tm, tm), 1)
    triu = (ri <= ci).astype(bf16)
    incl = jnp.dot(sel.astype(bf16), triu, preferred_element_type=f32)
    rank = incl - sel + carry[:, 0:1]
    r1 = jnp.sum(jnp.where(sel1, rank, 0.0), axis=0, keepdims=True)
    r2 = jnp.sum(jnp.where(sel2, rank, 0.0), axis=0, keepdims=True)
    new_cnt = carry[:, 0:1] + incl[:, tm - 1:tm]
    carry[...] = jnp.broadcast_to(new_cnt, carry.shape)
    cnt_ref[...] = jnp.broadcast_to(new_cnt, cnt_ref.shape)

    srow = lax.broadcasted_iota(i32, (SUBLANE, tm), 0)
    meta = jnp.where(srow == 0, i1.astype(f32), 0.0)
    meta = jnp.where(srow == 1, i2.astype(f32), meta)
    meta = jnp.where(srow == 2, r1, meta)
    meta = jnp.where(srow == 3, r2, meta)
    meta_ref[...] = meta
    prow = lax.broadcasted_iota(i32, (LANE, tm), 0)
    wpad = jnp.where(prow == 0, w1, jnp.where(prow == 1, w2, 0.0))
    wcol_ref[...] = wpad.T


def _mix_router(l, x2, ys, yg, w_glu, b_glu, g_s5, w_out, g, w_hi, w_lo):
    n = x2.shape[0]
    tm = TM_ROUTE
    row = lambda i: (i, 0)
    const = lambda i: (0, 0)
    return pl.pallas_call(
        _mix_router_body,
        grid=(n // tm,),
        in_specs=_mix_specs(l, tm, w_glu, b_glu, g_s5, w_out) + [
            _layer(g, l),
            pl.BlockSpec(w_hi.shape, const),
            pl.BlockSpec(w_lo.shape, const)],
        out_specs=[pl.BlockSpec((tm, D_MODEL), row),
                   pl.BlockSpec((2, tm, SC_ROW), lambda i: (0, i, 0)),
                   pl.BlockSpec((SUBLANE, tm), lambda i: (0, i)),
                   pl.BlockSpec((tm, LANE), row),
                   pl.BlockSpec(w_hi.shape[:1] + (LANE,), const)],
        out_shape=[SDS((n, D_MODEL), f32), SDS((2, n, SC_ROW), u32), SDS((SUBLANE, n), f32),
                   SDS((n, LANE), f32), SDS(w_hi.shape[:1] + (LANE,), f32)],
        scratch_shapes=[pltpu.VMEM(w_hi.shape[:1] + (LANE,), f32)],
        compiler_params=_cparams("arbitrary"),
        name="mix_router",
    )(x2, ys, yg, w_glu, b_glu, g_s5, w_out, g, w_hi, w_lo)


def _sc_gather(table, idx):
    ni = idx.shape[0]
    mesh = plsc.VectorSubcoreMesh(core_axis_name="core", subcore_axis_name="subcore")
    idx2 = idx.reshape(1, ni)

    @pl.kernel(out_type=SDS((ni, SC_ROW), table.dtype), mesh=mesh)
    def kern(t_hbm, i_hbm, o_hbm):
        def body(i_vmem, o_vmem):
            pltpu.sync_copy(t_hbm.at[i_vmem.at[0]], o_vmem)

        pltpu.emit_pipeline(
            body, grid=(ni // SC_WINDOW,),
            in_specs=[pl.BlockSpec((1, SC_WINDOW), index_map=lambda i: (0, i))],
            out_specs=[pl.BlockSpec((SC_WINDOW, SC_ROW), index_map=lambda i: (i, 0))],
            core_axis_name=("core", "subcore"),
            dimension_semantics=(pltpu.PARALLEL,),
        )(i_hbm, o_hbm)

    return kern(table, idx2)


def _sc_scatter2(x, idx0, idx1, nrows):
    ni = x.shape[0]
    mesh = plsc.VectorSubcoreMesh(core_axis_name="core", subcore_axis_name="subcore")

    @pl.kernel(out_type=SDS((nrows, SC_ROW), x.dtype), mesh=mesh)
    def kern(x_hbm, i0_hbm, i1_hbm, o_hbm):
        def body(x_vmem, i0_vmem, i1_vmem):
            pltpu.sync_copy(x_vmem, o_hbm.at[i0_vmem.at[0]])
            pltpu.sync_copy(x_vmem, o_hbm.at[i1_vmem.at[0]])

        pltpu.emit_pipeline(
            body, grid=(ni // SC_WINDOW,),
            in_specs=[pl.BlockSpec((SC_WINDOW, SC_ROW), index_map=lambda i: (i, 0)),
                      pl.BlockSpec((1, SC_WINDOW), index_map=lambda i: (0, i)),
                      pl.BlockSpec((1, SC_WINDOW), index_map=lambda i: (0, i))],
            out_specs=[],
            core_axis_name=("core", "subcore"),
            dimension_semantics=(pltpu.PARALLEL,),
        )(x_hbm, i0_hbm, i1_hbm)

    return kern(x, idx0.reshape(1, ni), idx1.reshape(1, ni))


def _moe_ffn_body(be_ref, bn_ref, xs_ref, wg_ref, wu_ref, wd_ref, ys_ref, xb, act0, act1, acc):
    i = pl.program_id(0)
    f = pl.program_id(1)
    nf = pl.num_programs(1) - 1
    nvalid = bn_ref[i]
    half = xs_ref.shape[1] // MOE_SUB

    for h in range(MOE_SUB):
        rs = pl.ds(h * half, half)
        live_half = nvalid > h * half

        def gate_up(rs=rs):
            x = xb[rs, :]
            gate = jnp.dot(x, wg_ref[...].astype(bf16), preferred_element_type=f32)
            up = jnp.dot(x, wu_ref[...].astype(bf16), preferred_element_type=f32)
            return (jax.nn.silu(gate) * up).astype(bf16)

        def down(act_ref, rs=rs):
            return jnp.dot(act_ref[rs, :], wd_ref[...].astype(bf16), preferred_element_type=f32)

        @pl.when(live_half & (f == 0))
        def _(rs=rs, h=h, gate_up=gate_up):
            live = lax.broadcasted_iota(i32, (half, SC_ROW), 0) + h * half < nvalid
            hi0, lo0 = _unpack_bf16_pairs(jnp.where(live, xs_ref[0, rs, :], jnp.uint32(0)))
            hi1, lo1 = _unpack_bf16_pairs(jnp.where(live, xs_ref[1, rs, :], jnp.uint32(0)))
            xb[rs, :] = jnp.concatenate([hi0, hi1, lo0, lo1], axis=1)
            acc[rs, :] = jnp.zeros((half, D_MODEL), f32)
            act0[rs, :] = gate_up()

        @pl.when(live_half & (f > 0) & (f < nf) & (f % 2 == 1))
        def _(rs=rs, gate_up=gate_up, down=down):
            act1[rs, :] = gate_up()
            acc[rs, :] += down(act0)

        @pl.when(live_half & (f > 0) & (f < nf) & (f % 2 == 0))
        def _(rs=rs, gate_up=gate_up, down=down):
            act0[rs, :] = gate_up()
            acc[rs, :] += down(act1)

        @pl.when(live_half & (f == nf))
        def _(rs=rs, down=down):
            last = act0 if (D_FF_EXPERT // TF_MOE - 1) % 2 == 0 else act1
            packed = _pack_bf16_pairs(acc[rs, :] + down(last))
            ys_ref[0, rs, :] = packed[:, :SC_ROW]
            ys_ref[1, rs, :] = packed[:, SC_ROW:]

        @pl.when(jnp.logical_not(live_half) & (f == nf))
        def _(rs=rs):
            ys_ref[:, rs, :] = jnp.zeros((2, half, SC_ROW), u32)


def _moe_ffn(i_moe, blk_e, blk_n, xs, wg, wu, wd):
    npad = xs.shape[1]
    tm, tf = TM_MOE, TF_MOE
    nblk = npad // tm
    nf = D_FF_EXPERT // tf

    def nxt(i):
        return jnp.minimum(i + 1, nblk - 1)

    def x_idx(i, f, be, bn):
        return (0, jnp.where(f == nf, nxt(i), i), 0)

    def gu_idx(i, f, be, bn):
        ahead = (f == nf) & (bn[i] > 0) & (bn[nxt(i)] > 0)
        e = jnp.where(ahead, be[nxt(i)], be[i])
        t = jnp.where(ahead, 0, jnp.where(bn[i] > 0, jnp.minimum(f, nf - 1), nf - 1))
        return (i_moe, e, 0, t)

    def d_idx(i, f, be, bn):
        prev = jnp.maximum(i - 1, 0)
        keep = (f == 0) & (i > 0)
        e = jnp.where(keep, be[prev], be[i])
        t = jnp.where(keep | (bn[i] == 0), nf - 1, jnp.maximum(f - 1, 0))
        return (i_moe, e, t, 0)

    grid_spec = pltpu.PrefetchScalarGridSpec(
        num_scalar_prefetch=2,
        grid=(nblk, nf + 1),
        in_specs=[pl.BlockSpec((2, tm, SC_ROW), x_idx),
                  pl.BlockSpec((None, None, D_MODEL, tf), gu_idx),
                  pl.BlockSpec((None, None, D_MODEL, tf), gu_idx),
                  pl.BlockSpec((None, None, tf, D_MODEL), d_idx)],
        out_specs=pl.BlockSpec((2, tm, SC_ROW), lambda i, f, be, bn: (0, i, 0)),
        scratch_shapes=[pltpu.VMEM((tm, D_MODEL), bf16), pltpu.VMEM((tm, tf), bf16),
                        pltpu.VMEM((tm, tf), bf16), pltpu.VMEM((tm, D_MODEL), f32)],
    )
    return pl.pallas_call(
        _moe_ffn_body,
        grid_spec=grid_spec,
        out_shape=SDS((2, npad, SC_ROW), u32),
        compiler_params=_cparams("parallel", "arbitrary"),
        name="moe_ffn",
    )(blk_e, blk_n, xs, wg, wu, wd)


def _combine_body(x_ref, yg_ref, wcol_ref, g_ref, o_ref, *, final_norm):
    w1 = wcol_ref[:, 0:1]
    w2 = wcol_ref[:, 1:2]

    def rows(k):
        hi0, lo0 = _unpack_bf16_pairs(yg_ref[0, k])
        hi1, lo1 = _unpack_bf16_pairs(yg_ref[1, k])
        return jnp.concatenate([hi0, hi1, lo0, lo1], axis=1).astype(f32)

    xo = x_ref[...] + (w1 * rows(0) + w2 * rows(1))
    if final_norm:
        xo = _rms(xo, g_ref[...])
    o_ref[...] = xo


def _combine(x2, yg, wcol, g_final, final_norm):
    n = x2.shape[0]
    tm = TM_ROUTE
    row = lambda i: (i, 0)
    return pl.pallas_call(
        functools.partial(_combine_body, final_norm=final_norm),
        grid=(n // tm,),
        in_specs=[pl.BlockSpec((tm, D_MODEL), row),
                  pl.BlockSpec((2, 2, tm, SC_ROW), lambda i: (0, 0, i, 0)),
                  pl.BlockSpec((tm, LANE), row),
                  pl.BlockSpec((1, D_MODEL), lambda i: (0, 0))],
        out_specs=pl.BlockSpec((tm, D_MODEL), row),
        out_shape=SDS((n, D_MODEL), f32),
        compiler_params=_cparams("parallel"),
        name="moe_combine",
    )(x2, yg, wcol, g_final)


def _moe_layer(l, i_moe, mix_args, g_ffn, w_router, wg, wu, wd, g_final, final_norm):
    n = mix_args[0].shape[0]
    tm = TM_MOE
    npad = 2 * n + N_EXPERTS * tm
    wr = jnp.zeros((2 * SUBLANE, D_MODEL), f32).at[:N_EXPERTS].set(w_router[i_moe].astype(f32).T)
    wr_hi = wr.astype(bf16)
    wr_lo = (wr - wr_hi.astype(f32)).astype(bf16)
    x2, hp, meta, wcol, cnt = _mix_router(l, *mix_args, g_ffn, wr_hi, wr_lo)

    counts = cnt[:N_EXPERTS, 0].astype(i32)
    padded = ((counts + tm - 1) // tm) * tm
    ends = jnp.cumsum(padded)
    offs = ends - padded
    ids = jnp.arange(N_EXPERTS, dtype=i32)
    e12 = meta[0:2].astype(i32)
    r12 = meta[2:4].astype(i32)
    pos_t = r12 + jnp.sum(jnp.where(e12[..., None] == ids, offs, 0), axis=-1)
    blk_start = jnp.arange(npad // tm, dtype=i32) * tm
    blk_e = jnp.minimum(jnp.sum((blk_start[:, None] >= ends[None, :]).astype(i32), axis=1), N_EXPERTS - 1)
    own = blk_e[:, None] == ids
    blk_end = jnp.sum(jnp.where(own, offs + counts, 0), axis=1)
    blk_n = jnp.where(blk_start < ends[-1], jnp.clip(blk_end - blk_start, 0, tm), 0)
    last_e = jnp.max(jnp.where(blk_n > 0, blk_e, 0))
    blk_e = jnp.where(blk_n > 0, blk_e, last_e)

    xs = _sc_scatter2(hp.reshape(2 * n, SC_ROW),
                      jnp.concatenate([pos_t[0], pos_t[0] + npad]),
                      jnp.concatenate([pos_t[1], pos_t[1] + npad]),
                      2 * npad).reshape(2, npad, SC_ROW)
    ys = _moe_ffn(i_moe, blk_e, blk_n, xs, wg, wu, wd)
    gidx = jnp.concatenate([pos_t.reshape(-1), pos_t.reshape(-1) + npad])
    yg = _sc_gather(ys.reshape(2 * npad, SC_ROW), gidx).reshape(2, 2, n, SC_ROW)
    return _combine(x2, yg, wcol, g_final, final_norm)


def _rows(v):
    return v.astype(f32)[:, None, :]


def kernel(x, norm_mix, w_in, s5_lambda_re, s5_lambda_im, s5_log_dt, s5_b_re, s5_b_im, s5_c_re, s5_c_im, s5_d, s5_w_glu, s5_b_glu, s5_out_norm, gla_w_a2, gla_b_a2, gla_out_norm, w_out, norm_ffn, ffn_w_gate, ffn_w_up, ffn_w_down, moe_w_router, moe_w_gate, moe_w_up, moe_w_down, norm_final):
    bsz, seq, _ = x.shape
    n = bsz * seq
    depth = w_in.shape[0]
    x2 = x.reshape(n, D_MODEL)

    w_gp = jnp.zeros((depth, D_MODEL, LANE), f32).at[:, :, :GLA_GATE_RANK].set(w_in[:, :, 2048:2064])
    w_cat = jnp.concatenate([w_in[:, :, 0:512], w_in[:, :, 512:768] * (GLA_DK ** -0.5),
                             w_in[:, :, 1024:1536], w_in[:, :, 1536:2048], w_gp], axis=2).astype(bf16)
    w_kt = jnp.swapaxes(w_in[:, :, 768:1024], 1, 2).astype(bf16)
    s5_mats = jax.vmap(_s5_prep)(s5_lambda_re, s5_lambda_im, s5_log_dt, s5_b_re, s5_b_im, s5_c_re, s5_c_im)
    wa = jnp.zeros((depth, LANE, D_GLA_K), f32).at[:, :GLA_GATE_RANK].set(gla_w_a2).astype(bf16)
    g_mix, g_ffn, d_skip = _rows(norm_mix), _rows(norm_ffn), _rows(s5_d)
    ba, gn = _rows(gla_b_a2), _rows(gla_out_norm)
    mix_w = (s5_w_glu.astype(bf16), _rows(s5_b_glu), _rows(s5_out_norm), w_out.astype(bf16))
    ffn_w = (ffn_w_gate.astype(bf16), ffn_w_up.astype(bf16), ffn_w_down.astype(bf16))
    g_final = norm_final.astype(f32).reshape(1, D_MODEL)

    for l in range(depth):
        u, q, kt, v, r, gl = _inproj(l, x2, g_mix, w_cat, w_kt, bsz, seq)
        ys = _s5(l, u.reshape(bsz, seq, D_S5), *s5_mats, d_skip)
        yg = _gla(l, q.reshape(bsz, seq, D_GLA_K), kt, v.reshape(bsz, seq, D_GLA),
                  r.reshape(bsz, seq, D_GLA), gl.reshape(bsz, seq, LANE), wa, ba, gn)
        mix_args = (x2, ys.reshape(n, D_S5), yg.reshape(n, D_GLA)) + mix_w

        last = l == depth - 1
        if l % 2 == 0:
            x2 = _mix_ffn(l, l // 2, *mix_args, g_ffn, *ffn_w)
            if last:
                x2 = _final_norm(x2, g_final)
        else:
            x2 = _moe_layer(l, l // 2, mix_args, g_ffn, moe_w_router, moe_w_gate, moe_w_up,
                            moe_w_down, g_final, last)
    return x2.reshape(bsz, seq, D_MODEL)


def _final_norm_body(x_ref, g_ref, o_ref):
    o_ref[...] = _rms(x_ref[...], g_ref[...])


def _final_norm(x2, g):
    n = x2.shape[0]
    tm = TM_ROUTE
    return pl.pallas_call(
        _final_norm_body,
        grid=(n // tm,),
        in_specs=[pl.BlockSpec((tm, D_MODEL), lambda i: (i, 0)),
                  pl.BlockSpec((1, D_MODEL), lambda i: (0, 0))],
        out_specs=pl.BlockSpec((tm, D_MODEL), lambda i: (i, 0)),
        out_shape=SDS((n, D_MODEL), f32),
        compiler_params=_cparams("parallel"),
        name="final_norm",
    )(x2, g)
```

```python
import functools
import math

import jax
import jax.numpy as jnp
from jax import lax
from jax.experimental import pallas as pl
from jax.experimental.pallas import tpu as pltpu
from jax.experimental.pallas import tpu_sc as plsc

f32 = jnp.float32
bf16 = jnp.bfloat16
u32 = jnp.uint32
i32 = jnp.int32
SDS = jax.ShapeDtypeStruct

D_MODEL = 1024
D_S5 = 512
S5_GROUP = 16
S5_GROUPS = 32
S5_STATE = 64
N_STATE = S5_GROUPS * S5_STATE
D_GLA = 512
GLA_HEADS = 4
GLA_DV = 128
GLA_DK = 64
D_GLA_K = 256
GLA_GATE_RANK = 16
GLA_TAU = 16.0
GLA_CHUNK = 64
D_FF = 2816
N_EXPERTS = 8
D_FF_EXPERT = 3584
EPS = 1e-6

LANE = 128
SUBLANE = 8
VMEM_LIMIT = 56 * 1024 * 1024

TM_PROJ = 2048
TT_S5 = 256
S5_SLABS = 4
S5_PITCH_PAD = 8
TG_GLA = 1024
GLA_SUB = 256
TM_FFN = 1024
FFN_CHUNKS = ((0, 1536), (1536, 2816))
TM_MOE = 2048
MOE_SUB = 2
TF_MOE = 512
TM_ROUTE = 1024
SC_WINDOW = 128
SC_ROW = 256


def _cparams(*sem):
    return pltpu.CompilerParams(dimension_semantics=sem, vmem_limit_bytes=VMEM_LIMIT)


def _layer(arr, l, **kw):
    return pl.BlockSpec((None,) + arr.shape[1:], lambda *_: (l,) + (0,) * (arr.ndim - 1), **kw)


def _rms(x, g):
    ms = jnp.mean(x * x, axis=-1, keepdims=True)
    return x * lax.rsqrt(ms + EPS) * g


def _inproj_body(x_ref, g_ref, w_ref, wkt_ref, u_ref, q_ref, kt_ref, v_ref, r_ref, gl_ref):
    hn = _rms(x_ref[...], g_ref[...]).astype(bf16)

    def proj(lo, hi):
        return jnp.dot(hn, w_ref[:, lo:hi], preferred_element_type=f32).astype(bf16)

    u_ref[...] = proj(0, 512)
    q_ref[...] = proj(512, 768)
    v_ref[...] = proj(768, 1280)
    r_ref[...] = proj(1280, 1792)
    gl_ref[...] = proj(1792, 1920)
    kt_ref[...] = lax.dot_general(wkt_ref[...], hn, (((1,), (1,)), ((), ())),
                                  preferred_element_type=f32).astype(bf16)


def _inproj(l, x2, g, w_cat, w_kt, bsz, seq):
    n = x2.shape[0]
    tm = TM_PROJ
    per_b = seq // tm
    row = lambda i: (i, 0)
    return pl.pallas_call(
        _inproj_body,
        grid=(n // tm,),
        in_specs=[pl.BlockSpec((tm, D_MODEL), row),
                  _layer(g, l), _layer(w_cat, l), _layer(w_kt, l)],
        out_specs=[pl.BlockSpec((tm, D_S5), row),
                   pl.BlockSpec((tm, D_GLA_K), row),
                   pl.BlockSpec((None, D_GLA_K, tm), lambda i: (i // per_b, 0, i % per_b)),
                   pl.BlockSpec((tm, D_GLA), row),
                   pl.BlockSpec((tm, D_GLA), row),
                   pl.BlockSpec((tm, LANE), row)],
        out_shape=[SDS((n, D_S5), bf16), SDS((n, D_GLA_K), bf16),
                   SDS((bsz, D_GLA_K, seq), bf16), SDS((n, D_GLA), bf16),
                   SDS((n, D_GLA), bf16), SDS((n, LANE), bf16)],
        compiler_params=_cparams("parallel"),
        name="inproj",
    )(x2, g, w_cat, w_kt)


N_SLAB = N_STATE // LANE


def _s5_prep(lam_re, lam_im, log_dt, b_re, b_im, c_re, c_im):
    lr = jnp.minimum(lam_re.astype(f32), -1e-4)
    li = lam_im.astype(f32)
    dt = jnp.exp(log_dt.astype(f32))[:, None]
    mag = jnp.exp(lr * dt)
    ab_re = mag * jnp.cos(li * dt)
    ab_im = mag * jnp.sin(li * dt)
    nr = ab_re - 1.0
    ni = ab_im
    den = lr * lr + li * li
    f_re = (nr * lr + ni * li) / den
    f_im = (ni * lr - nr * li) / den
    br = b_re.astype(f32)
    bi = b_im.astype(f32)
    bb_re = f_re[..., None] * br - f_im[..., None] * bi
    bb_im = f_re[..., None] * bi + f_im[..., None] * br
    abb_re = ab_re[..., None] * bb_re - ab_im[..., None] * bb_im
    abb_im = ab_re[..., None] * bb_im + ab_im[..., None] * bb_re
    cr = c_re.astype(f32)
    ci = c_im.astype(f32)
    ca_re = cr * ab_re[:, None, :] - ci * ab_im[:, None, :]
    ca_im = cr * ab_im[:, None, :] + ci * ab_re[:, None, :]
    cb = jnp.einsum('gop,gpc->goc', cr, bb_re) - jnp.einsum('gop,gpc->goc', ci, bb_im)

    bb4 = jnp.stack([abb_re, bb_re, abb_im, bb_im]).astype(bf16)
    bt = jnp.transpose(bb4, (0, 1, 3, 2)).reshape(4, 8, 4, S5_GROUP, S5_STATE)
    gl = jnp.arange(8)[None, :, None]
    gs = jnp.arange(4)[None, None, :]
    nn = jnp.arange(8)[:, None, None]
    sel = (gl == 4 * (nn % 2) + gs).astype(bf16)
    t = jnp.einsum('ngs,qnscp->qngcsp', sel, bt, preferred_element_type=bf16)
    t = t.reshape(2, 2, 8, LANE, 4 * S5_STATE)
    bmat = jnp.transpose(t, (0, 2, 1, 3, 4)).reshape(16, 2 * LANE, 4 * S5_STATE)

    c4 = jnp.stack([cr, -ci, ca_re, -ca_im]).astype(bf16)
    ct = jnp.transpose(c4, (0, 1, 3, 2)).reshape(4, 2, 16, S5_STATE, S5_GROUP)
    eye = jnp.eye(16, dtype=bf16)
    cmat = jnp.einsum('gh,qjgpo->qjgpho', eye, ct, preferred_element_type=bf16)
    cmat = cmat.reshape(4, 2, 16 * S5_STATE, 16 * S5_GROUP)
    cbt = jnp.transpose(cb, (0, 2, 1)).reshape(2, 16, S5_GROUP, S5_GROUP)
    cbmat = jnp.einsum('gh,jgco->jgcho', jnp.eye(16, dtype=f32), cbt).reshape(2, 256, 256).astype(bf16)
    a2_re = (ab_re * ab_re - ab_im * ab_im).reshape(N_SLAB, 1, LANE)
    a2_im = (2.0 * ab_re * ab_im).reshape(N_SLAB, 1, LANE)
    return bmat, cmat, cbmat, a2_re, a2_im


def _s5_body(u_ref, bm_ref, cm_ref, cbm_ref, are_ref, aim_ref, d_ref, y_ref,
             wbuf, hbuf, obuf, hstate, zcarry):
    tt = u_ref.shape[1]
    tp = tt // 2
    rows = SUBLANE * tp
    pitch = tp + S5_PITCH_PAD
    nl = D_S5 // LANE

    @pl.when(pl.program_id(0) == 0)
    def _():
        hstate[...] = jnp.zeros_like(hstate)
        zcarry[...] = jnp.zeros_like(zcarry)

    w = pltpu.bitcast(u_ref[...].reshape(SUBLANE * tt, D_S5), u32)
    for b in range(SUBLANE):
        for k in range(nl):
            wbuf[k, b * pitch:b * pitch + tp, :] = w[b * tp:(b + 1) * tp, k * LANE:(k + 1) * LANE]
    w = jnp.concatenate(
        [jnp.concatenate([wbuf[k, pl.ds(m, SUBLANE, stride=pitch), :] for m in range(tp)], axis=0)
         for k in range(nl)], axis=1)
    ue_f = pltpu.bitcast(w << 16, f32)
    uo_f = pltpu.bitcast(w & jnp.uint32(0xFFFF0000), f32)
    ue = ue_f.astype(bf16)
    uo = uo_f.astype(bf16)

    for n in range(2 * 8):
        ks = LANE * ((n % 8) // 2)
        lhs = jnp.concatenate([ue[:, ks:ks + LANE], uo[:, ks:ks + LANE]], axis=1)
        res = jnp.dot(lhs, bm_ref[n], preferred_element_type=f32)
        hbuf[2 * n] = res[:, :LANE]
        hbuf[2 * n + 1] = res[:, LANE:]

    for c0 in range(0, N_SLAB, S5_SLABS):
        ar = [jnp.broadcast_to(are_ref[c0 + s], (SUBLANE, LANE)) for s in range(S5_SLABS)]
        ai = [jnp.broadcast_to(aim_ref[c0 + s], (SUBLANE, LANE)) for s in range(S5_SLABS)]

        def step(t, carry):
            out = []
            sl = pl.ds(pl.multiple_of(t * SUBLANE, SUBLANE), SUBLANE)
            for s in range(S5_SLABS):
                hr, hi = carry[2 * s], carry[2 * s + 1]
                nr = ar[s] * hr - ai[s] * hi + hbuf[c0 + s, sl, :]
                ni = ar[s] * hi + ai[s] * hr + hbuf[N_SLAB + c0 + s, sl, :]
                hbuf[c0 + s, sl, :] = nr
                hbuf[N_SLAB + c0 + s, sl, :] = ni
                out += [nr, ni]
            return tuple(out)

        init = []
        for s in range(S5_SLABS):
            init += [hstate[c0 + s], hstate[N_SLAB + c0 + s]]
        fin = lax.fori_loop(0, tp, step, tuple(init), unroll=4)
        for s in range(S5_SLABS):
            hstate[c0 + s] = fin[2 * s]
            hstate[N_SLAB + c0 + s] = fin[2 * s + 1]

    for j in range(2):
        h_re = jnp.concatenate([hbuf[8 * j + s] for s in range(8)], axis=1).astype(bf16)
        h_im = jnp.concatenate([hbuf[N_SLAB + 8 * j + s] for s in range(8)], axis=1).astype(bf16)
        cs = slice(256 * j, 256 * j + 256)
        dj = d_ref[:, cs]
        yo = jnp.dot(h_re, cm_ref[0, j], preferred_element_type=f32)
        yo = yo + jnp.dot(h_im, cm_ref[1, j], preferred_element_type=f32)
        yo = yo + dj * uo_f[:, cs]
        z = jnp.dot(h_re, cm_ref[2, j], preferred_element_type=f32)
        z = z + jnp.dot(h_im, cm_ref[3, j], preferred_element_type=f32)
        zs = jnp.concatenate([zcarry[:, cs], z[:rows - SUBLANE, :]], axis=0)
        zcarry[:, cs] = z[rows - SUBLANE:, :]
        ye = zs + jnp.dot(ue[:, cs], cbm_ref[j], preferred_element_type=f32) + dj * ue_f[:, cs]
        ge = pltpu.bitcast(jax.nn.gelu(ye).astype(bf16).astype(f32), u32)
        go = pltpu.bitcast(jax.nn.gelu(yo).astype(bf16).astype(f32), u32)
        packed = (ge >> 16) | go
        obuf[2 * j] = packed[:, :LANE]
        obuf[2 * j + 1] = packed[:, LANE:]

    for b in range(SUBLANE):
        yb = jnp.concatenate(
            [jnp.concatenate([obuf[k, pl.ds(8 * SUBLANE * i + b, SUBLANE, stride=SUBLANE), :]
                              for i in range(tp // SUBLANE)], axis=0) for k in range(nl)], axis=1)
        y_ref[b] = pltpu.bitcast(yb, bf16)


def _s5(l, u3, bmat, cmat, cbmat, a2_re, a2_im, d_skip):
    bsz, seq, _ = u3.shape
    assert bsz == SUBLANE
    tt = TT_S5
    blk = lambda i: (0, i, 0)
    return pl.pallas_call(
        _s5_body,
        grid=(seq // tt,),
        in_specs=[pl.BlockSpec((bsz, tt, D_S5), blk)]
        + [_layer(a, l) for a in (bmat, cmat, cbmat, a2_re, a2_im, d_skip)],
        out_specs=pl.BlockSpec((bsz, tt, D_S5), blk),
        out_shape=SDS((bsz, seq, D_S5), bf16),
        scratch_shapes=[pltpu.VMEM((D_S5 // LANE, bsz * (tt // 2 + S5_PITCH_PAD), LANE), u32),
                        pltpu.VMEM((2 * N_SLAB, bsz * (tt // 2), LANE), f32),
                        pltpu.VMEM((D_S5 // LANE, bsz * (tt // 2), LANE), u32),
                        pltpu.VMEM((2 * N_SLAB, SUBLANE, LANE), f32),
                        pltpu.VMEM((SUBLANE, D_S5), f32)],
        compiler_params=_cparams("arbitrary"),
        name="s5_scan",
    )(u3, bmat, cmat, cbmat, a2_re, a2_im, d_skip)


def _split2(x):
    hi = x.astype(bf16)
    lo = (x - hi.astype(f32)).astype(bf16)
    return hi, lo


def _log_sigmoid(x):
    return -(jnp.maximum(-x, 0.0) + jnp.log1p(jnp.exp(-jnp.abs(x))))


def _gla_body(q_ref, kt_ref, v_ref, r_ref, g_ref, wa_ref, ba_ref, gn_ref,
              o_ref, s_ref):
    c = GLA_CHUNK
    tg = q_ref.shape[0]

    @pl.when(pl.program_id(1) == 0)
    def _():
        s_ref[...] = jnp.zeros_like(s_ref)

    ts = GLA_SUB
    nc = ts // c
    nh = GLA_HEADS
    ri = lax.broadcasted_iota(i32, (ts, ts), 0)
    ci = lax.broadcasted_iota(i32, (ts, ts), 1)
    same = (ri // c) == (ci // c)
    tril = (same & (ri >= ci)).astype(bf16)
    triu = (same & (ri <= ci)).astype(bf16)
    blk = same.astype(bf16)
    head_of_lane = lax.broadcasted_iota(i32, (c, D_GLA_K), 1) // GLA_DK
    r_idx = lax.broadcasted_iota(i32, (nc * nh * c, ts), 0)
    c_idx = lax.broadcasted_iota(i32, (nc * nh * c, ts), 1)
    causal = (r_idx // (nh * c) == c_idx // c) & (r_idx % c >= c_idx % c)
    chunk_of_lane = lax.broadcasted_iota(i32, (D_GLA_K, ts), 1) // c

    nsub = tg // ts
    subs = [slice(st * ts, (st + 1) * ts) for st in range(nsub)]

    def decay_stage(rs):
        g = g_ref[rs, :]
        la = _log_sigmoid(jnp.dot(g, wa_ref[...], preferred_element_type=f32) + ba_ref[...]) / GLA_TAU
        lat = la.T
        la_hi, la_lo = _split2(la)
        cum = (jnp.dot(tril, la_hi, preferred_element_type=f32)
               + jnp.dot(tril, la_lo, preferred_element_type=f32))
        lat_hi, lat_lo = _split2(lat)
        cumt = (jnp.dot(lat_hi, triu, preferred_element_type=f32)
                + jnp.dot(lat_lo, triu, preferred_element_type=f32))
        clt = (jnp.dot(lat_hi, blk, preferred_element_type=f32)
               + jnp.dot(lat_lo, blk, preferred_element_type=f32))
        qt = q_ref[rs, :].astype(f32) * jnp.exp(cum)
        ktt = kt_ref[:, rs].astype(f32)
        k_t = (ktt * jnp.exp(-cumt)).astype(bf16)
        k_end = ktt * jnp.exp(clt - cumt)
        q_stack = jnp.concatenate(
            [jnp.where(head_of_lane == h, qt[cc * c:(cc + 1) * c, :], 0.0)
             for cc in range(nc) for h in range(nh)], axis=0).astype(bf16)
        k_stack = jnp.concatenate(
            [jnp.where(chunk_of_lane == cc, k_end, 0.0) for cc in range(nc)], axis=0).astype(bf16)
        dec = [jnp.exp(clt[:, cc * c:cc * c + 1]) for cc in range(nc)]
        return q_stack, k_t, k_stack, dec

    def matmul_stage(rs, staged, state):
        q_stack, k_t, k_stack, dec = staged
        v = v_ref[rs, :]
        scores = jnp.dot(q_stack, k_t, preferred_element_type=f32)
        scores = jnp.where(causal, scores, 0.0).astype(bf16)
        o_full = jnp.dot(scores, v, preferred_element_type=f32)
        upd_full = jnp.dot(k_stack, v, preferred_element_type=f32)
        o_inter = []
        for cc in range(nc):
            o_inter.append(jnp.dot(q_stack[cc * nh * c:(cc + 1) * nh * c, :], state.astype(bf16),
                                   preferred_element_type=f32))
            upd = jnp.concatenate(
                [upd_full[cc * D_GLA_K + h * GLA_DK:cc * D_GLA_K + (h + 1) * GLA_DK,
                          h * GLA_DV:(h + 1) * GLA_DV] for h in range(nh)], axis=0)
            state = dec[cc] * state + upd
        return o_full, o_inter, state

    def output_stage(rs, o_full, o_inter):
        rows = []
        for cc in range(nc):
            outs = []
            for h in range(nh):
                r0 = (cc * nh + h) * c
                o = o_full[r0:r0 + c, h * GLA_DV:(h + 1) * GLA_DV] + o_inter[cc][h * c:(h + 1) * c, :]
                o = o * lax.rsqrt(jnp.mean(o * o, axis=-1, keepdims=True) + EPS)
                outs.append(o)
            rows.append(jnp.concatenate(outs, axis=1))
        o_cat = jnp.concatenate(rows, axis=0)
        o_ref[rs, :] = (o_cat * gn_ref[...] * jax.nn.silu(r_ref[rs, :].astype(f32))).astype(bf16)

    state = s_ref[...]
    staged = decay_stage(subs[0])
    for st in range(nsub):
        nxt = decay_stage(subs[st + 1]) if st + 1 < nsub else None
        o_full, o_inter, state = matmul_stage(subs[st], staged, state)
        output_stage(subs[st], o_full, o_inter)
        staged = nxt
    s_ref[...] = state


def _gla(l, q3, kt3, v3, r3, g3, wa, ba, gn):
    bsz, seq, _ = q3.shape
    tg = TG_GLA
    tok = lambda b, i: (b, i, 0)
    return pl.pallas_call(
        _gla_body,
        grid=(bsz, seq // tg),
        in_specs=[pl.BlockSpec((None, tg, D_GLA_K), tok),
                  pl.BlockSpec((None, D_GLA_K, tg), lambda b, i: (b, 0, i)),
                  pl.BlockSpec((None, tg, D_GLA), tok),
                  pl.BlockSpec((None, tg, D_GLA), tok),
                  pl.BlockSpec((None, tg, LANE), tok),
                  _layer(wa, l), _layer(ba, l), _layer(gn, l)],
        out_specs=pl.BlockSpec((None, tg, D_GLA), tok),
        out_shape=SDS((bsz, seq, D_GLA), bf16),
        scratch_shapes=[pltpu.VMEM((D_GLA_K, GLA_DV), f32)],
        compiler_params=_cparams("parallel", "arbitrary"),
        name="gla",
    )(q3, kt3, v3, r3, g3, wa, ba, gn)


def _mix_out(x_ref, ys_ref, yg_ref, wgl_ref, bg_ref, gs_ref, wo_ref):
    y = ys_ref[...]
    z = jnp.dot(y, wgl_ref[...], preferred_element_type=f32) + bg_ref[...]
    yf = y.astype(f32) * jax.nn.sigmoid(z)
    ys = _rms(yf, gs_ref[...]).astype(bf16)
    acc = jnp.dot(ys, wo_ref[0:D_S5, :], preferred_element_type=f32)
    acc = acc + jnp.dot(yg_ref[...], wo_ref[D_S5:, :], preferred_element_type=f32)
    return x_ref[...] + acc


def _mix_specs(l, tm, w_glu, b_glu, g_s5, w_out):
    row = lambda i: (i, 0)
    once = dict(pipeline_mode=pl.Buffered(1))
    return [pl.BlockSpec((tm, D_MODEL), row),
            pl.BlockSpec((tm, D_S5), row),
            pl.BlockSpec((tm, D_GLA), row),
            _layer(w_glu, l, **once), _layer(b_glu, l), _layer(g_s5, l), _layer(w_out, l, **once)]


def _mix_ffn_body(x_ref, ys_ref, yg_ref, wgl_ref, bg_ref, gs_ref, wo_ref,
                  g_ref, wg_ref, wu_ref, wd_ref, o_ref):
    x = _mix_out(x_ref, ys_ref, yg_ref, wgl_ref, bg_ref, gs_ref, wo_ref)
    hn = _rms(x, g_ref[...]).astype(bf16)
    acc = x
    for lo, hi in FFN_CHUNKS:
        gate = jnp.dot(hn, wg_ref[:, lo:hi], preferred_element_type=f32)
        up = jnp.dot(hn, wu_ref[:, lo:hi], preferred_element_type=f32)
        act = (jax.nn.silu(gate) * up).astype(bf16)
        acc = acc + jnp.dot(act, wd_ref[lo:hi, :], preferred_element_type=f32)
    o_ref[...] = acc


def _mix_ffn(l, i_ffn, x2, ys, yg, w_glu, b_glu, g_s5, w_out, g, wg, wu, wd):
    n = x2.shape[0]
    tm = TM_FFN
    row = lambda i: (i, 0)
    once = dict(pipeline_mode=pl.Buffered(1))
    return pl.pallas_call(
        _mix_ffn_body,
        grid=(n // tm,),
        in_specs=_mix_specs(l, tm, w_glu, b_glu, g_s5, w_out) + [
            _layer(g, l), _layer(wg, i_ffn, **once), _layer(wu, i_ffn, **once),
            _layer(wd, i_ffn, **once)],
        out_specs=pl.BlockSpec((tm, D_MODEL), row),
        out_shape=SDS((n, D_MODEL), f32),
        compiler_params=_cparams("parallel"),
        name="mix_ffn",
    )(x2, ys, yg, w_glu, b_glu, g_s5, w_out, g, wg, wu, wd)


def _pack_bf16_pairs(a):
    bits = pltpu.bitcast(a.astype(bf16).astype(f32), u32)
    half = a.shape[1] // 2
    return bits[:, :half] | (bits[:, half:] >> 16)


def _unpack_bf16_pairs(p):
    hi = pltpu.bitcast(p & jnp.uint32(0xFFFF0000), f32).astype(bf16)
    lo = pltpu.bitcast(p << 16, f32).astype(bf16)
    return hi, lo


def _mix_router_body(x_ref, ys_ref, yg_ref, wgl_ref, bg_ref, gs_ref, wo_ref,
                     g_ref, wh_ref, wl_ref, xo_ref, hp_ref, meta_ref, wcol_ref, cnt_ref, carry):
    tm = x_ref.shape[0]

    @pl.when(pl.program_id(0) == 0)
    def _():
        carry[...] = jnp.zeros_like(carry)

    x = _mix_out(x_ref, ys_ref, yg_ref, wgl_ref, bg_ref, gs_ref, wo_ref)
    xo_ref[...] = x
    hn = _rms(x, g_ref[...])
    packed = _pack_bf16_pairs(hn)
    hp_ref[0] = packed[:, :SC_ROW]
    hp_ref[1] = packed[:, SC_ROW:]

    h_hi, h_lo = _split2(hn)
    nt = (((1,), (1,)), ((), ()))
    logits = (lax.dot_general(wh_ref[...], h_hi, nt, preferred_element_type=f32)
              + lax.dot_general(wh_ref[...], h_lo, nt, preferred_element_type=f32)
              + lax.dot_general(wl_ref[...], h_hi, nt, preferred_element_type=f32))
    er = logits.shape[0]
    row = lax.broadcasted_iota(i32, (er, tm), 0)
    neg = jnp.float32(-jnp.inf)
    logits = jnp.where(row < N_EXPERTS, logits, neg)
    m1 = jnp.max(logits, axis=0, keepdims=True)
    i1 = jnp.min(jnp.where(logits == m1, row, er), axis=0, keepdims=True)
    l2 = jnp.where(row == i1, neg, logits)
    m2 = jnp.max(l2, axis=0, keepdims=True)
    i2 = jnp.min(jnp.where(l2 == m2, row, er), axis=0, keepdims=True)
    e21 = jnp.exp(m2 - m1)
    w1 = 1.0 / (1.0 + e21)
    w2 = e21 / (1.0 + e21)

    sel1 = row == i1
    sel2 = row == i2
    sel = (sel1 | sel2).astype(f32)
    ri = lax.broadcasted_iota(i32, (tm, tm), 0)
    ci = lax.broadcasted_iota(i32, (tm, tm), 1)
    triu = (ri <= ci).astype(bf16)
    incl = jnp.dot(sel.astype(bf16), triu, preferred_element_type=f32)
    rank = incl - sel + carry[:, 0:1]
    r1 = jnp.sum(jnp.where(sel1, rank, 0.0), axis=0, keepdims=True)
    r2 = jnp.sum(jnp.where(sel2, rank, 0.0), axis=0, keepdims=True)
    new_cnt = carry[:, 0:1] + incl[:, tm - 1:tm]
    carry[...] = jnp.broadcast_to(new_cnt, carry.shape)
    cnt_ref[...] = jnp.broadcast_to(new_cnt, cnt_ref.shape)

    srow = lax.broadcasted_iota(i32, (SUBLANE, tm), 0)
    meta = jnp.where(srow == 0, i1.astype(f32), 0.0)
    meta = jnp.where(srow == 1, i2.astype(f32), meta)
    meta = jnp.where(srow == 2, r1, meta)
    meta = jnp.where(srow == 3, r2, meta)
    meta_ref[...] = meta
    prow = lax.broadcasted_iota(i32, (LANE, tm), 0)
    wpad = jnp.where(prow == 0, w1, jnp.where(prow == 1, w2, 0.0))
    wcol_ref[...] = wpad.T


def _mix_router(l, x2, ys, yg, w_glu, b_glu, g_s5, w_out, g, w_hi, w_lo):
    n = x2.shape[0]
    tm = TM_ROUTE
    row = lambda i: (i, 0)
    const = lambda i: (0, 0)
    return pl.pallas_call(
        _mix_router_body,
        grid=(n // tm,),
        in_specs=_mix_specs(l, tm, w_glu, b_glu, g_s5, w_out) + [
            _layer(g, l),
            pl.BlockSpec(w_hi.shape, const),
            pl.BlockSpec(w_lo.shape, const)],
        out_specs=[pl.BlockSpec((tm, D_MODEL), row),
                   pl.BlockSpec((2, tm, SC_ROW), lambda i: (0, i, 0)),
                   pl.BlockSpec((SUBLANE, tm), lambda i: (0, i)),
                   pl.BlockSpec((tm, LANE), row),
                   pl.BlockSpec(w_hi.shape[:1] + (LANE,), const)],
        out_shape=[SDS((n, D_MODEL), f32), SDS((2, n, SC_ROW), u32), SDS((SUBLANE, n), f32),
                   SDS((n, LANE), f32), SDS(w_hi.shape[:1] + (LANE,), f32)],
        scratch_shapes=[pltpu.VMEM(w_hi.shape[:1] + (LANE,), f32)],
        compiler_params=_cparams("arbitrary"),
        name="mix_router",
    )(x2, ys, yg, w_glu, b_glu, g_s5, w_out, g, w_hi, w_lo)


def _sc_gather(table, idx):
    ni = idx.shape[0]
    mesh = plsc.VectorSubcoreMesh(core_axis_name="core", subcore_axis_name="subcore")
    idx2 = idx.reshape(1, ni)

    @pl.kernel(out_type=SDS((ni, SC_ROW), table.dtype), mesh=mesh)
    def kern(t_hbm, i_hbm, o_hbm):
        def body(i_vmem, o_vmem):
            pltpu.sync_copy(t_hbm.at[i_vmem.at[0]], o_vmem)

        pltpu.emit_pipeline(
            body, grid=(ni // SC_WINDOW,),
            in_specs=[pl.BlockSpec((1, SC_WINDOW), index_map=lambda i: (0, i))],
            out_specs=[pl.BlockSpec((SC_WINDOW, SC_ROW), index_map=lambda i: (i, 0))],
            core_axis_name=("core", "subcore"),
            dimension_semantics=(pltpu.PARALLEL,),
        )(i_hbm, o_hbm)

    return kern(table, idx2)


def _sc_scatter2(x, idx0, idx1, nrows):
    ni = x.shape[0]
    mesh = plsc.VectorSubcoreMesh(core_axis_name="core", subcore_axis_name="subcore")

    @pl.kernel(out_type=SDS((nrows, SC_ROW), x.dtype), mesh=mesh)
    def kern(x_hbm, i0_hbm, i1_hbm, o_hbm):
        def body(x_vmem, i0_vmem, i1_vmem):
            pltpu.sync_copy(x_vmem, o_hbm.at[i0_vmem.at[0]])
            pltpu.sync_copy(x_vmem, o_hbm.at[i1_vmem.at[0]])

        pltpu.emit_pipeline(
            body, grid=(ni // SC_WINDOW,),
            in_specs=[pl.BlockSpec((SC_WINDOW, SC_ROW), index_map=lambda i: (i, 0)),
                      pl.BlockSpec((1, SC_WINDOW), index_map=lambda i: (0, i)),
                      pl.BlockSpec((1, SC_WINDOW), index_map=lambda i: (0, i))],
            out_specs=[],
            core_axis_name=("core", "subcore"),
            dimension_semantics=(pltpu.PARALLEL,),
        )(x_hbm, i0_hbm, i1_hbm)

    return kern(x, idx0.reshape(1, ni), idx1.reshape(1, ni))


def _moe_ffn_body(be_ref, bn_ref, xs_ref, wg_ref, wu_ref, wd_ref, ys_ref, xb, act0, act1, acc):
    i = pl.program_id(0)
    f = pl.program_id(1)
    nf = pl.num_programs(1) - 1
    nvalid = bn_ref[i]
    half = xs_ref.shape[1] // MOE_SUB

    for h in range(MOE_SUB):
        rs = pl.ds(h * half, half)
        live_half = nvalid > h * half

        def gate_up(rs=rs):
            x = xb[rs, :]
            gate = jnp.dot(x, wg_ref[...].astype(bf16), preferred_element_type=f32)
            up = jnp.dot(x, wu_ref[...].astype(bf16), preferred_element_type=f32)
            return (jax.nn.silu(gate) * up).astype(bf16)

        def down(act_ref, rs=rs):
            return jnp.dot(act_ref[rs, :], wd_ref[...].astype(bf16), preferred_element_type=f32)

        @pl.when(live_half & (f == 0))
        def _(rs=rs, h=h, gate_up=gate_up):
            live = lax.broadcasted_iota(i32, (half, SC_ROW), 0) + h * half < nvalid
            hi0, lo0 = _unpack_bf16_pairs(jnp.where(live, xs_ref[0, rs, :], jnp.uint32(0)))
            hi1, lo1 = _unpack_bf16_pairs(jnp.where(live, xs_ref[1, rs, :], jnp.uint32(0)))
            xb[rs, :] = jnp.concatenate([hi0, hi1, lo0, lo1], axis=1)
            acc[rs, :] = jnp.zeros((half, D_MODEL), f32)
            act0[rs, :] = gate_up()

        @pl.when(live_half & (f > 0) & (f < nf) & (f % 2 == 1))
        def _(rs=rs, gate_up=gate_up, down=down):
            act1[rs, :] = gate_up()
            acc[rs, :] += down(act0)

        @pl.when(live_half & (f > 0) & (f < nf) & (f % 2 == 0))
        def _(rs=rs, gate_up=gate_up, down=down):
            act0[rs, :] = gate_up()
            acc[rs, :] += down(act1)

        @pl.when(live_half & (f == nf))
        def _(rs=rs, down=down):
            last = act0 if (D_FF_EXPERT // TF_MOE - 1) % 2 == 0 else act1
            packed = _pack_bf16_pairs(acc[rs, :] + down(last))
            ys_ref[0, rs, :] = packed[:, :SC_ROW]
            ys_ref[1, rs, :] = packed[:, SC_ROW:]

        @pl.when(jnp.logical_not(live_half) & (f == nf))
        def _(rs=rs):
            ys_ref[:, rs, :] = jnp.zeros((2, half, SC_ROW), u32)


def _moe_ffn(i_moe, blk_e, blk_n, xs, wg, wu, wd):
    npad = xs.shape[1]
    tm, tf = TM_MOE, TF_MOE
    nblk = npad // tm
    nf = D_FF_EXPERT // tf

    def nxt(i):
        return jnp.minimum(i + 1, nblk - 1)

    def x_idx(i, f, be, bn):
        return (0, jnp.where(f == nf, nxt(i), i), 0)

    def gu_idx(i, f, be, bn):
        ahead = (f == nf) & (bn[i] > 0) & (bn[nxt(i)] > 0)
        e = jnp.where(ahead, be[nxt(i)], be[i])
        t = jnp.where(ahead, 0, jnp.where(bn[i] > 0, jnp.minimum(f, nf - 1), nf - 1))
        return (i_moe, e, 0, t)

    def d_idx(i, f, be, bn):
        prev = jnp.maximum(i - 1, 0)
        keep = (f == 0) & (i > 0)
        e = jnp.where(keep, be[prev], be[i])
        t = jnp.where(keep | (bn[i] == 0), nf - 1, jnp.maximum(f - 1, 0))
        return (i_moe, e, t, 0)

    grid_spec = pltpu.PrefetchScalarGridSpec(
        num_scalar_prefetch=2,
        grid=(nblk, nf + 1),
        in_specs=[pl.BlockSpec((2, tm, SC_ROW), x_idx),
                  pl.BlockSpec((None, None, D_MODEL, tf), gu_idx),
                  pl.BlockSpec((None, None, D_MODEL, tf), gu_idx),
                  pl.BlockSpec((None, None, tf, D_MODEL), d_idx)],
        out_specs=pl.BlockSpec((2, tm, SC_ROW), lambda i, f, be, bn: (0, i, 0)),
        scratch_shapes=[pltpu.VMEM((tm, D_MODEL), bf16), pltpu.VMEM((tm, tf), bf16),
                        pltpu.VMEM((tm, tf), bf16), pltpu.VMEM((tm, D_MODEL), f32)],
    )
    return pl.pallas_call(
        _moe_ffn_body,
        grid_spec=grid_spec,
        out_shape=SDS((2, npad, SC_ROW), u32),
        compiler_params=_cparams("parallel", "arbitrary"),
        name="moe_ffn",
    )(blk_e, blk_n, xs, wg, wu, wd)


def _combine_body(x_ref, yg_ref, wcol_ref, g_ref, o_ref, *, final_norm):
    w1 = wcol_ref[:, 0:1]
    w2 = wcol_ref[:, 1:2]

    def rows(k):
        hi0, lo0 = _unpack_bf16_pairs(yg_ref[0, k])
        hi1, lo1 = _unpack_bf16_pairs(yg_ref[1, k])
        return jnp.concatenate([hi0, hi1, lo0, lo1], axis=1).astype(f32)

    xo = x_ref[...] + (w1 * rows(0) + w2 * rows(1))
    if final_norm:
        xo = _rms(xo, g_ref[...])
    o_ref[...] = xo


def _combine(x2, yg, wcol, g_final, final_norm):
    n = x2.shape[0]
    tm = TM_ROUTE
    row = lambda i: (i, 0)
    return pl.pallas_call(
        functools.partial(_combine_body, final_norm=final_norm),
        grid=(n // tm,),
        in_specs=[pl.BlockSpec((tm, D_MODEL), row),
                  pl.BlockSpec((2, 2, tm, SC_ROW), lambda i: (0, 0, i, 0)),
                  pl.BlockSpec((tm, LANE), row),
                  pl.BlockSpec((1, D_MODEL), lambda i: (0, 0))],
        out_specs=pl.BlockSpec((tm, D_MODEL), row),
        out_shape=SDS((n, D_MODEL), f32),
        compiler_params=_cparams("parallel"),
        name="moe_combine",
    )(x2, yg, wcol, g_final)


def _moe_layer(l, i_moe, mix_args, g_ffn, w_router, wg, wu, wd, g_final, final_norm):
    n = mix_args[0].shape[0]
    tm = TM_MOE
    npad = 2 * n + N_EXPERTS * tm
    wr = jnp.zeros((2 * SUBLANE, D_MODEL), f32).at[:N_EXPERTS].set(w_router[i_moe].astype(f32).T)
    wr_hi = wr.astype(bf16)
    wr_lo = (wr - wr_hi.astype(f32)).astype(bf16)
    x2, hp, meta, wcol, cnt = _mix_router(l, *mix_args, g_ffn, wr_hi, wr_lo)

    counts = cnt[:N_EXPERTS, 0].astype(i32)
    padded = ((counts + tm - 1) // tm) * tm
    ends = jnp.cumsum(padded)
    offs = ends - padded
    ids = jnp.arange(N_EXPERTS, dtype=i32)
    e12 = meta[0:2].astype(i32)
    r12 = meta[2:4].astype(i32)
    pos_t = r12 + jnp.sum(jnp.where(e12[..., None] == ids, offs, 0), axis=-1)
    blk_start = jnp.arange(npad // tm, dtype=i32) * tm
    blk_e = jnp.minimum(jnp.sum((blk_start[:, None] >= ends[None, :]).astype(i32), axis=1), N_EXPERTS - 1)
    own = blk_e[:, None] == ids
    blk_end = jnp.sum(jnp.where(own, offs + counts, 0), axis=1)
    blk_n = jnp.where(blk_start < ends[-1], jnp.clip(blk_end - blk_start, 0, tm), 0)
    last_e = jnp.max(jnp.where(blk_n > 0, blk_e, 0))
    blk_e = jnp.where(blk_n > 0, blk_e, last_e)

    xs = _sc_scatter2(hp.reshape(2 * n, SC_ROW),
                      jnp.concatenate([pos_t[0], pos_t[0] + npad]),
                      jnp.concatenate([pos_t[1], pos_t[1] + npad]),
                      2 * npad).reshape(2, npad, SC_ROW)
    ys = _moe_ffn(i_moe, blk_e, blk_n, xs, wg, wu, wd)
    gidx = jnp.concatenate([pos_t.reshape(-1), pos_t.reshape(-1) + npad])
    yg = _sc_gather(ys.reshape(2 * npad, SC_ROW), gidx).reshape(2, 2, n, SC_ROW)
    return _combine(x2, yg, wcol, g_final, final_norm)


def _rows(v):
    return v.astype(f32)[:, None, :]


def kernel(x, norm_mix, w_in, s5_lambda_re, s5_lambda_im, s5_log_dt, s5_b_re, s5_b_im, s5_c_re, s5_c_im, s5_d, s5_w_glu, s5_b_glu, s5_out_norm, gla_w_a2, gla_b_a2, gla_out_norm, w_out, norm_ffn, ffn_w_gate, ffn_w_up, ffn_w_down, moe_w_router, moe_w_gate, moe_w_up, moe_w_down, norm_final):
    bsz, seq, _ = x.shape
    n = bsz * seq
    depth = w_in.shape[0]
    x2 = x.reshape(n, D_MODEL)

    w_gp = jnp.zeros((depth, D_MODEL, LANE), f32).at[:, :, :GLA_GATE_RANK].set(w_in[:, :, 2048:2064])
    w_cat = jnp.concatenate([w_in[:, :, 0:512], w_in[:, :, 512:768] * (GLA_DK ** -0.5),
                             w_in[:, :, 1024:1536], w_in[:, :, 1536:2048], w_gp], axis=2).astype(bf16)
    w_kt = jnp.swapaxes(w_in[:, :, 768:1024], 1, 2).astype(bf16)
    s5_mats = jax.vmap(_s5_prep)(s5_lambda_re, s5_lambda_im, s5_log_dt, s5_b_re, s5_b_im, s5_c_re, s5_c_im)
    wa = jnp.zeros((depth, LANE, D_GLA_K), f32).at[:, :GLA_GATE_RANK].set(gla_w_a2).astype(bf16)
    g_mix, g_ffn, d_skip = _rows(norm_mix), _rows(norm_ffn), _rows(s5_d)
    ba, gn = _rows(gla_b_a2), _rows(gla_out_norm)
    mix_w = (s5_w_glu.astype(bf16), _rows(s5_b_glu), _rows(s5_out_norm), w_out.astype(bf16))
    ffn_w = (ffn_w_gate.astype(bf16), ffn_w_up.astype(bf16), ffn_w_down.astype(bf16))
    g_final = norm_final.astype(f32).reshape(1, D_MODEL)

    for l in range(depth):
        u, q, kt, v, r, gl = _inproj(l, x2, g_mix, w_cat, w_kt, bsz, seq)
        ys = _s5(l, u.reshape(bsz, seq, D_S5), *s5_mats, d_skip)
        yg = _gla(l, q.reshape(bsz, seq, D_GLA_K), kt, v.reshape(bsz, seq, D_GLA),
                  r.reshape(bsz, seq, D_GLA), gl.reshape(bsz, seq, LANE), wa, ba, gn)
        mix_args = (x2, ys.reshape(n, D_S5), yg.reshape(n, D_GLA)) + mix_w

        last = l == depth - 1
        if l % 2 == 0:
            x2 = _mix_ffn(l, l // 2, *mix_args, g_ffn, *ffn_w)
            if last:
                x2 = _final_norm(x2, g_final)
        else:
            x2 = _moe_layer(l, l // 2, mix_args, g_ffn, moe_w_router, moe_w_gate, moe_w_up,
                            moe_w_down, g_final, last)
    return x2.reshape(bsz, seq, D_MODEL)


def _final_norm_body(x_ref, g_ref, o_ref):
    o_ref[...] = _rms(x_ref[...], g_ref[...])


def _final_norm(x2, g):
    n = x2.shape[0]
    tm = TM_ROUTE
    return pl.pallas_call(
        _final_norm_body,
        grid=(n // tm,),
        in_specs=[pl.BlockSpec((tm, D_MODEL), lambda i: (i, 0)),
                  pl.BlockSpec((1, D_MODEL), lambda i: (0, 0))],
        out_specs=pl.BlockSpec((tm, D_MODEL), lambda i: (i, 0)),
        out_shape=SDS((n, D_MODEL), f32),
        compiler_params=_cparams("parallel"),
        name="final_norm",
    )(x2, g)
```

```python
import functools

import jax
import jax.numpy as jnp
from jax import lax
from jax.experimental import pallas as pl
from jax.experimental.pallas import tpu as pltpu
from jax.experimental.pallas import tpu_sc as plsc

f32 = jnp.float32
bf16 = jnp.bfloat16
u32 = jnp.uint32
i32 = jnp.int32
SDS = jax.ShapeDtypeStruct

D_MODEL = 1024
D_S5 = 512
S5_GROUP = 16
S5_GROUPS = 32
S5_STATE = 64
N_STATE = S5_GROUPS * S5_STATE
D_GLA = 512
GLA_HEADS = 4
GLA_DV = 128
GLA_DK = 64
D_GLA_K = 256
GLA_GATE_RANK = 16
GLA_TAU = 16.0
GLA_CHUNK = 64
D_FF = 2816
N_EXPERTS = 8
D_FF_EXPERT = 3584
EPS = 1e-6

LANE = 128
SUBLANE = 8
VMEM_LIMIT = 56 * 1024 * 1024

TM_PROJ = 2048
TT_S5 = 256
S5_SLABS = 4
S5_PITCH_PAD = 8
TG_GLA = 1024
GLA_SUB = 256
TM_FFN = 1024
FFN_CHUNKS = ((0, 1536), (1536, 2816))
TM_MOE = 2048
MOE_SUB = 2
TF_MOE = 512
TM_ROUTE = 1024
SC_WINDOW = 128
SC_ROW = 256


def _cparams(*sem):
    return pltpu.CompilerParams(dimension_semantics=sem, vmem_limit_bytes=VMEM_LIMIT)


def _layer(arr, l, **kw):
    return pl.BlockSpec((None,) + arr.shape[1:], lambda *_: (l,) + (0,) * (arr.ndim - 1), **kw)


def _rms(x, g):
    ms = jnp.mean(x * x, axis=-1, keepdims=True)
    return x * lax.rsqrt(ms + EPS) * g


def _inproj_body(x_ref, g_ref, w_ref, wkt_ref, u_ref, q_ref, kt_ref, v_ref, r_ref, gl_ref):
    hn = _rms(x_ref[...], g_ref[...]).astype(bf16)

    def proj(lo, hi):
        return jnp.dot(hn, w_ref[:, lo:hi], preferred_element_type=f32).astype(bf16)

    u_ref[...] = proj(0, 512)
    q_ref[...] = proj(512, 768)
    v_ref[...] = proj(768, 1280)
    r_ref[...] = proj(1280, 1792)
    gl_ref[...] = proj(1792, 1920)
    kt_ref[...] = lax.dot_general(wkt_ref[...], hn, (((1,), (1,)), ((), ())),
                                  preferred_element_type=f32).astype(bf16)


def _inproj(l, x2, g, w_cat, w_kt, bsz, seq):
    n = x2.shape[0]
    tm = TM_PROJ
    per_b = seq // tm
    row = lambda i: (i, 0)
    return pl.pallas_call(
        _inproj_body,
        grid=(n // tm,),
        in_specs=[pl.BlockSpec((tm, D_MODEL), row),
                  _layer(g, l), _layer(w_cat, l), _layer(w_kt, l)],
        out_specs=[pl.BlockSpec((tm, D_S5), row),
                   pl.BlockSpec((tm, D_GLA_K), row),
                   pl.BlockSpec((None, D_GLA_K, tm), lambda i: (i // per_b, 0, i % per_b)),
                   pl.BlockSpec((tm, D_GLA), row),
                   pl.BlockSpec((tm, D_GLA), row),
                   pl.BlockSpec((tm, LANE), row)],
        out_shape=[SDS((n, D_S5), bf16), SDS((n, D_GLA_K), bf16),
                   SDS((bsz, D_GLA_K, seq), bf16), SDS((n, D_GLA), bf16),
                   SDS((n, D_GLA), bf16), SDS((n, LANE), bf16)],
        compiler_params=_cparams("parallel"),
        name="inproj",
    )(x2, g, w_cat, w_kt)


N_SLAB = N_STATE // LANE


def _s5_prep(lam_re, lam_im, log_dt, b_re, b_im, c_re, c_im):
    lr = jnp.minimum(lam_re.astype(f32), -1e-4)
    li = lam_im.astype(f32)
    dt = jnp.exp(log_dt.astype(f32))[:, None]
    mag = jnp.exp(lr * dt)
    ab_re = mag * jnp.cos(li * dt)
    ab_im = mag * jnp.sin(li * dt)
    nr = ab_re - 1.0
    ni = ab_im
    den = lr * lr + li * li
    f_re = (nr * lr + ni * li) / den
    f_im = (ni * lr - nr * li) / den
    br = b_re.astype(f32)
    bi = b_im.astype(f32)
    bb_re = f_re[..., None] * br - f_im[..., None] * bi
    bb_im = f_re[..., None] * bi + f_im[..., None] * br
    abb_re = ab_re[..., None] * bb_re - ab_im[..., None] * bb_im
    abb_im = ab_re[..., None] * bb_im + ab_im[..., None] * bb_re
    cr = c_re.astype(f32)
    ci = c_im.astype(f32)
    ca_re = cr * ab_re[:, None, :] - ci * ab_im[:, None, :]
    ca_im = cr * ab_im[:, None, :] + ci * ab_re[:, None, :]
    cb = jnp.einsum('gop,gpc->goc', cr, bb_re) - jnp.einsum('gop,gpc->goc', ci, bb_im)

    bb4 = jnp.stack([abb_re, bb_re, abb_im, bb_im]).astype(bf16)
    bt = jnp.transpose(bb4, (0, 1, 3, 2)).reshape(4, 8, 4, S5_GROUP, S5_STATE)
    gl = jnp.arange(8)[None, :, None]
    gs = jnp.arange(4)[None, None, :]
    nn = jnp.arange(8)[:, None, None]
    sel = (gl == 4 * (nn % 2) + gs).astype(bf16)
    t = jnp.einsum('ngs,qnscp->qngcsp', sel, bt, preferred_element_type=bf16)
    t = t.reshape(2, 2, 8, LANE, 4 * S5_STATE)
    bmat = jnp.transpose(t, (0, 2, 1, 3, 4)).reshape(16, 2 * LANE, 4 * S5_STATE)

    c4 = jnp.stack([cr, -ci, ca_re, -ca_im]).astype(bf16)
    ct = jnp.transpose(c4, (0, 1, 3, 2)).reshape(4, 2, 16, S5_STATE, S5_GROUP)
    eye = jnp.eye(16, dtype=bf16)
    cmat = jnp.einsum('gh,qjgpo->qjgpho', eye, ct, preferred_element_type=bf16)
    cmat = cmat.reshape(4, 2, 16 * S5_STATE, 16 * S5_GROUP)
    cbt = jnp.transpose(cb, (0, 2, 1)).reshape(2, 16, S5_GROUP, S5_GROUP)
    cbmat = jnp.einsum('gh,jgco->jgcho', jnp.eye(16, dtype=f32), cbt).reshape(2, 256, 256).astype(bf16)
    a2_re = (ab_re * ab_re - ab_im * ab_im).reshape(N_SLAB, 1, LANE)
    a2_im = (2.0 * ab_re * ab_im).reshape(N_SLAB, 1, LANE)
    return bmat, cmat, cbmat, a2_re, a2_im


def _s5_body(u_ref, bm_ref, cm_ref, cbm_ref, are_ref, aim_ref, d_ref, y_ref,
             wbuf, hbuf, obuf, hstate, zcarry):
    tt = u_ref.shape[1]
    tp = tt // 2
    rows = SUBLANE * tp
    pitch = tp + S5_PITCH_PAD
    nl = D_S5 // LANE

    @pl.when(pl.program_id(0) == 0)
    def _():
        hstate[...] = jnp.zeros_like(hstate)
        zcarry[...] = jnp.zeros_like(zcarry)

    w = pltpu.bitcast(u_ref[...].reshape(SUBLANE * tt, D_S5), u32)
    for b in range(SUBLANE):
        for k in range(nl):
            wbuf[k, b * pitch:b * pitch + tp, :] = w[b * tp:(b + 1) * tp, k * LANE:(k + 1) * LANE]
    w = jnp.concatenate(
        [jnp.concatenate([wbuf[k, pl.ds(m, SUBLANE, stride=pitch), :] for m in range(tp)], axis=0)
         for k in range(nl)], axis=1)
    ue_f = pltpu.bitcast(w << 16, f32)
    uo_f = pltpu.bitcast(w & jnp.uint32(0xFFFF0000), f32)
    ue = ue_f.astype(bf16)
    uo = uo_f.astype(bf16)

    for n in range(2 * 8):
        ks = LANE * ((n % 8) // 2)
        lhs = jnp.concatenate([ue[:, ks:ks + LANE], uo[:, ks:ks + LANE]], axis=1)
        res = jnp.dot(lhs, bm_ref[n], preferred_element_type=f32)
        hbuf[2 * n] = res[:, :LANE]
        hbuf[2 * n + 1] = res[:, LANE:]

    for c0 in range(0, N_SLAB, S5_SLABS):
        ar = [jnp.broadcast_to(are_ref[c0 + s], (SUBLANE, LANE)) for s in range(S5_SLABS)]
        ai = [jnp.broadcast_to(aim_ref[c0 + s], (SUBLANE, LANE)) for s in range(S5_SLABS)]

        def step(t, carry):
            out = []
            sl = pl.ds(pl.multiple_of(t * SUBLANE, SUBLANE), SUBLANE)
            for s in range(S5_SLABS):
                hr, hi = carry[2 * s], carry[2 * s + 1]
                nr = ar[s] * hr - ai[s] * hi + hbuf[c0 + s, sl, :]
                ni = ar[s] * hi + ai[s] * hr + hbuf[N_SLAB + c0 + s, sl, :]
                hbuf[c0 + s, sl, :] = nr
                hbuf[N_SLAB + c0 + s, sl, :] = ni
                out += [nr, ni]
            return tuple(out)

        init = []
        for s in range(S5_SLABS):
            init += [hstate[c0 + s], hstate[N_SLAB + c0 + s]]
        fin = lax.fori_loop(0, tp, step, tuple(init), unroll=4)
        for s in range(S5_SLABS):
            hstate[c0 + s] = fin[2 * s]
            hstate[N_SLAB + c0 + s] = fin[2 * s + 1]

    for j in range(2):
        h_re = jnp.concatenate([hbuf[8 * j + s] for s in range(8)], axis=1).astype(bf16)
        h_im = jnp.concatenate([hbuf[N_SLAB + 8 * j + s] for s in range(8)], axis=1).astype(bf16)
        cs = slice(256 * j, 256 * j + 256)
        dj = d_ref[:, cs]
        yo = jnp.dot(h_re, cm_ref[0, j], preferred_element_type=f32)
        yo = yo + jnp.dot(h_im, cm_ref[1, j], preferred_element_type=f32)
        yo = yo + dj * uo_f[:, cs]
        z = jnp.dot(h_re, cm_ref[2, j], preferred_element_type=f32)
        z = z + jnp.dot(h_im, cm_ref[3, j], preferred_element_type=f32)
        zs = jnp.concatenate([zcarry[:, cs], z[:rows - SUBLANE, :]], axis=0)
        zcarry[:, cs] = z[rows - SUBLANE:, :]
        ye = zs + jnp.dot(ue[:, cs], cbm_ref[j], preferred_element_type=f32) + dj * ue_f[:, cs]
        ge = pltpu.bitcast(jax.nn.gelu(ye).astype(bf16).astype(f32), u32)
        go = pltpu.bitcast(jax.nn.gelu(yo).astype(bf16).astype(f32), u32)
        packed = (ge >> 16) | go
        obuf[2 * j] = packed[:, :LANE]
        obuf[2 * j + 1] = packed[:, LANE:]

    for b in range(SUBLANE):
        yb = jnp.concatenate(
            [jnp.concatenate([obuf[k, pl.ds(8 * SUBLANE * i + b, SUBLANE, stride=SUBLANE), :]
                              for i in range(tp // SUBLANE)], axis=0) for k in range(nl)], axis=1)
        y_ref[b] = pltpu.bitcast(yb, bf16)


def _s5(l, u3, bmat, cmat, cbmat, a2_re, a2_im, d_skip):
    bsz, seq, _ = u3.shape
    assert bsz == SUBLANE
    tt = TT_S5
    blk = lambda i: (0, i, 0)
    return pl.pallas_call(
        _s5_body,
        grid=(seq // tt,),
        in_specs=[pl.BlockSpec((bsz, tt, D_S5), blk)]
        + [_layer(a, l) for a in (bmat, cmat, cbmat, a2_re, a2_im, d_skip)],
        out_specs=pl.BlockSpec((bsz, tt, D_S5), blk),
        out_shape=SDS((bsz, seq, D_S5), bf16),
        scratch_shapes=[pltpu.VMEM((D_S5 // LANE, bsz * (tt // 2 + S5_PITCH_PAD), LANE), u32),
                        pltpu.VMEM((2 * N_SLAB, bsz * (tt // 2), LANE), f32),
                        pltpu.VMEM((D_S5 // LANE, bsz * (tt // 2), LANE), u32),
                        pltpu.VMEM((2 * N_SLAB, SUBLANE, LANE), f32),
                        pltpu.VMEM((SUBLANE, D_S5), f32)],
        compiler_params=_cparams("arbitrary"),
        name="s5_scan",
    )(u3, bmat, cmat, cbmat, a2_re, a2_im, d_skip)


def _split2(x):
    hi = x.astype(bf16)
    lo = (x - hi.astype(f32)).astype(bf16)
    return hi, lo


def _log_sigmoid(x):
    return -(jnp.maximum(-x, 0.0) + jnp.log1p(jnp.exp(-jnp.abs(x))))


def _gla_body(q_ref, kt_ref, v_ref, r_ref, g_ref, wa_ref, ba_ref, gn_ref,
              o_ref, s_ref):
    c = GLA_CHUNK
    tg = q_ref.shape[0]

    @pl.when(pl.program_id(1) == 0)
    def _():
        s_ref[...] = jnp.zeros_like(s_ref)

    ts = GLA_SUB
    nc = ts // c
    nh = GLA_HEADS
    ri = lax.broadcasted_iota(i32, (ts, ts), 0)
    ci = lax.broadcasted_iota(i32, (ts, ts), 1)
    same = (ri // c) == (ci // c)
    tril = (same & (ri >= ci)).astype(bf16)
    triu = (same & (ri <= ci)).astype(bf16)
    blk = same.astype(bf16)
    head_of_lane = lax.broadcasted_iota(i32, (c, D_GLA_K), 1) // GLA_DK
    r_idx = lax.broadcasted_iota(i32, (nc * nh * c, ts), 0)
    c_idx = lax.broadcasted_iota(i32, (nc * nh * c, ts), 1)
    causal = (r_idx // (nh * c) == c_idx // c) & (r_idx % c >= c_idx % c)
    chunk_of_lane = lax.broadcasted_iota(i32, (D_GLA_K, ts), 1) // c

    nsub = tg // ts
    subs = [slice(st * ts, (st + 1) * ts) for st in range(nsub)]

    def decay_stage(rs):
        g = g_ref[rs, :]
        la = _log_sigmoid(jnp.dot(g, wa_ref[...], preferred_element_type=f32) + ba_ref[...]) / GLA_TAU
        lat = la.T
        la_hi, la_lo = _split2(la)
        cum = (jnp.dot(tril, la_hi, preferred_element_type=f32)
               + jnp.dot(tril, la_lo, preferred_element_type=f32))
        lat_hi, lat_lo = _split2(lat)
        cumt = (jnp.dot(lat_hi, triu, preferred_element_type=f32)
                + jnp.dot(lat_lo, triu, preferred_element_type=f32))
        clt = (jnp.dot(lat_hi, blk, preferred_element_type=f32)
               + jnp.dot(lat_lo, blk, preferred_element_type=f32))
        qt = q_ref[rs, :].astype(f32) * jnp.exp(cum)
        ktt = kt_ref[:, rs].astype(f32)
        k_t = (ktt * jnp.exp(-cumt)).astype(bf16)
        k_end = ktt * jnp.exp(clt - cumt)
        q_stack = jnp.concatenate(
            [jnp.where(head_of_lane == h, qt[cc * c:(cc + 1) * c, :], 0.0)
             for cc in range(nc) for h in range(nh)], axis=0).astype(bf16)
        k_stack = jnp.concatenate(
            [jnp.where(chunk_of_lane == cc, k_end, 0.0) for cc in range(nc)], axis=0).astype(bf16)
        dec = [jnp.exp(clt[:, cc * c:cc * c + 1]) for cc in range(nc)]
        return q_stack, k_t, k_stack, dec

    def matmul_stage(rs, staged, state):
        q_stack, k_t, k_stack, dec = staged
        v = v_ref[rs, :]
        scores = jnp.dot(q_stack, k_t, preferred_element_type=f32)
        scores = jnp.where(causal, scores, 0.0).astype(bf16)
        o_full = jnp.dot(scores, v, preferred_element_type=f32)
        upd_full = jnp.dot(k_stack, v, preferred_element_type=f32)
        o_inter = []
        for cc in range(nc):
            o_inter.append(jnp.dot(q_stack[cc * nh * c:(cc + 1) * nh * c, :], state.astype(bf16),
                                   preferred_element_type=f32))
            upd = jnp.concatenate(
                [upd_full[cc * D_GLA_K + h * GLA_DK:cc * D_GLA_K + (h + 1) * GLA_DK,
                          h * GLA_DV:(h + 1) * GLA_DV] for h in range(nh)], axis=0)
            state = dec[cc] * state + upd
        return o_full, o_inter, state

    def output_stage(rs, o_full, o_inter):
        rows = []
        for cc in range(nc):
            outs = []
            for h in range(nh):
                r0 = (cc * nh + h) * c
                o = o_full[r0:r0 + c, h * GLA_DV:(h + 1) * GLA_DV] + o_inter[cc][h * c:(h + 1) * c, :]
                o = o * lax.rsqrt(jnp.mean(o * o, axis=-1, keepdims=True) + EPS)
                outs.append(o)
            rows.append(jnp.concatenate(outs, axis=1))
        o_cat = jnp.concatenate(rows, axis=0)
        o_ref[rs, :] = (o_cat * gn_ref[...] * jax.nn.silu(r_ref[rs, :].astype(f32))).astype(bf16)

    state = s_ref[...]
    staged = decay_stage(subs[0])
    for st in range(nsub):
        nxt = decay_stage(subs[st + 1]) if st + 1 < nsub else None
        o_full, o_inter, state = matmul_stage(subs[st], staged, state)
        output_stage(subs[st], o_full, o_inter)
        staged = nxt
    s_ref[...] = state


def _gla(l, q3, kt3, v3, r3, g3, wa, ba, gn):
    bsz, seq, _ = q3.shape
    tg = TG_GLA
    tok = lambda b, i: (b, i, 0)
    return pl.pallas_call(
        _gla_body,
        grid=(bsz, seq // tg),
        in_specs=[pl.BlockSpec((None, tg, D_GLA_K), tok),
                  pl.BlockSpec((None, D_GLA_K, tg), lambda b, i: (b, 0, i)),
                  pl.BlockSpec((None, tg, D_GLA), tok),
                  pl.BlockSpec((None, tg, D_GLA), tok),
                  pl.BlockSpec((None, tg, LANE), tok),
                  _layer(wa, l), _layer(ba, l), _layer(gn, l)],
        out_specs=pl.BlockSpec((None, tg, D_GLA), tok),
        out_shape=SDS((bsz, seq, D_GLA), bf16),
        scratch_shapes=[pltpu.VMEM((D_GLA_K, GLA_DV), f32)],
        compiler_params=_cparams("parallel", "arbitrary"),
        name="gla",
    )(q3, kt3, v3, r3, g3, wa, ba, gn)


def _mix_out(x_ref, ys_ref, yg_ref, wgl_ref, bg_ref, gs_ref, wo_ref):
    y = ys_ref[...]
    z = jnp.dot(y, wgl_ref[...], preferred_element_type=f32) + bg_ref[...]
    yf = y.astype(f32) * jax.nn.sigmoid(z)
    ys = _rms(yf, gs_ref[...]).astype(bf16)
    acc = jnp.dot(ys, wo_ref[0:D_S5, :], preferred_element_type=f32)
    acc = acc + jnp.dot(yg_ref[...], wo_ref[D_S5:, :], preferred_element_type=f32)
    return x_ref[...] + acc


def _mix_specs(l, tm, w_glu, b_glu, g_s5, w_out):
    row = lambda i: (i, 0)
    once = dict(pipeline_mode=pl.Buffered(1))
    return [pl.BlockSpec((tm, D_MODEL), row),
            pl.BlockSpec((tm, D_S5), row),
            pl.BlockSpec((tm, D_GLA), row),
            _layer(w_glu, l, **once), _layer(b_glu, l), _layer(g_s5, l), _layer(w_out, l, **once)]


def _mix_ffn_body(x_ref, ys_ref, yg_ref, wgl_ref, bg_ref, gs_ref, wo_ref,
                  g_ref, wg_ref, wu_ref, wd_ref, o_ref):
    x = _mix_out(x_ref, ys_ref, yg_ref, wgl_ref, bg_ref, gs_ref, wo_ref)
    hn = _rms(x, g_ref[...]).astype(bf16)
    acc = x
    for lo, hi in FFN_CHUNKS:
        gate = jnp.dot(hn, wg_ref[:, lo:hi], preferred_element_type=f32)
        up = jnp.dot(hn, wu_ref[:, lo:hi], preferred_element_type=f32)
        act = (jax.nn.silu(gate) * up).astype(bf16)
        acc = acc + jnp.dot(act, wd_ref[lo:hi, :], preferred_element_type=f32)
    o_ref[...] = acc


def _mix_ffn(l, i_ffn, x2, ys, yg, w_glu, b_glu, g_s5, w_out, g, wg, wu, wd):
    n = x2.shape[0]
    tm = TM_FFN
    row = lambda i: (i, 0)
    once = dict(pipeline_mode=pl.Buffered(1))
    return pl.pallas_call(
        _mix_ffn_body,
        grid=(n // tm,),
        in_specs=_mix_specs(l, tm, w_glu, b_glu, g_s5, w_out) + [
            _layer(g, l), _layer(wg, i_ffn, **once), _layer(wu, i_ffn, **once),
            _layer(wd, i_ffn, **once)],
        out_specs=pl.BlockSpec((tm, D_MODEL), row),
        out_shape=SDS((n, D_MODEL), f32),
        compiler_params=_cparams("parallel"),
        name="mix_ffn",
    )(x2, ys, yg, w_glu, b_glu, g_s5, w_out, g, wg, wu, wd)


def _pack_bf16_pairs(a):
    bits = pltpu.bitcast(a.astype(bf16).astype(f32), u32)
    half = a.shape[1] // 2
    return bits[:, :half] | (bits[:, half:] >> 16)


def _unpack_bf16_pairs(p):
    hi = pltpu.bitcast(p & jnp.uint32(0xFFFF0000), f32).astype(bf16)
    lo = pltpu.bitcast(p << 16, f32).astype(bf16)
    return hi, lo


def _mix_router_body(x_ref, ys_ref, yg_ref, wgl_ref, bg_ref, gs_ref, wo_ref,
                     g_ref, wh_ref, wl_ref, xo_ref, hp_ref, meta_ref, wcol_ref, cnt_ref, carry):
    tm = x_ref.shape[0]

    @pl.when(pl.program_id(0) == 0)
    def _():
        carry[...] = jnp.zeros_like(carry)

    x = _mix_out(x_ref, ys_ref, yg_ref, wgl_ref, bg_ref, gs_ref, wo_ref)
    xo_ref[...] = x
    hn = _rms(x, g_ref[...])
    packed = _pack_bf16_pairs(hn)
    hp_ref[0] = packed[:, :SC_ROW]
    hp_ref[1] = packed[:, SC_ROW:]

    h_hi, h_lo = _split2(hn)
    nt = (((1,), (1,)), ((), ()))
    logits = (lax.dot_general(wh_ref[...], h_hi, nt, preferred_element_type=f32)
              + lax.dot_general(wh_ref[...], h_lo, nt, preferred_element_type=f32)
              + lax.dot_general(wl_ref[...], h_hi, nt, preferred_element_type=f32))
    er = logits.shape[0]
    row = lax.broadcasted_iota(i32, (er, tm), 0)
    neg = jnp.float32(-jnp.inf)
    logits = jnp.where(row < N_EXPERTS, logits, neg)
    m1 = jnp.max(logits, axis=0, keepdims=True)
    i1 = jnp.min(jnp.where(logits == m1, row, er), axis=0, keepdims=True)
    l2 = jnp.where(row == i1, neg, logits)
    m2 = jnp.max(l2, axis=0, keepdims=True)
    i2 = jnp.min(jnp.where(l2 == m2, row, er), axis=0, keepdims=True)
    e21 = jnp.exp(m2 - m1)
    w1 = 1.0 / (1.0 + e21)
    w2 = e21 / (1.0 + e21)

    sel1 = row == i1
    sel2 = row == i2
    sel = (sel1 | sel2).astype(f32)
    ri = lax.broadcasted_iota(i32, (tm, tm), 0)
    ci = lax.broadcasted_iota(i32, (tm, tm), 1)
    triu = (ri <= ci).astype(bf16)
    incl = jnp.dot(sel.astype(bf16), triu, preferred_element_type=f32)
    rank = incl - sel + carry[:, 0:1]
    r1 = jnp.sum(jnp.where(sel1, rank, 0.0), axis=0, keepdims=True)
    r2 = jnp.sum(jnp.where(sel2, rank, 0.0), axis=0, keepdims=True)
    new_cnt = carry[:, 0:1] + incl[:, tm - 1:tm]
    carry[...] = jnp.broadcast_to(new_cnt, carry.shape)
    cnt_ref[...] = jnp.broadcast_to(new_cnt, cnt_ref.shape)

    srow = lax.broadcasted_iota(i32, (SUBLANE, tm), 0)
    meta = jnp.where(srow == 0, i1.astype(f32), 0.0)
    meta = jnp.where(srow == 1, i2.astype(f32), meta)
    meta = jnp.where(srow == 2, r1, meta)
    meta = jnp.where(srow == 3, r2, meta)
    meta_ref[...] = meta
    prow = lax.broadcasted_iota(i32, (LANE, tm), 0)
    wpad = jnp.where(prow == 0, w1, jnp.where(prow == 1, w2, 0.0))
    wcol_ref[...] = wpad.T


def _mix_router(l, x2, ys, yg, w_glu, b_glu, g_s5, w_out, g, w_hi, w_lo):
    n = x2.shape[0]
    tm = TM_ROUTE
    row = lambda i: (i, 0)
    const = lambda i: (0, 0)
    return pl.pallas_call(
        _mix_router_body,
        grid=(n // tm,),
        in_specs=_mix_specs(l, tm, w_glu, b_glu, g_s5, w_out) + [
            _layer(g, l),
            pl.BlockSpec(w_hi.shape, const),
            pl.BlockSpec(w_lo.shape, const)],
        out_specs=[pl.BlockSpec((tm, D_MODEL), row),
                   pl.BlockSpec((2, tm, SC_ROW), lambda i: (0, i, 0)),
                   pl.BlockSpec((SUBLANE, tm), lambda i: (0, i)),
                   pl.BlockSpec((tm, LANE), row),
                   pl.BlockSpec(w_hi.shape[:1] + (LANE,), const)],
        out_shape=[SDS((n, D_MODEL), f32), SDS((2, n, SC_ROW), u32), SDS((SUBLANE, n), f32),
                   SDS((n, LANE), f32), SDS(w_hi.shape[:1] + (LANE,), f32)],
        scratch_shapes=[pltpu.VMEM(w_hi.shape[:1] + (LANE,), f32)],
        compiler_params=_cparams("arbitrary"),
        name="mix_router",
    )(x2, ys, yg, w_glu, b_glu, g_s5, w_out, g, w_hi, w_lo)


def _sc_gather(table, idx):
    ni = idx.shape[0]
    mesh = plsc.VectorSubcoreMesh(core_axis_name="core", subcore_axis_name="subcore")
    idx2 = idx.reshape(1, ni)

    @pl.kernel(out_type=SDS((ni, SC_ROW), table.dtype), mesh=mesh)
    def kern(t_hbm, i_hbm, o_hbm):
        def body(i_vmem, o_vmem):
            pltpu.sync_copy(t_hbm.at[i_vmem.at[0]], o_vmem)

        pltpu.emit_pipeline(
            body, grid=(ni // SC_WINDOW,),
            in_specs=[pl.BlockSpec((1, SC_WINDOW), index_map=lambda i: (0, i))],
            out_specs=[pl.BlockSpec((SC_WINDOW, SC_ROW), index_map=lambda i: (i, 0))],
            core_axis_name=("core", "subcore"),
            dimension_semantics=(pltpu.PARALLEL,),
        )(i_hbm, o_hbm)

    return kern(table, idx2)


def _sc_scatter2(x, idx0, idx1, nrows):
    ni = x.shape[0]
    mesh = plsc.VectorSubcoreMesh(core_axis_name="core", subcore_axis_name="subcore")

    @pl.kernel(out_type=SDS((nrows, SC_ROW), x.dtype), mesh=mesh)
    def kern(x_hbm, i0_hbm, i1_hbm, o_hbm):
        def body(x_vmem, i0_vmem, i1_vmem):
            pltpu.sync_copy(x_vmem, o_hbm.at[i0_vmem.at[0]])
            pltpu.sync_copy(x_vmem, o_hbm.at[i1_vmem.at[0]])

        pltpu.emit_pipeline(
            body, grid=(ni // SC_WINDOW,),
            in_specs=[pl.BlockSpec((SC_WINDOW, SC_ROW), index_map=lambda i: (i, 0)),
                      pl.BlockSpec((1, SC_WINDOW), index_map=lambda i: (0, i)),
                      pl.BlockSpec((1, SC_WINDOW), index_map=lambda i: (0, i))],
            out_specs=[],
            core_axis_name=("core", "subcore"),
            dimension_semantics=(pltpu.PARALLEL,),
        )(x_hbm, i0_hbm, i1_hbm)

    return kern(x, idx0.reshape(1, ni), idx1.reshape(1, ni))


def _moe_ffn_body(be_ref, bn_ref, xs_ref, wg_ref, wu_ref, wd_ref, ys_ref, xb, act0, act1, acc):
    i = pl.program_id(0)
    f = pl.program_id(1)
    nf = pl.num_programs(1) - 1
    nvalid = bn_ref[i]
    sub = xs_ref.shape[1] // MOE_SUB

    def emit(start, size, cond):
        rs = pl.ds(start, size)

        def gate_up():
            x = xb[rs, :]
            gate = jnp.dot(x, wg_ref[...].astype(bf16), preferred_element_type=f32)
            up = jnp.dot(x, wu_ref[...].astype(bf16), preferred_element_type=f32)
            return (jax.nn.silu(gate) * up).astype(bf16)

        def down(act_ref):
            return jnp.dot(act_ref[rs, :], wd_ref[...].astype(bf16), preferred_element_type=f32)

        @pl.when(cond & (f == 0))
        def _():
            live = lax.broadcasted_iota(i32, (size, SC_ROW), 0) + start < nvalid
            hi0, lo0 = _unpack_bf16_pairs(jnp.where(live, xs_ref[0, rs, :], jnp.uint32(0)))
            hi1, lo1 = _unpack_bf16_pairs(jnp.where(live, xs_ref[1, rs, :], jnp.uint32(0)))
            xb[rs, :] = jnp.concatenate([hi0, hi1, lo0, lo1], axis=1)
            acc[rs, :] = jnp.zeros((size, D_MODEL), f32)
            act0[rs, :] = gate_up()

        @pl.when(cond & (f > 0) & (f < nf) & (f % 2 == 1))
        def _():
            act1[rs, :] = gate_up()
            acc[rs, :] += down(act0)

        @pl.when(cond & (f > 0) & (f < nf) & (f % 2 == 0))
        def _():
            act0[rs, :] = gate_up()
            acc[rs, :] += down(act1)

        @pl.when(cond & (f == nf))
        def _():
            last = act0 if (D_FF_EXPERT // TF_MOE - 1) % 2 == 0 else act1
            packed = _pack_bf16_pairs(acc[rs, :] + down(last))
            ys_ref[0, rs, :] = packed[:, :SC_ROW]
            ys_ref[1, rs, :] = packed[:, SC_ROW:]

    def zero_out(start, size, cond):
        @pl.when(cond & (f == nf))
        def _():
            ys_ref[:, pl.ds(start, size), :] = jnp.zeros((2, size, SC_ROW), u32)

    for h in range(MOE_SUB):
        live = nvalid - h * sub
        whole = live > sub // 2
        head = (live > 0) & (live <= sub // 2)
        emit(h * sub, sub, whole)
        emit(h * sub, sub // 2, head)
        zero_out(h * sub + sub // 2, sub // 2, head)
        zero_out(h * sub, sub, live <= 0)


def _moe_ffn(i_moe, blk_e, blk_n, xs, wg, wu, wd):
    npad = xs.shape[1]
    tm, tf = TM_MOE, TF_MOE
    nblk = npad // tm
    nf = D_FF_EXPERT // tf

    def nxt(i):
        return jnp.minimum(i + 1, nblk - 1)

    def x_idx(i, f, be, bn):
        return (0, jnp.where(f == nf, nxt(i), i), 0)

    def gu_idx(i, f, be, bn):
        ahead = (f == nf) & (bn[i] > 0) & (bn[nxt(i)] > 0)
        e = jnp.where(ahead, be[nxt(i)], be[i])
        t = jnp.where(ahead, 0, jnp.where(bn[i] > 0, jnp.minimum(f, nf - 1), nf - 1))
        return (i_moe, e, 0, t)

    def d_idx(i, f, be, bn):
        prev = jnp.maximum(i - 1, 0)
        keep = (f == 0) & (i > 0)
        e = jnp.where(keep, be[prev], be[i])
        t = jnp.where(keep | (bn[i] == 0), nf - 1, jnp.maximum(f - 1, 0))
        return (i_moe, e, t, 0)

    grid_spec = pltpu.PrefetchScalarGridSpec(
        num_scalar_prefetch=2,
        grid=(nblk, nf + 1),
        in_specs=[pl.BlockSpec((2, tm, SC_ROW), x_idx),
                  pl.BlockSpec((None, None, D_MODEL, tf), gu_idx),
                  pl.BlockSpec((None, None, D_MODEL, tf), gu_idx),
                  pl.BlockSpec((None, None, tf, D_MODEL), d_idx)],
        out_specs=pl.BlockSpec((2, tm, SC_ROW), lambda i, f, be, bn: (0, i, 0)),
        scratch_shapes=[pltpu.VMEM((tm, D_MODEL), bf16), pltpu.VMEM((tm, tf), bf16),
                        pltpu.VMEM((tm, tf), bf16), pltpu.VMEM((tm, D_MODEL), f32)],
    )
    return pl.pallas_call(
        _moe_ffn_body,
        grid_spec=grid_spec,
        out_shape=SDS((2, npad, SC_ROW), u32),
        compiler_params=_cparams("parallel", "arbitrary"),
        name="moe_ffn",
    )(blk_e, blk_n, xs, wg, wu, wd)


def _combine_body(x_ref, yg_ref, wcol_ref, g_ref, o_ref, *, final_norm):
    w1 = wcol_ref[:, 0:1]
    w2 = wcol_ref[:, 1:2]

    def rows(k):
        hi0, lo0 = _unpack_bf16_pairs(yg_ref[0, k])
        hi1, lo1 = _unpack_bf16_pairs(yg_ref[1, k])
        return jnp.concatenate([hi0, hi1, lo0, lo1], axis=1).astype(f32)

    xo = x_ref[...] + (w1 * rows(0) + w2 * rows(1))
    if final_norm:
        xo = _rms(xo, g_ref[...])
    o_ref[...] = xo


def _combine(x2, yg, wcol, g_final, final_norm):
    n = x2.shape[0]
    tm = TM_ROUTE
    row = lambda i: (i, 0)
    return pl.pallas_call(
        functools.partial(_combine_body, final_norm=final_norm),
        grid=(n // tm,),
        in_specs=[pl.BlockSpec((tm, D_MODEL), row),
                  pl.BlockSpec((2, 2, tm, SC_ROW), lambda i: (0, 0, i, 0)),
                  pl.BlockSpec((tm, LANE), row),
                  pl.BlockSpec((1, D_MODEL), lambda i: (0, 0))],
        out_specs=pl.BlockSpec((tm, D_MODEL), row),
        out_shape=SDS((n, D_MODEL), f32),
        compiler_params=_cparams("parallel"),
        name="moe_combine",
    )(x2, yg, wcol, g_final)


def _moe_layer(l, i_moe, mix_args, g_ffn, w_router, wg, wu, wd, g_final, final_norm):
    n = mix_args[0].shape[0]
    tm = TM_MOE
    npad = 2 * n + N_EXPERTS * tm
    wr = jnp.zeros((2 * SUBLANE, D_MODEL), f32).at[:N_EXPERTS].set(w_router[i_moe].astype(f32).T)
    wr_hi = wr.astype(bf16)
    wr_lo = (wr - wr_hi.astype(f32)).astype(bf16)
    x2, hp, meta, wcol, cnt = _mix_router(l, *mix_args, g_ffn, wr_hi, wr_lo)

    counts = cnt[:N_EXPERTS, 0].astype(i32)
    padded = ((counts + tm - 1) // tm) * tm
    ends = jnp.cumsum(padded)
    offs = ends - padded
    ids = jnp.arange(N_EXPERTS, dtype=i32)
    e12 = meta[0:2].astype(i32)
    r12 = meta[2:4].astype(i32)
    pos_t = r12 + jnp.sum(jnp.where(e12[..., None] == ids, offs, 0), axis=-1)
    blk_start = jnp.arange(npad // tm, dtype=i32) * tm
    blk_e = jnp.minimum(jnp.sum((blk_start[:, None] >= ends[None, :]).astype(i32), axis=1), N_EXPERTS - 1)
    own = blk_e[:, None] == ids
    blk_end = jnp.sum(jnp.where(own, offs + counts, 0), axis=1)
    blk_n = jnp.where(blk_start < ends[-1], jnp.clip(blk_end - blk_start, 0, tm), 0)
    last_e = jnp.max(jnp.where(blk_n > 0, blk_e, 0))
    blk_e = jnp.where(blk_n > 0, blk_e, last_e)

    xs = _sc_scatter2(hp.reshape(2 * n, SC_ROW),
                      jnp.concatenate([pos_t[0], pos_t[0] + npad]),
                      jnp.concatenate([pos_t[1], pos_t[1] + npad]),
                      2 * npad).reshape(2, npad, SC_ROW)
    ys = _moe_ffn(i_moe, blk_e, blk_n, xs, wg, wu, wd)
    gidx = jnp.concatenate([pos_t.reshape(-1), pos_t.reshape(-1) + npad])
    yg = _sc_gather(ys.reshape(2 * npad, SC_ROW), gidx).reshape(2, 2, n, SC_ROW)
    return _combine(x2, yg, wcol, g_final, final_norm)


def _rows(v):
    return v.astype(f32)[:, None, :]


def kernel(x, norm_mix, w_in, s5_lambda_re, s5_lambda_im, s5_log_dt, s5_b_re, s5_b_im, s5_c_re, s5_c_im, s5_d, s5_w_glu, s5_b_glu, s5_out_norm, gla_w_a2, gla_b_a2, gla_out_norm, w_out, norm_ffn, ffn_w_gate, ffn_w_up, ffn_w_down, moe_w_router, moe_w_gate, moe_w_up, moe_w_down, norm_final):
    bsz, seq, _ = x.shape
    n = bsz * seq
    depth = w_in.shape[0]
    x2 = x.reshape(n, D_MODEL)

    w_gp = jnp.zeros((depth, D_MODEL, LANE), f32).at[:, :, :GLA_GATE_RANK].set(w_in[:, :, 2048:2064])
    w_cat = jnp.concatenate([w_in[:, :, 0:512], w_in[:, :, 512:768] * (GLA_DK ** -0.5),
                             w_in[:, :, 1024:1536], w_in[:, :, 1536:2048], w_gp], axis=2).astype(bf16)
    w_kt = jnp.swapaxes(w_in[:, :, 768:1024], 1, 2).astype(bf16)
    s5_mats = jax.vmap(_s5_prep)(s5_lambda_re, s5_lambda_im, s5_log_dt, s5_b_re, s5_b_im, s5_c_re, s5_c_im)
    wa = jnp.zeros((depth, LANE, D_GLA_K), f32).at[:, :GLA_GATE_RANK].set(gla_w_a2).astype(bf16)
    g_mix, g_ffn, d_skip = _rows(norm_mix), _rows(norm_ffn), _rows(s5_d)
    ba, gn = _rows(gla_b_a2), _rows(gla_out_norm)
    mix_w = (s5_w_glu.astype(bf16), _rows(s5_b_glu), _rows(s5_out_norm), w_out.astype(bf16))
    ffn_w = (ffn_w_gate.astype(bf16), ffn_w_up.astype(bf16), ffn_w_down.astype(bf16))
    g_final = norm_final.astype(f32).reshape(1, D_MODEL)

    for l in range(depth):
        u, q, kt, v, r, gl = _inproj(l, x2, g_mix, w_cat, w_kt, bsz, seq)
        ys = _s5(l, u.reshape(bsz, seq, D_S5), *s5_mats, d_skip)
        yg = _gla(l, q.reshape(bsz, seq, D_GLA_K), kt, v.reshape(bsz, seq, D_GLA),
                  r.reshape(bsz, seq, D_GLA), gl.reshape(bsz, seq, LANE), wa, ba, gn)
        mix_args = (x2, ys.reshape(n, D_S5), yg.reshape(n, D_GLA)) + mix_w

        last = l == depth - 1
        if l % 2 == 0:
            x2 = _mix_ffn(l, l // 2, *mix_args, g_ffn, *ffn_w)
            if last:
                x2 = _final_norm(x2, g_final)
        else:
            x2 = _moe_layer(l, l // 2, mix_args, g_ffn, moe_w_router, moe_w_gate, moe_w_up,
                            moe_w_down, g_final, last)
    return x2.reshape(bsz, seq, D_MODEL)


def _final_norm_body(x_ref, g_ref, o_ref):
    o_ref[...] = _rms(x_ref[...], g_ref[...])


def _final_norm(x2, g):
    n = x2.shape[0]
    tm = TM_ROUTE
    return pl.pallas_call(
        _final_norm_body,
        grid=(n // tm,),
        in_specs=[pl.BlockSpec((tm, D_MODEL), lambda i: (i, 0)),
                  pl.BlockSpec((1, D_MODEL), lambda i: (0, 0))],
        out_specs=pl.BlockSpec((tm, D_MODEL), lambda i: (i, 0)),
        out_shape=SDS((n, D_MODEL), f32),
        compiler_params=_cparams("parallel"),
        name="final_norm",
    )(x2, g)
```

```python
import functools

import jax
import jax.numpy as jnp
from jax import lax
from jax.experimental import pallas as pl
from jax.experimental.pallas import tpu as pltpu
from jax.experimental.pallas import tpu_sc as plsc

f32 = jnp.float32
bf16 = jnp.bfloat16
u32 = jnp.uint32
i32 = jnp.int32
SDS = jax.ShapeDtypeStruct

D_MODEL = 1024
D_S5 = 512
S5_GROUP = 16
S5_GROUPS = 32
S5_STATE = 64
N_STATE = S5_GROUPS * S5_STATE
D_GLA = 512
GLA_HEADS = 4
GLA_DV = 128
GLA_DK = 64
D_GLA_K = 256
GLA_GATE_RANK = 16
GLA_TAU = 16.0
GLA_CHUNK = 64
D_FF = 2816
N_EXPERTS = 8
D_FF_EXPERT = 3584
EPS = 1e-6

LANE = 128
SUBLANE = 8
VMEM_LIMIT = 56 * 1024 * 1024

TM_PROJ = 2048
TT_S5 = 256
S5_SLABS = 4
S5_PITCH_PAD = 8
TG_GLA = 1024
GLA_SUB = 256
TM_FFN = 1024
FFN_CHUNKS = ((0, 1536), (1536, 2816))
TM_MOE = 2048
MOE_SUB = 2
TF_MOE = 512
TM_ROUTE = 1024
SC_WINDOW = 128
SC_ROW = 256


def _cparams(*sem):
    return pltpu.CompilerParams(dimension_semantics=sem, vmem_limit_bytes=VMEM_LIMIT)


def _layer(arr, l, **kw):
    return pl.BlockSpec((None,) + arr.shape[1:], lambda *_: (l,) + (0,) * (arr.ndim - 1), **kw)


def _rms(x, g):
    ms = jnp.mean(x * x, axis=-1, keepdims=True)
    return x * lax.rsqrt(ms + EPS) * g


def _inproj_body(x_ref, g_ref, w_ref, wkt_ref, u_ref, q_ref, kt_ref, v_ref, r_ref, gl_ref):
    hn = _rms(x_ref[...], g_ref[...]).astype(bf16)

    def proj(lo, hi):
        return jnp.dot(hn, w_ref[:, lo:hi], preferred_element_type=f32).astype(bf16)

    u_ref[...] = proj(0, 512)
    q_ref[...] = proj(512, 768)
    v_ref[...] = proj(768, 1280)
    r_ref[...] = proj(1280, 1792)
    gl_ref[...] = proj(1792, 1920)
    kt_ref[...] = lax.dot_general(wkt_ref[...], hn, (((1,), (1,)), ((), ())),
                                  preferred_element_type=f32).astype(bf16)


def _inproj(l, x2, g, w_cat, w_kt, bsz, seq):
    n = x2.shape[0]
    tm = TM_PROJ
    per_b = seq // tm
    row = lambda i: (i, 0)
    return pl.pallas_call(
        _inproj_body,
        grid=(n // tm,),
        in_specs=[pl.BlockSpec((tm, D_MODEL), row),
                  _layer(g, l), _layer(w_cat, l), _layer(w_kt, l)],
        out_specs=[pl.BlockSpec((tm, D_S5), row),
                   pl.BlockSpec((tm, D_GLA_K), row),
                   pl.BlockSpec((None, D_GLA_K, tm), lambda i: (i // per_b, 0, i % per_b)),
                   pl.BlockSpec((tm, D_GLA), row),
                   pl.BlockSpec((tm, D_GLA), row),
                   pl.BlockSpec((tm, LANE), row)],
        out_shape=[SDS((n, D_S5), bf16), SDS((n, D_GLA_K), bf16),
                   SDS((bsz, D_GLA_K, seq), bf16), SDS((n, D_GLA), bf16),
                   SDS((n, D_GLA), bf16), SDS((n, LANE), bf16)],
        compiler_params=_cparams("parallel"),
        name="inproj",
    )(x2, g, w_cat, w_kt)


N_SLAB = N_STATE // LANE


def _s5_prep(lam_re, lam_im, log_dt, b_re, b_im, c_re, c_im):
    lr = jnp.minimum(lam_re.astype(f32), -1e-4)
    li = lam_im.astype(f32)
    dt = jnp.exp(log_dt.astype(f32))[:, None]
    mag = jnp.exp(lr * dt)
    ab_re = mag * jnp.cos(li * dt)
    ab_im = mag * jnp.sin(li * dt)
    nr = ab_re - 1.0
    ni = ab_im
    den = lr * lr + li * li
    f_re = (nr * lr + ni * li) / den
    f_im = (ni * lr - nr * li) / den
    br = b_re.astype(f32)
    bi = b_im.astype(f32)
    bb_re = f_re[..., None] * br - f_im[..., None] * bi
    bb_im = f_re[..., None] * bi + f_im[..., None] * br
    abb_re = ab_re[..., None] * bb_re - ab_im[..., None] * bb_im
    abb_im = ab_re[..., None] * bb_im + ab_im[..., None] * bb_re
    cr = c_re.astype(f32)
    ci = c_im.astype(f32)
    ca_re = cr * ab_re[:, None, :] - ci * ab_im[:, None, :]
    ca_im = cr * ab_im[:, None, :] + ci * ab_re[:, None, :]
    cb = jnp.einsum('gop,gpc->goc', cr, bb_re) - jnp.einsum('gop,gpc->goc', ci, bb_im)

    bb4 = jnp.stack([abb_re, bb_re, abb_im, bb_im]).astype(bf16)
    bt = jnp.transpose(bb4, (0, 1, 3, 2)).reshape(4, 8, 4, S5_GROUP, S5_STATE)
    gl = jnp.arange(8)[None, :, None]
    gs = jnp.arange(4)[None, None, :]
    nn = jnp.arange(8)[:, None, None]
    sel = (gl == 4 * (nn % 2) + gs).astype(bf16)
    t = jnp.einsum('ngs,qnscp->qngcsp', sel, bt, preferred_element_type=bf16)
    t = t.reshape(2, 2, 8, LANE, 4 * S5_STATE)
    bmat = jnp.transpose(t, (0, 2, 1, 3, 4)).reshape(16, 2 * LANE, 4 * S5_STATE)

    c4 = jnp.stack([cr, -ci, ca_re, -ca_im]).astype(bf16)
    ct = jnp.transpose(c4, (0, 1, 3, 2)).reshape(4, 2, 16, S5_STATE, S5_GROUP)
    eye = jnp.eye(16, dtype=bf16)
    cmat = jnp.einsum('gh,qjgpo->qjgpho', eye, ct, preferred_element_type=bf16)
    cmat = cmat.reshape(4, 2, 16 * S5_STATE, 16 * S5_GROUP)
    cbt = jnp.transpose(cb, (0, 2, 1)).reshape(2, 16, S5_GROUP, S5_GROUP)
    cbmat = jnp.einsum('gh,jgco->jgcho', jnp.eye(16, dtype=f32), cbt).reshape(2, 256, 256).astype(bf16)
    a2_re = (ab_re * ab_re - ab_im * ab_im).reshape(N_SLAB, 1, LANE)
    a2_im = (2.0 * ab_re * ab_im).reshape(N_SLAB, 1, LANE)
    return bmat, cmat, cbmat, a2_re, a2_im


def _s5_body(u_ref, bm_ref, cm_ref, cbm_ref, are_ref, aim_ref, d_ref, y_ref,
             wbuf, hbuf, obuf, hstate, zcarry):
    tt = u_ref.shape[1]
    tp = tt // 2
    rows = SUBLANE * tp
    pitch = tp + S5_PITCH_PAD
    nl = D_S5 // LANE

    @pl.when(pl.program_id(0) == 0)
    def _():
        hstate[...] = jnp.zeros_like(hstate)
        zcarry[...] = jnp.zeros_like(zcarry)

    w = pltpu.bitcast(u_ref[...].reshape(SUBLANE * tt, D_S5), u32)
    for b in range(SUBLANE):
        for k in range(nl):
            wbuf[k, b * pitch:b * pitch + tp, :] = w[b * tp:(b + 1) * tp, k * LANE:(k + 1) * LANE]
    w = jnp.concatenate(
        [jnp.concatenate([wbuf[k, pl.ds(m, SUBLANE, stride=pitch), :] for m in range(tp)], axis=0)
         for k in range(nl)], axis=1)
    ue_f = pltpu.bitcast(w << 16, f32)
    uo_f = pltpu.bitcast(w & jnp.uint32(0xFFFF0000), f32)
    ue = ue_f.astype(bf16)
    uo = uo_f.astype(bf16)

    for n in range(2 * 8):
        ks = LANE * ((n % 8) // 2)
        lhs = jnp.concatenate([ue[:, ks:ks + LANE], uo[:, ks:ks + LANE]], axis=1)
        res = jnp.dot(lhs, bm_ref[n], preferred_element_type=f32)
        hbuf[2 * n] = res[:, :LANE]
        hbuf[2 * n + 1] = res[:, LANE:]

    for c0 in range(0, N_SLAB, S5_SLABS):
        ar = [jnp.broadcast_to(are_ref[c0 + s], (SUBLANE, LANE)) for s in range(S5_SLABS)]
        ai = [jnp.broadcast_to(aim_ref[c0 + s], (SUBLANE, LANE)) for s in range(S5_SLABS)]

        def step(t, carry):
            out = []
            sl = pl.ds(pl.multiple_of(t * SUBLANE, SUBLANE), SUBLANE)
            for s in range(S5_SLABS):
                hr, hi = carry[2 * s], carry[2 * s + 1]
                nr = ar[s] * hr - ai[s] * hi + hbuf[c0 + s, sl, :]
                ni = ar[s] * hi + ai[s] * hr + hbuf[N_SLAB + c0 + s, sl, :]
                hbuf[c0 + s, sl, :] = nr
                hbuf[N_SLAB + c0 + s, sl, :] = ni
                out += [nr, ni]
            return tuple(out)

        init = []
        for s in range(S5_SLABS):
            init += [hstate[c0 + s], hstate[N_SLAB + c0 + s]]
        fin = lax.fori_loop(0, tp, step, tuple(init), unroll=4)
        for s in range(S5_SLABS):
            hstate[c0 + s] = fin[2 * s]
            hstate[N_SLAB + c0 + s] = fin[2 * s + 1]

    for j in range(2):
        h_re = jnp.concatenate([hbuf[8 * j + s] for s in range(8)], axis=1).astype(bf16)
        h_im = jnp.concatenate([hbuf[N_SLAB + 8 * j + s] for s in range(8)], axis=1).astype(bf16)
        cs = slice(256 * j, 256 * j + 256)
        dj = d_ref[:, cs]
        yo = jnp.dot(h_re, cm_ref[0, j], preferred_element_type=f32)
        yo = yo + jnp.dot(h_im, cm_ref[1, j], preferred_element_type=f32)
        yo = yo + dj * uo_f[:, cs]
        z = jnp.dot(h_re, cm_ref[2, j], preferred_element_type=f32)
        z = z + jnp.dot(h_im, cm_ref[3, j], preferred_element_type=f32)
        zs = jnp.concatenate([zcarry[:, cs], z[:rows - SUBLANE, :]], axis=0)
        zcarry[:, cs] = z[rows - SUBLANE:, :]
        ye = zs + jnp.dot(ue[:, cs], cbm_ref[j], preferred_element_type=f32) + dj * ue_f[:, cs]
        ge = pltpu.bitcast(jax.nn.gelu(ye).astype(bf16).astype(f32), u32)
        go = pltpu.bitcast(jax.nn.gelu(yo).astype(bf16).astype(f32), u32)
        packed = (ge >> 16) | go
        obuf[2 * j] = packed[:, :LANE]
        obuf[2 * j + 1] = packed[:, LANE:]

    for b in range(SUBLANE):
        yb = jnp.concatenate(
            [jnp.concatenate([obuf[k, pl.ds(8 * SUBLANE * i + b, SUBLANE, stride=SUBLANE), :]
                              for i in range(tp // SUBLANE)], axis=0) for k in range(nl)], axis=1)
        y_ref[b] = pltpu.bitcast(yb, bf16)


def _s5(l, u3, bmat, cmat, cbmat, a2_re, a2_im, d_skip):
    bsz, seq, _ = u3.shape
    assert bsz == SUBLANE
    tt = TT_S5
    blk = lambda i: (0, i, 0)
    return pl.pallas_call(
        _s5_body,
        grid=(seq // tt,),
        in_specs=[pl.BlockSpec((bsz, tt, D_S5), blk)]
        + [_layer(a, l) for a in (bmat, cmat, cbmat, a2_re, a2_im, d_skip)],
        out_specs=pl.BlockSpec((bsz, tt, D_S5), blk),
        out_shape=SDS((bsz, seq, D_S5), bf16),
        scratch_shapes=[pltpu.VMEM((D_S5 // LANE, bsz * (tt // 2 + S5_PITCH_PAD), LANE), u32),
                        pltpu.VMEM((2 * N_SLAB, bsz * (tt // 2), LANE), f32),
                        pltpu.VMEM((D_S5 // LANE, bsz * (tt // 2), LANE), u32),
                        pltpu.VMEM((2 * N_SLAB, SUBLANE, LANE), f32),
                        pltpu.VMEM((SUBLANE, D_S5), f32)],
        compiler_params=_cparams("arbitrary"),
        name="s5_scan",
    )(u3, bmat, cmat, cbmat, a2_re, a2_im, d_skip)


def _split2(x):
    hi = x.astype(bf16)
    lo = (x - hi.astype(f32)).astype(bf16)
    return hi, lo


def _log_sigmoid(x):
    return -(jnp.maximum(-x, 0.0) + jnp.log1p(jnp.exp(-jnp.abs(x))))


def _gla_body(q_ref, kt_ref, v_ref, r_ref, g_ref, wa_ref, ba_ref, gn_ref,
              o_ref, s_ref):
    c = GLA_CHUNK
    tg = q_ref.shape[0]

    @pl.when(pl.program_id(1) == 0)
    def _():
        s_ref[...] = jnp.zeros_like(s_ref)

    ts = GLA_SUB
    nc = ts // c
    nh = GLA_HEADS
    ri = lax.broadcasted_iota(i32, (ts, ts), 0)
    ci = lax.broadcasted_iota(i32, (ts, ts), 1)
    same = (ri // c) == (ci // c)
    tril = (same & (ri >= ci)).astype(bf16)
    head_of_lane = lax.broadcasted_iota(i32, (c, D_GLA_K), 1) // GLA_DK
    r_idx = lax.broadcasted_iota(i32, (nc * nh * c, ts), 0)
    c_idx = lax.broadcasted_iota(i32, (nc * nh * c, ts), 1)
    causal = (r_idx // (nh * c) == c_idx // c) & (r_idx % c >= c_idx % c)
    chunk_of_lane = lax.broadcasted_iota(i32, (D_GLA_K, ts), 1) // c

    nsub = tg // ts
    subs = [slice(st * ts, (st + 1) * ts) for st in range(nsub)]

    def decay_stage(rs):
        g = g_ref[rs, :]
        la = _log_sigmoid(jnp.dot(g, wa_ref[...], preferred_element_type=f32) + ba_ref[...]) / GLA_TAU
        la_hi, la_lo = _split2(la)
        cum = (jnp.dot(tril, la_hi, preferred_element_type=f32)
               + jnp.dot(tril, la_lo, preferred_element_type=f32))
        cumt = cum.T
        ends = [cumt[:, (cc + 1) * c - 1:(cc + 1) * c] for cc in range(nc)]
        clt = ends[nc - 1]
        for cc in range(nc - 2, -1, -1):
            clt = jnp.where(chunk_of_lane <= cc, ends[cc], clt)
        qt = q_ref[rs, :].astype(f32) * jnp.exp(cum)
        ktt = kt_ref[:, rs].astype(f32)
        k_t = (ktt * jnp.exp(-cumt)).astype(bf16)
        k_end = ktt * jnp.exp(clt - cumt)
        q_stack = jnp.concatenate(
            [jnp.where(head_of_lane == h, qt[cc * c:(cc + 1) * c, :], 0.0)
             for cc in range(nc) for h in range(nh)], axis=0).astype(bf16)
        k_stack = jnp.concatenate(
            [jnp.where(chunk_of_lane == cc, k_end, 0.0) for cc in range(nc)], axis=0).astype(bf16)
        dec = [jnp.exp(clt[:, cc * c:cc * c + 1]) for cc in range(nc)]
        return q_stack, k_t, k_stack, dec

    def matmul_stage(rs, staged, state):
        q_stack, k_t, k_stack, dec = staged
        v = v_ref[rs, :]
        scores = jnp.dot(q_stack, k_t, preferred_element_type=f32)
        scores = jnp.where(causal, scores, 0.0).astype(bf16)
        o_full = jnp.dot(scores, v, preferred_element_type=f32)
        upd_full = jnp.dot(k_stack, v, preferred_element_type=f32)
        o_inter = []
        for cc in range(nc):
            o_inter.append(jnp.dot(q_stack[cc * nh * c:(cc + 1) * nh * c, :], state.astype(bf16),
                                   preferred_element_type=f32))
            upd = jnp.concatenate(
                [upd_full[cc * D_GLA_K + h * GLA_DK:cc * D_GLA_K + (h + 1) * GLA_DK,
                          h * GLA_DV:(h + 1) * GLA_DV] for h in range(nh)], axis=0)
            state = dec[cc] * state + upd
        return o_full, o_inter, state

    def output_stage(rs, o_full, o_inter):
        rows = []
        for cc in range(nc):
            outs = []
            for h in range(nh):
                r0 = (cc * nh + h) * c
                o = o_full[r0:r0 + c, h * GLA_DV:(h + 1) * GLA_DV] + o_inter[cc][h * c:(h + 1) * c, :]
                o = o * lax.rsqrt(jnp.mean(o * o, axis=-1, keepdims=True) + EPS)
                outs.append(o)
            rows.append(jnp.concatenate(outs, axis=1))
        o_cat = jnp.concatenate(rows, axis=0)
        o_ref[rs, :] = (o_cat * gn_ref[...] * jax.nn.silu(r_ref[rs, :].astype(f32))).astype(bf16)

    state = s_ref[...]
    staged = decay_stage(subs[0])
    for st in range(nsub):
        nxt = decay_stage(subs[st + 1]) if st + 1 < nsub else None
        o_full, o_inter, state = matmul_stage(subs[st], staged, state)
        output_stage(subs[st], o_full, o_inter)
        staged = nxt
    s_ref[...] = state


def _gla(l, q3, kt3, v3, r3, g3, wa, ba, gn):
    bsz, seq, _ = q3.shape
    tg = TG_GLA
    tok = lambda b, i: (b, i, 0)
    return pl.pallas_call(
        _gla_body,
        grid=(bsz, seq // tg),
        in_specs=[pl.BlockSpec((None, tg, D_GLA_K), tok),
                  pl.BlockSpec((None, D_GLA_K, tg), lambda b, i: (b, 0, i)),
                  pl.BlockSpec((None, tg, D_GLA), tok),
                  pl.BlockSpec((None, tg, D_GLA), tok),
                  pl.BlockSpec((None, tg, LANE), tok),
                  _layer(wa, l), _layer(ba, l), _layer(gn, l)],
        out_specs=pl.BlockSpec((None, tg, D_GLA), tok),
        out_shape=SDS((bsz, seq, D_GLA), bf16),
        scratch_shapes=[pltpu.VMEM((D_GLA_K, GLA_DV), f32)],
        compiler_params=_cparams("parallel", "arbitrary"),
        name="gla",
    )(q3, kt3, v3, r3, g3, wa, ba, gn)


def _mix_out(x_ref, ys_ref, yg_ref, wgl_ref, bg_ref, gs_ref, wo_ref):
    y = ys_ref[...]
    z = jnp.dot(y, wgl_ref[...], preferred_element_type=f32) + bg_ref[...]
    yf = y.astype(f32) * jax.nn.sigmoid(z)
    ys = _rms(yf, gs_ref[...]).astype(bf16)
    acc = jnp.dot(ys, wo_ref[0:D_S5, :], preferred_element_type=f32)
    acc = acc + jnp.dot(yg_ref[...], wo_ref[D_S5:, :], preferred_element_type=f32)
    return x_ref[...] + acc


def _mix_specs(l, tm, w_glu, b_glu, g_s5, w_out):
    row = lambda i: (i, 0)
    once = dict(pipeline_mode=pl.Buffered(1))
    return [pl.BlockSpec((tm, D_MODEL), row),
            pl.BlockSpec((tm, D_S5), row),
            pl.BlockSpec((tm, D_GLA), row),
            _layer(w_glu, l, **once), _layer(b_glu, l), _layer(g_s5, l), _layer(w_out, l, **once)]


def _mix_ffn_body(x_ref, ys_ref, yg_ref, wgl_ref, bg_ref, gs_ref, wo_ref,
                  g_ref, wg_ref, wu_ref, wd_ref, o_ref):
    x = _mix_out(x_ref, ys_ref, yg_ref, wgl_ref, bg_ref, gs_ref, wo_ref)
    hn = _rms(x, g_ref[...]).astype(bf16)
    acc = x
    for lo, hi in FFN_CHUNKS:
        gate = jnp.dot(hn, wg_ref[:, lo:hi], preferred_element_type=f32)
        up = jnp.dot(hn, wu_ref[:, lo:hi], preferred_element_type=f32)
        act = (jax.nn.silu(gate) * up).astype(bf16)
        acc = acc + jnp.dot(act, wd_ref[lo:hi, :], preferred_element_type=f32)
    o_ref[...] = acc


def _mix_ffn(l, i_ffn, x2, ys, yg, w_glu, b_glu, g_s5, w_out, g, wg, wu, wd):
    n = x2.shape[0]
    tm = TM_FFN
    row = lambda i: (i, 0)
    once = dict(pipeline_mode=pl.Buffered(1))
    return pl.pallas_call(
        _mix_ffn_body,
        grid=(n // tm,),
        in_specs=_mix_specs(l, tm, w_glu, b_glu, g_s5, w_out) + [
            _layer(g, l), _layer(wg, i_ffn, **once), _layer(wu, i_ffn, **once),
            _layer(wd, i_ffn, **once)],
        out_specs=pl.BlockSpec((tm, D_MODEL), row),
        out_shape=SDS((n, D_MODEL), f32),
        compiler_params=_cparams("parallel"),
        name="mix_ffn",
    )(x2, ys, yg, w_glu, b_glu, g_s5, w_out, g, wg, wu, wd)


def _pack_bf16_pairs(a):
    bits = pltpu.bitcast(a.astype(bf16).astype(f32), u32)
    half = a.shape[1] // 2
    return bits[:, :half] | (bits[:, half:] >> 16)


def _unpack_bf16_pairs(p):
    hi = pltpu.bitcast(p & jnp.uint32(0xFFFF0000), f32).astype(bf16)
    lo = pltpu.bitcast(p << 16, f32).astype(bf16)
    return hi, lo


def _mix_router_body(x_ref, ys_ref, yg_ref, wgl_ref, bg_ref, gs_ref, wo_ref,
                     g_ref, wh_ref, wl_ref, xo_ref, hp_ref, meta_ref, wcol_ref, cnt_ref, carry):
    tm = x_ref.shape[0]

    @pl.when(pl.program_id(0) == 0)
    def _():
        carry[...] = jnp.zeros_like(carry)

    x = _mix_out(x_ref, ys_ref, yg_ref, wgl_ref, bg_ref, gs_ref, wo_ref)
    xo_ref[...] = x
    hn = _rms(x, g_ref[...])
    packed = _pack_bf16_pairs(hn)
    hp_ref[0] = packed[:, :SC_ROW]
    hp_ref[1] = packed[:, SC_ROW:]

    h_hi, h_lo = _split2(hn)
    nt = (((1,), (1,)), ((), ()))
    logits = (lax.dot_general(wh_ref[...], h_hi, nt, preferred_element_type=f32)
              + lax.dot_general(wh_ref[...], h_lo, nt, preferred_element_type=f32)
              + lax.dot_general(wl_ref[...], h_hi, nt, preferred_element_type=f32))
    er = logits.shape[0]
    row = lax.broadcasted_iota(i32, (er, tm), 0)
    neg = jnp.float32(-jnp.inf)
    logits = jnp.where(row < N_EXPERTS, logits, neg)
    m1 = jnp.max(logits, axis=0, keepdims=True)
    i1 = jnp.min(jnp.where(logits == m1, row, er), axis=0, keepdims=True)
    l2 = jnp.where(row == i1, neg, logits)
    m2 = jnp.max(l2, axis=0, keepdims=True)
    i2 = jnp.min(jnp.where(l2 == m2, row, er), axis=0, keepdims=True)
    e21 = jnp.exp(m2 - m1)
    w1 = 1.0 / (1.0 + e21)
    w2 = e21 / (1.0 + e21)

    sel1 = row == i1
    sel2 = row == i2
    sel = (sel1 | sel2).astype(f32)
    ri = lax.broadcasted_iota(i32, (tm, tm), 0)
    ci = lax.broadcasted_iota(i32, (tm, tm), 1)
    triu = (ri <= ci).astype(bf16)
    incl = jnp.dot(sel.astype(bf16), triu, preferred_element_type=f32)
    rank = incl - sel + carry[:, 0:1]
    r1 = jnp.sum(jnp.where(sel1, rank, 0.0), axis=0, keepdims=True)
    r2 = jnp.sum(jnp.where(sel2, rank, 0.0), axis=0, keepdims=True)
    new_cnt = carry[:, 0:1] + incl[:, tm - 1:tm]
    carry[...] = jnp.broadcast_to(new_cnt, carry.shape)
    cnt_ref[...] = jnp.broadcast_to(new_cnt, cnt_ref.shape)

    srow = lax.broadcasted_iota(i32, (SUBLANE, tm), 0)
    meta = jnp.where(srow == 0, i1.astype(f32), 0.0)
    meta = jnp.where(srow == 1, i2.astype(f32), meta)
    meta = jnp.where(srow == 2, r1, meta)
    meta = jnp.where(srow == 3, r2, meta)
    meta_ref[...] = meta
    prow = lax.broadcasted_iota(i32, (LANE, tm), 0)
    wpad = jnp.where(prow == 0, w1, jnp.where(prow == 1, w2, 0.0))
    wcol_ref[...] = wpad.T


def _mix_router(l, x2, ys, yg, w_glu, b_glu, g_s5, w_out, g, w_hi, w_lo):
    n = x2.shape[0]
    tm = TM_ROUTE
    row = lambda i: (i, 0)
    const = lambda i: (0, 0)
    return pl.pallas_call(
        _mix_router_body,
        grid=(n // tm,),
        in_specs=_mix_specs(l, tm, w_glu, b_glu, g_s5, w_out) + [
            _layer(g, l),
            pl.BlockSpec(w_hi.shape, const),
            pl.BlockSpec(w_lo.shape, const)],
        out_specs=[pl.BlockSpec((tm, D_MODEL), row),
                   pl.BlockSpec((2, tm, SC_ROW), lambda i: (0, i, 0)),
                   pl.BlockSpec((SUBLANE, tm), lambda i: (0, i)),
                   pl.BlockSpec((tm, LANE), row),
                   pl.BlockSpec(w_hi.shape[:1] + (LANE,), const)],
        out_shape=[SDS((n, D_MODEL), f32), SDS((2, n, SC_ROW), u32), SDS((SUBLANE, n), f32),
                   SDS((n, LANE), f32), SDS(w_hi.shape[:1] + (LANE,), f32)],
        scratch_shapes=[pltpu.VMEM(w_hi.shape[:1] + (LANE,), f32)],
        compiler_params=_cparams("arbitrary"),
        name="mix_router",
    )(x2, ys, yg, w_glu, b_glu, g_s5, w_out, g, w_hi, w_lo)


def _sc_gather(table, idx):
    ni = idx.shape[0]
    mesh = plsc.VectorSubcoreMesh(core_axis_name="core", subcore_axis_name="subcore")
    idx2 = idx.reshape(1, ni)

    @pl.kernel(out_type=SDS((ni, SC_ROW), table.dtype), mesh=mesh)
    def kern(t_hbm, i_hbm, o_hbm):
        def body(i_vmem, o_vmem):
            pltpu.sync_copy(t_hbm.at[i_vmem.at[0]], o_vmem)

        pltpu.emit_pipeline(
            body, grid=(ni // SC_WINDOW,),
            in_specs=[pl.BlockSpec((1, SC_WINDOW), index_map=lambda i: (0, i))],
            out_specs=[pl.BlockSpec((SC_WINDOW, SC_ROW), index_map=lambda i: (i, 0))],
            core_axis_name=("core", "subcore"),
            dimension_semantics=(pltpu.PARALLEL,),
        )(i_hbm, o_hbm)

    return kern(table, idx2)


def _sc_scatter2(x, idx0, idx1, nrows):
    ni = x.shape[0]
    mesh = plsc.VectorSubcoreMesh(core_axis_name="core", subcore_axis_name="subcore")

    @pl.kernel(out_type=SDS((nrows, SC_ROW), x.dtype), mesh=mesh)
    def kern(x_hbm, i0_hbm, i1_hbm, o_hbm):
        def body(x_vmem, i0_vmem, i1_vmem):
            pltpu.sync_copy(x_vmem, o_hbm.at[i0_vmem.at[0]])
            pltpu.sync_copy(x_vmem, o_hbm.at[i1_vmem.at[0]])

        pltpu.emit_pipeline(
            body, grid=(ni // SC_WINDOW,),
            in_specs=[pl.BlockSpec((SC_WINDOW, SC_ROW), index_map=lambda i: (i, 0)),
                      pl.BlockSpec((1, SC_WINDOW), index_map=lambda i: (0, i)),
                      pl.BlockSpec((1, SC_WINDOW), index_map=lambda i: (0, i))],
            out_specs=[],
            core_axis_name=("core", "subcore"),
            dimension_semantics=(pltpu.PARALLEL,),
        )(x_hbm, i0_hbm, i1_hbm)

    return kern(x, idx0.reshape(1, ni), idx1.reshape(1, ni))


def _moe_ffn_body(be_ref, bn_ref, xs_ref, wg_ref, wu_ref, wd_ref, ys_ref, xb, act0, act1, acc):
    i = pl.program_id(0)
    f = pl.program_id(1)
    nf = pl.num_programs(1) - 1
    nvalid = bn_ref[i]
    sub = xs_ref.shape[1] // MOE_SUB

    def emit(start, size, cond):
        rs = pl.ds(start, size)

        def gate_up():
            x = xb[rs, :]
            gate = jnp.dot(x, wg_ref[...].astype(bf16), preferred_element_type=f32)
            up = jnp.dot(x, wu_ref[...].astype(bf16), preferred_element_type=f32)
            return (jax.nn.silu(gate) * up).astype(bf16)

        def down(act_ref):
            return jnp.dot(act_ref[rs, :], wd_ref[...].astype(bf16), preferred_element_type=f32)

        @pl.when(cond & (f == 0))
        def _():
            live = lax.broadcasted_iota(i32, (size, SC_ROW), 0) + start < nvalid
            hi0, lo0 = _unpack_bf16_pairs(jnp.where(live, xs_ref[0, rs, :], jnp.uint32(0)))
            hi1, lo1 = _unpack_bf16_pairs(jnp.where(live, xs_ref[1, rs, :], jnp.uint32(0)))
            xb[rs, :] = jnp.concatenate([hi0, hi1, lo0, lo1], axis=1)
            acc[rs, :] = jnp.zeros((size, D_MODEL), f32)
            act0[rs, :] = gate_up()

        @pl.when(cond & (f > 0) & (f < nf) & (f % 2 == 1))
        def _():
            act1[rs, :] = gate_up()
            acc[rs, :] += down(act0)

        @pl.when(cond & (f > 0) & (f < nf) & (f % 2 == 0))
        def _():
            act0[rs, :] = gate_up()
            acc[rs, :] += down(act1)

        @pl.when(cond & (f == nf))
        def _():
            last = act0 if (D_FF_EXPERT // TF_MOE - 1) % 2 == 0 else act1
            packed = _pack_bf16_pairs(acc[rs, :] + down(last))
            ys_ref[0, rs, :] = packed[:, :SC_ROW]
            ys_ref[1, rs, :] = packed[:, SC_ROW:]

    def zero_out(start, size, cond):
        @pl.when(cond & (f == nf))
        def _():
            ys_ref[:, pl.ds(start, size), :] = jnp.zeros((2, size, SC_ROW), u32)

    for h in range(MOE_SUB):
        live = nvalid - h * sub
        whole = live > sub // 2
        head = (live > 0) & (live <= sub // 2)
        emit(h * sub, sub, whole)
        emit(h * sub, sub // 2, head)
        zero_out(h * sub + sub // 2, sub // 2, head)
        zero_out(h * sub, sub, live <= 0)


def _moe_ffn(i_moe, blk_e, blk_n, xs, wg, wu, wd):
    npad = xs.shape[1]
    tm, tf = TM_MOE, TF_MOE
    nblk = npad // tm
    nf = D_FF_EXPERT // tf

    def nxt(i):
        return jnp.minimum(i + 1, nblk - 1)

    def x_idx(i, f, be, bn):
        return (0, jnp.where(f == nf, nxt(i), i), 0)

    def gu_idx(i, f, be, bn):
        ahead = (f == nf) & (bn[i] > 0) & (bn[nxt(i)] > 0)
        e = jnp.where(ahead, be[nxt(i)], be[i])
        t = jnp.where(ahead, 0, jnp.where(bn[i] > 0, jnp.minimum(f, nf - 1), nf - 1))
        return (i_moe, e, 0, t)

    def d_idx(i, f, be, bn):
        prev = jnp.maximum(i - 1, 0)
        keep = (f == 0) & (i > 0)
        e = jnp.where(keep, be[prev], be[i])
        t = jnp.where(keep | (bn[i] == 0), nf - 1, jnp.maximum(f - 1, 0))
        return (i_moe, e, t, 0)

    grid_spec = pltpu.PrefetchScalarGridSpec(
        num_scalar_prefetch=2,
        grid=(nblk, nf + 1),
        in_specs=[pl.BlockSpec((2, tm, SC_ROW), x_idx),
                  pl.BlockSpec((None, None, D_MODEL, tf), gu_idx),
                  pl.BlockSpec((None, None, D_MODEL, tf), gu_idx),
                  pl.BlockSpec((None, None, tf, D_MODEL), d_idx)],
        out_specs=pl.BlockSpec((2, tm, SC_ROW), lambda i, f, be, bn: (0, i, 0)),
        scratch_shapes=[pltpu.VMEM((tm, D_MODEL), bf16), pltpu.VMEM((tm, tf), bf16),
                        pltpu.VMEM((tm, tf), bf16), pltpu.VMEM((tm, D_MODEL), f32)],
    )
    return pl.pallas_call(
        _moe_ffn_body,
        grid_spec=grid_spec,
        out_shape=SDS((2, npad, SC_ROW), u32),
        compiler_params=_cparams("parallel", "arbitrary"),
        name="moe_ffn",
    )(blk_e, blk_n, xs, wg, wu, wd)


def _combine_body(x_ref, yg_ref, wcol_ref, g_ref, o_ref, *, final_norm):
    w1 = wcol_ref[:, 0:1]
    w2 = wcol_ref[:, 1:2]

    def rows(k):
        hi0, lo0 = _unpack_bf16_pairs(yg_ref[0, k])
        hi1, lo1 = _unpack_bf16_pairs(yg_ref[1, k])
        return jnp.concatenate([hi0, hi1, lo0, lo1], axis=1).astype(f32)

    xo = x_ref[...] + (w1 * rows(0) + w2 * rows(1))
    if final_norm:
        xo = _rms(xo, g_ref[...])
    o_ref[...] = xo


def _combine(x2, yg, wcol, g_final, final_norm):
    n = x2.shape[0]
    tm = TM_ROUTE
    row = lambda i: (i, 0)
    return pl.pallas_call(
        functools.partial(_combine_body, final_norm=final_norm),
        grid=(n // tm,),
        in_specs=[pl.BlockSpec((tm, D_MODEL), row),
                  pl.BlockSpec((2, 2, tm, SC_ROW), lambda i: (0, 0, i, 0)),
                  pl.BlockSpec((tm, LANE), row),
                  pl.BlockSpec((1, D_MODEL), lambda i: (0, 0))],
        out_specs=pl.BlockSpec((tm, D_MODEL), row),
        out_shape=SDS((n, D_MODEL), f32),
        compiler_params=_cparams("parallel"),
        name="moe_combine",
    )(x2, yg, wcol, g_final)


def _moe_layer(l, i_moe, mix_args, g_ffn, w_router, wg, wu, wd, g_final, final_norm):
    n = mix_args[0].shape[0]
    tm = TM_MOE
    npad = 2 * n + N_EXPERTS * tm
    wr = jnp.zeros((2 * SUBLANE, D_MODEL), f32).at[:N_EXPERTS].set(w_router[i_moe].astype(f32).T)
    wr_hi = wr.astype(bf16)
    wr_lo = (wr - wr_hi.astype(f32)).astype(bf16)
    x2, hp, meta, wcol, cnt = _mix_router(l, *mix_args, g_ffn, wr_hi, wr_lo)

    counts = cnt[:N_EXPERTS, 0].astype(i32)
    padded = ((counts + tm - 1) // tm) * tm
    ends = jnp.cumsum(padded)
    offs = ends - padded
    ids = jnp.arange(N_EXPERTS, dtype=i32)
    e12 = meta[0:2].astype(i32)
    r12 = meta[2:4].astype(i32)
    pos_t = r12 + jnp.sum(jnp.where(e12[..., None] == ids, offs, 0), axis=-1)
    blk_start = jnp.arange(npad // tm, dtype=i32) * tm
    blk_e = jnp.minimum(jnp.sum((blk_start[:, None] >= ends[None, :]).astype(i32), axis=1), N_EXPERTS - 1)
    own = blk_e[:, None] == ids
    blk_end = jnp.sum(jnp.where(own, offs + counts, 0), axis=1)
    blk_n = jnp.where(blk_start < ends[-1], jnp.clip(blk_end - blk_start, 0, tm), 0)
    last_e = jnp.max(jnp.where(blk_n > 0, blk_e, 0))
    blk_e = jnp.where(blk_n > 0, blk_e, last_e)

    xs = _sc_scatter2(hp.reshape(2 * n, SC_ROW),
                      jnp.concatenate([pos_t[0], pos_t[0] + npad]),
                      jnp.concatenate([pos_t[1], pos_t[1] + npad]),
                      2 * npad).reshape(2, npad, SC_ROW)
    ys = _moe_ffn(i_moe, blk_e, blk_n, xs, wg, wu, wd)
    gidx = jnp.concatenate([pos_t.reshape(-1), pos_t.reshape(-1) + npad])
    yg = _sc_gather(ys.reshape(2 * npad, SC_ROW), gidx).reshape(2, 2, n, SC_ROW)
    return _combine(x2, yg, wcol, g_final, final_norm)


def _rows(v):
    return v.astype(f32)[:, None, :]


def kernel(x, norm_mix, w_in, s5_lambda_re, s5_lambda_im, s5_log_dt, s5_b_re, s5_b_im, s5_c_re, s5_c_im, s5_d, s5_w_glu, s5_b_glu, s5_out_norm, gla_w_a2, gla_b_a2, gla_out_norm, w_out, norm_ffn, ffn_w_gate, ffn_w_up, ffn_w_down, moe_w_router, moe_w_gate, moe_w_up, moe_w_down, norm_final):
    bsz, seq, _ = x.shape
    n = bsz * seq
    depth = w_in.shape[0]
    x2 = x.reshape(n, D_MODEL)

    w_gp = jnp.zeros((depth, D_MODEL, LANE), f32).at[:, :, :GLA_GATE_RANK].set(w_in[:, :, 2048:2064])
    w_cat = jnp.concatenate([w_in[:, :, 0:512], w_in[:, :, 512:768] * (GLA_DK ** -0.5),
                             w_in[:, :, 1024:1536], w_in[:, :, 1536:2048], w_gp], axis=2).astype(bf16)
    w_kt = jnp.swapaxes(w_in[:, :, 768:1024], 1, 2).astype(bf16)
    s5_mats = jax.vmap(_s5_prep)(s5_lambda_re, s5_lambda_im, s5_log_dt, s5_b_re, s5_b_im, s5_c_re, s5_c_im)
    wa = jnp.zeros((depth, LANE, D_GLA_K), f32).at[:, :GLA_GATE_RANK].set(gla_w_a2).astype(bf16)
    g_mix, g_ffn, d_skip = _rows(norm_mix), _rows(norm_ffn), _rows(s5_d)
    ba, gn = _rows(gla_b_a2), _rows(gla_out_norm)
    mix_w = (s5_w_glu.astype(bf16), _rows(s5_b_glu), _rows(s5_out_norm), w_out.astype(bf16))
    ffn_w = (ffn_w_gate.astype(bf16), ffn_w_up.astype(bf16), ffn_w_down.astype(bf16))
    g_final = norm_final.astype(f32).reshape(1, D_MODEL)

    for l in range(depth):
        u, q, kt, v, r, gl = _inproj(l, x2, g_mix, w_cat, w_kt, bsz, seq)
        ys = _s5(l, u.reshape(bsz, seq, D_S5), *s5_mats, d_skip)
        yg = _gla(l, q.reshape(bsz, seq, D_GLA_K), kt, v.reshape(bsz, seq, D_GLA),
                  r.reshape(bsz, seq, D_GLA), gl.reshape(bsz, seq, LANE), wa, ba, gn)
        mix_args = (x2, ys.reshape(n, D_S5), yg.reshape(n, D_GLA)) + mix_w

        last = l == depth - 1
        if l % 2 == 0:
            x2 = _mix_ffn(l, l // 2, *mix_args, g_ffn, *ffn_w)
            if last:
                x2 = _final_norm(x2, g_final)
        else:
            x2 = _moe_layer(l, l // 2, mix_args, g_ffn, moe_w_router, moe_w_gate, moe_w_up,
                            moe_w_down, g_final, last)
    return x2.reshape(bsz, seq, D_MODEL)


def _final_norm_body(x_ref, g_ref, o_ref):
    o_ref[...] = _rms(x_ref[...], g_ref[...])


def _final_norm(x2, g):
    n = x2.shape[0]
    tm = TM_ROUTE
    return pl.pallas_call(
        _final_norm_body,
        grid=(n // tm,),
        in_specs=[pl.BlockSpec((tm, D_MODEL), lambda i: (i, 0)),
                  pl.BlockSpec((1, D_MODEL), lambda i: (0, 0))],
        out_specs=pl.BlockSpec((tm, D_MODEL), lambda i: (i, 0)),
        out_shape=SDS((n, D_MODEL), f32),
        compiler_params=_cparams("parallel"),
        name="final_norm",
    )(x2, g)
```

```python
import functools

import jax
import jax.numpy as jnp
from jax import lax
from jax.experimental import pallas as pl
from jax.experimental.pallas import tpu as pltpu
from jax.experimental.pallas import tpu_sc as plsc

f32 = jnp.float32
bf16 = jnp.bfloat16
u32 = jnp.uint32
i32 = jnp.int32
SDS = jax.ShapeDtypeStruct

D_MODEL = 1024
D_S5 = 512
S5_GROUP = 16
S5_GROUPS = 32
S5_STATE = 64
N_STATE = S5_GROUPS * S5_STATE
D_GLA = 512
GLA_HEADS = 4
GLA_DV = 128
GLA_DK = 64
D_GLA_K = 256
GLA_GATE_RANK = 16
GLA_TAU = 16.0
GLA_CHUNK = 64
D_FF = 2816
N_EXPERTS = 8
D_FF_EXPERT = 3584
EPS = 1e-6

LANE = 128
SUBLANE = 8
VMEM_LIMIT = 56 * 1024 * 1024

TM_PROJ = 2048
TT_S5 = 256
S5_SLABS = 4
S5_PITCH_PAD = 8
TG_GLA = 1024
GLA_SUB = 256
TM_FFN = 1024
FFN_CHUNKS = ((0, 1536), (1536, 2816))
TM_MOE = 2048
MOE_SUB = 2
TF_MOE = 512
TM_ROUTE = 1024
SC_WINDOW = 128
SC_ROW = 256


def _cparams(*sem):
    return pltpu.CompilerParams(dimension_semantics=sem, vmem_limit_bytes=VMEM_LIMIT)


def _layer(arr, l, **kw):
    return pl.BlockSpec((None,) + arr.shape[1:], lambda *_: (l,) + (0,) * (arr.ndim - 1), **kw)


def _rms(x, g):
    ms = jnp.mean(x * x, axis=-1, keepdims=True)
    return x * lax.rsqrt(ms + EPS) * g


def _inproj_body(x_ref, g_ref, w_ref, wkt_ref, u_ref, q_ref, kt_ref, v_ref, r_ref, gl_ref):
    hn = _rms(x_ref[...], g_ref[...]).astype(bf16)

    def proj(lo, hi):
        return jnp.dot(hn, w_ref[:, lo:hi], preferred_element_type=f32).astype(bf16)

    u_ref[...] = proj(0, 512)
    q_ref[...] = proj(512, 768)
    v_ref[...] = proj(768, 1280)
    r_ref[...] = proj(1280, 1792)
    gl_ref[...] = proj(1792, 1920)
    kt_ref[...] = lax.dot_general(wkt_ref[...], hn, (((1,), (1,)), ((), ())),
                                  preferred_element_type=f32).astype(bf16)


def _inproj(l, x2, g, w_cat, w_kt, bsz, seq):
    n = x2.shape[0]
    tm = TM_PROJ
    per_b = seq // tm
    row = lambda i: (i, 0)
    return pl.pallas_call(
        _inproj_body,
        grid=(n // tm,),
        in_specs=[pl.BlockSpec((tm, D_MODEL), row),
                  _layer(g, l), _layer(w_cat, l), _layer(w_kt, l)],
        out_specs=[pl.BlockSpec((tm, D_S5), row),
                   pl.BlockSpec((tm, D_GLA_K), row),
                   pl.BlockSpec((None, D_GLA_K, tm), lambda i: (i // per_b, 0, i % per_b)),
                   pl.BlockSpec((tm, D_GLA), row),
                   pl.BlockSpec((tm, D_GLA), row),
                   pl.BlockSpec((tm, LANE), row)],
        out_shape=[SDS((n, D_S5), bf16), SDS((n, D_GLA_K), bf16),
                   SDS((bsz, D_GLA_K, seq), bf16), SDS((n, D_GLA), bf16),
                   SDS((n, D_GLA), bf16), SDS((n, LANE), bf16)],
        compiler_params=_cparams("parallel"),
        name="inproj",
    )(x2, g, w_cat, w_kt)


N_SLAB = N_STATE // LANE


def _s5_prep(lam_re, lam_im, log_dt, b_re, b_im, c_re, c_im):
    lr = jnp.minimum(lam_re.astype(f32), -1e-4)
    li = lam_im.astype(f32)
    dt = jnp.exp(log_dt.astype(f32))[:, None]
    mag = jnp.exp(lr * dt)
    ab_re = mag * jnp.cos(li * dt)
    ab_im = mag * jnp.sin(li * dt)
    nr = ab_re - 1.0
    ni = ab_im
    den = lr * lr + li * li
    f_re = (nr * lr + ni * li) / den
    f_im = (ni * lr - nr * li) / den
    br = b_re.astype(f32)
    bi = b_im.astype(f32)
    bb_re = f_re[..., None] * br - f_im[..., None] * bi
    bb_im = f_re[..., None] * bi + f_im[..., None] * br
    abb_re = ab_re[..., None] * bb_re - ab_im[..., None] * bb_im
    abb_im = ab_re[..., None] * bb_im + ab_im[..., None] * bb_re
    cr = c_re.astype(f32)
    ci = c_im.astype(f32)
    ca_re = cr * ab_re[:, None, :] - ci * ab_im[:, None, :]
    ca_im = cr * ab_im[:, None, :] + ci * ab_re[:, None, :]
    cb = jnp.einsum('gop,gpc->goc', cr, bb_re) - jnp.einsum('gop,gpc->goc', ci, bb_im)

    bb4 = jnp.stack([abb_re, bb_re, abb_im, bb_im]).astype(bf16)
    bt = jnp.transpose(bb4, (0, 1, 3, 2)).reshape(4, 8, 4, S5_GROUP, S5_STATE)
    gl = jnp.arange(8)[None, :, None]
    gs = jnp.arange(4)[None, None, :]
    nn = jnp.arange(8)[:, None, None]
    sel = (gl == 4 * (nn % 2) + gs).astype(bf16)
    t = jnp.einsum('ngs,qnscp->qngcsp', sel, bt, preferred_element_type=bf16)
    t = t.reshape(2, 2, 8, LANE, 4 * S5_STATE)
    bmat = jnp.transpose(t, (0, 2, 1, 3, 4)).reshape(16, 2 * LANE, 4 * S5_STATE)

    c4 = jnp.stack([cr, -ci, ca_re, -ca_im]).astype(bf16)
    ct = jnp.transpose(c4, (0, 1, 3, 2)).reshape(4, 2, 16, S5_STATE, S5_GROUP)
    eye = jnp.eye(16, dtype=bf16)
    cmat = jnp.einsum('gh,qjgpo->qjgpho', eye, ct, preferred_element_type=bf16)
    cmat = cmat.reshape(4, 2, 16 * S5_STATE, 16 * S5_GROUP)
    cbt = jnp.transpose(cb, (0, 2, 1)).reshape(2, 16, S5_GROUP, S5_GROUP)
    cbmat = jnp.einsum('gh,jgco->jgcho', jnp.eye(16, dtype=f32), cbt).reshape(2, 256, 256).astype(bf16)
    a2_re = (ab_re * ab_re - ab_im * ab_im).reshape(N_SLAB, 1, LANE)
    a2_im = (2.0 * ab_re * ab_im).reshape(N_SLAB, 1, LANE)
    return bmat, cmat, cbmat, a2_re, a2_im


def _s5_body(u_ref, bm_ref, cm_ref, cbm_ref, are_ref, aim_ref, d_ref, y_ref,
             wbuf, hbuf, obuf, hstate, zcarry):
    tt = u_ref.shape[1]
    tp = tt // 2
    rows = SUBLANE * tp
    pitch = tp + S5_PITCH_PAD
    nl = D_S5 // LANE

    @pl.when(pl.program_id(0) == 0)
    def _():
        hstate[...] = jnp.zeros_like(hstate)
        zcarry[...] = jnp.zeros_like(zcarry)

    w = pltpu.bitcast(u_ref[...].reshape(SUBLANE * tt, D_S5), u32)
    for b in range(SUBLANE):
        for k in range(nl):
            wbuf[k, b * pitch:b * pitch + tp, :] = w[b * tp:(b + 1) * tp, k * LANE:(k + 1) * LANE]
    w = jnp.concatenate(
        [jnp.concatenate([wbuf[k, pl.ds(m, SUBLANE, stride=pitch), :] for m in range(tp)], axis=0)
         for k in range(nl)], axis=1)
    ue_f = pltpu.bitcast(w << 16, f32)
    uo_f = pltpu.bitcast(w & jnp.uint32(0xFFFF0000), f32)
    ue = ue_f.astype(bf16)
    uo = uo_f.astype(bf16)

    for n in range(2 * 8):
        ks = LANE * ((n % 8) // 2)
        lhs = jnp.concatenate([ue[:, ks:ks + LANE], uo[:, ks:ks + LANE]], axis=1)
        res = jnp.dot(lhs, bm_ref[n], preferred_element_type=f32)
        hbuf[2 * n] = res[:, :LANE]
        hbuf[2 * n + 1] = res[:, LANE:]

    for c0 in range(0, N_SLAB, S5_SLABS):
        ar = [jnp.broadcast_to(are_ref[c0 + s], (SUBLANE, LANE)) for s in range(S5_SLABS)]
        ai = [jnp.broadcast_to(aim_ref[c0 + s], (SUBLANE, LANE)) for s in range(S5_SLABS)]

        def step(t, carry):
            out = []
            sl = pl.ds(pl.multiple_of(t * SUBLANE, SUBLANE), SUBLANE)
            for s in range(S5_SLABS):
                hr, hi = carry[2 * s], carry[2 * s + 1]
                nr = ar[s] * hr - ai[s] * hi + hbuf[c0 + s, sl, :]
                ni = ar[s] * hi + ai[s] * hr + hbuf[N_SLAB + c0 + s, sl, :]
                hbuf[c0 + s, sl, :] = nr
                hbuf[N_SLAB + c0 + s, sl, :] = ni
                out += [nr, ni]
            return tuple(out)

        init = []
        for s in range(S5_SLABS):
            init += [hstate[c0 + s], hstate[N_SLAB + c0 + s]]
        fin = lax.fori_loop(0, tp, step, tuple(init), unroll=4)
        for s in range(S5_SLABS):
            hstate[c0 + s] = fin[2 * s]
            hstate[N_SLAB + c0 + s] = fin[2 * s + 1]

    for j in range(2):
        h_re = jnp.concatenate([hbuf[8 * j + s] for s in range(8)], axis=1).astype(bf16)
        h_im = jnp.concatenate([hbuf[N_SLAB + 8 * j + s] for s in range(8)], axis=1).astype(bf16)
        cs = slice(256 * j, 256 * j + 256)
        dj = d_ref[:, cs]
        yo = jnp.dot(h_re, cm_ref[0, j], preferred_element_type=f32)
        yo = yo + jnp.dot(h_im, cm_ref[1, j], preferred_element_type=f32)
        yo = yo + dj * uo_f[:, cs]
        z = jnp.dot(h_re, cm_ref[2, j], preferred_element_type=f32)
        z = z + jnp.dot(h_im, cm_ref[3, j], preferred_element_type=f32)
        zs = jnp.concatenate([zcarry[:, cs], z[:rows - SUBLANE, :]], axis=0)
        zcarry[:, cs] = z[rows - SUBLANE:, :]
        ye = zs + jnp.dot(ue[:, cs], cbm_ref[j], preferred_element_type=f32) + dj * ue_f[:, cs]
        ge = pltpu.bitcast(jax.nn.gelu(ye).astype(bf16).astype(f32), u32)
        go = pltpu.bitcast(jax.nn.gelu(yo).astype(bf16).astype(f32), u32)
        packed = (ge >> 16) | go
        obuf[2 * j] = packed[:, :LANE]
        obuf[2 * j + 1] = packed[:, LANE:]

    for b in range(SUBLANE):
        yb = jnp.concatenate(
            [jnp.concatenate([obuf[k, pl.ds(8 * SUBLANE * i + b, SUBLANE, stride=SUBLANE), :]
                              for i in range(tp // SUBLANE)], axis=0) for k in range(nl)], axis=1)
        y_ref[b] = pltpu.bitcast(yb, bf16)


def _s5(l, u3, bmat, cmat, cbmat, a2_re, a2_im, d_skip):
    bsz, seq, _ = u3.shape
    assert bsz == SUBLANE
    tt = TT_S5
    blk = lambda i: (0, i, 0)
    return pl.pallas_call(
        _s5_body,
        grid=(seq // tt,),
        in_specs=[pl.BlockSpec((bsz, tt, D_S5), blk)]
        + [_layer(a, l) for a in (bmat, cmat, cbmat, a2_re, a2_im, d_skip)],
        out_specs=pl.BlockSpec((bsz, tt, D_S5), blk),
        out_shape=SDS((bsz, seq, D_S5), bf16),
        scratch_shapes=[pltpu.VMEM((D_S5 // LANE, bsz * (tt // 2 + S5_PITCH_PAD), LANE), u32),
                        pltpu.VMEM((2 * N_SLAB, bsz * (tt // 2), LANE), f32),
                        pltpu.VMEM((D_S5 // LANE, bsz * (tt // 2), LANE), u32),
                        pltpu.VMEM((2 * N_SLAB, SUBLANE, LANE), f32),
                        pltpu.VMEM((SUBLANE, D_S5), f32)],
        compiler_params=_cparams("arbitrary"),
        name="s5_scan",
    )(u3, bmat, cmat, cbmat, a2_re, a2_im, d_skip)


def _split2(x):
    hi = x.astype(bf16)
    lo = (x - hi.astype(f32)).astype(bf16)
    return hi, lo


def _log_sigmoid(x):
    return -(jnp.maximum(-x, 0.0) + jnp.log1p(jnp.exp(-jnp.abs(x))))


def _gla_body(q_ref, kt_ref, v_ref, r_ref, g_ref, wa_ref, ba_ref, gn_ref,
              o_ref, s_ref):
    c = GLA_CHUNK
    tg = q_ref.shape[0]

    @pl.when(pl.program_id(1) == 0)
    def _():
        s_ref[...] = jnp.zeros_like(s_ref)

    ts = GLA_SUB
    nc = ts // c
    nh = GLA_HEADS
    ri = lax.broadcasted_iota(i32, (ts, ts), 0)
    ci = lax.broadcasted_iota(i32, (ts, ts), 1)
    same = (ri // c) == (ci // c)
    tril = (same & (ri >= ci)).astype(bf16)
    head_of_lane = lax.broadcasted_iota(i32, (c, D_GLA_K), 1) // GLA_DK
    r_idx = lax.broadcasted_iota(i32, (nc * nh * c, ts), 0)
    c_idx = lax.broadcasted_iota(i32, (nc * nh * c, ts), 1)
    causal = ((r_idx // c) % nc == c_idx // c) & (r_idx % c >= c_idx % c)
    chunk_of_lane = lax.broadcasted_iota(i32, (D_GLA_K, ts), 1) // c
    chunk_of_lane_h = lax.broadcasted_iota(i32, (GLA_DK, ts), 1) // c

    nsub = tg // ts
    subs = [slice(st * ts, (st + 1) * ts) for st in range(nsub)]

    def decay_stage(rs):
        g = g_ref[rs, :]
        la = _log_sigmoid(jnp.dot(g, wa_ref[...], preferred_element_type=f32) + ba_ref[...]) / GLA_TAU
        la_hi, la_lo = _split2(la)
        cum = (jnp.dot(tril, la_hi, preferred_element_type=f32)
               + jnp.dot(tril, la_lo, preferred_element_type=f32))
        cumt = cum.T
        ends = [cumt[:, (cc + 1) * c - 1:(cc + 1) * c] for cc in range(nc)]
        clt = ends[nc - 1]
        for cc in range(nc - 2, -1, -1):
            clt = jnp.where(chunk_of_lane <= cc, ends[cc], clt)
        qt = q_ref[rs, :].astype(f32) * jnp.exp(cum)
        ktt = kt_ref[:, rs].astype(f32)
        k_t = (ktt * jnp.exp(-cumt)).astype(bf16)
        k_end = ktt * jnp.exp(clt - cumt)
        q_stack = jnp.concatenate(
            [jnp.where(head_of_lane == h, qt[cc * c:(cc + 1) * c, :], 0.0)
             for h in range(nh) for cc in range(nc)], axis=0).astype(bf16)
        k_stack = [jnp.concatenate(
            [jnp.where(chunk_of_lane_h == cc, k_end[h * GLA_DK:(h + 1) * GLA_DK, :], 0.0)
             for cc in range(nc)], axis=0).astype(bf16) for h in range(nh)]
        dec = [jnp.exp(clt[:, cc * c:cc * c + 1]) for cc in range(nc)]
        return q_stack, k_t, k_stack, dec

    def matmul_stage(rs, staged, state):
        q_stack, k_t, k_stack, dec = staged
        scores = jnp.dot(q_stack, k_t, preferred_element_type=f32)
        scores = jnp.where(causal, scores, 0.0).astype(bf16)
        o_intra, upd_h = [], []
        for h in range(nh):
            vh = v_ref[rs, h * GLA_DV:(h + 1) * GLA_DV]
            o_intra.append(jnp.dot(scores[h * nc * c:(h + 1) * nc * c, :], vh,
                                   preferred_element_type=f32))
            upd_h.append(jnp.dot(k_stack[h], vh, preferred_element_type=f32))
        o_inter = []
        for cc in range(nc):
            q_cc = jnp.concatenate([q_stack[(h * nc + cc) * c:(h * nc + cc + 1) * c, :] for h in range(nh)],
                                   axis=0)
            o_inter.append(jnp.dot(q_cc, state.astype(bf16), preferred_element_type=f32))
            upd = jnp.concatenate([upd_h[h][cc * GLA_DK:(cc + 1) * GLA_DK, :] for h in range(nh)], axis=0)
            state = dec[cc] * state + upd
        return o_intra, o_inter, state

    def output_stage(rs, o_intra, o_inter):
        rows = []
        for cc in range(nc):
            outs = []
            for h in range(nh):
                o = o_intra[h][cc * c:(cc + 1) * c, :] + o_inter[cc][h * c:(h + 1) * c, :]
                o = o * lax.rsqrt(jnp.mean(o * o, axis=-1, keepdims=True) + EPS)
                outs.append(o)
            rows.append(jnp.concatenate(outs, axis=1))
        o_cat = jnp.concatenate(rows, axis=0)
        o_ref[rs, :] = (o_cat * gn_ref[...] * jax.nn.silu(r_ref[rs, :].astype(f32))).astype(bf16)

    state = s_ref[...]
    staged = decay_stage(subs[0])
    for st in range(nsub):
        nxt = decay_stage(subs[st + 1]) if st + 1 < nsub else None
        o_full, o_inter, state = matmul_stage(subs[st], staged, state)
        output_stage(subs[st], o_full, o_inter)
        staged = nxt
    s_ref[...] = state


def _gla(l, q3, kt3, v3, r3, g3, wa, ba, gn):
    bsz, seq, _ = q3.shape
    tg = TG_GLA
    tok = lambda b, i: (b, i, 0)
    return pl.pallas_call(
        _gla_body,
        grid=(bsz, seq // tg),
        in_specs=[pl.BlockSpec((None, tg, D_GLA_K), tok),
                  pl.BlockSpec((None, D_GLA_K, tg), lambda b, i: (b, 0, i)),
                  pl.BlockSpec((None, tg, D_GLA), tok),
                  pl.BlockSpec((None, tg, D_GLA), tok),
                  pl.BlockSpec((None, tg, LANE), tok),
                  _layer(wa, l), _layer(ba, l), _layer(gn, l)],
        out_specs=pl.BlockSpec((None, tg, D_GLA), tok),
        out_shape=SDS((bsz, seq, D_GLA), bf16),
        scratch_shapes=[pltpu.VMEM((D_GLA_K, GLA_DV), f32)],
        compiler_params=_cparams("parallel", "arbitrary"),
        name="gla",
    )(q3, kt3, v3, r3, g3, wa, ba, gn)


def _mix_out(x_ref, ys_ref, yg_ref, wgl_ref, bg_ref, gs_ref, wo_ref):
    y = ys_ref[...]
    z = jnp.dot(y, wgl_ref[...], preferred_element_type=f32) + bg_ref[...]
    yf = y.astype(f32) * jax.nn.sigmoid(z)
    ys = _rms(yf, gs_ref[...]).astype(bf16)
    acc = jnp.dot(ys, wo_ref[0:D_S5, :], preferred_element_type=f32)
    acc = acc + jnp.dot(yg_ref[...], wo_ref[D_S5:, :], preferred_element_type=f32)
    return x_ref[...] + acc


def _mix_specs(l, tm, w_glu, b_glu, g_s5, w_out):
    row = lambda i: (i, 0)
    once = dict(pipeline_mode=pl.Buffered(1))
    return [pl.BlockSpec((tm, D_MODEL), row),
            pl.BlockSpec((tm, D_S5), row),
            pl.BlockSpec((tm, D_GLA), row),
            _layer(w_glu, l, **once), _layer(b_glu, l), _layer(g_s5, l), _layer(w_out, l, **once)]


def _mix_ffn_body(x_ref, ys_ref, yg_ref, wgl_ref, bg_ref, gs_ref, wo_ref,
                  g_ref, wg_ref, wu_ref, wd_ref, o_ref):
    x = _mix_out(x_ref, ys_ref, yg_ref, wgl_ref, bg_ref, gs_ref, wo_ref)
    hn = _rms(x, g_ref[...]).astype(bf16)
    acc = x
    for lo, hi in FFN_CHUNKS:
        gate = jnp.dot(hn, wg_ref[:, lo:hi], preferred_element_type=f32)
        up = jnp.dot(hn, wu_ref[:, lo:hi], preferred_element_type=f32)
        act = (jax.nn.silu(gate) * up).astype(bf16)
        acc = acc + jnp.dot(act, wd_ref[lo:hi, :], preferred_element_type=f32)
    o_ref[...] = acc


def _mix_ffn(l, i_ffn, x2, ys, yg, w_glu, b_glu, g_s5, w_out, g, wg, wu, wd):
    n = x2.shape[0]
    tm = TM_FFN
    row = lambda i: (i, 0)
    once = dict(pipeline_mode=pl.Buffered(1))
    return pl.pallas_call(
        _mix_ffn_body,
        grid=(n // tm,),
        in_specs=_mix_specs(l, tm, w_glu, b_glu, g_s5, w_out) + [
            _layer(g, l), _layer(wg, i_ffn, **once), _layer(wu, i_ffn, **once),
            _layer(wd, i_ffn, **once)],
        out_specs=pl.BlockSpec((tm, D_MODEL), row),
        out_shape=SDS((n, D_MODEL), f32),
        compiler_params=_cparams("parallel"),
        name="mix_ffn",
    )(x2, ys, yg, w_glu, b_glu, g_s5, w_out, g, wg, wu, wd)


def _pack_bf16_pairs(a):
    bits = pltpu.bitcast(a.astype(bf16).astype(f32), u32)
    half = a.shape[1] // 2
    return bits[:, :half] | (bits[:, half:] >> 16)


def _unpack_bf16_pairs(p):
    hi = pltpu.bitcast(p & jnp.uint32(0xFFFF0000), f32).astype(bf16)
    lo = pltpu.bitcast(p << 16, f32).astype(bf16)
    return hi, lo


def _mix_router_body(x_ref, ys_ref, yg_ref, wgl_ref, bg_ref, gs_ref, wo_ref,
                     g_ref, wh_ref, wl_ref, xo_ref, hp_ref, meta_ref, wcol_ref, cnt_ref, carry):
    tm = x_ref.shape[0]

    @pl.when(pl.program_id(0) == 0)
    def _():
        carry[...] = jnp.zeros_like(carry)

    x = _mix_out(x_ref, ys_ref, yg_ref, wgl_ref, bg_ref, gs_ref, wo_ref)
    xo_ref[...] = x
    hn = _rms(x, g_ref[...])
    packed = _pack_bf16_pairs(hn)
    hp_ref[0] = packed[:, :SC_ROW]
    hp_ref[1] = packed[:, SC_ROW:]

    h_hi, h_lo = _split2(hn)
    nt = (((1,), (1,)), ((), ()))
    logits = (lax.dot_general(wh_ref[...], h_hi, nt, preferred_element_type=f32)
              + lax.dot_general(wh_ref[...], h_lo, nt, preferred_element_type=f32)
              + lax.dot_general(wl_ref[...], h_hi, nt, preferred_element_type=f32))
    er = logits.shape[0]
    row = lax.broadcasted_iota(i32, (er, tm), 0)
    neg = jnp.float32(-jnp.inf)
    logits = jnp.where(row < N_EXPERTS, logits, neg)
    m1 = jnp.max(logits, axis=0, keepdims=True)
    i1 = jnp.min(jnp.where(logits == m1, row, er), axis=0, keepdims=True)
    l2 = jnp.where(row == i1, neg, logits)
    m2 = jnp.max(l2, axis=0, keepdims=True)
    i2 = jnp.min(jnp.where(l2 == m2, row, er), axis=0, keepdims=True)
    e21 = jnp.exp(m2 - m1)
    w1 = 1.0 / (1.0 + e21)
    w2 = e21 / (1.0 + e21)

    sel1 = row == i1
    sel2 = row == i2
    sel = (sel1 | sel2).astype(f32)
    ri = lax.broadcasted_iota(i32, (tm, tm), 0)
    ci = lax.broadcasted_iota(i32, (tm, tm), 1)
    triu = (ri <= ci).astype(bf16)
    incl = jnp.dot(sel.astype(bf16), triu, preferred_element_type=f32)
    rank = incl - sel + carry[:, 0:1]
    r1 = jnp.sum(jnp.where(sel1, rank, 0.0), axis=0, keepdims=True)
    r2 = jnp.sum(jnp.where(sel2, rank, 0.0), axis=0, keepdims=True)
    new_cnt = carry[:, 0:1] + incl[:, tm - 1:tm]
    carry[...] = jnp.broadcast_to(new_cnt, carry.shape)
    cnt_ref[...] = jnp.broadcast_to(new_cnt, cnt_ref.shape)

    srow = lax.broadcasted_iota(i32, (SUBLANE, tm), 0)
    meta = jnp.where(srow == 0, i1.astype(f32), 0.0)
    meta = jnp.where(srow == 1, i2.astype(f32), meta)
    meta = jnp.where(srow == 2, r1, meta)
    meta = jnp.where(srow == 3, r2, meta)
    meta_ref[...] = meta
    prow = lax.broadcasted_iota(i32, (LANE, tm), 0)
    wpad = jnp.where(prow == 0, w1, jnp.where(prow == 1, w2, 0.0))
    wcol_ref[...] = wpad.T


def _mix_router(l, x2, ys, yg, w_glu, b_glu, g_s5, w_out, g, w_hi, w_lo):
    n = x2.shape[0]
    tm = TM_ROUTE
    row = lambda i: (i, 0)
    const = lambda i: (0, 0)
    return pl.pallas_call(
        _mix_router_body,
        grid=(n // tm,),
        in_specs=_mix_specs(l, tm, w_glu, b_glu, g_s5, w_out) + [
            _layer(g, l),
            pl.BlockSpec(w_hi.shape, const),
            pl.BlockSpec(w_lo.shape, const)],
        out_specs=[pl.BlockSpec((tm, D_MODEL), row),
                   pl.BlockSpec((2, tm, SC_ROW), lambda i: (0, i, 0)),
                   pl.BlockSpec((SUBLANE, tm), lambda i: (0, i)),
                   pl.BlockSpec((tm, LANE), row),
                   pl.BlockSpec(w_hi.shape[:1] + (LANE,), const)],
        out_shape=[SDS((n, D_MODEL), f32), SDS((2, n, SC_ROW), u32), SDS((SUBLANE, n), f32),
                   SDS((n, LANE), f32), SDS(w_hi.shape[:1] + (LANE,), f32)],
        scratch_shapes=[pltpu.VMEM(w_hi.shape[:1] + (LANE,), f32)],
        compiler_params=_cparams("arbitrary"),
        name="mix_router",
    )(x2, ys, yg, w_glu, b_glu, g_s5, w_out, g, w_hi, w_lo)


def _sc_gather(table, idx):
    ni = idx.shape[0]
    mesh = plsc.VectorSubcoreMesh(core_axis_name="core", subcore_axis_name="subcore")
    idx2 = idx.reshape(1, ni)

    @pl.kernel(out_type=SDS((ni, SC_ROW), table.dtype), mesh=mesh)
    def kern(t_hbm, i_hbm, o_hbm):
        def body(i_vmem, o_vmem):
            pltpu.sync_copy(t_hbm.at[i_vmem.at[0]], o_vmem)

        pltpu.emit_pipeline(
            body, grid=(ni // SC_WINDOW,),
            in_specs=[pl.BlockSpec((1, SC_WINDOW), index_map=lambda i: (0, i))],
            out_specs=[pl.BlockSpec((SC_WINDOW, SC_ROW), index_map=lambda i: (i, 0))],
            core_axis_name=("core", "subcore"),
            dimension_semantics=(pltpu.PARALLEL,),
        )(i_hbm, o_hbm)

    return kern(table, idx2)


def _sc_scatter2(x, idx0, idx1, nrows):
    ni = x.shape[0]
    mesh = plsc.VectorSubcoreMesh(core_axis_name="core", subcore_axis_name="subcore")

    @pl.kernel(out_type=SDS((nrows, SC_ROW), x.dtype), mesh=mesh)
    def kern(x_hbm, i0_hbm, i1_hbm, o_hbm):
        def body(x_vmem, i0_vmem, i1_vmem):
            pltpu.sync_copy(x_vmem, o_hbm.at[i0_vmem.at[0]])
            pltpu.sync_copy(x_vmem, o_hbm.at[i1_vmem.at[0]])

        pltpu.emit_pipeline(
            body, grid=(ni // SC_WINDOW,),
            in_specs=[pl.BlockSpec((SC_WINDOW, SC_ROW), index_map=lambda i: (i, 0)),
                      pl.BlockSpec((1, SC_WINDOW), index_map=lambda i: (0, i)),
                      pl.BlockSpec((1, SC_WINDOW), index_map=lambda i: (0, i))],
            out_specs=[],
            core_axis_name=("core", "subcore"),
            dimension_semantics=(pltpu.PARALLEL,),
        )(x_hbm, i0_hbm, i1_hbm)

    return kern(x, idx0.reshape(1, ni), idx1.reshape(1, ni))


def _moe_ffn_body(be_ref, bn_ref, xs_ref, wg_ref, wu_ref, wd_ref, ys_ref, xb, act0, act1, acc):
    i = pl.program_id(0)
    f = pl.program_id(1)
    nf = pl.num_programs(1) - 1
    nvalid = bn_ref[i]
    sub = xs_ref.shape[1] // MOE_SUB

    def emit(start, size, cond):
        rs = pl.ds(start, size)

        def gate_up():
            x = xb[rs, :]
            gate = jnp.dot(x, wg_ref[...].astype(bf16), preferred_element_type=f32)
            up = jnp.dot(x, wu_ref[...].astype(bf16), preferred_element_type=f32)
            return (jax.nn.silu(gate) * up).astype(bf16)

        def down(act_ref):
            return jnp.dot(act_ref[rs, :], wd_ref[...].astype(bf16), preferred_element_type=f32)

        @pl.when(cond & (f == 0))
        def _():
            live = lax.broadcasted_iota(i32, (size, SC_ROW), 0) + start < nvalid
            hi0, lo0 = _unpack_bf16_pairs(jnp.where(live, xs_ref[0, rs, :], jnp.uint32(0)))
            hi1, lo1 = _unpack_bf16_pairs(jnp.where(live, xs_ref[1, rs, :], jnp.uint32(0)))
            xb[rs, :] = jnp.concatenate([hi0, hi1, lo0, lo1], axis=1)
            acc[rs, :] = jnp.zeros((size, D_MODEL), f32)
            act0[rs, :] = gate_up()

        @pl.when(cond & (f > 0) & (f < nf) & (f % 2 == 1))
        def _():
            act1[rs, :] = gate_up()
            acc[rs, :] += down(act0)

        @pl.when(cond & (f > 0) & (f < nf) & (f % 2 == 0))
        def _():
            act0[rs, :] = gate_up()
            acc[rs, :] += down(act1)

        @pl.when(cond & (f == nf))
        def _():
            last = act0 if (D_FF_EXPERT // TF_MOE - 1) % 2 == 0 else act1
            packed = _pack_bf16_pairs(acc[rs, :] + down(last))
            ys_ref[0, rs, :] = packed[:, :SC_ROW]
            ys_ref[1, rs, :] = packed[:, SC_ROW:]

    def zero_out(start, size, cond):
        @pl.when(cond & (f == nf))
        def _():
            ys_ref[:, pl.ds(start, size), :] = jnp.zeros((2, size, SC_ROW), u32)

    for h in range(MOE_SUB):
        live = nvalid - h * sub
        whole = live > sub // 2
        head = (live > 0) & (live <= sub // 2)
        emit(h * sub, sub, whole)
        emit(h * sub, sub // 2, head)
        zero_out(h * sub + sub // 2, sub // 2, head)
        zero_out(h * sub, sub, live <= 0)


def _moe_ffn(i_moe, blk_e, blk_n, xs, wg, wu, wd):
    npad = xs.shape[1]
    tm, tf = TM_MOE, TF_MOE
    nblk = npad // tm
    nf = D_FF_EXPERT // tf

    def nxt(i):
        return jnp.minimum(i + 1, nblk - 1)

    def x_idx(i, f, be, bn):
        return (0, jnp.where(f == nf, nxt(i), i), 0)

    def gu_idx(i, f, be, bn):
        ahead = (f == nf) & (bn[i] > 0) & (bn[nxt(i)] > 0)
        e = jnp.where(ahead, be[nxt(i)], be[i])
        t = jnp.where(ahead, 0, jnp.where(bn[i] > 0, jnp.minimum(f, nf - 1), nf - 1))
        return (i_moe, e, 0, t)

    def d_idx(i, f, be, bn):
        prev = jnp.maximum(i - 1, 0)
        keep = (f == 0) & (i > 0)
        e = jnp.where(keep, be[prev], be[i])
        t = jnp.where(keep | (bn[i] == 0), nf - 1, jnp.maximum(f - 1, 0))
        return (i_moe, e, t, 0)

    grid_spec = pltpu.PrefetchScalarGridSpec(
        num_scalar_prefetch=2,
        grid=(nblk, nf + 1),
        in_specs=[pl.BlockSpec((2, tm, SC_ROW), x_idx),
                  pl.BlockSpec((None, None, D_MODEL, tf), gu_idx),
                  pl.BlockSpec((None, None, D_MODEL, tf), gu_idx),
                  pl.BlockSpec((None, None, tf, D_MODEL), d_idx)],
        out_specs=pl.BlockSpec((2, tm, SC_ROW), lambda i, f, be, bn: (0, i, 0)),
        scratch_shapes=[pltpu.VMEM((tm, D_MODEL), bf16), pltpu.VMEM((tm, tf), bf16),
                        pltpu.VMEM((tm, tf), bf16), pltpu.VMEM((tm, D_MODEL), f32)],
    )
    return pl.pallas_call(
        _moe_ffn_body,
        grid_spec=grid_spec,
        out_shape=SDS((2, npad, SC_ROW), u32),
        compiler_params=_cparams("parallel", "arbitrary"),
        name="moe_ffn",
    )(blk_e, blk_n, xs, wg, wu, wd)


def _combine_body(x_ref, yg_ref, wcol_ref, g_ref, o_ref, *, final_norm):
    w1 = wcol_ref[:, 0:1]
    w2 = wcol_ref[:, 1:2]

    def rows(k):
        hi0, lo0 = _unpack_bf16_pairs(yg_ref[0, k])
        hi1, lo1 = _unpack_bf16_pairs(yg_ref[1, k])
        return jnp.concatenate([hi0, hi1, lo0, lo1], axis=1).astype(f32)

    xo = x_ref[...] + (w1 * rows(0) + w2 * rows(1))
    if final_norm:
        xo = _rms(xo, g_ref[...])
    o_ref[...] = xo


def _combine(x2, yg, wcol, g_final, final_norm):
    n = x2.shape[0]
    tm = TM_ROUTE
    row = lambda i: (i, 0)
    return pl.pallas_call(
        functools.partial(_combine_body, final_norm=final_norm),
        grid=(n // tm,),
        in_specs=[pl.BlockSpec((tm, D_MODEL), row),
                  pl.BlockSpec((2, 2, tm, SC_ROW), lambda i: (0, 0, i, 0)),
                  pl.BlockSpec((tm, LANE), row),
                  pl.BlockSpec((1, D_MODEL), lambda i: (0, 0))],
        out_specs=pl.BlockSpec((tm, D_MODEL), row),
        out_shape=SDS((n, D_MODEL), f32),
        compiler_params=_cparams("parallel"),
        name="moe_combine",
    )(x2, yg, wcol, g_final)


def _moe_layer(l, i_moe, mix_args, g_ffn, w_router, wg, wu, wd, g_final, final_norm):
    n = mix_args[0].shape[0]
    tm = TM_MOE
    npad = 2 * n + N_EXPERTS * tm
    wr = jnp.zeros((2 * SUBLANE, D_MODEL), f32).at[:N_EXPERTS].set(w_router[i_moe].astype(f32).T)
    wr_hi = wr.astype(bf16)
    wr_lo = (wr - wr_hi.astype(f32)).astype(bf16)
    x2, hp, meta, wcol, cnt = _mix_router(l, *mix_args, g_ffn, wr_hi, wr_lo)

    counts = cnt[:N_EXPERTS, 0].astype(i32)
    padded = ((counts + tm - 1) // tm) * tm
    ends = jnp.cumsum(padded)
    offs = ends - padded
    ids = jnp.arange(N_EXPERTS, dtype=i32)
    e12 = meta[0:2].astype(i32)
    r12 = meta[2:4].astype(i32)
    pos_t = r12 + jnp.sum(jnp.where(e12[..., None] == ids, offs, 0), axis=-1)
    blk_start = jnp.arange(npad // tm, dtype=i32) * tm
    blk_e = jnp.minimum(jnp.sum((blk_start[:, None] >= ends[None, :]).astype(i32), axis=1), N_EXPERTS - 1)
    own = blk_e[:, None] == ids
    blk_end = jnp.sum(jnp.where(own, offs + counts, 0), axis=1)
    blk_n = jnp.where(blk_start < ends[-1], jnp.clip(blk_end - blk_start, 0, tm), 0)
    last_e = jnp.max(jnp.where(blk_n > 0, blk_e, 0))
    blk_e = jnp.where(blk_n > 0, blk_e, last_e)

    xs = _sc_scatter2(hp.reshape(2 * n, SC_ROW),
                      jnp.concatenate([pos_t[0], pos_t[0] + npad]),
                      jnp.concatenate([pos_t[1], pos_t[1] + npad]),
                      2 * npad).reshape(2, npad, SC_ROW)
    ys = _moe_ffn(i_moe, blk_e, blk_n, xs, wg, wu, wd)
    gidx = jnp.concatenate([pos_t.reshape(-1), pos_t.reshape(-1) + npad])
    yg = _sc_gather(ys.reshape(2 * npad, SC_ROW), gidx).reshape(2, 2, n, SC_ROW)
    return _combine(x2, yg, wcol, g_final, final_norm)


def _rows(v):
    return v.astype(f32)[:, None, :]


def kernel(x, norm_mix, w_in, s5_lambda_re, s5_lambda_im, s5_log_dt, s5_b_re, s5_b_im, s5_c_re, s5_c_im, s5_d, s5_w_glu, s5_b_glu, s5_out_norm, gla_w_a2, gla_b_a2, gla_out_norm, w_out, norm_ffn, ffn_w_gate, ffn_w_up, ffn_w_down, moe_w_router, moe_w_gate, moe_w_up, moe_w_down, norm_final):
    bsz, seq, _ = x.shape
    n = bsz * seq
    depth = w_in.shape[0]
    x2 = x.reshape(n, D_MODEL)

    w_gp = jnp.zeros((depth, D_MODEL, LANE), f32).at[:, :, :GLA_GATE_RANK].set(w_in[:, :, 2048:2064])
    w_cat = jnp.concatenate([w_in[:, :, 0:512], w_in[:, :, 512:768] * (GLA_DK ** -0.5),
                             w_in[:, :, 1024:1536], w_in[:, :, 1536:2048], w_gp], axis=2).astype(bf16)
    w_kt = jnp.swapaxes(w_in[:, :, 768:1024], 1, 2).astype(bf16)
    s5_mats = jax.vmap(_s5_prep)(s5_lambda_re, s5_lambda_im, s5_log_dt, s5_b_re, s5_b_im, s5_c_re, s5_c_im)
    wa = jnp.zeros((depth, LANE, D_GLA_K), f32).at[:, :GLA_GATE_RANK].set(gla_w_a2).astype(bf16)
    g_mix, g_ffn, d_skip = _rows(norm_mix), _rows(norm_ffn), _rows(s5_d)
    ba, gn = _rows(gla_b_a2), _rows(gla_out_norm)
    mix_w = (s5_w_glu.astype(bf16), _rows(s5_b_glu), _rows(s5_out_norm), w_out.astype(bf16))
    ffn_w = (ffn_w_gate.astype(bf16), ffn_w_up.astype(bf16), ffn_w_down.astype(bf16))
    g_final = norm_final.astype(f32).reshape(1, D_MODEL)

    for l in range(depth):
        u, q, kt, v, r, gl = _inproj(l, x2, g_mix, w_cat, w_kt, bsz, seq)
        ys = _s5(l, u.reshape(bsz, seq, D_S5), *s5_mats, d_skip)
        yg = _gla(l, q.reshape(bsz, seq, D_GLA_K), kt, v.reshape(bsz, seq, D_GLA),
                  r.reshape(bsz, seq, D_GLA), gl.reshape(bsz, seq, LANE), wa, ba, gn)
        mix_args = (x2, ys.reshape(n, D_S5), yg.reshape(n, D_GLA)) + mix_w

        last = l == depth - 1
        if l % 2 == 0:
            x2 = _mix_ffn(l, l // 2, *mix_args, g_ffn, *ffn_w)
            if last:
                x2 = _final_norm(x2, g_final)
        else:
            x2 = _moe_layer(l, l // 2, mix_args, g_ffn, moe_w_router, moe_w_gate, moe_w_up,
                            moe_w_down, g_final, last)
    return x2.reshape(bsz, seq, D_MODEL)


def _final_norm_body(x_ref, g_ref, o_ref):
    o_ref[...] = _rms(x_ref[...], g_ref[...])


def _final_norm(x2, g):
    n = x2.shape[0]
    tm = TM_ROUTE
    return pl.pallas_call(
        _final_norm_body,
        grid=(n // tm,),
        in_specs=[pl.BlockSpec((tm, D_MODEL), lambda i: (i, 0)),
                  pl.BlockSpec((1, D_MODEL), lambda i: (0, 0))],
        out_specs=pl.BlockSpec((tm, D_MODEL), lambda i: (i, 0)),
        out_shape=SDS((n, D_MODEL), f32),
        compiler_params=_cparams("parallel"),
        name="final_norm",
    )(x2, g)
```

```python
import functools

import jax
import jax.numpy as jnp
from jax import lax
from jax.experimental import pallas as pl
from jax.experimental.pallas import tpu as pltpu
from jax.experimental.pallas import tpu_sc as plsc

f32 = jnp.float32
bf16 = jnp.bfloat16
u32 = jnp.uint32
i32 = jnp.int32
SDS = jax.ShapeDtypeStruct

D_MODEL = 1024
D_S5 = 512
S5_GROUP = 16
S5_GROUPS = 32
S5_STATE = 64
N_STATE = S5_GROUPS * S5_STATE
D_GLA = 512
GLA_HEADS = 4
GLA_DV = 128
GLA_DK = 64
D_GLA_K = 256
GLA_GATE_RANK = 16
GLA_TAU = 16.0
GLA_CHUNK = 64
D_FF = 2816
N_EXPERTS = 8
D_FF_EXPERT = 3584
EPS = 1e-6

LANE = 128
SUBLANE = 8
VMEM_LIMIT = 56 * 1024 * 1024

TM_PROJ = 2048
TT_S5 = 256
S5_SLABS = 4
S5_PITCH_PAD = 8
TG_GLA = 1024
GLA_SUB = 256
TM_FFN = 1024
FFN_CHUNKS = ((0, 1536), (1536, 2816))
TM_MOE = 2048
MOE_SUB = 2
TF_MOE = 512
TM_ROUTE = 1024
SC_WINDOW = 128
SC_ROW = 256


def _cparams(*sem):
    return pltpu.CompilerParams(dimension_semantics=sem, vmem_limit_bytes=VMEM_LIMIT)


def _layer(arr, l, **kw):
    return pl.BlockSpec((None,) + arr.shape[1:], lambda *_: (l,) + (0,) * (arr.ndim - 1), **kw)


def _rms(x, g):
    ms = jnp.mean(x * x, axis=-1, keepdims=True)
    return x * lax.rsqrt(ms + EPS) * g


def _inproj_body(x_ref, g_ref, w_ref, wkt_ref, u_ref, q_ref, kt_ref, v_ref, r_ref, gl_ref):
    hn = _rms(x_ref[...], g_ref[...]).astype(bf16)

    def proj(lo, hi):
        return jnp.dot(hn, w_ref[:, lo:hi], preferred_element_type=f32).astype(bf16)

    u_ref[...] = proj(0, 512)
    q_ref[...] = proj(512, 768)
    v_ref[...] = proj(768, 1280)
    r_ref[...] = proj(1280, 1792)
    gl_ref[...] = proj(1792, 1920)
    kt_ref[...] = lax.dot_general(wkt_ref[...], hn, (((1,), (1,)), ((), ())),
                                  preferred_element_type=f32).astype(bf16)


def _inproj(l, x2, g, w_cat, w_kt, bsz, seq):
    n = x2.shape[0]
    tm = TM_PROJ
    per_b = seq // tm
    row = lambda i: (i, 0)
    return pl.pallas_call(
        _inproj_body,
        grid=(n // tm,),
        in_specs=[pl.BlockSpec((tm, D_MODEL), row),
                  _layer(g, l), _layer(w_cat, l), _layer(w_kt, l)],
        out_specs=[pl.BlockSpec((tm, D_S5), row),
                   pl.BlockSpec((tm, D_GLA_K), row),
                   pl.BlockSpec((None, D_GLA_K, tm), lambda i: (i // per_b, 0, i % per_b)),
                   pl.BlockSpec((tm, D_GLA), row),
                   pl.BlockSpec((tm, D_GLA), row),
                   pl.BlockSpec((tm, LANE), row)],
        out_shape=[SDS((n, D_S5), bf16), SDS((n, D_GLA_K), bf16),
                   SDS((bsz, D_GLA_K, seq), bf16), SDS((n, D_GLA), bf16),
                   SDS((n, D_GLA), bf16), SDS((n, LANE), bf16)],
        compiler_params=_cparams("parallel"),
        name="inproj",
    )(x2, g, w_cat, w_kt)


N_SLAB = N_STATE // LANE


def _s5_prep(lam_re, lam_im, log_dt, b_re, b_im, c_re, c_im):
    lr = jnp.minimum(lam_re.astype(f32), -1e-4)
    li = lam_im.astype(f32)
    dt = jnp.exp(log_dt.astype(f32))[:, None]
    mag = jnp.exp(lr * dt)
    ab_re = mag * jnp.cos(li * dt)
    ab_im = mag * jnp.sin(li * dt)
    nr = ab_re - 1.0
    ni = ab_im
    den = lr * lr + li * li
    f_re = (nr * lr + ni * li) / den
    f_im = (ni * lr - nr * li) / den
    br = b_re.astype(f32)
    bi = b_im.astype(f32)
    bb_re = f_re[..., None] * br - f_im[..., None] * bi
    bb_im = f_re[..., None] * bi + f_im[..., None] * br
    abb_re = ab_re[..., None] * bb_re - ab_im[..., None] * bb_im
    abb_im = ab_re[..., None] * bb_im + ab_im[..., None] * bb_re
    cr = c_re.astype(f32)
    ci = c_im.astype(f32)
    ca_re = cr * ab_re[:, None, :] - ci * ab_im[:, None, :]
    ca_im = cr * ab_im[:, None, :] + ci * ab_re[:, None, :]
    cb = jnp.einsum('gop,gpc->goc', cr, bb_re) - jnp.einsum('gop,gpc->goc', ci, bb_im)

    bb4 = jnp.stack([abb_re, bb_re, abb_im, bb_im]).astype(bf16)
    bt = jnp.transpose(bb4, (0, 1, 3, 2)).reshape(4, 8, 4, S5_GROUP, S5_STATE)
    gl = jnp.arange(8)[None, :, None]
    gs = jnp.arange(4)[None, None, :]
    nn = jnp.arange(8)[:, None, None]
    sel = (gl == 4 * (nn % 2) + gs).astype(bf16)
    t = jnp.einsum('ngs,qnscp->qngcsp', sel, bt, preferred_element_type=bf16)
    t = t.reshape(2, 2, 8, LANE, 4 * S5_STATE)
    bmat = jnp.transpose(t, (0, 2, 1, 3, 4)).reshape(16, 2 * LANE, 4 * S5_STATE)

    c4 = jnp.stack([cr, -ci, ca_re, -ca_im]).astype(bf16)
    ct = jnp.transpose(c4, (0, 1, 3, 2)).reshape(4, 2, 16 * S5_STATE, S5_GROUP)
    col = jnp.arange(16 * S5_GROUP)
    spread = (col[None, :] % S5_GROUP == jnp.arange(S5_GROUP)[:, None]).astype(bf16)
    own = (jnp.arange(16 * S5_STATE)[:, None] // S5_STATE == col[None, :] // S5_GROUP)
    cmat = jnp.where(own, jnp.einsum('qjro,oc->qjrc', ct, spread, preferred_element_type=bf16),
                     jnp.zeros((), bf16))
    cbt = jnp.transpose(cb, (0, 2, 1)).reshape(2, 16, S5_GROUP, S5_GROUP)
    cbmat = jnp.einsum('gh,jgco->jgcho', jnp.eye(16, dtype=f32), cbt).reshape(2, 256, 256).astype(bf16)
    a2_re = (ab_re * ab_re - ab_im * ab_im).reshape(N_SLAB, 1, LANE)
    a2_im = (2.0 * ab_re * ab_im).reshape(N_SLAB, 1, LANE)
    return bmat, cmat, cbmat, a2_re, a2_im


def _s5_body(u_ref, bm_ref, cm_ref, cbm_ref, are_ref, aim_ref, d_ref, y_ref,
             wbuf, hbuf, obuf, hstate, zcarry):
    tt = u_ref.shape[1]
    tp = tt // 2
    rows = SUBLANE * tp
    pitch = tp + S5_PITCH_PAD
    nl = D_S5 // LANE

    @pl.when(pl.program_id(0) == 0)
    def _():
        hstate[...] = jnp.zeros_like(hstate)
        zcarry[...] = jnp.zeros_like(zcarry)

    w = pltpu.bitcast(u_ref[...].reshape(SUBLANE * tt, D_S5), u32)
    for b in range(SUBLANE):
        for k in range(nl):
            wbuf[k, b * pitch:b * pitch + tp, :] = w[b * tp:(b + 1) * tp, k * LANE:(k + 1) * LANE]
    w = jnp.concatenate(
        [jnp.concatenate([wbuf[k, pl.ds(m, SUBLANE, stride=pitch), :] for m in range(tp)], axis=0)
         for k in range(nl)], axis=1)
    ue_f = pltpu.bitcast(w << 16, f32)
    uo_f = pltpu.bitcast(w & jnp.uint32(0xFFFF0000), f32)
    ue = ue_f.astype(bf16)
    uo = uo_f.astype(bf16)

    for n in range(2 * 8):
        ks = LANE * ((n % 8) // 2)
        lhs = jnp.concatenate([ue[:, ks:ks + LANE], uo[:, ks:ks + LANE]], axis=1)
        res = jnp.dot(lhs, bm_ref[n], preferred_element_type=f32)
        hbuf[2 * n] = res[:, :LANE]
        hbuf[2 * n + 1] = res[:, LANE:]

    for c0 in range(0, N_SLAB, S5_SLABS):
        ar = [jnp.broadcast_to(are_ref[c0 + s], (SUBLANE, LANE)) for s in range(S5_SLABS)]
        ai = [jnp.broadcast_to(aim_ref[c0 + s], (SUBLANE, LANE)) for s in range(S5_SLABS)]

        def step(t, carry):
            out = []
            sl = pl.ds(pl.multiple_of(t * SUBLANE, SUBLANE), SUBLANE)
            for s in range(S5_SLABS):
                hr, hi = carry[2 * s], carry[2 * s + 1]
                nr = ar[s] * hr - ai[s] * hi + hbuf[c0 + s, sl, :]
                ni = ar[s] * hi + ai[s] * hr + hbuf[N_SLAB + c0 + s, sl, :]
                hbuf[c0 + s, sl, :] = nr
                hbuf[N_SLAB + c0 + s, sl, :] = ni
                out += [nr, ni]
            return tuple(out)

        init = []
        for s in range(S5_SLABS):
            init += [hstate[c0 + s], hstate[N_SLAB + c0 + s]]
        fin = lax.fori_loop(0, tp, step, tuple(init), unroll=4)
        for s in range(S5_SLABS):
            hstate[c0 + s] = fin[2 * s]
            hstate[N_SLAB + c0 + s] = fin[2 * s + 1]

    for j in range(2):
        h_re = jnp.concatenate([hbuf[8 * j + s] for s in range(8)], axis=1).astype(bf16)
        h_im = jnp.concatenate([hbuf[N_SLAB + 8 * j + s] for s in range(8)], axis=1).astype(bf16)
        cs = slice(256 * j, 256 * j + 256)
        dj = d_ref[:, cs]
        yo = jnp.dot(h_re, cm_ref[0, j], preferred_element_type=f32)
        yo = yo + jnp.dot(h_im, cm_ref[1, j], preferred_element_type=f32)
        yo = yo + dj * uo_f[:, cs]
        z = jnp.dot(h_re, cm_ref[2, j], preferred_element_type=f32)
        z = z + jnp.dot(h_im, cm_ref[3, j], preferred_element_type=f32)
        zs = jnp.concatenate([zcarry[:, cs], z[:rows - SUBLANE, :]], axis=0)
        zcarry[:, cs] = z[rows - SUBLANE:, :]
        ye = zs + jnp.dot(ue[:, cs], cbm_ref[j], preferred_element_type=f32) + dj * ue_f[:, cs]
        ge = pltpu.bitcast(jax.nn.gelu(ye).astype(bf16).astype(f32), u32)
        go = pltpu.bitcast(jax.nn.gelu(yo).astype(bf16).astype(f32), u32)
        packed = (ge >> 16) | go
        obuf[2 * j] = packed[:, :LANE]
        obuf[2 * j + 1] = packed[:, LANE:]

    for b in range(SUBLANE):
        yb = jnp.concatenate(
            [jnp.concatenate([obuf[k, pl.ds(8 * SUBLANE * i + b, SUBLANE, stride=SUBLANE), :]
                              for i in range(tp // SUBLANE)], axis=0) for k in range(nl)], axis=1)
        y_ref[b] = pltpu.bitcast(yb, bf16)


def _s5(l, u3, bmat, cmat, cbmat, a2_re, a2_im, d_skip):
    bsz, seq, _ = u3.shape
    assert bsz == SUBLANE
    tt = TT_S5
    blk = lambda i: (0, i, 0)
    return pl.pallas_call(
        _s5_body,
        grid=(seq // tt,),
        in_specs=[pl.BlockSpec((bsz, tt, D_S5), blk)]
        + [_layer(a, l) for a in (bmat, cmat, cbmat, a2_re, a2_im, d_skip)],
        out_specs=pl.BlockSpec((bsz, tt, D_S5), blk),
        out_shape=SDS((bsz, seq, D_S5), bf16),
        scratch_shapes=[pltpu.VMEM((D_S5 // LANE, bsz * (tt // 2 + S5_PITCH_PAD), LANE), u32),
                        pltpu.VMEM((2 * N_SLAB, bsz * (tt // 2), LANE), f32),
                        pltpu.VMEM((D_S5 // LANE, bsz * (tt // 2), LANE), u32),
                        pltpu.VMEM((2 * N_SLAB, SUBLANE, LANE), f32),
                        pltpu.VMEM((SUBLANE, D_S5), f32)],
        compiler_params=_cparams("arbitrary"),
        name="s5_scan",
    )(u3, bmat, cmat, cbmat, a2_re, a2_im, d_skip)


def _split2(x):
    hi = x.astype(bf16)
    lo = (x - hi.astype(f32)).astype(bf16)
    return hi, lo


def _log_sigmoid(x):
    return -(jnp.maximum(-x, 0.0) + jnp.log1p(jnp.exp(-jnp.abs(x))))


def _gla_body(q_ref, kt_ref, v_ref, r_ref, g_ref, wa_ref, ba_ref, gn_ref,
              o_ref, s_ref):
    c = GLA_CHUNK
    tg = q_ref.shape[0]

    @pl.when(pl.program_id(1) == 0)
    def _():
        s_ref[...] = jnp.zeros_like(s_ref)

    ts = GLA_SUB
    nc = ts // c
    nh = GLA_HEADS
    ri = lax.broadcasted_iota(i32, (ts, ts), 0)
    ci = lax.broadcasted_iota(i32, (ts, ts), 1)
    same = (ri // c) == (ci // c)
    tril = (same & (ri >= ci)).astype(bf16)
    head_of_lane = lax.broadcasted_iota(i32, (c, D_GLA_K), 1) // GLA_DK
    r_idx = lax.broadcasted_iota(i32, (nc * nh * c, ts), 0)
    c_idx = lax.broadcasted_iota(i32, (nc * nh * c, ts), 1)
    causal = ((r_idx // c) % nc == c_idx // c) & (r_idx % c >= c_idx % c)
    chunk_of_lane = lax.broadcasted_iota(i32, (D_GLA_K, ts), 1) // c
    chunk_of_lane_h = lax.broadcasted_iota(i32, (GLA_DK, ts), 1) // c

    nsub = tg // ts
    subs = [slice(st * ts, (st + 1) * ts) for st in range(nsub)]

    def decay_stage(rs):
        g = g_ref[rs, :]
        la = _log_sigmoid(jnp.dot(g, wa_ref[...], preferred_element_type=f32) + ba_ref[...]) / GLA_TAU
        la_hi, la_lo = _split2(la)
        cum = (jnp.dot(tril, la_hi, preferred_element_type=f32)
               + jnp.dot(tril, la_lo, preferred_element_type=f32))
        cumt = cum.T
        ends = [cumt[:, (cc + 1) * c - 1:(cc + 1) * c] for cc in range(nc)]
        clt = ends[nc - 1]
        for cc in range(nc - 2, -1, -1):
            clt = jnp.where(chunk_of_lane <= cc, ends[cc], clt)
        qt = q_ref[rs, :].astype(f32) * jnp.exp(cum)
        ktt = kt_ref[:, rs].astype(f32)
        k_t = (ktt * jnp.exp(-cumt)).astype(bf16)
        k_end = ktt * jnp.exp(clt - cumt)
        q_stack = jnp.concatenate(
            [jnp.where(head_of_lane == h, qt[cc * c:(cc + 1) * c, :], 0.0)
             for h in range(nh) for cc in range(nc)], axis=0).astype(bf16)
        k_stack = [jnp.concatenate(
            [jnp.where(chunk_of_lane_h == cc, k_end[h * GLA_DK:(h + 1) * GLA_DK, :], 0.0)
             for cc in range(nc)], axis=0).astype(bf16) for h in range(nh)]
        dec = [jnp.exp(clt[:, cc * c:cc * c + 1]) for cc in range(nc)]
        return q_stack, k_t, k_stack, dec

    def matmul_stage(rs, staged, state):
        q_stack, k_t, k_stack, dec = staged
        scores = jnp.dot(q_stack, k_t, preferred_element_type=f32)
        scores = jnp.where(causal, scores, 0.0).astype(bf16)
        o_intra, upd_h = [], []
        for h in range(nh):
            vh = v_ref[rs, h * GLA_DV:(h + 1) * GLA_DV]
            o_intra.append(jnp.dot(scores[h * nc * c:(h + 1) * nc * c, :], vh,
                                   preferred_element_type=f32))
            upd_h.append(jnp.dot(k_stack[h], vh, preferred_element_type=f32))
        o_inter = []
        for cc in range(nc):
            q_cc = jnp.concatenate([q_stack[(h * nc + cc) * c:(h * nc + cc + 1) * c, :] for h in range(nh)],
                                   axis=0)
            o_inter.append(jnp.dot(q_cc, state.astype(bf16), preferred_element_type=f32))
            upd = jnp.concatenate([upd_h[h][cc * GLA_DK:(cc + 1) * GLA_DK, :] for h in range(nh)], axis=0)
            state = dec[cc] * state + upd
        return o_intra, o_inter, state

    def output_stage(rs, o_intra, o_inter):
        rows = []
        for cc in range(nc):
            outs = []
            for h in range(nh):
                o = o_intra[h][cc * c:(cc + 1) * c, :] + o_inter[cc][h * c:(h + 1) * c, :]
                o = o * lax.rsqrt(jnp.mean(o * o, axis=-1, keepdims=True) + EPS)
                outs.append(o)
            rows.append(jnp.concatenate(outs, axis=1))
        o_cat = jnp.concatenate(rows, axis=0)
        o_ref[rs, :] = (o_cat * gn_ref[...] * jax.nn.silu(r_ref[rs, :].astype(f32))).astype(bf16)

    state = s_ref[...]
    staged = decay_stage(subs[0])
    for st in range(nsub):
        nxt = decay_stage(subs[st + 1]) if st + 1 < nsub else None
        o_full, o_inter, state = matmul_stage(subs[st], staged, state)
        output_stage(subs[st], o_full, o_inter)
        staged = nxt
    s_ref[...] = state


def _gla(l, q3, kt3, v3, r3, g3, wa, ba, gn):
    bsz, seq, _ = q3.shape
    tg = TG_GLA
    tok = lambda b, i: (b, i, 0)
    return pl.pallas_call(
        _gla_body,
        grid=(bsz, seq // tg),
        in_specs=[pl.BlockSpec((None, tg, D_GLA_K), tok),
                  pl.BlockSpec((None, D_GLA_K, tg), lambda b, i: (b, 0, i)),
                  pl.BlockSpec((None, tg, D_GLA), tok),
                  pl.BlockSpec((None, tg, D_GLA), tok),
                  pl.BlockSpec((None, tg, LANE), tok),
                  _layer(wa, l), _layer(ba, l), _layer(gn, l)],
        out_specs=pl.BlockSpec((None, tg, D_GLA), tok),
        out_shape=SDS((bsz, seq, D_GLA), bf16),
        scratch_shapes=[pltpu.VMEM((D_GLA_K, GLA_DV), f32)],
        compiler_params=_cparams("parallel", "arbitrary"),
        name="gla",
    )(q3, kt3, v3, r3, g3, wa, ba, gn)


def _mix_out(x_ref, ys_ref, yg_ref, wgl_ref, bg_ref, gs_ref, wo_ref):
    y = ys_ref[...]
    z = jnp.dot(y, wgl_ref[...], preferred_element_type=f32) + bg_ref[...]
    yf = y.astype(f32) * jax.nn.sigmoid(z)
    ys = _rms(yf, gs_ref[...]).astype(bf16)
    acc = jnp.dot(ys, wo_ref[0:D_S5, :], preferred_element_type=f32)
    acc = acc + jnp.dot(yg_ref[...], wo_ref[D_S5:, :], preferred_element_type=f32)
    return x_ref[...] + acc


def _mix_specs(l, tm, w_glu, b_glu, g_s5, w_out):
    row = lambda i: (i, 0)
    once = dict(pipeline_mode=pl.Buffered(1))
    return [pl.BlockSpec((tm, D_MODEL), row),
            pl.BlockSpec((tm, D_S5), row),
            pl.BlockSpec((tm, D_GLA), row),
            _layer(w_glu, l, **once), _layer(b_glu, l), _layer(g_s5, l), _layer(w_out, l, **once)]


def _mix_ffn_body(x_ref, ys_ref, yg_ref, wgl_ref, bg_ref, gs_ref, wo_ref,
                  g_ref, wg_ref, wu_ref, wd_ref, o_ref):
    x = _mix_out(x_ref, ys_ref, yg_ref, wgl_ref, bg_ref, gs_ref, wo_ref)
    hn = _rms(x, g_ref[...]).astype(bf16)
    acc = x
    for lo, hi in FFN_CHUNKS:
        gate = jnp.dot(hn, wg_ref[:, lo:hi], preferred_element_type=f32)
        up = jnp.dot(hn, wu_ref[:, lo:hi], preferred_element_type=f32)
        act = (jax.nn.silu(gate) * up).astype(bf16)
        acc = acc + jnp.dot(act, wd_ref[lo:hi, :], preferred_element_type=f32)
    o_ref[...] = acc


def _mix_ffn(l, i_ffn, x2, ys, yg, w_glu, b_glu, g_s5, w_out, g, wg, wu, wd):
    n = x2.shape[0]
    tm = TM_FFN
    row = lambda i: (i, 0)
    once = dict(pipeline_mode=pl.Buffered(1))
    return pl.pallas_call(
        _mix_ffn_body,
        grid=(n // tm,),
        in_specs=_mix_specs(l, tm, w_glu, b_glu, g_s5, w_out) + [
            _layer(g, l), _layer(wg, i_ffn, **once), _layer(wu, i_ffn, **once),
            _layer(wd, i_ffn, **once)],
        out_specs=pl.BlockSpec((tm, D_MODEL), row),
        out_shape=SDS((n, D_MODEL), f32),
        compiler_params=_cparams("parallel"),
        name="mix_ffn",
    )(x2, ys, yg, w_glu, b_glu, g_s5, w_out, g, wg, wu, wd)


def _pack_bf16_pairs(a):
    bits = pltpu.bitcast(a.astype(bf16).astype(f32), u32)
    half = a.shape[1] // 2
    return bits[:, :half] | (bits[:, half:] >> 16)


def _unpack_bf16_pairs(p):
    hi = pltpu.bitcast(p & jnp.uint32(0xFFFF0000), f32).astype(bf16)
    lo = pltpu.bitcast(p << 16, f32).astype(bf16)
    return hi, lo


def _mix_router_body(x_ref, ys_ref, yg_ref, wgl_ref, bg_ref, gs_ref, wo_ref,
                     g_ref, wh_ref, wl_ref, xo_ref, hp_ref, meta_ref, wcol_ref, cnt_ref, carry):
    tm = x_ref.shape[0]

    @pl.when(pl.program_id(0) == 0)
    def _():
        carry[...] = jnp.zeros_like(carry)

    x = _mix_out(x_ref, ys_ref, yg_ref, wgl_ref, bg_ref, gs_ref, wo_ref)
    xo_ref[...] = x
    hn = _rms(x, g_ref[...])
    packed = _pack_bf16_pairs(hn)
    hp_ref[0] = packed[:, :SC_ROW]
    hp_ref[1] = packed[:, SC_ROW:]

    h_hi, h_lo = _split2(hn)
    nt = (((1,), (1,)), ((), ()))
    logits = (lax.dot_general(wh_ref[...], h_hi, nt, preferred_element_type=f32)
              + lax.dot_general(wh_ref[...], h_lo, nt, preferred_element_type=f32)
              + lax.dot_general(wl_ref[...], h_hi, nt, preferred_element_type=f32))
    er = logits.shape[0]
    row = lax.broadcasted_iota(i32, (er, tm), 0)
    neg = jnp.float32(-jnp.inf)
    logits = jnp.where(row < N_EXPERTS, logits, neg)
    m1 = jnp.max(logits, axis=0, keepdims=True)
    i1 = jnp.min(jnp.where(logits == m1, row, er), axis=0, keepdims=True)
    l2 = jnp.where(row == i1, neg, logits)
    m2 = jnp.max(l2, axis=0, keepdims=True)
    i2 = jnp.min(jnp.where(l2 == m2, row, er), axis=0, keepdims=True)
    e21 = jnp.exp(m2 - m1)
    w1 = 1.0 / (1.0 + e21)
    w2 = e21 / (1.0 + e21)

    sel1 = row == i1
    sel2 = row == i2
    sel = (sel1 | sel2).astype(f32)
    ri = lax.broadcasted_iota(i32, (tm, tm), 0)
    ci = lax.broadcasted_iota(i32, (tm, tm), 1)
    triu = (ri <= ci).astype(bf16)
    incl = jnp.dot(sel.astype(bf16), triu, preferred_element_type=f32)
    rank = incl - sel + carry[:, 0:1]
    r1 = jnp.sum(jnp.where(sel1, rank, 0.0), axis=0, keepdims=True)
    r2 = jnp.sum(jnp.where(sel2, rank, 0.0), axis=0, keepdims=True)
    new_cnt = carry[:, 0:1] + incl[:, tm - 1:tm]
    carry[...] = jnp.broadcast_to(new_cnt, carry.shape)
    cnt_ref[...] = jnp.broadcast_to(new_cnt, cnt_ref.shape)

    srow = lax.broadcasted_iota(i32, (SUBLANE, tm), 0)
    meta = jnp.where(srow == 0, i1.astype(f32), 0.0)
    meta = jnp.where(srow == 1, i2.astype(f32), meta)
    meta = jnp.where(srow == 2, r1, meta)
    meta = jnp.where(srow == 3, r2, meta)
    meta_ref[...] = meta
    prow = lax.broadcasted_iota(i32, (LANE, tm), 0)
    wpad = jnp.where(prow == 0, w1, jnp.where(prow == 1, w2, 0.0))
    wcol_ref[...] = wpad.T


def _mix_router(l, x2, ys, yg, w_glu, b_glu, g_s5, w_out, g, w_hi, w_lo):
    n = x2.shape[0]
    tm = TM_ROUTE
    row = lambda i: (i, 0)
    const = lambda i: (0, 0)
    return pl.pallas_call(
        _mix_router_body,
        grid=(n // tm,),
        in_specs=_mix_specs(l, tm, w_glu, b_glu, g_s5, w_out) + [
            _layer(g, l),
            pl.BlockSpec(w_hi.shape, const),
            pl.BlockSpec(w_lo.shape, const)],
        out_specs=[pl.BlockSpec((tm, D_MODEL), row),
                   pl.BlockSpec((2, tm, SC_ROW), lambda i: (0, i, 0)),
                   pl.BlockSpec((SUBLANE, tm), lambda i: (0, i)),
                   pl.BlockSpec((tm, LANE), row),
                   pl.BlockSpec(w_hi.shape[:1] + (LANE,), const)],
        out_shape=[SDS((n, D_MODEL), f32), SDS((2, n, SC_ROW), u32), SDS((SUBLANE, n), f32),
                   SDS((n, LANE), f32), SDS(w_hi.shape[:1] + (LANE,), f32)],
        scratch_shapes=[pltpu.VMEM(w_hi.shape[:1] + (LANE,), f32)],
        compiler_params=_cparams("arbitrary"),
        name="mix_router",
    )(x2, ys, yg, w_glu, b_glu, g_s5, w_out, g, w_hi, w_lo)


def _sc_gather(table, idx):
    ni = idx.shape[0]
    mesh = plsc.VectorSubcoreMesh(core_axis_name="core", subcore_axis_name="subcore")
    idx2 = idx.reshape(1, ni)

    @pl.kernel(out_type=SDS((ni, SC_ROW), table.dtype), mesh=mesh)
    def kern(t_hbm, i_hbm, o_hbm):
        def body(i_vmem, o_vmem):
            pltpu.sync_copy(t_hbm.at[i_vmem.at[0]], o_vmem)

        pltpu.emit_pipeline(
            body, grid=(ni // SC_WINDOW,),
            in_specs=[pl.BlockSpec((1, SC_WINDOW), index_map=lambda i: (0, i))],
            out_specs=[pl.BlockSpec((SC_WINDOW, SC_ROW), index_map=lambda i: (i, 0))],
            core_axis_name=("core", "subcore"),
            dimension_semantics=(pltpu.PARALLEL,),
        )(i_hbm, o_hbm)

    return kern(table, idx2)


def _sc_scatter2(x, idx0, idx1, nrows):
    ni = x.shape[0]
    mesh = plsc.VectorSubcoreMesh(core_axis_name="core", subcore_axis_name="subcore")

    @pl.kernel(out_type=SDS((nrows, SC_ROW), x.dtype), mesh=mesh)
    def kern(x_hbm, i0_hbm, i1_hbm, o_hbm):
        def body(x_vmem, i0_vmem, i1_vmem):
            pltpu.sync_copy(x_vmem, o_hbm.at[i0_vmem.at[0]])
            pltpu.sync_copy(x_vmem, o_hbm.at[i1_vmem.at[0]])

        pltpu.emit_pipeline(
            body, grid=(ni // SC_WINDOW,),
            in_specs=[pl.BlockSpec((SC_WINDOW, SC_ROW), index_map=lambda i: (i, 0)),
                      pl.BlockSpec((1, SC_WINDOW), index_map=lambda i: (0, i)),
                      pl.BlockSpec((1, SC_WINDOW), index_map=lambda i: (0, i))],
            out_specs=[],
            core_axis_name=("core", "subcore"),
            dimension_semantics=(pltpu.PARALLEL,),
        )(x_hbm, i0_hbm, i1_hbm)

    return kern(x, idx0.reshape(1, ni), idx1.reshape(1, ni))


def _moe_ffn_body(be_ref, bn_ref, xs_ref, wg_ref, wu_ref, wd_ref, ys_ref, xb, act0, act1, acc):
    i = pl.program_id(0)
    f = pl.program_id(1)
    nf = pl.num_programs(1) - 1
    nvalid = bn_ref[i]
    sub = xs_ref.shape[1] // MOE_SUB

    def emit(start, size, cond):
        rs = pl.ds(start, size)

        def gate_up():
            x = xb[rs, :]
            gate = jnp.dot(x, wg_ref[...].astype(bf16), preferred_element_type=f32)
            up = jnp.dot(x, wu_ref[...].astype(bf16), preferred_element_type=f32)
            return (jax.nn.silu(gate) * up).astype(bf16)

        def down(act_ref):
            return jnp.dot(act_ref[rs, :], wd_ref[...].astype(bf16), preferred_element_type=f32)

        @pl.when(cond & (f == 0))
        def _():
            live = lax.broadcasted_iota(i32, (size, SC_ROW), 0) + start < nvalid
            hi0, lo0 = _unpack_bf16_pairs(jnp.where(live, xs_ref[0, rs, :], jnp.uint32(0)))
            hi1, lo1 = _unpack_bf16_pairs(jnp.where(live, xs_ref[1, rs, :], jnp.uint32(0)))
            xb[rs, :] = jnp.concatenate([hi0, hi1, lo0, lo1], axis=1)
            acc[rs, :] = jnp.zeros((size, D_MODEL), f32)
            act0[rs, :] = gate_up()

        @pl.when(cond & (f > 0) & (f < nf) & (f % 2 == 1))
        def _():
            act1[rs, :] = gate_up()
            acc[rs, :] += down(act0)

        @pl.when(cond & (f > 0) & (f < nf) & (f % 2 == 0))
        def _():
            act0[rs, :] = gate_up()
            acc[rs, :] += down(act1)

        @pl.when(cond & (f == nf))
        def _():
            last = act0 if (D_FF_EXPERT // TF_MOE - 1) % 2 == 0 else act1
            packed = _pack_bf16_pairs(acc[rs, :] + down(last))
            ys_ref[0, rs, :] = packed[:, :SC_ROW]
            ys_ref[1, rs, :] = packed[:, SC_ROW:]

    def zero_out(start, size, cond):
        @pl.when(cond & (f == nf))
        def _():
            ys_ref[:, pl.ds(start, size), :] = jnp.zeros((2, size, SC_ROW), u32)

    for h in range(MOE_SUB):
        live = nvalid - h * sub
        whole = live > sub // 2
        head = (live > 0) & (live <= sub // 2)
        emit(h * sub, sub, whole)
        emit(h * sub, sub // 2, head)
        zero_out(h * sub + sub // 2, sub // 2, head)
        zero_out(h * sub, sub, live <= 0)


def _moe_ffn(i_moe, blk_e, blk_n, xs, wg, wu, wd):
    npad = xs.shape[1]
    tm, tf = TM_MOE, TF_MOE
    nblk = npad // tm
    nf = D_FF_EXPERT // tf

    def nxt(i):
        return jnp.minimum(i + 1, nblk - 1)

    def x_idx(i, f, be, bn):
        return (0, jnp.where(f == nf, nxt(i), i), 0)

    def gu_idx(i, f, be, bn):
        ahead = (f == nf) & (bn[i] > 0) & (bn[nxt(i)] > 0)
        e = jnp.where(ahead, be[nxt(i)], be[i])
        t = jnp.where(ahead, 0, jnp.where(bn[i] > 0, jnp.minimum(f, nf - 1), nf - 1))
        return (i_moe, e, 0, t)

    def d_idx(i, f, be, bn):
        prev = jnp.maximum(i - 1, 0)
        keep = (f == 0) & (i > 0)
        e = jnp.where(keep, be[prev], be[i])
        t = jnp.where(keep | (bn[i] == 0), nf - 1, jnp.maximum(f - 1, 0))
        return (i_moe, e, t, 0)

    grid_spec = pltpu.PrefetchScalarGridSpec(
        num_scalar_prefetch=2,
        grid=(nblk, nf + 1),
        in_specs=[pl.BlockSpec((2, tm, SC_ROW), x_idx),
                  pl.BlockSpec((None, None, D_MODEL, tf), gu_idx),
                  pl.BlockSpec((None, None, D_MODEL, tf), gu_idx),
                  pl.BlockSpec((None, None, tf, D_MODEL), d_idx)],
        out_specs=pl.BlockSpec((2, tm, SC_ROW), lambda i, f, be, bn: (0, i, 0)),
        scratch_shapes=[pltpu.VMEM((tm, D_MODEL), bf16), pltpu.VMEM((tm, tf), bf16),
                        pltpu.VMEM((tm, tf), bf16), pltpu.VMEM((tm, D_MODEL), f32)],
    )
    return pl.pallas_call(
        _moe_ffn_body,
        grid_spec=grid_spec,
        out_shape=SDS((2, npad, SC_ROW), u32),
        compiler_params=_cparams("parallel", "arbitrary"),
        name="moe_ffn",
    )(blk_e, blk_n, xs, wg, wu, wd)


def _combine_body(x_ref, yg_ref, wcol_ref, g_ref, o_ref, *, final_norm):
    w1 = wcol_ref[:, 0:1]
    w2 = wcol_ref[:, 1:2]

    def rows(k):
        hi0, lo0 = _unpack_bf16_pairs(yg_ref[0, k])
        hi1, lo1 = _unpack_bf16_pairs(yg_ref[1, k])
        return jnp.concatenate([hi0, hi1, lo0, lo1], axis=1).astype(f32)

    xo = x_ref[...] + (w1 * rows(0) + w2 * rows(1))
    if final_norm:
        xo = _rms(xo, g_ref[...])
    o_ref[...] = xo


def _combine(x2, yg, wcol, g_final, final_norm):
    n = x2.shape[0]
    tm = TM_ROUTE
    row = lambda i: (i, 0)
    return pl.pallas_call(
        functools.partial(_combine_body, final_norm=final_norm),
        grid=(n // tm,),
        in_specs=[pl.BlockSpec((tm, D_MODEL), row),
                  pl.BlockSpec((2, 2, tm, SC_ROW), lambda i: (0, 0, i, 0)),
                  pl.BlockSpec((tm, LANE), row),
                  pl.BlockSpec((1, D_MODEL), lambda i: (0, 0))],
        out_specs=pl.BlockSpec((tm, D_MODEL), row),
        out_shape=SDS((n, D_MODEL), f32),
        compiler_params=_cparams("parallel"),
        name="moe_combine",
    )(x2, yg, wcol, g_final)


def _moe_layer(l, i_moe, mix_args, g_ffn, w_router, wg, wu, wd, g_final, final_norm):
    n = mix_args[0].shape[0]
    tm = TM_MOE
    npad = 2 * n + N_EXPERTS * tm
    wr = jnp.zeros((2 * SUBLANE, D_MODEL), f32).at[:N_EXPERTS].set(w_router[i_moe].astype(f32).T)
    wr_hi = wr.astype(bf16)
    wr_lo = (wr - wr_hi.astype(f32)).astype(bf16)
    x2, hp, meta, wcol, cnt = _mix_router(l, *mix_args, g_ffn, wr_hi, wr_lo)

    counts = cnt[:N_EXPERTS, 0].astype(i32)
    padded = ((counts + tm - 1) // tm) * tm
    ends = jnp.cumsum(padded)
    offs = ends - padded
    ids = jnp.arange(N_EXPERTS, dtype=i32)
    e12 = meta[0:2].astype(i32)
    r12 = meta[2:4].astype(i32)
    pos_t = r12 + jnp.sum(jnp.where(e12[..., None] == ids, offs, 0), axis=-1)
    blk_start = jnp.arange(npad // tm, dtype=i32) * tm
    blk_e = jnp.minimum(jnp.sum((blk_start[:, None] >= ends[None, :]).astype(i32), axis=1), N_EXPERTS - 1)
    own = blk_e[:, None] == ids
    blk_end = jnp.sum(jnp.where(own, offs + counts, 0), axis=1)
    blk_n = jnp.where(blk_start < ends[-1], jnp.clip(blk_end - blk_start, 0, tm), 0)
    last_e = jnp.max(jnp.where(blk_n > 0, blk_e, 0))
    blk_e = jnp.where(blk_n > 0, blk_e, last_e)

    xs = _sc_scatter2(hp.reshape(2 * n, SC_ROW),
                      jnp.concatenate([pos_t[0], pos_t[0] + npad]),
                      jnp.concatenate([pos_t[1], pos_t[1] + npad]),
                      2 * npad).reshape(2, npad, SC_ROW)
    ys = _moe_ffn(i_moe, blk_e, blk_n, xs, wg, wu, wd)
    gidx = jnp.concatenate([pos_t.reshape(-1), pos_t.reshape(-1) + npad])
    yg = _sc_gather(ys.reshape(2 * npad, SC_ROW), gidx).reshape(2, 2, n, SC_ROW)
    return _combine(x2, yg, wcol, g_final, final_norm)


def _rows(v):
    return v.astype(f32)[:, None, :]


def kernel(x, norm_mix, w_in, s5_lambda_re, s5_lambda_im, s5_log_dt, s5_b_re, s5_b_im, s5_c_re, s5_c_im, s5_d, s5_w_glu, s5_b_glu, s5_out_norm, gla_w_a2, gla_b_a2, gla_out_norm, w_out, norm_ffn, ffn_w_gate, ffn_w_up, ffn_w_down, moe_w_router, moe_w_gate, moe_w_up, moe_w_down, norm_final):
    bsz, seq, _ = x.shape
    n = bsz * seq
    depth = w_in.shape[0]
    x2 = x.reshape(n, D_MODEL)

    w_gp = jnp.zeros((depth, D_MODEL, LANE), f32).at[:, :, :GLA_GATE_RANK].set(w_in[:, :, 2048:2064])
    w_cat = jnp.concatenate([w_in[:, :, 0:512], w_in[:, :, 512:768] * (GLA_DK ** -0.5),
                             w_in[:, :, 1024:1536], w_in[:, :, 1536:2048], w_gp], axis=2).astype(bf16)
    w_kt = jnp.swapaxes(w_in[:, :, 768:1024], 1, 2).astype(bf16)
    s5_mats = jax.vmap(_s5_prep)(s5_lambda_re, s5_lambda_im, s5_log_dt, s5_b_re, s5_b_im, s5_c_re, s5_c_im)
    wa = jnp.zeros((depth, LANE, D_GLA_K), f32).at[:, :GLA_GATE_RANK].set(gla_w_a2).astype(bf16)
    g_mix, g_ffn, d_skip = _rows(norm_mix), _rows(norm_ffn), _rows(s5_d)
    ba, gn = _rows(gla_b_a2), _rows(gla_out_norm)
    mix_w = (s5_w_glu.astype(bf16), _rows(s5_b_glu), _rows(s5_out_norm), w_out.astype(bf16))
    ffn_w = (ffn_w_gate.astype(bf16), ffn_w_up.astype(bf16), ffn_w_down.astype(bf16))
    g_final = norm_final.astype(f32).reshape(1, D_MODEL)

    for l in range(depth):
        u, q, kt, v, r, gl = _inproj(l, x2, g_mix, w_cat, w_kt, bsz, seq)
        ys = _s5(l, u.reshape(bsz, seq, D_S5), *s5_mats, d_skip)
        yg = _gla(l, q.reshape(bsz, seq, D_GLA_K), kt, v.reshape(bsz, seq, D_GLA),
                  r.reshape(bsz, seq, D_GLA), gl.reshape(bsz, seq, LANE), wa, ba, gn)
        mix_args = (x2, ys.reshape(n, D_S5), yg.reshape(n, D_GLA)) + mix_w

        last = l == depth - 1
        if l % 2 == 0:
            x2 = _mix_ffn(l, l // 2, *mix_args, g_ffn, *ffn_w)
            if last:
                x2 = _final_norm(x2, g_final)
        else:
            x2 = _moe_layer(l, l // 2, mix_args, g_ffn, moe_w_router, moe_w_gate, moe_w_up,
                            moe_w_down, g_final, last)
    return x2.reshape(bsz, seq, D_MODEL)


def _final_norm_body(x_ref, g_ref, o_ref):
    o_ref[...] = _rms(x_ref[...], g_ref[...])


def _final_norm(x2, g):
    n = x2.shape[0]
    tm = TM_ROUTE
    return pl.pallas_call(
        _final_norm_body,
        grid=(n // tm,),
        in_specs=[pl.BlockSpec((tm, D_MODEL), lambda i: (i, 0)),
                  pl.BlockSpec((1, D_MODEL), lambda i: (0, 0))],
        out_specs=pl.BlockSpec((tm, D_MODEL), lambda i: (i, 0)),
        out_shape=SDS((n, D_MODEL), f32),
        compiler_params=_cparams("parallel"),
        name="final_norm",
    )(x2, g)
```

```python
import functools

import jax
import jax.numpy as jnp
from jax import lax
from jax.experimental import pallas as pl
from jax.experimental.pallas import tpu as pltpu
from jax.experimental.pallas import tpu_sc as plsc

f32 = jnp.float32
bf16 = jnp.bfloat16
u32 = jnp.uint32
i32 = jnp.int32
SDS = jax.ShapeDtypeStruct

D_MODEL = 1024
D_S5 = 512
S5_GROUP = 16
S5_GROUPS = 32
S5_STATE = 64
N_STATE = S5_GROUPS * S5_STATE
D_GLA = 512
GLA_HEADS = 4
GLA_DV = 128
GLA_DK = 64
D_GLA_K = 256
GLA_GATE_RANK = 16
GLA_TAU = 16.0
GLA_CHUNK = 64
D_FF = 2816
N_EXPERTS = 8
D_FF_EXPERT = 3584
EPS = 1e-6

LANE = 128
SUBLANE = 8
VMEM_LIMIT = 56 * 1024 * 1024

TM_PROJ = 2048
TT_S5 = 256
S5_SLABS = 4
S5_PITCH_PAD = 8
TG_GLA = 1024
GLA_SUB = 256
TM_FFN = 1024
FFN_CHUNKS = ((0, 1536), (1536, 2816))
TM_MOE = 2048
MOE_SUB = 2
TF_MOE = 512
TM_ROUTE = 1024
SC_WINDOW = 128
SC_ROW = 256


def _cparams(*sem):
    return pltpu.CompilerParams(dimension_semantics=sem, vmem_limit_bytes=VMEM_LIMIT)


def _layer(arr, l, **kw):
    return pl.BlockSpec((None,) + arr.shape[1:], lambda *_: (l,) + (0,) * (arr.ndim - 1), **kw)


def _rms(x, g):
    ms = jnp.mean(x * x, axis=-1, keepdims=True)
    return x * lax.rsqrt(ms + EPS) * g


def _inproj_body(x_ref, g_ref, w_ref, wkt_ref, u_ref, q_ref, kt_ref, v_ref, r_ref, gl_ref):
    hn = _rms(x_ref[...], g_ref[...]).astype(bf16)

    def proj(lo, hi):
        return jnp.dot(hn, w_ref[:, lo:hi], preferred_element_type=f32).astype(bf16)

    u_ref[...] = proj(0, 512)
    q_ref[...] = proj(512, 768)
    v_ref[...] = proj(768, 1280)
    r_ref[...] = proj(1280, 1792)
    gl_ref[...] = proj(1792, 1920)
    kt_ref[...] = lax.dot_general(wkt_ref[...], hn, (((1,), (1,)), ((), ())),
                                  preferred_element_type=f32).astype(bf16)


def _inproj(l, x2, g, w_cat, w_kt, bsz, seq):
    n = x2.shape[0]
    tm = TM_PROJ
    per_b = seq // tm
    row = lambda i: (i, 0)
    return pl.pallas_call(
        _inproj_body,
        grid=(n // tm,),
        in_specs=[pl.BlockSpec((tm, D_MODEL), row),
                  _layer(g, l), _layer(w_cat, l), _layer(w_kt, l)],
        out_specs=[pl.BlockSpec((tm, D_S5), row),
                   pl.BlockSpec((tm, D_GLA_K), row),
                   pl.BlockSpec((None, D_GLA_K, tm), lambda i: (i // per_b, 0, i % per_b)),
                   pl.BlockSpec((tm, D_GLA), row),
                   pl.BlockSpec((tm, D_GLA), row),
                   pl.BlockSpec((tm, LANE), row)],
        out_shape=[SDS((n, D_S5), bf16), SDS((n, D_GLA_K), bf16),
                   SDS((bsz, D_GLA_K, seq), bf16), SDS((n, D_GLA), bf16),
                   SDS((n, D_GLA), bf16), SDS((n, LANE), bf16)],
        compiler_params=_cparams("parallel"),
        name="inproj",
    )(x2, g, w_cat, w_kt)


N_SLAB = N_STATE // LANE


def _s5_prep(lam_re, lam_im, log_dt, b_re, b_im, c_re, c_im):
    lr = jnp.minimum(lam_re.astype(f32), -1e-4)
    li = lam_im.astype(f32)
    dt = jnp.exp(log_dt.astype(f32))[:, None]
    mag = jnp.exp(lr * dt)
    ab_re = mag * jnp.cos(li * dt)
    ab_im = mag * jnp.sin(li * dt)
    nr = ab_re - 1.0
    ni = ab_im
    den = lr * lr + li * li
    f_re = (nr * lr + ni * li) / den
    f_im = (ni * lr - nr * li) / den
    br = b_re.astype(f32)
    bi = b_im.astype(f32)
    bb_re = f_re[..., None] * br - f_im[..., None] * bi
    bb_im = f_re[..., None] * bi + f_im[..., None] * br
    abb_re = ab_re[..., None] * bb_re - ab_im[..., None] * bb_im
    abb_im = ab_re[..., None] * bb_im + ab_im[..., None] * bb_re
    cr = c_re.astype(f32)
    ci = c_im.astype(f32)
    ca_re = cr * ab_re[:, None, :] - ci * ab_im[:, None, :]
    ca_im = cr * ab_im[:, None, :] + ci * ab_re[:, None, :]
    cb = jnp.einsum('gop,gpc->goc', cr, bb_re) - jnp.einsum('gop,gpc->goc', ci, bb_im)

    bb4 = jnp.stack([abb_re, bb_re, abb_im, bb_im]).astype(bf16)
    bt = bb4.reshape(2, 2, 8, 4, S5_STATE, S5_GROUP)
    bt = jnp.transpose(bt, (0, 2, 1, 5, 3, 4)).reshape(2, 8, 2 * S5_GROUP, 4 * S5_STATE)
    row = jnp.arange(2 * LANE)
    pick = ((row // LANE) * S5_GROUP + row % S5_GROUP)[:, None] == jnp.arange(2 * S5_GROUP)[None, :]
    spread = jnp.einsum('rk,anky->anry', pick.astype(bf16), bt, preferred_element_type=bf16)
    gl = (row % LANE) // S5_GROUP
    first_group = (4 * (jnp.arange(8) % 2))[:, None, None]
    own = gl[None, :, None] == first_group + (jnp.arange(4 * S5_STATE) // S5_STATE)[None, None, :]
    bmat = jnp.where(own, spread, jnp.zeros((), bf16)).reshape(16, 2 * LANE, 4 * S5_STATE)

    c4 = jnp.stack([cr, -ci, ca_re, -ca_im]).astype(bf16)
    ct = jnp.transpose(c4, (0, 1, 3, 2)).reshape(4, 2, 16 * S5_STATE, S5_GROUP)
    col = jnp.arange(16 * S5_GROUP)
    spread = (col[None, :] % S5_GROUP == jnp.arange(S5_GROUP)[:, None]).astype(bf16)
    own = (jnp.arange(16 * S5_STATE)[:, None] // S5_STATE == col[None, :] // S5_GROUP)
    cmat = jnp.where(own, jnp.einsum('qjro,oc->qjrc', ct, spread, preferred_element_type=bf16),
                     jnp.zeros((), bf16))
    cbt = jnp.transpose(cb, (0, 2, 1)).reshape(2, 16, S5_GROUP, S5_GROUP)
    cbmat = jnp.einsum('gh,jgco->jgcho', jnp.eye(16, dtype=f32), cbt).reshape(2, 256, 256).astype(bf16)
    a2_re = (ab_re * ab_re - ab_im * ab_im).reshape(N_SLAB, 1, LANE)
    a2_im = (2.0 * ab_re * ab_im).reshape(N_SLAB, 1, LANE)
    return bmat, cmat, cbmat, a2_re, a2_im


def _s5_body(u_ref, bm_ref, cm_ref, cbm_ref, are_ref, aim_ref, d_ref, y_ref,
             wbuf, hbuf, obuf, hstate, zcarry):
    tt = u_ref.shape[1]
    tp = tt // 2
    rows = SUBLANE * tp
    pitch = tp + S5_PITCH_PAD
    nl = D_S5 // LANE

    @pl.when(pl.program_id(0) == 0)
    def _():
        hstate[...] = jnp.zeros_like(hstate)
        zcarry[...] = jnp.zeros_like(zcarry)

    w = pltpu.bitcast(u_ref[...].reshape(SUBLANE * tt, D_S5), u32)
    for b in range(SUBLANE):
        for k in range(nl):
            wbuf[k, b * pitch:b * pitch + tp, :] = w[b * tp:(b + 1) * tp, k * LANE:(k + 1) * LANE]
    w = jnp.concatenate(
        [jnp.concatenate([wbuf[k, pl.ds(m, SUBLANE, stride=pitch), :] for m in range(tp)], axis=0)
         for k in range(nl)], axis=1)
    ue_f = pltpu.bitcast(w << 16, f32)
    uo_f = pltpu.bitcast(w & jnp.uint32(0xFFFF0000), f32)
    ue = ue_f.astype(bf16)
    uo = uo_f.astype(bf16)

    for n in range(2 * 8):
        ks = LANE * ((n % 8) // 2)
        lhs = jnp.concatenate([ue[:, ks:ks + LANE], uo[:, ks:ks + LANE]], axis=1)
        res = jnp.dot(lhs, bm_ref[n], preferred_element_type=f32)
        hbuf[2 * n] = res[:, :LANE]
        hbuf[2 * n + 1] = res[:, LANE:]

    for c0 in range(0, N_SLAB, S5_SLABS):
        ar = [jnp.broadcast_to(are_ref[c0 + s], (SUBLANE, LANE)) for s in range(S5_SLABS)]
        ai = [jnp.broadcast_to(aim_ref[c0 + s], (SUBLANE, LANE)) for s in range(S5_SLABS)]

        def step(t, carry):
            out = []
            sl = pl.ds(pl.multiple_of(t * SUBLANE, SUBLANE), SUBLANE)
            for s in range(S5_SLABS):
                hr, hi = carry[2 * s], carry[2 * s + 1]
                nr = ar[s] * hr - ai[s] * hi + hbuf[c0 + s, sl, :]
                ni = ar[s] * hi + ai[s] * hr + hbuf[N_SLAB + c0 + s, sl, :]
                hbuf[c0 + s, sl, :] = nr
                hbuf[N_SLAB + c0 + s, sl, :] = ni
                out += [nr, ni]
            return tuple(out)

        init = []
        for s in range(S5_SLABS):
            init += [hstate[c0 + s], hstate[N_SLAB + c0 + s]]
        fin = lax.fori_loop(0, tp, step, tuple(init), unroll=4)
        for s in range(S5_SLABS):
            hstate[c0 + s] = fin[2 * s]
            hstate[N_SLAB + c0 + s] = fin[2 * s + 1]

    for j in range(2):
        h_re = jnp.concatenate([hbuf[8 * j + s] for s in range(8)], axis=1).astype(bf16)
        h_im = jnp.concatenate([hbuf[N_SLAB + 8 * j + s] for s in range(8)], axis=1).astype(bf16)
        cs = slice(256 * j, 256 * j + 256)
        dj = d_ref[:, cs]
        yo = jnp.dot(h_re, cm_ref[0, j], preferred_element_type=f32)
        yo = yo + jnp.dot(h_im, cm_ref[1, j], preferred_element_type=f32)
        yo = yo + dj * uo_f[:, cs]
        z = jnp.dot(h_re, cm_ref[2, j], preferred_element_type=f32)
        z = z + jnp.dot(h_im, cm_ref[3, j], preferred_element_type=f32)
        zs = jnp.concatenate([zcarry[:, cs], z[:rows - SUBLANE, :]], axis=0)
        zcarry[:, cs] = z[rows - SUBLANE:, :]
        ye = zs + jnp.dot(ue[:, cs], cbm_ref[j], preferred_element_type=f32) + dj * ue_f[:, cs]
        ge = pltpu.bitcast(jax.nn.gelu(ye).astype(bf16).astype(f32), u32)
        go = pltpu.bitcast(jax.nn.gelu(yo).astype(bf16).astype(f32), u32)
        packed = (ge >> 16) | go
        obuf[2 * j] = packed[:, :LANE]
        obuf[2 * j + 1] = packed[:, LANE:]

    for b in range(SUBLANE):
        yb = jnp.concatenate(
            [jnp.concatenate([obuf[k, pl.ds(8 * SUBLANE * i + b, SUBLANE, stride=SUBLANE), :]
                              for i in range(tp // SUBLANE)], axis=0) for k in range(nl)], axis=1)
        y_ref[b] = pltpu.bitcast(yb, bf16)


def _s5(l, u3, bmat, cmat, cbmat, a2_re, a2_im, d_skip):
    bsz, seq, _ = u3.shape
    assert bsz == SUBLANE
    tt = TT_S5
    blk = lambda i: (0, i, 0)
    return pl.pallas_call(
        _s5_body,
        grid=(seq // tt,),
        in_specs=[pl.BlockSpec((bsz, tt, D_S5), blk)]
        + [_layer(a, l) for a in (bmat, cmat, cbmat, a2_re, a2_im, d_skip)],
        out_specs=pl.BlockSpec((bsz, tt, D_S5), blk),
        out_shape=SDS((bsz, seq, D_S5), bf16),
        scratch_shapes=[pltpu.VMEM((D_S5 // LANE, bsz * (tt // 2 + S5_PITCH_PAD), LANE), u32),
                        pltpu.VMEM((2 * N_SLAB, bsz * (tt // 2), LANE), f32),
                        pltpu.VMEM((D_S5 // LANE, bsz * (tt // 2), LANE), u32),
                        pltpu.VMEM((2 * N_SLAB, SUBLANE, LANE), f32),
                        pltpu.VMEM((SUBLANE, D_S5), f32)],
        compiler_params=_cparams("arbitrary"),
        name="s5_scan",
    )(u3, bmat, cmat, cbmat, a2_re, a2_im, d_skip)


def _split2(x):
    hi = x.astype(bf16)
    lo = (x - hi.astype(f32)).astype(bf16)
    return hi, lo


def _log_sigmoid(x):
    return -(jnp.maximum(-x, 0.0) + jnp.log1p(jnp.exp(-jnp.abs(x))))


def _gla_body(q_ref, kt_ref, v_ref, r_ref, g_ref, wa_ref, ba_ref, gn_ref,
              o_ref, s_ref):
    c = GLA_CHUNK
    tg = q_ref.shape[0]

    @pl.when(pl.program_id(1) == 0)
    def _():
        s_ref[...] = jnp.zeros_like(s_ref)

    ts = GLA_SUB
    nc = ts // c
    nh = GLA_HEADS
    ri = lax.broadcasted_iota(i32, (ts, ts), 0)
    ci = lax.broadcasted_iota(i32, (ts, ts), 1)
    same = (ri // c) == (ci // c)
    tril = (same & (ri >= ci)).astype(bf16)
    head_of_lane = lax.broadcasted_iota(i32, (c, D_GLA_K), 1) // GLA_DK
    r_idx = lax.broadcasted_iota(i32, (nc * nh * c, ts), 0)
    c_idx = lax.broadcasted_iota(i32, (nc * nh * c, ts), 1)
    causal = ((r_idx // c) % nc == c_idx // c) & (r_idx % c >= c_idx % c)
    chunk_of_lane = lax.broadcasted_iota(i32, (D_GLA_K, ts), 1) // c
    chunk_of_lane_h = lax.broadcasted_iota(i32, (GLA_DK, ts), 1) // c

    nsub = tg // ts
    subs = [slice(st * ts, (st + 1) * ts) for st in range(nsub)]

    def decay_stage(rs):
        g = g_ref[rs, :]
        la = _log_sigmoid(jnp.dot(g, wa_ref[...], preferred_element_type=f32) + ba_ref[...]) / GLA_TAU
        la_hi, la_lo = _split2(la)
        cum = (jnp.dot(tril, la_hi, preferred_element_type=f32)
               + jnp.dot(tril, la_lo, preferred_element_type=f32))
        cumt = cum.T
        ends = [cumt[:, (cc + 1) * c - 1:(cc + 1) * c] for cc in range(nc)]
        clt = ends[nc - 1]
        for cc in range(nc - 2, -1, -1):
            clt = jnp.where(chunk_of_lane <= cc, ends[cc], clt)
        qt = q_ref[rs, :].astype(f32) * jnp.exp(cum)
        ktt = kt_ref[:, rs].astype(f32)
        k_t = (ktt * jnp.exp(-cumt)).astype(bf16)
        k_end = ktt * jnp.exp(clt - cumt)
        q_stack = jnp.concatenate(
            [jnp.where(head_of_lane == h, qt[cc * c:(cc + 1) * c, :], 0.0)
             for h in range(nh) for cc in range(nc)], axis=0).astype(bf16)
        k_stack = [jnp.concatenate(
            [jnp.where(chunk_of_lane_h == cc, k_end[h * GLA_DK:(h + 1) * GLA_DK, :], 0.0)
             for cc in range(nc)], axis=0).astype(bf16) for h in range(nh)]
        dec = [jnp.exp(clt[:, cc * c:cc * c + 1]) for cc in range(nc)]
        return q_stack, k_t, k_stack, dec

    def matmul_stage(rs, staged, state):
        q_stack, k_t, k_stack, dec = staged
        scores = jnp.dot(q_stack, k_t, preferred_element_type=f32)
        scores = jnp.where(causal, scores, 0.0).astype(bf16)
        o_intra, upd_h = [], []
        for h in range(nh):
            vh = v_ref[rs, h * GLA_DV:(h + 1) * GLA_DV]
            o_intra.append(jnp.dot(scores[h * nc * c:(h + 1) * nc * c, :], vh,
                                   preferred_element_type=f32))
            upd_h.append(jnp.dot(k_stack[h], vh, preferred_element_type=f32))
        o_inter = []
        for cc in range(nc):
            q_cc = jnp.concatenate([q_stack[(h * nc + cc) * c:(h * nc + cc + 1) * c, :] for h in range(nh)],
                                   axis=0)
            o_inter.append(jnp.dot(q_cc, state.astype(bf16), preferred_element_type=f32))
            upd = jnp.concatenate([upd_h[h][cc * GLA_DK:(cc + 1) * GLA_DK, :] for h in range(nh)], axis=0)
            state = dec[cc] * state + upd
        return o_intra, o_inter, state

    def output_stage(rs, o_intra, o_inter):
        rows = []
        for cc in range(nc):
            outs = []
            for h in range(nh):
                o = o_intra[h][cc * c:(cc + 1) * c, :] + o_inter[cc][h * c:(h + 1) * c, :]
                o = o * lax.rsqrt(jnp.mean(o * o, axis=-1, keepdims=True) + EPS)
                outs.append(o)
            rows.append(jnp.concatenate(outs, axis=1))
        o_cat = jnp.concatenate(rows, axis=0)
        o_ref[rs, :] = (o_cat * gn_ref[...] * jax.nn.silu(r_ref[rs, :].astype(f32))).astype(bf16)

    state = s_ref[...]
    staged = decay_stage(subs[0])
    for st in range(nsub):
        nxt = decay_stage(subs[st + 1]) if st + 1 < nsub else None
        o_full, o_inter, state = matmul_stage(subs[st], staged, state)
        output_stage(subs[st], o_full, o_inter)
        staged = nxt
    s_ref[...] = state


def _gla(l, q3, kt3, v3, r3, g3, wa, ba, gn):
    bsz, seq, _ = q3.shape
    tg = TG_GLA
    tok = lambda b, i: (b, i, 0)
    return pl.pallas_call(
        _gla_body,
        grid=(bsz, seq // tg),
        in_specs=[pl.BlockSpec((None, tg, D_GLA_K), tok),
                  pl.BlockSpec((None, D_GLA_K, tg), lambda b, i: (b, 0, i)),
                  pl.BlockSpec((None, tg, D_GLA), tok),
                  pl.BlockSpec((None, tg, D_GLA), tok),
                  pl.BlockSpec((None, tg, LANE), tok),
                  _layer(wa, l), _layer(ba, l), _layer(gn, l)],
        out_specs=pl.BlockSpec((None, tg, D_GLA), tok),
        out_shape=SDS((bsz, seq, D_GLA), bf16),
        scratch_shapes=[pltpu.VMEM((D_GLA_K, GLA_DV), f32)],
        compiler_params=_cparams("parallel", "arbitrary"),
        name="gla",
    )(q3, kt3, v3, r3, g3, wa, ba, gn)


def _mix_out(x_ref, ys_ref, yg_ref, wgl_ref, bg_ref, gs_ref, wo_ref):
    y = ys_ref[...]
    z = jnp.dot(y, wgl_ref[...], preferred_element_type=f32) + bg_ref[...]
    yf = y.astype(f32) * jax.nn.sigmoid(z)
    ys = _rms(yf, gs_ref[...]).astype(bf16)
    acc = jnp.dot(ys, wo_ref[0:D_S5, :], preferred_element_type=f32)
    acc = acc + jnp.dot(yg_ref[...], wo_ref[D_S5:, :], preferred_element_type=f32)
    return x_ref[...] + acc


def _mix_specs(l, tm, w_glu, b_glu, g_s5, w_out):
    row = lambda i: (i, 0)
    once = dict(pipeline_mode=pl.Buffered(1))
    return [pl.BlockSpec((tm, D_MODEL), row),
            pl.BlockSpec((tm, D_S5), row),
            pl.BlockSpec((tm, D_GLA), row),
            _layer(w_glu, l, **once), _layer(b_glu, l), _layer(g_s5, l), _layer(w_out, l, **once)]


def _mix_ffn_body(x_ref, ys_ref, yg_ref, wgl_ref, bg_ref, gs_ref, wo_ref,
                  g_ref, wg_ref, wu_ref, wd_ref, o_ref):
    x = _mix_out(x_ref, ys_ref, yg_ref, wgl_ref, bg_ref, gs_ref, wo_ref)
    hn = _rms(x, g_ref[...]).astype(bf16)
    acc = x
    for lo, hi in FFN_CHUNKS:
        gate = jnp.dot(hn, wg_ref[:, lo:hi], preferred_element_type=f32)
        up = jnp.dot(hn, wu_ref[:, lo:hi], preferred_element_type=f32)
        act = (jax.nn.silu(gate) * up).astype(bf16)
        acc = acc + jnp.dot(act, wd_ref[lo:hi, :], preferred_element_type=f32)
    o_ref[...] = acc


def _mix_ffn(l, i_ffn, x2, ys, yg, w_glu, b_glu, g_s5, w_out, g, wg, wu, wd):
    n = x2.shape[0]
    tm = TM_FFN
    row = lambda i: (i, 0)
    once = dict(pipeline_mode=pl.Buffered(1))
    return pl.pallas_call(
        _mix_ffn_body,
        grid=(n // tm,),
        in_specs=_mix_specs(l, tm, w_glu, b_glu, g_s5, w_out) + [
            _layer(g, l), _layer(wg, i_ffn, **once), _layer(wu, i_ffn, **once),
            _layer(wd, i_ffn, **once)],
        out_specs=pl.BlockSpec((tm, D_MODEL), row),
        out_shape=SDS((n, D_MODEL), f32),
        compiler_params=_cparams("parallel"),
        name="mix_ffn",
    )(x2, ys, yg, w_glu, b_glu, g_s5, w_out, g, wg, wu, wd)


def _pack_bf16_pairs(a):
    bits = pltpu.bitcast(a.astype(bf16).astype(f32), u32)
    half = a.shape[1] // 2
    return bits[:, :half] | (bits[:, half:] >> 16)


def _unpack_bf16_pairs(p):
    hi = pltpu.bitcast(p & jnp.uint32(0xFFFF0000), f32).astype(bf16)
    lo = pltpu.bitcast(p << 16, f32).astype(bf16)
    return hi, lo


def _mix_router_body(x_ref, ys_ref, yg_ref, wgl_ref, bg_ref, gs_ref, wo_ref,
                     g_ref, wh_ref, wl_ref, xo_ref, hp_ref, meta_ref, wcol_ref, cnt_ref, carry):
    tm = x_ref.shape[0]

    @pl.when(pl.program_id(0) == 0)
    def _():
        carry[...] = jnp.zeros_like(carry)

    x = _mix_out(x_ref, ys_ref, yg_ref, wgl_ref, bg_ref, gs_ref, wo_ref)
    xo_ref[...] = x
    hn = _rms(x, g_ref[...])
    packed = _pack_bf16_pairs(hn)
    hp_ref[0] = packed[:, :SC_ROW]
    hp_ref[1] = packed[:, SC_ROW:]

    h_hi, h_lo = _split2(hn)
    nt = (((1,), (1,)), ((), ()))
    logits = (lax.dot_general(wh_ref[...], h_hi, nt, preferred_element_type=f32)
              + lax.dot_general(wh_ref[...], h_lo, nt, preferred_element_type=f32)
              + lax.dot_general(wl_ref[...], h_hi, nt, preferred_element_type=f32))
    er = logits.shape[0]
    row = lax.broadcasted_iota(i32, (er, tm), 0)
    neg = jnp.float32(-jnp.inf)
    logits = jnp.where(row < N_EXPERTS, logits, neg)
    m1 = jnp.max(logits, axis=0, keepdims=True)
    i1 = jnp.min(jnp.where(logits == m1, row, er), axis=0, keepdims=True)
    l2 = jnp.where(row == i1, neg, logits)
    m2 = jnp.max(l2, axis=0, keepdims=True)
    i2 = jnp.min(jnp.where(l2 == m2, row, er), axis=0, keepdims=True)
    e21 = jnp.exp(m2 - m1)
    w1 = 1.0 / (1.0 + e21)
    w2 = e21 / (1.0 + e21)

    sel1 = row == i1
    sel2 = row == i2
    sel = (sel1 | sel2).astype(f32)
    ri = lax.broadcasted_iota(i32, (tm, tm), 0)
    ci = lax.broadcasted_iota(i32, (tm, tm), 1)
    triu = (ri <= ci).astype(bf16)
    incl = jnp.dot(sel.astype(bf16), triu, preferred_element_type=f32)
    rank = incl - sel + carry[:, 0:1]
    r1 = jnp.sum(jnp.where(sel1, rank, 0.0), axis=0, keepdims=True)
    r2 = jnp.sum(jnp.where(sel2, rank, 0.0), axis=0, keepdims=True)
    new_cnt = carry[:, 0:1] + incl[:, tm - 1:tm]
    carry[...] = jnp.broadcast_to(new_cnt, carry.shape)
    cnt_ref[...] = jnp.broadcast_to(new_cnt, cnt_ref.shape)

    srow = lax.broadcasted_iota(i32, (SUBLANE, tm), 0)
    meta = jnp.where(srow == 0, i1.astype(f32), 0.0)
    meta = jnp.where(srow == 1, i2.astype(f32), meta)
    meta = jnp.where(srow == 2, r1, meta)
    meta = jnp.where(srow == 3, r2, meta)
    meta_ref[...] = meta
    prow = lax.broadcasted_iota(i32, (LANE, tm), 0)
    wpad = jnp.where(prow == 0, w1, jnp.where(prow == 1, w2, 0.0))
    wcol_ref[...] = wpad.T


def _mix_router(l, x2, ys, yg, w_glu, b_glu, g_s5, w_out, g, w_hi, w_lo):
    n = x2.shape[0]
    tm = TM_ROUTE
    row = lambda i: (i, 0)
    const = lambda i: (0, 0)
    return pl.pallas_call(
        _mix_router_body,
        grid=(n // tm,),
        in_specs=_mix_specs(l, tm, w_glu, b_glu, g_s5, w_out) + [
            _layer(g, l),
            pl.BlockSpec(w_hi.shape, const),
            pl.BlockSpec(w_lo.shape, const)],
        out_specs=[pl.BlockSpec((tm, D_MODEL), row),
                   pl.BlockSpec((2, tm, SC_ROW), lambda i: (0, i, 0)),
                   pl.BlockSpec((SUBLANE, tm), lambda i: (0, i)),
                   pl.BlockSpec((tm, LANE), row),
                   pl.BlockSpec(w_hi.shape[:1] + (LANE,), const)],
        out_shape=[SDS((n, D_MODEL), f32), SDS((2, n, SC_ROW), u32), SDS((SUBLANE, n), f32),
                   SDS((n, LANE), f32), SDS(w_hi.shape[:1] + (LANE,), f32)],
        scratch_shapes=[pltpu.VMEM(w_hi.shape[:1] + (LANE,), f32)],
        compiler_params=_cparams("arbitrary"),
        name="mix_router",
    )(x2, ys, yg, w_glu, b_glu, g_s5, w_out, g, w_hi, w_lo)


def _sc_gather(table, idx):
    ni = idx.shape[0]
    mesh = plsc.VectorSubcoreMesh(core_axis_name="core", subcore_axis_name="subcore")
    idx2 = idx.reshape(1, ni)

    @pl.kernel(out_type=SDS((ni, SC_ROW), table.dtype), mesh=mesh)
    def kern(t_hbm, i_hbm, o_hbm):
        def body(i_vmem, o_vmem):
            pltpu.sync_copy(t_hbm.at[i_vmem.at[0]], o_vmem)

        pltpu.emit_pipeline(
            body, grid=(ni // SC_WINDOW,),
            in_specs=[pl.BlockSpec((1, SC_WINDOW), index_map=lambda i: (0, i))],
            out_specs=[pl.BlockSpec((SC_WINDOW, SC_ROW), index_map=lambda i: (i, 0))],
            core_axis_name=("core", "subcore"),
            dimension_semantics=(pltpu.PARALLEL,),
        )(i_hbm, o_hbm)

    return kern(table, idx2)


def _sc_scatter2(x, idx0, idx1, nrows):
    ni = x.shape[0]
    mesh = plsc.VectorSubcoreMesh(core_axis_name="core", subcore_axis_name="subcore")

    @pl.kernel(out_type=SDS((nrows, SC_ROW), x.dtype), mesh=mesh)
    def kern(x_hbm, i0_hbm, i1_hbm, o_hbm):
        def body(x_vmem, i0_vmem, i1_vmem):
            pltpu.sync_copy(x_vmem, o_hbm.at[i0_vmem.at[0]])
            pltpu.sync_copy(x_vmem, o_hbm.at[i1_vmem.at[0]])

        pltpu.emit_pipeline(
            body, grid=(ni // SC_WINDOW,),
            in_specs=[pl.BlockSpec((SC_WINDOW, SC_ROW), index_map=lambda i: (i, 0)),
                      pl.BlockSpec((1, SC_WINDOW), index_map=lambda i: (0, i)),
                      pl.BlockSpec((1, SC_WINDOW), index_map=lambda i: (0, i))],
            out_specs=[],
            core_axis_name=("core", "subcore"),
            dimension_semantics=(pltpu.PARALLEL,),
        )(x_hbm, i0_hbm, i1_hbm)

    return kern(x, idx0.reshape(1, ni), idx1.reshape(1, ni))


def _moe_ffn_body(be_ref, bn_ref, xs_ref, wg_ref, wu_ref, wd_ref, ys_ref, xb, act0, act1, acc):
    i = pl.program_id(0)
    f = pl.program_id(1)
    nf = pl.num_programs(1) - 1
    nvalid = bn_ref[i]
    sub = xs_ref.shape[1] // MOE_SUB

    def emit(start, size, cond):
        rs = pl.ds(start, size)

        def gate_up():
            x = xb[rs, :]
            gate = jnp.dot(x, wg_ref[...].astype(bf16), preferred_element_type=f32)
            up = jnp.dot(x, wu_ref[...].astype(bf16), preferred_element_type=f32)
            return (jax.nn.silu(gate) * up).astype(bf16)

        def down(act_ref):
            return jnp.dot(act_ref[rs, :], wd_ref[...].astype(bf16), preferred_element_type=f32)

        @pl.when(cond & (f == 0))
        def _():
            live = lax.broadcasted_iota(i32, (size, SC_ROW), 0) + start < nvalid
            hi0, lo0 = _unpack_bf16_pairs(jnp.where(live, xs_ref[0, rs, :], jnp.uint32(0)))
            hi1, lo1 = _unpack_bf16_pairs(jnp.where(live, xs_ref[1, rs, :], jnp.uint32(0)))
            xb[rs, :] = jnp.concatenate([hi0, hi1, lo0, lo1], axis=1)
            acc[rs, :] = jnp.zeros((size, D_MODEL), f32)
            act0[rs, :] = gate_up()

        @pl.when(cond & (f > 0) & (f < nf) & (f % 2 == 1))
        def _():
            act1[rs, :] = gate_up()
            acc[rs, :] += down(act0)

        @pl.when(cond & (f > 0) & (f < nf) & (f % 2 == 0))
        def _():
            act0[rs, :] = gate_up()
            acc[rs, :] += down(act1)

        @pl.when(cond & (f == nf))
        def _():
            last = act0 if (D_FF_EXPERT // TF_MOE - 1) % 2 == 0 else act1
            packed = _pack_bf16_pairs(acc[rs, :] + down(last))
            ys_ref[0, rs, :] = packed[:, :SC_ROW]
            ys_ref[1, rs, :] = packed[:, SC_ROW:]

    def zero_out(start, size, cond):
        @pl.when(cond & (f == nf))
        def _():
            ys_ref[:, pl.ds(start, size), :] = jnp.zeros((2, size, SC_ROW), u32)

    for h in range(MOE_SUB):
        live = nvalid - h * sub
        whole = live > sub // 2
        head = (live > 0) & (live <= sub // 2)
        emit(h * sub, sub, whole)
        emit(h * sub, sub // 2, head)
        zero_out(h * sub + sub // 2, sub // 2, head)
        zero_out(h * sub, sub, live <= 0)


def _moe_ffn(i_moe, blk_e, blk_n, xs, wg, wu, wd):
    npad = xs.shape[1]
    tm, tf = TM_MOE, TF_MOE
    nblk = npad // tm
    nf = D_FF_EXPERT // tf

    def nxt(i):
        return jnp.minimum(i + 1, nblk - 1)

    def x_idx(i, f, be, bn):
        return (0, jnp.where(f == nf, nxt(i), i), 0)

    def gu_idx(i, f, be, bn):
        ahead = (f == nf) & (bn[i] > 0) & (bn[nxt(i)] > 0)
        e = jnp.where(ahead, be[nxt(i)], be[i])
        t = jnp.where(ahead, 0, jnp.where(bn[i] > 0, jnp.minimum(f, nf - 1), nf - 1))
        return (i_moe, e, 0, t)

    def d_idx(i, f, be, bn):
        prev = jnp.maximum(i - 1, 0)
        keep = (f == 0) & (i > 0)
        e = jnp.where(keep, be[prev], be[i])
        t = jnp.where(keep | (bn[i] == 0), nf - 1, jnp.maximum(f - 1, 0))
        return (i_moe, e, t, 0)

    grid_spec = pltpu.PrefetchScalarGridSpec(
        num_scalar_prefetch=2,
        grid=(nblk, nf + 1),
        in_specs=[pl.BlockSpec((2, tm, SC_ROW), x_idx),
                  pl.BlockSpec((None, None, D_MODEL, tf), gu_idx),
                  pl.BlockSpec((None, None, D_MODEL, tf), gu_idx),
                  pl.BlockSpec((None, None, tf, D_MODEL), d_idx)],
        out_specs=pl.BlockSpec((2, tm, SC_ROW), lambda i, f, be, bn: (0, i, 0)),
        scratch_shapes=[pltpu.VMEM((tm, D_MODEL), bf16), pltpu.VMEM((tm, tf), bf16),
                        pltpu.VMEM((tm, tf), bf16), pltpu.VMEM((tm, D_MODEL), f32)],
    )
    return pl.pallas_call(
        _moe_ffn_body,
        grid_spec=grid_spec,
        out_shape=SDS((2, npad, SC_ROW), u32),
        compiler_params=_cparams("parallel", "arbitrary"),
        name="moe_ffn",
    )(blk_e, blk_n, xs, wg, wu, wd)


def _combine_body(x_ref, yg_ref, wcol_ref, g_ref, o_ref, *, final_norm):
    w1 = wcol_ref[:, 0:1]
    w2 = wcol_ref[:, 1:2]

    def rows(k):
        hi0, lo0 = _unpack_bf16_pairs(yg_ref[0, k])
        hi1, lo1 = _unpack_bf16_pairs(yg_ref[1, k])
        return jnp.concatenate([hi0, hi1, lo0, lo1], axis=1).astype(f32)

    xo = x_ref[...] + (w1 * rows(0) + w2 * rows(1))
    if final_norm:
        xo = _rms(xo, g_ref[...])
    o_ref[...] = xo


def _combine(x2, yg, wcol, g_final, final_norm):
    n = x2.shape[0]
    tm = TM_ROUTE
    row = lambda i: (i, 0)
    return pl.pallas_call(
        functools.partial(_combine_body, final_norm=final_norm),
        grid=(n // tm,),
        in_specs=[pl.BlockSpec((tm, D_MODEL), row),
                  pl.BlockSpec((2, 2, tm, SC_ROW), lambda i: (0, 0, i, 0)),
                  pl.BlockSpec((tm, LANE), row),
                  pl.BlockSpec((1, D_MODEL), lambda i: (0, 0))],
        out_specs=pl.BlockSpec((tm, D_MODEL), row),
        out_shape=SDS((n, D_MODEL), f32),
        compiler_params=_cparams("parallel"),
        name="moe_combine",
    )(x2, yg, wcol, g_final)


def _moe_layer(l, i_moe, mix_args, g_ffn, w_router, wg, wu, wd, g_final, final_norm):
    n = mix_args[0].shape[0]
    tm = TM_MOE
    npad = 2 * n + N_EXPERTS * tm
    wr = jnp.zeros((2 * SUBLANE, D_MODEL), f32).at[:N_EXPERTS].set(w_router[i_moe].astype(f32).T)
    wr_hi = wr.astype(bf16)
    wr_lo = (wr - wr_hi.astype(f32)).astype(bf16)
    x2, hp, meta, wcol, cnt = _mix_router(l, *mix_args, g_ffn, wr_hi, wr_lo)

    counts = cnt[:N_EXPERTS, 0].astype(i32)
    padded = ((counts + tm - 1) // tm) * tm
    ends = jnp.cumsum(padded)
    offs = ends - padded
    ids = jnp.arange(N_EXPERTS, dtype=i32)
    e12 = meta[0:2].astype(i32)
    r12 = meta[2:4].astype(i32)
    pos_t = r12 + jnp.sum(jnp.where(e12[..., None] == ids, offs, 0), axis=-1)
    blk_start = jnp.arange(npad // tm, dtype=i32) * tm
    blk_e = jnp.minimum(jnp.sum((blk_start[:, None] >= ends[None, :]).astype(i32), axis=1), N_EXPERTS - 1)
    own = blk_e[:, None] == ids
    blk_end = jnp.sum(jnp.where(own, offs + counts, 0), axis=1)
    blk_n = jnp.where(blk_start < ends[-1], jnp.clip(blk_end - blk_start, 0, tm), 0)
    last_e = jnp.max(jnp.where(blk_n > 0, blk_e, 0))
    blk_e = jnp.where(blk_n > 0, blk_e, last_e)

    xs = _sc_scatter2(hp.reshape(2 * n, SC_ROW),
                      jnp.concatenate([pos_t[0], pos_t[0] + npad]),
                      jnp.concatenate([pos_t[1], pos_t[1] + npad]),
                      2 * npad).reshape(2, npad, SC_ROW)
    ys = _moe_ffn(i_moe, blk_e, blk_n, xs, wg, wu, wd)
    gidx = jnp.concatenate([pos_t.reshape(-1), pos_t.reshape(-1) + npad])
    yg = _sc_gather(ys.reshape(2 * npad, SC_ROW), gidx).reshape(2, 2, n, SC_ROW)
    return _combine(x2, yg, wcol, g_final, final_norm)


def _rows(v):
    return v.astype(f32)[:, None, :]


def kernel(x, norm_mix, w_in, s5_lambda_re, s5_lambda_im, s5_log_dt, s5_b_re, s5_b_im, s5_c_re, s5_c_im, s5_d, s5_w_glu, s5_b_glu, s5_out_norm, gla_w_a2, gla_b_a2, gla_out_norm, w_out, norm_ffn, ffn_w_gate, ffn_w_up, ffn_w_down, moe_w_router, moe_w_gate, moe_w_up, moe_w_down, norm_final):
    bsz, seq, _ = x.shape
    n = bsz * seq
    depth = w_in.shape[0]
    x2 = x.reshape(n, D_MODEL)

    w_gp = jnp.zeros((depth, D_MODEL, LANE), f32).at[:, :, :GLA_GATE_RANK].set(w_in[:, :, 2048:2064])
    w_cat = jnp.concatenate([w_in[:, :, 0:512], w_in[:, :, 512:768] * (GLA_DK ** -0.5),
                             w_in[:, :, 1024:1536], w_in[:, :, 1536:2048], w_gp], axis=2).astype(bf16)
    w_kt = jnp.swapaxes(w_in[:, :, 768:1024], 1, 2).astype(bf16)
    s5_mats = jax.vmap(_s5_prep)(s5_lambda_re, s5_lambda_im, s5_log_dt, s5_b_re, s5_b_im, s5_c_re, s5_c_im)
    wa = jnp.zeros((depth, LANE, D_GLA_K), f32).at[:, :GLA_GATE_RANK].set(gla_w_a2).astype(bf16)
    g_mix, g_ffn, d_skip = _rows(norm_mix), _rows(norm_ffn), _rows(s5_d)
    ba, gn = _rows(gla_b_a2), _rows(gla_out_norm)
    mix_w = (s5_w_glu.astype(bf16), _rows(s5_b_glu), _rows(s5_out_norm), w_out.astype(bf16))
    ffn_w = (ffn_w_gate.astype(bf16), ffn_w_up.astype(bf16), ffn_w_down.astype(bf16))
    g_final = norm_final.astype(f32).reshape(1, D_MODEL)

    for l in range(depth):
        u, q, kt, v, r, gl = _inproj(l, x2, g_mix, w_cat, w_kt, bsz, seq)
        ys = _s5(l, u.reshape(bsz, seq, D_S5), *s5_mats, d_skip)
        yg = _gla(l, q.reshape(bsz, seq, D_GLA_K), kt, v.reshape(bsz, seq, D_GLA),
                  r.reshape(bsz, seq, D_GLA), gl.reshape(bsz, seq, LANE), wa, ba, gn)
        mix_args = (x2, ys.reshape(n, D_S5), yg.reshape(n, D_GLA)) + mix_w

        last = l == depth - 1
        if l % 2 == 0:
            x2 = _mix_ffn(l, l // 2, *mix_args, g_ffn, *ffn_w)
            if last:
                x2 = _final_norm(x2, g_final)
        else:
            x2 = _moe_layer(l, l // 2, mix_args, g_ffn, moe_w_router, moe_w_gate, moe_w_up,
                            moe_w_down, g_final, last)
    return x2.reshape(bsz, seq, D_MODEL)


def _final_norm_body(x_ref, g_ref, o_ref):
    o_ref[...] = _rms(x_ref[...], g_ref[...])


def _final_norm(x2, g):
    n = x2.shape[0]
    tm = TM_ROUTE
    return pl.pallas_call(
        _final_norm_body,
        grid=(n // tm,),
        in_specs=[pl.BlockSpec((tm, D_MODEL), lambda i: (i, 0)),
                  pl.BlockSpec((1, D_MODEL), lambda i: (0, 0))],
        out_specs=pl.BlockSpec((tm, D_MODEL), lambda i: (i, 0)),
        out_shape=SDS((n, D_MODEL), f32),
        compiler_params=_cparams("parallel"),
        name="final_norm",
    )(x2, g)
```

```python
import functools

import jax
import jax.numpy as jnp
from jax import lax
from jax.experimental import pallas as pl
from jax.experimental.pallas import tpu as pltpu
from jax.experimental.pallas import tpu_sc as plsc

f32 = jnp.float32
bf16 = jnp.bfloat16
u32 = jnp.uint32
i32 = jnp.int32
SDS = jax.ShapeDtypeStruct

D_MODEL = 1024
D_S5 = 512
S5_GROUP = 16
S5_GROUPS = 32
S5_STATE = 64
N_STATE = S5_GROUPS * S5_STATE
D_GLA = 512
GLA_HEADS = 4
GLA_DV = 128
GLA_DK = 64
D_GLA_K = 256
GLA_GATE_RANK = 16
GLA_TAU = 16.0
GLA_CHUNK = 64
D_FF = 2816
N_EXPERTS = 8
D_FF_EXPERT = 3584
EPS = 1e-6

LANE = 128
SUBLANE = 8
VMEM_LIMIT = 56 * 1024 * 1024

TM_PROJ = 2048
TT_S5 = 256
S5_SLABS = 8
S5_PITCH_PAD = 8
TG_GLA = 1024
GLA_SUB = 256
TM_FFN = 1024
FFN_CHUNKS = ((0, 1536), (1536, 2816))
TM_MOE = 2048
MOE_SUB = 2
TF_MOE = 512
TM_ROUTE = 1024
SC_WINDOW = 128
SC_ROW = 256


def _cparams(*sem):
    return pltpu.CompilerParams(dimension_semantics=sem, vmem_limit_bytes=VMEM_LIMIT)


def _layer(arr, l, **kw):
    return pl.BlockSpec((None,) + arr.shape[1:], lambda *_: (l,) + (0,) * (arr.ndim - 1), **kw)


def _rms(x, g):
    ms = jnp.mean(x * x, axis=-1, keepdims=True)
    return x * lax.rsqrt(ms + EPS) * g


def _inproj_body(x_ref, g_ref, w_ref, wkt_ref, u_ref, q_ref, kt_ref, v_ref, r_ref, gl_ref):
    hn = _rms(x_ref[...], g_ref[...]).astype(bf16)

    def proj(lo, hi):
        return jnp.dot(hn, w_ref[:, lo:hi], preferred_element_type=f32).astype(bf16)

    u_ref[...] = proj(0, 512)
    q_ref[...] = proj(512, 768)
    v_ref[...] = proj(768, 1280)
    r_ref[...] = proj(1280, 1792)
    gl_ref[...] = proj(1792, 1920)
    kt_ref[...] = lax.dot_general(wkt_ref[...], hn, (((1,), (1,)), ((), ())),
                                  preferred_element_type=f32).astype(bf16)


def _inproj(l, x2, g, w_cat, w_kt, bsz, seq):
    n = x2.shape[0]
    tm = TM_PROJ
    per_b = seq // tm
    row = lambda i: (i, 0)
    return pl.pallas_call(
        _inproj_body,
        grid=(n // tm,),
        in_specs=[pl.BlockSpec((tm, D_MODEL), row),
                  _layer(g, l), _layer(w_cat, l), _layer(w_kt, l)],
        out_specs=[pl.BlockSpec((tm, D_S5), row),
                   pl.BlockSpec((tm, D_GLA_K), row),
                   pl.BlockSpec((None, D_GLA_K, tm), lambda i: (i // per_b, 0, i % per_b)),
                   pl.BlockSpec((tm, D_GLA), row),
                   pl.BlockSpec((tm, D_GLA), row),
                   pl.BlockSpec((tm, LANE), row)],
        out_shape=[SDS((n, D_S5), bf16), SDS((n, D_GLA_K), bf16),
                   SDS((bsz, D_GLA_K, seq), bf16), SDS((n, D_GLA), bf16),
                   SDS((n, D_GLA), bf16), SDS((n, LANE), bf16)],
        compiler_params=_cparams("parallel"),
        name="inproj",
    )(x2, g, w_cat, w_kt)


N_SLAB = N_STATE // LANE


def _s5_prep(lam_re, lam_im, log_dt, b_re, b_im, c_re, c_im):
    lr = jnp.minimum(lam_re.astype(f32), -1e-4)
    li = lam_im.astype(f32)
    dt = jnp.exp(log_dt.astype(f32))[:, None]
    mag = jnp.exp(lr * dt)
    ab_re = mag * jnp.cos(li * dt)
    ab_im = mag * jnp.sin(li * dt)
    nr = ab_re - 1.0
    ni = ab_im
    den = lr * lr + li * li
    f_re = (nr * lr + ni * li) / den
    f_im = (ni * lr - nr * li) / den
    br = b_re.astype(f32)
    bi = b_im.astype(f32)
    bb_re = f_re[..., None] * br - f_im[..., None] * bi
    bb_im = f_re[..., None] * bi + f_im[..., None] * br
    abb_re = ab_re[..., None] * bb_re - ab_im[..., None] * bb_im
    abb_im = ab_re[..., None] * bb_im + ab_im[..., None] * bb_re
    cr = c_re.astype(f32)
    ci = c_im.astype(f32)
    ca_re = cr * ab_re[:, None, :] - ci * ab_im[:, None, :]
    ca_im = cr * ab_im[:, None, :] + ci * ab_re[:, None, :]
    cb = jnp.einsum('gop,gpc->goc', cr, bb_re) - jnp.einsum('gop,gpc->goc', ci, bb_im)

    bb4 = jnp.stack([abb_re, bb_re, abb_im, bb_im]).astype(bf16)
    bt = bb4.reshape(2, 2, 8, 4, S5_STATE, S5_GROUP)
    bt = jnp.transpose(bt, (0, 2, 1, 5, 3, 4)).reshape(2, 8, 2 * S5_GROUP, 4 * S5_STATE)
    row = jnp.arange(2 * LANE)
    pick = ((row // LANE) * S5_GROUP + row % S5_GROUP)[:, None] == jnp.arange(2 * S5_GROUP)[None, :]
    spread = jnp.einsum('rk,anky->anry', pick.astype(bf16), bt, preferred_element_type=bf16)
    gl = (row % LANE) // S5_GROUP
    first_group = (4 * (jnp.arange(8) % 2))[:, None, None]
    own = gl[None, :, None] == first_group + (jnp.arange(4 * S5_STATE) // S5_STATE)[None, None, :]
    bmat = jnp.where(own, spread, jnp.zeros((), bf16)).reshape(16, 2 * LANE, 4 * S5_STATE)

    c4 = jnp.stack([cr, -ci, ca_re, -ca_im]).astype(bf16)
    ct = jnp.transpose(c4, (0, 1, 3, 2)).reshape(4, 2, 16 * S5_STATE, S5_GROUP)
    col = jnp.arange(16 * S5_GROUP)
    spread = (col[None, :] % S5_GROUP == jnp.arange(S5_GROUP)[:, None]).astype(bf16)
    own = (jnp.arange(16 * S5_STATE)[:, None] // S5_STATE == col[None, :] // S5_GROUP)
    cmat = jnp.where(own, jnp.einsum('qjro,oc->qjrc', ct, spread, preferred_element_type=bf16),
                     jnp.zeros((), bf16))
    cbt = jnp.transpose(cb, (0, 2, 1)).reshape(2, 16, S5_GROUP, S5_GROUP)
    cbmat = jnp.einsum('gh,jgco->jgcho', jnp.eye(16, dtype=f32), cbt).reshape(2, 256, 256).astype(bf16)
    a2_re = (ab_re * ab_re - ab_im * ab_im).reshape(N_SLAB, 1, LANE)
    a2_im = (2.0 * ab_re * ab_im).reshape(N_SLAB, 1, LANE)
    return bmat, cmat, cbmat, a2_re, a2_im


def _s5_body(u_ref, bm_ref, cm_ref, cbm_ref, are_ref, aim_ref, d_ref, y_ref,
             wbuf, hbuf, obuf, hstate, zcarry):
    tt = u_ref.shape[1]
    tp = tt // 2
    rows = SUBLANE * tp
    pitch = tp + S5_PITCH_PAD
    nl = D_S5 // LANE

    @pl.when(pl.program_id(0) == 0)
    def _():
        hstate[...] = jnp.zeros_like(hstate)
        zcarry[...] = jnp.zeros_like(zcarry)

    w = pltpu.bitcast(u_ref[...].reshape(SUBLANE * tt, D_S5), u32)
    for b in range(SUBLANE):
        for k in range(nl):
            wbuf[k, b * pitch:b * pitch + tp, :] = w[b * tp:(b + 1) * tp, k * LANE:(k + 1) * LANE]
    w = jnp.concatenate(
        [jnp.concatenate([wbuf[k, pl.ds(m, SUBLANE, stride=pitch), :] for m in range(tp)], axis=0)
         for k in range(nl)], axis=1)
    ue_f = pltpu.bitcast(w << 16, f32)
    uo_f = pltpu.bitcast(w & jnp.uint32(0xFFFF0000), f32)
    ue = ue_f.astype(bf16)
    uo = uo_f.astype(bf16)

    for n in range(2 * 8):
        ks = LANE * ((n % 8) // 2)
        lhs = jnp.concatenate([ue[:, ks:ks + LANE], uo[:, ks:ks + LANE]], axis=1)
        res = jnp.dot(lhs, bm_ref[n], preferred_element_type=f32)
        hbuf[2 * n] = res[:, :LANE]
        hbuf[2 * n + 1] = res[:, LANE:]

    for c0 in range(0, N_SLAB, S5_SLABS):
        ar = [jnp.broadcast_to(are_ref[c0 + s], (SUBLANE, LANE)) for s in range(S5_SLABS)]
        ai = [jnp.broadcast_to(aim_ref[c0 + s], (SUBLANE, LANE)) for s in range(S5_SLABS)]

        def step(t, carry):
            out = []
            sl = pl.ds(pl.multiple_of(t * SUBLANE, SUBLANE), SUBLANE)
            for s in range(S5_SLABS):
                hr, hi = carry[2 * s], carry[2 * s + 1]
                nr = ar[s] * hr - ai[s] * hi + hbuf[c0 + s, sl, :]
                ni = ar[s] * hi + ai[s] * hr + hbuf[N_SLAB + c0 + s, sl, :]
                hbuf[c0 + s, sl, :] = nr
                hbuf[N_SLAB + c0 + s, sl, :] = ni
                out += [nr, ni]
            return tuple(out)

        init = []
        for s in range(S5_SLABS):
            init += [hstate[c0 + s], hstate[N_SLAB + c0 + s]]
        fin = lax.fori_loop(0, tp, step, tuple(init), unroll=4)
        for s in range(S5_SLABS):
            hstate[c0 + s] = fin[2 * s]
            hstate[N_SLAB + c0 + s] = fin[2 * s + 1]

    for j in range(2):
        h_re = jnp.concatenate([hbuf[8 * j + s] for s in range(8)], axis=1).astype(bf16)
        h_im = jnp.concatenate([hbuf[N_SLAB + 8 * j + s] for s in range(8)], axis=1).astype(bf16)
        cs = slice(256 * j, 256 * j + 256)
        dj = d_ref[:, cs]
        yo = jnp.dot(h_re, cm_ref[0, j], preferred_element_type=f32)
        yo = yo + jnp.dot(h_im, cm_ref[1, j], preferred_element_type=f32)
        yo = yo + dj * uo_f[:, cs]
        z = jnp.dot(h_re, cm_ref[2, j], preferred_element_type=f32)
        z = z + jnp.dot(h_im, cm_ref[3, j], preferred_element_type=f32)
        zs = jnp.concatenate([zcarry[:, cs], z[:rows - SUBLANE, :]], axis=0)
        zcarry[:, cs] = z[rows - SUBLANE:, :]
        ye = zs + jnp.dot(ue[:, cs], cbm_ref[j], preferred_element_type=f32) + dj * ue_f[:, cs]
        ge = pltpu.bitcast(jax.nn.gelu(ye).astype(bf16).astype(f32), u32)
        go = pltpu.bitcast(jax.nn.gelu(yo).astype(bf16).astype(f32), u32)
        packed = (ge >> 16) | go
        obuf[2 * j] = packed[:, :LANE]
        obuf[2 * j + 1] = packed[:, LANE:]

    for b in range(SUBLANE):
        yb = jnp.concatenate(
            [jnp.concatenate([obuf[k, pl.ds(8 * SUBLANE * i + b, SUBLANE, stride=SUBLANE), :]
                              for i in range(tp // SUBLANE)], axis=0) for k in range(nl)], axis=1)
        y_ref[b] = pltpu.bitcast(yb, bf16)


def _s5(l, u3, bmat, cmat, cbmat, a2_re, a2_im, d_skip):
    bsz, seq, _ = u3.shape
    assert bsz == SUBLANE
    tt = TT_S5
    blk = lambda i: (0, i, 0)
    return pl.pallas_call(
        _s5_body,
        grid=(seq // tt,),
        in_specs=[pl.BlockSpec((bsz, tt, D_S5), blk)]
        + [_layer(a, l) for a in (bmat, cmat, cbmat, a2_re, a2_im, d_skip)],
        out_specs=pl.BlockSpec((bsz, tt, D_S5), blk),
        out_shape=SDS((bsz, seq, D_S5), bf16),
        scratch_shapes=[pltpu.VMEM((D_S5 // LANE, bsz * (tt // 2 + S5_PITCH_PAD), LANE), u32),
                        pltpu.VMEM((2 * N_SLAB, bsz * (tt // 2), LANE), f32),
                        pltpu.VMEM((D_S5 // LANE, bsz * (tt // 2), LANE), u32),
                        pltpu.VMEM((2 * N_SLAB, SUBLANE, LANE), f32),
                        pltpu.VMEM((SUBLANE, D_S5), f32)],
        compiler_params=_cparams("arbitrary"),
        name="s5_scan",
    )(u3, bmat, cmat, cbmat, a2_re, a2_im, d_skip)


def _split2(x):
    hi = x.astype(bf16)
    lo = (x - hi.astype(f32)).astype(bf16)
    return hi, lo


def _log_sigmoid(x):
    return -(jnp.maximum(-x, 0.0) + jnp.log1p(jnp.exp(-jnp.abs(x))))


def _gla_body(q_ref, kt_ref, v_ref, r_ref, g_ref, wa_ref, ba_ref, gn_ref,
              o_ref, s_ref):
    c = GLA_CHUNK
    tg = q_ref.shape[0]

    @pl.when(pl.program_id(1) == 0)
    def _():
        s_ref[...] = jnp.zeros_like(s_ref)

    ts = GLA_SUB
    nc = ts // c
    nh = GLA_HEADS
    ri = lax.broadcasted_iota(i32, (ts, ts), 0)
    ci = lax.broadcasted_iota(i32, (ts, ts), 1)
    same = (ri // c) == (ci // c)
    tril = (same & (ri >= ci)).astype(bf16)
    head_of_lane = lax.broadcasted_iota(i32, (c, D_GLA_K), 1) // GLA_DK
    r_idx = lax.broadcasted_iota(i32, (nc * nh * c, ts), 0)
    c_idx = lax.broadcasted_iota(i32, (nc * nh * c, ts), 1)
    causal = ((r_idx // c) % nc == c_idx // c) & (r_idx % c >= c_idx % c)
    chunk_of_lane = lax.broadcasted_iota(i32, (D_GLA_K, ts), 1) // c
    chunk_of_lane_h = lax.broadcasted_iota(i32, (GLA_DK, ts), 1) // c

    nsub = tg // ts
    subs = [slice(st * ts, (st + 1) * ts) for st in range(nsub)]

    def decay_stage(rs):
        g = g_ref[rs, :]
        la = _log_sigmoid(jnp.dot(g, wa_ref[...], preferred_element_type=f32) + ba_ref[...]) / GLA_TAU
        la_hi, la_lo = _split2(la)
        cum = (jnp.dot(tril, la_hi, preferred_element_type=f32)
               + jnp.dot(tril, la_lo, preferred_element_type=f32))
        cumt = cum.T
        ends = [cumt[:, (cc + 1) * c - 1:(cc + 1) * c] for cc in range(nc)]
        clt = ends[nc - 1]
        for cc in range(nc - 2, -1, -1):
            clt = jnp.where(chunk_of_lane <= cc, ends[cc], clt)
        qt = q_ref[rs, :].astype(f32) * jnp.exp(cum)
        ktt = kt_ref[:, rs].astype(f32)
        k_t = (ktt * jnp.exp(-cumt)).astype(bf16)
        k_end = ktt * jnp.exp(clt - cumt)
        q_stack = jnp.concatenate(
            [jnp.where(head_of_lane == h, qt[cc * c:(cc + 1) * c, :], 0.0)
             for h in range(nh) for cc in range(nc)], axis=0).astype(bf16)
        k_stack = [jnp.concatenate(
            [jnp.where(chunk_of_lane_h == cc, k_end[h * GLA_DK:(h + 1) * GLA_DK, :], 0.0)
             for cc in range(nc)], axis=0).astype(bf16) for h in range(nh)]
        dec = [jnp.exp(clt[:, cc * c:cc * c + 1]) for cc in range(nc)]
        return q_stack, k_t, k_stack, dec

    def matmul_stage(rs, staged, state):
        q_stack, k_t, k_stack, dec = staged
        scores = jnp.dot(q_stack, k_t, preferred_element_type=f32)
        scores = jnp.where(causal, scores, 0.0).astype(bf16)
        o_intra, upd_h = [], []
        for h in range(nh):
            vh = v_ref[rs, h * GLA_DV:(h + 1) * GLA_DV]
            o_intra.append(jnp.dot(scores[h * nc * c:(h + 1) * nc * c, :], vh,
                                   preferred_element_type=f32))
            upd_h.append(jnp.dot(k_stack[h], vh, preferred_element_type=f32))
        o_inter = []
        for cc in range(nc):
            q_cc = jnp.concatenate([q_stack[(h * nc + cc) * c:(h * nc + cc + 1) * c, :] for h in range(nh)],
                                   axis=0)
            o_inter.append(jnp.dot(q_cc, state.astype(bf16), preferred_element_type=f32))
            upd = jnp.concatenate([upd_h[h][cc * GLA_DK:(cc + 1) * GLA_DK, :] for h in range(nh)], axis=0)
            state = dec[cc] * state + upd
        return o_intra, o_inter, state

    def output_stage(rs, o_intra, o_inter):
        rows = []
        for cc in range(nc):
            outs = []
            for h in range(nh):
                o = o_intra[h][cc * c:(cc + 1) * c, :] + o_inter[cc][h * c:(h + 1) * c, :]
                o = o * lax.rsqrt(jnp.mean(o * o, axis=-1, keepdims=True) + EPS)
                outs.append(o)
            rows.append(jnp.concatenate(outs, axis=1))
        o_cat = jnp.concatenate(rows, axis=0)
        o_ref[rs, :] = (o_cat * gn_ref[...] * jax.nn.silu(r_ref[rs, :].astype(f32))).astype(bf16)

    state = s_ref[...]
    staged = decay_stage(subs[0])
    for st in range(nsub):
        nxt = decay_stage(subs[st + 1]) if st + 1 < nsub else None
        o_full, o_inter, state = matmul_stage(subs[st], staged, state)
        output_stage(subs[st], o_full, o_inter)
        staged = nxt
    s_ref[...] = state


def _gla(l, q3, kt3, v3, r3, g3, wa, ba, gn):
    bsz, seq, _ = q3.shape
    tg = TG_GLA
    tok = lambda b, i: (b, i, 0)
    return pl.pallas_call(
        _gla_body,
        grid=(bsz, seq // tg),
        in_specs=[pl.BlockSpec((None, tg, D_GLA_K), tok),
                  pl.BlockSpec((None, D_GLA_K, tg), lambda b, i: (b, 0, i)),
                  pl.BlockSpec((None, tg, D_GLA), tok),
                  pl.BlockSpec((None, tg, D_GLA), tok),
                  pl.BlockSpec((None, tg, LANE), tok),
                  _layer(wa, l), _layer(ba, l), _layer(gn, l)],
        out_specs=pl.BlockSpec((None, tg, D_GLA), tok),
        out_shape=SDS((bsz, seq, D_GLA), bf16),
        scratch_shapes=[pltpu.VMEM((D_GLA_K, GLA_DV), f32)],
        compiler_params=_cparams("parallel", "arbitrary"),
        name="gla",
    )(q3, kt3, v3, r3, g3, wa, ba, gn)


def _mix_out(x_ref, ys_ref, yg_ref, wgl_ref, bg_ref, gs_ref, wo_ref):
    y = ys_ref[...]
    z = jnp.dot(y, wgl_ref[...], preferred_element_type=f32) + bg_ref[...]
    yf = y.astype(f32) * jax.nn.sigmoid(z)
    ys = _rms(yf, gs_ref[...]).astype(bf16)
    acc = jnp.dot(ys, wo_ref[0:D_S5, :], preferred_element_type=f32)
    acc = acc + jnp.dot(yg_ref[...], wo_ref[D_S5:, :], preferred_element_type=f32)
    return x_ref[...] + acc


def _mix_specs(l, tm, w_glu, b_glu, g_s5, w_out):
    row = lambda i: (i, 0)
    once = dict(pipeline_mode=pl.Buffered(1))
    return [pl.BlockSpec((tm, D_MODEL), row),
            pl.BlockSpec((tm, D_S5), row),
            pl.BlockSpec((tm, D_GLA), row),
            _layer(w_glu, l, **once), _layer(b_glu, l), _layer(g_s5, l), _layer(w_out, l, **once)]


def _mix_ffn_body(x_ref, ys_ref, yg_ref, wgl_ref, bg_ref, gs_ref, wo_ref,
                  g_ref, wg_ref, wu_ref, wd_ref, o_ref):
    x = _mix_out(x_ref, ys_ref, yg_ref, wgl_ref, bg_ref, gs_ref, wo_ref)
    hn = _rms(x, g_ref[...]).astype(bf16)
    acc = x
    for lo, hi in FFN_CHUNKS:
        gate = jnp.dot(hn, wg_ref[:, lo:hi], preferred_element_type=f32)
        up = jnp.dot(hn, wu_ref[:, lo:hi], preferred_element_type=f32)
        act = (jax.nn.silu(gate) * up).astype(bf16)
        acc = acc + jnp.dot(act, wd_ref[lo:hi, :], preferred_element_type=f32)
    o_ref[...] = acc


def _mix_ffn(l, i_ffn, x2, ys, yg, w_glu, b_glu, g_s5, w_out, g, wg, wu, wd):
    n = x2.shape[0]
    tm = TM_FFN
    row = lambda i: (i, 0)
    once = dict(pipeline_mode=pl.Buffered(1))
    return pl.pallas_call(
        _mix_ffn_body,
        grid=(n // tm,),
        in_specs=_mix_specs(l, tm, w_glu, b_glu, g_s5, w_out) + [
            _layer(g, l), _layer(wg, i_ffn, **once), _layer(wu, i_ffn, **once),
            _layer(wd, i_ffn, **once)],
        out_specs=pl.BlockSpec((tm, D_MODEL), row),
        out_shape=SDS((n, D_MODEL), f32),
        compiler_params=_cparams("parallel"),
        name="mix_ffn",
    )(x2, ys, yg, w_glu, b_glu, g_s5, w_out, g, wg, wu, wd)


def _pack_bf16_pairs(a):
    bits = pltpu.bitcast(a.astype(bf16).astype(f32), u32)
    half = a.shape[1] // 2
    return bits[:, :half] | (bits[:, half:] >> 16)


def _unpack_bf16_pairs(p):
    hi = pltpu.bitcast(p & jnp.uint32(0xFFFF0000), f32).astype(bf16)
    lo = pltpu.bitcast(p << 16, f32).astype(bf16)
    return hi, lo


def _mix_router_body(x_ref, ys_ref, yg_ref, wgl_ref, bg_ref, gs_ref, wo_ref,
                     g_ref, wh_ref, wl_ref, xo_ref, hp_ref, meta_ref, wcol_ref, cnt_ref, carry):
    tm = x_ref.shape[0]

    @pl.when(pl.program_id(0) == 0)
    def _():
        carry[...] = jnp.zeros_like(carry)

    x = _mix_out(x_ref, ys_ref, yg_ref, wgl_ref, bg_ref, gs_ref, wo_ref)
    xo_ref[...] = x
    hn = _rms(x, g_ref[...])
    packed = _pack_bf16_pairs(hn)
    hp_ref[0] = packed[:, :SC_ROW]
    hp_ref[1] = packed[:, SC_ROW:]

    h_hi, h_lo = _split2(hn)
    nt = (((1,), (1,)), ((), ()))
    logits = (lax.dot_general(wh_ref[...], h_hi, nt, preferred_element_type=f32)
              + lax.dot_general(wh_ref[...], h_lo, nt, preferred_element_type=f32)
              + lax.dot_general(wl_ref[...], h_hi, nt, preferred_element_type=f32))
    er = logits.shape[0]
    row = lax.broadcasted_iota(i32, (er, tm), 0)
    neg = jnp.float32(-jnp.inf)
    logits = jnp.where(row < N_EXPERTS, logits, neg)
    m1 = jnp.max(logits, axis=0, keepdims=True)
    i1 = jnp.min(jnp.where(logits == m1, row, er), axis=0, keepdims=True)
    l2 = jnp.where(row == i1, neg, logits)
    m2 = jnp.max(l2, axis=0, keepdims=True)
    i2 = jnp.min(jnp.where(l2 == m2, row, er), axis=0, keepdims=True)
    e21 = jnp.exp(m2 - m1)
    w1 = 1.0 / (1.0 + e21)
    w2 = e21 / (1.0 + e21)

    sel1 = row == i1
    sel2 = row == i2
    sel = (sel1 | sel2).astype(f32)
    ri = lax.broadcasted_iota(i32, (tm, tm), 0)
    ci = lax.broadcasted_iota(i32, (tm, tm), 1)
    triu = (ri <= ci).astype(bf16)
    incl = jnp.dot(sel.astype(bf16), triu, preferred_element_type=f32)
    rank = incl - sel + carry[:, 0:1]
    r1 = jnp.sum(jnp.where(sel1, rank, 0.0), axis=0, keepdims=True)
    r2 = jnp.sum(jnp.where(sel2, rank, 0.0), axis=0, keepdims=True)
    new_cnt = carry[:, 0:1] + incl[:, tm - 1:tm]
    carry[...] = jnp.broadcast_to(new_cnt, carry.shape)
    cnt_ref[...] = jnp.broadcast_to(new_cnt, cnt_ref.shape)

    srow = lax.broadcasted_iota(i32, (SUBLANE, tm), 0)
    meta = jnp.where(srow == 0, i1.astype(f32), 0.0)
    meta = jnp.where(srow == 1, i2.astype(f32), meta)
    meta = jnp.where(srow == 2, r1, meta)
    meta = jnp.where(srow == 3, r2, meta)
    meta_ref[...] = meta
    prow = lax.broadcasted_iota(i32, (LANE, tm), 0)
    wpad = jnp.where(prow == 0, w1, jnp.where(prow == 1, w2, 0.0))
    wcol_ref[...] = wpad.T


def _mix_router(l, x2, ys, yg, w_glu, b_glu, g_s5, w_out, g, w_hi, w_lo):
    n = x2.shape[0]
    tm = TM_ROUTE
    row = lambda i: (i, 0)
    const = lambda i: (0, 0)
    return pl.pallas_call(
        _mix_router_body,
        grid=(n // tm,),
        in_specs=_mix_specs(l, tm, w_glu, b_glu, g_s5, w_out) + [
            _layer(g, l),
            pl.BlockSpec(w_hi.shape, const),
            pl.BlockSpec(w_lo.shape, const)],
        out_specs=[pl.BlockSpec((tm, D_MODEL), row),
                   pl.BlockSpec((2, tm, SC_ROW), lambda i: (0, i, 0)),
                   pl.BlockSpec((SUBLANE, tm), lambda i: (0, i)),
                   pl.BlockSpec((tm, LANE), row),
                   pl.BlockSpec(w_hi.shape[:1] + (LANE,), const)],
        out_shape=[SDS((n, D_MODEL), f32), SDS((2, n, SC_ROW), u32), SDS((SUBLANE, n), f32),
                   SDS((n, LANE), f32), SDS(w_hi.shape[:1] + (LANE,), f32)],
        scratch_shapes=[pltpu.VMEM(w_hi.shape[:1] + (LANE,), f32)],
        compiler_params=_cparams("arbitrary"),
        name="mix_router",
    )(x2, ys, yg, w_glu, b_glu, g_s5, w_out, g, w_hi, w_lo)


def _sc_gather(table, idx):
    ni = idx.shape[0]
    mesh = plsc.VectorSubcoreMesh(core_axis_name="core", subcore_axis_name="subcore")
    idx2 = idx.reshape(1, ni)

    @pl.kernel(out_type=SDS((ni, SC_ROW), table.dtype), mesh=mesh)
    def kern(t_hbm, i_hbm, o_hbm):
        def body(i_vmem, o_vmem):
            pltpu.sync_copy(t_hbm.at[i_vmem.at[0]], o_vmem)

        pltpu.emit_pipeline(
            body, grid=(ni // SC_WINDOW,),
            in_specs=[pl.BlockSpec((1, SC_WINDOW), index_map=lambda i: (0, i))],
            out_specs=[pl.BlockSpec((SC_WINDOW, SC_ROW), index_map=lambda i: (i, 0))],
            core_axis_name=("core", "subcore"),
            dimension_semantics=(pltpu.PARALLEL,),
        )(i_hbm, o_hbm)

    return kern(table, idx2)


def _sc_scatter2(x, idx0, idx1, nrows):
    ni = x.shape[0]
    mesh = plsc.VectorSubcoreMesh(core_axis_name="core", subcore_axis_name="subcore")

    @pl.kernel(out_type=SDS((nrows, SC_ROW), x.dtype), mesh=mesh)
    def kern(x_hbm, i0_hbm, i1_hbm, o_hbm):
        def body(x_vmem, i0_vmem, i1_vmem):
            pltpu.sync_copy(x_vmem, o_hbm.at[i0_vmem.at[0]])
            pltpu.sync_copy(x_vmem, o_hbm.at[i1_vmem.at[0]])

        pltpu.emit_pipeline(
            body, grid=(ni // SC_WINDOW,),
            in_specs=[pl.BlockSpec((SC_WINDOW, SC_ROW), index_map=lambda i: (i, 0)),
                      pl.BlockSpec((1, SC_WINDOW), index_map=lambda i: (0, i)),
                      pl.BlockSpec((1, SC_WINDOW), index_map=lambda i: (0, i))],
            out_specs=[],
            core_axis_name=("core", "subcore"),
            dimension_semantics=(pltpu.PARALLEL,),
        )(x_hbm, i0_hbm, i1_hbm)

    return kern(x, idx0.reshape(1, ni), idx1.reshape(1, ni))


def _moe_ffn_body(be_ref, bn_ref, xs_ref, wg_ref, wu_ref, wd_ref, ys_ref, xb, act0, act1, acc):
    i = pl.program_id(0)
    f = pl.program_id(1)
    nf = pl.num_programs(1) - 1
    nvalid = bn_ref[i]
    sub = xs_ref.shape[1] // MOE_SUB

    def emit(start, size, cond):
        rs = pl.ds(start, size)

        def gate_up():
            x = xb[rs, :]
            gate = jnp.dot(x, wg_ref[...].astype(bf16), preferred_element_type=f32)
            up = jnp.dot(x, wu_ref[...].astype(bf16), preferred_element_type=f32)
            return (jax.nn.silu(gate) * up).astype(bf16)

        def down(act_ref):
            return jnp.dot(act_ref[rs, :], wd_ref[...].astype(bf16), preferred_element_type=f32)

        @pl.when(cond & (f == 0))
        def _():
            live = lax.broadcasted_iota(i32, (size, SC_ROW), 0) + start < nvalid
            hi0, lo0 = _unpack_bf16_pairs(jnp.where(live, xs_ref[0, rs, :], jnp.uint32(0)))
            hi1, lo1 = _unpack_bf16_pairs(jnp.where(live, xs_ref[1, rs, :], jnp.uint32(0)))
            xb[rs, :] = jnp.concatenate([hi0, hi1, lo0, lo1], axis=1)
            acc[rs, :] = jnp.zeros((size, D_MODEL), f32)
            act0[rs, :] = gate_up()

        @pl.when(cond & (f > 0) & (f < nf) & (f % 2 == 1))
        def _():
            act1[rs, :] = gate_up()
            acc[rs, :] += down(act0)

        @pl.when(cond & (f > 0) & (f < nf) & (f % 2 == 0))
        def _():
            act0[rs, :] = gate_up()
            acc[rs, :] += down(act1)

        @pl.when(cond & (f == nf))
        def _():
            last = act0 if (D_FF_EXPERT // TF_MOE - 1) % 2 == 0 else act1
            packed = _pack_bf16_pairs(acc[rs, :] + down(last))
            ys_ref[0, rs, :] = packed[:, :SC_ROW]
            ys_ref[1, rs, :] = packed[:, SC_ROW:]

    def zero_out(start, size, cond):
        @pl.when(cond & (f == nf))
        def _():
            ys_ref[:, pl.ds(start, size), :] = jnp.zeros((2, size, SC_ROW), u32)

    for h in range(MOE_SUB):
        live = nvalid - h * sub
        whole = live > sub // 2
        head = (live > 0) & (live <= sub // 2)
        emit(h * sub, sub, whole)
        emit(h * sub, sub // 2, head)
        zero_out(h * sub + sub // 2, sub // 2, head)
        zero_out(h * sub, sub, live <= 0)


def _moe_ffn(i_moe, blk_e, blk_n, xs, wg, wu, wd):
    npad = xs.shape[1]
    tm, tf = TM_MOE, TF_MOE
    nblk = npad // tm
    nf = D_FF_EXPERT // tf

    def nxt(i):
        return jnp.minimum(i + 1, nblk - 1)

    def x_idx(i, f, be, bn):
        return (0, jnp.where(f == nf, nxt(i), i), 0)

    def gu_idx(i, f, be, bn):
        ahead = (f == nf) & (bn[i] > 0) & (bn[nxt(i)] > 0)
        e = jnp.where(ahead, be[nxt(i)], be[i])
        t = jnp.where(ahead, 0, jnp.where(bn[i] > 0, jnp.minimum(f, nf - 1), nf - 1))
        return (i_moe, e, 0, t)

    def d_idx(i, f, be, bn):
        prev = jnp.maximum(i - 1, 0)
        keep = (f == 0) & (i > 0)
        e = jnp.where(keep, be[prev], be[i])
        t = jnp.where(keep | (bn[i] == 0), nf - 1, jnp.maximum(f - 1, 0))
        return (i_moe, e, t, 0)

    grid_spec = pltpu.PrefetchScalarGridSpec(
        num_scalar_prefetch=2,
        grid=(nblk, nf + 1),
        in_specs=[pl.BlockSpec((2, tm, SC_ROW), x_idx),
                  pl.BlockSpec((None, None, D_MODEL, tf), gu_idx),
                  pl.BlockSpec((None, None, D_MODEL, tf), gu_idx),
                  pl.BlockSpec((None, None, tf, D_MODEL), d_idx)],
        out_specs=pl.BlockSpec((2, tm, SC_ROW), lambda i, f, be, bn: (0, i, 0)),
        scratch_shapes=[pltpu.VMEM((tm, D_MODEL), bf16), pltpu.VMEM((tm, tf), bf16),
                        pltpu.VMEM((tm, tf), bf16), pltpu.VMEM((tm, D_MODEL), f32)],
    )
    return pl.pallas_call(
        _moe_ffn_body,
        grid_spec=grid_spec,
        out_shape=SDS((2, npad, SC_ROW), u32),
        compiler_params=_cparams("parallel", "arbitrary"),
        name="moe_ffn",
    )(blk_e, blk_n, xs, wg, wu, wd)


def _combine_body(x_ref, yg_ref, wcol_ref, g_ref, o_ref, *, final_norm):
    w1 = wcol_ref[:, 0:1]
    w2 = wcol_ref[:, 1:2]

    def rows(k):
        hi0, lo0 = _unpack_bf16_pairs(yg_ref[0, k])
        hi1, lo1 = _unpack_bf16_pairs(yg_ref[1, k])
        return jnp.concatenate([hi0, hi1, lo0, lo1], axis=1).astype(f32)

    xo = x_ref[...] + (w1 * rows(0) + w2 * rows(1))
    if final_norm:
        xo = _rms(xo, g_ref[...])
    o_ref[...] = xo


def _combine(x2, yg, wcol, g_final, final_norm):
    n = x2.shape[0]
    tm = TM_ROUTE
    row = lambda i: (i, 0)
    return pl.pallas_call(
        functools.partial(_combine_body, final_norm=final_norm),
        grid=(n // tm,),
        in_specs=[pl.BlockSpec((tm, D_MODEL), row),
                  pl.BlockSpec((2, 2, tm, SC_ROW), lambda i: (0, 0, i, 0)),
                  pl.BlockSpec((tm, LANE), row),
                  pl.BlockSpec((1, D_MODEL), lambda i: (0, 0))],
        out_specs=pl.BlockSpec((tm, D_MODEL), row),
        out_shape=SDS((n, D_MODEL), f32),
        compiler_params=_cparams("parallel"),
        name="moe_combine",
    )(x2, yg, wcol, g_final)


def _moe_layer(l, i_moe, mix_args, g_ffn, w_router, wg, wu, wd, g_final, final_norm):
    n = mix_args[0].shape[0]
    tm = TM_MOE
    npad = 2 * n + N_EXPERTS * tm
    wr = jnp.zeros((2 * SUBLANE, D_MODEL), f32).at[:N_EXPERTS].set(w_router[i_moe].astype(f32).T)
    wr_hi = wr.astype(bf16)
    wr_lo = (wr - wr_hi.astype(f32)).astype(bf16)
    x2, hp, meta, wcol, cnt = _mix_router(l, *mix_args, g_ffn, wr_hi, wr_lo)

    counts = cnt[:N_EXPERTS, 0].astype(i32)
    padded = ((counts + tm - 1) // tm) * tm
    ends = jnp.cumsum(padded)
    offs = ends - padded
    ids = jnp.arange(N_EXPERTS, dtype=i32)
    e12 = meta[0:2].astype(i32)
    r12 = meta[2:4].astype(i32)
    pos_t = r12 + jnp.sum(jnp.where(e12[..., None] == ids, offs, 0), axis=-1)
    blk_start = jnp.arange(npad // tm, dtype=i32) * tm
    blk_e = jnp.minimum(jnp.sum((blk_start[:, None] >= ends[None, :]).astype(i32), axis=1), N_EXPERTS - 1)
    own = blk_e[:, None] == ids
    blk_end = jnp.sum(jnp.where(own, offs + counts, 0), axis=1)
    blk_n = jnp.where(blk_start < ends[-1], jnp.clip(blk_end - blk_start, 0, tm), 0)
    last_e = jnp.max(jnp.where(blk_n > 0, blk_e, 0))
    blk_e = jnp.where(blk_n > 0, blk_e, last_e)

    xs = _sc_scatter2(hp.reshape(2 * n, SC_ROW),
                      jnp.concatenate([pos_t[0], pos_t[0] + npad]),
                      jnp.concatenate([pos_t[1], pos_t[1] + npad]),
                      2 * npad).reshape(2, npad, SC_ROW)
    ys = _moe_ffn(i_moe, blk_e, blk_n, xs, wg, wu, wd)
    gidx = jnp.concatenate([pos_t.reshape(-1), pos_t.reshape(-1) + npad])
    yg = _sc_gather(ys.reshape(2 * npad, SC_ROW), gidx).reshape(2, 2, n, SC_ROW)
    return _combine(x2, yg, wcol, g_final, final_norm)


def _rows(v):
    return v.astype(f32)[:, None, :]


def kernel(x, norm_mix, w_in, s5_lambda_re, s5_lambda_im, s5_log_dt, s5_b_re, s5_b_im, s5_c_re, s5_c_im, s5_d, s5_w_glu, s5_b_glu, s5_out_norm, gla_w_a2, gla_b_a2, gla_out_norm, w_out, norm_ffn, ffn_w_gate, ffn_w_up, ffn_w_down, moe_w_router, moe_w_gate, moe_w_up, moe_w_down, norm_final):
    bsz, seq, _ = x.shape
    n = bsz * seq
    depth = w_in.shape[0]
    x2 = x.reshape(n, D_MODEL)

    w_gp = jnp.zeros((depth, D_MODEL, LANE), f32).at[:, :, :GLA_GATE_RANK].set(w_in[:, :, 2048:2064])
    w_cat = jnp.concatenate([w_in[:, :, 0:512], w_in[:, :, 512:768] * (GLA_DK ** -0.5),
                             w_in[:, :, 1024:1536], w_in[:, :, 1536:2048], w_gp], axis=2).astype(bf16)
    w_kt = jnp.swapaxes(w_in[:, :, 768:1024], 1, 2).astype(bf16)
    s5_mats = jax.vmap(_s5_prep)(s5_lambda_re, s5_lambda_im, s5_log_dt, s5_b_re, s5_b_im, s5_c_re, s5_c_im)
    wa = jnp.zeros((depth, LANE, D_GLA_K), f32).at[:, :GLA_GATE_RANK].set(gla_w_a2).astype(bf16)
    g_mix, g_ffn, d_skip = _rows(norm_mix), _rows(norm_ffn), _rows(s5_d)
    ba, gn = _rows(gla_b_a2), _rows(gla_out_norm)
    mix_w = (s5_w_glu.astype(bf16), _rows(s5_b_glu), _rows(s5_out_norm), w_out.astype(bf16))
    ffn_w = (ffn_w_gate.astype(bf16), ffn_w_up.astype(bf16), ffn_w_down.astype(bf16))
    g_final = norm_final.astype(f32).reshape(1, D_MODEL)

    for l in range(depth):
        u, q, kt, v, r, gl = _inproj(l, x2, g_mix, w_cat, w_kt, bsz, seq)
        ys = _s5(l, u.reshape(bsz, seq, D_S5), *s5_mats, d_skip)
        yg = _gla(l, q.reshape(bsz, seq, D_GLA_K), kt, v.reshape(bsz, seq, D_GLA),
                  r.reshape(bsz, seq, D_GLA), gl.reshape(bsz, seq, LANE), wa, ba, gn)
        mix_args = (x2, ys.reshape(n, D_S5), yg.reshape(n, D_GLA)) + mix_w

        last = l == depth - 1
        if l % 2 == 0:
            x2 = _mix_ffn(l, l // 2, *mix_args, g_ffn, *ffn_w)
            if last:
                x2 = _final_norm(x2, g_final)
        else:
            x2 = _moe_layer(l, l // 2, mix_args, g_ffn, moe_w_router, moe_w_gate, moe_w_up,
                            moe_w_down, g_final, last)
    return x2.reshape(bsz, seq, D_MODEL)


def _final_norm_body(x_ref, g_ref, o_ref):
    o_ref[...] = _rms(x_ref[...], g_ref[...])


def _final_norm(x2, g):
    n = x2.shape[0]
    tm = TM_ROUTE
    return pl.pallas_call(
        _final_norm_body,
        grid=(n // tm,),
        in_specs=[pl.BlockSpec((tm, D_MODEL), lambda i: (i, 0)),
                  pl.BlockSpec((1, D_MODEL), lambda i: (0, 0))],
        out_specs=pl.BlockSpec((tm, D_MODEL), lambda i: (i, 0)),
        out_shape=SDS((n, D_MODEL), f32),
        compiler_params=_cparams("parallel"),
        name="final_norm",
    )(x2, g)
```
